```python
import math
import jax, jax.numpy as jnp
from jax import lax
import numpy as np

D_MODEL = 2048
BATCH = 8
SEQ = 2048
DEPTH = 2

HEAD_DIM = 64
BLOCK = 128
A_Q_HEADS = 16
A_KV_HEADS = 2
A_GROUP = A_Q_HEADS // A_KV_HEADS
A_WINDOW = 128
B_HEADS = 8
B_PATTERNS = ((128, 1), (512, 4), (2048, 16))
C_HEADS = 16
D_HEADS = 16
D_Q_RANK = 512
D_KV_RANK = 256
D_NOPE = 64
D_ROPE = 32
D_V = 64
ROPE_BASE = 10000.0
D_FF = 4 * D_MODEL
LN_EPS = 1e-5
RMS_EPS = 1e-6
ALPHA = (2 * DEPTH) ** 0.25
BETA = (8 * DEPTH) ** -0.25
N_EVEN = (DEPTH + 1) // 2
N_ODD = DEPTH // 2
A_Q_W = A_Q_HEADS * HEAD_DIM
A_KV_W = A_KV_HEADS * HEAD_DIM
B_W = B_HEADS * HEAD_DIM
EVEN_IN = A_Q_W + 2 * A_KV_W + 3 * B_W * len(B_PATTERNS)
EVEN_OUT = A_Q_W + B_W
C_W = C_HEADS * HEAD_DIM
ODD_IN = 3 * C_W + D_Q_RANK + D_KV_RANK + D_ROPE
ODD_OUT = C_W + D_HEADS * D_V

kernel_name = 'hybrid_swa_dilated_stickbreak_mla_deepnorm'


def layer_norm(x, g, b):
    xf = x.astype(jnp.float32)
    mu = jnp.mean(xf, axis=-1, keepdims=True)
    xc = xf - mu
    var = jnp.mean(xc * xc, axis=-1, keepdims=True)
    y = xc * lax.rsqrt(var + LN_EPS) * g.astype(jnp.float32) + b.astype(jnp.float32)
    return y.astype(x.dtype)


def rms_norm(x, g):
    xf = x.astype(jnp.float32)
    y = xf * lax.rsqrt(jnp.mean(xf * xf, axis=-1, keepdims=True) + RMS_EPS) * g.astype(jnp.float32)
    return y.astype(x.dtype)


def alibi_slopes(n):
    return 2.0 ** (-8.0 * jnp.arange(1, n + 1, dtype=jnp.float32) / n)


def banded_attention(q, k, v, n_back, dist_scale, slopes, sinks=None):
    bsz, n, kh, g, dh = q.shape
    nb = -(-n // BLOCK)
    n_prev = -(-n_back // BLOCK)
    pad = nb * BLOCK - n
    qb = jnp.pad(q, ((0, 0), (0, pad), (0, 0), (0, 0), (0, 0))).reshape(bsz, nb, BLOCK, kh, g, dh)
    kv_pad = ((0, 0), (n_prev * BLOCK, pad), (0, 0), (0, 0))
    kp = jnp.pad(k, kv_pad).reshape(bsz, nb + n_prev, BLOCK, kh, dh)
    vp = jnp.pad(v, kv_pad).reshape(bsz, nb + n_prev, BLOCK, kh, dh)
    kb = jnp.concatenate([kp[:, p:p + nb] for p in range(n_prev + 1)], axis=2)
    vb = jnp.concatenate([vp[:, p:p + nb] for p in range(n_prev + 1)], axis=2)
    scores = jnp.einsum('bnqkgd,bnskd->bnkgqs', qb, kb).astype(jnp.float32) * (1.0 / math.sqrt(dh))
    n_keys = (n_prev + 1) * BLOCK
    qpos = jnp.arange(BLOCK)
    spos = jnp.arange(n_keys)
    rel = n_prev * BLOCK + qpos[:, None] - spos[None, :]
    key_idx = (jnp.arange(nb)[:, None] - n_prev) * BLOCK + spos[None, :]
    valid = (rel >= 0)[None] & (rel <= n_back)[None] & (key_idx >= 0)[:, None, :]
    bias = -slopes.astype(jnp.float32)[:, :, None, None] * (rel * dist_scale).astype(jnp.float32)
    scores = jnp.where(valid[None, :, None, None], scores + bias[None, None], -jnp.inf)
    m = jnp.max(scores, axis=-1)
    if sinks is not None:
        sink = sinks.astype(jnp.float32)[None, None, :, :, None]
        m = jnp.maximum(m, sink)
    p = jnp.exp(scores - m[..., None])
    denom = jnp.sum(p, axis=-1)
    if sinks is not None:
        denom = denom + jnp.exp(sink - m)
    out = jnp.einsum('bnkgqs,bnskd->bnqkgd', p.astype(v.dtype), vb).astype(jnp.float32)
    out = out / jnp.moveaxis(denom, -1, 2)[..., None]
    lse = jnp.moveaxis(m + jnp.log(denom), -1, 2)
    out = out.reshape(bsz, nb * BLOCK, kh, g, dh)[:, :n].astype(q.dtype)
    lse = lse.reshape(bsz, nb * BLOCK, kh, g)[:, :n]
    return out, lse


def to_strided(t, d):
    b, s = t.shape[:2]
    rest = t.shape[2:]
    t = jnp.moveaxis(t.reshape(b, s // d, d, *rest), 2, 1)
    return t.reshape(b * d, s // d, *rest)


def from_strided(t, b):
    bd, n = t.shape[:2]
    d = bd // b
    rest = t.shape[2:]
    t = jnp.moveaxis(t.reshape(b, d, n, *rest), 1, 2)
    return t.reshape(b, n * d, *rest)


def even_mixer(x, w_in, sinks, w_out):
    bsz, seq, _ = x.shape
    h = jnp.einsum('bsd,de->bse', x, w_in)
    qa, ka, va, hb = jnp.split(h, [A_Q_W, A_Q_W + A_KV_W, A_Q_W + 2 * A_KV_W], axis=-1)
    qa = qa.reshape(bsz, seq, A_KV_HEADS, A_GROUP, HEAD_DIM)
    ka = ka.reshape(bsz, seq, A_KV_HEADS, HEAD_DIM)
    va = va.reshape(bsz, seq, A_KV_HEADS, HEAD_DIM)
    oa, _ = banded_attention(qa, ka, va, A_WINDOW - 1, 1,
                             alibi_slopes(A_Q_HEADS).reshape(A_KV_HEADS, A_GROUP),
                             sinks.reshape(A_KV_HEADS, A_GROUP))
    slopes_b = alibi_slopes(B_HEADS).reshape(B_HEADS, 1)
    outs, lses = [], []
    for gi, (window, dil) in enumerate(B_PATTERNS):
        blk = hb[..., gi * 3 * B_W:(gi + 1) * 3 * B_W].reshape(bsz, seq, 3, B_HEADS, HEAD_DIM)
        qb, kb, vb = blk[:, :, 0], blk[:, :, 1], blk[:, :, 2]
        o, lse = banded_attention(to_strided(qb[:, :, :, None], dil), to_strided(kb, dil),
                                  to_strided(vb, dil), window // dil, dil, slopes_b)
        outs.append(from_strided(o[:, :, :, 0], bsz))
        lses.append(from_strided(lse[:, :, :, 0], bsz))
    mix = jax.nn.softmax(jnp.stack(lses), axis=0)
    ob = jnp.einsum('gbsh,gbshd->bshd', mix, jnp.stack(outs).astype(jnp.float32)).astype(x.dtype)
    y = jnp.concatenate([oa.reshape(bsz, seq, A_Q_W), ob.reshape(bsz, seq, B_W)], axis=-1)
    return jnp.einsum('bse,ed->bsd', y, w_out)


def stick_breaking_attention(q, k, v):
    bsz, seq, nh, dh = q.shape
    scale = 1.0 / math.sqrt(dh)
    outs = []
    for i in range(seq // BLOCK):
        lo, hi = i * BLOCK, (i + 1) * BLOCK
        z = jnp.einsum('bqhd,bshd->bhqs', q[:, lo:hi], k[:, :hi]).astype(jnp.float32) * scale
        strict = jnp.arange(hi)[None, :] < (lo + jnp.arange(BLOCK))[:, None]
        log_beta = jax.nn.log_sigmoid(z)
        log_keep = jnp.where(strict, jax.nn.log_sigmoid(-z), 0.0)
        after = lax.cumsum(log_keep, axis=3, reverse=True) - log_keep
        w = jnp.where(strict, jnp.exp(log_beta + after), 0.0)
        outs.append(jnp.einsum('bhqs,bshd->bqhd', w.astype(v.dtype), v[:, :hi]))
    return jnp.concatenate(outs, axis=1)


def apply_rope(x, cos, sin):
    half = x.shape[-1] // 2
    shape = (1, cos.shape[0]) + (1,) * (x.ndim - 3) + (half,)
    c = cos.reshape(shape).astype(x.dtype)
    s = sin.reshape(shape).astype(x.dtype)
    x1, x2 = x[..., :half], x[..., half:]
    return jnp.concatenate([x1 * c - x2 * s, x1 * s + x2 * c], axis=-1)


def mla_attention(q_nope, q_rope, k_nope, k_rope, v):
    seq = q_nope.shape[1]
    scale = 1.0 / math.sqrt(D_NOPE + D_ROPE)
    outs = []
    for i in range(seq // BLOCK):
        lo, hi = i * BLOCK, (i + 1) * BLOCK
        s = (jnp.einsum('bqhd,bshd->bhqs', q_nope[:, lo:hi], k_nope[:, :hi]).astype(jnp.float32)
             + jnp.einsum('bqhr,bsr->bhqs', q_rope[:, lo:hi], k_rope[:, :hi]).astype(jnp.float32)) * scale
        causal = jnp.arange(hi)[None, :] <= (lo + jnp.arange(BLOCK))[:, None]
        p = jax.nn.softmax(jnp.where(causal, s, -jnp.inf), axis=-1)
        outs.append(jnp.einsum('bhqs,bshd->bqhd', p.astype(v.dtype), v[:, :hi]))
    return jnp.concatenate(outs, axis=1)


def odd_mixer(x, w_in, q_norm_g, kv_norm_g, w_uq, w_ukv, w_out):
    bsz, seq, _ = x.shape
    h = jnp.einsum('bsd,de->bse', x, w_in)
    qc, kc, vc, cq, ckv, kr = jnp.split(
        h, [C_W, 2 * C_W, 3 * C_W, 3 * C_W + D_Q_RANK, 3 * C_W + D_Q_RANK + D_KV_RANK], axis=-1)
    oc = stick_breaking_attention(qc.reshape(bsz, seq, C_HEADS, HEAD_DIM),
                                  kc.reshape(bsz, seq, C_HEADS, HEAD_DIM),
                                  vc.reshape(bsz, seq, C_HEADS, HEAD_DIM))
    q = jnp.einsum('bsr,re->bse', rms_norm(cq, q_norm_g), w_uq).reshape(bsz, seq, D_HEADS, D_NOPE + D_ROPE)
    kv = jnp.einsum('bsr,re->bse', rms_norm(ckv, kv_norm_g), w_ukv).reshape(bsz, seq, D_HEADS, D_NOPE + D_V)
    q_nope, q_rope = q[..., :D_NOPE], q[..., D_NOPE:]
    k_nope, v = kv[..., :D_NOPE], kv[..., D_NOPE:]
    inv_freq = ROPE_BASE ** (-jnp.arange(0, D_ROPE, 2, dtype=jnp.float32) / D_ROPE)
    ang = jnp.arange(seq, dtype=jnp.float32)[:, None] * inv_freq[None, :]
    cos, sin = jnp.cos(ang), jnp.sin(ang)
    od = mla_attention(q_nope, apply_rope(q_rope, cos, sin), k_nope, apply_rope(kr, cos, sin), v)
    y = jnp.concatenate([oc.reshape(bsz, seq, C_W), od.reshape(bsz, seq, D_HEADS * D_V)], axis=-1)
    return jnp.einsum('bse,ed->bsd', y, w_out)


def sqrelu_mlp(x, w1, w2):
    hid = jax.nn.relu(jnp.einsum('bsd,df->bsf', x, w1))
    return jnp.einsum('bsf,fd->bsd', hid * hid, w2)


def _fwd_setup_inputs(seed: int = 0) -> dict:
    key = jax.random.key(seed)
    ks = jax.random.split(key, 16)
    f32 = jnp.float32

    def normal(k, shape, scale):
        return jax.random.normal(k, shape, f32) * scale

    x = normal(ks[0], (BATCH, SEQ, D_MODEL), 1.0)
    b_group_col = jnp.concatenate([jnp.ones(2 * B_W, f32), jnp.full((B_W,), BETA, f32)])
    even_col = jnp.concatenate([jnp.ones(A_Q_W + A_KV_W, f32), jnp.full((A_KV_W,), BETA, f32)]
                               + [b_group_col] * len(B_PATTERNS))
    even_w_in = normal(ks[1], (N_EVEN, D_MODEL, EVEN_IN), D_MODEL ** -0.5) * even_col
    even_sinks = 1.0 + normal(ks[2], (N_EVEN, A_Q_HEADS), 1.0)
    even_w_out = normal(ks[3], (N_EVEN, EVEN_OUT, D_MODEL), BETA * EVEN_OUT ** -0.5)
    odd_col = jnp.concatenate([jnp.ones(2 * C_W, f32), jnp.full((C_W,), BETA, f32),
                               jnp.ones(D_Q_RANK + D_KV_RANK + D_ROPE, f32)])
    odd_w_in = normal(ks[4], (N_ODD, D_MODEL, ODD_IN), D_MODEL ** -0.5) * odd_col
    odd_q_norm_g = 1.0 + normal(ks[5], (N_ODD, D_Q_RANK), 0.02)
    odd_kv_norm_g = 1.0 + normal(ks[6], (N_ODD, D_KV_RANK), 0.02)
    odd_w_uq = normal(ks[7], (N_ODD, D_Q_RANK, D_HEADS * (D_NOPE + D_ROPE)), D_Q_RANK ** -0.5)
    ukv_col = jnp.tile(jnp.concatenate([jnp.ones(D_NOPE, f32), jnp.full((D_V,), BETA, f32)]), D_HEADS)
    odd_w_ukv = normal(ks[8], (N_ODD, D_KV_RANK, D_HEADS * (D_NOPE + D_V)), D_KV_RANK ** -0.5) * ukv_col
    odd_w_out = normal(ks[9], (N_ODD, ODD_OUT, D_MODEL), BETA * ODD_OUT ** -0.5)
    ln1_g = 1.0 + normal(ks[10], (DEPTH, D_MODEL), 0.02)
    ln1_b = normal(ks[11], (DEPTH, D_MODEL), 0.02)
    mlp_w1 = normal(ks[12], (DEPTH, D_MODEL, D_FF), D_MODEL ** -0.5)
    mlp_w2 = normal(ks[13], (DEPTH, D_FF, D_MODEL), BETA * D_FF ** -0.5)
    ln2_g = 1.0 + normal(ks[14], (DEPTH, D_MODEL), 0.02)
    ln2_b = normal(ks[15], (DEPTH, D_MODEL), 0.02)
    return {'x': x, 'even_w_in': even_w_in, 'even_sinks': even_sinks, 'even_w_out': even_w_out,
            'odd_w_in': odd_w_in, 'odd_q_norm_g': odd_q_norm_g, 'odd_kv_norm_g': odd_kv_norm_g,
            'odd_w_uq': odd_w_uq, 'odd_w_ukv': odd_w_ukv, 'odd_w_out': odd_w_out,
            'ln1_g': ln1_g, 'ln1_b': ln1_b, 'mlp_w1': mlp_w1, 'mlp_w2': mlp_w2,
            'ln2_g': ln2_g, 'ln2_b': ln2_b}


def _fwd_reference(x, even_w_in, even_sinks, even_w_out, odd_w_in, odd_q_norm_g, odd_kv_norm_g,
              odd_w_uq, odd_w_ukv, odd_w_out, ln1_g, ln1_b, mlp_w1, mlp_w2, ln2_g, ln2_b):
    for layer in range(DEPTH):
        j = layer // 2
        if layer % 2 == 0:
            mixed = even_mixer(x, even_w_in[j], even_sinks[j], even_w_out[j])
        else:
            mixed = odd_mixer(x, odd_w_in[j], odd_q_norm_g[j], odd_kv_norm_g[j],
                              odd_w_uq[j], odd_w_ukv[j], odd_w_out[j])
        x = layer_norm(ALPHA * x + mixed, ln1_g[layer], ln1_b[layer])
        x = layer_norm(ALPHA * x + sqrelu_mlp(x, mlp_w1[layer], mlp_w2[layer]), ln2_g[layer], ln2_b[layer])
    return x


import jax as _jax
import jax.numpy as _jnp

TWIN_FORMAT = 'train_step'
FWD_PARAMS = ['x', 'even_w_in', 'even_sinks', 'even_w_out', 'odd_w_in', 'odd_q_norm_g', 'odd_kv_norm_g', 'odd_w_uq', 'odd_w_ukv', 'odd_w_out', 'ln1_g', 'ln1_b', 'mlp_w1', 'mlp_w2', 'ln2_g', 'ln2_b']
TWIN_WEIGHTS = ['even_w_in', 'even_sinks', 'even_w_out', 'odd_w_in', 'odd_q_norm_g', 'odd_kv_norm_g', 'odd_w_uq', 'odd_w_ukv', 'odd_w_out', 'ln1_g', 'ln1_b', 'mlp_w1', 'mlp_w2', 'ln2_g', 'ln2_b']
TWIN_DIFF_INPUT = 'x'
TWIN_INPUTS = ['x', 'even_w_in', 'even_sinks', 'even_w_out', 'odd_w_in', 'odd_q_norm_g', 'odd_kv_norm_g', 'odd_w_uq', 'odd_w_ukv', 'odd_w_out', 'ln1_g', 'ln1_b', 'mlp_w1', 'mlp_w2', 'ln2_g', 'ln2_b', 'loss_target', 'm_even_w_in', 'm_even_sinks', 'm_even_w_out', 'm_odd_w_in', 'm_odd_q_norm_g', 'm_odd_kv_norm_g', 'm_odd_w_uq', 'm_odd_w_ukv', 'm_odd_w_out', 'm_ln1_g', 'm_ln1_b', 'm_mlp_w1', 'm_mlp_w2', 'm_ln2_g', 'm_ln2_b', 'v_even_w_in', 'v_even_sinks', 'v_even_w_out', 'v_odd_w_in', 'v_odd_q_norm_g', 'v_odd_kv_norm_g', 'v_odd_w_uq', 'v_odd_w_ukv', 'v_odd_w_out', 'v_ln1_g', 'v_ln1_b', 'v_mlp_w1', 'v_mlp_w2', 'v_ln2_g', 'v_ln2_b']
TWIN_OUTPUTS = ['loss', 'grad_x', 'grad_even_w_in', 'grad_even_sinks', 'grad_even_w_out', 'grad_odd_w_in', 'grad_odd_q_norm_g', 'grad_odd_kv_norm_g', 'grad_odd_w_uq', 'grad_odd_w_ukv', 'grad_odd_w_out', 'grad_ln1_g', 'grad_ln1_b', 'grad_mlp_w1', 'grad_mlp_w2', 'grad_ln2_g', 'grad_ln2_b', 'delta_even_w_in', 'delta_even_sinks', 'delta_even_w_out', 'delta_odd_w_in', 'delta_odd_q_norm_g', 'delta_odd_kv_norm_g', 'delta_odd_w_uq', 'delta_odd_w_ukv', 'delta_odd_w_out', 'delta_ln1_g', 'delta_ln1_b', 'delta_mlp_w1', 'delta_mlp_w2', 'delta_ln2_g', 'delta_ln2_b', 'new_m_even_w_in', 'new_m_even_sinks', 'new_m_even_w_out', 'new_m_odd_w_in', 'new_m_odd_q_norm_g', 'new_m_odd_kv_norm_g', 'new_m_odd_w_uq', 'new_m_odd_w_ukv', 'new_m_odd_w_out', 'new_m_ln1_g', 'new_m_ln1_b', 'new_m_mlp_w1', 'new_m_mlp_w2', 'new_m_ln2_g', 'new_m_ln2_b', 'new_v_even_w_in', 'new_v_even_sinks', 'new_v_even_w_out', 'new_v_odd_w_in', 'new_v_odd_q_norm_g', 'new_v_odd_kv_norm_g', 'new_v_odd_w_uq', 'new_v_odd_w_ukv', 'new_v_odd_w_out', 'new_v_ln1_g', 'new_v_ln1_b', 'new_v_mlp_w1', 'new_v_mlp_w2', 'new_v_ln2_g', 'new_v_ln2_b']
TWIN_LEAF_KINDS = {'loss': 'loss', 'grad_x': 'grad_x', 'grad_even_w_in': 'grad_w', 'grad_even_sinks': 'grad_w', 'grad_even_w_out': 'grad_w', 'grad_odd_w_in': 'grad_w', 'grad_odd_q_norm_g': 'grad_w', 'grad_odd_kv_norm_g': 'grad_w', 'grad_odd_w_uq': 'grad_w', 'grad_odd_w_ukv': 'grad_w', 'grad_odd_w_out': 'grad_w', 'grad_ln1_g': 'grad_w', 'grad_ln1_b': 'grad_w', 'grad_mlp_w1': 'grad_w', 'grad_mlp_w2': 'grad_w', 'grad_ln2_g': 'grad_w', 'grad_ln2_b': 'grad_w', 'delta_even_w_in': 'delta_w', 'delta_even_sinks': 'delta_w', 'delta_even_w_out': 'delta_w', 'delta_odd_w_in': 'delta_w', 'delta_odd_q_norm_g': 'delta_w', 'delta_odd_kv_norm_g': 'delta_w', 'delta_odd_w_uq': 'delta_w', 'delta_odd_w_ukv': 'delta_w', 'delta_odd_w_out': 'delta_w', 'delta_ln1_g': 'delta_w', 'delta_ln1_b': 'delta_w', 'delta_mlp_w1': 'delta_w', 'delta_mlp_w2': 'delta_w', 'delta_ln2_g': 'delta_w', 'delta_ln2_b': 'delta_w', 'new_m_even_w_in': 'new_m', 'new_m_even_sinks': 'new_m', 'new_m_even_w_out': 'new_m', 'new_m_odd_w_in': 'new_m', 'new_m_odd_q_norm_g': 'new_m', 'new_m_odd_kv_norm_g': 'new_m', 'new_m_odd_w_uq': 'new_m', 'new_m_odd_w_ukv': 'new_m', 'new_m_odd_w_out': 'new_m', 'new_m_ln1_g': 'new_m', 'new_m_ln1_b': 'new_m', 'new_m_mlp_w1': 'new_m', 'new_m_mlp_w2': 'new_m', 'new_m_ln2_g': 'new_m', 'new_m_ln2_b': 'new_m', 'new_v_even_w_in': 'new_v', 'new_v_even_sinks': 'new_v', 'new_v_even_w_out': 'new_v', 'new_v_odd_w_in': 'new_v', 'new_v_odd_q_norm_g': 'new_v', 'new_v_odd_kv_norm_g': 'new_v', 'new_v_odd_w_uq': 'new_v', 'new_v_odd_w_ukv': 'new_v', 'new_v_odd_w_out': 'new_v', 'new_v_ln1_g': 'new_v', 'new_v_ln1_b': 'new_v', 'new_v_mlp_w1': 'new_v', 'new_v_mlp_w2': 'new_v', 'new_v_ln2_g': 'new_v', 'new_v_ln2_b': 'new_v'}


def _forward(args):
    return _fwd_reference(*[args[k] for k in FWD_PARAMS])


def _output_shape():
    out = _jax.eval_shape(lambda: _forward(_fwd_setup_inputs(0)))
    return out.shape, out.dtype

N_MICROBATCH = 1
ADAM_LR = 0.001
ADAM_B1 = 0.9
ADAM_B2 = 0.999
ADAM_EPS = 1e-08
ADAM_WD = 0.01
ADAM_STEP = 10
PER_EXAMPLE_BATCH_AXIS = {'x': 0, 'loss_target': 0}
SHARED_INPUTS = []
_WEIGHT_DTYPES = {'even_w_in': _jnp.float32, 'even_sinks': _jnp.float32, 'even_w_out': _jnp.float32, 'odd_w_in': _jnp.float32, 'odd_q_norm_g': _jnp.float32, 'odd_kv_norm_g': _jnp.float32, 'odd_w_uq': _jnp.float32, 'odd_w_ukv': _jnp.float32, 'odd_w_out': _jnp.float32, 'ln1_g': _jnp.float32, 'ln1_b': _jnp.float32, 'mlp_w1': _jnp.float32, 'mlp_w2': _jnp.float32, 'ln2_g': _jnp.float32, 'ln2_b': _jnp.float32}
MOMENT_SCALE = {'even_w_in': 4.315109e-03, 'even_sinks': 6.593240e-03, 'even_w_out': 6.357589e-03, 'odd_w_in': 8.774871e-03, 'odd_q_norm_g': 2.918938e-03, 'odd_kv_norm_g': 7.739764e-03, 'odd_w_uq': 1.624627e-03, 'odd_w_ukv': 5.288069e-03, 'odd_w_out': 1.256639e-02, 'ln1_g': 2.197082e-01, 'ln1_b': 1.439603e-01, 'mlp_w1': 1.470205e-02, 'mlp_w2': 6.431784e-02, 'ln2_g': 5.683155e+00, 'ln2_b': 1.276570e+00}


def _to_microbatches(a, axis):
    t = _jnp.moveaxis(a, axis, 0)
    t = t.reshape((N_MICROBATCH, t.shape[0] // N_MICROBATCH) + t.shape[1:])
    return _jnp.moveaxis(t, 1, axis + 1)


def setup_inputs(seed: int = 0) -> dict:
    inp = _fwd_setup_inputs(seed)
    key = _jax.random.fold_in(_jax.random.key(seed), 7919)
    shape, _ = _output_shape()
    out = dict(inp)
    out["loss_target"] = _jax.random.normal(_jax.random.fold_in(key, 0), shape, _jnp.float32)
    for i, name in enumerate(TWIN_WEIGHTS):
        w = inp[name].astype(_jnp.float32)
        if MOMENT_SCALE is None:
            s = _jnp.sqrt(_jnp.mean(_jnp.square(w)) + 1e-30)
        else:
            s = MOMENT_SCALE[name]
        km, kv = _jax.random.split(_jax.random.fold_in(key, i + 1))
        out[name] = w
        out["m_" + name] = s * _jax.random.normal(km, w.shape, _jnp.float32)
        out["v_" + name] = (s * s) * _jax.random.uniform(kv, w.shape, _jnp.float32, 0.5, 1.5)
    if N_MICROBATCH > 1:
        for name, axis in PER_EXAMPLE_BATCH_AXIS.items():
            out[name] = _to_microbatches(out[name], axis)
    return {'x': out['x'], 'even_w_in': out['even_w_in'], 'even_sinks': out['even_sinks'], 'even_w_out': out['even_w_out'], 'odd_w_in': out['odd_w_in'], 'odd_q_norm_g': out['odd_q_norm_g'], 'odd_kv_norm_g': out['odd_kv_norm_g'], 'odd_w_uq': out['odd_w_uq'], 'odd_w_ukv': out['odd_w_ukv'], 'odd_w_out': out['odd_w_out'], 'ln1_g': out['ln1_g'], 'ln1_b': out['ln1_b'], 'mlp_w1': out['mlp_w1'], 'mlp_w2': out['mlp_w2'], 'ln2_g': out['ln2_g'], 'ln2_b': out['ln2_b'], 'loss_target': out['loss_target'], 'm_even_w_in': out['m_even_w_in'], 'm_even_sinks': out['m_even_sinks'], 'm_even_w_out': out['m_even_w_out'], 'm_odd_w_in': out['m_odd_w_in'], 'm_odd_q_norm_g': out['m_odd_q_norm_g'], 'm_odd_kv_norm_g': out['m_odd_kv_norm_g'], 'm_odd_w_uq': out['m_odd_w_uq'], 'm_odd_w_ukv': out['m_odd_w_ukv'], 'm_odd_w_out': out['m_odd_w_out'], 'm_ln1_g': out['m_ln1_g'], 'm_ln1_b': out['m_ln1_b'], 'm_mlp_w1': out['m_mlp_w1'], 'm_mlp_w2': out['m_mlp_w2'], 'm_ln2_g': out['m_ln2_g'], 'm_ln2_b': out['m_ln2_b'], 'v_even_w_in': out['v_even_w_in'], 'v_even_sinks': out['v_even_sinks'], 'v_even_w_out': out['v_even_w_out'], 'v_odd_w_in': out['v_odd_w_in'], 'v_odd_q_norm_g': out['v_odd_q_norm_g'], 'v_odd_kv_norm_g': out['v_odd_kv_norm_g'], 'v_odd_w_uq': out['v_odd_w_uq'], 'v_odd_w_ukv': out['v_odd_w_ukv'], 'v_odd_w_out': out['v_odd_w_out'], 'v_ln1_g': out['v_ln1_g'], 'v_ln1_b': out['v_ln1_b'], 'v_mlp_w1': out['v_mlp_w1'], 'v_mlp_w2': out['v_mlp_w2'], 'v_ln2_g': out['v_ln2_g'], 'v_ln2_b': out['v_ln2_b']}


def _loss(weights, diff, rest, loss_target):
    with _jax.named_scope("forward"):
        args = {**rest, TWIN_DIFF_INPUT: diff, **{k: w.astype(_WEIGHT_DTYPES[k]) for k, w in weights.items()}}
        y = _forward(args)
    with _jax.named_scope("loss_head"):
        err = _jnp.square(y.astype(_jnp.float32) - loss_target)
        return 0.5 * _jnp.sum(_jnp.mean(err, axis=-1)) if err.ndim else 0.5 * err


def _adamw(w, g, m, v):
    m = ADAM_B1 * m + (1.0 - ADAM_B1) * g
    v = ADAM_B2 * v + (1.0 - ADAM_B2) * _jnp.square(g)
    m_hat = m / (1.0 - ADAM_B1 ** ADAM_STEP)
    v_hat = v / (1.0 - ADAM_B2 ** ADAM_STEP)
    delta = -ADAM_LR * (m_hat / (_jnp.sqrt(v_hat) + ADAM_EPS) + ADAM_WD * w)
    return delta, m, v


def reference(x, even_w_in, even_sinks, even_w_out, odd_w_in, odd_q_norm_g, odd_kv_norm_g, odd_w_uq, odd_w_ukv, odd_w_out, ln1_g, ln1_b, mlp_w1, mlp_w2, ln2_g, ln2_b, loss_target, m_even_w_in, m_even_sinks, m_even_w_out, m_odd_w_in, m_odd_q_norm_g, m_odd_kv_norm_g, m_odd_w_uq, m_odd_w_ukv, m_odd_w_out, m_ln1_g, m_ln1_b, m_mlp_w1, m_mlp_w2, m_ln2_g, m_ln2_b, v_even_w_in, v_even_sinks, v_even_w_out, v_odd_w_in, v_odd_q_norm_g, v_odd_kv_norm_g, v_odd_w_uq, v_odd_w_ukv, v_odd_w_out, v_ln1_g, v_ln1_b, v_mlp_w1, v_mlp_w2, v_ln2_g, v_ln2_b):
    given = dict(x=x, even_w_in=even_w_in, even_sinks=even_sinks, even_w_out=even_w_out, odd_w_in=odd_w_in, odd_q_norm_g=odd_q_norm_g, odd_kv_norm_g=odd_kv_norm_g, odd_w_uq=odd_w_uq, odd_w_ukv=odd_w_ukv, odd_w_out=odd_w_out, ln1_g=ln1_g, ln1_b=ln1_b, mlp_w1=mlp_w1, mlp_w2=mlp_w2, ln2_g=ln2_g, ln2_b=ln2_b, loss_target=loss_target, m_even_w_in=m_even_w_in, m_even_sinks=m_even_sinks, m_even_w_out=m_even_w_out, m_odd_w_in=m_odd_w_in, m_odd_q_norm_g=m_odd_q_norm_g, m_odd_kv_norm_g=m_odd_kv_norm_g, m_odd_w_uq=m_odd_w_uq, m_odd_w_ukv=m_odd_w_ukv, m_odd_w_out=m_odd_w_out, m_ln1_g=m_ln1_g, m_ln1_b=m_ln1_b, m_mlp_w1=m_mlp_w1, m_mlp_w2=m_mlp_w2, m_ln2_g=m_ln2_g, m_ln2_b=m_ln2_b, v_even_w_in=v_even_w_in, v_even_sinks=v_even_sinks, v_even_w_out=v_even_w_out, v_odd_w_in=v_odd_w_in, v_odd_q_norm_g=v_odd_q_norm_g, v_odd_kv_norm_g=v_odd_kv_norm_g, v_odd_w_uq=v_odd_w_uq, v_odd_w_ukv=v_odd_w_ukv, v_odd_w_out=v_odd_w_out, v_ln1_g=v_ln1_g, v_ln1_b=v_ln1_b, v_mlp_w1=v_mlp_w1, v_mlp_w2=v_mlp_w2, v_ln2_g=v_ln2_g, v_ln2_b=v_ln2_b)
    weights = {n: given[n] for n in TWIN_WEIGHTS}
    shared = {n: given[n] for n in SHARED_INPUTS}
    per_example = {n: given[n] for n in ['x']}
    grad_fn = _jax.value_and_grad(_loss, argnums=(0, 1))

    def one_microbatch(ex, loss_target):
        ex = dict(ex)
        diff = ex.pop(TWIN_DIFF_INPUT)
        return grad_fn(weights, diff, {**shared, **ex}, loss_target)

    if N_MICROBATCH == 1:
        loss, (grad_w, grad_x) = one_microbatch(per_example, given["loss_target"])
    else:
        def body(carry, xs):
            loss_sum, grad_sum = carry
            l_k, (gw_k, gx_k) = one_microbatch(xs[0], xs[1])
            with _jax.named_scope("update"):
                return (loss_sum + l_k, _jax.tree.map(_jnp.add, grad_sum, gw_k)), gx_k

        init = (_jnp.zeros((), _jnp.float32), _jax.tree.map(_jnp.zeros_like, weights))
        (loss, grad_w), grad_x = _jax.lax.scan(body, init, (per_example, given["loss_target"]))
    with _jax.named_scope("update"):
        delta_w, new_m, new_v = {}, {}, {}
        for n in TWIN_WEIGHTS:
            delta_w[n], new_m[n], new_v[n] = _adamw(weights[n], grad_w[n], given["m_" + n], given["v_" + n])
    return (loss, grad_x, *[grad_w[n] for n in TWIN_WEIGHTS], *[delta_w[n] for n in TWIN_WEIGHTS],
            *[new_m[n] for n in TWIN_WEIGHTS], *[new_v[n] for n in TWIN_WEIGHTS])
```

```python
import functools
import math

import jax
import jax.numpy as jnp
import numpy as np
from jax import lax
from jax.experimental import pallas as pl
from jax.experimental.pallas import tpu as pltpu

F32 = jnp.float32
BF16 = jnp.bfloat16
MXU_DTYPE = BF16

HEAD_DIM = 64
A_Q_HEADS, A_KV_HEADS, A_WINDOW = 16, 2, 128
B_HEADS = 8
B_PATTERNS = ((128, 1), (512, 4), (2048, 16))
C_HEADS = 16
D_HEADS, D_Q_RANK, D_KV_RANK, D_NOPE, D_ROPE, D_V = 16, 512, 256, 64, 32, 64
ROPE_BASE = 10000.0
LN_EPS, RMS_EPS = 1e-5, 1e-6
DEPTH = 2
ALPHA = (2 * DEPTH) ** 0.25
A_Q_W, A_KV_W, B_W = A_Q_HEADS * HEAD_DIM, A_KV_HEADS * HEAD_DIM, B_HEADS * HEAD_DIM
EVEN_IN = A_Q_W + 2 * A_KV_W + 3 * B_W * len(B_PATTERNS)
C_W = C_HEADS * HEAD_DIM
ODD_IN = 3 * C_W + D_Q_RANK + D_KV_RANK + D_ROPE
ADAM_LR, ADAM_B1, ADAM_B2, ADAM_EPS, ADAM_WD, ADAM_STEP = 0.001, 0.9, 0.999, 1e-08, 0.01, 10

N_DEV = 8
BLK = 128
NEG = -1e30
LANES = 128
PACK_COLS = 512
PACK_ROWS = 512
VMEM_LIMIT = 48 * 1024 * 1024

NN = ((1,), (0,))
NT = ((1,), (1,))
TN = ((0,), (0,))
MESH = pl.DeviceIdType.MESH
ANY = pl.BlockSpec(memory_space=pl.ANY)


def _dot(a, b, dims):
    return lax.dot_general(a, b, (dims, ((), ())), preferred_element_type=F32)


def _params(*sem):
    return pltpu.CompilerParams(dimension_semantics=tuple(sem), vmem_limit_bytes=VMEM_LIMIT)


def _pick(n, cap, mult=LANES):
    if n <= cap:
        return n
    for t in range(cap - cap % mult, 0, -mult):
        if n % t == 0:
            return t
    raise ValueError(f"no tile for {n}")


def _mm(a, b, *, name, nt=False, out_dtypes=(F32,), epilogue=None, extra=None):
    m, k = a.shape
    n = b.shape[0] if nt else b.shape[1]
    tm, tn, tk = _pick(m, 1024, 8), _pick(n, 512), _pick(k, 3072)
    nk = k // tk
    n_out = len(out_dtypes)

    def body(*refs):
        a_ref, b_ref = refs[0], refs[1]
        e_ref = refs[2] if extra is not None else None
        first_out = 2 + (extra is not None)
        out_refs = refs[first_out:first_out + n_out]

        def finish(acc):
            e = None if e_ref is None else e_ref[...]
            outs = (acc,) if epilogue is None else epilogue(acc, e)
            for r, o in zip(out_refs, outs):
                r[...] = o.astype(r.dtype)

        part = _dot(a_ref[...], b_ref[...], NT if nt else NN)
        if nk == 1:
            finish(part)
        else:
            acc_ref = refs[first_out + n_out]
            kk = pl.program_id(2)

            @pl.when(kk == 0)
            def _():
                acc_ref[...] = part

            @pl.when(kk > 0)
            def _():
                acc_ref[...] += part

            @pl.when(kk == nk - 1)
            def _():
                finish(acc_ref[...])

    in_specs = [
        pl.BlockSpec((tm, tk), lambda i, j, kk: (i, kk)),
        pl.BlockSpec((tn, tk), lambda i, j, kk: (j, kk)) if nt else pl.BlockSpec((tk, tn), lambda i, j, kk: (kk, j)),
    ]
    ins = [a.astype(MXU_DTYPE), b.astype(MXU_DTYPE)]
    if extra is not None:
        in_specs.append(pl.BlockSpec((tm, tn), lambda i, j, kk: (i, j)))
        ins.append(extra)
    outs = pl.pallas_call(
        body,
        out_shape=tuple(jax.ShapeDtypeStruct((m, n), d) for d in out_dtypes),
        grid=(m // tm, n // tn, nk),
        in_specs=in_specs,
        out_specs=tuple(pl.BlockSpec((tm, tn), lambda i, j, kk: (i, j)) for _ in out_dtypes),
        scratch_shapes=[pltpu.VMEM((tm, tn), F32)] if nk > 1 else [],
        compiler_params=_params("parallel", "parallel", "arbitrary"),
        name=name,
    )(*ins)
    return outs[0] if n_out == 1 else outs


def _relu_sq(acc, _):
    act = jnp.maximum(acc, 0.0)
    return act, act * act


def _relu_sq_grad(acc, act):
    return (acc * (2.0 * act.astype(F32)),)


def _add_alpha(acc, du):
    return (acc + ALPHA * du,)


def _ln_fwd(x, mixed, g, b, *, name):
    s, d = x.shape
    tr = _pick(s, 256, 8)

    def body(x_ref, m_ref, g_ref, b_ref, y_ref, yb_ref, xh_ref, r_ref):
        u = ALPHA * x_ref[...] + m_ref[...]
        mu = jnp.mean(u, axis=-1, keepdims=True)
        xc = u - mu
        var = jnp.mean(xc * xc, axis=-1, keepdims=True)
        r = lax.rsqrt(var + LN_EPS)
        xh = xc * r
        y = xh * g_ref[...] + b_ref[...]
        y_ref[...] = y
        yb_ref[...] = y.astype(MXU_DTYPE)
        xh_ref[...] = xh
        r_ref[...] = r

    row = pl.BlockSpec((tr, d), lambda i: (i, 0))
    vec = pl.BlockSpec((1, d), lambda i: (0, 0))
    return pl.pallas_call(
        body,
        out_shape=(jax.ShapeDtypeStruct((s, d), F32), jax.ShapeDtypeStruct((s, d), MXU_DTYPE),
                   jax.ShapeDtypeStruct((s, d), F32), jax.ShapeDtypeStruct((s, 1), F32)),
        grid=(s // tr,),
        in_specs=[row, row, vec, vec],
        out_specs=(row, row, row, pl.BlockSpec((tr, 1), lambda i: (i, 0))),
        compiler_params=_params("parallel"),
        name=name,
    )(x, mixed, g.reshape(1, d), b.reshape(1, d))


def _ln_bwd(dy, xh, r, g, *, name):
    s, d = dy.shape
    tr = _pick(s, 256, 8)

    def body(dy_ref, xh_ref, r_ref, g_ref, du_ref, dub_ref, dg_ref, db_ref):
        dyv, xhv = dy_ref[...], xh_ref[...]
        dxh = dyv * g_ref[...]
        c1 = jnp.mean(dxh, axis=-1, keepdims=True)
        c2 = jnp.mean(dxh * xhv, axis=-1, keepdims=True)
        du = r_ref[...] * (dxh - c1 - xhv * c2)
        du_ref[...] = du
        dub_ref[...] = du.astype(MXU_DTYPE)

        @pl.when(pl.program_id(0) == 0)
        def _():
            dg_ref[...] = jnp.zeros_like(dg_ref)
            db_ref[...] = jnp.zeros_like(db_ref)

        dg_ref[...] += jnp.sum(dyv * xhv, axis=0, keepdims=True)
        db_ref[...] += jnp.sum(dyv, axis=0, keepdims=True)

    row = pl.BlockSpec((tr, d), lambda i: (i, 0))
    vec = pl.BlockSpec((1, d), lambda i: (0, 0))
    return pl.pallas_call(
        body,
        out_shape=(jax.ShapeDtypeStruct((s, d), F32), jax.ShapeDtypeStruct((s, d), MXU_DTYPE),
                   jax.ShapeDtypeStruct((1, d), F32), jax.ShapeDtypeStruct((1, d), F32)),
        grid=(s // tr,),
        in_specs=[row, row, pl.BlockSpec((tr, 1), lambda i: (i, 0)), vec],
        out_specs=(row, row, vec, vec),
        compiler_params=_params("arbitrary"),
        name=name,
    )(dy, xh, r, g.reshape(1, d))


def _rms_fwd(x, g, *, name):
    s, d = x.shape
    tr = _pick(s, 512, 8)

    def body(x_ref, g_ref, y_ref, r_ref):
        xv = x_ref[...]
        r = lax.rsqrt(jnp.mean(xv * xv, axis=-1, keepdims=True) + RMS_EPS)
        y_ref[...] = (xv * r * g_ref[...]).astype(y_ref.dtype)
        r_ref[...] = r

    return pl.pallas_call(
        body,
        out_shape=(jax.ShapeDtypeStruct((s, d), MXU_DTYPE), jax.ShapeDtypeStruct((s, 1), F32)),
        grid=(s // tr,),
        in_specs=[pl.BlockSpec((tr, d), lambda i: (i, 0)), pl.BlockSpec((1, d), lambda i: (0, 0))],
        out_specs=(pl.BlockSpec((tr, d), lambda i: (i, 0)), pl.BlockSpec((tr, 1), lambda i: (i, 0))),
        compiler_params=_params("parallel"),
        name=name,
    )(x, g.reshape(1, d))


def _rms_bwd(dy, x, r, g, *, name):
    s, d = x.shape
    tr = _pick(s, 512, 8)

    def body(dy_ref, x_ref, r_ref, g_ref, dx_ref, dg_ref):
        dyv, rv = dy_ref[...], r_ref[...]
        xn = x_ref[...] * rv
        dxn = dyv * g_ref[...]
        dx_ref[...] = rv * (dxn - xn * jnp.mean(dxn * xn, axis=-1, keepdims=True))

        @pl.when(pl.program_id(0) == 0)
        def _():
            dg_ref[...] = jnp.zeros_like(dg_ref)

        dg_ref[...] += jnp.sum(dyv * xn, axis=0, keepdims=True)

    row = pl.BlockSpec((tr, d), lambda i: (i, 0))
    vec = pl.BlockSpec((1, d), lambda i: (0, 0))
    return pl.pallas_call(
        body,
        out_shape=(jax.ShapeDtypeStruct((s, d), F32), jax.ShapeDtypeStruct((1, d), F32)),
        grid=(s // tr,),
        in_specs=[row, row, pl.BlockSpec((tr, 1), lambda i: (i, 0)), vec],
        out_specs=(row, vec),
        compiler_params=_params("arbitrary"),
        name=name,
    )(dy, x, r, g.reshape(1, d))


def _rope_tables(s, inverse):
    inv_freq = ROPE_BASE ** (-jnp.arange(0, D_ROPE, 2, dtype=F32) / D_ROPE)
    ang = jnp.arange(s, dtype=F32)[:, None] * inv_freq[None, :]
    cos, sin = jnp.cos(ang), jnp.sin(ang)
    if inverse:
        sin = -sin
    half = D_ROPE // 2
    one, zero = jnp.ones((s, D_NOPE), F32), jnp.zeros((s, D_NOPE), F32)
    pad1, pad0 = jnp.ones((s, LANES - D_NOPE - D_ROPE), F32), jnp.zeros((s, LANES - D_NOPE - D_ROPE), F32)
    zh = jnp.zeros((s, half), F32)
    c = jnp.concatenate([one, cos, cos, pad1], axis=1)
    s_lo = jnp.concatenate([zero, -sin, zh, pad0], axis=1)
    s_hi = jnp.concatenate([zero, zh, sin, pad0], axis=1)
    return c, s_lo, s_hi


def _rope(x, tables, *, out_dtype, head_sum=False, name):
    h, s, w = x.shape
    ts = _pick(s, 512, 8)
    half = D_ROPE // 2

    def body(x_ref, c_ref, lo_ref, hi_ref, y_ref, *sum_ref):
        xv = x_ref[0]
        y = (xv * c_ref[...] + pltpu.roll(xv, w - half, 1) * lo_ref[...]
             + pltpu.roll(xv, half, 1) * hi_ref[...])
        y_ref[0] = y.astype(y_ref.dtype)
        if head_sum:
            @pl.when(pl.program_id(1) == 0)
            def _():
                sum_ref[0][...] = jnp.zeros_like(sum_ref[0])

            sum_ref[0][...] += y

    tab = pl.BlockSpec((ts, w), lambda i, hh: (i, 0))
    blk = pl.BlockSpec((1, ts, w), lambda i, hh: (hh, i, 0))
    out_shape = [jax.ShapeDtypeStruct((h, s, w), out_dtype)]
    out_specs = [blk]
    if head_sum:
        out_shape.append(jax.ShapeDtypeStruct((s, w), F32))
        out_specs.append(tab)
    res = pl.pallas_call(
        body,
        out_shape=tuple(out_shape),
        grid=(s // ts, h),
        in_specs=[blk, tab, tab, tab],
        out_specs=tuple(out_specs),
        compiler_params=_params("parallel", "arbitrary"),
        name=name,
    )(x, *tables)
    return res if head_sum else res[0]


def _lane_rows(vals, h):
    return jnp.broadcast_to(jnp.asarray(vals, F32).reshape(h, 1, 1), (h, 1, BLK))


def _attn_pieces(i, j, row, col, n_back):
    rel = (i - j) * BLK + row - col
    valid = rel >= 0
    if n_back is not None:
        valid = valid & (rel <= n_back)
    return rel, valid


def _attn_fwd(q, k, v, *, scale, n_back, bps, slopes=None, sinks=None, name):
    h, s, dq = q.shape
    hk, _, dv = v.shape
    group = h // hk
    nq = s // BLK
    n_prev = nq if n_back is None else -(-n_back // BLK)
    use_bias, use_sink = slopes is not None, sinks is not None

    def body(*refs):
        q_ref, k_ref, v_ref = refs[:3]
        pos = 3
        slope_ref = refs[pos] if use_bias else None
        pos += use_bias
        sink_ref = refs[pos] if use_sink else None
        pos += use_sink
        o_ref, lse_ref = refs[pos], refs[pos + 1]
        i = pl.program_id(1)
        qb = q_ref[0]
        row = lax.broadcasted_iota(jnp.int32, (BLK, BLK), 0)
        col = lax.broadcasted_iota(jnp.int32, (BLK, BLK), 1)
        j_lo = jnp.maximum(i - n_prev, (i // bps) * bps)

        def step(j, carry):
            m, l, acc = carry
            off = pl.multiple_of(j * BLK, BLK)
            kb = k_ref[0, pl.ds(off, BLK), :]
            vb = v_ref[0, pl.ds(off, BLK), :]
            sc = _dot(qb, kb, NT) * scale
            rel, valid = _attn_pieces(i, j, row, col, n_back)
            if use_bias:
                sc = sc - slope_ref[0] * rel.astype(F32)
            sc = jnp.where(valid, sc, NEG)
            m_new = jnp.maximum(m, jnp.max(sc, axis=-1, keepdims=True))
            a = jnp.exp(m - m_new)
            p = jnp.where(valid, jnp.exp(sc - m_new), 0.0)
            l = a * l + jnp.sum(p, axis=-1, keepdims=True)
            acc = a * acc + _dot(p.astype(MXU_DTYPE), vb, NN)
            return m_new, l, acc

        if use_sink:
            m0 = jnp.zeros((BLK, 1), F32) + sink_ref[0][:, 0:1]
            l0 = jnp.ones((BLK, 1), F32)
        else:
            m0 = jnp.full((BLK, 1), NEG, F32)
            l0 = jnp.zeros((BLK, 1), F32)
        m, l, acc = lax.fori_loop(j_lo, i + 1, step, (m0, l0, jnp.zeros((BLK, dv), F32)))
        o_ref[0] = acc / l
        lse_ref[0] = m + jnp.log(l)

    ins = [q, k, v]
    in_specs = [
        pl.BlockSpec((1, BLK, dq), lambda hh, i: (hh, i, 0)),
        pl.BlockSpec((1, s, dq), lambda hh, i: (hh // group, 0, 0)),
        pl.BlockSpec((1, s, dv), lambda hh, i: (hh // group, 0, 0)),
    ]
    per_head = pl.BlockSpec((1, 1, BLK), lambda hh, i: (hh, 0, 0))
    if use_bias:
        ins.append(_lane_rows(slopes, h))
        in_specs.append(per_head)
    if use_sink:
        ins.append(_lane_rows(sinks, h))
        in_specs.append(per_head)
    return pl.pallas_call(
        body,
        out_shape=(jax.ShapeDtypeStruct((h, s, dv), F32), jax.ShapeDtypeStruct((h, s, 1), F32)),
        grid=(h, nq),
        in_specs=in_specs,
        out_specs=(pl.BlockSpec((1, BLK, dv), lambda hh, i: (hh, i, 0)),
                   pl.BlockSpec((1, BLK, 1), lambda hh, i: (hh, i, 0))),
        compiler_params=_params("parallel", "arbitrary"),
        name=name,
    )(*ins)


def _attn_bwd(q, k, v, do, o, lse, *, scale, n_back, bps, slopes=None, sinks=None, name):
    h, s, dq = q.shape
    hk, _, dv = v.shape
    group = h // hk
    nq = s // BLK
    n_prev = nq if n_back is None else -(-n_back // BLK)
    use_bias, use_sink = slopes is not None, sinks is not None

    def body(*refs):
        q_ref, k_ref, v_ref, do_ref, o_ref, lse_ref = refs[:6]
        pos = 6
        slope_ref = refs[pos] if use_bias else None
        pos += use_bias
        sink_ref = refs[pos] if use_sink else None
        pos += use_sink
        dq_ref, dk_ref, dv_ref = refs[pos:pos + 3]
        dsink_ref = refs[pos + 3] if use_sink else None
        hh, i = pl.program_id(0), pl.program_id(1)

        @pl.when((hh % group == 0) & (i == 0))
        def _():
            dk_ref[...] = jnp.zeros_like(dk_ref)
            dv_ref[...] = jnp.zeros_like(dv_ref)

        qb = q_ref[0]
        dof = do_ref[0]
        dob = dof.astype(MXU_DTYPE)
        lse_b = lse_ref[0]
        delta = jnp.sum(dof * o_ref[0], axis=-1, keepdims=True)
        row = lax.broadcasted_iota(jnp.int32, (BLK, BLK), 0)
        col = lax.broadcasted_iota(jnp.int32, (BLK, BLK), 1)
        j_lo = jnp.maximum(i - n_prev, (i // bps) * bps)

        def step(j, dq_acc):
            off = pl.multiple_of(j * BLK, BLK)
            kb = k_ref[0, pl.ds(off, BLK), :]
            vb = v_ref[0, pl.ds(off, BLK), :]
            sc = _dot(qb, kb, NT) * scale
            rel, valid = _attn_pieces(i, j, row, col, n_back)
            if use_bias:
                sc = sc - slope_ref[0] * rel.astype(F32)
            p = jnp.where(valid, jnp.exp(jnp.where(valid, sc, NEG) - lse_b), 0.0)
            dp = _dot(dob, vb, NT)
            ds = (p * (dp - delta) * scale).astype(MXU_DTYPE)
            dk_ref[0, pl.ds(off, BLK), :] += _dot(ds, qb, TN)
            dv_ref[0, pl.ds(off, BLK), :] += _dot(p.astype(MXU_DTYPE), dob, TN)
            return dq_acc + _dot(ds, kb, NN)

        dq_ref[0] = lax.fori_loop(j_lo, i + 1, step, jnp.zeros((BLK, dq), F32))

        if use_sink:
            @pl.when(i == 0)
            def _():
                dsink_ref[...] = jnp.zeros_like(dsink_ref)

            p_sink = jnp.exp(sink_ref[0][:, 0:1] - lse_b)
            dsink_ref[0] += -jnp.sum(p_sink * delta, axis=0, keepdims=True)

    ins = [q, k, v, do, o, lse]
    qspec = lambda w: pl.BlockSpec((1, BLK, w), lambda hh, i: (hh, i, 0))
    kspec = lambda w: pl.BlockSpec((1, s, w), lambda hh, i: (hh // group, 0, 0))
    in_specs = [qspec(dq), kspec(dq), kspec(dv), qspec(dv), qspec(dv), qspec(1)]
    per_head = pl.BlockSpec((1, 1, BLK), lambda hh, i: (hh, 0, 0))
    if use_bias:
        ins.append(_lane_rows(slopes, h))
        in_specs.append(per_head)
    if use_sink:
        ins.append(_lane_rows(sinks, h))
        in_specs.append(per_head)
    out_shape = [jax.ShapeDtypeStruct((h, s, dq), F32), jax.ShapeDtypeStruct((hk, s, dq), F32),
                 jax.ShapeDtypeStruct((hk, s, dv), F32)]
    out_specs = [qspec(dq), kspec(dq), kspec(dv)]
    if use_sink:
        out_shape.append(jax.ShapeDtypeStruct((h, 1, BLK), F32))
        out_specs.append(per_head)
    return pl.pallas_call(
        body,
        out_shape=tuple(out_shape),
        grid=(h, nq),
        in_specs=in_specs,
        out_specs=tuple(out_specs),
        compiler_params=_params("arbitrary", "arbitrary"),
        name=name,
    )(*ins)


def _merge(outs, lses, *, name):
    h, s, dv = outs[0].shape
    ts = _pick(s, 512, 8)

    def body(o0, o1, o2, l0, l1, l2, ob_ref, lt_ref):
        a, b, c = l0[0], l1[0], l2[0]
        m = jnp.maximum(jnp.maximum(a, b), c)
        ea, eb, ec = jnp.exp(a - m), jnp.exp(b - m), jnp.exp(c - m)
        den = ea + eb + ec
        ob_ref[0] = (ea / den) * o0[0] + (eb / den) * o1[0] + (ec / den) * o2[0]
        lt_ref[0] = m + jnp.log(den)

    ospec = pl.BlockSpec((1, ts, dv), lambda hh, i: (hh, i, 0))
    lspec = pl.BlockSpec((1, ts, 1), lambda hh, i: (hh, i, 0))
    return pl.pallas_call(
        body,
        out_shape=(jax.ShapeDtypeStruct((h, s, dv), F32), jax.ShapeDtypeStruct((h, s, 1), F32)),
        grid=(h, s // ts),
        in_specs=[ospec] * 3 + [lspec] * 3,
        out_specs=(ospec, lspec),
        compiler_params=_params("parallel", "parallel"),
        name=name,
    )(*outs, *lses)


def _split_cumsum(x, tri):
    hi = x.astype(BF16)
    r1 = x - hi.astype(F32)
    mid = r1.astype(BF16)
    lo = (r1 - mid.astype(F32)).astype(BF16)
    return _dot(hi, tri, NN) + _dot(mid, tri, NN) + _dot(lo, tri, NN)


def _sb_logs(z, strict):
    l1 = jnp.log(1.0 + jnp.exp(-jnp.abs(z)))
    log_beta = jnp.minimum(z, 0.0) - l1
    log_keep = jnp.where(strict, -jnp.maximum(z, 0.0) - l1, 0.0)
    return log_beta, log_keep


def _sb_fwd(q, k, v, *, scale, name):
    h, s, dh = q.shape
    nq = s // BLK

    def body(q_ref, k_ref, v_ref, o_ref, t_ref):
        i = pl.program_id(1)
        qb = q_ref[0]
        row = lax.broadcasted_iota(jnp.int32, (BLK, BLK), 0)
        col = lax.broadcasted_iota(jnp.int32, (BLK, BLK), 1)
        after = (row > col).astype(BF16)

        def step(jj, carry):
            run, acc = carry
            j = i - jj
            off = pl.multiple_of(j * BLK, BLK)
            kb = k_ref[0, pl.ds(off, BLK), :]
            vb = v_ref[0, pl.ds(off, BLK), :]
            z = _dot(qb, kb, NT) * scale
            strict = ((i - j) * BLK + row - col) > 0
            log_beta, log_keep = _sb_logs(z, strict)
            a = run + _split_cumsum(log_keep, after)
            w = jnp.where(strict, jnp.exp(log_beta + a), 0.0)
            acc = acc + _dot(w.astype(MXU_DTYPE), vb, NN)
            return run + jnp.sum(log_keep, axis=-1, keepdims=True), acc

        run, acc = lax.fori_loop(0, i + 1, step, (jnp.zeros((BLK, 1), F32), jnp.zeros((BLK, dh), F32)))
        o_ref[0] = acc
        t_ref[0] = run

    qspec = pl.BlockSpec((1, BLK, dh), lambda hh, i: (hh, i, 0))
    kspec = pl.BlockSpec((1, s, dh), lambda hh, i: (hh, 0, 0))
    return pl.pallas_call(
        body,
        out_shape=(jax.ShapeDtypeStruct((h, s, dh), F32), jax.ShapeDtypeStruct((h, s, 1), F32)),
        grid=(h, nq),
        in_specs=[qspec, kspec, kspec],
        out_specs=(qspec, pl.BlockSpec((1, BLK, 1), lambda hh, i: (hh, i, 0))),
        compiler_params=_params("parallel", "arbitrary"),
        name=name,
    )(q, k, v)


def _sb_bwd(q, k, v, do, total, *, scale, name):
    h, s, dh = q.shape
    nq = s // BLK

    def body(q_ref, k_ref, v_ref, do_ref, t_ref, dq_ref, dk_ref, dv_ref):
        i = pl.program_id(1)

        @pl.when(i == 0)
        def _():
            dk_ref[...] = jnp.zeros_like(dk_ref)
            dv_ref[...] = jnp.zeros_like(dv_ref)

        qb = q_ref[0]
        dob = do_ref[0].astype(MXU_DTYPE)
        tot = t_ref[0]
        row = lax.broadcasted_iota(jnp.int32, (BLK, BLK), 0)
        col = lax.broadcasted_iota(jnp.int32, (BLK, BLK), 1)
        upto = (row <= col).astype(BF16)
        before = (row < col).astype(BF16)

        def step(j, carry):
            run_keep, run_g, dq_acc = carry
            off = pl.multiple_of(j * BLK, BLK)
            kb = k_ref[0, pl.ds(off, BLK), :]
            vb = v_ref[0, pl.ds(off, BLK), :]
            z = _dot(qb, kb, NT) * scale
            strict = ((i - j) * BLK + row - col) > 0
            log_beta, log_keep = _sb_logs(z, strict)
            a = tot - run_keep - _split_cumsum(log_keep, upto)
            w = jnp.where(strict, jnp.exp(log_beta + a), 0.0)
            g = w * _dot(dob, vb, NT)
            prefix = run_g + _split_cumsum(g, before)
            e = jnp.exp(-jnp.abs(z))
            sig = jnp.where(z >= 0.0, 1.0, e) / (1.0 + e)
            dz = (jnp.where(strict, g * (1.0 - sig) - sig * prefix, 0.0) * scale).astype(MXU_DTYPE)
            dk_ref[0, pl.ds(off, BLK), :] += _dot(dz, qb, TN)
            dv_ref[0, pl.ds(off, BLK), :] += _dot(w.astype(MXU_DTYPE), dob, TN)
            return (run_keep + jnp.sum(log_keep, axis=-1, keepdims=True),
                    run_g + jnp.sum(g, axis=-1, keepdims=True), dq_acc + _dot(dz, kb, NN))

        zero = jnp.zeros((BLK, 1), F32)
        _, _, dq = lax.fori_loop(0, i + 1, step, (zero, zero, jnp.zeros((BLK, dh), F32)))
        dq_ref[0] = dq

    qspec = pl.BlockSpec((1, BLK, dh), lambda hh, i: (hh, i, 0))
    kspec = pl.BlockSpec((1, s, dh), lambda hh, i: (hh, 0, 0))
    shp = jax.ShapeDtypeStruct((h, s, dh), F32)
    return pl.pallas_call(
        body,
        out_shape=(shp, shp, shp),
        grid=(h, nq),
        in_specs=[qspec, kspec, kspec, qspec, pl.BlockSpec((1, BLK, 1), lambda hh, i: (hh, i, 0))],
        out_specs=(qspec, kspec, kspec),
        compiler_params=_params("arbitrary", "arbitrary"),
        name=name,
    )(q, k, v, do, total)


def _loss_head(y, target, *, name):
    s, d = y.shape
    tr = _pick(s, 256, 8)

    def body(y_ref, t_ref, dy_ref, loss_ref):
        err = y_ref[...] - t_ref[...]
        dy_ref[...] = err * (1.0 / d)

        @pl.when(pl.program_id(0) == 0)
        def _():
            loss_ref[...] = jnp.zeros_like(loss_ref)

        per_tok = jnp.mean(err * err, axis=-1, keepdims=True)
        loss_ref[...] += 0.5 * jnp.sum(per_tok, axis=0, keepdims=True)

    row = pl.BlockSpec((tr, d), lambda i: (i, 0))
    return pl.pallas_call(
        body,
        out_shape=(jax.ShapeDtypeStruct((s, d), F32), jax.ShapeDtypeStruct((1, LANES), F32)),
        grid=(s // tr,),
        in_specs=[row, row],
        out_specs=(row, pl.BlockSpec((1, LANES), lambda i: (0, 0))),
        compiler_params=_params("arbitrary"),
        name=name,
    )(y, target)


def _adamw(w, g, m, v, *, name):
    r, c = w.shape
    tr = _pick(r, 256, 8)

    def body(w_ref, g_ref, m_ref, v_ref, d_ref, m2_ref, v2_ref):
        gv = g_ref[...]
        m2 = ADAM_B1 * m_ref[...] + (1.0 - ADAM_B1) * gv
        v2 = ADAM_B2 * v_ref[...] + (1.0 - ADAM_B2) * (gv * gv)
        m_hat = m2 / (1.0 - ADAM_B1 ** ADAM_STEP)
        v_hat = v2 / (1.0 - ADAM_B2 ** ADAM_STEP)
        d_ref[...] = -ADAM_LR * (m_hat / (jnp.sqrt(v_hat) + ADAM_EPS) + ADAM_WD * w_ref[...])
        m2_ref[...] = m2
        v2_ref[...] = v2

    blk = pl.BlockSpec((tr, c), lambda i: (i, 0))
    shp = jax.ShapeDtypeStruct((r, c), F32)
    return pl.pallas_call(
        body,
        out_shape=(shp, shp, shp),
        grid=(r // tr,),
        in_specs=[blk] * 4,
        out_specs=(blk, blk, blk),
        compiler_params=_params("parallel"),
        name=name,
    )(w, g, m, v)


def _pair_sum(mine, recv, my_c, *, name):
    _, r, c = mine.shape

    def body(c_ref, a_ref, b_ref, o_ref):
        o_ref[0] = (a_ref[0].astype(F32) + b_ref[0].astype(F32)).astype(o_ref.dtype)

    grid_spec = pltpu.PrefetchScalarGridSpec(
        num_scalar_prefetch=1,
        grid=(4, r // PACK_ROWS),
        in_specs=[pl.BlockSpec((1, PACK_ROWS, c), lambda kk, i, c_ref: (2 * kk + c_ref[0], i, 0)),
                  pl.BlockSpec((1, PACK_ROWS, c), lambda kk, i, c_ref: (kk, i, 0))],
        out_specs=pl.BlockSpec((1, PACK_ROWS, c), lambda kk, i, c_ref: (kk, i, 0)),
    )
    return pl.pallas_call(
        body,
        out_shape=jax.ShapeDtypeStruct((4, r, c), mine.dtype),
        grid_spec=grid_spec,
        compiler_params=_params("parallel", "parallel"),
        name=name,
    )(my_c.reshape(1).astype(jnp.int32), mine, recv)


def _final_sum(partial, recv, my_chip, *, name):
    _, r, c = partial.shape

    def body(chip_ref, p_ref, r0, r1, r2, o_ref):
        o_ref[...] = ((p_ref[0].astype(F32) + r0[0].astype(F32)) + r1[0].astype(F32)) + r2[0].astype(F32)

    def slot(n):
        return pl.BlockSpec((1, PACK_ROWS, c), lambda i, chip_ref: (n, i, 0))

    grid_spec = pltpu.PrefetchScalarGridSpec(
        num_scalar_prefetch=1,
        grid=(r // PACK_ROWS,),
        in_specs=[pl.BlockSpec((1, PACK_ROWS, c), lambda i, chip_ref: (chip_ref[0], i, 0)), slot(0), slot(1), slot(2)],
        out_specs=pl.BlockSpec((PACK_ROWS, c), lambda i, chip_ref: (i, 0)),
    )
    return pl.pallas_call(
        body,
        out_shape=jax.ShapeDtypeStruct((r, c), F32),
        grid_spec=grid_spec,
        compiler_params=_params("parallel"),
        name=name,
    )(my_chip.reshape(1).astype(jnp.int32), partial, recv, recv, recv)


def _sum_devices(stack, *, name):
    n, r, c = stack.shape

    def body(s_ref, o_ref):
        acc = s_ref[0]
        for dev in range(1, n):
            acc = acc + s_ref[dev]
        o_ref[...] = acc

    return pl.pallas_call(
        body,
        out_shape=jax.ShapeDtypeStruct((r, c), F32),
        in_specs=[pl.BlockSpec(memory_space=pltpu.VMEM)],
        out_specs=pl.BlockSpec(memory_space=pltpu.VMEM),
        name=name,
    )(stack)


def _mesh_pos():
    return lax.axis_index("x"), lax.axis_index("y"), lax.axis_index("c")


def _all_gather(shard, *, name):
    r, c = shard.shape

    def body(x_ref, out_ref, send_sems, recv_sems, local_sem):
        x, y, cc = _mesh_pos()
        me, sibling = (x, y, cc), (x, y, 1 - cc)
        chips = [(1 - x, y), (x, 1 - y), (1 - x, 1 - y)]

        def rows(px, py, pc):
            return out_ref.at[4 * px + 2 * py + pc]

        def copy(kk, block, to, src=None):
            return pltpu.make_async_remote_copy(
                src_ref=rows(*block) if src is None else src, dst_ref=rows(*block),
                send_sem=send_sems.at[kk], recv_sem=recv_sems.at[kk], device_id=to, device_id_type=MESH)

        mine = pltpu.make_async_copy(x_ref, rows(*me), local_sem)
        mine.start()
        first = [copy(0, me, sibling, src=x_ref)]
        first += [copy(1 + j, me, (*chip, cc), src=x_ref) for j, chip in enumerate(chips)]
        for cp in first:
            cp.start()
        passed = [copy(4 + j, (*chip, cc), sibling) for j, chip in enumerate(chips)]
        for j, chip in enumerate(chips):
            copy(1 + j, (*chip, cc), me).wait_recv()
            passed[j].start()
        copy(0, sibling, me).wait_recv()
        for j, chip in enumerate(chips):
            copy(4 + j, (*chip, 1 - cc), me).wait_recv()
        for cp in first + passed:
            cp.wait_send()
        mine.wait()

    return pl.pallas_call(
        body,
        out_shape=jax.ShapeDtypeStruct((N_DEV, r, c), shard.dtype),
        in_specs=[ANY],
        out_specs=ANY,
        scratch_shapes=[pltpu.SemaphoreType.DMA((7,)), pltpu.SemaphoreType.DMA((7,)), pltpu.SemaphoreType.DMA],
        name=name,
    )(shard)


def _to_sibling(blocks, *, name):
    _, r, c = blocks.shape

    def body(g_ref, out_ref, send_sems, recv_sems):
        x, y, cc = _mesh_pos()
        copies = [
            pltpu.make_async_remote_copy(
                src_ref=g_ref.at[2 * chip + (1 - cc)], dst_ref=out_ref.at[chip],
                send_sem=send_sems.at[chip], recv_sem=recv_sems.at[chip],
                device_id=(x, y, 1 - cc), device_id_type=MESH)
            for chip in range(4)
        ]
        for cp in copies:
            cp.start()
        for cp in copies:
            cp.wait_recv()
        for cp in copies:
            cp.wait_send()

    return pl.pallas_call(
        body,
        out_shape=jax.ShapeDtypeStruct((4, r, c), blocks.dtype),
        in_specs=[ANY],
        out_specs=ANY,
        scratch_shapes=[pltpu.SemaphoreType.DMA((4,)), pltpu.SemaphoreType.DMA((4,))],
        name=name,
    )(blocks)


def _to_chips(partial, *, name):
    _, r, c = partial.shape

    def body(p_ref, out_ref, send_sems, recv_sems):
        x, y, cc = _mesh_pos()
        chips = [(1 - x, y), (x, 1 - y), (1 - x, 1 - y)]
        copies = [
            pltpu.make_async_remote_copy(
                src_ref=p_ref.at[2 * px + py], dst_ref=out_ref.at[j],
                send_sem=send_sems.at[j], recv_sem=recv_sems.at[j],
                device_id=(px, py, cc), device_id_type=MESH)
            for j, (px, py) in enumerate(chips)
        ]
        for cp in copies:
            cp.start()
        for cp in copies:
            cp.wait_recv()
        for cp in copies:
            cp.wait_send()

    return pl.pallas_call(
        body,
        out_shape=jax.ShapeDtypeStruct((3, r, c), partial.dtype),
        in_specs=[ANY],
        out_specs=ANY,
        scratch_shapes=[pltpu.SemaphoreType.DMA((3,)), pltpu.SemaphoreType.DMA((3,))],
        name=name,
    )(partial)


_BIG = (("even_w_in", "col"), ("even_w_out", "col"), ("odd_w_in", "col"), ("odd_w_uq", "col"),
        ("odd_w_ukv", "col"), ("odd_w_out", "row"), ("mlp_w1", "col"), ("mlp_w2", "row"))


def _packed_rows(n_elems):
    rows = -(-n_elems // PACK_COLS)
    return -(-rows // PACK_ROWS) * PACK_ROWS


def _pack_flat(parts, dtype):
    lead = parts[0].shape[:-1]
    n = sum(p.shape[-1] for p in parts)
    rows = _packed_rows(n)
    pad = jnp.zeros(lead + (rows * PACK_COLS - n,), dtype)
    return jnp.concatenate([p.astype(dtype) for p in parts] + [pad], axis=-1).reshape(lead + (rows, PACK_COLS))


def _unshard(gathered, shard_shape, kind):
    l, r, c = shard_shape
    t = gathered.reshape(N_DEV, l, r, c)
    if kind == "col":
        return t.transpose(1, 2, 0, 3).reshape(l, r, N_DEV * c)
    return t.transpose(1, 0, 2, 3).reshape(l, N_DEV * r, c)


def _by_device(full, shard_shape, kind):
    l, r, c = shard_shape
    if kind == "col":
        t = full.reshape(l, r, N_DEV, c).transpose(2, 0, 1, 3)
    else:
        t = full.reshape(l, N_DEV, r, c).transpose(1, 0, 2, 3)
    return t.reshape(N_DEV, l * r * c)


def _alibi(n):
    return 2.0 ** (-8.0 * np.arange(1, n + 1, dtype=np.float32) / n)


def _heads(t, n):
    s = t.shape[0]
    return t.reshape(s, n, t.shape[1] // n).transpose(1, 0, 2)


def _unheads(t):
    n, s, dh = t.shape
    return t.transpose(1, 0, 2).reshape(s, n * dh)


def _to_strided(t, d):
    h, s, x = t.shape
    return t.reshape(h, s // d, d, x).transpose(0, 2, 1, 3).reshape(h, s, x)


def _from_strided(t, d):
    h, s, x = t.shape
    return t.reshape(h, d, s // d, x).transpose(0, 2, 1, 3).reshape(h, s, x)


def _local_step(x0, target, w, sinks, gq, gkv, ln1_g, ln1_b, ln2_g, ln2_b):
    s, d = x0.shape
    odd_in_pad = w["odd_w_in"].shape[1]
    scale_h = 1.0 / math.sqrt(HEAD_DIM)
    scale_d = 1.0 / math.sqrt(D_NOPE + D_ROPE)
    slopes_a = _alibi(A_Q_HEADS)
    slopes_b = _alibi(B_HEADS)
    nq = s // BLK
    bf = lambda t: t.astype(MXU_DTYPE)

    x0b = bf(x0)
    h_e = _mm(x0b, w["even_w_in"], out_dtypes=(MXU_DTYPE,), name="even_in_fwd")
    qa = _heads(h_e[:, :A_Q_W], A_Q_HEADS)
    ka = _heads(h_e[:, A_Q_W:A_Q_W + A_KV_W], A_KV_HEADS)
    va = _heads(h_e[:, A_Q_W + A_KV_W:A_Q_W + 2 * A_KV_W], A_KV_HEADS)
    a_cfg = dict(scale=scale_h, n_back=A_WINDOW - 1, bps=nq, slopes=slopes_a, sinks=sinks)
    oa, lse_a = _attn_fwd(qa, ka, va, name="swa_fwd", **a_cfg)
    b_qkv, b_cfg, b_out, b_lse = [], [], [], []
    base = A_Q_W + 2 * A_KV_W
    for gi, (window, dil) in enumerate(B_PATTERNS):
        blk = h_e[:, base + gi * 3 * B_W: base + (gi + 1) * 3 * B_W].reshape(s, 3, B_HEADS, HEAD_DIM)
        qkv = [_to_strided(blk[:, n].transpose(1, 0, 2), dil) for n in range(3)]
        cfg = dict(scale=scale_h, n_back=window // dil, bps=nq // dil, slopes=slopes_b * dil)
        o, lse = _attn_fwd(*qkv, name=f"dil{gi}_fwd", **cfg)
        b_qkv.append(qkv)
        b_cfg.append(cfg)
        b_out.append(_from_strided(o, dil))
        b_lse.append(_from_strided(lse, dil))
    ob, lse_b = _merge(b_out, b_lse, name="dil_merge")
    y_e = bf(jnp.concatenate([_unheads(oa), _unheads(ob)], axis=1))
    mixed = _mm(y_e, w["even_w_out"], name="even_out_fwd")
    x0n, x0nb, xh1_0, r1_0 = _ln_fwd(x0, mixed, ln1_g[0], ln1_b[0], name="ln1_fwd_l0")
    act0, hid0 = _mm(x0nb, w["mlp_w1"][0], out_dtypes=(MXU_DTYPE, MXU_DTYPE), epilogue=_relu_sq, name="mlp1_fwd_l0")
    mlp = _mm(hid0, w["mlp_w2"][0], name="mlp2_fwd_l0")
    x1, x1b, xh2_0, r2_0 = _ln_fwd(x0n, mlp, ln2_g[0], ln2_b[0], name="ln2_fwd_l0")

    h_o = _mm(x1b, w["odd_w_in"], name="odd_in_fwd")
    qc, kc, vc = (bf(_heads(h_o[:, n * C_W:(n + 1) * C_W], C_HEADS)) for n in range(3))
    oc, sb_total = _sb_fwd(qc, kc, vc, scale=scale_h, name="sb_fwd")
    o_cq, o_ckv, o_kr = 3 * C_W, 3 * C_W + D_Q_RANK, 3 * C_W + D_Q_RANK + D_KV_RANK
    cq, ckv, kr = h_o[:, o_cq:o_ckv], h_o[:, o_ckv:o_kr], h_o[:, o_kr:o_kr + D_ROPE]
    ncq, rq = _rms_fwd(cq, gq, name="rms_q_fwd")
    nckv, rkv = _rms_fwd(ckv, gkv, name="rms_kv_fwd")
    qd = _mm(ncq, w["odd_w_uq"], name="uq_fwd")
    kvd = _mm(nckv, w["odd_w_ukv"], name="ukv_fwd").reshape(s, D_HEADS, D_NOPE + D_V)
    lane_pad = LANES - D_NOPE - D_ROPE
    q_h = jnp.pad(qd.reshape(s, D_HEADS, D_NOPE + D_ROPE), ((0, 0), (0, 0), (0, lane_pad))).transpose(1, 0, 2)
    k_h = jnp.concatenate([kvd[:, :, :D_NOPE], jnp.broadcast_to(kr[:, None, :], (s, D_HEADS, D_ROPE)),
                           jnp.zeros((s, D_HEADS, lane_pad), F32)], axis=2).transpose(1, 0, 2)
    v_h = bf(kvd[:, :, D_NOPE:].transpose(1, 0, 2))
    rope_t = _rope_tables(s, inverse=False)
    q_r = _rope(q_h, rope_t, out_dtype=MXU_DTYPE, name="rope_q_fwd")
    k_r = _rope(k_h, rope_t, out_dtype=MXU_DTYPE, name="rope_k_fwd")
    d_cfg = dict(scale=scale_d, n_back=None, bps=nq)
    od, lse_d = _attn_fwd(q_r, k_r, v_h, name="mla_fwd", **d_cfg)
    y_o = bf(jnp.concatenate([_unheads(oc), _unheads(od)], axis=1))
    mixed = _mm(y_o, w["odd_w_out"], name="odd_out_fwd")
    x1n, x1nb, xh1_1, r1_1 = _ln_fwd(x1, mixed, ln1_g[1], ln1_b[1], name="ln1_fwd_l1")
    act1, hid1 = _mm(x1nb, w["mlp_w1"][1], out_dtypes=(MXU_DTYPE, MXU_DTYPE), epilogue=_relu_sq, name="mlp1_fwd_l1")
    mlp = _mm(hid1, w["mlp_w2"][1], name="mlp2_fwd_l1")
    y, _, xh2_1, r2_1 = _ln_fwd(x1n, mlp, ln2_g[1], ln2_b[1], name="ln2_fwd_l1")
    dy, loss_vec = _loss_head(y, target, name="loss_head")

    def mlp_block_bwd(g_out, layer, xh2, r2, xh1, r1, act, hid, xnb):
        du2, du2b, dg2, db2 = _ln_bwd(g_out, xh2, r2, ln2_g[layer], name=f"ln2_bwd_l{layer}")
        dpre = _mm(du2b, w["mlp_w2"][layer], nt=True, out_dtypes=(MXU_DTYPE,), epilogue=_relu_sq_grad,
                   extra=act, name=f"mlp2_dx_l{layer}")
        dw2 = _mm(hid.T, du2b, name=f"mlp2_dw_l{layer}")
        dw1 = _mm(xnb.T, dpre, name=f"mlp1_dw_l{layer}")
        dxn = _mm(dpre, w["mlp_w1"][layer], nt=True, epilogue=_add_alpha, extra=du2, name=f"mlp1_dx_l{layer}")
        du1, du1b, dg1, db1 = _ln_bwd(dxn, xh1, r1, ln1_g[layer], name=f"ln1_bwd_l{layer}")
        return du1, du1b, dw1, dw2, (dg1, db1, dg2, db2)

    du1, du1b, dw1_1, dw2_1, ln_1 = mlp_block_bwd(dy, 1, xh2_1, r2_1, xh1_1, r1_1, act1, hid1, x1nb)
    d_odd_out = _mm(y_o.T, du1b, name="odd_out_dw")
    dy_o = _mm(du1b, w["odd_w_out"], nt=True, name="odd_out_dx")
    doc, dod = _heads(dy_o[:, :C_W], C_HEADS), _heads(dy_o[:, C_W:], D_HEADS)
    dqc, dkc, dvc = _sb_bwd(qc, kc, vc, doc, sb_total, scale=scale_h, name="sb_bwd")
    dq_r, dk_r, dv_h = _attn_bwd(q_r, k_r, v_h, dod, od, lse_d, name="mla_bwd", **d_cfg)
    rope_inv = _rope_tables(s, inverse=True)
    dq_h = _rope(dq_r, rope_inv, out_dtype=F32, name="rope_q_bwd")
    dk_h, dkr_sum = _rope(dk_r, rope_inv, out_dtype=F32, head_sum=True, name="rope_k_bwd")
    dqd = bf(dq_h[:, :, :D_NOPE + D_ROPE].transpose(1, 0, 2).reshape(s, D_HEADS * (D_NOPE + D_ROPE)))
    dkvd = bf(jnp.concatenate([dk_h[:, :, :D_NOPE], dv_h], axis=2).transpose(1, 0, 2).reshape(s, D_HEADS * (D_NOPE + D_V)))
    d_uq = _mm(ncq.T, dqd, name="uq_dw")
    dncq = _mm(dqd, w["odd_w_uq"], nt=True, name="uq_dx")
    d_ukv = _mm(nckv.T, dkvd, name="ukv_dw")
    dnckv = _mm(dkvd, w["odd_w_ukv"], nt=True, name="ukv_dx")
    dcq, dgq = _rms_bwd(dncq, cq, rq, gq, name="rms_q_bwd")
    dckv, dgkv = _rms_bwd(dnckv, ckv, rkv, gkv, name="rms_kv_bwd")
    dh_o = bf(jnp.concatenate(
        [_unheads(dqc), _unheads(dkc), _unheads(dvc), dcq, dckv, dkr_sum[:, D_NOPE:D_NOPE + D_ROPE],
         jnp.zeros((s, odd_in_pad - ODD_IN), F32)], axis=1))
    d_odd_in = _mm(x1b.T, dh_o, name="odd_in_dw")[:, :ODD_IN]
    dx1 = _mm(dh_o, w["odd_w_in"], nt=True, epilogue=_add_alpha, extra=du1, name="odd_in_dx")

    du1, du1b, dw1_0, dw2_0, ln_0 = mlp_block_bwd(dx1, 0, xh2_0, r2_0, xh1_0, r1_0, act0, hid0, x0nb)
    d_even_out = _mm(y_e.T, du1b, name="even_out_dw")
    dy_e = _mm(du1b, w["even_w_out"], nt=True, name="even_out_dx")
    doa, dob = _heads(dy_e[:, :A_Q_W], A_Q_HEADS), _heads(dy_e[:, A_Q_W:], B_HEADS)
    dqa, dka, dva, dsink = _attn_bwd(qa, ka, va, doa, oa, lse_a, name="swa_bwd", **a_cfg)
    pieces = [_unheads(dqa), _unheads(dka), _unheads(dva)]
    for gi, (_, dil) in enumerate(B_PATTERNS):
        grads = _attn_bwd(*b_qkv[gi], _to_strided(dob, dil), _to_strided(ob, dil), _to_strided(lse_b, dil),
                          name=f"dil{gi}_bwd", **b_cfg[gi])
        pieces += [_unheads(_from_strided(t, dil)) for t in grads]
    dh_e = bf(jnp.concatenate(pieces, axis=1))
    d_even_in = _mm(x0b.T, dh_e, name="even_in_dw")
    grad_x = _mm(dh_e, w["even_w_in"], nt=True, epilogue=_add_alpha, extra=du1, name="even_in_dx")

    big = {"even_w_in": d_even_in[None], "even_w_out": d_even_out[None], "odd_w_in": d_odd_in[None],
           "odd_w_uq": d_uq[None], "odd_w_ukv": d_ukv[None], "odd_w_out": d_odd_out[None],
           "mlp_w1": jnp.stack([dw1_0, dw1_1]), "mlp_w2": jnp.stack([dw2_0, dw2_1])}
    ln = [jnp.concatenate([a, b], axis=0) for a, b in zip(ln_0, ln_1)]
    small = {"ln": ln, "sinks": dsink[:, 0, 0], "gq": dgq[0], "gkv": dgkv[0], "loss": loss_vec[0, :1]}
    return grad_x, big, small


def kernel(x, even_w_in, even_sinks, even_w_out, odd_w_in, odd_q_norm_g, odd_kv_norm_g, odd_w_uq, odd_w_ukv, odd_w_out, ln1_g, ln1_b, mlp_w1, mlp_w2, ln2_g, ln2_b, loss_target, m_even_w_in, m_even_sinks, m_even_w_out, m_odd_w_in, m_odd_q_norm_g, m_odd_kv_norm_g, m_odd_w_uq, m_odd_w_ukv, m_odd_w_out, m_ln1_g, m_ln1_b, m_mlp_w1, m_mlp_w2, m_ln2_g, m_ln2_b, v_even_w_in, v_even_sinks, v_even_w_out, v_odd_w_in, v_odd_q_norm_g, v_odd_kv_norm_g, v_odd_w_uq, v_odd_w_ukv, v_odd_w_out, v_ln1_g, v_ln1_b, v_mlp_w1, v_mlp_w2, v_ln2_g, v_ln2_b):
    weights = dict(even_w_in=even_w_in, even_sinks=even_sinks, even_w_out=even_w_out, odd_w_in=odd_w_in,
                   odd_q_norm_g=odd_q_norm_g, odd_kv_norm_g=odd_kv_norm_g, odd_w_uq=odd_w_uq, odd_w_ukv=odd_w_ukv,
                   odd_w_out=odd_w_out, ln1_g=ln1_g, ln1_b=ln1_b, mlp_w1=mlp_w1, mlp_w2=mlp_w2, ln2_g=ln2_g, ln2_b=ln2_b)
    mom_m = dict(even_w_in=m_even_w_in, even_sinks=m_even_sinks, even_w_out=m_even_w_out, odd_w_in=m_odd_w_in,
                 odd_q_norm_g=m_odd_q_norm_g, odd_kv_norm_g=m_odd_kv_norm_g, odd_w_uq=m_odd_w_uq, odd_w_ukv=m_odd_w_ukv,
                 odd_w_out=m_odd_w_out, ln1_g=m_ln1_g, ln1_b=m_ln1_b, mlp_w1=m_mlp_w1, mlp_w2=m_mlp_w2, ln2_g=m_ln2_g, ln2_b=m_ln2_b)
    mom_v = dict(even_w_in=v_even_w_in, even_sinks=v_even_sinks, even_w_out=v_even_w_out, odd_w_in=v_odd_w_in,
                 odd_q_norm_g=v_odd_q_norm_g, odd_kv_norm_g=v_odd_kv_norm_g, odd_w_uq=v_odd_w_uq, odd_w_ukv=v_odd_w_ukv,
                 odd_w_out=v_odd_w_out, ln1_g=v_ln1_g, ln1_b=v_ln1_b, mlp_w1=v_mlp_w1, mlp_w2=v_mlp_w2, ln2_g=v_ln2_g, ln2_b=v_ln2_b)
    order = list(weights)
    mx, my, mc = _mesh_pos()
    dev = 4 * mx + 2 * my + mc

    gains = [lax.bitcast_convert_type(weights[n], BF16).reshape(-1) for n in ("odd_q_norm_g", "odd_kv_norm_g")]
    shard_parts = [weights[n].reshape(-1) for n, _ in _BIG] + gains
    sizes = [p.shape[0] for p in shard_parts]
    gathered = _all_gather(_pack_flat(shard_parts, BF16), name="comm_weights_gather").reshape(N_DEV, -1)
    full, off = {}, 0
    for (n, kind), size in zip(_BIG, sizes):
        full[n] = _unshard(gathered[:, off:off + size], weights[n].shape, kind)
        off += size
    gq = lax.bitcast_convert_type(gathered[:, off:off + sizes[-2]].reshape(N_DEV, -1, 2), F32).reshape(-1)
    off += sizes[-2]
    gkv = lax.bitcast_convert_type(gathered[:, off:off + sizes[-1]].reshape(N_DEV, -1, 2), F32).reshape(-1)
    odd_in_pad = -(-ODD_IN // 512) * 512
    w_full = {n: (t if n.startswith("mlp") else t[0]) for n, t in full.items()}
    w_full["odd_w_in"] = jnp.pad(w_full["odd_w_in"], ((0, 0), (0, odd_in_pad - ODD_IN)))

    grad_x, big, small = _local_step(x[0], loss_target[0], w_full, even_sinks[0], gq, gkv, ln1_g, ln1_b, ln2_g, ln2_b)

    by_dev = _pack_flat([_by_device(big[n], weights[n].shape, kind) for n, kind in _BIG], BF16)
    from_sibling = _to_sibling(by_dev, name="comm_grads_sibling")
    partial = _pair_sum(by_dev, from_sibling, mc, name="grads_pair_sum")
    from_chips = _to_chips(partial, name="comm_grads_chips")
    reduced = _final_sum(partial, from_chips, 2 * mx + my, name="grads_final_sum").reshape(-1)
    grads, off = {}, 0
    for (n, _), size in zip(_BIG, sizes):
        grads[n] = reduced[off:off + size].reshape(weights[n].shape)
        off += size

    small_parts = [t.reshape(-1) for t in small["ln"]] + [small["sinks"], small["gq"], small["gkv"], small["loss"]]
    small_sizes = [p.shape[0] for p in small_parts]
    n_small = sum(small_sizes)
    small_rows = -(-n_small // (8 * LANES)) * 8
    small_flat = jnp.concatenate(small_parts + [jnp.zeros((small_rows * LANES - n_small,), F32)]).reshape(small_rows, LANES)
    totals = _sum_devices(_all_gather(small_flat, name="comm_small_gather"), name="small_sum").reshape(-1)
    tot, off = [], 0
    for size in small_sizes:
        tot.append(totals[off:off + size])
        off += size
    for i, n in enumerate(("ln1_g", "ln1_b", "ln2_g", "ln2_b")):
        grads[n] = tot[i].reshape(weights[n].shape)
    grads["even_sinks"] = tot[4].reshape(even_sinks.shape)
    n_q, n_kv = odd_q_norm_g.shape[1], odd_kv_norm_g.shape[1]
    grads["odd_q_norm_g"] = lax.dynamic_slice(tot[5], (dev * n_q,), (n_q,)).reshape(odd_q_norm_g.shape)
    grads["odd_kv_norm_g"] = lax.dynamic_slice(tot[6], (dev * n_kv,), (n_kv,)).reshape(odd_kv_norm_g.shape)
    loss = tot[7][0]

    delta, new_m, new_v = {}, {}, {}
    for n, _ in _BIG:
        shape = weights[n].shape
        two_d = (shape[0] * shape[1], shape[2])
        res = _adamw(weights[n].reshape(two_d), grads[n].reshape(two_d), mom_m[n].reshape(two_d),
                     mom_v[n].reshape(two_d), name=f"adamw_{n}")
        delta[n], new_m[n], new_v[n] = (t.reshape(shape) for t in res)
    small_names = [n for n in order if n not in dict(_BIG)]
    n_sm = sum(weights[n].size for n in small_names)
    sm_rows = -(-n_sm // (8 * LANES)) * 8

    def pack_small(group):
        flat = [group[n].reshape(-1) for n in small_names]
        return jnp.concatenate(flat + [jnp.zeros((sm_rows * LANES - n_sm,), F32)]).reshape(sm_rows, LANES)

    res = _adamw(pack_small(weights), pack_small(grads), pack_small(mom_m), pack_small(mom_v), name="adamw_small")
    off = 0
    for n in small_names:
        size = weights[n].size
        delta[n], new_m[n], new_v[n] = (t.reshape(-1)[off:off + size].reshape(weights[n].shape) for t in res)
        off += size

    return (loss, grad_x[None], *[grads[n] for n in order], *[delta[n] for n in order],
            *[new_m[n] for n in order], *[new_v[n] for n in order])
```

```python
import math

import jax
import jax.numpy as jnp
import numpy as np
from jax import lax
from jax.experimental import pallas as pl
from jax.experimental.pallas import tpu as pltpu

F32 = jnp.float32
BF16 = jnp.bfloat16
MXU_DTYPE = BF16

HEAD_DIM = 64
A_Q_HEADS, A_KV_HEADS, A_WINDOW = 16, 2, 128
A_GROUP = A_Q_HEADS // A_KV_HEADS
B_HEADS = 8
B_PATTERNS = ((128, 1), (512, 4), (2048, 16))
C_HEADS = 16
D_HEADS, D_Q_RANK, D_KV_RANK, D_NOPE, D_ROPE, D_V = 16, 512, 256, 64, 32, 64
ROPE_BASE = 10000.0
LN_EPS, RMS_EPS = 1e-5, 1e-6
DEPTH = 2
ALPHA = (2 * DEPTH) ** 0.25
A_Q_W, A_KV_W, B_W = A_Q_HEADS * HEAD_DIM, A_KV_HEADS * HEAD_DIM, B_HEADS * HEAD_DIM
EVEN_IN = A_Q_W + 2 * A_KV_W + 3 * B_W * len(B_PATTERNS)
C_W = C_HEADS * HEAD_DIM
ODD_IN = 3 * C_W + D_Q_RANK + D_KV_RANK + D_ROPE
ADAM_LR, ADAM_B1, ADAM_B2, ADAM_EPS, ADAM_WD, ADAM_STEP = 0.001, 0.9, 0.999, 1e-08, 0.01, 10

N_DEV = 8
LANES = 128
BLK = 128
CAUSAL_TILE = 512
CUM_CHUNK = 256
NEG = -1e30
VMEM_LIMIT = 48 * 1024 * 1024

NN = ((1,), (0,))
NT = ((1,), (1,))
TN = ((0,), (0,))
MESH = pl.DeviceIdType.MESH
ANY = pl.BlockSpec(memory_space=pl.ANY)


def _dot(a, b, dims):
    return lax.dot_general(a, b, (dims, ((), ())), preferred_element_type=F32)


def _bdot(a, b, dims):
    return jnp.stack([_dot(a[n], b[n], dims) for n in range(a.shape[0])])


def _params(*sem):
    return pltpu.CompilerParams(dimension_semantics=tuple(sem), vmem_limit_bytes=VMEM_LIMIT)


def _pick(n, cap, mult=LANES):
    if n <= cap:
        return n
    for t in range(cap - cap % mult, 0, -mult):
        if n % t == 0:
            return t
    raise ValueError(f"no tile for {n}")


def _lane_pad(c):
    return -(-c // LANES) * LANES


def _mm(a, b, *, name, nt=False, b_blocks=False, out_blocks=False, out_dtypes=(F32,), epilogue=None, extra=None):
    m, k = a.shape
    if b_blocks:
        nb, kin, c = b.shape
        n = kin if nt else nb * c
        k_full = nb * c if nt else kin
    else:
        n, k_full = (b.shape if nt else b.shape[::-1])
    assert k == k_full, (a.shape, b.shape, nt, b_blocks)
    tm = _pick(m, 1024, 8)
    if b_blocks and not nt:
        tn, tk = c, _pick(k, 3072)
    elif b_blocks:
        tn, tk = _pick(n, 512), c
    elif out_blocks:
        tn, tk = n // N_DEV, _pick(k, 3072)
    else:
        tn, tk = _pick(n, 512), _pick(k, 3072)
    nk = k // tk
    n_out = len(out_dtypes)

    def body(*refs):
        a_ref, b_ref = refs[0], refs[1]
        e_ref = refs[2] if extra is not None else None
        first_out = 2 + (extra is not None)
        out_refs = refs[first_out:first_out + n_out]

        def finish(acc):
            e = None if e_ref is None else e_ref[...]
            outs = (acc,) if epilogue is None else epilogue(acc, e)
            for r, o in zip(out_refs, outs):
                r[...] = o.astype(r.dtype).reshape(r.shape)

        bt = b_ref[0] if b_blocks else b_ref[...]
        part = _dot(a_ref[...], bt, NT if nt else NN)
        if nk == 1:
            finish(part)
        else:
            acc_ref = refs[first_out + n_out]
            kk = pl.program_id(2)

            @pl.when(kk == 0)
            def _():
                acc_ref[...] = part

            @pl.when(kk > 0)
            def _():
                acc_ref[...] += part

            @pl.when(kk == nk - 1)
            def _():
                finish(acc_ref[...])

    if b_blocks and not nt:
        b_spec = pl.BlockSpec((1, tk, tn), lambda i, j, kk: (j, kk, 0))
    elif b_blocks:
        b_spec = pl.BlockSpec((1, tn, tk), lambda i, j, kk: (kk, j, 0))
    elif nt:
        b_spec = pl.BlockSpec((tn, tk), lambda i, j, kk: (j, kk))
    else:
        b_spec = pl.BlockSpec((tk, tn), lambda i, j, kk: (kk, j))
    in_specs = [pl.BlockSpec((tm, tk), lambda i, j, kk: (i, kk)), b_spec]
    ins = [a.astype(MXU_DTYPE), b.astype(MXU_DTYPE)]
    if extra is not None:
        in_specs.append(pl.BlockSpec((tm, tn), lambda i, j, kk: (i, j)))
        ins.append(extra)
    if out_blocks:
        out_shape = tuple(jax.ShapeDtypeStruct((N_DEV, m, tn), d) for d in out_dtypes)
        out_specs = tuple(pl.BlockSpec((1, tm, tn), lambda i, j, kk: (j, i, 0)) for _ in out_dtypes)
    else:
        out_shape = tuple(jax.ShapeDtypeStruct((m, n), d) for d in out_dtypes)
        out_specs = tuple(pl.BlockSpec((tm, tn), lambda i, j, kk: (i, j)) for _ in out_dtypes)
    outs = pl.pallas_call(
        body,
        out_shape=out_shape,
        grid=(m // tm, n // tn, nk),
        in_specs=in_specs,
        out_specs=out_specs,
        scratch_shapes=[pltpu.VMEM((tm, tn), F32)] if nk > 1 else [],
        compiler_params=_params("parallel", "parallel", "arbitrary"),
        name=name,
    )(*ins)
    return outs[0] if n_out == 1 else outs


def _relu_sq(acc, _):
    act = jnp.maximum(acc, 0.0)
    return act, act * act


def _relu_sq_grad(acc, act):
    return (acc * (2.0 * act.astype(F32)),)


def _add_alpha(acc, du):
    return (acc + ALPHA * du,)


def _ln_fwd(x, mixed, g, b, *, name):
    s, d = x.shape
    tr = _pick(s, 256, 8)

    def body(x_ref, m_ref, g_ref, b_ref, y_ref, yb_ref, xh_ref, r_ref):
        u = ALPHA * x_ref[...] + m_ref[...]
        mu = jnp.mean(u, axis=-1, keepdims=True)
        xc = u - mu
        var = jnp.mean(xc * xc, axis=-1, keepdims=True)
        r = lax.rsqrt(var + LN_EPS)
        xh = xc * r
        y = xh * g_ref[...] + b_ref[...]
        y_ref[...] = y
        yb_ref[...] = y.astype(MXU_DTYPE)
        xh_ref[...] = xh
        r_ref[...] = r

    row = pl.BlockSpec((tr, d), lambda i: (i, 0))
    vec = pl.BlockSpec((1, d), lambda i: (0, 0))
    return pl.pallas_call(
        body,
        out_shape=(jax.ShapeDtypeStruct((s, d), F32), jax.ShapeDtypeStruct((s, d), MXU_DTYPE),
                   jax.ShapeDtypeStruct((s, d), F32), jax.ShapeDtypeStruct((s, 1), F32)),
        grid=(s // tr,),
        in_specs=[row, row, vec, vec],
        out_specs=(row, row, row, pl.BlockSpec((tr, 1), lambda i: (i, 0))),
        compiler_params=_params("parallel"),
        name=name,
    )(x, mixed, g.reshape(1, d), b.reshape(1, d))


def _ln_bwd(dy, xh, r, g, *, name):
    s, d = dy.shape
    tr = _pick(s, 256, 8)

    def body(dy_ref, xh_ref, r_ref, g_ref, du_ref, dub_ref, dg_ref, db_ref):
        dyv, xhv = dy_ref[...], xh_ref[...]
        dxh = dyv * g_ref[...]
        c1 = jnp.mean(dxh, axis=-1, keepdims=True)
        c2 = jnp.mean(dxh * xhv, axis=-1, keepdims=True)
        du = r_ref[...] * (dxh - c1 - xhv * c2)
        du_ref[...] = du
        dub_ref[...] = du.astype(MXU_DTYPE)

        @pl.when(pl.program_id(0) == 0)
        def _():
            dg_ref[...] = jnp.zeros_like(dg_ref)
            db_ref[...] = jnp.zeros_like(db_ref)

        dg_ref[...] += jnp.sum(dyv * xhv, axis=0, keepdims=True)
        db_ref[...] += jnp.sum(dyv, axis=0, keepdims=True)

    row = pl.BlockSpec((tr, d), lambda i: (i, 0))
    vec = pl.BlockSpec((1, d), lambda i: (0, 0))
    return pl.pallas_call(
        body,
        out_shape=(jax.ShapeDtypeStruct((s, d), F32), jax.ShapeDtypeStruct((s, d), MXU_DTYPE),
                   jax.ShapeDtypeStruct((1, d), F32), jax.ShapeDtypeStruct((1, d), F32)),
        grid=(s // tr,),
        in_specs=[row, row, pl.BlockSpec((tr, 1), lambda i: (i, 0)), vec],
        out_specs=(row, row, vec, vec),
        compiler_params=_params("arbitrary"),
        name=name,
    )(dy, xh, r, g.reshape(1, d))


def _rms_fwd(x, g, *, name):
    s, d = x.shape
    tr = _pick(s, 512, 8)

    def body(x_ref, g_ref, y_ref, r_ref):
        xv = x_ref[...]
        r = lax.rsqrt(jnp.mean(xv * xv, axis=-1, keepdims=True) + RMS_EPS)
        y_ref[...] = (xv * r * g_ref[...]).astype(y_ref.dtype)
        r_ref[...] = r

    return pl.pallas_call(
        body,
        out_shape=(jax.ShapeDtypeStruct((s, d), MXU_DTYPE), jax.ShapeDtypeStruct((s, 1), F32)),
        grid=(s // tr,),
        in_specs=[pl.BlockSpec((tr, d), lambda i: (i, 0)), pl.BlockSpec((1, d), lambda i: (0, 0))],
        out_specs=(pl.BlockSpec((tr, d), lambda i: (i, 0)), pl.BlockSpec((tr, 1), lambda i: (i, 0))),
        compiler_params=_params("parallel"),
        name=name,
    )(x, g.reshape(1, d))


def _rms_bwd(dy, x, r, g, *, name):
    s, d = x.shape
    tr = _pick(s, 512, 8)

    def body(dy_ref, x_ref, r_ref, g_ref, dx_ref, dg_ref):
        dyv, rv = dy_ref[...], r_ref[...]
        xn = x_ref[...] * rv
        dxn = dyv * g_ref[...]
        dx_ref[...] = rv * (dxn - xn * jnp.mean(dxn * xn, axis=-1, keepdims=True))

        @pl.when(pl.program_id(0) == 0)
        def _():
            dg_ref[...] = jnp.zeros_like(dg_ref)

        dg_ref[...] += jnp.sum(dyv * xn, axis=0, keepdims=True)

    row = pl.BlockSpec((tr, d), lambda i: (i, 0))
    vec = pl.BlockSpec((1, d), lambda i: (0, 0))
    return pl.pallas_call(
        body,
        out_shape=(jax.ShapeDtypeStruct((s, d), F32), jax.ShapeDtypeStruct((1, d), F32)),
        grid=(s // tr,),
        in_specs=[row, row, pl.BlockSpec((tr, 1), lambda i: (i, 0)), vec],
        out_specs=(row, vec),
        compiler_params=_params("arbitrary"),
        name=name,
    )(dy, x, r, g.reshape(1, d))


def _rope_tables(s, inverse):
    inv_freq = ROPE_BASE ** (-jnp.arange(0, D_ROPE, 2, dtype=F32) / D_ROPE)
    ang = jnp.arange(s, dtype=F32)[:, None] * inv_freq[None, :]
    cos, sin = jnp.cos(ang), jnp.sin(ang)
    if inverse:
        sin = -sin
    half = D_ROPE // 2
    one, zero = jnp.ones((s, D_NOPE), F32), jnp.zeros((s, D_NOPE), F32)
    pad1, pad0 = jnp.ones((s, LANES - D_NOPE - D_ROPE), F32), jnp.zeros((s, LANES - D_NOPE - D_ROPE), F32)
    zh = jnp.zeros((s, half), F32)
    c = jnp.concatenate([one, cos, cos, pad1], axis=1)
    s_lo = jnp.concatenate([zero, -sin, zh, pad0], axis=1)
    s_hi = jnp.concatenate([zero, zh, sin, pad0], axis=1)
    return c, s_lo, s_hi


def _rope(x, tables, *, out_dtype, head_sum=False, name):
    h, s, w = x.shape
    ts = _pick(s, 512, 8)
    half = D_ROPE // 2

    def body(x_ref, c_ref, lo_ref, hi_ref, y_ref, *sum_ref):
        xv = x_ref[0]
        y = (xv * c_ref[...] + pltpu.roll(xv, w - half, 1) * lo_ref[...]
             + pltpu.roll(xv, half, 1) * hi_ref[...])
        y_ref[0] = y.astype(y_ref.dtype)
        if head_sum:
            @pl.when(pl.program_id(1) == 0)
            def _():
                sum_ref[0][...] = jnp.zeros_like(sum_ref[0])

            sum_ref[0][...] += y

    tab = pl.BlockSpec((ts, w), lambda i, hh: (i, 0))
    blk = pl.BlockSpec((1, ts, w), lambda i, hh: (hh, i, 0))
    out_shape = [jax.ShapeDtypeStruct((h, s, w), out_dtype)]
    out_specs = [blk]
    if head_sum:
        out_shape.append(jax.ShapeDtypeStruct((s, w), F32))
        out_specs.append(tab)
    res = pl.pallas_call(
        body,
        out_shape=tuple(out_shape),
        grid=(s // ts, h),
        in_specs=[blk, tab, tab, tab],
        out_specs=tuple(out_specs),
        compiler_params=_params("parallel", "arbitrary"),
        name=name,
    )(x, *tables)
    return res if head_sum else res[0]


def _band_scores(q, kw, slope, i, *, scale, n_back, bps):
    b, r, _ = q.shape
    sc = _bdot(q, kw, NT) * scale
    shape = (b, r, 2 * BLK)
    row = lax.broadcasted_iota(jnp.int32, shape, 1) & (BLK - 1)
    col = lax.broadcasted_iota(jnp.int32, shape, 2)
    rel = BLK + row - col
    first_col = jnp.where(i % bps == 0, BLK, 0)
    valid = (rel >= 0) & (rel <= n_back) & (col >= first_col)
    return jnp.where(valid, sc - slope * rel.astype(F32), NEG)


def _band_fwd(q, k, v, slope, sink, *, scale, n_back, bps, name):
    g, b, rows, dh = q.shape
    r = slope.shape[2]
    nq = rows // r
    skv = k.shape[2]
    use_sink = sink is not None

    def body(*refs):
        q_ref, k_ref, v_ref, slope_ref = refs[:4]
        sink_ref = refs[4] if use_sink else None
        o_ref, lse_ref = refs[4 + use_sink:]
        i = pl.program_id(1)
        off = pl.multiple_of(i * BLK, BLK)
        kw = k_ref[0, :, pl.ds(off, 2 * BLK), :]
        vw = v_ref[0, :, pl.ds(off, 2 * BLK), :]
        sc = _band_scores(q_ref[0], kw, slope_ref[0], i, scale=scale, n_back=n_back, bps=bps)
        m = jnp.max(sc, axis=-1, keepdims=True)
        if use_sink:
            m = jnp.maximum(m, sink_ref[0])
        p = jnp.exp(sc - m)
        l = jnp.sum(p, axis=-1, keepdims=True)
        if use_sink:
            l = l + jnp.exp(sink_ref[0] - m)
        o_ref[0] = _bdot(p.astype(MXU_DTYPE), vw, NN) / l
        lse_ref[0] = m + jnp.log(l)

    qspec = pl.BlockSpec((1, b, r, dh), lambda gg, i: (gg, 0, i, 0))
    kspec = pl.BlockSpec((1, b, skv, dh), lambda gg, i: (gg, 0, 0, 0))
    rspec = pl.BlockSpec((1, b, r, 1), lambda gg, i: (gg, 0, 0, 0))
    ins = [q, k, v, slope] + ([sink] if use_sink else [])
    return pl.pallas_call(
        body,
        out_shape=(jax.ShapeDtypeStruct((g, b, rows, dh), F32), jax.ShapeDtypeStruct((g, b, rows, 1), F32)),
        grid=(g, nq),
        in_specs=[qspec, kspec, kspec, rspec] + ([rspec] if use_sink else []),
        out_specs=(qspec, pl.BlockSpec((1, b, r, 1), lambda gg, i: (gg, 0, i, 0))),
        compiler_params=_params("parallel", "arbitrary"),
        name=name,
    )(*ins)


def _band_bwd(q, k, v, do, o, lse, slope, sink, *, scale, n_back, bps, name):
    g, b, rows, dh = q.shape
    r = slope.shape[2]
    nq = rows // r
    skv = k.shape[2]
    use_sink = sink is not None
    stacked = r // BLK

    def body(*refs):
        q_ref, k_ref, v_ref, do_ref, o_ref, lse_ref, slope_ref = refs[:7]
        sink_ref = refs[7] if use_sink else None
        dq_ref, dk_ref, dv_ref = refs[7 + use_sink:10 + use_sink]
        i = pl.program_id(1)

        @pl.when(i == 0)
        def _():
            dk_ref[...] = jnp.zeros_like(dk_ref)
            dv_ref[...] = jnp.zeros_like(dv_ref)

        off = pl.multiple_of(i * BLK, BLK)
        qb = q_ref[0]
        kw = k_ref[0, :, pl.ds(off, 2 * BLK), :]
        vw = v_ref[0, :, pl.ds(off, 2 * BLK), :]
        dof = do_ref[0]
        dob = dof.astype(MXU_DTYPE)
        lse_b = lse_ref[0]
        delta = jnp.sum(dof * o_ref[0], axis=-1, keepdims=True)
        sc = _band_scores(qb, kw, slope_ref[0], i, scale=scale, n_back=n_back, bps=bps)
        p = jnp.exp(sc - lse_b)
        ds = (p * (_bdot(dob, vw, NT) - delta) * scale).astype(MXU_DTYPE)
        dq_ref[0] = _bdot(ds, kw, NN)
        dk_ref[0, :, pl.ds(off, 2 * BLK), :] += _bdot(ds, qb, TN)
        dv_ref[0, :, pl.ds(off, 2 * BLK), :] += _bdot(p.astype(MXU_DTYPE), dob, TN)

        if use_sink:
            dsink_ref = refs[10 + use_sink]

            @pl.when(i == 0)
            def _():
                dsink_ref[...] = jnp.zeros_like(dsink_ref)

            contrib = -jnp.exp(sink_ref[0] - lse_b) * delta
            for n in range(stacked):
                part = jnp.sum(contrib[0, n * BLK:(n + 1) * BLK, :], axis=0, keepdims=True)
                dsink_ref[0, n:n + 1, :] += jnp.broadcast_to(part, (1, LANES))

    def qspec(w):
        return pl.BlockSpec((1, b, r, w), lambda gg, i: (gg, 0, i, 0))

    kspec = pl.BlockSpec((1, b, skv, dh), lambda gg, i: (gg, 0, 0, 0))
    rspec = pl.BlockSpec((1, b, r, 1), lambda gg, i: (gg, 0, 0, 0))
    ins = [q, k, v, do, o, lse, slope] + ([sink] if use_sink else [])
    in_specs = [qspec(dh), kspec, kspec, qspec(dh), qspec(dh), qspec(1), rspec] + ([rspec] if use_sink else [])
    out_shape = [jax.ShapeDtypeStruct((g, b, rows, dh), F32), jax.ShapeDtypeStruct((g, b, skv, dh), F32),
                 jax.ShapeDtypeStruct((g, b, skv, dh), F32)]
    out_specs = [qspec(dh), kspec, kspec]
    if use_sink:
        assert b == 1
        out_shape.append(jax.ShapeDtypeStruct((g, stacked, LANES), F32))
        out_specs.append(pl.BlockSpec((1, stacked, LANES), lambda gg, i: (gg, 0, 0)))
    return pl.pallas_call(
        body,
        out_shape=tuple(out_shape),
        grid=(g, nq),
        in_specs=in_specs,
        out_specs=tuple(out_specs),
        compiler_params=_params("parallel", "arbitrary"),
        name=name,
    )(*ins)


def _merge(outs, lses, *, name):
    h, s, dv = outs[0].shape
    ts = _pick(s, 512, 8)

    def body(o0, o1, o2, l0, l1, l2, ob_ref, lt_ref):
        a, b, c = l0[0], l1[0], l2[0]
        m = jnp.maximum(jnp.maximum(a, b), c)
        ea, eb, ec = jnp.exp(a - m), jnp.exp(b - m), jnp.exp(c - m)
        den = ea + eb + ec
        ob_ref[0] = (ea / den) * o0[0] + (eb / den) * o1[0] + (ec / den) * o2[0]
        lt_ref[0] = m + jnp.log(den)

    ospec = pl.BlockSpec((1, ts, dv), lambda hh, i: (hh, i, 0))
    lspec = pl.BlockSpec((1, ts, 1), lambda hh, i: (hh, i, 0))
    return pl.pallas_call(
        body,
        out_shape=(jax.ShapeDtypeStruct((h, s, dv), F32), jax.ShapeDtypeStruct((h, s, 1), F32)),
        grid=(h, s // ts),
        in_specs=[ospec] * 3 + [lspec] * 3,
        out_specs=(ospec, lspec),
        compiler_params=_params("parallel", "parallel"),
        name=name,
    )(*outs, *lses)


def _tile_iotas(t):
    return lax.broadcasted_iota(jnp.int32, (t, t), 0), lax.broadcasted_iota(jnp.int32, (t, t), 1)


def _causal_fwd(q, k, v, *, scale, name):
    h, s, dq = q.shape
    dv = v.shape[-1]
    t = min(CAUSAL_TILE, s)

    def body(q_ref, k_ref, v_ref, o_ref, lse_ref):
        i = pl.program_id(1)
        qb = q_ref[0]

        def tile(j, carry, diagonal):
            m, l, acc = carry
            off = pl.multiple_of(j * t, t)
            sc = _dot(qb, k_ref[0, pl.ds(off, t), :], NT) * scale
            if diagonal:
                row, col = _tile_iotas(t)
                sc = jnp.where(row >= col, sc, NEG)
            m_new = jnp.maximum(m, jnp.max(sc, axis=-1, keepdims=True))
            a = jnp.exp(m - m_new)
            p = jnp.exp(sc - m_new)
            l = a * l + jnp.sum(p, axis=-1, keepdims=True)
            acc = a * acc + _dot(p.astype(MXU_DTYPE), v_ref[0, pl.ds(off, t), :], NN)
            return m_new, l, acc

        init = (jnp.full((t, 1), NEG, F32), jnp.zeros((t, 1), F32), jnp.zeros((t, dv), F32))
        carry = lax.fori_loop(0, i, lambda j, c: tile(j, c, False), init)
        m, l, acc = tile(i, carry, True)
        o_ref[0] = acc / l
        lse_ref[0] = m + jnp.log(l)

    def qspec(w):
        return pl.BlockSpec((1, t, w), lambda hh, i: (hh, i, 0))

    def kspec(w):
        return pl.BlockSpec((1, s, w), lambda hh, i: (hh, 0, 0))

    return pl.pallas_call(
        body,
        out_shape=(jax.ShapeDtypeStruct((h, s, dv), F32), jax.ShapeDtypeStruct((h, s, 1), F32)),
        grid=(h, s // t),
        in_specs=[qspec(dq), kspec(dq), kspec(dv)],
        out_specs=(qspec(dv), qspec(1)),
        compiler_params=_params("parallel", "arbitrary"),
        name=name,
    )(q, k, v)


def _causal_bwd(q, k, v, do, o, lse, *, scale, name):
    h, s, dq = q.shape
    dv = v.shape[-1]
    t = min(CAUSAL_TILE, s)

    def body(q_ref, k_ref, v_ref, do_ref, o_ref, lse_ref, dq_ref, dk_ref, dv_ref):
        i = pl.program_id(1)

        @pl.when(i == 0)
        def _():
            dk_ref[...] = jnp.zeros_like(dk_ref)
            dv_ref[...] = jnp.zeros_like(dv_ref)

        qb = q_ref[0]
        dof = do_ref[0]
        dob = dof.astype(MXU_DTYPE)
        lse_b = lse_ref[0]
        delta = jnp.sum(dof * o_ref[0], axis=-1, keepdims=True)

        def tile(j, dq_acc, diagonal):
            off = pl.multiple_of(j * t, t)
            kb = k_ref[0, pl.ds(off, t), :]
            vb = v_ref[0, pl.ds(off, t), :]
            sc = _dot(qb, kb, NT) * scale
            if diagonal:
                row, col = _tile_iotas(t)
                sc = jnp.where(row >= col, sc, NEG)
            p = jnp.exp(sc - lse_b)
            ds = (p * (_dot(dob, vb, NT) - delta) * scale).astype(MXU_DTYPE)
            dk_ref[0, pl.ds(off, t), :] += _dot(ds, qb, TN)
            dv_ref[0, pl.ds(off, t), :] += _dot(p.astype(MXU_DTYPE), dob, TN)
            return dq_acc + _dot(ds, kb, NN)

        dq_acc = lax.fori_loop(0, i, lambda j, c: tile(j, c, False), jnp.zeros((t, dq), F32))
        dq_ref[0] = tile(i, dq_acc, True)

    def qspec(w):
        return pl.BlockSpec((1, t, w), lambda hh, i: (hh, i, 0))

    def kspec(w):
        return pl.BlockSpec((1, s, w), lambda hh, i: (hh, 0, 0))

    return pl.pallas_call(
        body,
        out_shape=(jax.ShapeDtypeStruct((h, s, dq), F32), jax.ShapeDtypeStruct((h, s, dq), F32),
                   jax.ShapeDtypeStruct((h, s, dv), F32)),
        grid=(h, s // t),
        in_specs=[qspec(dq), kspec(dq), kspec(dv), qspec(dv), qspec(dv), qspec(1)],
        out_specs=(qspec(dq), kspec(dq), kspec(dv)),
        compiler_params=_params("parallel", "arbitrary"),
        name=name,
    )(q, k, v, do, o, lse)


def _split_cumsum(x, tri):
    hi = x.astype(BF16)
    lo = (x - hi.astype(F32)).astype(BF16)
    return _dot(hi, tri, NN) + _dot(lo, tri, NN)


def _chunked_cumsum(x, tri, run, *, reverse, sign):
    c = tri.shape[0]
    n = x.shape[1] // c
    parts = [None] * n
    for idx in (reversed(range(n)) if reverse else range(n)):
        xc = x[:, idx * c:(idx + 1) * c]
        parts[idx] = sign * (run + _split_cumsum(xc, tri))
        run = run + jnp.sum(xc, axis=-1, keepdims=True)
    return (parts[0] if n == 1 else jnp.concatenate(parts, axis=1)), run


def _sb_logs(z):
    e = jnp.exp(-jnp.abs(z))
    l1 = jnp.log(1.0 + e)
    return e, jnp.minimum(z, 0.0) - l1, -jnp.maximum(z, 0.0) - l1


def _sb_fwd(q, k, v, *, scale, name):
    h, s, dh = q.shape
    t = min(CAUSAL_TILE, s)
    cc = min(CUM_CHUNK, t)

    def body(q_ref, k_ref, v_ref, o_ref, t_ref):
        i = pl.program_id(1)
        qb = q_ref[0]
        crow, ccol = _tile_iotas(cc)
        after = (crow > ccol).astype(BF16)

        def tile(j, carry, diagonal):
            run, acc = carry
            off = pl.multiple_of(j * t, t)
            z = _dot(qb, k_ref[0, pl.ds(off, t), :], NT) * scale
            _, log_beta, log_keep = _sb_logs(z)
            if diagonal:
                row, col = _tile_iotas(t)
                strict = row > col
                log_keep = jnp.where(strict, log_keep, 0.0)
            a, run = _chunked_cumsum(log_keep, after, run, reverse=True, sign=1.0)
            w = jnp.exp(log_beta + a)
            if diagonal:
                w = jnp.where(strict, w, 0.0)
            return run, acc + _dot(w.astype(MXU_DTYPE), v_ref[0, pl.ds(off, t), :], NN)

        carry = tile(i, (jnp.zeros((t, 1), F32), jnp.zeros((t, dh), F32)), True)
        run, acc = lax.fori_loop(0, i, lambda jj, c: tile(i - 1 - jj, c, False), carry)
        o_ref[0] = acc
        t_ref[0] = run

    qspec = pl.BlockSpec((1, t, dh), lambda hh, i: (hh, i, 0))
    kspec = pl.BlockSpec((1, s, dh), lambda hh, i: (hh, 0, 0))
    return pl.pallas_call(
        body,
        out_shape=(jax.ShapeDtypeStruct((h, s, dh), F32), jax.ShapeDtypeStruct((h, s, 1), F32)),
        grid=(h, s // t),
        in_specs=[qspec, kspec, kspec],
        out_specs=(qspec, pl.BlockSpec((1, t, 1), lambda hh, i: (hh, i, 0))),
        compiler_params=_params("parallel", "arbitrary"),
        name=name,
    )(q, k, v)


def _sb_bwd(q, k, v, do, total, *, scale, name):
    h, s, dh = q.shape
    t = min(CAUSAL_TILE, s)
    cc = min(CUM_CHUNK, t)

    def body(q_ref, k_ref, v_ref, do_ref, t_ref, dq_ref, dk_ref, dv_ref):
        i = pl.program_id(1)

        @pl.when(i == 0)
        def _():
            dk_ref[...] = jnp.zeros_like(dk_ref)
            dv_ref[...] = jnp.zeros_like(dv_ref)

        qb = q_ref[0]
        dob = do_ref[0].astype(MXU_DTYPE)
        tot = t_ref[0]
        crow, ccol = _tile_iotas(cc)
        upto = (crow <= ccol).astype(BF16)
        before = (crow < ccol).astype(BF16)

        def tile(j, carry, diagonal):
            run_keep, run_g, dq_acc = carry
            off = pl.multiple_of(j * t, t)
            kb = k_ref[0, pl.ds(off, t), :]
            vb = v_ref[0, pl.ds(off, t), :]
            z = _dot(qb, kb, NT) * scale
            e, log_beta, log_keep = _sb_logs(z)
            if diagonal:
                row, col = _tile_iotas(t)
                strict = row > col
                log_keep = jnp.where(strict, log_keep, 0.0)
            a, run_keep = _chunked_cumsum(log_keep, upto, run_keep - tot, reverse=False, sign=-1.0)
            run_keep = run_keep + tot
            w = jnp.exp(log_beta + a)
            if diagonal:
                w = jnp.where(strict, w, 0.0)
            g = w * _dot(dob, vb, NT)
            prefix, run_g = _chunked_cumsum(g, before, run_g, reverse=False, sign=1.0)
            pos = z >= 0.0
            dz = (g * jnp.where(pos, e, 1.0) - jnp.where(pos, 1.0, e) * prefix) / (1.0 + e)
            if diagonal:
                dz = jnp.where(strict, dz, 0.0)
            dz = (dz * scale).astype(MXU_DTYPE)
            dk_ref[0, pl.ds(off, t), :] += _dot(dz, qb, TN)
            dv_ref[0, pl.ds(off, t), :] += _dot(w.astype(MXU_DTYPE), dob, TN)
            return run_keep, run_g, dq_acc + _dot(dz, kb, NN)

        zero = jnp.zeros((t, 1), F32)
        carry = lax.fori_loop(0, i, lambda j, c: tile(j, c, False), (zero, zero, jnp.zeros((t, dh), F32)))
        dq_ref[0] = tile(i, carry, True)[2]

    qspec = pl.BlockSpec((1, t, dh), lambda hh, i: (hh, i, 0))
    kspec = pl.BlockSpec((1, s, dh), lambda hh, i: (hh, 0, 0))
    shp = jax.ShapeDtypeStruct((h, s, dh), F32)
    return pl.pallas_call(
        body,
        out_shape=(shp, shp, shp),
        grid=(h, s // t),
        in_specs=[qspec, kspec, kspec, qspec, pl.BlockSpec((1, t, 1), lambda hh, i: (hh, i, 0))],
        out_specs=(qspec, kspec, kspec),
        compiler_params=_params("parallel", "arbitrary"),
        name=name,
    )(q, k, v, do, total)


def _loss_head(y, target, *, name):
    s, d = y.shape
    tr = _pick(s, 256, 8)

    def body(y_ref, t_ref, dy_ref, loss_ref):
        err = y_ref[...] - t_ref[...]
        dy_ref[...] = err * (1.0 / d)

        @pl.when(pl.program_id(0) == 0)
        def _():
            loss_ref[...] = jnp.zeros_like(loss_ref)

        per_tok = jnp.mean(err * err, axis=-1, keepdims=True)
        loss_ref[...] += 0.5 * jnp.sum(per_tok, axis=0, keepdims=True)

    row = pl.BlockSpec((tr, d), lambda i: (i, 0))
    return pl.pallas_call(
        body,
        out_shape=(jax.ShapeDtypeStruct((s, d), F32), jax.ShapeDtypeStruct((1, LANES), F32)),
        grid=(s // tr,),
        in_specs=[row, row],
        out_specs=(row, pl.BlockSpec((1, LANES), lambda i: (0, 0))),
        compiler_params=_params("arbitrary"),
        name=name,
    )(y, target)


def _adamw(w, grads, m, v, *, name):
    nl, r, c = w.shape
    cp = grads[0].shape[1]
    tr = _pick(r, 256, 8)

    def body(*refs):
        w_ref, m_ref, v_ref = refs[:3]
        g_refs = refs[3:3 + nl]
        g_out, d_ref, m2_ref, v2_ref = refs[3 + nl:]
        layer = pl.program_id(0)
        gv = g_refs[0][:, :c]
        for n in range(1, nl):
            gv = jnp.where(layer == n, g_refs[n][:, :c], gv)
        m2 = ADAM_B1 * m_ref[0] + (1.0 - ADAM_B1) * gv
        v2 = ADAM_B2 * v_ref[0] + (1.0 - ADAM_B2) * (gv * gv)
        m_hat = m2 / (1.0 - ADAM_B1 ** ADAM_STEP)
        v_hat = v2 / (1.0 - ADAM_B2 ** ADAM_STEP)
        g_out[0] = gv
        d_ref[0] = -ADAM_LR * (m_hat / (jnp.sqrt(v_hat) + ADAM_EPS) + ADAM_WD * w_ref[0])
        m2_ref[0] = m2
        v2_ref[0] = v2

    blk = pl.BlockSpec((1, tr, c), lambda l, i: (l, i, 0))
    gspec = pl.BlockSpec((tr, cp), lambda l, i: (i, 0))
    shp = jax.ShapeDtypeStruct((nl, r, c), F32)
    return pl.pallas_call(
        body,
        out_shape=(shp, shp, shp, shp),
        grid=(nl, r // tr),
        in_specs=[blk, blk, blk] + [gspec] * nl,
        out_specs=(blk, blk, blk, blk),
        compiler_params=_params("parallel", "parallel"),
        name=name,
    )(w, m, v, *grads)


def _pair_sum(mine, recv, my_c, *, name):
    _, r, c = mine.shape
    tr = _pick(r, 512, 16)

    def body(c_ref, a_ref, b_ref, o_ref):
        o_ref[0] = (a_ref[0].astype(F32) + b_ref[0].astype(F32)).astype(o_ref.dtype)

    grid_spec = pltpu.PrefetchScalarGridSpec(
        num_scalar_prefetch=1,
        grid=(4, r // tr),
        in_specs=[pl.BlockSpec((1, tr, c), lambda kk, i, c_ref: (2 * kk + c_ref[0], i, 0)),
                  pl.BlockSpec((1, tr, c), lambda kk, i, c_ref: (kk, i, 0))],
        out_specs=pl.BlockSpec((1, tr, c), lambda kk, i, c_ref: (kk, i, 0)),
    )
    return pl.pallas_call(
        body,
        out_shape=jax.ShapeDtypeStruct((4, r, c), mine.dtype),
        grid_spec=grid_spec,
        compiler_params=_params("parallel", "parallel"),
        name=name,
    )(my_c.reshape(1).astype(jnp.int32), mine, recv)


def _final_sum(partial, recv, my_chip, *, name):
    _, r, c = partial.shape
    tr = _pick(r, 512, 16)

    def body(chip_ref, p_ref, r0, r1, r2, o_ref):
        o_ref[...] = ((p_ref[0].astype(F32) + r0[0].astype(F32)) + r1[0].astype(F32)) + r2[0].astype(F32)

    def slot(n):
        return pl.BlockSpec((1, tr, c), lambda i, chip_ref: (n, i, 0))

    grid_spec = pltpu.PrefetchScalarGridSpec(
        num_scalar_prefetch=1,
        grid=(r // tr,),
        in_specs=[pl.BlockSpec((1, tr, c), lambda i, chip_ref: (chip_ref[0], i, 0)), slot(0), slot(1), slot(2)],
        out_specs=pl.BlockSpec((tr, c), lambda i, chip_ref: (i, 0)),
    )
    return pl.pallas_call(
        body,
        out_shape=jax.ShapeDtypeStruct((r, c), F32),
        grid_spec=grid_spec,
        compiler_params=_params("parallel"),
        name=name,
    )(my_chip.reshape(1).astype(jnp.int32), partial, recv, recv, recv)


def _sum_devices(stack, *, name):
    n, r, c = stack.shape

    def body(s_ref, o_ref):
        acc = s_ref[0]
        for dev in range(1, n):
            acc = acc + s_ref[dev]
        o_ref[...] = acc

    return pl.pallas_call(
        body,
        out_shape=jax.ShapeDtypeStruct((r, c), F32),
        in_specs=[pl.BlockSpec(memory_space=pltpu.VMEM)],
        out_specs=pl.BlockSpec(memory_space=pltpu.VMEM),
        name=name,
    )(stack)


def _mesh_pos():
    return lax.axis_index("x"), lax.axis_index("y"), lax.axis_index("c")


def _all_gather(shards, *, name):
    n = len(shards)

    def body(*refs):
        x_refs, out_refs = refs[:n], refs[n:2 * n]
        send_sems, recv_sems, local_sems = refs[2 * n:]
        x, y, cc = _mesh_pos()
        me, sibling = (x, y, cc), (x, y, 1 - cc)
        chips = [(1 - x, y), (x, 1 - y), (1 - x, 1 - y)]

        def rows(a, px, py, pc):
            return out_refs[a].at[4 * px + 2 * py + pc]

        def copy(a, kk, block, to, src=None):
            return pltpu.make_async_remote_copy(
                src_ref=rows(a, *block) if src is None else src, dst_ref=rows(a, *block),
                send_sem=send_sems.at[7 * a + kk], recv_sem=recv_sems.at[7 * a + kk],
                device_id=to, device_id_type=MESH)

        sends, own = [], []
        for a in range(n):
            own.append(pltpu.make_async_copy(x_refs[a], rows(a, *me), local_sems.at[a]))
            own[a].start()
            first = [copy(a, 0, me, sibling, src=x_refs[a])]
            first += [copy(a, 1 + j, me, (*chip, cc), src=x_refs[a]) for j, chip in enumerate(chips)]
            for cp in first:
                cp.start()
            sends += first
        for a in range(n):
            for j, chip in enumerate(chips):
                copy(a, 1 + j, (*chip, cc), me).wait_recv()
                passed = copy(a, 4 + j, (*chip, cc), sibling)
                passed.start()
                sends.append(passed)
        for a in range(n):
            copy(a, 0, sibling, me).wait_recv()
            for j, chip in enumerate(chips):
                copy(a, 4 + j, (*chip, 1 - cc), me).wait_recv()
        for cp in sends:
            cp.wait_send()
        for cp in own:
            cp.wait()

    return pl.pallas_call(
        body,
        out_shape=tuple(jax.ShapeDtypeStruct((N_DEV,) + t.shape, t.dtype) for t in shards),
        in_specs=[ANY] * n,
        out_specs=tuple([ANY] * n),
        scratch_shapes=[pltpu.SemaphoreType.DMA((7 * n,)), pltpu.SemaphoreType.DMA((7 * n,)),
                        pltpu.SemaphoreType.DMA((n,))],
        name=name,
    )(*shards)


def _to_sibling(blocks, *, name):
    n = len(blocks)

    def body(*refs):
        g_refs, out_refs = refs[:n], refs[n:2 * n]
        send_sems, recv_sems = refs[2 * n:]
        x, y, cc = _mesh_pos()
        copies = [
            pltpu.make_async_remote_copy(
                src_ref=g_refs[a].at[2 * chip + (1 - cc)], dst_ref=out_refs[a].at[chip],
                send_sem=send_sems.at[4 * a + chip], recv_sem=recv_sems.at[4 * a + chip],
                device_id=(x, y, 1 - cc), device_id_type=MESH)
            for a in range(n) for chip in range(4)
        ]
        for cp in copies:
            cp.start()
        for cp in copies:
            cp.wait_recv()
        for cp in copies:
            cp.wait_send()

    return pl.pallas_call(
        body,
        out_shape=tuple(jax.ShapeDtypeStruct((4,) + t.shape[1:], t.dtype) for t in blocks),
        in_specs=[ANY] * n,
        out_specs=tuple([ANY] * n),
        scratch_shapes=[pltpu.SemaphoreType.DMA((4 * n,)), pltpu.SemaphoreType.DMA((4 * n,))],
        name=name,
    )(*blocks)


def _to_chips(partials, *, name):
    n = len(partials)

    def body(*refs):
        p_refs, out_refs = refs[:n], refs[n:2 * n]
        send_sems, recv_sems = refs[2 * n:]
        x, y, cc = _mesh_pos()
        chips = [(1 - x, y), (x, 1 - y), (1 - x, 1 - y)]
        copies = [
            pltpu.make_async_remote_copy(
                src_ref=p_refs[a].at[2 * px + py], dst_ref=out_refs[a].at[j],
                send_sem=send_sems.at[3 * a + j], recv_sem=recv_sems.at[3 * a + j],
                device_id=(px, py, cc), device_id_type=MESH)
            for a in range(n) for j, (px, py) in enumerate(chips)
        ]
        for cp in copies:
            cp.start()
        for cp in copies:
            cp.wait_recv()
        for cp in copies:
            cp.wait_send()

    return pl.pallas_call(
        body,
        out_shape=tuple(jax.ShapeDtypeStruct((3,) + t.shape[1:], t.dtype) for t in partials),
        in_specs=[ANY] * n,
        out_specs=tuple([ANY] * n),
        scratch_shapes=[pltpu.SemaphoreType.DMA((3 * n,)), pltpu.SemaphoreType.DMA((3 * n,))],
        name=name,
    )(*partials)


def _alibi(n):
    return 2.0 ** (-8.0 * np.arange(1, n + 1, dtype=np.float32) / n)


def _heads(t, n):
    s = t.shape[0]
    return t.reshape(s, n, t.shape[1] // n).transpose(1, 0, 2)


def _unheads(t):
    n, s, dh = t.shape
    return t.transpose(1, 0, 2).reshape(s, n * dh)


def _to_strided(t, d):
    h, s, x = t.shape
    return t.reshape(h, s // d, d, x).transpose(0, 2, 1, 3).reshape(h, s, x)


def _from_strided(t, d):
    h, s, x = t.shape
    return t.reshape(h, d, s // d, x).transpose(0, 2, 1, 3).reshape(h, s, x)


def _drop_block_pad(t, c):
    s = t.shape[0]
    return t.reshape(s, N_DEV, -1)[:, :, :c].reshape(s, N_DEV * c)


def _add_block_pad(t, cp):
    s, n = t.shape
    c = n // N_DEV
    return jnp.pad(t.reshape(s, N_DEV, c), ((0, 0), (0, 0), (0, cp - c))).reshape(s, N_DEV * cp)


def _stack_rows(t, nq):
    return t.reshape(nq, BLK, A_KV_HEADS, A_GROUP, HEAD_DIM).transpose(2, 0, 3, 1, 4).reshape(
        A_KV_HEADS, 1, nq * A_GROUP * BLK, HEAD_DIM)


def _unstack_rows(t, nq):
    return t.reshape(A_KV_HEADS, nq, A_GROUP, BLK, HEAD_DIM).transpose(1, 3, 0, 2, 4).reshape(
        nq * BLK, A_Q_W)


def _lead_block(t):
    return jnp.pad(t, ((0, 0), (BLK, 0), (0, 0)))


def _local_step(x0, target, w, sinks, gq, gkv, ln1_g, ln1_b, ln2_g, ln2_b):
    s, d = x0.shape
    scale_h = 1.0 / math.sqrt(HEAD_DIM)
    scale_d = 1.0 / math.sqrt(D_NOPE + D_ROPE)
    nq = s // BLK
    even_c, odd_c = EVEN_IN // N_DEV, ODD_IN // N_DEV
    even_cp, odd_cp = w["even_w_in"].shape[2], w["odd_w_in"].shape[2]
    bf = lambda t: t.astype(MXU_DTYPE)
    blocks = dict(b_blocks=True)

    x0b = bf(x0)
    h_e = _drop_block_pad(_mm(x0b, w["even_w_in"], out_dtypes=(MXU_DTYPE,), name="even_in_fwd", **blocks), even_c)
    qa = _stack_rows(h_e[:, :A_Q_W], nq)
    ka = _lead_block(_heads(h_e[:, A_Q_W:A_Q_W + A_KV_W], A_KV_HEADS))[:, None]
    va = _lead_block(_heads(h_e[:, A_Q_W + A_KV_W:A_Q_W + 2 * A_KV_W], A_KV_HEADS))[:, None]
    rows_a = A_GROUP * BLK
    slope_a = jnp.asarray(np.repeat(_alibi(A_Q_HEADS).reshape(A_KV_HEADS, A_GROUP), BLK, axis=1).reshape(
        A_KV_HEADS, 1, rows_a, 1))
    sink_a = jnp.broadcast_to(sinks.reshape(A_KV_HEADS, A_GROUP, 1), (A_KV_HEADS, A_GROUP, BLK)).reshape(
        A_KV_HEADS, 1, rows_a, 1)
    a_cfg = dict(scale=scale_h, n_back=A_WINDOW - 1, bps=nq)
    oa, lse_a = _band_fwd(qa, ka, va, slope_a, sink_a, name="swa_fwd", **a_cfg)
    b_in, b_cfg, b_out, b_lse = [], [], [], []
    base = A_Q_W + 2 * A_KV_W
    for gi, (window, dil) in enumerate(B_PATTERNS):
        blk = h_e[:, base + gi * 3 * B_W: base + (gi + 1) * 3 * B_W].reshape(s, 3, B_HEADS, HEAD_DIM)
        qs, ks, vs = (_to_strided(blk[:, n].transpose(1, 0, 2), dil) for n in range(3))
        slope = jnp.asarray(np.broadcast_to((_alibi(B_HEADS) * dil).reshape(1, B_HEADS, 1, 1), (1, B_HEADS, BLK, 1)))
        ins = (qs[None], _lead_block(ks)[None], _lead_block(vs)[None], slope)
        cfg = dict(scale=scale_h, n_back=window // dil, bps=nq // dil)
        o, lse = _band_fwd(*ins, None, name=f"dil{gi}_fwd", **cfg)
        b_in.append(ins)
        b_cfg.append(cfg)
        b_out.append(_from_strided(o[0], dil))
        b_lse.append(_from_strided(lse[0], dil))
    ob, lse_b = _merge(b_out, b_lse, name="dil_merge")
    y_e = bf(jnp.concatenate([_unstack_rows(oa, nq), _unheads(ob)], axis=1))
    mixed = _mm(y_e, w["even_w_out"], name="even_out_fwd")
    x0n, x0nb, xh1_0, r1_0 = _ln_fwd(x0, mixed, ln1_g[0], ln1_b[0], name="ln1_fwd_l0")
    act0, hid0 = _mm(x0nb, w["mlp_w1"][0], out_dtypes=(MXU_DTYPE, MXU_DTYPE), epilogue=_relu_sq,
                     name="mlp1_fwd_l0", **blocks)
    mlp = _mm(hid0, w["mlp_w2"][0], name="mlp2_fwd_l0")
    x1, x1b, xh2_0, r2_0 = _ln_fwd(x0n, mlp, ln2_g[0], ln2_b[0], name="ln2_fwd_l0")

    h_o = _drop_block_pad(_mm(x1b, w["odd_w_in"], name="odd_in_fwd", **blocks), odd_c)
    qc, kc, vc = (bf(_heads(h_o[:, n * C_W:(n + 1) * C_W], C_HEADS)) for n in range(3))
    oc, sb_total = _sb_fwd(qc, kc, vc, scale=scale_h, name="sb_fwd")
    o_cq, o_ckv, o_kr = 3 * C_W, 3 * C_W + D_Q_RANK, 3 * C_W + D_Q_RANK + D_KV_RANK
    cq, ckv, kr = h_o[:, o_cq:o_ckv], h_o[:, o_ckv:o_kr], h_o[:, o_kr:o_kr + D_ROPE]
    ncq, rq = _rms_fwd(cq, gq, name="rms_q_fwd")
    nckv, rkv = _rms_fwd(ckv, gkv, name="rms_kv_fwd")
    qd = _mm(ncq, w["odd_w_uq"], name="uq_fwd")
    kvd = _mm(nckv, w["odd_w_ukv"], name="ukv_fwd").reshape(s, D_HEADS, D_NOPE + D_V)
    lane_pad = LANES - D_NOPE - D_ROPE
    q_h = jnp.pad(qd.reshape(s, D_HEADS, D_NOPE + D_ROPE), ((0, 0), (0, 0), (0, lane_pad))).transpose(1, 0, 2)
    k_h = jnp.concatenate([kvd[:, :, :D_NOPE], jnp.broadcast_to(kr[:, None, :], (s, D_HEADS, D_ROPE)),
                           jnp.zeros((s, D_HEADS, lane_pad), F32)], axis=2).transpose(1, 0, 2)
    v_h = bf(kvd[:, :, D_NOPE:].transpose(1, 0, 2))
    rope_t = _rope_tables(s, inverse=False)
    q_r = _rope(q_h, rope_t, out_dtype=MXU_DTYPE, name="rope_q_fwd")
    k_r = _rope(k_h, rope_t, out_dtype=MXU_DTYPE, name="rope_k_fwd")
    od, lse_d = _causal_fwd(q_r, k_r, v_h, scale=scale_d, name="mla_fwd")
    y_o = bf(jnp.concatenate([_unheads(oc), _unheads(od)], axis=1))
    mixed = _mm(y_o, w["odd_w_out"], name="odd_out_fwd")
    x1n, x1nb, xh1_1, r1_1 = _ln_fwd(x1, mixed, ln1_g[1], ln1_b[1], name="ln1_fwd_l1")
    act1, hid1 = _mm(x1nb, w["mlp_w1"][1], out_dtypes=(MXU_DTYPE, MXU_DTYPE), epilogue=_relu_sq,
                     name="mlp1_fwd_l1", **blocks)
    mlp = _mm(hid1, w["mlp_w2"][1], name="mlp2_fwd_l1")
    y, _, xh2_1, r2_1 = _ln_fwd(x1n, mlp, ln2_g[1], ln2_b[1], name="ln2_fwd_l1")
    dy, loss_vec = _loss_head(y, target, name="loss_head")

    def mlp_block_bwd(g_out, layer, xh2, r2, xh1, r1, act, hid, xnb):
        du2, du2b, dg2, db2 = _ln_bwd(g_out, xh2, r2, ln2_g[layer], name=f"ln2_bwd_l{layer}")
        dpre = _mm(du2b, w["mlp_w2"][layer], nt=True, out_dtypes=(MXU_DTYPE,), epilogue=_relu_sq_grad,
                   extra=act, name=f"mlp2_dx_l{layer}")
        dw2 = _mm(hid.T, du2b, out_dtypes=(BF16,), name=f"mlp2_dw_l{layer}")
        dw1 = _mm(xnb.T, dpre, out_blocks=True, out_dtypes=(BF16,), name=f"mlp1_dw_l{layer}")
        dxn = _mm(dpre, w["mlp_w1"][layer], nt=True, epilogue=_add_alpha, extra=du2, name=f"mlp1_dx_l{layer}",
                  **blocks)
        du1, du1b, dg1, db1 = _ln_bwd(dxn, xh1, r1, ln1_g[layer], name=f"ln1_bwd_l{layer}")
        return du1, du1b, dw1, dw2.reshape(N_DEV, -1, d), (dg1, db1, dg2, db2)

    du1, du1b, dw1_1, dw2_1, ln_1 = mlp_block_bwd(dy, 1, xh2_1, r2_1, xh1_1, r1_1, act1, hid1, x1nb)
    d_odd_out = _mm(y_o.T, du1b, out_dtypes=(BF16,), name="odd_out_dw").reshape(N_DEV, -1, d)
    dy_o = _mm(du1b, w["odd_w_out"], nt=True, name="odd_out_dx")
    doc, dod = _heads(dy_o[:, :C_W], C_HEADS), _heads(dy_o[:, C_W:], D_HEADS)
    dqc, dkc, dvc = _sb_bwd(qc, kc, vc, doc, sb_total, scale=scale_h, name="sb_bwd")
    dq_r, dk_r, dv_h = _causal_bwd(q_r, k_r, v_h, dod, od, lse_d, scale=scale_d, name="mla_bwd")
    rope_inv = _rope_tables(s, inverse=True)
    dq_h = _rope(dq_r, rope_inv, out_dtype=F32, name="rope_q_bwd")
    dk_h, dkr_sum = _rope(dk_r, rope_inv, out_dtype=F32, head_sum=True, name="rope_k_bwd")
    dqd = bf(dq_h[:, :, :D_NOPE + D_ROPE].transpose(1, 0, 2).reshape(s, D_HEADS * (D_NOPE + D_ROPE)))
    dkvd = bf(jnp.concatenate([dk_h[:, :, :D_NOPE], dv_h], axis=2).transpose(1, 0, 2).reshape(s, D_HEADS * (D_NOPE + D_V)))
    d_uq = _mm(ncq.T, dqd, out_dtypes=(BF16,), name="uq_dw")
    dncq = _mm(dqd, w["odd_w_uq"], nt=True, name="uq_dx")
    d_ukv = _mm(nckv.T, dkvd, out_dtypes=(BF16,), name="ukv_dw")
    dnckv = _mm(dkvd, w["odd_w_ukv"], nt=True, name="ukv_dx")
    dcq, dgq = _rms_bwd(dncq, cq, rq, gq, name="rms_q_bwd")
    dckv, dgkv = _rms_bwd(dnckv, ckv, rkv, gkv, name="rms_kv_bwd")
    dh_o = _add_block_pad(bf(jnp.concatenate(
        [_unheads(dqc), _unheads(dkc), _unheads(dvc), dcq, dckv, dkr_sum[:, D_NOPE:D_NOPE + D_ROPE]], axis=1)), odd_cp)
    d_odd_in = _mm(x1b.T, dh_o, out_blocks=True, out_dtypes=(BF16,), name="odd_in_dw")
    dx1 = _mm(dh_o, w["odd_w_in"], nt=True, epilogue=_add_alpha, extra=du1, name="odd_in_dx", **blocks)

    du1, du1b, dw1_0, dw2_0, ln_0 = mlp_block_bwd(dx1, 0, xh2_0, r2_0, xh1_0, r1_0, act0, hid0, x0nb)
    d_even_out = _mm(y_e.T, du1b, out_dtypes=(BF16,), name="even_out_dw")
    dy_e = _mm(du1b, w["even_w_out"], nt=True, name="even_out_dx")
    doa, dob = _stack_rows(dy_e[:, :A_Q_W], nq), _heads(dy_e[:, A_Q_W:], B_HEADS)
    dqa, dka, dva, dsink = _band_bwd(qa, ka, va, doa, oa, lse_a, slope_a, sink_a, name="swa_bwd", **a_cfg)
    pieces = [_unstack_rows(dqa, nq), _unheads(dka[:, 0, BLK:]), _unheads(dva[:, 0, BLK:])]
    for gi, (_, dil) in enumerate(B_PATTERNS):
        qs, ks, vs, slope = b_in[gi]
        dq, dk, dv = _band_bwd(qs, ks, vs, _to_strided(dob, dil)[None], _to_strided(ob, dil)[None],
                               _to_strided(lse_b, dil)[None], slope, None, name=f"dil{gi}_bwd", **b_cfg[gi])
        pieces += [_unheads(_from_strided(t, dil)) for t in (dq[0], dk[0, :, BLK:], dv[0, :, BLK:])]
    dh_e = _add_block_pad(bf(jnp.concatenate(pieces, axis=1)), even_cp)
    d_even_in = _mm(x0b.T, dh_e, out_blocks=True, out_dtypes=(BF16,), name="even_in_dw")
    grad_x = _mm(dh_e, w["even_w_in"], nt=True, epilogue=_add_alpha, extra=du1, name="even_in_dx", **blocks)

    def by_column_block(t):
        r = t.shape[0]
        return t.reshape(r, N_DEV, -1).transpose(1, 0, 2)

    big = {"even_w_in": d_even_in, "even_w_out": by_column_block(d_even_out), "odd_w_in": d_odd_in,
           "odd_w_uq": by_column_block(d_uq), "odd_w_ukv": by_column_block(d_ukv), "odd_w_out": d_odd_out,
           "mlp_w1_0": dw1_0, "mlp_w1_1": dw1_1, "mlp_w2_0": dw2_0, "mlp_w2_1": dw2_1}
    ln = [jnp.concatenate([a, b], axis=0) for a, b in zip(ln_0, ln_1)]
    small = {"ln": ln, "sinks": dsink[:, :, 0].reshape(-1), "gq": dgq[0], "gkv": dgkv[0], "loss": loss_vec[0, :1]}
    return grad_x, big, small


_BIG_ORDER = ("even_w_in", "even_w_out", "mlp_w1_0", "mlp_w2_0", "odd_w_in", "odd_w_uq", "odd_w_ukv", "odd_w_out",
              "mlp_w1_1", "mlp_w2_1")


def kernel(x, even_w_in, even_sinks, even_w_out, odd_w_in, odd_q_norm_g, odd_kv_norm_g, odd_w_uq, odd_w_ukv, odd_w_out, ln1_g, ln1_b, mlp_w1, mlp_w2, ln2_g, ln2_b, loss_target, m_even_w_in, m_even_sinks, m_even_w_out, m_odd_w_in, m_odd_q_norm_g, m_odd_kv_norm_g, m_odd_w_uq, m_odd_w_ukv, m_odd_w_out, m_ln1_g, m_ln1_b, m_mlp_w1, m_mlp_w2, m_ln2_g, m_ln2_b, v_even_w_in, v_even_sinks, v_even_w_out, v_odd_w_in, v_odd_q_norm_g, v_odd_kv_norm_g, v_odd_w_uq, v_odd_w_ukv, v_odd_w_out, v_ln1_g, v_ln1_b, v_mlp_w1, v_mlp_w2, v_ln2_g, v_ln2_b):
    weights = dict(even_w_in=even_w_in, even_sinks=even_sinks, even_w_out=even_w_out, odd_w_in=odd_w_in,
                   odd_q_norm_g=odd_q_norm_g, odd_kv_norm_g=odd_kv_norm_g, odd_w_uq=odd_w_uq, odd_w_ukv=odd_w_ukv,
                   odd_w_out=odd_w_out, ln1_g=ln1_g, ln1_b=ln1_b, mlp_w1=mlp_w1, mlp_w2=mlp_w2, ln2_g=ln2_g, ln2_b=ln2_b)
    mom_m = dict(even_w_in=m_even_w_in, even_sinks=m_even_sinks, even_w_out=m_even_w_out, odd_w_in=m_odd_w_in,
                 odd_q_norm_g=m_odd_q_norm_g, odd_kv_norm_g=m_odd_kv_norm_g, odd_w_uq=m_odd_w_uq, odd_w_ukv=m_odd_w_ukv,
                 odd_w_out=m_odd_w_out, ln1_g=m_ln1_g, ln1_b=m_ln1_b, mlp_w1=m_mlp_w1, mlp_w2=m_mlp_w2, ln2_g=m_ln2_g, ln2_b=m_ln2_b)
    mom_v = dict(even_w_in=v_even_w_in, even_sinks=v_even_sinks, even_w_out=v_even_w_out, odd_w_in=v_odd_w_in,
                 odd_q_norm_g=v_odd_q_norm_g, odd_kv_norm_g=v_odd_kv_norm_g, odd_w_uq=v_odd_w_uq, odd_w_ukv=v_odd_w_ukv,
                 odd_w_out=v_odd_w_out, ln1_g=v_ln1_g, ln1_b=v_ln1_b, mlp_w1=v_mlp_w1, mlp_w2=v_mlp_w2, ln2_g=v_ln2_g, ln2_b=v_ln2_b)
    order = list(weights)
    mx, my, mc = _mesh_pos()
    dev = 4 * mx + 2 * my + mc
    n_q, n_kv = odd_q_norm_g.shape[1], odd_kv_norm_g.shape[1]

    def lane_padded(t):
        return jnp.pad(t, ((0, 0), (0, _lane_pad(t.shape[1]) - t.shape[1]))).astype(BF16)

    shards = {"even_w_in": lane_padded(even_w_in[0]), "even_w_out": even_w_out[0].astype(BF16),
              "mlp_w1_0": mlp_w1[0].astype(BF16), "mlp_w2_0": mlp_w2[0].astype(BF16),
              "odd_w_in": lane_padded(odd_w_in[0]), "odd_w_uq": odd_w_uq[0].astype(BF16),
              "odd_w_ukv": odd_w_ukv[0].astype(BF16), "odd_w_out": odd_w_out[0].astype(BF16),
              "mlp_w1_1": mlp_w1[1].astype(BF16), "mlp_w2_1": mlp_w2[1].astype(BF16)}
    gains = jnp.concatenate([odd_q_norm_g, odd_kv_norm_g, jnp.zeros((1, LANES - n_q - n_kv), F32)], axis=1)
    gathered = _all_gather([shards[n] for n in _BIG_ORDER] + [gains], name="comm_weights_gather")
    full = dict(zip(_BIG_ORDER, gathered[:-1]))
    all_gains = gathered[-1][:, 0]

    def columns(t):
        return t.transpose(1, 0, 2).reshape(t.shape[1], -1)

    def rows(t):
        return t.reshape(-1, t.shape[2])

    w_full = {"even_w_in": full["even_w_in"], "even_w_out": columns(full["even_w_out"]),
              "odd_w_in": full["odd_w_in"], "odd_w_uq": columns(full["odd_w_uq"]),
              "odd_w_ukv": columns(full["odd_w_ukv"]), "odd_w_out": rows(full["odd_w_out"]),
              "mlp_w1": (full["mlp_w1_0"], full["mlp_w1_1"]), "mlp_w2": (rows(full["mlp_w2_0"]), rows(full["mlp_w2_1"]))}
    gq = all_gains[:, :n_q].reshape(-1)
    gkv = all_gains[:, n_q:n_q + n_kv].reshape(-1)

    grad_x, big, small = _local_step(x[0], loss_target[0], w_full, even_sinks[0], gq, gkv, ln1_g, ln1_b, ln2_g, ln2_b)

    by_dev = [big[n] for n in _BIG_ORDER]
    from_sibling = _to_sibling(by_dev, name="comm_grads_sibling")
    partial = [_pair_sum(a, b, mc, name=f"pair_sum_{n}") for n, a, b in zip(_BIG_ORDER, by_dev, from_sibling)]
    from_chips = _to_chips(partial, name="comm_grads_chips")
    reduced = {n: _final_sum(a, b, 2 * mx + my, name=f"final_sum_{n}") for n, a, b in zip(_BIG_ORDER, partial, from_chips)}

    small_parts = [t.reshape(-1) for t in small["ln"]] + [small["sinks"], small["gq"], small["gkv"], small["loss"]]
    small_sizes = [p.shape[0] for p in small_parts]
    n_small = sum(small_sizes)
    small_rows = -(-n_small // (8 * LANES)) * 8
    small_flat = jnp.concatenate(small_parts + [jnp.zeros((small_rows * LANES - n_small,), F32)]).reshape(small_rows, LANES)
    (small_all,) = _all_gather([small_flat], name="comm_small_gather")
    totals = _sum_devices(small_all, name="small_sum").reshape(-1)
    tot, off = [], 0
    for size in small_sizes:
        tot.append(totals[off:off + size])
        off += size
    grads = {}
    for i, n in enumerate(("ln1_g", "ln1_b", "ln2_g", "ln2_b")):
        grads[n] = tot[i].reshape(weights[n].shape)
    grads["even_sinks"] = tot[4].reshape(even_sinks.shape)
    grads["odd_q_norm_g"] = lax.dynamic_slice(tot[5], (dev * n_q,), (n_q,)).reshape(odd_q_norm_g.shape)
    grads["odd_kv_norm_g"] = lax.dynamic_slice(tot[6], (dev * n_kv,), (n_kv,)).reshape(odd_kv_norm_g.shape)
    loss = tot[7][0]

    delta, new_m, new_v = {}, {}, {}
    big_names = ("even_w_in", "even_w_out", "odd_w_in", "odd_w_uq", "odd_w_ukv", "odd_w_out", "mlp_w1", "mlp_w2")
    for n in big_names:
        g_list = [reduced[f"{n}_0"], reduced[f"{n}_1"]] if n.startswith("mlp") else [reduced[n]]
        grads[n], delta[n], new_m[n], new_v[n] = _adamw(weights[n], g_list, mom_m[n], mom_v[n], name=f"adamw_{n}")
    small_names = [n for n in order if n not in big_names]
    n_sm = sum(weights[n].size for n in small_names)
    sm_rows = -(-n_sm // (8 * LANES)) * 8

    def pack_small(group):
        flat = [group[n].reshape(-1) for n in small_names]
        return jnp.concatenate(flat + [jnp.zeros((sm_rows * LANES - n_sm,), F32)]).reshape(1, sm_rows, LANES)

    res = _adamw(pack_small(weights), [pack_small(grads)[0]], pack_small(mom_m), pack_small(mom_v), name="adamw_small")
    off = 0
    for n in small_names:
        size = weights[n].size
        delta[n], new_m[n], new_v[n] = (t.reshape(-1)[off:off + size].reshape(weights[n].shape) for t in res[1:])
        off += size

    return (loss, grad_x[None], *[grads[n] for n in order], *[delta[n] for n in order],
            *[new_m[n] for n in order], *[new_v[n] for n in order])
```

```python
import math

import jax
import jax.numpy as jnp
import numpy as np
from jax import lax
from jax.experimental import pallas as pl
from jax.experimental.pallas import tpu as pltpu

F32 = jnp.float32
BF16 = jnp.bfloat16
MXU_DTYPE = BF16

HEAD_DIM = 64
A_Q_HEADS, A_KV_HEADS, A_WINDOW = 16, 2, 128
A_GROUP = A_Q_HEADS // A_KV_HEADS
B_HEADS = 8
B_PATTERNS = ((128, 1), (512, 4), (2048, 16))
C_HEADS = 16
D_HEADS, D_Q_RANK, D_KV_RANK, D_NOPE, D_ROPE, D_V = 16, 512, 256, 64, 32, 64
ROPE_BASE = 10000.0
LN_EPS, RMS_EPS = 1e-5, 1e-6
DEPTH = 2
ALPHA = (2 * DEPTH) ** 0.25
A_Q_W, A_KV_W, B_W = A_Q_HEADS * HEAD_DIM, A_KV_HEADS * HEAD_DIM, B_HEADS * HEAD_DIM
EVEN_IN = A_Q_W + 2 * A_KV_W + 3 * B_W * len(B_PATTERNS)
C_W = C_HEADS * HEAD_DIM
ODD_IN = 3 * C_W + D_Q_RANK + D_KV_RANK + D_ROPE
ADAM_LR, ADAM_B1, ADAM_B2, ADAM_EPS, ADAM_WD, ADAM_STEP = 0.001, 0.9, 0.999, 1e-08, 0.01, 10

N_DEV = 8
LANES = 128
BLK = 128
CAUSAL_TILE = 512
CUM_CHUNK = 256
NEG = -1e30
VMEM_LIMIT = 48 * 1024 * 1024

NN = ((1,), (0,))
NT = ((1,), (1,))
TN = ((0,), (0,))
MESH = pl.DeviceIdType.MESH
ANY = pl.BlockSpec(memory_space=pl.ANY)
HBM_SPEC = pl.BlockSpec(memory_space=pltpu.HBM)
SEM_SPEC = pl.BlockSpec(memory_space=pltpu.SEMAPHORE)
DATAFLOW_EFFECT = pltpu.SideEffectType.DATAFLOW_SIDE_EFFECTING


def _dot(a, b, dims):
    return lax.dot_general(a, b, (dims, ((), ())), preferred_element_type=F32)


def _bdot(a, b, dims):
    return jnp.stack([_dot(a[n], b[n], dims) for n in range(a.shape[0])])


def _params(*sem):
    return pltpu.CompilerParams(dimension_semantics=tuple(sem), vmem_limit_bytes=VMEM_LIMIT)


def _pick(n, cap, mult=LANES):
    if n <= cap:
        return n
    for t in range(cap - cap % mult, 0, -mult):
        if n % t == 0:
            return t
    raise ValueError(f"no tile for {n}")


def _lane_pad(c):
    return -(-c // LANES) * LANES


def _mm(a, b, *, name, nt=False, b_blocks=False, out_blocks=False, out_dtypes=(F32,), epilogue=None, extra=None, deps=()):
    m, k = a.shape
    if b_blocks:
        nb, kin, c = b.shape
        n = kin if nt else nb * c
        k_full = nb * c if nt else kin
    else:
        n, k_full = (b.shape if nt else b.shape[::-1])
    assert k == k_full, (a.shape, b.shape, nt, b_blocks)
    tm = _pick(m, 1024, 8)
    if b_blocks and not nt:
        tn, tk = c, _pick(k, 3072)
    elif b_blocks:
        per_step = max(g for g in (1, 2, 4, 8) if g * c <= 3072)
        tn, tk = _pick(n, 512), per_step * c
    elif out_blocks:
        tn, tk = n // N_DEV, _pick(k, 3072)
    else:
        tn, tk = _pick(n, 512), _pick(k, 3072)
    nk = k // tk
    n_out = len(out_dtypes)

    def body(*refs):
        a_ref, b_ref = refs[0], refs[1]
        e_ref = refs[2] if extra is not None else None
        first_out = 2 + (extra is not None) + len(deps)
        out_refs = refs[first_out:first_out + n_out]

        def finish(acc):
            e = None if e_ref is None else e_ref[...]
            outs = (acc,) if epilogue is None else epilogue(acc, e)
            for r, o in zip(out_refs, outs):
                r[...] = o.astype(r.dtype).reshape(r.shape)

        if b_blocks and nt:
            part = _dot(a_ref[:, :c], b_ref[0], NT)
            for blk in range(1, per_step):
                part += _dot(a_ref[:, blk * c:(blk + 1) * c], b_ref[blk], NT)
        else:
            part = _dot(a_ref[...], b_ref[0] if b_blocks else b_ref[...], NT if nt else NN)
        if nk == 1:
            finish(part)
        else:
            acc_ref = refs[first_out + n_out]
            kk = pl.program_id(2)

            @pl.when(kk == 0)
            def _():
                acc_ref[...] = part

            @pl.when(kk > 0)
            def _():
                acc_ref[...] += part

            @pl.when(kk == nk - 1)
            def _():
                finish(acc_ref[...])

    if b_blocks and not nt:
        b_spec = pl.BlockSpec((1, tk, tn), lambda i, j, kk: (j, kk, 0))
    elif b_blocks:
        b_spec = pl.BlockSpec((per_step, tn, c), lambda i, j, kk: (kk, j, 0))
    elif nt:
        b_spec = pl.BlockSpec((tn, tk), lambda i, j, kk: (j, kk))
    else:
        b_spec = pl.BlockSpec((tk, tn), lambda i, j, kk: (kk, j))
    in_specs = [pl.BlockSpec((tm, tk), lambda i, j, kk: (i, kk)), b_spec]
    ins = [a.astype(MXU_DTYPE), b.astype(MXU_DTYPE)]
    if extra is not None:
        in_specs.append(pl.BlockSpec((tm, tn), lambda i, j, kk: (i, j)))
        ins.append(extra)
    in_specs += [ANY] * len(deps)
    ins += list(deps)
    if out_blocks:
        out_shape = tuple(jax.ShapeDtypeStruct((N_DEV, m, tn), d) for d in out_dtypes)
        out_specs = tuple(pl.BlockSpec((1, tm, tn), lambda i, j, kk: (j, i, 0)) for _ in out_dtypes)
    else:
        out_shape = tuple(jax.ShapeDtypeStruct((m, n), d) for d in out_dtypes)
        out_specs = tuple(pl.BlockSpec((tm, tn), lambda i, j, kk: (i, j)) for _ in out_dtypes)
    outs = pl.pallas_call(
        body,
        out_shape=out_shape,
        grid=(m // tm, n // tn, nk),
        in_specs=in_specs,
        out_specs=out_specs,
        scratch_shapes=[pltpu.VMEM((tm, tn), F32)] if nk > 1 else [],
        compiler_params=_params("parallel", "parallel", "arbitrary"),
        name=name,
    )(*ins)
    return outs[0] if n_out == 1 else outs


def _relu_sq(acc, _):
    act = jnp.maximum(acc, 0.0)
    return act, act * act


def _relu_sq_grad(acc, act):
    return (acc * (2.0 * act.astype(F32)),)


def _add_alpha(acc, du):
    return (acc + ALPHA * du,)


def _ln_fwd(x, mixed, g, b, *, name, deps=()):
    s, d = x.shape
    tr = _pick(s, 256, 8)

    def body(x_ref, m_ref, g_ref, b_ref, *rest):
        y_ref, yb_ref, xh_ref, r_ref = rest[len(deps):]
        u = ALPHA * x_ref[...] + m_ref[...]
        mu = jnp.mean(u, axis=-1, keepdims=True)
        xc = u - mu
        var = jnp.mean(xc * xc, axis=-1, keepdims=True)
        r = lax.rsqrt(var + LN_EPS)
        xh = xc * r
        y = xh * g_ref[...] + b_ref[...]
        y_ref[...] = y
        yb_ref[...] = y.astype(MXU_DTYPE)
        xh_ref[...] = xh
        r_ref[...] = r

    row = pl.BlockSpec((tr, d), lambda i: (i, 0))
    vec = pl.BlockSpec((1, d), lambda i: (0, 0))
    return pl.pallas_call(
        body,
        out_shape=(jax.ShapeDtypeStruct((s, d), F32), jax.ShapeDtypeStruct((s, d), MXU_DTYPE),
                   jax.ShapeDtypeStruct((s, d), F32), jax.ShapeDtypeStruct((s, 1), F32)),
        grid=(s // tr,),
        in_specs=[row, row, vec, vec] + [ANY] * len(deps),
        out_specs=(row, row, row, pl.BlockSpec((tr, 1), lambda i: (i, 0))),
        compiler_params=_params("parallel"),
        name=name,
    )(x, mixed, g.reshape(1, d), b.reshape(1, d), *deps)


def _ln_bwd(dy, xh, r, g, *, name):
    s, d = dy.shape
    tr = _pick(s, 256, 8)

    def body(dy_ref, xh_ref, r_ref, g_ref, du_ref, dub_ref, dg_ref, db_ref):
        dyv, xhv = dy_ref[...], xh_ref[...]
        dxh = dyv * g_ref[...]
        c1 = jnp.mean(dxh, axis=-1, keepdims=True)
        c2 = jnp.mean(dxh * xhv, axis=-1, keepdims=True)
        du = r_ref[...] * (dxh - c1 - xhv * c2)
        du_ref[...] = du
        dub_ref[...] = du.astype(MXU_DTYPE)

        @pl.when(pl.program_id(0) == 0)
        def _():
            dg_ref[...] = jnp.zeros_like(dg_ref)
            db_ref[...] = jnp.zeros_like(db_ref)

        dg_ref[...] += jnp.sum(dyv * xhv, axis=0, keepdims=True)
        db_ref[...] += jnp.sum(dyv, axis=0, keepdims=True)

    row = pl.BlockSpec((tr, d), lambda i: (i, 0))
    vec = pl.BlockSpec((1, d), lambda i: (0, 0))
    return pl.pallas_call(
        body,
        out_shape=(jax.ShapeDtypeStruct((s, d), F32), jax.ShapeDtypeStruct((s, d), MXU_DTYPE),
                   jax.ShapeDtypeStruct((1, d), F32), jax.ShapeDtypeStruct((1, d), F32)),
        grid=(s // tr,),
        in_specs=[row, row, pl.BlockSpec((tr, 1), lambda i: (i, 0)), vec],
        out_specs=(row, row, vec, vec),
        compiler_params=_params("arbitrary"),
        name=name,
    )(dy, xh, r, g.reshape(1, d))


def _rms_fwd(x, g, *, name):
    s, d = x.shape
    tr = _pick(s, 512, 8)

    def body(x_ref, g_ref, y_ref, r_ref):
        xv = x_ref[...]
        r = lax.rsqrt(jnp.mean(xv * xv, axis=-1, keepdims=True) + RMS_EPS)
        y_ref[...] = (xv * r * g_ref[...]).astype(y_ref.dtype)
        r_ref[...] = r

    return pl.pallas_call(
        body,
        out_shape=(jax.ShapeDtypeStruct((s, d), MXU_DTYPE), jax.ShapeDtypeStruct((s, 1), F32)),
        grid=(s // tr,),
        in_specs=[pl.BlockSpec((tr, d), lambda i: (i, 0)), pl.BlockSpec((1, d), lambda i: (0, 0))],
        out_specs=(pl.BlockSpec((tr, d), lambda i: (i, 0)), pl.BlockSpec((tr, 1), lambda i: (i, 0))),
        compiler_params=_params("parallel"),
        name=name,
    )(x, g.reshape(1, d))


def _rms_bwd(dy, x, r, g, *, name):
    s, d = x.shape
    tr = _pick(s, 512, 8)

    def body(dy_ref, x_ref, r_ref, g_ref, dx_ref, dg_ref):
        dyv, rv = dy_ref[...], r_ref[...]
        xn = x_ref[...] * rv
        dxn = dyv * g_ref[...]
        dx_ref[...] = rv * (dxn - xn * jnp.mean(dxn * xn, axis=-1, keepdims=True))

        @pl.when(pl.program_id(0) == 0)
        def _():
            dg_ref[...] = jnp.zeros_like(dg_ref)

        dg_ref[...] += jnp.sum(dyv * xn, axis=0, keepdims=True)

    row = pl.BlockSpec((tr, d), lambda i: (i, 0))
    vec = pl.BlockSpec((1, d), lambda i: (0, 0))
    return pl.pallas_call(
        body,
        out_shape=(jax.ShapeDtypeStruct((s, d), F32), jax.ShapeDtypeStruct((1, d), F32)),
        grid=(s // tr,),
        in_specs=[row, row, pl.BlockSpec((tr, 1), lambda i: (i, 0)), vec],
        out_specs=(row, vec),
        compiler_params=_params("arbitrary"),
        name=name,
    )(dy, x, r, g.reshape(1, d))


def _rope_tables(s, inverse):
    inv_freq = ROPE_BASE ** (-jnp.arange(0, D_ROPE, 2, dtype=F32) / D_ROPE)
    ang = jnp.arange(s, dtype=F32)[:, None] * inv_freq[None, :]
    cos, sin = jnp.cos(ang), jnp.sin(ang)
    if inverse:
        sin = -sin
    half = D_ROPE // 2
    one, zero = jnp.ones((s, D_NOPE), F32), jnp.zeros((s, D_NOPE), F32)
    pad1, pad0 = jnp.ones((s, LANES - D_NOPE - D_ROPE), F32), jnp.zeros((s, LANES - D_NOPE - D_ROPE), F32)
    zh = jnp.zeros((s, half), F32)
    c = jnp.concatenate([one, cos, cos, pad1], axis=1)
    s_lo = jnp.concatenate([zero, -sin, zh, pad0], axis=1)
    s_hi = jnp.concatenate([zero, zh, sin, pad0], axis=1)
    return c, s_lo, s_hi


def _rope(x, tables, *, out_dtype, head_sum=False, name):
    h, s, w = x.shape
    ts = _pick(s, 2048, 8)
    half = D_ROPE // 2

    def body(x_ref, c_ref, lo_ref, hi_ref, y_ref, *sum_ref):
        xv = x_ref[0]
        y = (xv * c_ref[...] + pltpu.roll(xv, w - half, 1) * lo_ref[...]
             + pltpu.roll(xv, half, 1) * hi_ref[...])
        y_ref[0] = y.astype(y_ref.dtype)
        if head_sum:
            @pl.when(pl.program_id(1) == 0)
            def _():
                sum_ref[0][...] = jnp.zeros_like(sum_ref[0])

            sum_ref[0][...] += y

    tab = pl.BlockSpec((ts, w), lambda i, hh: (i, 0))
    blk = pl.BlockSpec((1, ts, w), lambda i, hh: (hh, i, 0))
    out_shape = [jax.ShapeDtypeStruct((h, s, w), out_dtype)]
    out_specs = [blk]
    if head_sum:
        out_shape.append(jax.ShapeDtypeStruct((s, w), F32))
        out_specs.append(tab)
    res = pl.pallas_call(
        body,
        out_shape=tuple(out_shape),
        grid=(s // ts, h),
        in_specs=[blk, tab, tab, tab],
        out_specs=tuple(out_specs),
        compiler_params=_params("parallel", "arbitrary"),
        name=name,
    )(x, *tables)
    return res if head_sum else res[0]


def _band_scores(q, kw, slope, i, *, scale, n_back, bps):
    b, r, _ = q.shape
    sc = _bdot(q, kw, NT) * scale
    shape = (b, r, 2 * BLK)
    row = lax.broadcasted_iota(jnp.int32, shape, 1) & (BLK - 1)
    col = lax.broadcasted_iota(jnp.int32, shape, 2)
    rel = BLK + row - col
    first_col = jnp.where(i % bps == 0, BLK, 0)
    valid = (rel >= 0) & (rel <= n_back) & (col >= first_col)
    return jnp.where(valid, sc - slope * rel.astype(F32), NEG)


def _band_fwd(q, k, v, slope, sink, *, scale, n_back, bps, name):
    g, b, rows, dh = q.shape
    r = slope.shape[2]
    nq = rows // r
    skv = k.shape[2]
    use_sink = sink is not None

    def body(*refs):
        q_ref, k_ref, v_ref, slope_ref = refs[:4]
        sink_ref = refs[4] if use_sink else None
        o_ref, lse_ref = refs[4 + use_sink:]
        i = pl.program_id(1)
        off = pl.multiple_of(i * BLK, BLK)
        kw = k_ref[0, :, pl.ds(off, 2 * BLK), :]
        vw = v_ref[0, :, pl.ds(off, 2 * BLK), :]
        sc = _band_scores(q_ref[0], kw, slope_ref[0], i, scale=scale, n_back=n_back, bps=bps)
        m = jnp.max(sc, axis=-1, keepdims=True)
        if use_sink:
            m = jnp.maximum(m, sink_ref[0])
        p = jnp.exp(sc - m)
        l = jnp.sum(p, axis=-1, keepdims=True)
        if use_sink:
            l = l + jnp.exp(sink_ref[0] - m)
        o_ref[0] = _bdot(p.astype(MXU_DTYPE), vw, NN) / l
        lse_ref[0] = m + jnp.log(l)

    qspec = pl.BlockSpec((1, b, r, dh), lambda gg, i: (gg, 0, i, 0))
    kspec = pl.BlockSpec((1, b, skv, dh), lambda gg, i: (gg, 0, 0, 0))
    rspec = pl.BlockSpec((1, b, r, 1), lambda gg, i: (gg, 0, 0, 0))
    ins = [q, k, v, slope] + ([sink] if use_sink else [])
    return pl.pallas_call(
        body,
        out_shape=(jax.ShapeDtypeStruct((g, b, rows, dh), F32), jax.ShapeDtypeStruct((g, b, rows, 1), F32)),
        grid=(g, nq),
        in_specs=[qspec, kspec, kspec, rspec] + ([rspec] if use_sink else []),
        out_specs=(qspec, pl.BlockSpec((1, b, r, 1), lambda gg, i: (gg, 0, i, 0))),
        compiler_params=_params("parallel", "arbitrary"),
        name=name,
    )(*ins)


def _band_bwd(q, k, v, do, o, lse, slope, sink, *, scale, n_back, bps, name):
    g, b, rows, dh = q.shape
    r = slope.shape[2]
    nq = rows // r
    skv = k.shape[2]
    use_sink = sink is not None
    stacked = r // BLK

    def body(*refs):
        q_ref, k_ref, v_ref, do_ref, o_ref, lse_ref, slope_ref = refs[:7]
        sink_ref = refs[7] if use_sink else None
        dq_ref, dk_ref, dv_ref = refs[7 + use_sink:10 + use_sink]
        i = pl.program_id(1)

        @pl.when(i == 0)
        def _():
            dk_ref[...] = jnp.zeros_like(dk_ref)
            dv_ref[...] = jnp.zeros_like(dv_ref)

        off = pl.multiple_of(i * BLK, BLK)
        qb = q_ref[0]
        kw = k_ref[0, :, pl.ds(off, 2 * BLK), :]
        vw = v_ref[0, :, pl.ds(off, 2 * BLK), :]
        dof = do_ref[0]
        dob = dof.astype(MXU_DTYPE)
        lse_b = lse_ref[0]
        delta = jnp.sum(dof * o_ref[0], axis=-1, keepdims=True)
        sc = _band_scores(qb, kw, slope_ref[0], i, scale=scale, n_back=n_back, bps=bps)
        p = jnp.exp(sc - lse_b)
        ds = (p * (_bdot(dob, vw, NT) - delta) * scale).astype(MXU_DTYPE)
        dq_ref[0] = _bdot(ds, kw, NN)
        dk_ref[0, :, pl.ds(off, 2 * BLK), :] += _bdot(ds, qb, TN)
        dv_ref[0, :, pl.ds(off, 2 * BLK), :] += _bdot(p.astype(MXU_DTYPE), dob, TN)

        if use_sink:
            dsink_ref = refs[10 + use_sink]

            @pl.when(i == 0)
            def _():
                dsink_ref[...] = jnp.zeros_like(dsink_ref)

            contrib = -jnp.exp(sink_ref[0] - lse_b) * delta
            for n in range(stacked):
                part = jnp.sum(contrib[0, n * BLK:(n + 1) * BLK, :], axis=0, keepdims=True)
                dsink_ref[0, n:n + 1, :] += jnp.broadcast_to(part, (1, LANES))

    def qspec(w):
        return pl.BlockSpec((1, b, r, w), lambda gg, i: (gg, 0, i, 0))

    kspec = pl.BlockSpec((1, b, skv, dh), lambda gg, i: (gg, 0, 0, 0))
    rspec = pl.BlockSpec((1, b, r, 1), lambda gg, i: (gg, 0, 0, 0))
    ins = [q, k, v, do, o, lse, slope] + ([sink] if use_sink else [])
    in_specs = [qspec(dh), kspec, kspec, qspec(dh), qspec(dh), qspec(1), rspec] + ([rspec] if use_sink else [])
    out_shape = [jax.ShapeDtypeStruct((g, b, rows, dh), F32), jax.ShapeDtypeStruct((g, b, skv, dh), F32),
                 jax.ShapeDtypeStruct((g, b, skv, dh), F32)]
    out_specs = [qspec(dh), kspec, kspec]
    if use_sink:
        assert b == 1
        out_shape.append(jax.ShapeDtypeStruct((g, stacked, LANES), F32))
        out_specs.append(pl.BlockSpec((1, stacked, LANES), lambda gg, i: (gg, 0, 0)))
    return pl.pallas_call(
        body,
        out_shape=tuple(out_shape),
        grid=(g, nq),
        in_specs=in_specs,
        out_specs=tuple(out_specs),
        compiler_params=_params("parallel", "arbitrary"),
        name=name,
    )(*ins)


def _merge(outs, lses, *, name):
    h, s, dv = outs[0].shape
    ts = _pick(s, 512, 8)

    def body(o0, o1, o2, l0, l1, l2, ob_ref, lt_ref):
        a, b, c = l0[0], l1[0], l2[0]
        m = jnp.maximum(jnp.maximum(a, b), c)
        ea, eb, ec = jnp.exp(a - m), jnp.exp(b - m), jnp.exp(c - m)
        den = ea + eb + ec
        ob_ref[0] = (ea / den) * o0[0] + (eb / den) * o1[0] + (ec / den) * o2[0]
        lt_ref[0] = m + jnp.log(den)

    ospec = pl.BlockSpec((1, ts, dv), lambda hh, i: (hh, i, 0))
    lspec = pl.BlockSpec((1, ts, 1), lambda hh, i: (hh, i, 0))
    return pl.pallas_call(
        body,
        out_shape=(jax.ShapeDtypeStruct((h, s, dv), F32), jax.ShapeDtypeStruct((h, s, 1), F32)),
        grid=(h, s // ts),
        in_specs=[ospec] * 3 + [lspec] * 3,
        out_specs=(ospec, lspec),
        compiler_params=_params("parallel", "parallel"),
        name=name,
    )(*outs, *lses)


def _tile_iotas(t):
    return lax.broadcasted_iota(jnp.int32, (t, t), 0), lax.broadcasted_iota(jnp.int32, (t, t), 1)


def _causal_fwd(q, k, v, *, scale, name):
    h, s, dq = q.shape
    dv = v.shape[-1]
    t = min(CAUSAL_TILE, s)

    def body(q_ref, k_ref, v_ref, o_ref, lse_ref):
        i = pl.program_id(1)
        qb = q_ref[0]

        def tile(j, carry, diagonal):
            m, l, acc = carry
            off = pl.multiple_of(j * t, t)
            sc = _dot(qb, k_ref[0, pl.ds(off, t), :], NT) * scale
            if diagonal:
                row, col = _tile_iotas(t)
                sc = jnp.where(row >= col, sc, NEG)
            m_new = jnp.maximum(m, jnp.max(sc, axis=-1, keepdims=True))
            a = jnp.exp(m - m_new)
            p = jnp.exp(sc - m_new)
            l = a * l + jnp.sum(p, axis=-1, keepdims=True)
            acc = a * acc + _dot(p.astype(MXU_DTYPE), v_ref[0, pl.ds(off, t), :], NN)
            return m_new, l, acc

        init = (jnp.full((t, 1), NEG, F32), jnp.zeros((t, 1), F32), jnp.zeros((t, dv), F32))
        carry = lax.fori_loop(0, i, lambda j, c: tile(j, c, False), init)
        m, l, acc = tile(i, carry, True)
        o_ref[0] = acc / l
        lse_ref[0] = m + jnp.log(l)

    def qspec(w):
        return pl.BlockSpec((1, t, w), lambda hh, i: (hh, i, 0))

    def kspec(w):
        return pl.BlockSpec((1, s, w), lambda hh, i: (hh, 0, 0))

    return pl.pallas_call(
        body,
        out_shape=(jax.ShapeDtypeStruct((h, s, dv), F32), jax.ShapeDtypeStruct((h, s, 1), F32)),
        grid=(h, s // t),
        in_specs=[qspec(dq), kspec(dq), kspec(dv)],
        out_specs=(qspec(dv), qspec(1)),
        compiler_params=_params("parallel", "arbitrary"),
        name=name,
    )(q, k, v)


def _causal_bwd(q, k, v, do, o, lse, *, scale, name):
    h, s, dq = q.shape
    dv = v.shape[-1]
    t = min(CAUSAL_TILE, s)

    def body(q_ref, k_ref, v_ref, do_ref, o_ref, lse_ref, dq_ref, dk_ref, dv_ref):
        i = pl.program_id(1)

        @pl.when(i == 0)
        def _():
            dk_ref[...] = jnp.zeros_like(dk_ref)
            dv_ref[...] = jnp.zeros_like(dv_ref)

        qb = q_ref[0]
        dof = do_ref[0]
        dob = dof.astype(MXU_DTYPE)
        lse_b = lse_ref[0]
        delta = jnp.sum(dof * o_ref[0], axis=-1, keepdims=True)

        def tile(j, dq_acc, diagonal):
            off = pl.multiple_of(j * t, t)
            kb = k_ref[0, pl.ds(off, t), :]
            vb = v_ref[0, pl.ds(off, t), :]
            sc = _dot(qb, kb, NT) * scale
            if diagonal:
                row, col = _tile_iotas(t)
                sc = jnp.where(row >= col, sc, NEG)
            p = jnp.exp(sc - lse_b)
            ds = (p * (_dot(dob, vb, NT) - delta) * scale).astype(MXU_DTYPE)
            dk_ref[0, pl.ds(off, t), :] += _dot(ds, qb, TN)
            dv_ref[0, pl.ds(off, t), :] += _dot(p.astype(MXU_DTYPE), dob, TN)
            return dq_acc + _dot(ds, kb, NN)

        dq_acc = lax.fori_loop(0, i, lambda j, c: tile(j, c, False), jnp.zeros((t, dq), F32))
        dq_ref[0] = tile(i, dq_acc, True)

    def qspec(w):
        return pl.BlockSpec((1, t, w), lambda hh, i: (hh, i, 0))

    def kspec(w):
        return pl.BlockSpec((1, s, w), lambda hh, i: (hh, 0, 0))

    return pl.pallas_call(
        body,
        out_shape=(jax.ShapeDtypeStruct((h, s, dq), F32), jax.ShapeDtypeStruct((h, s, dq), F32),
                   jax.ShapeDtypeStruct((h, s, dv), F32)),
        grid=(h, s // t),
        in_specs=[qspec(dq), kspec(dq), kspec(dv), qspec(dv), qspec(dv), qspec(1)],
        out_specs=(qspec(dq), kspec(dq), kspec(dv)),
        compiler_params=_params("parallel", "arbitrary"),
        name=name,
    )(q, k, v, do, o, lse)


def _split_cumsum(x, tri):
    hi = x.astype(BF16)
    lo = (x - hi.astype(F32)).astype(BF16)
    return _dot(hi, tri, NN) + _dot(lo, tri, NN)


def _chunked_cumsum(x, tri, run, *, reverse, sign):
    c = tri.shape[0]
    n = x.shape[1] // c
    parts = [None] * n
    for idx in (reversed(range(n)) if reverse else range(n)):
        xc = x[:, idx * c:(idx + 1) * c]
        parts[idx] = sign * (run + _split_cumsum(xc, tri))
        run = run + jnp.sum(xc, axis=-1, keepdims=True)
    return (parts[0] if n == 1 else jnp.concatenate(parts, axis=1)), run


def _sb_logs(z):
    e = jnp.exp(-jnp.abs(z))
    l1 = jnp.log(1.0 + e)
    return e, jnp.minimum(z, 0.0) - l1, -jnp.maximum(z, 0.0) - l1


def _sb_fwd(q, k, v, *, scale, name):
    h, s, dh = q.shape
    t = min(CAUSAL_TILE, s)
    cc = min(CUM_CHUNK, t)

    def body(q_ref, k_ref, v_ref, o_ref, t_ref):
        i = pl.program_id(1)
        qb = q_ref[0]
        crow, ccol = _tile_iotas(cc)
        after = (crow > ccol).astype(BF16)

        def tile(j, carry, diagonal):
            run, acc = carry
            off = pl.multiple_of(j * t, t)
            z = _dot(qb, k_ref[0, pl.ds(off, t), :], NT) * scale
            _, log_beta, log_keep = _sb_logs(z)
            if diagonal:
                row, col = _tile_iotas(t)
                strict = row > col
                log_keep = jnp.where(strict, log_keep, 0.0)
            a, run = _chunked_cumsum(log_keep, after, run, reverse=True, sign=1.0)
            w = jnp.exp(log_beta + a)
            if diagonal:
                w = jnp.where(strict, w, 0.0)
            return run, acc + _dot(w.astype(MXU_DTYPE), v_ref[0, pl.ds(off, t), :], NN)

        carry = tile(i, (jnp.zeros((t, 1), F32), jnp.zeros((t, dh), F32)), True)
        run, acc = lax.fori_loop(0, i, lambda jj, c: tile(i - 1 - jj, c, False), carry)
        o_ref[0] = acc
        t_ref[0] = run

    qspec = pl.BlockSpec((1, t, dh), lambda hh, i: (hh, i, 0))
    kspec = pl.BlockSpec((1, s, dh), lambda hh, i: (hh, 0, 0))
    return pl.pallas_call(
        body,
        out_shape=(jax.ShapeDtypeStruct((h, s, dh), F32), jax.ShapeDtypeStruct((h, s, 1), F32)),
        grid=(h, s // t),
        in_specs=[qspec, kspec, kspec],
        out_specs=(qspec, pl.BlockSpec((1, t, 1), lambda hh, i: (hh, i, 0))),
        compiler_params=_params("parallel", "arbitrary"),
        name=name,
    )(q, k, v)


def _sb_bwd(q, k, v, do, total, *, scale, name):
    h, s, dh = q.shape
    t = min(CAUSAL_TILE, s)
    cc = min(CUM_CHUNK, t)

    def body(q_ref, k_ref, v_ref, do_ref, t_ref, dq_ref, dk_ref, dv_ref):
        i = pl.program_id(1)

        @pl.when(i == 0)
        def _():
            dk_ref[...] = jnp.zeros_like(dk_ref)
            dv_ref[...] = jnp.zeros_like(dv_ref)

        qb = q_ref[0]
        dob = do_ref[0].astype(MXU_DTYPE)
        tot = t_ref[0]
        crow, ccol = _tile_iotas(cc)
        upto = (crow <= ccol).astype(BF16)
        before = (crow < ccol).astype(BF16)

        def tile(j, carry, diagonal):
            run_keep, run_g, dq_acc = carry
            off = pl.multiple_of(j * t, t)
            kb = k_ref[0, pl.ds(off, t), :]
            vb = v_ref[0, pl.ds(off, t), :]
            z = _dot(qb, kb, NT) * scale
            e, log_beta, log_keep = _sb_logs(z)
            if diagonal:
                row, col = _tile_iotas(t)
                strict = row > col
                log_keep = jnp.where(strict, log_keep, 0.0)
            a, run_keep = _chunked_cumsum(log_keep, upto, run_keep - tot, reverse=False, sign=-1.0)
            run_keep = run_keep + tot
            w = jnp.exp(log_beta + a)
            if diagonal:
                w = jnp.where(strict, w, 0.0)
            g = w * _dot(dob, vb, NT)
            prefix, run_g = _chunked_cumsum(g, before, run_g, reverse=False, sign=1.0)
            pos = z >= 0.0
            dz = (g * jnp.where(pos, e, 1.0) - jnp.where(pos, 1.0, e) * prefix) / (1.0 + e)
            if diagonal:
                dz = jnp.where(strict, dz, 0.0)
            dz = (dz * scale).astype(MXU_DTYPE)
            dk_ref[0, pl.ds(off, t), :] += _dot(dz, qb, TN)
            dv_ref[0, pl.ds(off, t), :] += _dot(w.astype(MXU_DTYPE), dob, TN)
            return run_keep, run_g, dq_acc + _dot(dz, kb, NN)

        zero = jnp.zeros((t, 1), F32)
        carry = lax.fori_loop(0, i, lambda j, c: tile(j, c, False), (zero, zero, jnp.zeros((t, dh), F32)))
        dq_ref[0] = tile(i, carry, True)[2]

    qspec = pl.BlockSpec((1, t, dh), lambda hh, i: (hh, i, 0))
    kspec = pl.BlockSpec((1, s, dh), lambda hh, i: (hh, 0, 0))
    shp = jax.ShapeDtypeStruct((h, s, dh), F32)
    return pl.pallas_call(
        body,
        out_shape=(shp, shp, shp),
        grid=(h, s // t),
        in_specs=[qspec, kspec, kspec, qspec, pl.BlockSpec((1, t, 1), lambda hh, i: (hh, i, 0))],
        out_specs=(qspec, kspec, kspec),
        compiler_params=_params("parallel", "arbitrary"),
        name=name,
    )(q, k, v, do, total)


def _loss_head(y, target, *, name):
    s, d = y.shape
    tr = _pick(s, 256, 8)

    def body(y_ref, t_ref, dy_ref, loss_ref):
        err = y_ref[...] - t_ref[...]
        dy_ref[...] = err * (1.0 / d)

        @pl.when(pl.program_id(0) == 0)
        def _():
            loss_ref[...] = jnp.zeros_like(loss_ref)

        per_tok = jnp.mean(err * err, axis=-1, keepdims=True)
        loss_ref[...] += 0.5 * jnp.sum(per_tok, axis=0, keepdims=True)

    row = pl.BlockSpec((tr, d), lambda i: (i, 0))
    return pl.pallas_call(
        body,
        out_shape=(jax.ShapeDtypeStruct((s, d), F32), jax.ShapeDtypeStruct((1, LANES), F32)),
        grid=(s // tr,),
        in_specs=[row, row],
        out_specs=(row, pl.BlockSpec((1, LANES), lambda i: (0, 0))),
        compiler_params=_params("arbitrary"),
        name=name,
    )(y, target)


def _adamw(w, grads, m, v, *, name, deps=()):
    nl, r, c = w.shape
    cp = grads[0].shape[1]
    tr = _pick(r, 256, 8)

    def body(*refs):
        w_ref, m_ref, v_ref = refs[:3]
        g_refs = refs[3:3 + nl]
        g_out, d_ref, m2_ref, v2_ref = refs[3 + nl + len(deps):]
        layer = pl.program_id(0)
        gv = g_refs[0][:, :c]
        for n in range(1, nl):
            gv = jnp.where(layer == n, g_refs[n][:, :c], gv)
        m2 = ADAM_B1 * m_ref[0] + (1.0 - ADAM_B1) * gv
        v2 = ADAM_B2 * v_ref[0] + (1.0 - ADAM_B2) * (gv * gv)
        m_hat = m2 / (1.0 - ADAM_B1 ** ADAM_STEP)
        v_hat = v2 / (1.0 - ADAM_B2 ** ADAM_STEP)
        g_out[0] = gv
        d_ref[0] = -ADAM_LR * (m_hat / (jnp.sqrt(v_hat) + ADAM_EPS) + ADAM_WD * w_ref[0])
        m2_ref[0] = m2
        v2_ref[0] = v2

    blk = pl.BlockSpec((1, tr, c), lambda l, i: (l, i, 0))
    gspec = pl.BlockSpec((tr, cp), lambda l, i: (i, 0))
    shp = jax.ShapeDtypeStruct((nl, r, c), F32)
    return pl.pallas_call(
        body,
        out_shape=(shp, shp, shp, shp),
        grid=(nl, r // tr),
        in_specs=[blk, blk, blk] + [gspec] * nl + [ANY] * len(deps),
        out_specs=(blk, blk, blk, blk),
        compiler_params=_params("parallel", "parallel"),
        name=name,
    )(w, m, v, *grads, *deps)


def _pair_sum(mine, recv, my_c, *, name):
    _, r, c = mine.shape
    tr = _pick(r, 512, 16)

    def body(c_ref, a_ref, b_ref, o_ref):
        o_ref[0] = (a_ref[0].astype(F32) + b_ref[0].astype(F32)).astype(o_ref.dtype)

    grid_spec = pltpu.PrefetchScalarGridSpec(
        num_scalar_prefetch=1,
        grid=(4, r // tr),
        in_specs=[pl.BlockSpec((1, tr, c), lambda kk, i, c_ref: (2 * kk + c_ref[0], i, 0)),
                  pl.BlockSpec((1, tr, c), lambda kk, i, c_ref: (kk, i, 0))],
        out_specs=pl.BlockSpec((1, tr, c), lambda kk, i, c_ref: (kk, i, 0)),
    )
    return pl.pallas_call(
        body,
        out_shape=jax.ShapeDtypeStruct((4, r, c), mine.dtype),
        grid_spec=grid_spec,
        compiler_params=_params("parallel", "parallel"),
        name=name,
    )(my_c.reshape(1).astype(jnp.int32), mine, recv)


def _final_sum(partial, recv, my_chip, *, name):
    _, r, c = partial.shape
    tr = _pick(r, 512, 16)

    def body(chip_ref, p_ref, r0, r1, r2, o_ref):
        o_ref[...] = ((p_ref[0].astype(F32) + r0[0].astype(F32)) + r1[0].astype(F32)) + r2[0].astype(F32)

    def slot(n):
        return pl.BlockSpec((1, tr, c), lambda i, chip_ref: (n, i, 0))

    grid_spec = pltpu.PrefetchScalarGridSpec(
        num_scalar_prefetch=1,
        grid=(r // tr,),
        in_specs=[pl.BlockSpec((1, tr, c), lambda i, chip_ref: (chip_ref[0], i, 0)), slot(0), slot(1), slot(2)],
        out_specs=pl.BlockSpec((tr, c), lambda i, chip_ref: (i, 0)),
    )
    return pl.pallas_call(
        body,
        out_shape=jax.ShapeDtypeStruct((r, c), F32),
        grid_spec=grid_spec,
        compiler_params=_params("parallel"),
        name=name,
    )(my_chip.reshape(1).astype(jnp.int32), partial, recv, recv, recv)


def _sum_devices(stack, *, name):
    n, r, c = stack.shape

    def body(s_ref, o_ref):
        acc = s_ref[0]
        for dev in range(1, n):
            acc = acc + s_ref[dev]
        o_ref[...] = acc

    return pl.pallas_call(
        body,
        out_shape=jax.ShapeDtypeStruct((r, c), F32),
        in_specs=[pl.BlockSpec(memory_space=pltpu.VMEM)],
        out_specs=pl.BlockSpec(memory_space=pltpu.VMEM),
        name=name,
    )(stack)


def _mesh_pos():
    return lax.axis_index("x"), lax.axis_index("y"), lax.axis_index("c")


def _all_gather(shards, *, name):
    n = len(shards)

    def body(*refs):
        x_refs, out_refs = refs[:n], refs[n:2 * n]
        send_sems, recv_sems, local_sems = refs[2 * n:]
        x, y, cc = _mesh_pos()
        me, sibling = (x, y, cc), (x, y, 1 - cc)
        chips = [(1 - x, y), (x, 1 - y), (1 - x, 1 - y)]

        def rows(a, px, py, pc):
            return out_refs[a].at[4 * px + 2 * py + pc]

        def copy(a, kk, block, to, src=None):
            return pltpu.make_async_remote_copy(
                src_ref=rows(a, *block) if src is None else src, dst_ref=rows(a, *block),
                send_sem=send_sems.at[7 * a + kk], recv_sem=recv_sems.at[7 * a + kk],
                device_id=to, device_id_type=MESH)

        sends, own = [], []
        for a in range(n):
            own.append(pltpu.make_async_copy(x_refs[a], rows(a, *me), local_sems.at[a]))
            own[a].start()
            first = [copy(a, 0, me, sibling, src=x_refs[a])]
            first += [copy(a, 1 + j, me, (*chip, cc), src=x_refs[a]) for j, chip in enumerate(chips)]
            for cp in first:
                cp.start()
            sends += first
        for a in range(n):
            for j, chip in enumerate(chips):
                copy(a, 1 + j, (*chip, cc), me).wait_recv()
                passed = copy(a, 4 + j, (*chip, cc), sibling)
                passed.start()
                sends.append(passed)
        for a in range(n):
            copy(a, 0, sibling, me).wait_recv()
            for j, chip in enumerate(chips):
                copy(a, 4 + j, (*chip, 1 - cc), me).wait_recv()
        for cp in sends:
            cp.wait_send()
        for cp in own:
            cp.wait()

    return pl.pallas_call(
        body,
        out_shape=tuple(jax.ShapeDtypeStruct((N_DEV,) + t.shape, t.dtype) for t in shards),
        in_specs=[ANY] * n,
        out_specs=tuple([ANY] * n),
        scratch_shapes=[pltpu.SemaphoreType.DMA((7 * n,)), pltpu.SemaphoreType.DMA((7 * n,)),
                        pltpu.SemaphoreType.DMA((n,))],
        name=name,
    )(*shards)


def _plan_own_blocks(n):
    def plan(refs, send_sems, recv_sems, outgoing):
        x, y, cc = _mesh_pos()
        peers = [(x, y, 1 - cc), (1 - x, y, cc), (x, 1 - y, cc), (1 - x, 1 - y, cc)]
        copies = []
        for a in range(n):
            land = refs[n + a]
            for kk, (px, py, pc) in enumerate(peers):
                block = (x, y, cc) if outgoing else (px, py, pc)
                rows = land.at[4 * block[0] + 2 * block[1] + block[2]]
                copies.append(pltpu.make_async_remote_copy(
                    src_ref=refs[a] if outgoing else rows, dst_ref=rows, send_sem=send_sems.at[4 * a + kk],
                    recv_sem=recv_sems.at[4 * a + kk], device_id=(px, py, pc), device_id_type=MESH))
        return copies

    plan.n_sems = 4 * n
    return plan


def _plan_pass_on(n):
    def plan(refs, send_sems, recv_sems, outgoing):
        x, y, cc = _mesh_pos()
        copies = []
        for a in range(n):
            for j, (px, py) in enumerate([(1 - x, y), (x, 1 - y), (1 - x, 1 - y)]):
                rows = refs[a].at[4 * px + 2 * py + (cc if outgoing else 1 - cc)]
                copies.append(pltpu.make_async_remote_copy(
                    src_ref=rows, dst_ref=rows, send_sem=send_sems.at[3 * a + j], recv_sem=recv_sems.at[3 * a + j],
                    device_id=(x, y, 1 - cc), device_id_type=MESH))
        return copies

    plan.n_sems = 3 * n
    return plan


def _plan_to_sibling(n):
    def plan(refs, send_sems, recv_sems, outgoing):
        x, y, cc = _mesh_pos()
        copies = []
        for a in range(n):
            for chip in range(4):
                dst = refs[n + a].at[chip]
                copies.append(pltpu.make_async_remote_copy(
                    src_ref=refs[a].at[2 * chip + (1 - cc)] if outgoing else dst, dst_ref=dst,
                    send_sem=send_sems.at[4 * a + chip], recv_sem=recv_sems.at[4 * a + chip],
                    device_id=(x, y, 1 - cc), device_id_type=MESH))
        return copies

    plan.n_sems = 4 * n
    return plan


def _plan_to_chips(n):
    def plan(refs, send_sems, recv_sems, outgoing):
        x, y, cc = _mesh_pos()
        copies = []
        for a in range(n):
            for j, (px, py) in enumerate([(1 - x, y), (x, 1 - y), (1 - x, 1 - y)]):
                dst = refs[n + a].at[j]
                copies.append(pltpu.make_async_remote_copy(
                    src_ref=refs[a].at[2 * px + py] if outgoing else dst, dst_ref=dst,
                    send_sem=send_sems.at[3 * a + j], recv_sem=recv_sems.at[3 * a + j],
                    device_id=(px, py, cc), device_id_type=MESH))
        return copies

    plan.n_sems = 3 * n
    return plan


def _in_hbm(t):
    return pltpu.with_memory_space_constraint(t, pltpu.HBM)


def _exchange_start(plan, bufs, after, *, name):
    nb, na = len(bufs), len(after)

    def body(*refs):
        outs = refs[nb + na:]
        for cp in plan(refs[:nb], outs[0], outs[1], True):
            cp.start()
        outs[2 + nb][...] = jnp.zeros_like(outs[2 + nb])

    res = pl.pallas_call(
        body,
        out_shape=(pltpu.SemaphoreType.DMA((plan.n_sems,)), pltpu.SemaphoreType.DMA((plan.n_sems,)),
                   *[pltpu.HBM(t.shape, t.dtype) for t in bufs], jax.ShapeDtypeStruct((8, LANES), F32)),
        in_specs=[HBM_SPEC] * nb + [ANY] * na,
        out_specs=(SEM_SPEC, SEM_SPEC, *[HBM_SPEC] * nb, pl.BlockSpec(memory_space=pltpu.VMEM)),
        input_output_aliases={i: 2 + i for i in range(nb)},
        compiler_params=pltpu.CompilerParams(has_side_effects=DATAFLOW_EFFECT),
        name=name,
    )(*[_in_hbm(t) for t in bufs], *after)
    return plan, res[:2], list(res[2:2 + nb]), res[2 + nb]


def _exchange_wait(flight, after, *, name):
    plan, sems, bufs, _ = flight
    nb = len(bufs)

    def body(*refs):
        send_sems, recv_sems = refs[nb], refs[nb + 1]
        for cp in plan(refs[:nb], send_sems, recv_sems, False):
            cp.wait_recv()
        for cp in plan(refs[:nb], send_sems, recv_sems, True):
            cp.wait_send()

    res = pl.pallas_call(
        body,
        out_shape=tuple(pltpu.HBM(t.shape, t.dtype) for t in bufs),
        in_specs=[HBM_SPEC] * nb + [SEM_SPEC, SEM_SPEC] + [ANY] * len(after),
        out_specs=tuple([HBM_SPEC] * nb),
        input_output_aliases={i: i for i in range(nb)},
        compiler_params=pltpu.CompilerParams(has_side_effects=DATAFLOW_EFFECT),
        name=name,
    )(*bufs, *sems, *after)
    return list(res)


_W_GROUPS = {"even": ("even_w_in", "even_w_out"), "mlp0": ("mlp_w1_0", "mlp_w2_0"),
             "odd": ("odd_w_in", "odd_w_uq", "odd_w_ukv", "odd_w_out"), "mlp1": ("mlp_w1_1", "mlp_w2_1")}


class _Exchanges:
    def __init__(self, shards, gains, n_q, n_kv):
        self.n_q, self.n_kv = n_q, n_kv
        self.mx, self.my, self.mc = _mesh_pos()
        self.dev = 4 * self.mx + 2 * self.my + self.mc
        self.shards = shards
        self.gains = gains
        self.flights, self.gathered, self.grad_blocks, self.grad_names, self.reduced = {}, {}, {}, {}, {}

    def start(self):
        names = _W_GROUPS["even"]
        got = _all_gather([self.shards[n] for n in names] + [self.gains], name="comm_even_gather")
        self.gathered.update(zip(names, got[:-1]))
        self.all_gains = got[-1][:, 0]
        return self._w_begin("mlp0", [got[0]])

    def _w_begin(self, group, after):
        srcs = [self.shards[n] for n in _W_GROUPS[group]]
        lands = [lax.dynamic_update_slice(lax.empty((N_DEV,) + t.shape, t.dtype), t[None], (self.dev, 0, 0)) for t in srcs]
        self.flights[group] = _exchange_start(_plan_own_blocks(len(srcs)), srcs + lands, after, name=f"comm_{group}_own_start")
        return [self.flights[group][3]]

    def _w_turn(self, group, after):
        bufs = _exchange_wait(self.flights[group], after, name=f"comm_{group}_own_wait")
        n = len(bufs) // 2
        self.flights[group] = _exchange_start(_plan_pass_on(n), bufs[n:], [], name=f"comm_{group}_pass_start")
        return [self.flights[group][3]]

    def _w_end(self, group, after):
        self.gathered.update(zip(_W_GROUPS[group], _exchange_wait(self.flights.pop(group), after, name=f"comm_{group}_pass_wait")))

    def weights(self, group):
        return {n: self.gathered[n] for n in _W_GROUPS[group]}

    def norm_gains(self):
        return (self.all_gains[:, :self.n_q].reshape(-1), self.all_gains[:, self.n_q:self.n_q + self.n_kv].reshape(-1))

    def grads(self, group, named_blocks):
        self.grad_names[group] = [n for n, _ in named_blocks]
        self.grad_blocks[group] = [t for _, t in named_blocks]

    def _g_begin(self, group, after):
        blocks = self.grad_blocks[group]
        lands = [lax.empty((4,) + t.shape[1:], t.dtype) for t in blocks]
        self.flights[group] = _exchange_start(_plan_to_sibling(len(blocks)), blocks + lands, after, name=f"comm_{group}_sib_start")
        return [self.flights[group][3]]

    def _g_turn(self, group, after):
        bufs = _exchange_wait(self.flights[group], after, name=f"comm_{group}_sib_wait")
        n = len(bufs) // 2
        partial = [_pair_sum(a, b, self.mc, name=f"pair_sum_{nm}") for nm, a, b in zip(self.grad_names[group], bufs[:n], bufs[n:])]
        lands = [lax.empty((3,) + t.shape[1:], t.dtype) for t in partial]
        self.flights[group] = _exchange_start(_plan_to_chips(n), partial + lands, [], name=f"comm_{group}_chips_start")
        return [self.flights[group][3]]

    def _g_end(self, group, after):
        bufs = _exchange_wait(self.flights.pop(group), after, name=f"comm_{group}_chips_wait")
        n = len(bufs) // 2
        for nm, a, b in zip(self.grad_names[group], bufs[:n], bufs[n:]):
            self.reduced[nm] = _final_sum(a, b, 2 * self.mx + self.my, name=f"final_sum_{nm}")

    _SCHEDULE = {
        "even_out": (("w_turn", "mlp0"), ("w_begin", "odd")),
        "ln1_l0": (("w_end", "mlp0"),),
        "ln2_l0": (("w_turn", "odd"), ("w_begin", "mlp1"), ("w_end", "odd")),
        "odd_out": (("w_turn", "mlp1"),),
        "ln1_l1": (("w_end", "mlp1"),),
        "dw_l1": (("g_begin", "mlp1"),),
        "ln1_bwd_l1": (("g_turn", "mlp1"),),
        "odd_in_dw": (("g_begin", "odd"),),
        "mlp2_dx_l0": (("g_end", "mlp1"), ("g_turn", "odd")),
        "dw_l0": (("g_begin", "mlp0"),),
        "ln1_bwd_l0": (("g_end", "odd"), ("g_turn", "mlp0")),
        "even_in_dw": (("g_begin", "even"),),
        "even_in_dx": (("g_end", "mlp0"), ("g_turn", "even")),
        "finish": (("g_end", "even"),),
    }

    def sync(self, tag, after):
        latest, started = list(after), []
        for what, group in self._SCHEDULE[tag]:
            out = getattr(self, "_" + what)(group, latest)
            if out:
                latest = started = out
        return started


def _alibi(n):
    return 2.0 ** (-8.0 * np.arange(1, n + 1, dtype=np.float32) / n)


def _heads(t, n):
    s = t.shape[0]
    return t.reshape(s, n, t.shape[1] // n).transpose(1, 0, 2)


def _unheads(t):
    n, s, dh = t.shape
    return t.transpose(1, 0, 2).reshape(s, n * dh)


def _to_strided(t, d):
    h, s, x = t.shape
    return t.reshape(h, s // d, d, x).transpose(0, 2, 1, 3).reshape(h, s, x)


def _from_strided(t, d):
    h, s, x = t.shape
    return t.reshape(h, d, s // d, x).transpose(0, 2, 1, 3).reshape(h, s, x)


def _drop_block_pad(t, c):
    s = t.shape[0]
    return t.reshape(s, N_DEV, -1)[:, :, :c].reshape(s, N_DEV * c)


def _add_block_pad(t, cp):
    s, n = t.shape
    c = n // N_DEV
    return jnp.pad(t.reshape(s, N_DEV, c), ((0, 0), (0, 0), (0, cp - c))).reshape(s, N_DEV * cp)


def _stack_rows(t, nq):
    return t.reshape(nq, BLK, A_KV_HEADS, A_GROUP, HEAD_DIM).transpose(2, 0, 3, 1, 4).reshape(
        A_KV_HEADS, 1, nq * A_GROUP * BLK, HEAD_DIM)


def _unstack_rows(t, nq):
    return t.reshape(A_KV_HEADS, nq, A_GROUP, BLK, HEAD_DIM).transpose(1, 3, 0, 2, 4).reshape(
        nq * BLK, A_Q_W)


def _lead_block(t):
    return jnp.pad(t, ((0, 0), (BLK, 0), (0, 0)))


def _columns(t):
    return t.transpose(1, 0, 2).reshape(t.shape[1], -1)


def _rows(t):
    return t.reshape(-1, t.shape[2])


def _by_column_block(t):
    return t.reshape(t.shape[0], N_DEV, -1).transpose(1, 0, 2)


def _local_step(x0, target, comm, sinks, ln1_g, ln1_b, ln2_g, ln2_b):
    s, d = x0.shape
    scale_h = 1.0 / math.sqrt(HEAD_DIM)
    scale_d = 1.0 / math.sqrt(D_NOPE + D_ROPE)
    nq = s // BLK
    even_c, odd_c = EVEN_IN // N_DEV, ODD_IN // N_DEV
    bf = lambda t: t.astype(MXU_DTYPE)
    blocks = dict(b_blocks=True)

    tok = comm.start()
    w_even = comm.weights("even")
    w_even_in, w_even_out = w_even["even_w_in"], _columns(w_even["even_w_out"])
    even_cp = w_even_in.shape[2]
    x0b = bf(x0)
    h_e = _drop_block_pad(_mm(x0b, w_even_in, out_dtypes=(MXU_DTYPE,), name="even_in_fwd", deps=tok, **blocks), even_c)
    qa = _stack_rows(h_e[:, :A_Q_W], nq)
    ka = _lead_block(_heads(h_e[:, A_Q_W:A_Q_W + A_KV_W], A_KV_HEADS))[:, None]
    va = _lead_block(_heads(h_e[:, A_Q_W + A_KV_W:A_Q_W + 2 * A_KV_W], A_KV_HEADS))[:, None]
    rows_a = A_GROUP * BLK
    slope_a = jnp.asarray(np.repeat(_alibi(A_Q_HEADS).reshape(A_KV_HEADS, A_GROUP), BLK, axis=1).reshape(
        A_KV_HEADS, 1, rows_a, 1))
    sink_a = jnp.broadcast_to(sinks.reshape(A_KV_HEADS, A_GROUP, 1), (A_KV_HEADS, A_GROUP, BLK)).reshape(
        A_KV_HEADS, 1, rows_a, 1)
    a_cfg = dict(scale=scale_h, n_back=A_WINDOW - 1, bps=nq)
    oa, lse_a = _band_fwd(qa, ka, va, slope_a, sink_a, name="swa_fwd", **a_cfg)
    b_in, b_cfg, b_out, b_lse = [], [], [], []
    base = A_Q_W + 2 * A_KV_W
    for gi, (window, dil) in enumerate(B_PATTERNS):
        blk = h_e[:, base + gi * 3 * B_W: base + (gi + 1) * 3 * B_W].reshape(s, 3, B_HEADS, HEAD_DIM)
        qs, ks, vs = (_to_strided(blk[:, n].transpose(1, 0, 2), dil) for n in range(3))
        slope = jnp.asarray(np.broadcast_to((_alibi(B_HEADS) * dil).reshape(1, B_HEADS, 1, 1), (1, B_HEADS, BLK, 1)))
        ins = (qs[None], _lead_block(ks)[None], _lead_block(vs)[None], slope)
        cfg = dict(scale=scale_h, n_back=window // dil, bps=nq // dil)
        o, lse = _band_fwd(*ins, None, name=f"dil{gi}_fwd", **cfg)
        b_in.append(ins)
        b_cfg.append(cfg)
        b_out.append(_from_strided(o[0], dil))
        b_lse.append(_from_strided(lse[0], dil))
    ob, lse_b = _merge(b_out, b_lse, name="dil_merge")
    y_e = bf(jnp.concatenate([_unstack_rows(oa, nq), _unheads(ob)], axis=1))
    mixed = _mm(y_e, w_even_out, name="even_out_fwd")
    tok = comm.sync("even_out", [mixed])
    x0n, x0nb, xh1_0, r1_0 = _ln_fwd(x0, mixed, ln1_g[0], ln1_b[0], name="ln1_fwd_l0", deps=tok)
    comm.sync("ln1_l0", [x0nb])
    w_mlp0 = comm.weights("mlp0")
    w1_0, w2_0 = w_mlp0["mlp_w1_0"], _rows(w_mlp0["mlp_w2_0"])
    act0, hid0 = _mm(x0nb, w1_0, out_dtypes=(MXU_DTYPE, MXU_DTYPE), epilogue=_relu_sq, name="mlp1_fwd_l0", **blocks)
    mlp = _mm(hid0, w2_0, name="mlp2_fwd_l0")
    x1, x1b, xh2_0, r2_0 = _ln_fwd(x0n, mlp, ln2_g[0], ln2_b[0], name="ln2_fwd_l0")

    tok = comm.sync("ln2_l0", [x1b])
    w_odd = comm.weights("odd")
    w_odd_in, w_uq, w_ukv, w_odd_out = (w_odd["odd_w_in"], _columns(w_odd["odd_w_uq"]), _columns(w_odd["odd_w_ukv"]),
                                        _rows(w_odd["odd_w_out"]))
    odd_cp = w_odd_in.shape[2]
    gq, gkv = comm.norm_gains()
    h_o = _drop_block_pad(_mm(x1b, w_odd_in, name="odd_in_fwd", deps=tok, **blocks), odd_c)
    qc, kc, vc = (bf(_heads(h_o[:, n * C_W:(n + 1) * C_W], C_HEADS)) for n in range(3))
    oc, sb_total = _sb_fwd(qc, kc, vc, scale=scale_h, name="sb_fwd")
    o_cq, o_ckv, o_kr = 3 * C_W, 3 * C_W + D_Q_RANK, 3 * C_W + D_Q_RANK + D_KV_RANK
    cq, ckv, kr = h_o[:, o_cq:o_ckv], h_o[:, o_ckv:o_kr], h_o[:, o_kr:o_kr + D_ROPE]
    ncq, rq = _rms_fwd(cq, gq, name="rms_q_fwd")
    nckv, rkv = _rms_fwd(ckv, gkv, name="rms_kv_fwd")
    qd = _mm(ncq, w_uq, name="uq_fwd")
    kvd = _mm(nckv, w_ukv, name="ukv_fwd").reshape(s, D_HEADS, D_NOPE + D_V)
    lane_pad = LANES - D_NOPE - D_ROPE
    q_h = jnp.pad(qd.reshape(s, D_HEADS, D_NOPE + D_ROPE), ((0, 0), (0, 0), (0, lane_pad))).transpose(1, 0, 2)
    k_h = jnp.concatenate([kvd[:, :, :D_NOPE], jnp.broadcast_to(kr[:, None, :], (s, D_HEADS, D_ROPE)),
                           jnp.zeros((s, D_HEADS, lane_pad), F32)], axis=2).transpose(1, 0, 2)
    v_h = bf(kvd[:, :, D_NOPE:].transpose(1, 0, 2))
    rope_t = _rope_tables(s, inverse=False)
    q_r = _rope(q_h, rope_t, out_dtype=MXU_DTYPE, name="rope_q_fwd")
    k_r = _rope(k_h, rope_t, out_dtype=MXU_DTYPE, name="rope_k_fwd")
    od, lse_d = _causal_fwd(q_r, k_r, v_h, scale=scale_d, name="mla_fwd")
    y_o = bf(jnp.concatenate([_unheads(oc), _unheads(od)], axis=1))
    mixed = _mm(y_o, w_odd_out, name="odd_out_fwd")
    tok = comm.sync("odd_out", [mixed])
    x1n, x1nb, xh1_1, r1_1 = _ln_fwd(x1, mixed, ln1_g[1], ln1_b[1], name="ln1_fwd_l1", deps=tok)
    comm.sync("ln1_l1", [x1nb])
    w_mlp1 = comm.weights("mlp1")
    w1_1, w2_1 = w_mlp1["mlp_w1_1"], _rows(w_mlp1["mlp_w2_1"])
    act1, hid1 = _mm(x1nb, w1_1, out_dtypes=(MXU_DTYPE, MXU_DTYPE), epilogue=_relu_sq, name="mlp1_fwd_l1", **blocks)
    mlp = _mm(hid1, w2_1, name="mlp2_fwd_l1")
    y, _, xh2_1, r2_1 = _ln_fwd(x1n, mlp, ln2_g[1], ln2_b[1], name="ln2_fwd_l1")
    dy, loss_vec = _loss_head(y, target, name="loss_head")

    def mlp_block_bwd(g_out, layer, w1, w2, xh2, r2, xh1, r1, act, hid, xnb):
        du2, du2b, dg2, db2 = _ln_bwd(g_out, xh2, r2, ln2_g[layer], name=f"ln2_bwd_l{layer}")
        dpre = _mm(du2b, w2, nt=True, out_dtypes=(MXU_DTYPE,), epilogue=_relu_sq_grad, extra=act, name=f"mlp2_dx_l{layer}")
        tok = comm.sync("mlp2_dx_l0", [dpre]) if layer == 0 else []
        dw2 = _mm(hid.T, du2b, out_dtypes=(BF16,), name=f"mlp2_dw_l{layer}", deps=tok)
        dw1 = _mm(xnb.T, dpre, out_blocks=True, out_dtypes=(BF16,), name=f"mlp1_dw_l{layer}")
        comm.grads(f"mlp{layer}", [(f"mlp_w1_{layer}", dw1), (f"mlp_w2_{layer}", dw2.reshape(N_DEV, -1, d))])
        tok = comm.sync(f"dw_l{layer}", [dw1])
        dxn = _mm(dpre, w1, nt=True, epilogue=_add_alpha, extra=du2, name=f"mlp1_dx_l{layer}", deps=tok, **blocks)
        du1, du1b, dg1, db1 = _ln_bwd(dxn, xh1, r1, ln1_g[layer], name=f"ln1_bwd_l{layer}")
        return du1, du1b, comm.sync(f"ln1_bwd_l{layer}", [du1b]), (dg1, db1, dg2, db2)

    du1, du1b, tok, ln_1 = mlp_block_bwd(dy, 1, w1_1, w2_1, xh2_1, r2_1, xh1_1, r1_1, act1, hid1, x1nb)
    d_odd_out = _mm(y_o.T, du1b, out_dtypes=(BF16,), name="odd_out_dw", deps=tok).reshape(N_DEV, -1, d)
    dy_o = _mm(du1b, w_odd_out, nt=True, name="odd_out_dx")
    doc, dod = _heads(dy_o[:, :C_W], C_HEADS), _heads(dy_o[:, C_W:], D_HEADS)
    dqc, dkc, dvc = _sb_bwd(qc, kc, vc, doc, sb_total, scale=scale_h, name="sb_bwd")
    dq_r, dk_r, dv_h = _causal_bwd(q_r, k_r, v_h, dod, od, lse_d, scale=scale_d, name="mla_bwd")
    rope_inv = _rope_tables(s, inverse=True)
    dq_h = _rope(dq_r, rope_inv, out_dtype=F32, name="rope_q_bwd")
    dk_h, dkr_sum = _rope(dk_r, rope_inv, out_dtype=F32, head_sum=True, name="rope_k_bwd")
    dqd = bf(dq_h[:, :, :D_NOPE + D_ROPE].transpose(1, 0, 2).reshape(s, D_HEADS * (D_NOPE + D_ROPE)))
    dkvd = bf(jnp.concatenate([dk_h[:, :, :D_NOPE], dv_h], axis=2).transpose(1, 0, 2).reshape(s, D_HEADS * (D_NOPE + D_V)))
    d_uq = _mm(ncq.T, dqd, out_dtypes=(BF16,), name="uq_dw")
    dncq = _mm(dqd, w_uq, nt=True, name="uq_dx")
    d_ukv = _mm(nckv.T, dkvd, out_dtypes=(BF16,), name="ukv_dw")
    dnckv = _mm(dkvd, w_ukv, nt=True, name="ukv_dx")
    dcq, dgq = _rms_bwd(dncq, cq, rq, gq, name="rms_q_bwd")
    dckv, dgkv = _rms_bwd(dnckv, ckv, rkv, gkv, name="rms_kv_bwd")
    dh_o = _add_block_pad(bf(jnp.concatenate(
        [_unheads(dqc), _unheads(dkc), _unheads(dvc), dcq, dckv, dkr_sum[:, D_NOPE:D_NOPE + D_ROPE]], axis=1)), odd_cp)
    d_odd_in = _mm(x1b.T, dh_o, out_blocks=True, out_dtypes=(BF16,), name="odd_in_dw")
    comm.grads("odd", [("odd_w_in", d_odd_in), ("odd_w_uq", _by_column_block(d_uq)),
                       ("odd_w_ukv", _by_column_block(d_ukv)), ("odd_w_out", d_odd_out)])
    tok = comm.sync("odd_in_dw", [d_odd_in])
    dx1 = _mm(dh_o, w_odd_in, nt=True, epilogue=_add_alpha, extra=du1, name="odd_in_dx", deps=tok, **blocks)

    du1, du1b, tok, ln_0 = mlp_block_bwd(dx1, 0, w1_0, w2_0, xh2_0, r2_0, xh1_0, r1_0, act0, hid0, x0nb)
    d_even_out = _mm(y_e.T, du1b, out_dtypes=(BF16,), name="even_out_dw", deps=tok)
    dy_e = _mm(du1b, w_even_out, nt=True, name="even_out_dx")
    doa, dob = _stack_rows(dy_e[:, :A_Q_W], nq), _heads(dy_e[:, A_Q_W:], B_HEADS)
    dqa, dka, dva, dsink = _band_bwd(qa, ka, va, doa, oa, lse_a, slope_a, sink_a, name="swa_bwd", **a_cfg)
    pieces = [_unstack_rows(dqa, nq), _unheads(dka[:, 0, BLK:]), _unheads(dva[:, 0, BLK:])]
    for gi, (_, dil) in enumerate(B_PATTERNS):
        qs, ks, vs, slope = b_in[gi]
        dq, dk, dv = _band_bwd(qs, ks, vs, _to_strided(dob, dil)[None], _to_strided(ob, dil)[None],
                               _to_strided(lse_b, dil)[None], slope, None, name=f"dil{gi}_bwd", **b_cfg[gi])
        pieces += [_unheads(_from_strided(t, dil)) for t in (dq[0], dk[0, :, BLK:], dv[0, :, BLK:])]
    dh_e = _add_block_pad(bf(jnp.concatenate(pieces, axis=1)), even_cp)
    d_even_in = _mm(x0b.T, dh_e, out_blocks=True, out_dtypes=(BF16,), name="even_in_dw")
    comm.grads("even", [("even_w_in", d_even_in), ("even_w_out", _by_column_block(d_even_out))])
    tok = comm.sync("even_in_dw", [d_even_in])
    grad_x = _mm(dh_e, w_even_in, nt=True, epilogue=_add_alpha, extra=du1, name="even_in_dx", deps=tok, **blocks)
    tok = comm.sync("even_in_dx", [grad_x])

    ln = [jnp.concatenate([a, b], axis=0) for a, b in zip(ln_0, ln_1)]
    small = {"ln": ln, "sinks": dsink[:, :, 0].reshape(-1), "gq": dgq[0], "gkv": dgkv[0], "loss": loss_vec[0, :1]}
    return grad_x, small, tok


def kernel(x, even_w_in, even_sinks, even_w_out, odd_w_in, odd_q_norm_g, odd_kv_norm_g, odd_w_uq, odd_w_ukv, odd_w_out, ln1_g, ln1_b, mlp_w1, mlp_w2, ln2_g, ln2_b, loss_target, m_even_w_in, m_even_sinks, m_even_w_out, m_odd_w_in, m_odd_q_norm_g, m_odd_kv_norm_g, m_odd_w_uq, m_odd_w_ukv, m_odd_w_out, m_ln1_g, m_ln1_b, m_mlp_w1, m_mlp_w2, m_ln2_g, m_ln2_b, v_even_w_in, v_even_sinks, v_even_w_out, v_odd_w_in, v_odd_q_norm_g, v_odd_kv_norm_g, v_odd_w_uq, v_odd_w_ukv, v_odd_w_out, v_ln1_g, v_ln1_b, v_mlp_w1, v_mlp_w2, v_ln2_g, v_ln2_b):
    weights = dict(even_w_in=even_w_in, even_sinks=even_sinks, even_w_out=even_w_out, odd_w_in=odd_w_in,
                   odd_q_norm_g=odd_q_norm_g, odd_kv_norm_g=odd_kv_norm_g, odd_w_uq=odd_w_uq, odd_w_ukv=odd_w_ukv,
                   odd_w_out=odd_w_out, ln1_g=ln1_g, ln1_b=ln1_b, mlp_w1=mlp_w1, mlp_w2=mlp_w2, ln2_g=ln2_g, ln2_b=ln2_b)
    mom_m = dict(even_w_in=m_even_w_in, even_sinks=m_even_sinks, even_w_out=m_even_w_out, odd_w_in=m_odd_w_in,
                 odd_q_norm_g=m_odd_q_norm_g, odd_kv_norm_g=m_odd_kv_norm_g, odd_w_uq=m_odd_w_uq, odd_w_ukv=m_odd_w_ukv,
                 odd_w_out=m_odd_w_out, ln1_g=m_ln1_g, ln1_b=m_ln1_b, mlp_w1=m_mlp_w1, mlp_w2=m_mlp_w2, ln2_g=m_ln2_g, ln2_b=m_ln2_b)
    mom_v = dict(even_w_in=v_even_w_in, even_sinks=v_even_sinks, even_w_out=v_even_w_out, odd_w_in=v_odd_w_in,
                 odd_q_norm_g=v_odd_q_norm_g, odd_kv_norm_g=v_odd_kv_norm_g, odd_w_uq=v_odd_w_uq, odd_w_ukv=v_odd_w_ukv,
                 odd_w_out=v_odd_w_out, ln1_g=v_ln1_g, ln1_b=v_ln1_b, mlp_w1=v_mlp_w1, mlp_w2=v_mlp_w2, ln2_g=v_ln2_g, ln2_b=v_ln2_b)
    order = list(weights)
    n_q, n_kv = odd_q_norm_g.shape[1], odd_kv_norm_g.shape[1]

    def lane_padded(t):
        return jnp.pad(t, ((0, 0), (0, _lane_pad(t.shape[1]) - t.shape[1]))).astype(BF16)

    shards = {"even_w_in": lane_padded(even_w_in[0]), "even_w_out": even_w_out[0].astype(BF16),
              "mlp_w1_0": mlp_w1[0].astype(BF16), "mlp_w2_0": mlp_w2[0].astype(BF16),
              "odd_w_in": lane_padded(odd_w_in[0]), "odd_w_uq": odd_w_uq[0].astype(BF16),
              "odd_w_ukv": odd_w_ukv[0].astype(BF16), "odd_w_out": odd_w_out[0].astype(BF16),
              "mlp_w1_1": mlp_w1[1].astype(BF16), "mlp_w2_1": mlp_w2[1].astype(BF16)}
    gains = jnp.concatenate([odd_q_norm_g, odd_kv_norm_g, jnp.zeros((1, LANES - n_q - n_kv), F32)], axis=1)
    comm = _Exchanges(shards, gains, n_q, n_kv)
    dev = comm.dev

    grad_x, small, last_started = _local_step(x[0], loss_target[0], comm, even_sinks[0], ln1_g, ln1_b, ln2_g, ln2_b)

    small_parts = [t.reshape(-1) for t in small["ln"]] + [small["sinks"], small["gq"], small["gkv"], small["loss"]]
    small_sizes = [p.shape[0] for p in small_parts]
    n_small = sum(small_sizes)
    small_rows = -(-n_small // (8 * LANES)) * 8
    small_flat = jnp.concatenate(small_parts + [jnp.zeros((small_rows * LANES - n_small,), F32)]).reshape(small_rows, LANES)
    (small_all,) = _all_gather([small_flat], name="comm_small_gather")
    totals = _sum_devices(small_all, name="small_sum").reshape(-1)
    tot, off = [], 0
    for size in small_sizes:
        tot.append(totals[off:off + size])
        off += size
    grads = {}
    for i, n in enumerate(("ln1_g", "ln1_b", "ln2_g", "ln2_b")):
        grads[n] = tot[i].reshape(weights[n].shape)
    grads["even_sinks"] = tot[4].reshape(even_sinks.shape)
    grads["odd_q_norm_g"] = lax.dynamic_slice(tot[5], (dev * n_q,), (n_q,)).reshape(odd_q_norm_g.shape)
    grads["odd_kv_norm_g"] = lax.dynamic_slice(tot[6], (dev * n_kv,), (n_kv,)).reshape(odd_kv_norm_g.shape)
    loss = tot[7][0]

    delta, new_m, new_v = {}, {}, {}

    def update(n):
        g_list = [comm.reduced[f"{n}_0"], comm.reduced[f"{n}_1"]] if n.startswith("mlp") else [comm.reduced[n]]
        grads[n], delta[n], new_m[n], new_v[n] = _adamw(weights[n], g_list, mom_m[n], mom_v[n], name=f"adamw_{n}",
                                                        deps=last_started)

    early = ("mlp_w1", "mlp_w2", "odd_w_in", "odd_w_uq", "odd_w_ukv", "odd_w_out")
    for n in early:
        update(n)
    small_names = [n for n in order if n not in ("even_w_in", "even_w_out", "odd_w_in", "odd_w_uq", "odd_w_ukv", "odd_w_out", "mlp_w1", "mlp_w2")]
    n_sm = sum(weights[n].size for n in small_names)
    sm_rows = -(-n_sm // (8 * LANES)) * 8

    def pack_small(group):
        flat = [group[n].reshape(-1) for n in small_names]
        return jnp.concatenate(flat + [jnp.zeros((sm_rows * LANES - n_sm,), F32)]).reshape(1, sm_rows, LANES)

    res = _adamw(pack_small(weights), [pack_small(grads)[0]], pack_small(mom_m), pack_small(mom_v), name="adamw_small",
                 deps=last_started)
    off = 0
    for n in small_names:
        size = weights[n].size
        delta[n], new_m[n], new_v[n] = (t.reshape(-1)[off:off + size].reshape(weights[n].shape) for t in res[1:])
        off += size
    comm.sync("finish", [new_v[n] for n in early] + [res[0]])
    update("even_w_in")
    update("even_w_out")

    return (loss, grad_x[None], *[grads[n] for n in order], *[delta[n] for n in order],
            *[new_m[n] for n in order], *[new_v[n] for n in order])
```

```python
import math

import jax
import jax.numpy as jnp
import numpy as np
from jax import lax
from jax.experimental import pallas as pl
from jax.experimental.pallas import tpu as pltpu

F32 = jnp.float32
BF16 = jnp.bfloat16
MXU_DTYPE = BF16

HEAD_DIM = 64
A_Q_HEADS, A_KV_HEADS, A_WINDOW = 16, 2, 128
A_GROUP = A_Q_HEADS // A_KV_HEADS
B_HEADS = 8
B_PATTERNS = ((128, 1), (512, 4), (2048, 16))
C_HEADS = 16
D_HEADS, D_Q_RANK, D_KV_RANK, D_NOPE, D_ROPE, D_V = 16, 512, 256, 64, 32, 64
ROPE_BASE = 10000.0
LN_EPS, RMS_EPS = 1e-5, 1e-6
DEPTH = 2
ALPHA = (2 * DEPTH) ** 0.25
A_Q_W, A_KV_W, B_W = A_Q_HEADS * HEAD_DIM, A_KV_HEADS * HEAD_DIM, B_HEADS * HEAD_DIM
EVEN_IN = A_Q_W + 2 * A_KV_W + 3 * B_W * len(B_PATTERNS)
C_W = C_HEADS * HEAD_DIM
ODD_IN = 3 * C_W + D_Q_RANK + D_KV_RANK + D_ROPE
ADAM_LR, ADAM_B1, ADAM_B2, ADAM_EPS, ADAM_WD, ADAM_STEP = 0.001, 0.9, 0.999, 1e-08, 0.01, 10

N_DEV = 8
LANES = 128
BLK = 128
CAUSAL_TILE = 512
CUM_CHUNK = 256
NEG = -1e30
VMEM_LIMIT = 48 * 1024 * 1024

NN = ((1,), (0,))
NT = ((1,), (1,))
TN = ((0,), (0,))
MESH = pl.DeviceIdType.MESH
ANY = pl.BlockSpec(memory_space=pl.ANY)
HBM_SPEC = pl.BlockSpec(memory_space=pltpu.HBM)
SEM_SPEC = pl.BlockSpec(memory_space=pltpu.SEMAPHORE)
DATAFLOW_EFFECT = pltpu.SideEffectType.DATAFLOW_SIDE_EFFECTING


def _dot(a, b, dims):
    return lax.dot_general(a, b, (dims, ((), ())), preferred_element_type=F32)


def _bdot(a, b, dims):
    return jnp.stack([_dot(a[n], b[n], dims) for n in range(a.shape[0])])


def _params(*sem):
    return pltpu.CompilerParams(dimension_semantics=tuple(sem), vmem_limit_bytes=VMEM_LIMIT)


def _pick(n, cap, mult=LANES):
    if n <= cap:
        return n
    for t in range(cap - cap % mult, 0, -mult):
        if n % t == 0:
            return t
    raise ValueError(f"no tile for {n}")


def _lane_pad(c):
    return -(-c // LANES) * LANES


def _mm(a, b, *, name, nt=False, b_blocks=False, out_blocks=False, out_dtypes=(F32,), epilogue=None, extra=None, deps=()):
    m, k = a.shape
    if b_blocks:
        nb, kin, c = b.shape
        n = kin if nt else nb * c
        k_full = nb * c if nt else kin
    else:
        n, k_full = (b.shape if nt else b.shape[::-1])
    assert k == k_full, (a.shape, b.shape, nt, b_blocks)
    tm = _pick(m, 1024, 8)
    if b_blocks and not nt:
        tn, tk = c, _pick(k, 3072)
    elif b_blocks:
        per_step = max(g for g in (1, 2, 4, 8) if g * c <= 2048)
        tn, tk = _pick(n, 1024), per_step * c
    elif out_blocks:
        tn, tk = n // N_DEV, _pick(k, 3072)
    else:
        tk = _pick(k, 3072)
        tn = _pick(n, 1024 if k > tk else 512)
    nk = k // tk
    n_out = len(out_dtypes)

    def body(*refs):
        a_ref, b_ref = refs[0], refs[1]
        e_ref = refs[2] if extra is not None else None
        first_out = 2 + (extra is not None) + len(deps)
        out_refs = refs[first_out:first_out + n_out]

        def finish(acc):
            e = None if e_ref is None else e_ref[...]
            outs = (acc,) if epilogue is None else epilogue(acc, e)
            for r, o in zip(out_refs, outs):
                r[...] = o.astype(r.dtype).reshape(r.shape)

        if b_blocks and nt:
            part = _dot(a_ref[:, :c], b_ref[0], NT)
            for blk in range(1, per_step):
                part += _dot(a_ref[:, blk * c:(blk + 1) * c], b_ref[blk], NT)
        else:
            part = _dot(a_ref[...], b_ref[0] if b_blocks else b_ref[...], NT if nt else NN)
        if nk == 1:
            finish(part)
        else:
            acc_ref = refs[first_out + n_out]
            kk = pl.program_id(2)

            @pl.when(kk == 0)
            def _():
                acc_ref[...] = part

            @pl.when(kk > 0)
            def _():
                acc_ref[...] += part

            @pl.when(kk == nk - 1)
            def _():
                finish(acc_ref[...])

    if b_blocks and not nt:
        b_spec = pl.BlockSpec((1, tk, tn), lambda i, j, kk: (j, kk, 0))
    elif b_blocks:
        b_spec = pl.BlockSpec((per_step, tn, c), lambda i, j, kk: (kk, j, 0))
    elif nt:
        b_spec = pl.BlockSpec((tn, tk), lambda i, j, kk: (j, kk))
    else:
        b_spec = pl.BlockSpec((tk, tn), lambda i, j, kk: (kk, j))
    in_specs = [pl.BlockSpec((tm, tk), lambda i, j, kk: (i, kk)), b_spec]
    ins = [a.astype(MXU_DTYPE), b.astype(MXU_DTYPE)]
    if extra is not None:
        in_specs.append(pl.BlockSpec((tm, tn), lambda i, j, kk: (i, j)))
        ins.append(extra)
    in_specs += [ANY] * len(deps)
    ins += list(deps)
    if out_blocks:
        out_shape = tuple(jax.ShapeDtypeStruct((N_DEV, m, tn), d) for d in out_dtypes)
        out_specs = tuple(pl.BlockSpec((1, tm, tn), lambda i, j, kk: (j, i, 0)) for _ in out_dtypes)
    else:
        out_shape = tuple(jax.ShapeDtypeStruct((m, n), d) for d in out_dtypes)
        out_specs = tuple(pl.BlockSpec((tm, tn), lambda i, j, kk: (i, j)) for _ in out_dtypes)
    outs = pl.pallas_call(
        body,
        out_shape=out_shape,
        grid=(m // tm, n // tn, nk),
        in_specs=in_specs,
        out_specs=out_specs,
        scratch_shapes=[pltpu.VMEM((tm, tn), F32)] if nk > 1 else [],
        compiler_params=_params("parallel", "parallel", "arbitrary"),
        name=name,
    )(*ins)
    return outs[0] if n_out == 1 else outs


def _relu_sq(acc, _):
    act = jnp.maximum(acc, 0.0)
    return act, act * act


def _relu_sq_grad(acc, act):
    return (acc * (2.0 * act.astype(F32)),)


def _add_alpha(acc, du):
    return (acc + ALPHA * du,)


def _ln_fwd(x, mixed, g, b, *, name, deps=()):
    s, d = x.shape
    tr = _pick(s, 256, 8)

    def body(x_ref, m_ref, g_ref, b_ref, *rest):
        y_ref, yb_ref, xh_ref, r_ref = rest[len(deps):]
        u = ALPHA * x_ref[...] + m_ref[...]
        mu = jnp.mean(u, axis=-1, keepdims=True)
        xc = u - mu
        var = jnp.mean(xc * xc, axis=-1, keepdims=True)
        r = lax.rsqrt(var + LN_EPS)
        xh = xc * r
        y = xh * g_ref[...] + b_ref[...]
        y_ref[...] = y
        yb_ref[...] = y.astype(MXU_DTYPE)
        xh_ref[...] = xh
        r_ref[...] = r

    row = pl.BlockSpec((tr, d), lambda i: (i, 0))
    vec = pl.BlockSpec((1, d), lambda i: (0, 0))
    return pl.pallas_call(
        body,
        out_shape=(jax.ShapeDtypeStruct((s, d), F32), jax.ShapeDtypeStruct((s, d), MXU_DTYPE),
                   jax.ShapeDtypeStruct((s, d), F32), jax.ShapeDtypeStruct((s, 1), F32)),
        grid=(s // tr,),
        in_specs=[row, row, vec, vec] + [ANY] * len(deps),
        out_specs=(row, row, row, pl.BlockSpec((tr, 1), lambda i: (i, 0))),
        compiler_params=_params("parallel"),
        name=name,
    )(x, mixed, g.reshape(1, d), b.reshape(1, d), *deps)


def _ln_bwd(dy, xh, r, g, *, name):
    s, d = dy.shape
    tr = _pick(s, 256, 8)

    def body(dy_ref, xh_ref, r_ref, g_ref, du_ref, dub_ref, dg_ref, db_ref):
        dyv, xhv = dy_ref[...], xh_ref[...]
        dxh = dyv * g_ref[...]
        c1 = jnp.mean(dxh, axis=-1, keepdims=True)
        c2 = jnp.mean(dxh * xhv, axis=-1, keepdims=True)
        du = r_ref[...] * (dxh - c1 - xhv * c2)
        du_ref[...] = du
        dub_ref[...] = du.astype(MXU_DTYPE)

        @pl.when(pl.program_id(0) == 0)
        def _():
            dg_ref[...] = jnp.zeros_like(dg_ref)
            db_ref[...] = jnp.zeros_like(db_ref)

        dg_ref[...] += jnp.sum(dyv * xhv, axis=0, keepdims=True)
        db_ref[...] += jnp.sum(dyv, axis=0, keepdims=True)

    row = pl.BlockSpec((tr, d), lambda i: (i, 0))
    vec = pl.BlockSpec((1, d), lambda i: (0, 0))
    return pl.pallas_call(
        body,
        out_shape=(jax.ShapeDtypeStruct((s, d), F32), jax.ShapeDtypeStruct((s, d), MXU_DTYPE),
                   jax.ShapeDtypeStruct((1, d), F32), jax.ShapeDtypeStruct((1, d), F32)),
        grid=(s // tr,),
        in_specs=[row, row, pl.BlockSpec((tr, 1), lambda i: (i, 0)), vec],
        out_specs=(row, row, vec, vec),
        compiler_params=_params("arbitrary"),
        name=name,
    )(dy, xh, r, g.reshape(1, d))


def _rms_fwd(x, g, *, name):
    s, d = x.shape
    tr = _pick(s, 512, 8)

    def body(x_ref, g_ref, y_ref, r_ref):
        xv = x_ref[...]
        r = lax.rsqrt(jnp.mean(xv * xv, axis=-1, keepdims=True) + RMS_EPS)
        y_ref[...] = (xv * r * g_ref[...]).astype(y_ref.dtype)
        r_ref[...] = r

    return pl.pallas_call(
        body,
        out_shape=(jax.ShapeDtypeStruct((s, d), MXU_DTYPE), jax.ShapeDtypeStruct((s, 1), F32)),
        grid=(s // tr,),
        in_specs=[pl.BlockSpec((tr, d), lambda i: (i, 0)), pl.BlockSpec((1, d), lambda i: (0, 0))],
        out_specs=(pl.BlockSpec((tr, d), lambda i: (i, 0)), pl.BlockSpec((tr, 1), lambda i: (i, 0))),
        compiler_params=_params("parallel"),
        name=name,
    )(x, g.reshape(1, d))


def _rms_bwd(dy, x, r, g, *, name):
    s, d = x.shape
    tr = _pick(s, 512, 8)

    def body(dy_ref, x_ref, r_ref, g_ref, dx_ref, dg_ref):
        dyv, rv = dy_ref[...], r_ref[...]
        xn = x_ref[...] * rv
        dxn = dyv * g_ref[...]
        dx_ref[...] = rv * (dxn - xn * jnp.mean(dxn * xn, axis=-1, keepdims=True))

        @pl.when(pl.program_id(0) == 0)
        def _():
            dg_ref[...] = jnp.zeros_like(dg_ref)

        dg_ref[...] += jnp.sum(dyv * xn, axis=0, keepdims=True)

    row = pl.BlockSpec((tr, d), lambda i: (i, 0))
    vec = pl.BlockSpec((1, d), lambda i: (0, 0))
    return pl.pallas_call(
        body,
        out_shape=(jax.ShapeDtypeStruct((s, d), F32), jax.ShapeDtypeStruct((1, d), F32)),
        grid=(s // tr,),
        in_specs=[row, row, pl.BlockSpec((tr, 1), lambda i: (i, 0)), vec],
        out_specs=(row, vec),
        compiler_params=_params("arbitrary"),
        name=name,
    )(dy, x, r, g.reshape(1, d))


def _rope_tables(s, inverse):
    inv_freq = ROPE_BASE ** (-jnp.arange(0, D_ROPE, 2, dtype=F32) / D_ROPE)
    ang = jnp.arange(s, dtype=F32)[:, None] * inv_freq[None, :]
    cos, sin = jnp.cos(ang), jnp.sin(ang)
    if inverse:
        sin = -sin
    half = D_ROPE // 2
    one, zero = jnp.ones((s, D_NOPE), F32), jnp.zeros((s, D_NOPE), F32)
    pad1, pad0 = jnp.ones((s, LANES - D_NOPE - D_ROPE), F32), jnp.zeros((s, LANES - D_NOPE - D_ROPE), F32)
    zh = jnp.zeros((s, half), F32)
    c = jnp.concatenate([one, cos, cos, pad1], axis=1)
    s_lo = jnp.concatenate([zero, -sin, zh, pad0], axis=1)
    s_hi = jnp.concatenate([zero, zh, sin, pad0], axis=1)
    return c, s_lo, s_hi


def _rope(x, tables, *, out_dtype, head_sum=False, name):
    h, s, w = x.shape
    ts = _pick(s, 2048, 8)

    def body(x_ref, c_ref, lo_ref, hi_ref, y_ref, *sum_ref):
        y = _rotate(x_ref[0], c_ref[...], lo_ref[...], hi_ref[...])
        y_ref[0] = y.astype(y_ref.dtype)
        if head_sum:
            @pl.when(pl.program_id(1) == 0)
            def _():
                sum_ref[0][...] = jnp.zeros_like(sum_ref[0])

            sum_ref[0][...] += y

    tab = pl.BlockSpec((ts, w), lambda i, hh: (i, 0))
    blk = pl.BlockSpec((1, ts, w), lambda i, hh: (hh, i, 0))
    out_shape = [jax.ShapeDtypeStruct((h, s, w), out_dtype)]
    out_specs = [blk]
    if head_sum:
        out_shape.append(jax.ShapeDtypeStruct((s, w), F32))
        out_specs.append(tab)
    res = pl.pallas_call(
        body,
        out_shape=tuple(out_shape),
        grid=(s // ts, h),
        in_specs=[blk, tab, tab, tab],
        out_specs=tuple(out_specs),
        compiler_params=_params("parallel", "arbitrary"),
        name=name,
    )(x, *tables)
    return res if head_sum else res[0]


def _band_scores(q, kw, slope, i, *, scale, n_back, bps):
    b, r, _ = q.shape
    sc = _bdot(q, kw, NT) * scale
    shape = (b, r, 2 * BLK)
    row = lax.broadcasted_iota(jnp.int32, shape, 1) & (BLK - 1)
    col = lax.broadcasted_iota(jnp.int32, shape, 2)
    rel = BLK + row - col
    first_col = jnp.where(i % bps == 0, BLK, 0)
    valid = (rel >= 0) & (rel <= n_back) & (col >= first_col)
    return jnp.where(valid, sc - slope * rel.astype(F32), NEG)


def _band_fwd(q, k, v, slope, sink, *, scale, n_back, bps, name):
    g, b, rows, dh = q.shape
    r = slope.shape[2]
    nq = rows // r
    skv = k.shape[2]
    use_sink = sink is not None

    def body(*refs):
        q_ref, k_ref, v_ref, slope_ref = refs[:4]
        sink_ref = refs[4] if use_sink else None
        o_ref, lse_ref = refs[4 + use_sink:]
        i = pl.program_id(1)
        off = pl.multiple_of(i * BLK, BLK)
        kw = k_ref[0, :, pl.ds(off, 2 * BLK), :]
        vw = v_ref[0, :, pl.ds(off, 2 * BLK), :]
        sc = _band_scores(q_ref[0], kw, slope_ref[0], i, scale=scale, n_back=n_back, bps=bps)
        m = jnp.max(sc, axis=-1, keepdims=True)
        if use_sink:
            m = jnp.maximum(m, sink_ref[0])
        p = jnp.exp(sc - m)
        l = jnp.sum(p, axis=-1, keepdims=True)
        if use_sink:
            l = l + jnp.exp(sink_ref[0] - m)
        o_ref[0] = _bdot(p.astype(MXU_DTYPE), vw, NN) / l
        lse_ref[0] = m + jnp.log(l)

    qspec = pl.BlockSpec((1, b, r, dh), lambda gg, i: (gg, 0, i, 0))
    kspec = pl.BlockSpec((1, b, skv, dh), lambda gg, i: (gg, 0, 0, 0))
    rspec = pl.BlockSpec((1, b, r, 1), lambda gg, i: (gg, 0, 0, 0))
    ins = [q, k, v, slope] + ([sink] if use_sink else [])
    return pl.pallas_call(
        body,
        out_shape=(jax.ShapeDtypeStruct((g, b, rows, dh), F32), jax.ShapeDtypeStruct((g, b, rows, 1), F32)),
        grid=(g, nq),
        in_specs=[qspec, kspec, kspec, rspec] + ([rspec] if use_sink else []),
        out_specs=(qspec, pl.BlockSpec((1, b, r, 1), lambda gg, i: (gg, 0, i, 0))),
        compiler_params=_params("parallel", "arbitrary"),
        name=name,
    )(*ins)


def _band_bwd(q, k, v, do, o, lse, slope, sink, *, scale, n_back, bps, name):
    g, b, rows, dh = q.shape
    r = slope.shape[2]
    nq = rows // r
    skv = k.shape[2]
    use_sink = sink is not None
    stacked = r // BLK

    def body(*refs):
        q_ref, k_ref, v_ref, do_ref, o_ref, lse_ref, slope_ref = refs[:7]
        sink_ref = refs[7] if use_sink else None
        dq_ref, dk_ref, dv_ref = refs[7 + use_sink:10 + use_sink]
        i = pl.program_id(1)

        @pl.when(i == 0)
        def _():
            dk_ref[...] = jnp.zeros_like(dk_ref)
            dv_ref[...] = jnp.zeros_like(dv_ref)

        off = pl.multiple_of(i * BLK, BLK)
        qb = q_ref[0]
        kw = k_ref[0, :, pl.ds(off, 2 * BLK), :]
        vw = v_ref[0, :, pl.ds(off, 2 * BLK), :]
        dof = do_ref[0]
        dob = dof.astype(MXU_DTYPE)
        lse_b = lse_ref[0]
        delta = jnp.sum(dof * o_ref[0], axis=-1, keepdims=True)
        sc = _band_scores(qb, kw, slope_ref[0], i, scale=scale, n_back=n_back, bps=bps)
        p = jnp.exp(sc - lse_b)
        ds = (p * (_bdot(dob, vw, NT) - delta) * scale).astype(MXU_DTYPE)
        dq_ref[0] = _bdot(ds, kw, NN)
        dk_ref[0, :, pl.ds(off, 2 * BLK), :] += _bdot(ds, qb, TN)
        dv_ref[0, :, pl.ds(off, 2 * BLK), :] += _bdot(p.astype(MXU_DTYPE), dob, TN)

        if use_sink:
            dsink_ref = refs[10 + use_sink]

            @pl.when(i == 0)
            def _():
                dsink_ref[...] = jnp.zeros_like(dsink_ref)

            contrib = -jnp.exp(sink_ref[0] - lse_b) * delta
            for n in range(stacked):
                part = jnp.sum(contrib[0, n * BLK:(n + 1) * BLK, :], axis=0, keepdims=True)
                dsink_ref[0, n:n + 1, :] += jnp.broadcast_to(part, (1, LANES))

    def qspec(w):
        return pl.BlockSpec((1, b, r, w), lambda gg, i: (gg, 0, i, 0))

    kspec = pl.BlockSpec((1, b, skv, dh), lambda gg, i: (gg, 0, 0, 0))
    rspec = pl.BlockSpec((1, b, r, 1), lambda gg, i: (gg, 0, 0, 0))
    ins = [q, k, v, do, o, lse, slope] + ([sink] if use_sink else [])
    in_specs = [qspec(dh), kspec, kspec, qspec(dh), qspec(dh), qspec(1), rspec] + ([rspec] if use_sink else [])
    out_shape = [jax.ShapeDtypeStruct((g, b, rows, dh), F32), jax.ShapeDtypeStruct((g, b, skv, dh), F32),
                 jax.ShapeDtypeStruct((g, b, skv, dh), F32)]
    out_specs = [qspec(dh), kspec, kspec]
    if use_sink:
        assert b == 1
        out_shape.append(jax.ShapeDtypeStruct((g, stacked, LANES), F32))
        out_specs.append(pl.BlockSpec((1, stacked, LANES), lambda gg, i: (gg, 0, 0)))
    return pl.pallas_call(
        body,
        out_shape=tuple(out_shape),
        grid=(g, nq),
        in_specs=in_specs,
        out_specs=tuple(out_specs),
        compiler_params=_params("parallel", "arbitrary"),
        name=name,
    )(*ins)


def _merge(outs, lses, *, name):
    h, s, dv = outs[0].shape
    ts = _pick(s, 512, 8)

    def body(o0, o1, o2, l0, l1, l2, ob_ref, lt_ref):
        a, b, c = l0[0], l1[0], l2[0]
        m = jnp.maximum(jnp.maximum(a, b), c)
        ea, eb, ec = jnp.exp(a - m), jnp.exp(b - m), jnp.exp(c - m)
        den = ea + eb + ec
        ob_ref[0] = (ea / den) * o0[0] + (eb / den) * o1[0] + (ec / den) * o2[0]
        lt_ref[0] = m + jnp.log(den)

    ospec = pl.BlockSpec((1, ts, dv), lambda hh, i: (hh, i, 0))
    lspec = pl.BlockSpec((1, ts, 1), lambda hh, i: (hh, i, 0))
    return pl.pallas_call(
        body,
        out_shape=(jax.ShapeDtypeStruct((h, s, dv), F32), jax.ShapeDtypeStruct((h, s, 1), F32)),
        grid=(h, s // ts),
        in_specs=[ospec] * 3 + [lspec] * 3,
        out_specs=(ospec, lspec),
        compiler_params=_params("parallel", "parallel"),
        name=name,
    )(*outs, *lses)


def _tile_iotas(t):
    return lax.broadcasted_iota(jnp.int32, (t, t), 0), lax.broadcasted_iota(jnp.int32, (t, t), 1)


def _rotate(x, c, s_lo, s_hi):
    half = D_ROPE // 2
    return x * c + pltpu.roll(x, LANES - half, 1) * s_lo + pltpu.roll(x, half, 1) * s_hi


def _mla_keys(kv_h, kr_t, first):
    return jnp.where(first, kv_h, kr_t)


def _mla_fwd(qd, kvd, krp, tables, *, scale, name):
    s = qd.shape[0]
    pairs = qd.shape[1] // (2 * LANES)
    t = min(CAUSAL_TILE, s)

    def body(q_ref, kv_ref, kr_ref, c_ref, lo_ref, hi_ref, o_ref, lse_ref):
        i = pl.program_id(1)
        first = lax.broadcasted_iota(jnp.int32, (1, LANES), 1) < HEAD_DIM
        tabs = (c_ref[...], lo_ref[...], hi_ref[...])
        q_heads = [_rotate(q_ref[:, hh * LANES:(hh + 1) * LANES], *tabs).astype(MXU_DTYPE) for hh in range(2)]

        def tile(j, carry, diagonal):
            off = pl.multiple_of(j * t, t)
            kr_t = kr_ref[pl.ds(off, t), :]
            out = []
            for hh in range(2):
                m, l, acc = carry[3 * hh:3 * hh + 3]
                kv_h = kv_ref[pl.ds(off, t), hh * LANES:(hh + 1) * LANES]
                sc = _dot(q_heads[hh], _mla_keys(kv_h, kr_t, first), NT) * scale
                if diagonal:
                    row, col = _tile_iotas(t)
                    sc = jnp.where(row >= col, sc, NEG)
                m_new = jnp.maximum(m, jnp.max(sc, axis=-1, keepdims=True))
                a = jnp.exp(m - m_new)
                p = jnp.exp(sc - m_new)
                out += [m_new, a * l + jnp.sum(p, axis=-1, keepdims=True), a * acc + _dot(p.astype(MXU_DTYPE), kv_h, NN)]
            return tuple(out)

        init = (jnp.full((t, 1), NEG, F32), jnp.zeros((t, 1), F32), jnp.zeros((t, LANES), F32)) * 2
        carry = lax.fori_loop(0, i, lambda j, c: tile(j, c, False), init)
        m0, l0, acc0, m1, l1, acc1 = tile(i, carry, True)
        o_ref[...] = jnp.where(first, pltpu.roll(acc0 / l0, HEAD_DIM, 1), acc1 / l1)
        lse_ref[0] = jnp.where(lax.broadcasted_iota(jnp.int32, (t, 2), 1) == 0, m0 + jnp.log(l0), m1 + jnp.log(l1))

    tab = pl.BlockSpec((t, LANES), lambda p, i: (i, 0))
    return pl.pallas_call(
        body,
        out_shape=(jax.ShapeDtypeStruct((s, pairs * LANES), F32), jax.ShapeDtypeStruct((pairs, s, 2), F32)),
        grid=(pairs, s // t),
        in_specs=[pl.BlockSpec((t, 2 * LANES), lambda p, i: (i, p)), pl.BlockSpec((s, 2 * LANES), lambda p, i: (0, p)),
                  pl.BlockSpec((s, LANES), lambda p, i: (0, 0)), tab, tab, tab],
        out_specs=(pl.BlockSpec((t, LANES), lambda p, i: (i, p)), pl.BlockSpec((1, t, 2), lambda p, i: (p, i, 0))),
        compiler_params=_params("parallel", "arbitrary"),
        name=name,
    )(qd, kvd, krp, *tables)


def _mla_bwd(qd, kvd, krp, tables, do, o, lse, *, do_block0, scale, name):
    s = qd.shape[0]
    pairs = qd.shape[1] // (2 * LANES)
    t = min(CAUSAL_TILE, s)

    def body(q_ref, kv_ref, kr_ref, c_ref, lo_ref, hi_ref, do_ref, o_ref, lse_ref, dq_ref, dkv_ref, dkr_ref):
        i = pl.program_id(1)

        @pl.when(i == 0)
        def _():
            dkv_ref[...] = jnp.zeros_like(dkv_ref)
            dkr_ref[...] = jnp.zeros_like(dkr_ref)

        first = lax.broadcasted_iota(jnp.int32, (1, LANES), 1) < HEAD_DIM
        tabs = (c_ref[...], lo_ref[...], hi_ref[...])
        q_heads = [_rotate(q_ref[:, hh * LANES:(hh + 1) * LANES], *tabs).astype(MXU_DTYPE) for hh in range(2)]
        dof = do_ref[...]
        prod = dof * o_ref[...]
        deltas = [jnp.sum(jnp.where(first, prod, 0.0), axis=-1, keepdims=True),
                  jnp.sum(jnp.where(first, 0.0, prod), axis=-1, keepdims=True)]
        do_heads = [jnp.where(first, 0.0, pltpu.roll(dof, HEAD_DIM, 1)).astype(MXU_DTYPE),
                    jnp.where(first, 0.0, dof).astype(MXU_DTYPE)]
        lses = [lse_ref[0][:, hh:hh + 1] for hh in range(2)]

        def tile(j, carry, diagonal):
            off = pl.multiple_of(j * t, t)
            kr_t = kr_ref[pl.ds(off, t), :]
            out, dkr_add = [], None
            for hh in range(2):
                kv_h = kv_ref[pl.ds(off, t), hh * LANES:(hh + 1) * LANES]
                k_h = _mla_keys(kv_h, kr_t, first)
                sc = _dot(q_heads[hh], k_h, NT) * scale
                if diagonal:
                    row, col = _tile_iotas(t)
                    sc = jnp.where(row >= col, sc, NEG)
                p = jnp.exp(sc - lses[hh])
                ds = (p * (_dot(do_heads[hh], kv_h, NT) - deltas[hh]) * scale).astype(MXU_DTYPE)
                dk_full = _dot(ds, q_heads[hh], TN)
                dv_full = _dot(p.astype(MXU_DTYPE), do_heads[hh], TN)
                dkv_ref[pl.ds(off, t), hh * LANES:(hh + 1) * LANES] += jnp.where(first, dk_full, dv_full)
                rot = jnp.where(first, 0.0, dk_full)
                dkr_add = rot if dkr_add is None else dkr_add + rot
                out.append(carry[hh] + _dot(ds, k_h, NN))
            dkr_ref[0, pl.ds(off, t), :] += dkr_add
            return tuple(out)

        zacc = jnp.zeros((t, LANES), F32)
        carry = lax.fori_loop(0, i, lambda j, c: tile(j, c, False), (zacc, zacc))
        dq_heads = tile(i, carry, True)
        for hh in range(2):
            dq_ref[:, hh * LANES:(hh + 1) * LANES] = _rotate(dq_heads[hh], tabs[0], -tabs[1], -tabs[2])

    tab = pl.BlockSpec((t, LANES), lambda p, i: (i, 0))
    qspec = pl.BlockSpec((t, 2 * LANES), lambda p, i: (i, p))
    kvspec = pl.BlockSpec((s, 2 * LANES), lambda p, i: (0, p))
    return pl.pallas_call(
        body,
        out_shape=(jax.ShapeDtypeStruct(qd.shape, F32), jax.ShapeDtypeStruct(kvd.shape, F32),
                   jax.ShapeDtypeStruct((pairs, s, LANES), F32)),
        grid=(pairs, s // t),
        in_specs=[qspec, kvspec, pl.BlockSpec((s, LANES), lambda p, i: (0, 0)), tab, tab, tab,
                  pl.BlockSpec((t, LANES), lambda p, i: (i, do_block0 + p)), pl.BlockSpec((t, LANES), lambda p, i: (i, p)),
                  pl.BlockSpec((1, t, 2), lambda p, i: (p, i, 0))],
        out_specs=(qspec, kvspec, pl.BlockSpec((1, s, LANES), lambda p, i: (p, 0, 0))),
        compiler_params=_params("parallel", "arbitrary"),
        name=name,
    )(qd, kvd, krp, *tables, do, o, lse)


def _split_cumsum(x, tri, terms=2):
    hi = x.astype(BF16)
    if terms == 1:
        return _dot(hi, tri, NN)
    lo = (x - hi.astype(F32)).astype(BF16)
    return _dot(hi, tri, NN) + _dot(lo, tri, NN)


def _chunked_cumsum(x, tri, run, *, reverse, sign, terms=2):
    c = tri.shape[0]
    n = x.shape[1] // c
    parts = [None] * n
    for idx in (reversed(range(n)) if reverse else range(n)):
        xc = x[:, idx * c:(idx + 1) * c]
        parts[idx] = sign * (run + _split_cumsum(xc, tri, terms))
        run = run + jnp.sum(xc, axis=-1, keepdims=True)
    return (parts[0] if n == 1 else jnp.concatenate(parts, axis=1)), run


def _sb_logs(z):
    e = jnp.exp(-jnp.abs(z))
    l1 = jnp.log(1.0 + e)
    return e, jnp.minimum(z, 0.0) - l1, -jnp.maximum(z, 0.0) - l1


def _pair_masks():
    first = lax.broadcasted_iota(jnp.int32, (1, LANES), 1) < HEAD_DIM
    m0 = first.astype(MXU_DTYPE)
    return first, (m0, 1 - m0)


def _sb_fwd(qkv, *, heads, scale, name):
    s = qkv.shape[0]
    pairs = heads * HEAD_DIM // LANES
    t = min(CAUSAL_TILE, s)
    cc = min(CUM_CHUNK, t)

    def body(q_ref, k_ref, v_ref, o_ref, t_ref):
        i = pl.program_id(1)
        first, masks = _pair_masks()
        q_heads = [q_ref[...] * m for m in masks]
        crow, ccol = _tile_iotas(cc)
        after = (crow > ccol).astype(BF16)

        def tile(j, carry, diagonal):
            off = pl.multiple_of(j * t, t)
            kb = k_ref[pl.ds(off, t), :]
            vb = v_ref[pl.ds(off, t), :]
            if diagonal:
                row, col = _tile_iotas(t)
                strict = row > col
            out = []
            for hh in range(2):
                run, acc = carry[2 * hh], carry[2 * hh + 1]
                z = _dot(q_heads[hh], kb, NT) * scale
                _, log_beta, log_keep = _sb_logs(z)
                if diagonal:
                    log_keep = jnp.where(strict, log_keep, 0.0)
                a, run = _chunked_cumsum(log_keep, after, run, reverse=True, sign=1.0)
                w = jnp.exp(log_beta + a)
                if diagonal:
                    w = jnp.where(strict, w, 0.0)
                out += [run, acc + _dot(w.astype(MXU_DTYPE), vb, NN)]
            return tuple(out)

        zero, zacc = jnp.zeros((t, 1), F32), jnp.zeros((t, LANES), F32)
        carry = tile(i, (zero, zacc, zero, zacc), True)
        run0, acc0, run1, acc1 = lax.fori_loop(0, i, lambda jj, c: tile(i - 1 - jj, c, False), carry)
        o_ref[...] = jnp.where(first, acc0, acc1)
        t_ref[0] = jnp.where(lax.broadcasted_iota(jnp.int32, (t, 2), 1) == 0, run0, run1)

    return pl.pallas_call(
        body,
        out_shape=(jax.ShapeDtypeStruct((s, heads * HEAD_DIM), F32), jax.ShapeDtypeStruct((pairs, s, 2), F32)),
        grid=(pairs, s // t),
        in_specs=[pl.BlockSpec((t, LANES), lambda p, i: (i, p)),
                  pl.BlockSpec((s, LANES), lambda p, i: (0, pairs + p)),
                  pl.BlockSpec((s, LANES), lambda p, i: (0, 2 * pairs + p))],
        out_specs=(pl.BlockSpec((t, LANES), lambda p, i: (i, p)), pl.BlockSpec((1, t, 2), lambda p, i: (p, i, 0))),
        compiler_params=_params("parallel", "arbitrary"),
        name=name,
    )(qkv, qkv, qkv)


def _sb_bwd(qkv, do, total, *, heads, scale, name):
    s = qkv.shape[0]
    pairs = heads * HEAD_DIM // LANES
    t = min(CAUSAL_TILE, s)
    cc = min(CUM_CHUNK, t)

    def body(q_ref, k_ref, v_ref, do_ref, t_ref, dq_ref, dk_ref, dv_ref):
        i = pl.program_id(1)

        @pl.when(i == 0)
        def _():
            dk_ref[...] = jnp.zeros_like(dk_ref)
            dv_ref[...] = jnp.zeros_like(dv_ref)

        first, masks = _pair_masks()
        q_heads = [q_ref[...] * m for m in masks]
        do_b = do_ref[...].astype(MXU_DTYPE)
        do_heads = [do_b * m for m in masks]
        tots = [t_ref[0][:, hh:hh + 1] for hh in range(2)]
        crow, ccol = _tile_iotas(cc)
        upto = (crow <= ccol).astype(BF16)
        before = (crow < ccol).astype(BF16)

        def tile(j, carry, diagonal):
            off = pl.multiple_of(j * t, t)
            kb = k_ref[pl.ds(off, t), :]
            vb = v_ref[pl.ds(off, t), :]
            if diagonal:
                row, col = _tile_iotas(t)
                strict = row > col
            out, dk_add, dv_add = [], None, None
            for hh in range(2):
                run_keep, run_g, dq_acc = carry[3 * hh:3 * hh + 3]
                z = _dot(q_heads[hh], kb, NT) * scale
                e, log_beta, log_keep = _sb_logs(z)
                if diagonal:
                    log_keep = jnp.where(strict, log_keep, 0.0)
                a, run_keep = _chunked_cumsum(log_keep, upto, run_keep - tots[hh], reverse=False, sign=-1.0)
                run_keep = run_keep + tots[hh]
                w = jnp.exp(log_beta + a)
                if diagonal:
                    w = jnp.where(strict, w, 0.0)
                g = w * _dot(do_heads[hh], vb, NT)
                prefix, run_g = _chunked_cumsum(g, before, run_g, reverse=False, sign=1.0, terms=1)
                pos = z >= 0.0
                dz = (g * jnp.where(pos, e, 1.0) - jnp.where(pos, 1.0, e) * prefix) * pl.reciprocal(1.0 + e, approx=True)
                if diagonal:
                    dz = jnp.where(strict, dz, 0.0)
                dz = (dz * scale).astype(MXU_DTYPE)
                dk_h = _dot(dz, q_heads[hh], TN)
                dv_h = _dot(w.astype(MXU_DTYPE), do_heads[hh], TN)
                dk_add = dk_h if dk_add is None else dk_add + dk_h
                dv_add = dv_h if dv_add is None else dv_add + dv_h
                out += [run_keep, run_g, dq_acc + _dot(dz, kb, NN)]
            dk_ref[pl.ds(off, t), :] += dk_add
            dv_ref[pl.ds(off, t), :] += dv_add
            return tuple(out)

        zero, zacc = jnp.zeros((t, 1), F32), jnp.zeros((t, LANES), F32)
        carry = lax.fori_loop(0, i, lambda j, c: tile(j, c, False), (zero, zero, zacc, zero, zero, zacc))
        res = tile(i, carry, True)
        dq_ref[...] = jnp.where(first, res[2], res[5])

    qspec = pl.BlockSpec((t, LANES), lambda p, i: (i, p))
    shp = jax.ShapeDtypeStruct((s, heads * HEAD_DIM), F32)
    return pl.pallas_call(
        body,
        out_shape=(shp, shp, shp),
        grid=(pairs, s // t),
        in_specs=[qspec, pl.BlockSpec((s, LANES), lambda p, i: (0, pairs + p)),
                  pl.BlockSpec((s, LANES), lambda p, i: (0, 2 * pairs + p)), qspec,
                  pl.BlockSpec((1, t, 2), lambda p, i: (p, i, 0))],
        out_specs=(qspec, pl.BlockSpec((s, LANES), lambda p, i: (0, p)), pl.BlockSpec((s, LANES), lambda p, i: (0, p))),
        compiler_params=_params("parallel", "arbitrary"),
        name=name,
    )(qkv, qkv, qkv, do, total)


def _loss_head(y, target, *, name):
    s, d = y.shape
    tr = _pick(s, 256, 8)

    def body(y_ref, t_ref, dy_ref, loss_ref):
        err = y_ref[...] - t_ref[...]
        dy_ref[...] = err * (1.0 / d)

        @pl.when(pl.program_id(0) == 0)
        def _():
            loss_ref[...] = jnp.zeros_like(loss_ref)

        per_tok = jnp.mean(err * err, axis=-1, keepdims=True)
        loss_ref[...] += 0.5 * jnp.sum(per_tok, axis=0, keepdims=True)

    row = pl.BlockSpec((tr, d), lambda i: (i, 0))
    return pl.pallas_call(
        body,
        out_shape=(jax.ShapeDtypeStruct((s, d), F32), jax.ShapeDtypeStruct((1, LANES), F32)),
        grid=(s // tr,),
        in_specs=[row, row],
        out_specs=(row, pl.BlockSpec((1, LANES), lambda i: (0, 0))),
        compiler_params=_params("arbitrary"),
        name=name,
    )(y, target)


def _adamw(w, grads, m, v, *, name, deps=()):
    nl, r, c = w.shape
    cp = grads[0].shape[1]
    tr = _pick(r, 256, 8)

    def body(*refs):
        w_ref, m_ref, v_ref = refs[:3]
        g_refs = refs[3:3 + nl]
        g_out, d_ref, m2_ref, v2_ref = refs[3 + nl + len(deps):]
        layer = pl.program_id(0)
        gv = g_refs[0][:, :c]
        for n in range(1, nl):
            gv = jnp.where(layer == n, g_refs[n][:, :c], gv)
        m2 = ADAM_B1 * m_ref[0] + (1.0 - ADAM_B1) * gv
        v2 = ADAM_B2 * v_ref[0] + (1.0 - ADAM_B2) * (gv * gv)
        m_hat = m2 / (1.0 - ADAM_B1 ** ADAM_STEP)
        v_hat = v2 / (1.0 - ADAM_B2 ** ADAM_STEP)
        g_out[0] = gv
        d_ref[0] = -ADAM_LR * (m_hat / (jnp.sqrt(v_hat) + ADAM_EPS) + ADAM_WD * w_ref[0])
        m2_ref[0] = m2
        v2_ref[0] = v2

    blk = pl.BlockSpec((1, tr, c), lambda l, i: (l, i, 0))
    gspec = pl.BlockSpec((tr, cp), lambda l, i: (i, 0))
    shp = jax.ShapeDtypeStruct((nl, r, c), F32)
    return pl.pallas_call(
        body,
        out_shape=(shp, shp, shp, shp),
        grid=(nl, r // tr),
        in_specs=[blk, blk, blk] + [gspec] * nl + [ANY] * len(deps),
        out_specs=(blk, blk, blk, blk),
        compiler_params=_params("parallel", "parallel"),
        name=name,
    )(w, m, v, *grads, *deps)


def _pair_sum(mine, recv, my_c, *, name):
    _, r, c = mine.shape
    tr = _pick(r, 512, 16)

    def body(c_ref, a_ref, b_ref, o_ref):
        o_ref[0] = (a_ref[0].astype(F32) + b_ref[0].astype(F32)).astype(o_ref.dtype)

    grid_spec = pltpu.PrefetchScalarGridSpec(
        num_scalar_prefetch=1,
        grid=(4, r // tr),
        in_specs=[pl.BlockSpec((1, tr, c), lambda kk, i, c_ref: (2 * kk + c_ref[0], i, 0)),
                  pl.BlockSpec((1, tr, c), lambda kk, i, c_ref: (kk, i, 0))],
        out_specs=pl.BlockSpec((1, tr, c), lambda kk, i, c_ref: (kk, i, 0)),
    )
    return pl.pallas_call(
        body,
        out_shape=jax.ShapeDtypeStruct((4, r, c), mine.dtype),
        grid_spec=grid_spec,
        compiler_params=_params("parallel", "parallel"),
        name=name,
    )(my_c.reshape(1).astype(jnp.int32), mine, recv)


def _final_sum(partial, recv, my_chip, *, name):
    _, r, c = partial.shape
    tr = _pick(r, 512, 16)

    def body(chip_ref, p_ref, r0, r1, r2, o_ref):
        o_ref[...] = ((p_ref[0].astype(F32) + r0[0].astype(F32)) + r1[0].astype(F32)) + r2[0].astype(F32)

    def slot(n):
        return pl.BlockSpec((1, tr, c), lambda i, chip_ref: (n, i, 0))

    grid_spec = pltpu.PrefetchScalarGridSpec(
        num_scalar_prefetch=1,
        grid=(r // tr,),
        in_specs=[pl.BlockSpec((1, tr, c), lambda i, chip_ref: (chip_ref[0], i, 0)), slot(0), slot(1), slot(2)],
        out_specs=pl.BlockSpec((tr, c), lambda i, chip_ref: (i, 0)),
    )
    return pl.pallas_call(
        body,
        out_shape=jax.ShapeDtypeStruct((r, c), F32),
        grid_spec=grid_spec,
        compiler_params=_params("parallel"),
        name=name,
    )(my_chip.reshape(1).astype(jnp.int32), partial, recv, recv, recv)


def _sum_devices(stack, *, name):
    n, r, c = stack.shape

    def body(s_ref, o_ref):
        acc = s_ref[0]
        for dev in range(1, n):
            acc = acc + s_ref[dev]
        o_ref[...] = acc

    return pl.pallas_call(
        body,
        out_shape=jax.ShapeDtypeStruct((r, c), F32),
        in_specs=[pl.BlockSpec(memory_space=pltpu.VMEM)],
        out_specs=pl.BlockSpec(memory_space=pltpu.VMEM),
        name=name,
    )(stack)


def _mesh_pos():
    return lax.axis_index("x"), lax.axis_index("y"), lax.axis_index("c")


def _all_gather(shards, *, name):
    n = len(shards)

    def body(*refs):
        x_refs, out_refs = refs[:n], refs[n:2 * n]
        send_sems, recv_sems, local_sems = refs[2 * n:]
        x, y, cc = _mesh_pos()
        me, sibling = (x, y, cc), (x, y, 1 - cc)
        chips = [(1 - x, y), (x, 1 - y), (1 - x, 1 - y)]

        def rows(a, px, py, pc):
            return out_refs[a].at[4 * px + 2 * py + pc]

        def copy(a, kk, block, to, src=None):
            return pltpu.make_async_remote_copy(
                src_ref=rows(a, *block) if src is None else src, dst_ref=rows(a, *block),
                send_sem=send_sems.at[7 * a + kk], recv_sem=recv_sems.at[7 * a + kk],
                device_id=to, device_id_type=MESH)

        sends, own = [], []
        for a in range(n):
            own.append(pltpu.make_async_copy(x_refs[a], rows(a, *me), local_sems.at[a]))
            own[a].start()
            first = [copy(a, 0, me, sibling, src=x_refs[a])]
            first += [copy(a, 1 + j, me, (*chip, cc), src=x_refs[a]) for j, chip in enumerate(chips)]
            for cp in first:
                cp.start()
            sends += first
        for a in range(n):
            for j, chip in enumerate(chips):
                copy(a, 1 + j, (*chip, cc), me).wait_recv()
                passed = copy(a, 4 + j, (*chip, cc), sibling)
                passed.start()
                sends.append(passed)
        for a in range(n):
            copy(a, 0, sibling, me).wait_recv()
            for j, chip in enumerate(chips):
                copy(a, 4 + j, (*chip, 1 - cc), me).wait_recv()
        for cp in sends:
            cp.wait_send()
        for cp in own:
            cp.wait()

    return pl.pallas_call(
        body,
        out_shape=tuple(jax.ShapeDtypeStruct((N_DEV,) + t.shape, t.dtype) for t in shards),
        in_specs=[ANY] * n,
        out_specs=tuple([ANY] * n),
        scratch_shapes=[pltpu.SemaphoreType.DMA((7 * n,)), pltpu.SemaphoreType.DMA((7 * n,)),
                        pltpu.SemaphoreType.DMA((n,))],
        name=name,
    )(*shards)


def _plan_own_blocks(n):
    def plan(refs, send_sems, recv_sems, outgoing):
        x, y, cc = _mesh_pos()
        peers = [(x, y, 1 - cc), (1 - x, y, cc), (x, 1 - y, cc), (1 - x, 1 - y, cc)]
        copies = []
        for a in range(n):
            land = refs[n + a]
            for kk, (px, py, pc) in enumerate(peers):
                block = (x, y, cc) if outgoing else (px, py, pc)
                rows = land.at[4 * block[0] + 2 * block[1] + block[2]]
                copies.append(pltpu.make_async_remote_copy(
                    src_ref=refs[a] if outgoing else rows, dst_ref=rows, send_sem=send_sems.at[4 * a + kk],
                    recv_sem=recv_sems.at[4 * a + kk], device_id=(px, py, pc), device_id_type=MESH))
        return copies

    plan.n_sems = 4 * n
    return plan


def _plan_pass_on(n):
    def plan(refs, send_sems, recv_sems, outgoing):
        x, y, cc = _mesh_pos()
        copies = []
        for a in range(n):
            for j, (px, py) in enumerate([(1 - x, y), (x, 1 - y), (1 - x, 1 - y)]):
                rows = refs[a].at[4 * px + 2 * py + (cc if outgoing else 1 - cc)]
                copies.append(pltpu.make_async_remote_copy(
                    src_ref=rows, dst_ref=rows, send_sem=send_sems.at[3 * a + j], recv_sem=recv_sems.at[3 * a + j],
                    device_id=(x, y, 1 - cc), device_id_type=MESH))
        return copies

    plan.n_sems = 3 * n
    return plan


def _plan_to_sibling(n):
    def plan(refs, send_sems, recv_sems, outgoing):
        x, y, cc = _mesh_pos()
        copies = []
        for a in range(n):
            for chip in range(4):
                dst = refs[n + a].at[chip]
                copies.append(pltpu.make_async_remote_copy(
                    src_ref=refs[a].at[2 * chip + (1 - cc)] if outgoing else dst, dst_ref=dst,
                    send_sem=send_sems.at[4 * a + chip], recv_sem=recv_sems.at[4 * a + chip],
                    device_id=(x, y, 1 - cc), device_id_type=MESH))
        return copies

    plan.n_sems = 4 * n
    return plan


def _plan_to_chips(n):
    def plan(refs, send_sems, recv_sems, outgoing):
        x, y, cc = _mesh_pos()
        copies = []
        for a in range(n):
            for j, (px, py) in enumerate([(1 - x, y), (x, 1 - y), (1 - x, 1 - y)]):
                dst = refs[n + a].at[j]
                copies.append(pltpu.make_async_remote_copy(
                    src_ref=refs[a].at[2 * px + py] if outgoing else dst, dst_ref=dst,
                    send_sem=send_sems.at[3 * a + j], recv_sem=recv_sems.at[3 * a + j],
                    device_id=(px, py, cc), device_id_type=MESH))
        return copies

    plan.n_sems = 3 * n
    return plan


def _in_hbm(t):
    return pltpu.with_memory_space_constraint(t, pltpu.HBM)


def _exchange_start(plan, bufs, after, *, name):
    nb, na = len(bufs), len(after)

    def body(*refs):
        outs = refs[nb + na:]
        for cp in plan(refs[:nb], outs[0], outs[1], True):
            cp.start()
        outs[2 + nb][...] = jnp.zeros_like(outs[2 + nb])

    res = pl.pallas_call(
        body,
        out_shape=(pltpu.SemaphoreType.DMA((plan.n_sems,)), pltpu.SemaphoreType.DMA((plan.n_sems,)),
                   *[pltpu.HBM(t.shape, t.dtype) for t in bufs], jax.ShapeDtypeStruct((8, LANES), F32)),
        in_specs=[HBM_SPEC] * nb + [ANY] * na,
        out_specs=(SEM_SPEC, SEM_SPEC, *[HBM_SPEC] * nb, pl.BlockSpec(memory_space=pltpu.VMEM)),
        input_output_aliases={i: 2 + i for i in range(nb)},
        compiler_params=pltpu.CompilerParams(has_side_effects=DATAFLOW_EFFECT),
        name=name,
    )(*[_in_hbm(t) for t in bufs], *after)
    return plan, res[:2], list(res[2:2 + nb]), res[2 + nb]


def _exchange_wait(flight, after, *, name):
    plan, sems, bufs, _ = flight
    nb = len(bufs)

    def body(*refs):
        send_sems, recv_sems = refs[nb], refs[nb + 1]
        for cp in plan(refs[:nb], send_sems, recv_sems, False):
            cp.wait_recv()
        for cp in plan(refs[:nb], send_sems, recv_sems, True):
            cp.wait_send()

    res = pl.pallas_call(
        body,
        out_shape=tuple(pltpu.HBM(t.shape, t.dtype) for t in bufs),
        in_specs=[HBM_SPEC] * nb + [SEM_SPEC, SEM_SPEC] + [ANY] * len(after),
        out_specs=tuple([HBM_SPEC] * nb),
        input_output_aliases={i: i for i in range(nb)},
        compiler_params=pltpu.CompilerParams(has_side_effects=DATAFLOW_EFFECT),
        name=name,
    )(*bufs, *sems, *after)
    return list(res)


_W_GROUPS = {"even": ("even_w_in", "even_w_out"), "mlp0": ("mlp_w1_0", "mlp_w2_0"),
             "odd": ("odd_w_in", "odd_w_uq", "odd_w_ukv", "odd_w_out"), "mlp1": ("mlp_w1_1", "mlp_w2_1")}


class _Exchanges:
    def __init__(self, shards, gains, n_q, n_kv):
        self.n_q, self.n_kv = n_q, n_kv
        self.mx, self.my, self.mc = _mesh_pos()
        self.dev = 4 * self.mx + 2 * self.my + self.mc
        self.shards = shards
        self.gains = gains
        self.flights, self.gathered, self.grad_blocks, self.grad_names, self.reduced = {}, {}, {}, {}, {}

    def start(self):
        names = _W_GROUPS["even"]
        got = _all_gather([self.shards[n] for n in names] + [self.gains], name="comm_even_gather")
        self.gathered.update(zip(names, got[:-1]))
        self.all_gains = got[-1][:, 0]
        return self._w_begin("mlp0", [got[0]])

    def _w_begin(self, group, after):
        srcs = [self.shards[n] for n in _W_GROUPS[group]]
        lands = [lax.dynamic_update_slice(lax.empty((N_DEV,) + t.shape, t.dtype), t[None], (self.dev, 0, 0)) for t in srcs]
        self.flights[group] = _exchange_start(_plan_own_blocks(len(srcs)), srcs + lands, after, name=f"comm_{group}_own_start")
        return [self.flights[group][3]]

    def _w_turn(self, group, after):
        bufs = _exchange_wait(self.flights[group], after, name=f"comm_{group}_own_wait")
        n = len(bufs) // 2
        self.flights[group] = _exchange_start(_plan_pass_on(n), bufs[n:], [], name=f"comm_{group}_pass_start")
        return [self.flights[group][3]]

    def _w_end(self, group, after):
        self.gathered.update(zip(_W_GROUPS[group], _exchange_wait(self.flights.pop(group), after, name=f"comm_{group}_pass_wait")))

    def weights(self, group):
        return {n: self.gathered[n] for n in _W_GROUPS[group]}

    def norm_gains(self):
        return (self.all_gains[:, :self.n_q].reshape(-1), self.all_gains[:, self.n_q:self.n_q + self.n_kv].reshape(-1))

    def grads(self, group, named_blocks):
        self.grad_names[group] = [n for n, _ in named_blocks]
        self.grad_blocks[group] = [t for _, t in named_blocks]

    def _g_begin(self, group, after):
        blocks = self.grad_blocks[group]
        lands = [lax.empty((4,) + t.shape[1:], t.dtype) for t in blocks]
        self.flights[group] = _exchange_start(_plan_to_sibling(len(blocks)), blocks + lands, after, name=f"comm_{group}_sib_start")
        return [self.flights[group][3]]

    def _g_turn(self, group, after):
        bufs = _exchange_wait(self.flights[group], after, name=f"comm_{group}_sib_wait")
        n = len(bufs) // 2
        partial = [_pair_sum(a, b, self.mc, name=f"pair_sum_{nm}") for nm, a, b in zip(self.grad_names[group], bufs[:n], bufs[n:])]
        lands = [lax.empty((3,) + t.shape[1:], t.dtype) for t in partial]
        self.flights[group] = _exchange_start(_plan_to_chips(n), partial + lands, [], name=f"comm_{group}_chips_start")
        return [self.flights[group][3]]

    def _g_end(self, group, after):
        bufs = _exchange_wait(self.flights.pop(group), after, name=f"comm_{group}_chips_wait")
        n = len(bufs) // 2
        for nm, a, b in zip(self.grad_names[group], bufs[:n], bufs[n:]):
            self.reduced[nm] = _final_sum(a, b, 2 * self.mx + self.my, name=f"final_sum_{nm}")

    _SCHEDULE = {
        "even_out": (("w_turn", "mlp0"), ("w_begin", "odd")),
        "ln1_l0": (("w_end", "mlp0"),),
        "ln2_l0": (("w_turn", "odd"), ("w_begin", "mlp1"), ("w_end", "odd")),
        "odd_out": (("w_turn", "mlp1"),),
        "ln1_l1": (("w_end", "mlp1"),),
        "dw_l1": (("g_begin", "mlp1"),),
        "ln1_bwd_l1": (("g_turn", "mlp1"),),
        "odd_in_dw": (("g_begin", "odd"),),
        "mlp2_dx_l0": (("g_end", "mlp1"), ("g_turn", "odd")),
        "dw_l0": (("g_begin", "mlp0"),),
        "ln1_bwd_l0": (("g_end", "odd"), ("g_turn", "mlp0")),
        "even_in_dw": (("g_begin", "even"),),
        "even_in_dx": (("g_end", "mlp0"), ("g_turn", "even")),
        "finish": (("g_end", "even"),),
    }

    def sync(self, tag, after):
        latest, started = list(after), []
        for what, group in self._SCHEDULE[tag]:
            out = getattr(self, "_" + what)(group, latest)
            if out:
                latest = started = out
        return started


def _alibi(n):
    return 2.0 ** (-8.0 * np.arange(1, n + 1, dtype=np.float32) / n)


def _heads(t, n):
    s = t.shape[0]
    return t.reshape(s, n, t.shape[1] // n).transpose(1, 0, 2)


def _unheads(t):
    n, s, dh = t.shape
    return t.transpose(1, 0, 2).reshape(s, n * dh)


def _to_strided(t, d):
    h, s, x = t.shape
    return t.reshape(h, s // d, d, x).transpose(0, 2, 1, 3).reshape(h, s, x)


def _from_strided(t, d):
    h, s, x = t.shape
    return t.reshape(h, d, s // d, x).transpose(0, 2, 1, 3).reshape(h, s, x)


def _drop_block_pad(t, c):
    s = t.shape[0]
    return t.reshape(s, N_DEV, -1)[:, :, :c].reshape(s, N_DEV * c)


def _add_block_pad(t, cp):
    s, n = t.shape
    c = n // N_DEV
    return jnp.pad(t.reshape(s, N_DEV, c), ((0, 0), (0, 0), (0, cp - c))).reshape(s, N_DEV * cp)


def _stack_rows(t, nq):
    return t.reshape(nq, BLK, A_KV_HEADS, A_GROUP, HEAD_DIM).transpose(2, 0, 3, 1, 4).reshape(
        A_KV_HEADS, 1, nq * A_GROUP * BLK, HEAD_DIM)


def _unstack_rows(t, nq):
    return t.reshape(A_KV_HEADS, nq, A_GROUP, BLK, HEAD_DIM).transpose(1, 3, 0, 2, 4).reshape(
        nq * BLK, A_Q_W)


def _lead_block(t):
    return jnp.pad(t, ((0, 0), (BLK, 0), (0, 0)))


def _columns(t):
    return t.transpose(1, 0, 2).reshape(t.shape[1], -1)


def _rows(t):
    return t.reshape(-1, t.shape[2])


def _by_column_block(t):
    return t.reshape(t.shape[0], N_DEV, -1).transpose(1, 0, 2)


def _local_step(x0, target, comm, sinks, ln1_g, ln1_b, ln2_g, ln2_b):
    s, d = x0.shape
    scale_h = 1.0 / math.sqrt(HEAD_DIM)
    scale_d = 1.0 / math.sqrt(D_NOPE + D_ROPE)
    nq = s // BLK
    even_c, odd_c = EVEN_IN // N_DEV, ODD_IN // N_DEV
    bf = lambda t: t.astype(MXU_DTYPE)
    blocks = dict(b_blocks=True)

    tok = comm.start()
    w_even = comm.weights("even")
    w_even_in, w_even_out = w_even["even_w_in"], _columns(w_even["even_w_out"])
    even_cp = w_even_in.shape[2]
    x0b = bf(x0)
    h_e = _drop_block_pad(_mm(x0b, w_even_in, out_dtypes=(MXU_DTYPE,), name="even_in_fwd", deps=tok, **blocks), even_c)
    qa = _stack_rows(h_e[:, :A_Q_W], nq)
    ka = _lead_block(_heads(h_e[:, A_Q_W:A_Q_W + A_KV_W], A_KV_HEADS))[:, None]
    va = _lead_block(_heads(h_e[:, A_Q_W + A_KV_W:A_Q_W + 2 * A_KV_W], A_KV_HEADS))[:, None]
    rows_a = A_GROUP * BLK
    slope_a = jnp.asarray(np.repeat(_alibi(A_Q_HEADS).reshape(A_KV_HEADS, A_GROUP), BLK, axis=1).reshape(
        A_KV_HEADS, 1, rows_a, 1))
    sink_a = jnp.broadcast_to(sinks.reshape(A_KV_HEADS, A_GROUP, 1), (A_KV_HEADS, A_GROUP, BLK)).reshape(
        A_KV_HEADS, 1, rows_a, 1)
    a_cfg = dict(scale=scale_h, n_back=A_WINDOW - 1, bps=nq)
    oa, lse_a = _band_fwd(qa, ka, va, slope_a, sink_a, name="swa_fwd", **a_cfg)
    b_in, b_cfg, b_out, b_lse = [], [], [], []
    base = A_Q_W + 2 * A_KV_W
    for gi, (window, dil) in enumerate(B_PATTERNS):
        blk = h_e[:, base + gi * 3 * B_W: base + (gi + 1) * 3 * B_W].reshape(s, 3, B_HEADS, HEAD_DIM)
        qs, ks, vs = (_to_strided(blk[:, n].transpose(1, 0, 2), dil) for n in range(3))
        slope = jnp.asarray(np.broadcast_to((_alibi(B_HEADS) * dil).reshape(1, B_HEADS, 1, 1), (1, B_HEADS, BLK, 1)))
        ins = (qs[None], _lead_block(ks)[None], _lead_block(vs)[None], slope)
        cfg = dict(scale=scale_h, n_back=window // dil, bps=nq // dil)
        o, lse = _band_fwd(*ins, None, name=f"dil{gi}_fwd", **cfg)
        b_in.append(ins)
        b_cfg.append(cfg)
        b_out.append(_from_strided(o[0], dil))
        b_lse.append(_from_strided(lse[0], dil))
    ob, lse_b = _merge(b_out, b_lse, name="dil_merge")
    y_e = bf(jnp.concatenate([_unstack_rows(oa, nq), _unheads(ob)], axis=1))
    mixed = _mm(y_e, w_even_out, name="even_out_fwd")
    tok = comm.sync("even_out", [mixed])
    x0n, x0nb, xh1_0, r1_0 = _ln_fwd(x0, mixed, ln1_g[0], ln1_b[0], name="ln1_fwd_l0", deps=tok)
    comm.sync("ln1_l0", [x0nb])
    w_mlp0 = comm.weights("mlp0")
    w1_0, w2_0 = w_mlp0["mlp_w1_0"], _rows(w_mlp0["mlp_w2_0"])
    act0, hid0 = _mm(x0nb, w1_0, out_dtypes=(MXU_DTYPE, MXU_DTYPE), epilogue=_relu_sq, name="mlp1_fwd_l0", **blocks)
    mlp = _mm(hid0, w2_0, name="mlp2_fwd_l0")
    x1, x1b, xh2_0, r2_0 = _ln_fwd(x0n, mlp, ln2_g[0], ln2_b[0], name="ln2_fwd_l0")

    tok = comm.sync("ln2_l0", [x1b])
    w_odd = comm.weights("odd")
    w_odd_in, w_uq, w_ukv, w_odd_out = (w_odd["odd_w_in"], _columns(w_odd["odd_w_uq"]), _columns(w_odd["odd_w_ukv"]),
                                        _rows(w_odd["odd_w_out"]))
    odd_cp = w_odd_in.shape[2]
    gq, gkv = comm.norm_gains()
    h_o = _drop_block_pad(_mm(x1b, w_odd_in, name="odd_in_fwd", deps=tok, **blocks), odd_c)
    qkv_c = bf(h_o[:, :3 * C_W])
    oc, sb_total = _sb_fwd(qkv_c, heads=C_HEADS, scale=scale_h, name="sb_fwd")
    o_cq, o_ckv, o_kr = 3 * C_W, 3 * C_W + D_Q_RANK, 3 * C_W + D_Q_RANK + D_KV_RANK
    cq, ckv, kr = h_o[:, o_cq:o_ckv], h_o[:, o_ckv:o_kr], h_o[:, o_kr:o_kr + D_ROPE]
    ncq, rq = _rms_fwd(cq, gq, name="rms_q_fwd")
    nckv, rkv = _rms_fwd(ckv, gkv, name="rms_kv_fwd")
    lane_pad = LANES - D_NOPE - D_ROPE
    w_uq = jnp.pad(w_uq.reshape(D_Q_RANK, D_HEADS, D_NOPE + D_ROPE), ((0, 0), (0, 0), (0, lane_pad))).reshape(
        D_Q_RANK, D_HEADS * LANES)
    qd = _mm(ncq, w_uq, name="uq_fwd")
    kvd = _mm(nckv, w_ukv, out_dtypes=(MXU_DTYPE,), name="ukv_fwd")
    rope_t = _rope_tables(s, inverse=False)
    krp = _rope(jnp.pad(kr, ((0, 0), (D_NOPE, lane_pad)))[None], rope_t, out_dtype=MXU_DTYPE, name="rope_k_fwd")[0]
    od, lse_d = _mla_fwd(qd, kvd, krp, rope_t, scale=scale_d, name="mla_fwd")
    y_o = bf(jnp.concatenate([oc, od], axis=1))
    mixed = _mm(y_o, w_odd_out, name="odd_out_fwd")
    tok = comm.sync("odd_out", [mixed])
    x1n, x1nb, xh1_1, r1_1 = _ln_fwd(x1, mixed, ln1_g[1], ln1_b[1], name="ln1_fwd_l1", deps=tok)
    comm.sync("ln1_l1", [x1nb])
    w_mlp1 = comm.weights("mlp1")
    w1_1, w2_1 = w_mlp1["mlp_w1_1"], _rows(w_mlp1["mlp_w2_1"])
    act1, hid1 = _mm(x1nb, w1_1, out_dtypes=(MXU_DTYPE, MXU_DTYPE), epilogue=_relu_sq, name="mlp1_fwd_l1", **blocks)
    mlp = _mm(hid1, w2_1, name="mlp2_fwd_l1")
    y, _, xh2_1, r2_1 = _ln_fwd(x1n, mlp, ln2_g[1], ln2_b[1], name="ln2_fwd_l1")
    dy, loss_vec = _loss_head(y, target, name="loss_head")

    def mlp_block_bwd(g_out, layer, w1, w2, xh2, r2, xh1, r1, act, hid, xnb):
        du2, du2b, dg2, db2 = _ln_bwd(g_out, xh2, r2, ln2_g[layer], name=f"ln2_bwd_l{layer}")
        dpre = _mm(du2b, w2, nt=True, out_dtypes=(MXU_DTYPE,), epilogue=_relu_sq_grad, extra=act, name=f"mlp2_dx_l{layer}")
        tok = comm.sync("mlp2_dx_l0", [dpre]) if layer == 0 else []
        dw2 = _mm(hid.T, du2b, out_dtypes=(BF16,), name=f"mlp2_dw_l{layer}", deps=tok)
        dw1 = _mm(xnb.T, dpre, out_blocks=True, out_dtypes=(BF16,), name=f"mlp1_dw_l{layer}")
        comm.grads(f"mlp{layer}", [(f"mlp_w1_{layer}", dw1), (f"mlp_w2_{layer}", dw2.reshape(N_DEV, -1, d))])
        tok = comm.sync(f"dw_l{layer}", [dw1])
        dxn = _mm(dpre, w1, nt=True, epilogue=_add_alpha, extra=du2, name=f"mlp1_dx_l{layer}", deps=tok, **blocks)
        du1, du1b, dg1, db1 = _ln_bwd(dxn, xh1, r1, ln1_g[layer], name=f"ln1_bwd_l{layer}")
        return du1, du1b, comm.sync(f"ln1_bwd_l{layer}", [du1b]), (dg1, db1, dg2, db2)

    du1, du1b, tok, ln_1 = mlp_block_bwd(dy, 1, w1_1, w2_1, xh2_1, r2_1, xh1_1, r1_1, act1, hid1, x1nb)
    d_odd_out = _mm(y_o.T, du1b, out_dtypes=(BF16,), name="odd_out_dw", deps=tok).reshape(N_DEV, -1, d)
    dy_o = _mm(du1b, w_odd_out, nt=True, name="odd_out_dx")
    dqc, dkc, dvc = _sb_bwd(qkv_c, dy_o, sb_total, heads=C_HEADS, scale=scale_h, name="sb_bwd")
    dqd, dkvd, dkr_pairs = _mla_bwd(qd, kvd, krp, rope_t, dy_o, od, lse_d, do_block0=C_W // LANES, scale=scale_d,
                                    name="mla_bwd")
    _, dkr_sum = _rope(dkr_pairs, _rope_tables(s, inverse=True), out_dtype=F32, head_sum=True, name="rope_k_bwd")
    dqd, dkvd = bf(dqd), bf(dkvd)
    d_uq = _mm(ncq.T, dqd, out_dtypes=(BF16,), name="uq_dw").reshape(D_Q_RANK, D_HEADS, LANES)[:, :, :D_NOPE + D_ROPE].reshape(
        D_Q_RANK, D_HEADS * (D_NOPE + D_ROPE))
    dncq = _mm(dqd, w_uq, nt=True, name="uq_dx")
    d_ukv = _mm(nckv.T, dkvd, out_dtypes=(BF16,), name="ukv_dw")
    dnckv = _mm(dkvd, w_ukv, nt=True, name="ukv_dx")
    dcq, dgq = _rms_bwd(dncq, cq, rq, gq, name="rms_q_bwd")
    dckv, dgkv = _rms_bwd(dnckv, ckv, rkv, gkv, name="rms_kv_bwd")
    dh_o = _add_block_pad(bf(jnp.concatenate(
        [dqc, dkc, dvc, dcq, dckv, dkr_sum[:, D_NOPE:D_NOPE + D_ROPE]], axis=1)), odd_cp)
    d_odd_in = _mm(x1b.T, dh_o, out_blocks=True, out_dtypes=(BF16,), name="odd_in_dw")
    comm.grads("odd", [("odd_w_in", d_odd_in), ("odd_w_uq", _by_column_block(d_uq)),
                       ("odd_w_ukv", _by_column_block(d_ukv)), ("odd_w_out", d_odd_out)])
    tok = comm.sync("odd_in_dw", [d_odd_in])
    dx1 = _mm(dh_o, w_odd_in, nt=True, epilogue=_add_alpha, extra=du1, name="odd_in_dx", deps=tok, **blocks)

    du1, du1b, tok, ln_0 = mlp_block_bwd(dx1, 0, w1_0, w2_0, xh2_0, r2_0, xh1_0, r1_0, act0, hid0, x0nb)
    d_even_out = _mm(y_e.T, du1b, out_dtypes=(BF16,), name="even_out_dw", deps=tok)
    dy_e = _mm(du1b, w_even_out, nt=True, name="even_out_dx")
    doa, dob = _stack_rows(dy_e[:, :A_Q_W], nq), _heads(dy_e[:, A_Q_W:], B_HEADS)
    dqa, dka, dva, dsink = _band_bwd(qa, ka, va, doa, oa, lse_a, slope_a, sink_a, name="swa_bwd", **a_cfg)
    pieces = [_unstack_rows(dqa, nq), _unheads(dka[:, 0, BLK:]), _unheads(dva[:, 0, BLK:])]
    for gi, (_, dil) in enumerate(B_PATTERNS):
        qs, ks, vs, slope = b_in[gi]
        dq, dk, dv = _band_bwd(qs, ks, vs, _to_strided(dob, dil)[None], _to_strided(ob, dil)[None],
                               _to_strided(lse_b, dil)[None], slope, None, name=f"dil{gi}_bwd", **b_cfg[gi])
        pieces += [_unheads(_from_strided(t, dil)) for t in (dq[0], dk[0, :, BLK:], dv[0, :, BLK:])]
    dh_e = _add_block_pad(bf(jnp.concatenate(pieces, axis=1)), even_cp)
    d_even_in = _mm(x0b.T, dh_e, out_blocks=True, out_dtypes=(BF16,), name="even_in_dw")
    comm.grads("even", [("even_w_in", d_even_in), ("even_w_out", _by_column_block(d_even_out))])
    tok = comm.sync("even_in_dw", [d_even_in])
    grad_x = _mm(dh_e, w_even_in, nt=True, epilogue=_add_alpha, extra=du1, name="even_in_dx", deps=tok, **blocks)
    tok = comm.sync("even_in_dx", [grad_x])

    ln = [jnp.concatenate([a, b], axis=0) for a, b in zip(ln_0, ln_1)]
    small = {"ln": ln, "sinks": dsink[:, :, 0].reshape(-1), "gq": dgq[0], "gkv": dgkv[0], "loss": loss_vec[0, :1]}
    return grad_x, small, tok


def kernel(x, even_w_in, even_sinks, even_w_out, odd_w_in, odd_q_norm_g, odd_kv_norm_g, odd_w_uq, odd_w_ukv, odd_w_out, ln1_g, ln1_b, mlp_w1, mlp_w2, ln2_g, ln2_b, loss_target, m_even_w_in, m_even_sinks, m_even_w_out, m_odd_w_in, m_odd_q_norm_g, m_odd_kv_norm_g, m_odd_w_uq, m_odd_w_ukv, m_odd_w_out, m_ln1_g, m_ln1_b, m_mlp_w1, m_mlp_w2, m_ln2_g, m_ln2_b, v_even_w_in, v_even_sinks, v_even_w_out, v_odd_w_in, v_odd_q_norm_g, v_odd_kv_norm_g, v_odd_w_uq, v_odd_w_ukv, v_odd_w_out, v_ln1_g, v_ln1_b, v_mlp_w1, v_mlp_w2, v_ln2_g, v_ln2_b):
    weights = dict(even_w_in=even_w_in, even_sinks=even_sinks, even_w_out=even_w_out, odd_w_in=odd_w_in,
                   odd_q_norm_g=odd_q_norm_g, odd_kv_norm_g=odd_kv_norm_g, odd_w_uq=odd_w_uq, odd_w_ukv=odd_w_ukv,
                   odd_w_out=odd_w_out, ln1_g=ln1_g, ln1_b=ln1_b, mlp_w1=mlp_w1, mlp_w2=mlp_w2, ln2_g=ln2_g, ln2_b=ln2_b)
    mom_m = dict(even_w_in=m_even_w_in, even_sinks=m_even_sinks, even_w_out=m_even_w_out, odd_w_in=m_odd_w_in,
                 odd_q_norm_g=m_odd_q_norm_g, odd_kv_norm_g=m_odd_kv_norm_g, odd_w_uq=m_odd_w_uq, odd_w_ukv=m_odd_w_ukv,
                 odd_w_out=m_odd_w_out, ln1_g=m_ln1_g, ln1_b=m_ln1_b, mlp_w1=m_mlp_w1, mlp_w2=m_mlp_w2, ln2_g=m_ln2_g, ln2_b=m_ln2_b)
    mom_v = dict(even_w_in=v_even_w_in, even_sinks=v_even_sinks, even_w_out=v_even_w_out, odd_w_in=v_odd_w_in,
                 odd_q_norm_g=v_odd_q_norm_g, odd_kv_norm_g=v_odd_kv_norm_g, odd_w_uq=v_odd_w_uq, odd_w_ukv=v_odd_w_ukv,
                 odd_w_out=v_odd_w_out, ln1_g=v_ln1_g, ln1_b=v_ln1_b, mlp_w1=v_mlp_w1, mlp_w2=v_mlp_w2, ln2_g=v_ln2_g, ln2_b=v_ln2_b)
    order = list(weights)
    n_q, n_kv = odd_q_norm_g.shape[1], odd_kv_norm_g.shape[1]

    def lane_padded(t):
        return jnp.pad(t, ((0, 0), (0, _lane_pad(t.shape[1]) - t.shape[1]))).astype(BF16)

    shards = {"even_w_in": lane_padded(even_w_in[0]), "even_w_out": even_w_out[0].astype(BF16),
              "mlp_w1_0": mlp_w1[0].astype(BF16), "mlp_w2_0": mlp_w2[0].astype(BF16),
              "odd_w_in": lane_padded(odd_w_in[0]), "odd_w_uq": odd_w_uq[0].astype(BF16),
              "odd_w_ukv": odd_w_ukv[0].astype(BF16), "odd_w_out": odd_w_out[0].astype(BF16),
              "mlp_w1_1": mlp_w1[1].astype(BF16), "mlp_w2_1": mlp_w2[1].astype(BF16)}
    gains = jnp.concatenate([odd_q_norm_g, odd_kv_norm_g, jnp.zeros((1, LANES - n_q - n_kv), F32)], axis=1)
    comm = _Exchanges(shards, gains, n_q, n_kv)
    dev = comm.dev

    grad_x, small, last_started = _local_step(x[0], loss_target[0], comm, even_sinks[0], ln1_g, ln1_b, ln2_g, ln2_b)

    small_parts = [t.reshape(-1) for t in small["ln"]] + [small["sinks"], small["gq"], small["gkv"], small["loss"]]
    small_sizes = [p.shape[0] for p in small_parts]
    n_small = sum(small_sizes)
    small_rows = -(-n_small // (8 * LANES)) * 8
    small_flat = jnp.concatenate(small_parts + [jnp.zeros((small_rows * LANES - n_small,), F32)]).reshape(small_rows, LANES)
    (small_all,) = _all_gather([small_flat], name="comm_small_gather")
    totals = _sum_devices(small_all, name="small_sum").reshape(-1)
    tot, off = [], 0
    for size in small_sizes:
        tot.append(totals[off:off + size])
        off += size
    grads = {}
    for i, n in enumerate(("ln1_g", "ln1_b", "ln2_g", "ln2_b")):
        grads[n] = tot[i].reshape(weights[n].shape)
    grads["even_sinks"] = tot[4].reshape(even_sinks.shape)
    grads["odd_q_norm_g"] = lax.dynamic_slice(tot[5], (dev * n_q,), (n_q,)).reshape(odd_q_norm_g.shape)
    grads["odd_kv_norm_g"] = lax.dynamic_slice(tot[6], (dev * n_kv,), (n_kv,)).reshape(odd_kv_norm_g.shape)
    loss = tot[7][0]

    delta, new_m, new_v = {}, {}, {}

    def update(n):
        g_list = [comm.reduced[f"{n}_0"], comm.reduced[f"{n}_1"]] if n.startswith("mlp") else [comm.reduced[n]]
        grads[n], delta[n], new_m[n], new_v[n] = _adamw(weights[n], g_list, mom_m[n], mom_v[n], name=f"adamw_{n}",
                                                        deps=last_started)

    early = ("mlp_w1", "mlp_w2", "odd_w_in", "odd_w_uq", "odd_w_ukv", "odd_w_out")
    for n in early:
        update(n)
    small_names = [n for n in order if n not in ("even_w_in", "even_w_out", "odd_w_in", "odd_w_uq", "odd_w_ukv", "odd_w_out", "mlp_w1", "mlp_w2")]
    n_sm = sum(weights[n].size for n in small_names)
    sm_rows = -(-n_sm // (8 * LANES)) * 8

    def pack_small(group):
        flat = [group[n].reshape(-1) for n in small_names]
        return jnp.concatenate(flat + [jnp.zeros((sm_rows * LANES - n_sm,), F32)]).reshape(1, sm_rows, LANES)

    res = _adamw(pack_small(weights), [pack_small(grads)[0]], pack_small(mom_m), pack_small(mom_v), name="adamw_small",
                 deps=last_started)
    off = 0
    for n in small_names:
        size = weights[n].size
        delta[n], new_m[n], new_v[n] = (t.reshape(-1)[off:off + size].reshape(weights[n].shape) for t in res[1:])
        off += size
    comm.sync("finish", [new_v[n] for n in early] + [res[0]])
    update("even_w_in")
    update("even_w_out")

    return (loss, grad_x[None], *[grads[n] for n in order], *[delta[n] for n in order],
            *[new_m[n] for n in order], *[new_v[n] for n in order])
```

```python
import math

import jax
import jax.numpy as jnp
import numpy as np
from jax import lax
from jax.experimental import pallas as pl
from jax.experimental.pallas import tpu as pltpu

F32 = jnp.float32
BF16 = jnp.bfloat16
MXU_DTYPE = BF16

HEAD_DIM = 64
A_Q_HEADS, A_KV_HEADS, A_WINDOW = 16, 2, 128
A_GROUP = A_Q_HEADS // A_KV_HEADS
B_HEADS = 8
B_PATTERNS = ((128, 1), (512, 4), (2048, 16))
C_HEADS = 16
D_HEADS, D_Q_RANK, D_KV_RANK, D_NOPE, D_ROPE, D_V = 16, 512, 256, 64, 32, 64
ROPE_BASE = 10000.0
LN_EPS, RMS_EPS = 1e-5, 1e-6
DEPTH = 2
ALPHA = (2 * DEPTH) ** 0.25
A_Q_W, A_KV_W, B_W = A_Q_HEADS * HEAD_DIM, A_KV_HEADS * HEAD_DIM, B_HEADS * HEAD_DIM
EVEN_IN = A_Q_W + 2 * A_KV_W + 3 * B_W * len(B_PATTERNS)
C_W = C_HEADS * HEAD_DIM
ODD_IN = 3 * C_W + D_Q_RANK + D_KV_RANK + D_ROPE
ADAM_LR, ADAM_B1, ADAM_B2, ADAM_EPS, ADAM_WD, ADAM_STEP = 0.001, 0.9, 0.999, 1e-08, 0.01, 10

N_DEV = 8
LANES = 128
BLK = 128
CAUSAL_TILE = 512
CUM_CHUNK = 256
NEG = -1e30
VMEM_LIMIT = 48 * 1024 * 1024

NN = ((1,), (0,))
NT = ((1,), (1,))
TN = ((0,), (0,))
MESH = pl.DeviceIdType.MESH
ANY = pl.BlockSpec(memory_space=pl.ANY)
HBM_SPEC = pl.BlockSpec(memory_space=pltpu.HBM)
SEM_SPEC = pl.BlockSpec(memory_space=pltpu.SEMAPHORE)
DATAFLOW_EFFECT = pltpu.SideEffectType.DATAFLOW_SIDE_EFFECTING


def _dot(a, b, dims):
    return lax.dot_general(a, b, (dims, ((), ())), preferred_element_type=F32)


def _bdot(a, b, dims):
    return jnp.stack([_dot(a[n], b[n], dims) for n in range(a.shape[0])])


def _params(*sem):
    return pltpu.CompilerParams(dimension_semantics=tuple(sem), vmem_limit_bytes=VMEM_LIMIT)


def _pick(n, cap, mult=LANES):
    if n <= cap:
        return n
    for t in range(cap - cap % mult, 0, -mult):
        if n % t == 0:
            return t
    raise ValueError(f"no tile for {n}")


def _lane_pad(c):
    return -(-c // LANES) * LANES


def _mm(a, b, *, name, nt=False, ta=False, b_blocks=False, out_blocks=False, out_dtypes=(F32,), epilogue=None, extra=None, deps=()):
    m, k = a.shape[::-1] if ta else a.shape
    if b_blocks:
        nb, kin, c = b.shape
        n = kin if nt else nb * c
        k_full = nb * c if nt else kin
    else:
        n, k_full = (b.shape if nt else b.shape[::-1])
    assert k == k_full, (a.shape, b.shape, nt, b_blocks)
    tm = _pick(m, 1024, 8)
    if b_blocks and not nt:
        tn, tk = c, _pick(k, 3072)
    elif b_blocks:
        per_step = max(g for g in (1, 2, 4, 8) if g * c <= 2048)
        tn, tk = _pick(n, 1024), per_step * c
    elif out_blocks:
        tn, tk = n // N_DEV, _pick(k, 3072)
    else:
        tn, tk = _pick(n, 512), _pick(k, 3072)
        if k > tk:
            tn, tk = _pick(n, 1024), _pick(k, 2048)
    nk = k // tk
    n_out = len(out_dtypes)

    def body(*refs):
        a_ref, b_ref = refs[0], refs[1]
        e_ref = refs[2] if extra is not None else None
        first_out = 2 + (extra is not None) + len(deps)
        out_refs = refs[first_out:first_out + n_out]

        def finish(acc):
            e = None if e_ref is None else e_ref[...]
            outs = (acc,) if epilogue is None else epilogue(acc, e)
            for r, o in zip(out_refs, outs):
                r[...] = o.astype(r.dtype).reshape(r.shape)

        if b_blocks and nt:
            part = _dot(a_ref[:, :c], b_ref[0], NT)
            for blk in range(1, per_step):
                part += _dot(a_ref[:, blk * c:(blk + 1) * c], b_ref[blk], NT)
        elif ta:
            part = _dot(a_ref[...], b_ref[...], TN)
        else:
            part = _dot(a_ref[...], b_ref[0] if b_blocks else b_ref[...], NT if nt else NN)
        if nk == 1:
            finish(part)
        else:
            acc_ref = refs[first_out + n_out]
            kk = pl.program_id(2)

            @pl.when(kk == 0)
            def _():
                acc_ref[...] = part

            @pl.when(kk > 0)
            def _():
                acc_ref[...] += part

            @pl.when(kk == nk - 1)
            def _():
                finish(acc_ref[...])

    if b_blocks and not nt:
        b_spec = pl.BlockSpec((1, tk, tn), lambda i, j, kk: (j, kk, 0))
    elif b_blocks:
        b_spec = pl.BlockSpec((per_step, tn, c), lambda i, j, kk: (kk, j, 0))
    elif nt:
        b_spec = pl.BlockSpec((tn, tk), lambda i, j, kk: (j, kk))
    else:
        b_spec = pl.BlockSpec((tk, tn), lambda i, j, kk: (kk, j))
    a_spec = pl.BlockSpec((tk, tm), lambda i, j, kk: (kk, i)) if ta else pl.BlockSpec((tm, tk), lambda i, j, kk: (i, kk))
    in_specs = [a_spec, b_spec]
    ins = [a.astype(MXU_DTYPE), b.astype(MXU_DTYPE)]
    if extra is not None:
        in_specs.append(pl.BlockSpec((tm, tn), lambda i, j, kk: (i, j)))
        ins.append(extra)
    in_specs += [ANY] * len(deps)
    ins += list(deps)
    if out_blocks:
        out_shape = tuple(jax.ShapeDtypeStruct((N_DEV, m, tn), d) for d in out_dtypes)
        out_specs = tuple(pl.BlockSpec((1, tm, tn), lambda i, j, kk: (j, i, 0)) for _ in out_dtypes)
    else:
        out_shape = tuple(jax.ShapeDtypeStruct((m, n), d) for d in out_dtypes)
        out_specs = tuple(pl.BlockSpec((tm, tn), lambda i, j, kk: (i, j)) for _ in out_dtypes)
    outs = pl.pallas_call(
        body,
        out_shape=out_shape,
        grid=(m // tm, n // tn, nk),
        in_specs=in_specs,
        out_specs=out_specs,
        scratch_shapes=[pltpu.VMEM((tm, tn), F32)] if nk > 1 else [],
        compiler_params=_params("parallel", "parallel", "arbitrary"),
        name=name,
    )(*ins)
    return outs[0] if n_out == 1 else outs


def _relu_sq(acc, _):
    act = jnp.maximum(acc, 0.0)
    return act, act * act


def _relu_sq_grad(acc, act):
    return (acc * (2.0 * act.astype(F32)),)


def _add_alpha(acc, du):
    return (acc + ALPHA * du,)


def _ln_fwd(x, mixed, g, b, *, name, deps=()):
    s, d = x.shape
    tr = _pick(s, 256, 8)

    def body(x_ref, m_ref, g_ref, b_ref, *rest):
        y_ref, yb_ref, xh_ref, r_ref = rest[len(deps):]
        u = ALPHA * x_ref[...] + m_ref[...]
        mu = jnp.mean(u, axis=-1, keepdims=True)
        xc = u - mu
        var = jnp.mean(xc * xc, axis=-1, keepdims=True)
        r = lax.rsqrt(var + LN_EPS)
        xh = xc * r
        y = xh * g_ref[...] + b_ref[...]
        y_ref[...] = y
        yb_ref[...] = y.astype(MXU_DTYPE)
        xh_ref[...] = xh
        r_ref[...] = r

    row = pl.BlockSpec((tr, d), lambda i: (i, 0))
    vec = pl.BlockSpec((1, d), lambda i: (0, 0))
    return pl.pallas_call(
        body,
        out_shape=(jax.ShapeDtypeStruct((s, d), F32), jax.ShapeDtypeStruct((s, d), MXU_DTYPE),
                   jax.ShapeDtypeStruct((s, d), F32), jax.ShapeDtypeStruct((s, 1), F32)),
        grid=(s // tr,),
        in_specs=[row, row, vec, vec] + [ANY] * len(deps),
        out_specs=(row, row, row, pl.BlockSpec((tr, 1), lambda i: (i, 0))),
        compiler_params=_params("parallel"),
        name=name,
    )(x, mixed, g.reshape(1, d), b.reshape(1, d), *deps)


def _ln_bwd(dy, xh, r, g, *, name):
    s, d = dy.shape
    tr = _pick(s, 256, 8)

    def body(dy_ref, xh_ref, r_ref, g_ref, du_ref, dub_ref, dg_ref, db_ref):
        dyv, xhv = dy_ref[...], xh_ref[...]
        dxh = dyv * g_ref[...]
        c1 = jnp.mean(dxh, axis=-1, keepdims=True)
        c2 = jnp.mean(dxh * xhv, axis=-1, keepdims=True)
        du = r_ref[...] * (dxh - c1 - xhv * c2)
        du_ref[...] = du
        dub_ref[...] = du.astype(MXU_DTYPE)

        @pl.when(pl.program_id(0) == 0)
        def _():
            dg_ref[...] = jnp.zeros_like(dg_ref)
            db_ref[...] = jnp.zeros_like(db_ref)

        dg_ref[...] += jnp.sum(dyv * xhv, axis=0, keepdims=True)
        db_ref[...] += jnp.sum(dyv, axis=0, keepdims=True)

    row = pl.BlockSpec((tr, d), lambda i: (i, 0))
    vec = pl.BlockSpec((1, d), lambda i: (0, 0))
    return pl.pallas_call(
        body,
        out_shape=(jax.ShapeDtypeStruct((s, d), F32), jax.ShapeDtypeStruct((s, d), MXU_DTYPE),
                   jax.ShapeDtypeStruct((1, d), F32), jax.ShapeDtypeStruct((1, d), F32)),
        grid=(s // tr,),
        in_specs=[row, row, pl.BlockSpec((tr, 1), lambda i: (i, 0)), vec],
        out_specs=(row, row, vec, vec),
        compiler_params=_params("arbitrary"),
        name=name,
    )(dy, xh, r, g.reshape(1, d))


def _rms_fwd(x, g, *, name):
    s, d = x.shape
    tr = _pick(s, 512, 8)

    def body(x_ref, g_ref, y_ref, r_ref):
        xv = x_ref[...]
        r = lax.rsqrt(jnp.mean(xv * xv, axis=-1, keepdims=True) + RMS_EPS)
        y_ref[...] = (xv * r * g_ref[...]).astype(y_ref.dtype)
        r_ref[...] = r

    return pl.pallas_call(
        body,
        out_shape=(jax.ShapeDtypeStruct((s, d), MXU_DTYPE), jax.ShapeDtypeStruct((s, 1), F32)),
        grid=(s // tr,),
        in_specs=[pl.BlockSpec((tr, d), lambda i: (i, 0)), pl.BlockSpec((1, d), lambda i: (0, 0))],
        out_specs=(pl.BlockSpec((tr, d), lambda i: (i, 0)), pl.BlockSpec((tr, 1), lambda i: (i, 0))),
        compiler_params=_params("parallel"),
        name=name,
    )(x, g.reshape(1, d))


def _rms_bwd(dy, x, r, g, *, name):
    s, d = x.shape
    tr = _pick(s, 512, 8)

    def body(dy_ref, x_ref, r_ref, g_ref, dx_ref, dg_ref):
        dyv, rv = dy_ref[...], r_ref[...]
        xn = x_ref[...] * rv
        dxn = dyv * g_ref[...]
        dx_ref[...] = rv * (dxn - xn * jnp.mean(dxn * xn, axis=-1, keepdims=True))

        @pl.when(pl.program_id(0) == 0)
        def _():
            dg_ref[...] = jnp.zeros_like(dg_ref)

        dg_ref[...] += jnp.sum(dyv * xn, axis=0, keepdims=True)

    row = pl.BlockSpec((tr, d), lambda i: (i, 0))
    vec = pl.BlockSpec((1, d), lambda i: (0, 0))
    return pl.pallas_call(
        body,
        out_shape=(jax.ShapeDtypeStruct((s, d), F32), jax.ShapeDtypeStruct((1, d), F32)),
        grid=(s // tr,),
        in_specs=[row, row, pl.BlockSpec((tr, 1), lambda i: (i, 0)), vec],
        out_specs=(row, vec),
        compiler_params=_params("arbitrary"),
        name=name,
    )(dy, x, r, g.reshape(1, d))


def _rope_tables(s, inverse):
    inv_freq = ROPE_BASE ** (-jnp.arange(0, D_ROPE, 2, dtype=F32) / D_ROPE)
    ang = jnp.arange(s, dtype=F32)[:, None] * inv_freq[None, :]
    cos, sin = jnp.cos(ang), jnp.sin(ang)
    if inverse:
        sin = -sin
    half = D_ROPE // 2
    one, zero = jnp.ones((s, D_NOPE), F32), jnp.zeros((s, D_NOPE), F32)
    pad1, pad0 = jnp.ones((s, LANES - D_NOPE - D_ROPE), F32), jnp.zeros((s, LANES - D_NOPE - D_ROPE), F32)
    zh = jnp.zeros((s, half), F32)
    c = jnp.concatenate([one, cos, cos, pad1], axis=1)
    s_lo = jnp.concatenate([zero, -sin, zh, pad0], axis=1)
    s_hi = jnp.concatenate([zero, zh, sin, pad0], axis=1)
    return c, s_lo, s_hi


def _rope(x, tables, *, out_dtype, head_sum=False, name):
    h, s, w = x.shape
    ts = _pick(s, 2048, 8)

    def body(x_ref, c_ref, lo_ref, hi_ref, y_ref, *sum_ref):
        y = _rotate(x_ref[0], c_ref[...], lo_ref[...], hi_ref[...])
        y_ref[0] = y.astype(y_ref.dtype)
        if head_sum:
            @pl.when(pl.program_id(1) == 0)
            def _():
                sum_ref[0][...] = jnp.zeros_like(sum_ref[0])

            sum_ref[0][...] += y

    tab = pl.BlockSpec((ts, w), lambda i, hh: (i, 0))
    blk = pl.BlockSpec((1, ts, w), lambda i, hh: (hh, i, 0))
    out_shape = [jax.ShapeDtypeStruct((h, s, w), out_dtype)]
    out_specs = [blk]
    if head_sum:
        out_shape.append(jax.ShapeDtypeStruct((s, w), F32))
        out_specs.append(tab)
    res = pl.pallas_call(
        body,
        out_shape=tuple(out_shape),
        grid=(s // ts, h),
        in_specs=[blk, tab, tab, tab],
        out_specs=tuple(out_specs),
        compiler_params=_params("parallel", "arbitrary"),
        name=name,
    )(x, *tables)
    return res if head_sum else res[0]


def _band_scores(q, kw, slope, i, *, scale, n_back, bps):
    b, r, _ = q.shape
    sc = _bdot(q, kw, NT) * scale
    shape = (b, r, 2 * BLK)
    row = lax.broadcasted_iota(jnp.int32, shape, 1) & (BLK - 1)
    col = lax.broadcasted_iota(jnp.int32, shape, 2)
    rel = BLK + row - col
    first_col = jnp.where(i % bps == 0, BLK, 0)
    valid = (rel >= 0) & (rel <= n_back) & (col >= first_col)
    return jnp.where(valid, sc - slope * rel.astype(F32), NEG)


def _band_fwd(q, k, v, slope, sink, *, scale, n_back, bps, name):
    g, b, rows, dh = q.shape
    r = slope.shape[2]
    nq = rows // r
    skv = k.shape[2]
    use_sink = sink is not None

    def body(*refs):
        q_ref, k_ref, v_ref, slope_ref = refs[:4]
        sink_ref = refs[4] if use_sink else None
        o_ref, lse_ref = refs[4 + use_sink:]
        i = pl.program_id(1)
        off = pl.multiple_of(i * BLK, BLK)
        kw = k_ref[0, :, pl.ds(off, 2 * BLK), :]
        vw = v_ref[0, :, pl.ds(off, 2 * BLK), :]
        sc = _band_scores(q_ref[0], kw, slope_ref[0], i, scale=scale, n_back=n_back, bps=bps)
        m = jnp.max(sc, axis=-1, keepdims=True)
        if use_sink:
            m = jnp.maximum(m, sink_ref[0])
        p = jnp.exp(sc - m)
        l = jnp.sum(p, axis=-1, keepdims=True)
        if use_sink:
            l = l + jnp.exp(sink_ref[0] - m)
        o_ref[0] = _bdot(p.astype(MXU_DTYPE), vw, NN) / l
        lse_ref[0] = m + jnp.log(l)

    qspec = pl.BlockSpec((1, b, r, dh), lambda gg, i: (gg, 0, i, 0))
    kspec = pl.BlockSpec((1, b, skv, dh), lambda gg, i: (gg, 0, 0, 0))
    rspec = pl.BlockSpec((1, b, r, 1), lambda gg, i: (gg, 0, 0, 0))
    ins = [q, k, v, slope] + ([sink] if use_sink else [])
    return pl.pallas_call(
        body,
        out_shape=(jax.ShapeDtypeStruct((g, b, rows, dh), F32), jax.ShapeDtypeStruct((g, b, rows, 1), F32)),
        grid=(g, nq),
        in_specs=[qspec, kspec, kspec, rspec] + ([rspec] if use_sink else []),
        out_specs=(qspec, pl.BlockSpec((1, b, r, 1), lambda gg, i: (gg, 0, i, 0))),
        compiler_params=_params("parallel", "arbitrary"),
        name=name,
    )(*ins)


def _band_bwd(q, k, v, do, o, lse, slope, sink, *, scale, n_back, bps, name):
    g, b, rows, dh = q.shape
    r = slope.shape[2]
    nq = rows // r
    skv = k.shape[2]
    use_sink = sink is not None
    stacked = r // BLK

    def body(*refs):
        q_ref, k_ref, v_ref, do_ref, o_ref, lse_ref, slope_ref = refs[:7]
        sink_ref = refs[7] if use_sink else None
        dq_ref, dk_ref, dv_ref = refs[7 + use_sink:10 + use_sink]
        i = pl.program_id(1)

        @pl.when(i == 0)
        def _():
            dk_ref[...] = jnp.zeros_like(dk_ref)
            dv_ref[...] = jnp.zeros_like(dv_ref)

        off = pl.multiple_of(i * BLK, BLK)
        qb = q_ref[0]
        kw = k_ref[0, :, pl.ds(off, 2 * BLK), :]
        vw = v_ref[0, :, pl.ds(off, 2 * BLK), :]
        dof = do_ref[0]
        dob = dof.astype(MXU_DTYPE)
        lse_b = lse_ref[0]
        delta = jnp.sum(dof * o_ref[0], axis=-1, keepdims=True)
        sc = _band_scores(qb, kw, slope_ref[0], i, scale=scale, n_back=n_back, bps=bps)
        p = jnp.exp(sc - lse_b)
        ds = (p * (_bdot(dob, vw, NT) - delta) * scale).astype(MXU_DTYPE)
        dq_ref[0] = _bdot(ds, kw, NN)
        dk_ref[0, :, pl.ds(off, 2 * BLK), :] += _bdot(ds, qb, TN)
        dv_ref[0, :, pl.ds(off, 2 * BLK), :] += _bdot(p.astype(MXU_DTYPE), dob, TN)

        if use_sink:
            dsink_ref = refs[10 + use_sink]

            @pl.when(i == 0)
            def _():
                dsink_ref[...] = jnp.zeros_like(dsink_ref)

            contrib = -jnp.exp(sink_ref[0] - lse_b) * delta
            for n in range(stacked):
                part = jnp.sum(contrib[0, n * BLK:(n + 1) * BLK, :], axis=0, keepdims=True)
                dsink_ref[0, n:n + 1, :] += jnp.broadcast_to(part, (1, LANES))

    def qspec(w):
        return pl.BlockSpec((1, b, r, w), lambda gg, i: (gg, 0, i, 0))

    kspec = pl.BlockSpec((1, b, skv, dh), lambda gg, i: (gg, 0, 0, 0))
    rspec = pl.BlockSpec((1, b, r, 1), lambda gg, i: (gg, 0, 0, 0))
    ins = [q, k, v, do, o, lse, slope] + ([sink] if use_sink else [])
    in_specs = [qspec(dh), kspec, kspec, qspec(dh), qspec(dh), qspec(1), rspec] + ([rspec] if use_sink else [])
    out_shape = [jax.ShapeDtypeStruct((g, b, rows, dh), F32), jax.ShapeDtypeStruct((g, b, skv, dh), F32),
                 jax.ShapeDtypeStruct((g, b, skv, dh), F32)]
    out_specs = [qspec(dh), kspec, kspec]
    if use_sink:
        assert b == 1
        out_shape.append(jax.ShapeDtypeStruct((g, stacked, LANES), F32))
        out_specs.append(pl.BlockSpec((1, stacked, LANES), lambda gg, i: (gg, 0, 0)))
    return pl.pallas_call(
        body,
        out_shape=tuple(out_shape),
        grid=(g, nq),
        in_specs=in_specs,
        out_specs=tuple(out_specs),
        compiler_params=_params("parallel", "arbitrary"),
        name=name,
    )(*ins)


def _merge(outs, lses, *, name):
    h, s, dv = outs[0].shape
    ts = _pick(s, 512, 8)

    def body(o0, o1, o2, l0, l1, l2, ob_ref, lt_ref):
        a, b, c = l0[0], l1[0], l2[0]
        m = jnp.maximum(jnp.maximum(a, b), c)
        ea, eb, ec = jnp.exp(a - m), jnp.exp(b - m), jnp.exp(c - m)
        den = ea + eb + ec
        ob_ref[0] = (ea / den) * o0[0] + (eb / den) * o1[0] + (ec / den) * o2[0]
        lt_ref[0] = m + jnp.log(den)

    ospec = pl.BlockSpec((1, ts, dv), lambda hh, i: (hh, i, 0))
    lspec = pl.BlockSpec((1, ts, 1), lambda hh, i: (hh, i, 0))
    return pl.pallas_call(
        body,
        out_shape=(jax.ShapeDtypeStruct((h, s, dv), F32), jax.ShapeDtypeStruct((h, s, 1), F32)),
        grid=(h, s // ts),
        in_specs=[ospec] * 3 + [lspec] * 3,
        out_specs=(ospec, lspec),
        compiler_params=_params("parallel", "parallel"),
        name=name,
    )(*outs, *lses)


def _tile_iotas(t):
    return lax.broadcasted_iota(jnp.int32, (t, t), 0), lax.broadcasted_iota(jnp.int32, (t, t), 1)


def _rotate(x, c, s_lo, s_hi):
    half = D_ROPE // 2
    return x * c + pltpu.roll(x, LANES - half, 1) * s_lo + pltpu.roll(x, half, 1) * s_hi


def _mla_keys(kv_h, kr_t, first):
    return jnp.where(first, kv_h, kr_t)


def _mla_fwd(qd, kvd, krp, tables, *, scale, name):
    s = qd.shape[0]
    pairs = qd.shape[1] // (2 * LANES)
    t = min(CAUSAL_TILE, s)

    def body(q_ref, kv_ref, kr_ref, c_ref, lo_ref, hi_ref, o_ref, lse_ref):
        i = pl.program_id(1)
        first = lax.broadcasted_iota(jnp.int32, (1, LANES), 1) < HEAD_DIM
        tabs = (c_ref[...], lo_ref[...], hi_ref[...])
        q_heads = [_rotate(q_ref[:, hh * LANES:(hh + 1) * LANES], *tabs).astype(MXU_DTYPE) for hh in range(2)]

        def tile(j, carry, diagonal):
            off = pl.multiple_of(j * t, t)
            kr_t = kr_ref[pl.ds(off, t), :]
            out = []
            for hh in range(2):
                m, l, acc = carry[3 * hh:3 * hh + 3]
                kv_h = kv_ref[pl.ds(off, t), hh * LANES:(hh + 1) * LANES]
                sc = _dot(q_heads[hh], _mla_keys(kv_h, kr_t, first), NT) * scale
                if diagonal:
                    row, col = _tile_iotas(t)
                    sc = jnp.where(row >= col, sc, NEG)
                m_new = jnp.maximum(m, jnp.max(sc, axis=-1, keepdims=True))
                a = jnp.exp(m - m_new)
                p = jnp.exp(sc - m_new)
                out += [m_new, a * l + jnp.sum(p, axis=-1, keepdims=True), a * acc + _dot(p.astype(MXU_DTYPE), kv_h, NN)]
            return tuple(out)

        init = (jnp.full((t, 1), NEG, F32), jnp.zeros((t, 1), F32), jnp.zeros((t, LANES), F32)) * 2
        carry = lax.fori_loop(0, i, lambda j, c: tile(j, c, False), init)
        m0, l0, acc0, m1, l1, acc1 = tile(i, carry, True)
        o_ref[...] = jnp.where(first, pltpu.roll(acc0 / l0, HEAD_DIM, 1), acc1 / l1)
        lse_ref[0] = jnp.where(lax.broadcasted_iota(jnp.int32, (t, 2), 1) == 0, m0 + jnp.log(l0), m1 + jnp.log(l1))

    tab = pl.BlockSpec((t, LANES), lambda p, i: (i, 0))
    return pl.pallas_call(
        body,
        out_shape=(jax.ShapeDtypeStruct((s, pairs * LANES), F32), jax.ShapeDtypeStruct((pairs, s, 2), F32)),
        grid=(pairs, s // t),
        in_specs=[pl.BlockSpec((t, 2 * LANES), lambda p, i: (i, p)), pl.BlockSpec((s, 2 * LANES), lambda p, i: (0, p)),
                  pl.BlockSpec((s, LANES), lambda p, i: (0, 0)), tab, tab, tab],
        out_specs=(pl.BlockSpec((t, LANES), lambda p, i: (i, p)), pl.BlockSpec((1, t, 2), lambda p, i: (p, i, 0))),
        compiler_params=_params("parallel", "arbitrary"),
        name=name,
    )(qd, kvd, krp, *tables)


def _mla_bwd(qd, kvd, krp, tables, do, o, lse, *, do_block0, scale, name):
    s = qd.shape[0]
    pairs = qd.shape[1] // (2 * LANES)
    t = min(CAUSAL_TILE, s)

    def body(q_ref, kv_ref, kr_ref, c_ref, lo_ref, hi_ref, do_ref, o_ref, lse_ref, dq_ref, dkv_ref, dkr_ref):
        i = pl.program_id(1)

        @pl.when(i == 0)
        def _():
            dkv_ref[...] = jnp.zeros_like(dkv_ref)
            dkr_ref[...] = jnp.zeros_like(dkr_ref)

        first = lax.broadcasted_iota(jnp.int32, (1, LANES), 1) < HEAD_DIM
        tabs = (c_ref[...], lo_ref[...], hi_ref[...])
        q_heads = [_rotate(q_ref[:, hh * LANES:(hh + 1) * LANES], *tabs).astype(MXU_DTYPE) for hh in range(2)]
        dof = do_ref[...]
        prod = dof * o_ref[...]
        deltas = [jnp.sum(jnp.where(first, prod, 0.0), axis=-1, keepdims=True),
                  jnp.sum(jnp.where(first, 0.0, prod), axis=-1, keepdims=True)]
        do_heads = [jnp.where(first, 0.0, pltpu.roll(dof, HEAD_DIM, 1)).astype(MXU_DTYPE),
                    jnp.where(first, 0.0, dof).astype(MXU_DTYPE)]
        lses = [lse_ref[0][:, hh:hh + 1] for hh in range(2)]

        def tile(j, carry, diagonal):
            off = pl.multiple_of(j * t, t)
            kr_t = kr_ref[pl.ds(off, t), :]
            out, dkr_add = [], None
            for hh in range(2):
                kv_h = kv_ref[pl.ds(off, t), hh * LANES:(hh + 1) * LANES]
                k_h = _mla_keys(kv_h, kr_t, first)
                sc = _dot(q_heads[hh], k_h, NT) * scale
                if diagonal:
                    row, col = _tile_iotas(t)
                    sc = jnp.where(row >= col, sc, NEG)
                p = jnp.exp(sc - lses[hh])
                ds = (p * (_dot(do_heads[hh], kv_h, NT) - deltas[hh]) * scale).astype(MXU_DTYPE)
                dk_full = _dot(ds, q_heads[hh], TN)
                dv_full = _dot(p.astype(MXU_DTYPE), do_heads[hh], TN)
                dkv_ref[pl.ds(off, t), hh * LANES:(hh + 1) * LANES] += jnp.where(first, dk_full, dv_full)
                rot = jnp.where(first, 0.0, dk_full)
                dkr_add = rot if dkr_add is None else dkr_add + rot
                out.append(carry[hh] + _dot(ds, k_h, NN))
            dkr_ref[0, pl.ds(off, t), :] += dkr_add
            return tuple(out)

        zacc = jnp.zeros((t, LANES), F32)
        carry = lax.fori_loop(0, i, lambda j, c: tile(j, c, False), (zacc, zacc))
        dq_heads = tile(i, carry, True)
        for hh in range(2):
            dq_ref[:, hh * LANES:(hh + 1) * LANES] = _rotate(dq_heads[hh], tabs[0], -tabs[1], -tabs[2])

    tab = pl.BlockSpec((t, LANES), lambda p, i: (i, 0))
    qspec = pl.BlockSpec((t, 2 * LANES), lambda p, i: (i, p))
    kvspec = pl.BlockSpec((s, 2 * LANES), lambda p, i: (0, p))
    return pl.pallas_call(
        body,
        out_shape=(jax.ShapeDtypeStruct(qd.shape, F32), jax.ShapeDtypeStruct(kvd.shape, F32),
                   jax.ShapeDtypeStruct((pairs, s, LANES), F32)),
        grid=(pairs, s // t),
        in_specs=[qspec, kvspec, pl.BlockSpec((s, LANES), lambda p, i: (0, 0)), tab, tab, tab,
                  pl.BlockSpec((t, LANES), lambda p, i: (i, do_block0 + p)), pl.BlockSpec((t, LANES), lambda p, i: (i, p)),
                  pl.BlockSpec((1, t, 2), lambda p, i: (p, i, 0))],
        out_specs=(qspec, kvspec, pl.BlockSpec((1, s, LANES), lambda p, i: (p, 0, 0))),
        compiler_params=_params("parallel", "arbitrary"),
        name=name,
    )(qd, kvd, krp, *tables, do, o, lse)


def _split_cumsum(x, tri, terms=2):
    hi = x.astype(BF16)
    if terms == 1:
        return _dot(hi, tri, NN)
    lo = (x - hi.astype(F32)).astype(BF16)
    return _dot(hi, tri, NN) + _dot(lo, tri, NN)


def _chunked_cumsum(x, tri, run, *, reverse, sign, terms=2):
    c = tri.shape[0]
    n = x.shape[1] // c
    parts = [None] * n
    for idx in (reversed(range(n)) if reverse else range(n)):
        xc = x[:, idx * c:(idx + 1) * c]
        parts[idx] = sign * (run + _split_cumsum(xc, tri, terms))
        run = run + jnp.sum(xc, axis=-1, keepdims=True)
    return (parts[0] if n == 1 else jnp.concatenate(parts, axis=1)), run


def _sb_logs(z):
    e = jnp.exp(-jnp.abs(z))
    l1 = jnp.log(1.0 + e)
    return e, jnp.minimum(z, 0.0) - l1, -jnp.maximum(z, 0.0) - l1


def _pair_masks():
    first = lax.broadcasted_iota(jnp.int32, (1, LANES), 1) < HEAD_DIM
    m0 = first.astype(MXU_DTYPE)
    return first, (m0, 1 - m0)


def _sb_fwd(qkv, *, heads, scale, name):
    s = qkv.shape[0]
    pairs = heads * HEAD_DIM // LANES
    t = min(CAUSAL_TILE, s)
    cc = min(CUM_CHUNK, t)

    def body(q_ref, k_ref, v_ref, o_ref, t_ref):
        i = pl.program_id(1)
        first, masks = _pair_masks()
        q_heads = [q_ref[...] * m for m in masks]
        crow, ccol = _tile_iotas(cc)
        after = (crow > ccol).astype(BF16)

        def tile(j, carry, diagonal):
            off = pl.multiple_of(j * t, t)
            kb = k_ref[pl.ds(off, t), :]
            vb = v_ref[pl.ds(off, t), :]
            if diagonal:
                row, col = _tile_iotas(t)
                strict = row > col
            out = []
            for hh in range(2):
                run, acc = carry[2 * hh], carry[2 * hh + 1]
                z = _dot(q_heads[hh], kb, NT) * scale
                _, log_beta, log_keep = _sb_logs(z)
                if diagonal:
                    log_keep = jnp.where(strict, log_keep, 0.0)
                a, run = _chunked_cumsum(log_keep, after, run, reverse=True, sign=1.0)
                w = jnp.exp(log_beta + a)
                if diagonal:
                    w = jnp.where(strict, w, 0.0)
                out += [run, acc + _dot(w.astype(MXU_DTYPE), vb, NN)]
            return tuple(out)

        zero, zacc = jnp.zeros((t, 1), F32), jnp.zeros((t, LANES), F32)
        carry = tile(i, (zero, zacc, zero, zacc), True)
        run0, acc0, run1, acc1 = lax.fori_loop(0, i, lambda jj, c: tile(i - 1 - jj, c, False), carry)
        o_ref[...] = jnp.where(first, acc0, acc1)
        t_ref[0] = jnp.where(lax.broadcasted_iota(jnp.int32, (t, 2), 1) == 0, run0, run1)

    return pl.pallas_call(
        body,
        out_shape=(jax.ShapeDtypeStruct((s, heads * HEAD_DIM), F32), jax.ShapeDtypeStruct((pairs, s, 2), F32)),
        grid=(pairs, s // t),
        in_specs=[pl.BlockSpec((t, LANES), lambda p, i: (i, p)),
                  pl.BlockSpec((s, LANES), lambda p, i: (0, pairs + p)),
                  pl.BlockSpec((s, LANES), lambda p, i: (0, 2 * pairs + p))],
        out_specs=(pl.BlockSpec((t, LANES), lambda p, i: (i, p)), pl.BlockSpec((1, t, 2), lambda p, i: (p, i, 0))),
        compiler_params=_params("parallel", "arbitrary"),
        name=name,
    )(qkv, qkv, qkv)


def _sb_bwd(qkv, do, total, *, heads, scale, name):
    s = qkv.shape[0]
    pairs = heads * HEAD_DIM // LANES
    t = min(CAUSAL_TILE, s)
    cc = min(CUM_CHUNK, t)

    def body(q_ref, k_ref, v_ref, do_ref, t_ref, dq_ref, dk_ref, dv_ref):
        i = pl.program_id(1)

        @pl.when(i == 0)
        def _():
            dk_ref[...] = jnp.zeros_like(dk_ref)
            dv_ref[...] = jnp.zeros_like(dv_ref)

        first, masks = _pair_masks()
        q_heads = [q_ref[...] * m for m in masks]
        do_b = do_ref[...].astype(MXU_DTYPE)
        do_heads = [do_b * m for m in masks]
        tots = [t_ref[0][:, hh:hh + 1] for hh in range(2)]
        crow, ccol = _tile_iotas(cc)
        upto = (crow <= ccol).astype(BF16)
        before = (crow < ccol).astype(BF16)

        def tile(j, carry, diagonal):
            off = pl.multiple_of(j * t, t)
            kb = k_ref[pl.ds(off, t), :]
            vb = v_ref[pl.ds(off, t), :]
            if diagonal:
                row, col = _tile_iotas(t)
                strict = row > col
            out, dk_add, dv_add = [], None, None
            for hh in range(2):
                run_keep, run_g, dq_acc = carry[3 * hh:3 * hh + 3]
                z = _dot(q_heads[hh], kb, NT) * scale
                e, log_beta, log_keep = _sb_logs(z)
                if diagonal:
                    log_keep = jnp.where(strict, log_keep, 0.0)
                a, run_keep = _chunked_cumsum(log_keep, upto, run_keep - tots[hh], reverse=False, sign=-1.0)
                run_keep = run_keep + tots[hh]
                w = jnp.exp(log_beta + a)
                if diagonal:
                    w = jnp.where(strict, w, 0.0)
                g = w * _dot(do_heads[hh], vb, NT)
                prefix, run_g = _chunked_cumsum(g, before, run_g, reverse=False, sign=1.0, terms=1)
                pos = z >= 0.0
                dz = (g * jnp.where(pos, e, 1.0) - jnp.where(pos, 1.0, e) * prefix) * pl.reciprocal(1.0 + e, approx=True)
                if diagonal:
                    dz = jnp.where(strict, dz, 0.0)
                dz = (dz * scale).astype(MXU_DTYPE)
                dk_h = _dot(dz, q_heads[hh], TN)
                dv_h = _dot(w.astype(MXU_DTYPE), do_heads[hh], TN)
                dk_add = dk_h if dk_add is None else dk_add + dk_h
                dv_add = dv_h if dv_add is None else dv_add + dv_h
                out += [run_keep, run_g, dq_acc + _dot(dz, kb, NN)]
            dk_ref[pl.ds(off, t), :] += dk_add
            dv_ref[pl.ds(off, t), :] += dv_add
            return tuple(out)

        zero, zacc = jnp.zeros((t, 1), F32), jnp.zeros((t, LANES), F32)
        carry = lax.fori_loop(0, i, lambda j, c: tile(j, c, False), (zero, zero, zacc, zero, zero, zacc))
        res = tile(i, carry, True)
        dq_ref[...] = jnp.where(first, res[2], res[5])

    qspec = pl.BlockSpec((t, LANES), lambda p, i: (i, p))
    shp = jax.ShapeDtypeStruct((s, heads * HEAD_DIM), F32)
    return pl.pallas_call(
        body,
        out_shape=(shp, shp, shp),
        grid=(pairs, s // t),
        in_specs=[qspec, pl.BlockSpec((s, LANES), lambda p, i: (0, pairs + p)),
                  pl.BlockSpec((s, LANES), lambda p, i: (0, 2 * pairs + p)), qspec,
                  pl.BlockSpec((1, t, 2), lambda p, i: (p, i, 0))],
        out_specs=(qspec, pl.BlockSpec((s, LANES), lambda p, i: (0, p)), pl.BlockSpec((s, LANES), lambda p, i: (0, p))),
        compiler_params=_params("parallel", "arbitrary"),
        name=name,
    )(qkv, qkv, qkv, do, total)


def _loss_head(y, target, *, name):
    s, d = y.shape
    tr = _pick(s, 256, 8)

    def body(y_ref, t_ref, dy_ref, loss_ref):
        err = y_ref[...] - t_ref[...]
        dy_ref[...] = err * (1.0 / d)

        @pl.when(pl.program_id(0) == 0)
        def _():
            loss_ref[...] = jnp.zeros_like(loss_ref)

        per_tok = jnp.mean(err * err, axis=-1, keepdims=True)
        loss_ref[...] += 0.5 * jnp.sum(per_tok, axis=0, keepdims=True)

    row = pl.BlockSpec((tr, d), lambda i: (i, 0))
    return pl.pallas_call(
        body,
        out_shape=(jax.ShapeDtypeStruct((s, d), F32), jax.ShapeDtypeStruct((1, LANES), F32)),
        grid=(s // tr,),
        in_specs=[row, row],
        out_specs=(row, pl.BlockSpec((1, LANES), lambda i: (0, 0))),
        compiler_params=_params("arbitrary"),
        name=name,
    )(y, target)


def _adamw(w, grads, m, v, *, name, deps=()):
    nl, r, c = w.shape
    cp = grads[0].shape[1]
    tr = _pick(r, 256, 8)

    def body(*refs):
        w_ref, m_ref, v_ref = refs[:3]
        g_refs = refs[3:3 + nl]
        g_out, d_ref, m2_ref, v2_ref = refs[3 + nl + len(deps):]
        layer = pl.program_id(0)
        gv = g_refs[0][:, :c]
        for n in range(1, nl):
            gv = jnp.where(layer == n, g_refs[n][:, :c], gv)
        m2 = ADAM_B1 * m_ref[0] + (1.0 - ADAM_B1) * gv
        v2 = ADAM_B2 * v_ref[0] + (1.0 - ADAM_B2) * (gv * gv)
        m_hat = m2 / (1.0 - ADAM_B1 ** ADAM_STEP)
        v_hat = v2 / (1.0 - ADAM_B2 ** ADAM_STEP)
        g_out[0] = gv
        d_ref[0] = -ADAM_LR * (m_hat / (jnp.sqrt(v_hat) + ADAM_EPS) + ADAM_WD * w_ref[0])
        m2_ref[0] = m2
        v2_ref[0] = v2

    blk = pl.BlockSpec((1, tr, c), lambda l, i: (l, i, 0))
    gspec = pl.BlockSpec((tr, cp), lambda l, i: (i, 0))
    shp = jax.ShapeDtypeStruct((nl, r, c), F32)
    return pl.pallas_call(
        body,
        out_shape=(shp, shp, shp, shp),
        grid=(nl, r // tr),
        in_specs=[blk, blk, blk] + [gspec] * nl + [ANY] * len(deps),
        out_specs=(blk, blk, blk, blk),
        compiler_params=_params("parallel", "parallel"),
        name=name,
    )(w, m, v, *grads, *deps)


def _pair_sum(mine, recv, my_c, *, name):
    _, r, c = mine.shape
    tr = _pick(r, 512, 16)

    def body(c_ref, a_ref, b_ref, o_ref):
        o_ref[0] = (a_ref[0].astype(F32) + b_ref[0].astype(F32)).astype(o_ref.dtype)

    grid_spec = pltpu.PrefetchScalarGridSpec(
        num_scalar_prefetch=1,
        grid=(4, r // tr),
        in_specs=[pl.BlockSpec((1, tr, c), lambda kk, i, c_ref: (2 * kk + c_ref[0], i, 0)),
                  pl.BlockSpec((1, tr, c), lambda kk, i, c_ref: (kk, i, 0))],
        out_specs=pl.BlockSpec((1, tr, c), lambda kk, i, c_ref: (kk, i, 0)),
    )
    return pl.pallas_call(
        body,
        out_shape=jax.ShapeDtypeStruct((4, r, c), mine.dtype),
        grid_spec=grid_spec,
        compiler_params=_params("parallel", "parallel"),
        name=name,
    )(my_c.reshape(1).astype(jnp.int32), mine, recv)


def _final_sum(partial, recv, my_chip, *, name):
    _, r, c = partial.shape
    tr = _pick(r, 512, 16)

    def body(chip_ref, p_ref, r0, r1, r2, o_ref):
        o_ref[...] = ((p_ref[0].astype(F32) + r0[0].astype(F32)) + r1[0].astype(F32)) + r2[0].astype(F32)

    def slot(n):
        return pl.BlockSpec((1, tr, c), lambda i, chip_ref: (n, i, 0))

    grid_spec = pltpu.PrefetchScalarGridSpec(
        num_scalar_prefetch=1,
        grid=(r // tr,),
        in_specs=[pl.BlockSpec((1, tr, c), lambda i, chip_ref: (chip_ref[0], i, 0)), slot(0), slot(1), slot(2)],
        out_specs=pl.BlockSpec((tr, c), lambda i, chip_ref: (i, 0)),
    )
    return pl.pallas_call(
        body,
        out_shape=jax.ShapeDtypeStruct((r, c), F32),
        grid_spec=grid_spec,
        compiler_params=_params("parallel"),
        name=name,
    )(my_chip.reshape(1).astype(jnp.int32), partial, recv, recv, recv)


def _sum_devices(stack, *, name):
    n, r, c = stack.shape

    def body(s_ref, o_ref):
        acc = s_ref[0]
        for dev in range(1, n):
            acc = acc + s_ref[dev]
        o_ref[...] = acc

    return pl.pallas_call(
        body,
        out_shape=jax.ShapeDtypeStruct((r, c), F32),
        in_specs=[pl.BlockSpec(memory_space=pltpu.VMEM)],
        out_specs=pl.BlockSpec(memory_space=pltpu.VMEM),
        name=name,
    )(stack)


def _mesh_pos():
    return lax.axis_index("x"), lax.axis_index("y"), lax.axis_index("c")


def _all_gather(shards, *, name):
    n = len(shards)

    def body(*refs):
        x_refs, out_refs = refs[:n], refs[n:2 * n]
        send_sems, recv_sems, local_sems = refs[2 * n:]
        x, y, cc = _mesh_pos()
        me, sibling = (x, y, cc), (x, y, 1 - cc)
        chips = [(1 - x, y), (x, 1 - y), (1 - x, 1 - y)]

        def rows(a, px, py, pc):
            return out_refs[a].at[4 * px + 2 * py + pc]

        def copy(a, kk, block, to, src=None):
            return pltpu.make_async_remote_copy(
                src_ref=rows(a, *block) if src is None else src, dst_ref=rows(a, *block),
                send_sem=send_sems.at[7 * a + kk], recv_sem=recv_sems.at[7 * a + kk],
                device_id=to, device_id_type=MESH)

        sends, own = [], []
        for a in range(n):
            own.append(pltpu.make_async_copy(x_refs[a], rows(a, *me), local_sems.at[a]))
            own[a].start()
            first = [copy(a, 0, me, sibling, src=x_refs[a])]
            first += [copy(a, 1 + j, me, (*chip, cc), src=x_refs[a]) for j, chip in enumerate(chips)]
            for cp in first:
                cp.start()
            sends += first
        for a in range(n):
            for j, chip in enumerate(chips):
                copy(a, 1 + j, (*chip, cc), me).wait_recv()
                passed = copy(a, 4 + j, (*chip, cc), sibling)
                passed.start()
                sends.append(passed)
        for a in range(n):
            copy(a, 0, sibling, me).wait_recv()
            for j, chip in enumerate(chips):
                copy(a, 4 + j, (*chip, 1 - cc), me).wait_recv()
        for cp in sends:
            cp.wait_send()
        for cp in own:
            cp.wait()

    return pl.pallas_call(
        body,
        out_shape=tuple(jax.ShapeDtypeStruct((N_DEV,) + t.shape, t.dtype) for t in shards),
        in_specs=[ANY] * n,
        out_specs=tuple([ANY] * n),
        scratch_shapes=[pltpu.SemaphoreType.DMA((7 * n,)), pltpu.SemaphoreType.DMA((7 * n,)),
                        pltpu.SemaphoreType.DMA((n,))],
        name=name,
    )(*shards)


def _plan_own_blocks(n):
    def plan(refs, send_sems, recv_sems, outgoing):
        x, y, cc = _mesh_pos()
        peers = [(x, y, 1 - cc), (1 - x, y, cc), (x, 1 - y, cc), (1 - x, 1 - y, cc)]
        copies = []
        for a in range(n):
            land = refs[n + a]
            for kk, (px, py, pc) in enumerate(peers):
                block = (x, y, cc) if outgoing else (px, py, pc)
                rows = land.at[4 * block[0] + 2 * block[1] + block[2]]
                copies.append(pltpu.make_async_remote_copy(
                    src_ref=refs[a] if outgoing else rows, dst_ref=rows, send_sem=send_sems.at[4 * a + kk],
                    recv_sem=recv_sems.at[4 * a + kk], device_id=(px, py, pc), device_id_type=MESH))
        return copies

    plan.n_sems = 4 * n
    return plan


def _plan_pass_on(n):
    def plan(refs, send_sems, recv_sems, outgoing):
        x, y, cc = _mesh_pos()
        copies = []
        for a in range(n):
            for j, (px, py) in enumerate([(1 - x, y), (x, 1 - y), (1 - x, 1 - y)]):
                rows = refs[a].at[4 * px + 2 * py + (cc if outgoing else 1 - cc)]
                copies.append(pltpu.make_async_remote_copy(
                    src_ref=rows, dst_ref=rows, send_sem=send_sems.at[3 * a + j], recv_sem=recv_sems.at[3 * a + j],
                    device_id=(x, y, 1 - cc), device_id_type=MESH))
        return copies

    plan.n_sems = 3 * n
    return plan


def _plan_to_sibling(n):
    def plan(refs, send_sems, recv_sems, outgoing):
        x, y, cc = _mesh_pos()
        copies = []
        for a in range(n):
            for chip in range(4):
                dst = refs[n + a].at[chip]
                copies.append(pltpu.make_async_remote_copy(
                    src_ref=refs[a].at[2 * chip + (1 - cc)] if outgoing else dst, dst_ref=dst,
                    send_sem=send_sems.at[4 * a + chip], recv_sem=recv_sems.at[4 * a + chip],
                    device_id=(x, y, 1 - cc), device_id_type=MESH))
        return copies

    plan.n_sems = 4 * n
    return plan


def _plan_to_chips(n):
    def plan(refs, send_sems, recv_sems, outgoing):
        x, y, cc = _mesh_pos()
        copies = []
        for a in range(n):
            for j, (px, py) in enumerate([(1 - x, y), (x, 1 - y), (1 - x, 1 - y)]):
                dst = refs[n + a].at[j]
                copies.append(pltpu.make_async_remote_copy(
                    src_ref=refs[a].at[2 * px + py] if outgoing else dst, dst_ref=dst,
                    send_sem=send_sems.at[3 * a + j], recv_sem=recv_sems.at[3 * a + j],
                    device_id=(px, py, cc), device_id_type=MESH))
        return copies

    plan.n_sems = 3 * n
    return plan


def _in_hbm(t):
    return pltpu.with_memory_space_constraint(t, pltpu.HBM)


def _exchange_start(plan, bufs, after, *, name):
    nb, na = len(bufs), len(after)

    def body(*refs):
        outs = refs[nb + na:]
        for cp in plan(refs[:nb], outs[0], outs[1], True):
            cp.start()
        outs[2 + nb][...] = jnp.zeros_like(outs[2 + nb])

    res = pl.pallas_call(
        body,
        out_shape=(pltpu.SemaphoreType.DMA((plan.n_sems,)), pltpu.SemaphoreType.DMA((plan.n_sems,)),
                   *[pltpu.HBM(t.shape, t.dtype) for t in bufs], jax.ShapeDtypeStruct((8, LANES), F32)),
        in_specs=[HBM_SPEC] * nb + [ANY] * na,
        out_specs=(SEM_SPEC, SEM_SPEC, *[HBM_SPEC] * nb, pl.BlockSpec(memory_space=pltpu.VMEM)),
        input_output_aliases={i: 2 + i for i in range(nb)},
        compiler_params=pltpu.CompilerParams(has_side_effects=DATAFLOW_EFFECT),
        name=name,
    )(*[_in_hbm(t) for t in bufs], *after)
    return plan, res[:2], list(res[2:2 + nb]), res[2 + nb]


def _exchange_wait(flight, after, *, name):
    plan, sems, bufs, _ = flight
    nb = len(bufs)

    def body(*refs):
        send_sems, recv_sems = refs[nb], refs[nb + 1]
        for cp in plan(refs[:nb], send_sems, recv_sems, False):
            cp.wait_recv()
        for cp in plan(refs[:nb], send_sems, recv_sems, True):
            cp.wait_send()

    res = pl.pallas_call(
        body,
        out_shape=tuple(pltpu.HBM(t.shape, t.dtype) for t in bufs),
        in_specs=[HBM_SPEC] * nb + [SEM_SPEC, SEM_SPEC] + [ANY] * len(after),
        out_specs=tuple([HBM_SPEC] * nb),
        input_output_aliases={i: i for i in range(nb)},
        compiler_params=pltpu.CompilerParams(has_side_effects=DATAFLOW_EFFECT),
        name=name,
    )(*bufs, *sems, *after)
    return list(res)


_W_GROUPS = {"even": ("even_w_in", "even_w_out"), "mlp0": ("mlp_w1_0", "mlp_w2_0"),
             "odd": ("odd_w_in", "odd_w_uq", "odd_w_ukv", "odd_w_out"), "mlp1": ("mlp_w1_1", "mlp_w2_1")}


class _Exchanges:
    def __init__(self, shards, gains, n_q, n_kv):
        self.n_q, self.n_kv = n_q, n_kv
        self.mx, self.my, self.mc = _mesh_pos()
        self.dev = 4 * self.mx + 2 * self.my + self.mc
        self.shards = shards
        self.gains = gains
        self.flights, self.gathered, self.grad_blocks, self.grad_names, self.reduced = {}, {}, {}, {}, {}

    def start(self):
        names = _W_GROUPS["even"]
        got = _all_gather([self.shards[n] for n in names] + [self.gains], name="comm_even_gather")
        self.gathered.update(zip(names, got[:-1]))
        self.all_gains = got[-1][:, 0]
        return self._w_begin("mlp0", [got[0]])

    def _w_begin(self, group, after):
        srcs = [self.shards[n] for n in _W_GROUPS[group]]
        lands = [lax.dynamic_update_slice(lax.empty((N_DEV,) + t.shape, t.dtype), t[None], (self.dev, 0, 0)) for t in srcs]
        self.flights[group] = _exchange_start(_plan_own_blocks(len(srcs)), srcs + lands, after, name=f"comm_{group}_own_start")
        return [self.flights[group][3]]

    def _w_turn(self, group, after):
        bufs = _exchange_wait(self.flights[group], after, name=f"comm_{group}_own_wait")
        n = len(bufs) // 2
        self.flights[group] = _exchange_start(_plan_pass_on(n), bufs[n:], [], name=f"comm_{group}_pass_start")
        return [self.flights[group][3]]

    def _w_end(self, group, after):
        self.gathered.update(zip(_W_GROUPS[group], _exchange_wait(self.flights.pop(group), after, name=f"comm_{group}_pass_wait")))

    def weights(self, group):
        return {n: self.gathered[n] for n in _W_GROUPS[group]}

    def norm_gains(self):
        return (self.all_gains[:, :self.n_q].reshape(-1), self.all_gains[:, self.n_q:self.n_q + self.n_kv].reshape(-1))

    def grads(self, group, named_blocks):
        self.grad_names[group] = [n for n, _ in named_blocks]
        self.grad_blocks[group] = [t for _, t in named_blocks]

    def _g_begin(self, group, after):
        blocks = self.grad_blocks[group]
        lands = [lax.empty((4,) + t.shape[1:], t.dtype) for t in blocks]
        self.flights[group] = _exchange_start(_plan_to_sibling(len(blocks)), blocks + lands, after, name=f"comm_{group}_sib_start")
        return [self.flights[group][3]]

    def _g_turn(self, group, after):
        bufs = _exchange_wait(self.flights[group], after, name=f"comm_{group}_sib_wait")
        n = len(bufs) // 2
        partial = [_pair_sum(a, b, self.mc, name=f"pair_sum_{nm}") for nm, a, b in zip(self.grad_names[group], bufs[:n], bufs[n:])]
        lands = [lax.empty((3,) + t.shape[1:], t.dtype) for t in partial]
        self.flights[group] = _exchange_start(_plan_to_chips(n), partial + lands, [], name=f"comm_{group}_chips_start")
        return [self.flights[group][3]]

    def _g_end(self, group, after):
        bufs = _exchange_wait(self.flights.pop(group), after, name=f"comm_{group}_chips_wait")
        n = len(bufs) // 2
        for nm, a, b in zip(self.grad_names[group], bufs[:n], bufs[n:]):
            self.reduced[nm] = _final_sum(a, b, 2 * self.mx + self.my, name=f"final_sum_{nm}")

    _SCHEDULE = {
        "even_out": (("w_turn", "mlp0"), ("w_begin", "odd")),
        "ln1_l0": (("w_end", "mlp0"),),
        "ln2_l0": (("w_turn", "odd"), ("w_begin", "mlp1"), ("w_end", "odd")),
        "odd_out": (("w_turn", "mlp1"),),
        "ln1_l1": (("w_end", "mlp1"),),
        "dw_l1": (("g_begin", "mlp1"),),
        "ln1_bwd_l1": (("g_turn", "mlp1"),),
        "odd_in_dw": (("g_begin", "odd"),),
        "mlp2_dx_l0": (("g_end", "mlp1"), ("g_turn", "odd")),
        "dw_l0": (("g_begin", "mlp0"),),
        "ln1_bwd_l0": (("g_end", "odd"), ("g_turn", "mlp0")),
        "even_in_dw": (("g_begin", "even"),),
        "even_in_dx": (("g_end", "mlp0"), ("g_turn", "even")),
        "finish": (("g_end", "even"),),
    }

    def sync(self, tag, after):
        latest, started = list(after), []
        for what, group in self._SCHEDULE[tag]:
            out = getattr(self, "_" + what)(group, latest)
            if out:
                latest = started = out
        return started


def _alibi(n):
    return 2.0 ** (-8.0 * np.arange(1, n + 1, dtype=np.float32) / n)


def _heads(t, n):
    s = t.shape[0]
    return t.reshape(s, n, t.shape[1] // n).transpose(1, 0, 2)


def _unheads(t):
    n, s, dh = t.shape
    return t.transpose(1, 0, 2).reshape(s, n * dh)


def _to_strided(t, d):
    h, s, x = t.shape
    return t.reshape(h, s // d, d, x).transpose(0, 2, 1, 3).reshape(h, s, x)


def _from_strided(t, d):
    h, s, x = t.shape
    return t.reshape(h, d, s // d, x).transpose(0, 2, 1, 3).reshape(h, s, x)


def _true_columns(t, c, n_pad):
    r = t.shape[1]
    w = t[:, :, :c].transpose(1, 0, 2).reshape(r, N_DEV * c)
    return jnp.pad(w, ((0, 0), (0, n_pad - N_DEV * c)))


def _column_blocks(t, c, cp):
    r = t.shape[0]
    b = t[:, :N_DEV * c].reshape(r, N_DEV, c).transpose(1, 0, 2)
    return jnp.pad(b, ((0, 0), (0, 0), (0, cp - c)))


def _stack_rows(t, nq):
    return t.reshape(nq, BLK, A_KV_HEADS, A_GROUP, HEAD_DIM).transpose(2, 0, 3, 1, 4).reshape(
        A_KV_HEADS, 1, nq * A_GROUP * BLK, HEAD_DIM)


def _unstack_rows(t, nq):
    return t.reshape(A_KV_HEADS, nq, A_GROUP, BLK, HEAD_DIM).transpose(1, 3, 0, 2, 4).reshape(
        nq * BLK, A_Q_W)


def _lead_block(t):
    return jnp.pad(t, ((0, 0), (BLK, 0), (0, 0)))


def _columns(t):
    return t.transpose(1, 0, 2).reshape(t.shape[1], -1)


def _rows(t):
    return t.reshape(-1, t.shape[2])


def _by_column_block(t):
    return t.reshape(t.shape[0], N_DEV, -1).transpose(1, 0, 2)


def _local_step(x0, target, comm, sinks, ln1_g, ln1_b, ln2_g, ln2_b):
    s, d = x0.shape
    scale_h = 1.0 / math.sqrt(HEAD_DIM)
    scale_d = 1.0 / math.sqrt(D_NOPE + D_ROPE)
    nq = s // BLK
    even_c, odd_c = EVEN_IN // N_DEV, ODD_IN // N_DEV
    bf = lambda t: t.astype(MXU_DTYPE)
    blocks = dict(b_blocks=True)

    tok = comm.start()
    w_even = comm.weights("even")
    even_cp, even_n = w_even["even_w_in"].shape[2], -(-EVEN_IN // 1024) * 1024
    w_even_in, w_even_out = _true_columns(w_even["even_w_in"], even_c, even_n), _columns(w_even["even_w_out"])
    x0b = bf(x0)
    h_e = _mm(x0b, w_even_in, out_dtypes=(MXU_DTYPE,), name="even_in_fwd", deps=tok)
    qa = _stack_rows(h_e[:, :A_Q_W], nq)
    ka = _lead_block(_heads(h_e[:, A_Q_W:A_Q_W + A_KV_W], A_KV_HEADS))[:, None]
    va = _lead_block(_heads(h_e[:, A_Q_W + A_KV_W:A_Q_W + 2 * A_KV_W], A_KV_HEADS))[:, None]
    rows_a = A_GROUP * BLK
    slope_a = jnp.asarray(np.repeat(_alibi(A_Q_HEADS).reshape(A_KV_HEADS, A_GROUP), BLK, axis=1).reshape(
        A_KV_HEADS, 1, rows_a, 1))
    sink_a = jnp.broadcast_to(sinks.reshape(A_KV_HEADS, A_GROUP, 1), (A_KV_HEADS, A_GROUP, BLK)).reshape(
        A_KV_HEADS, 1, rows_a, 1)
    a_cfg = dict(scale=scale_h, n_back=A_WINDOW - 1, bps=nq)
    oa, lse_a = _band_fwd(qa, ka, va, slope_a, sink_a, name="swa_fwd", **a_cfg)
    b_in, b_cfg, b_out, b_lse = [], [], [], []
    base = A_Q_W + 2 * A_KV_W
    for gi, (window, dil) in enumerate(B_PATTERNS):
        blk = h_e[:, base + gi * 3 * B_W: base + (gi + 1) * 3 * B_W].reshape(s, 3, B_HEADS, HEAD_DIM)
        qs, ks, vs = (_to_strided(blk[:, n].transpose(1, 0, 2), dil) for n in range(3))
        slope = jnp.asarray(np.broadcast_to((_alibi(B_HEADS) * dil).reshape(1, B_HEADS, 1, 1), (1, B_HEADS, BLK, 1)))
        ins = (qs[None], _lead_block(ks)[None], _lead_block(vs)[None], slope)
        cfg = dict(scale=scale_h, n_back=window // dil, bps=nq // dil)
        o, lse = _band_fwd(*ins, None, name=f"dil{gi}_fwd", **cfg)
        b_in.append(ins)
        b_cfg.append(cfg)
        b_out.append(_from_strided(o[0], dil))
        b_lse.append(_from_strided(lse[0], dil))
    ob, lse_b = _merge(b_out, b_lse, name="dil_merge")
    y_e = bf(jnp.concatenate([_unstack_rows(oa, nq), _unheads(ob)], axis=1))
    mixed = _mm(y_e, w_even_out, name="even_out_fwd")
    tok = comm.sync("even_out", [mixed])
    x0n, x0nb, xh1_0, r1_0 = _ln_fwd(x0, mixed, ln1_g[0], ln1_b[0], name="ln1_fwd_l0", deps=tok)
    comm.sync("ln1_l0", [x0nb])
    w_mlp0 = comm.weights("mlp0")
    w1_0, w2_0 = w_mlp0["mlp_w1_0"], _rows(w_mlp0["mlp_w2_0"])
    act0, hid0 = _mm(x0nb, w1_0, out_dtypes=(MXU_DTYPE, MXU_DTYPE), epilogue=_relu_sq, name="mlp1_fwd_l0", **blocks)
    mlp = _mm(hid0, w2_0, name="mlp2_fwd_l0")
    x1, x1b, xh2_0, r2_0 = _ln_fwd(x0n, mlp, ln2_g[0], ln2_b[0], name="ln2_fwd_l0")

    tok = comm.sync("ln2_l0", [x1b])
    w_odd = comm.weights("odd")
    odd_cp, odd_n = w_odd["odd_w_in"].shape[2], -(-ODD_IN // 1024) * 1024
    w_odd_in, w_uq, w_ukv, w_odd_out = (_true_columns(w_odd["odd_w_in"], odd_c, odd_n), _columns(w_odd["odd_w_uq"]),
                                        _columns(w_odd["odd_w_ukv"]), _rows(w_odd["odd_w_out"]))
    gq, gkv = comm.norm_gains()
    h_o = _mm(x1b, w_odd_in, name="odd_in_fwd", deps=tok)
    qkv_c = bf(h_o[:, :3 * C_W])
    oc, sb_total = _sb_fwd(qkv_c, heads=C_HEADS, scale=scale_h, name="sb_fwd")
    o_cq, o_ckv, o_kr = 3 * C_W, 3 * C_W + D_Q_RANK, 3 * C_W + D_Q_RANK + D_KV_RANK
    cq, ckv, kr = h_o[:, o_cq:o_ckv], h_o[:, o_ckv:o_kr], h_o[:, o_kr:o_kr + D_ROPE]
    ncq, rq = _rms_fwd(cq, gq, name="rms_q_fwd")
    nckv, rkv = _rms_fwd(ckv, gkv, name="rms_kv_fwd")
    lane_pad = LANES - D_NOPE - D_ROPE
    w_uq = jnp.pad(w_uq.reshape(D_Q_RANK, D_HEADS, D_NOPE + D_ROPE), ((0, 0), (0, 0), (0, lane_pad))).reshape(
        D_Q_RANK, D_HEADS * LANES)
    qd = _mm(ncq, w_uq, name="uq_fwd")
    kvd = _mm(nckv, w_ukv, out_dtypes=(MXU_DTYPE,), name="ukv_fwd")
    rope_t = _rope_tables(s, inverse=False)
    krp = _rope(jnp.pad(kr, ((0, 0), (D_NOPE, lane_pad)))[None], rope_t, out_dtype=MXU_DTYPE, name="rope_k_fwd")[0]
    od, lse_d = _mla_fwd(qd, kvd, krp, rope_t, scale=scale_d, name="mla_fwd")
    y_o = bf(jnp.concatenate([oc, od], axis=1))
    mixed = _mm(y_o, w_odd_out, name="odd_out_fwd")
    tok = comm.sync("odd_out", [mixed])
    x1n, x1nb, xh1_1, r1_1 = _ln_fwd(x1, mixed, ln1_g[1], ln1_b[1], name="ln1_fwd_l1", deps=tok)
    comm.sync("ln1_l1", [x1nb])
    w_mlp1 = comm.weights("mlp1")
    w1_1, w2_1 = w_mlp1["mlp_w1_1"], _rows(w_mlp1["mlp_w2_1"])
    act1, hid1 = _mm(x1nb, w1_1, out_dtypes=(MXU_DTYPE, MXU_DTYPE), epilogue=_relu_sq, name="mlp1_fwd_l1", **blocks)
    mlp = _mm(hid1, w2_1, name="mlp2_fwd_l1")
    y, _, xh2_1, r2_1 = _ln_fwd(x1n, mlp, ln2_g[1], ln2_b[1], name="ln2_fwd_l1")
    dy, loss_vec = _loss_head(y, target, name="loss_head")

    def mlp_block_bwd(g_out, layer, w1, w2, xh2, r2, xh1, r1, act, hid, xnb):
        du2, du2b, dg2, db2 = _ln_bwd(g_out, xh2, r2, ln2_g[layer], name=f"ln2_bwd_l{layer}")
        dpre = _mm(du2b, w2, nt=True, out_dtypes=(MXU_DTYPE,), epilogue=_relu_sq_grad, extra=act, name=f"mlp2_dx_l{layer}")
        tok = comm.sync("mlp2_dx_l0", [dpre]) if layer == 0 else []
        dw2 = _mm(hid, du2b, ta=True, out_dtypes=(BF16,), name=f"mlp2_dw_l{layer}", deps=tok)
        dw1 = _mm(xnb, dpre, ta=True, out_blocks=True, out_dtypes=(BF16,), name=f"mlp1_dw_l{layer}")
        comm.grads(f"mlp{layer}", [(f"mlp_w1_{layer}", dw1), (f"mlp_w2_{layer}", dw2.reshape(N_DEV, -1, d))])
        tok = comm.sync(f"dw_l{layer}", [dw1])
        dxn = _mm(dpre, w1, nt=True, epilogue=_add_alpha, extra=du2, name=f"mlp1_dx_l{layer}", deps=tok, **blocks)
        du1, du1b, dg1, db1 = _ln_bwd(dxn, xh1, r1, ln1_g[layer], name=f"ln1_bwd_l{layer}")
        return du1, du1b, comm.sync(f"ln1_bwd_l{layer}", [du1b]), (dg1, db1, dg2, db2)

    du1, du1b, tok, ln_1 = mlp_block_bwd(dy, 1, w1_1, w2_1, xh2_1, r2_1, xh1_1, r1_1, act1, hid1, x1nb)
    d_odd_out = _mm(y_o, du1b, ta=True, out_dtypes=(BF16,), name="odd_out_dw", deps=tok).reshape(N_DEV, -1, d)
    dy_o = _mm(du1b, w_odd_out, nt=True, name="odd_out_dx")
    dqc, dkc, dvc = _sb_bwd(qkv_c, dy_o, sb_total, heads=C_HEADS, scale=scale_h, name="sb_bwd")
    dqd, dkvd, dkr_pairs = _mla_bwd(qd, kvd, krp, rope_t, dy_o, od, lse_d, do_block0=C_W // LANES, scale=scale_d,
                                    name="mla_bwd")
    _, dkr_sum = _rope(dkr_pairs, _rope_tables(s, inverse=True), out_dtype=F32, head_sum=True, name="rope_k_bwd")
    dqd, dkvd = bf(dqd), bf(dkvd)
    d_uq = _mm(ncq, dqd, ta=True, out_dtypes=(BF16,), name="uq_dw").reshape(D_Q_RANK, D_HEADS, LANES)[:, :, :D_NOPE + D_ROPE].reshape(
        D_Q_RANK, D_HEADS * (D_NOPE + D_ROPE))
    dncq = _mm(dqd, w_uq, nt=True, name="uq_dx")
    d_ukv = _mm(nckv, dkvd, ta=True, out_dtypes=(BF16,), name="ukv_dw")
    dnckv = _mm(dkvd, w_ukv, nt=True, name="ukv_dx")
    dcq, dgq = _rms_bwd(dncq, cq, rq, gq, name="rms_q_bwd")
    dckv, dgkv = _rms_bwd(dnckv, ckv, rkv, gkv, name="rms_kv_bwd")
    dh_o = bf(jnp.concatenate(
        [dqc, dkc, dvc, dcq, dckv, dkr_sum[:, D_NOPE:D_NOPE + D_ROPE], jnp.zeros((s, odd_n - ODD_IN), F32)], axis=1))
    d_odd_in = _column_blocks(_mm(x1b, dh_o, ta=True, out_dtypes=(BF16,), name="odd_in_dw"), odd_c, odd_cp)
    comm.grads("odd", [("odd_w_in", d_odd_in), ("odd_w_uq", _by_column_block(d_uq)),
                       ("odd_w_ukv", _by_column_block(d_ukv)), ("odd_w_out", d_odd_out)])
    tok = comm.sync("odd_in_dw", [d_odd_in])
    dx1 = _mm(dh_o, w_odd_in, nt=True, epilogue=_add_alpha, extra=du1, name="odd_in_dx", deps=tok)

    du1, du1b, tok, ln_0 = mlp_block_bwd(dx1, 0, w1_0, w2_0, xh2_0, r2_0, xh1_0, r1_0, act0, hid0, x0nb)
    d_even_out = _mm(y_e, du1b, ta=True, out_dtypes=(BF16,), name="even_out_dw", deps=tok)
    dy_e = _mm(du1b, w_even_out, nt=True, name="even_out_dx")
    doa, dob = _stack_rows(dy_e[:, :A_Q_W], nq), _heads(dy_e[:, A_Q_W:], B_HEADS)
    dqa, dka, dva, dsink = _band_bwd(qa, ka, va, doa, oa, lse_a, slope_a, sink_a, name="swa_bwd", **a_cfg)
    pieces = [_unstack_rows(dqa, nq), _unheads(dka[:, 0, BLK:]), _unheads(dva[:, 0, BLK:])]
    for gi, (_, dil) in enumerate(B_PATTERNS):
        qs, ks, vs, slope = b_in[gi]
        dq, dk, dv = _band_bwd(qs, ks, vs, _to_strided(dob, dil)[None], _to_strided(ob, dil)[None],
                               _to_strided(lse_b, dil)[None], slope, None, name=f"dil{gi}_bwd", **b_cfg[gi])
        pieces += [_unheads(_from_strided(t, dil)) for t in (dq[0], dk[0, :, BLK:], dv[0, :, BLK:])]
    dh_e = bf(jnp.concatenate(pieces + [jnp.zeros((s, even_n - EVEN_IN), F32)], axis=1))
    d_even_in = _column_blocks(_mm(x0b, dh_e, ta=True, out_dtypes=(BF16,), name="even_in_dw"), even_c, even_cp)
    comm.grads("even", [("even_w_in", d_even_in), ("even_w_out", _by_column_block(d_even_out))])
    tok = comm.sync("even_in_dw", [d_even_in])
    grad_x = _mm(dh_e, w_even_in, nt=True, epilogue=_add_alpha, extra=du1, name="even_in_dx", deps=tok)
    tok = comm.sync("even_in_dx", [grad_x])

    ln = [jnp.concatenate([a, b], axis=0) for a, b in zip(ln_0, ln_1)]
    small = {"ln": ln, "sinks": dsink[:, :, 0].reshape(-1), "gq": dgq[0], "gkv": dgkv[0], "loss": loss_vec[0, :1]}
    return grad_x, small, tok


def kernel(x, even_w_in, even_sinks, even_w_out, odd_w_in, odd_q_norm_g, odd_kv_norm_g, odd_w_uq, odd_w_ukv, odd_w_out, ln1_g, ln1_b, mlp_w1, mlp_w2, ln2_g, ln2_b, loss_target, m_even_w_in, m_even_sinks, m_even_w_out, m_odd_w_in, m_odd_q_norm_g, m_odd_kv_norm_g, m_odd_w_uq, m_odd_w_ukv, m_odd_w_out, m_ln1_g, m_ln1_b, m_mlp_w1, m_mlp_w2, m_ln2_g, m_ln2_b, v_even_w_in, v_even_sinks, v_even_w_out, v_odd_w_in, v_odd_q_norm_g, v_odd_kv_norm_g, v_odd_w_uq, v_odd_w_ukv, v_odd_w_out, v_ln1_g, v_ln1_b, v_mlp_w1, v_mlp_w2, v_ln2_g, v_ln2_b):
    weights = dict(even_w_in=even_w_in, even_sinks=even_sinks, even_w_out=even_w_out, odd_w_in=odd_w_in,
                   odd_q_norm_g=odd_q_norm_g, odd_kv_norm_g=odd_kv_norm_g, odd_w_uq=odd_w_uq, odd_w_ukv=odd_w_ukv,
                   odd_w_out=odd_w_out, ln1_g=ln1_g, ln1_b=ln1_b, mlp_w1=mlp_w1, mlp_w2=mlp_w2, ln2_g=ln2_g, ln2_b=ln2_b)
    mom_m = dict(even_w_in=m_even_w_in, even_sinks=m_even_sinks, even_w_out=m_even_w_out, odd_w_in=m_odd_w_in,
                 odd_q_norm_g=m_odd_q_norm_g, odd_kv_norm_g=m_odd_kv_norm_g, odd_w_uq=m_odd_w_uq, odd_w_ukv=m_odd_w_ukv,
                 odd_w_out=m_odd_w_out, ln1_g=m_ln1_g, ln1_b=m_ln1_b, mlp_w1=m_mlp_w1, mlp_w2=m_mlp_w2, ln2_g=m_ln2_g, ln2_b=m_ln2_b)
    mom_v = dict(even_w_in=v_even_w_in, even_sinks=v_even_sinks, even_w_out=v_even_w_out, odd_w_in=v_odd_w_in,
                 odd_q_norm_g=v_odd_q_norm_g, odd_kv_norm_g=v_odd_kv_norm_g, odd_w_uq=v_odd_w_uq, odd_w_ukv=v_odd_w_ukv,
                 odd_w_out=v_odd_w_out, ln1_g=v_ln1_g, ln1_b=v_ln1_b, mlp_w1=v_mlp_w1, mlp_w2=v_mlp_w2, ln2_g=v_ln2_g, ln2_b=v_ln2_b)
    order = list(weights)
    n_q, n_kv = odd_q_norm_g.shape[1], odd_kv_norm_g.shape[1]

    def lane_padded(t):
        return jnp.pad(t, ((0, 0), (0, _lane_pad(t.shape[1]) - t.shape[1]))).astype(BF16)

    shards = {"even_w_in": lane_padded(even_w_in[0]), "even_w_out": even_w_out[0].astype(BF16),
              "mlp_w1_0": mlp_w1[0].astype(BF16), "mlp_w2_0": mlp_w2[0].astype(BF16),
              "odd_w_in": lane_padded(odd_w_in[0]), "odd_w_uq": odd_w_uq[0].astype(BF16),
              "odd_w_ukv": odd_w_ukv[0].astype(BF16), "odd_w_out": odd_w_out[0].astype(BF16),
              "mlp_w1_1": mlp_w1[1].astype(BF16), "mlp_w2_1": mlp_w2[1].astype(BF16)}
    gains = jnp.concatenate([odd_q_norm_g, odd_kv_norm_g, jnp.zeros((1, LANES - n_q - n_kv), F32)], axis=1)
    comm = _Exchanges(shards, gains, n_q, n_kv)
    dev = comm.dev

    grad_x, small, last_started = _local_step(x[0], loss_target[0], comm, even_sinks[0], ln1_g, ln1_b, ln2_g, ln2_b)

    small_parts = [t.reshape(-1) for t in small["ln"]] + [small["sinks"], small["gq"], small["gkv"], small["loss"]]
    small_sizes = [p.shape[0] for p in small_parts]
    n_small = sum(small_sizes)
    small_rows = -(-n_small // (8 * LANES)) * 8
    small_flat = jnp.concatenate(small_parts + [jnp.zeros((small_rows * LANES - n_small,), F32)]).reshape(small_rows, LANES)
    (small_all,) = _all_gather([small_flat], name="comm_small_gather")
    totals = _sum_devices(small_all, name="small_sum").reshape(-1)
    tot, off = [], 0
    for size in small_sizes:
        tot.append(totals[off:off + size])
        off += size
    grads = {}
    for i, n in enumerate(("ln1_g", "ln1_b", "ln2_g", "ln2_b")):
        grads[n] = tot[i].reshape(weights[n].shape)
    grads["even_sinks"] = tot[4].reshape(even_sinks.shape)
    grads["odd_q_norm_g"] = lax.dynamic_slice(tot[5], (dev * n_q,), (n_q,)).reshape(odd_q_norm_g.shape)
    grads["odd_kv_norm_g"] = lax.dynamic_slice(tot[6], (dev * n_kv,), (n_kv,)).reshape(odd_kv_norm_g.shape)
    loss = tot[7][0]

    delta, new_m, new_v = {}, {}, {}

    def update(n):
        g_list = [comm.reduced[f"{n}_0"], comm.reduced[f"{n}_1"]] if n.startswith("mlp") else [comm.reduced[n]]
        grads[n], delta[n], new_m[n], new_v[n] = _adamw(weights[n], g_list, mom_m[n], mom_v[n], name=f"adamw_{n}",
                                                        deps=last_started)

    early = ("mlp_w1", "mlp_w2", "odd_w_in", "odd_w_uq", "odd_w_ukv", "odd_w_out")
    for n in early:
        update(n)
    small_names = [n for n in order if n not in ("even_w_in", "even_w_out", "odd_w_in", "odd_w_uq", "odd_w_ukv", "odd_w_out", "mlp_w1", "mlp_w2")]
    n_sm = sum(weights[n].size for n in small_names)
    sm_rows = -(-n_sm // (8 * LANES)) * 8

    def pack_small(group):
        flat = [group[n].reshape(-1) for n in small_names]
        return jnp.concatenate(flat + [jnp.zeros((sm_rows * LANES - n_sm,), F32)]).reshape(1, sm_rows, LANES)

    res = _adamw(pack_small(weights), [pack_small(grads)[0]], pack_small(mom_m), pack_small(mom_v), name="adamw_small",
                 deps=last_started)
    off = 0
    for n in small_names:
        size = weights[n].size
        delta[n], new_m[n], new_v[n] = (t.reshape(-1)[off:off + size].reshape(weights[n].shape) for t in res[1:])
        off += size
    comm.sync("finish", [new_v[n] for n in early] + [res[0]])
    update("even_w_in")
    update("even_w_out")

    return (loss, grad_x[None], *[grads[n] for n in order], *[delta[n] for n in order],
            *[new_m[n] for n in order], *[new_v[n] for n in order])
```

```python
import math

import jax
import jax.numpy as jnp
import numpy as np
from jax import lax
from jax.experimental import pallas as pl
from jax.experimental.pallas import tpu as pltpu

F32 = jnp.float32
BF16 = jnp.bfloat16
MXU_DTYPE = BF16

HEAD_DIM = 64
A_Q_HEADS, A_KV_HEADS, A_WINDOW = 16, 2, 128
A_GROUP = A_Q_HEADS // A_KV_HEADS
B_HEADS = 8
B_PATTERNS = ((128, 1), (512, 4), (2048, 16))
C_HEADS = 16
D_HEADS, D_Q_RANK, D_KV_RANK, D_NOPE, D_ROPE, D_V = 16, 512, 256, 64, 32, 64
ROPE_BASE = 10000.0
LN_EPS, RMS_EPS = 1e-5, 1e-6
DEPTH = 2
ALPHA = (2 * DEPTH) ** 0.25
A_Q_W, A_KV_W, B_W = A_Q_HEADS * HEAD_DIM, A_KV_HEADS * HEAD_DIM, B_HEADS * HEAD_DIM
EVEN_IN = A_Q_W + 2 * A_KV_W + 3 * B_W * len(B_PATTERNS)
C_W = C_HEADS * HEAD_DIM
ODD_IN = 3 * C_W + D_Q_RANK + D_KV_RANK + D_ROPE
ADAM_LR, ADAM_B1, ADAM_B2, ADAM_EPS, ADAM_WD, ADAM_STEP = 0.001, 0.9, 0.999, 1e-08, 0.01, 10

N_DEV = 8
LANES = 128
BLK = 128
CAUSAL_TILE = 512
CUM_CHUNK = 256
NEG = -1e30
VMEM_LIMIT = 48 * 1024 * 1024

NN = ((1,), (0,))
NT = ((1,), (1,))
TN = ((0,), (0,))
MESH = pl.DeviceIdType.MESH
ANY = pl.BlockSpec(memory_space=pl.ANY)
HBM_SPEC = pl.BlockSpec(memory_space=pltpu.HBM)
SEM_SPEC = pl.BlockSpec(memory_space=pltpu.SEMAPHORE)
DATAFLOW_EFFECT = pltpu.SideEffectType.DATAFLOW_SIDE_EFFECTING


def _dot(a, b, dims):
    return lax.dot_general(a, b, (dims, ((), ())), preferred_element_type=F32)


def _bdot(a, b, dims):
    return jnp.stack([_dot(a[n], b[n], dims) for n in range(a.shape[0])])


def _params(*sem):
    return pltpu.CompilerParams(dimension_semantics=tuple(sem), vmem_limit_bytes=VMEM_LIMIT)


def _pick(n, cap, mult=LANES):
    if n <= cap:
        return n
    for t in range(cap - cap % mult, 0, -mult):
        if n % t == 0:
            return t
    raise ValueError(f"no tile for {n}")


def _lane_pad(c):
    return -(-c // LANES) * LANES


def _mm(a, b, *, name, nt=False, ta=False, b_blocks=False, out_blocks=False, out_dtypes=(F32,), epilogue=None, extra=None, deps=()):
    m, k = a.shape[::-1] if ta else a.shape
    if b_blocks:
        nb, kin, c = b.shape
        n = kin if nt else nb * c
        k_full = nb * c if nt else kin
    else:
        n, k_full = (b.shape if nt else b.shape[::-1])
    assert k == k_full, (a.shape, b.shape, nt, b_blocks)
    tm = _pick(m, 1024, 8)
    if b_blocks and not nt:
        tn, tk = c, _pick(k, 3072)
    elif b_blocks:
        per_step = max(g for g in (1, 2, 4, 8) if g * c <= 2048)
        tn, tk = _pick(n, 1024), per_step * c
    elif out_blocks:
        tn, tk = n // N_DEV, _pick(k, 3072)
    else:
        tn, tk = _pick(n, 512), _pick(k, 3072)
        if k > tk:
            tn, tk = _pick(n, 1024), _pick(k, 2048)
    nk = k // tk
    n_out = len(out_dtypes)

    def body(*refs):
        a_ref, b_ref = refs[0], refs[1]
        e_ref = refs[2] if extra is not None else None
        first_out = 2 + (extra is not None) + len(deps)
        out_refs = refs[first_out:first_out + n_out]

        def finish(acc):
            e = None if e_ref is None else e_ref[...]
            outs = (acc,) if epilogue is None else epilogue(acc, e)
            for r, o in zip(out_refs, outs):
                r[...] = o.astype(r.dtype).reshape(r.shape)

        if b_blocks and nt:
            part = _dot(a_ref[:, :c], b_ref[0], NT)
            for blk in range(1, per_step):
                part += _dot(a_ref[:, blk * c:(blk + 1) * c], b_ref[blk], NT)
        elif ta:
            part = _dot(a_ref[...], b_ref[...], TN)
        else:
            part = _dot(a_ref[...], b_ref[0] if b_blocks else b_ref[...], NT if nt else NN)
        if nk == 1:
            finish(part)
        else:
            acc_ref = refs[first_out + n_out]
            kk = pl.program_id(2)

            @pl.when(kk == 0)
            def _():
                acc_ref[...] = part

            @pl.when(kk > 0)
            def _():
                acc_ref[...] += part

            @pl.when(kk == nk - 1)
            def _():
                finish(acc_ref[...])

    if b_blocks and not nt:
        b_spec = pl.BlockSpec((1, tk, tn), lambda i, j, kk: (j, kk, 0))
    elif b_blocks:
        b_spec = pl.BlockSpec((per_step, tn, c), lambda i, j, kk: (kk, j, 0))
    elif nt:
        b_spec = pl.BlockSpec((tn, tk), lambda i, j, kk: (j, kk))
    else:
        b_spec = pl.BlockSpec((tk, tn), lambda i, j, kk: (kk, j))
    a_spec = pl.BlockSpec((tk, tm), lambda i, j, kk: (kk, i)) if ta else pl.BlockSpec((tm, tk), lambda i, j, kk: (i, kk))
    in_specs = [a_spec, b_spec]
    ins = [a.astype(MXU_DTYPE), b.astype(MXU_DTYPE)]
    if extra is not None:
        in_specs.append(pl.BlockSpec((tm, tn), lambda i, j, kk: (i, j)))
        ins.append(extra)
    in_specs += [ANY] * len(deps)
    ins += list(deps)
    if out_blocks:
        out_shape = tuple(jax.ShapeDtypeStruct((N_DEV, m, tn), d) for d in out_dtypes)
        out_specs = tuple(pl.BlockSpec((1, tm, tn), lambda i, j, kk: (j, i, 0)) for _ in out_dtypes)
    else:
        out_shape = tuple(jax.ShapeDtypeStruct((m, n), d) for d in out_dtypes)
        out_specs = tuple(pl.BlockSpec((tm, tn), lambda i, j, kk: (i, j)) for _ in out_dtypes)
    outs = pl.pallas_call(
        body,
        out_shape=out_shape,
        grid=(m // tm, n // tn, nk),
        in_specs=in_specs,
        out_specs=out_specs,
        scratch_shapes=[pltpu.VMEM((tm, tn), F32)] if nk > 1 else [],
        compiler_params=_params("parallel", "parallel", "arbitrary"),
        name=name,
    )(*ins)
    return outs[0] if n_out == 1 else outs


def _relu_sq(acc, _):
    act = jnp.maximum(acc, 0.0)
    return act, act * act


def _relu_sq_grad(acc, act):
    return (acc * (2.0 * act.astype(F32)),)


def _add_alpha(acc, du):
    return (acc + ALPHA * du,)


def _ln_fwd(x, mixed, g, b, *, name, deps=()):
    s, d = x.shape
    tr = _pick(s, 256, 8)

    def body(x_ref, m_ref, g_ref, b_ref, *rest):
        y_ref, yb_ref, xh_ref, r_ref = rest[len(deps):]
        u = ALPHA * x_ref[...] + m_ref[...]
        mu = jnp.mean(u, axis=-1, keepdims=True)
        xc = u - mu
        var = jnp.mean(xc * xc, axis=-1, keepdims=True)
        r = lax.rsqrt(var + LN_EPS)
        xh = xc * r
        y = xh * g_ref[...] + b_ref[...]
        y_ref[...] = y
        yb_ref[...] = y.astype(MXU_DTYPE)
        xh_ref[...] = xh
        r_ref[...] = r

    row = pl.BlockSpec((tr, d), lambda i: (i, 0))
    vec = pl.BlockSpec((1, d), lambda i: (0, 0))
    return pl.pallas_call(
        body,
        out_shape=(jax.ShapeDtypeStruct((s, d), F32), jax.ShapeDtypeStruct((s, d), MXU_DTYPE),
                   jax.ShapeDtypeStruct((s, d), F32), jax.ShapeDtypeStruct((s, 1), F32)),
        grid=(s // tr,),
        in_specs=[row, row, vec, vec] + [ANY] * len(deps),
        out_specs=(row, row, row, pl.BlockSpec((tr, 1), lambda i: (i, 0))),
        compiler_params=_params("parallel"),
        name=name,
    )(x, mixed, g.reshape(1, d), b.reshape(1, d), *deps)


def _ln_bwd(dy, xh, r, g, *, name):
    s, d = dy.shape
    tr = _pick(s, 256, 8)

    def body(dy_ref, xh_ref, r_ref, g_ref, du_ref, dub_ref, dg_ref, db_ref):
        dyv, xhv = dy_ref[...], xh_ref[...]
        dxh = dyv * g_ref[...]
        c1 = jnp.mean(dxh, axis=-1, keepdims=True)
        c2 = jnp.mean(dxh * xhv, axis=-1, keepdims=True)
        du = r_ref[...] * (dxh - c1 - xhv * c2)
        du_ref[...] = du
        dub_ref[...] = du.astype(MXU_DTYPE)

        @pl.when(pl.program_id(0) == 0)
        def _():
            dg_ref[...] = jnp.zeros_like(dg_ref)
            db_ref[...] = jnp.zeros_like(db_ref)

        dg_ref[...] += jnp.sum(dyv * xhv, axis=0, keepdims=True)
        db_ref[...] += jnp.sum(dyv, axis=0, keepdims=True)

    row = pl.BlockSpec((tr, d), lambda i: (i, 0))
    vec = pl.BlockSpec((1, d), lambda i: (0, 0))
    return pl.pallas_call(
        body,
        out_shape=(jax.ShapeDtypeStruct((s, d), F32), jax.ShapeDtypeStruct((s, d), MXU_DTYPE),
                   jax.ShapeDtypeStruct((1, d), F32), jax.ShapeDtypeStruct((1, d), F32)),
        grid=(s // tr,),
        in_specs=[row, row, pl.BlockSpec((tr, 1), lambda i: (i, 0)), vec],
        out_specs=(row, row, vec, vec),
        compiler_params=_params("arbitrary"),
        name=name,
    )(dy, xh, r, g.reshape(1, d))


def _rms_fwd(x, g, *, name):
    s, d = x.shape
    tr = _pick(s, 512, 8)

    def body(x_ref, g_ref, y_ref, r_ref):
        xv = x_ref[...]
        r = lax.rsqrt(jnp.mean(xv * xv, axis=-1, keepdims=True) + RMS_EPS)
        y_ref[...] = (xv * r * g_ref[...]).astype(y_ref.dtype)
        r_ref[...] = r

    return pl.pallas_call(
        body,
        out_shape=(jax.ShapeDtypeStruct((s, d), MXU_DTYPE), jax.ShapeDtypeStruct((s, 1), F32)),
        grid=(s // tr,),
        in_specs=[pl.BlockSpec((tr, d), lambda i: (i, 0)), pl.BlockSpec((1, d), lambda i: (0, 0))],
        out_specs=(pl.BlockSpec((tr, d), lambda i: (i, 0)), pl.BlockSpec((tr, 1), lambda i: (i, 0))),
        compiler_params=_params("parallel"),
        name=name,
    )(x, g.reshape(1, d))


def _rms_bwd(dy, x, r, g, *, name):
    s, d = x.shape
    tr = _pick(s, 512, 8)

    def body(dy_ref, x_ref, r_ref, g_ref, dx_ref, dg_ref):
        dyv, rv = dy_ref[...], r_ref[...]
        xn = x_ref[...] * rv
        dxn = dyv * g_ref[...]
        dx_ref[...] = rv * (dxn - xn * jnp.mean(dxn * xn, axis=-1, keepdims=True))

        @pl.when(pl.program_id(0) == 0)
        def _():
            dg_ref[...] = jnp.zeros_like(dg_ref)

        dg_ref[...] += jnp.sum(dyv * xn, axis=0, keepdims=True)

    row = pl.BlockSpec((tr, d), lambda i: (i, 0))
    vec = pl.BlockSpec((1, d), lambda i: (0, 0))
    return pl.pallas_call(
        body,
        out_shape=(jax.ShapeDtypeStruct((s, d), F32), jax.ShapeDtypeStruct((1, d), F32)),
        grid=(s // tr,),
        in_specs=[row, row, pl.BlockSpec((tr, 1), lambda i: (i, 0)), vec],
        out_specs=(row, vec),
        compiler_params=_params("arbitrary"),
        name=name,
    )(dy, x, r, g.reshape(1, d))


def _rope_tables(s, inverse):
    inv_freq = ROPE_BASE ** (-jnp.arange(0, D_ROPE, 2, dtype=F32) / D_ROPE)
    ang = jnp.arange(s, dtype=F32)[:, None] * inv_freq[None, :]
    cos, sin = jnp.cos(ang), jnp.sin(ang)
    if inverse:
        sin = -sin
    half = D_ROPE // 2
    one, zero = jnp.ones((s, D_NOPE), F32), jnp.zeros((s, D_NOPE), F32)
    pad1, pad0 = jnp.ones((s, LANES - D_NOPE - D_ROPE), F32), jnp.zeros((s, LANES - D_NOPE - D_ROPE), F32)
    zh = jnp.zeros((s, half), F32)
    c = jnp.concatenate([one, cos, cos, pad1], axis=1)
    s_lo = jnp.concatenate([zero, -sin, zh, pad0], axis=1)
    s_hi = jnp.concatenate([zero, zh, sin, pad0], axis=1)
    return c, s_lo, s_hi


def _rope(x, tables, *, out_dtype, head_sum=False, name):
    h, s, w = x.shape
    ts = _pick(s, 2048, 8)

    def body(x_ref, c_ref, lo_ref, hi_ref, y_ref, *sum_ref):
        y = _rotate(x_ref[0], c_ref[...], lo_ref[...], hi_ref[...])
        y_ref[0] = y.astype(y_ref.dtype)
        if head_sum:
            @pl.when(pl.program_id(1) == 0)
            def _():
                sum_ref[0][...] = jnp.zeros_like(sum_ref[0])

            sum_ref[0][...] += y

    tab = pl.BlockSpec((ts, w), lambda i, hh: (i, 0))
    blk = pl.BlockSpec((1, ts, w), lambda i, hh: (hh, i, 0))
    out_shape = [jax.ShapeDtypeStruct((h, s, w), out_dtype)]
    out_specs = [blk]
    if head_sum:
        out_shape.append(jax.ShapeDtypeStruct((s, w), F32))
        out_specs.append(tab)
    res = pl.pallas_call(
        body,
        out_shape=tuple(out_shape),
        grid=(s // ts, h),
        in_specs=[blk, tab, tab, tab],
        out_specs=tuple(out_specs),
        compiler_params=_params("parallel", "arbitrary"),
        name=name,
    )(x, *tables)
    return res if head_sum else res[0]


def _band_scores(q, kw, slope, i, *, scale, n_back, bps):
    b, r, _ = q.shape
    sc = _bdot(q, kw, NT) * scale
    shape = (b, r, 2 * BLK)
    row = lax.broadcasted_iota(jnp.int32, shape, 1) & (BLK - 1)
    col = lax.broadcasted_iota(jnp.int32, shape, 2)
    rel = BLK + row - col
    first_col = jnp.where(i % bps == 0, BLK, 0)
    valid = (rel >= 0) & (rel <= n_back) & (col >= first_col)
    return jnp.where(valid, sc - slope * rel.astype(F32), NEG)


def _band_fwd(q, k, v, slope, sink, *, scale, n_back, bps, name):
    g, b, rows, dh = q.shape
    r = slope.shape[2]
    nq = rows // r
    skv = k.shape[2]
    use_sink = sink is not None

    def body(*refs):
        q_ref, k_ref, v_ref, slope_ref = refs[:4]
        sink_ref = refs[4] if use_sink else None
        o_ref, lse_ref = refs[4 + use_sink:]
        i = pl.program_id(1)
        off = pl.multiple_of(i * BLK, BLK)
        kw = k_ref[0, :, pl.ds(off, 2 * BLK), :]
        vw = v_ref[0, :, pl.ds(off, 2 * BLK), :]
        sc = _band_scores(q_ref[0], kw, slope_ref[0], i, scale=scale, n_back=n_back, bps=bps)
        m = jnp.max(sc, axis=-1, keepdims=True)
        if use_sink:
            m = jnp.maximum(m, sink_ref[0])
        p = jnp.exp(sc - m)
        l = jnp.sum(p, axis=-1, keepdims=True)
        if use_sink:
            l = l + jnp.exp(sink_ref[0] - m)
        o_ref[0] = _bdot(p.astype(MXU_DTYPE), vw, NN) / l
        lse_ref[0] = m + jnp.log(l)

    qspec = pl.BlockSpec((1, b, r, dh), lambda gg, i: (gg, 0, i, 0))
    kspec = pl.BlockSpec((1, b, skv, dh), lambda gg, i: (gg, 0, 0, 0))
    rspec = pl.BlockSpec((1, b, r, 1), lambda gg, i: (gg, 0, 0, 0))
    ins = [q, k, v, slope] + ([sink] if use_sink else [])
    return pl.pallas_call(
        body,
        out_shape=(jax.ShapeDtypeStruct((g, b, rows, dh), F32), jax.ShapeDtypeStruct((g, b, rows, 1), F32)),
        grid=(g, nq),
        in_specs=[qspec, kspec, kspec, rspec] + ([rspec] if use_sink else []),
        out_specs=(qspec, pl.BlockSpec((1, b, r, 1), lambda gg, i: (gg, 0, i, 0))),
        compiler_params=_params("parallel", "arbitrary"),
        name=name,
    )(*ins)


def _band_bwd(q, k, v, do, o, lse, slope, sink, *, scale, n_back, bps, name):
    g, b, rows, dh = q.shape
    r = slope.shape[2]
    nq = rows // r
    skv = k.shape[2]
    use_sink = sink is not None
    stacked = r // BLK

    def body(*refs):
        q_ref, k_ref, v_ref, do_ref, o_ref, lse_ref, slope_ref = refs[:7]
        sink_ref = refs[7] if use_sink else None
        dq_ref, dk_ref, dv_ref = refs[7 + use_sink:10 + use_sink]
        i = pl.program_id(1)

        @pl.when(i == 0)
        def _():
            dk_ref[...] = jnp.zeros_like(dk_ref)
            dv_ref[...] = jnp.zeros_like(dv_ref)

        off = pl.multiple_of(i * BLK, BLK)
        qb = q_ref[0]
        kw = k_ref[0, :, pl.ds(off, 2 * BLK), :]
        vw = v_ref[0, :, pl.ds(off, 2 * BLK), :]
        dof = do_ref[0]
        dob = dof.astype(MXU_DTYPE)
        lse_b = lse_ref[0]
        delta = jnp.sum(dof * o_ref[0], axis=-1, keepdims=True)
        sc = _band_scores(qb, kw, slope_ref[0], i, scale=scale, n_back=n_back, bps=bps)
        p = jnp.exp(sc - lse_b)
        ds = (p * (_bdot(dob, vw, NT) - delta) * scale).astype(MXU_DTYPE)
        dq_ref[0] = _bdot(ds, kw, NN)
        dk_ref[0, :, pl.ds(off, 2 * BLK), :] += _bdot(ds, qb, TN)
        dv_ref[0, :, pl.ds(off, 2 * BLK), :] += _bdot(p.astype(MXU_DTYPE), dob, TN)

        if use_sink:
            dsink_ref = refs[10 + use_sink]

            @pl.when(i == 0)
            def _():
                dsink_ref[...] = jnp.zeros_like(dsink_ref)

            contrib = -jnp.exp(sink_ref[0] - lse_b) * delta
            for n in range(stacked):
                part = jnp.sum(contrib[0, n * BLK:(n + 1) * BLK, :], axis=0, keepdims=True)
                dsink_ref[0, n:n + 1, :] += jnp.broadcast_to(part, (1, LANES))

    def qspec(w):
        return pl.BlockSpec((1, b, r, w), lambda gg, i: (gg, 0, i, 0))

    kspec = pl.BlockSpec((1, b, skv, dh), lambda gg, i: (gg, 0, 0, 0))
    rspec = pl.BlockSpec((1, b, r, 1), lambda gg, i: (gg, 0, 0, 0))
    ins = [q, k, v, do, o, lse, slope] + ([sink] if use_sink else [])
    in_specs = [qspec(dh), kspec, kspec, qspec(dh), qspec(dh), qspec(1), rspec] + ([rspec] if use_sink else [])
    out_shape = [jax.ShapeDtypeStruct((g, b, rows, dh), F32), jax.ShapeDtypeStruct((g, b, skv, dh), F32),
                 jax.ShapeDtypeStruct((g, b, skv, dh), F32)]
    out_specs = [qspec(dh), kspec, kspec]
    if use_sink:
        assert b == 1
        out_shape.append(jax.ShapeDtypeStruct((g, stacked, LANES), F32))
        out_specs.append(pl.BlockSpec((1, stacked, LANES), lambda gg, i: (gg, 0, 0)))
    return pl.pallas_call(
        body,
        out_shape=tuple(out_shape),
        grid=(g, nq),
        in_specs=in_specs,
        out_specs=tuple(out_specs),
        compiler_params=_params("parallel", "arbitrary"),
        name=name,
    )(*ins)


def _merge(outs, lses, *, name):
    h, s, dv = outs[0].shape
    ts = _pick(s, 512, 8)

    def body(o0, o1, o2, l0, l1, l2, ob_ref, lt_ref):
        a, b, c = l0[0], l1[0], l2[0]
        m = jnp.maximum(jnp.maximum(a, b), c)
        ea, eb, ec = jnp.exp(a - m), jnp.exp(b - m), jnp.exp(c - m)
        den = ea + eb + ec
        ob_ref[0] = (ea / den) * o0[0] + (eb / den) * o1[0] + (ec / den) * o2[0]
        lt_ref[0] = m + jnp.log(den)

    ospec = pl.BlockSpec((1, ts, dv), lambda hh, i: (hh, i, 0))
    lspec = pl.BlockSpec((1, ts, 1), lambda hh, i: (hh, i, 0))
    return pl.pallas_call(
        body,
        out_shape=(jax.ShapeDtypeStruct((h, s, dv), F32), jax.ShapeDtypeStruct((h, s, 1), F32)),
        grid=(h, s // ts),
        in_specs=[ospec] * 3 + [lspec] * 3,
        out_specs=(ospec, lspec),
        compiler_params=_params("parallel", "parallel"),
        name=name,
    )(*outs, *lses)


def _tile_iotas(t):
    return lax.broadcasted_iota(jnp.int32, (t, t), 0), lax.broadcasted_iota(jnp.int32, (t, t), 1)


def _rotate(x, c, s_lo, s_hi):
    half = D_ROPE // 2
    return x * c + pltpu.roll(x, LANES - half, 1) * s_lo + pltpu.roll(x, half, 1) * s_hi


def _mla_keys(kv_h, kr_t, first):
    return jnp.where(first, kv_h, kr_t)


def _mla_fwd(qd, kvd, krp, tables, *, scale, name):
    s = qd.shape[0]
    pairs = qd.shape[1] // (2 * LANES)
    t = min(CAUSAL_TILE, s)

    def body(q_ref, kv_ref, kr_ref, c_ref, lo_ref, hi_ref, o_ref, lse_ref):
        i = pl.program_id(1)
        first = lax.broadcasted_iota(jnp.int32, (1, LANES), 1) < HEAD_DIM
        tabs = (c_ref[...], lo_ref[...], hi_ref[...])
        q_heads = [_rotate(q_ref[:, hh * LANES:(hh + 1) * LANES], *tabs).astype(MXU_DTYPE) for hh in range(2)]

        def tile(j, carry, diagonal):
            off = pl.multiple_of(j * t, t)
            kr_t = kr_ref[pl.ds(off, t), :]
            out = []
            for hh in range(2):
                m, l, acc = carry[3 * hh:3 * hh + 3]
                kv_h = kv_ref[pl.ds(off, t), hh * LANES:(hh + 1) * LANES]
                sc = _dot(q_heads[hh], _mla_keys(kv_h, kr_t, first), NT) * scale
                if diagonal:
                    row, col = _tile_iotas(t)
                    sc = jnp.where(row >= col, sc, NEG)
                m_new = jnp.maximum(m, jnp.max(sc, axis=-1, keepdims=True))
                a = jnp.exp(m - m_new)
                p = jnp.exp(sc - m_new)
                out += [m_new, a * l + jnp.sum(p, axis=-1, keepdims=True), a * acc + _dot(p.astype(MXU_DTYPE), kv_h, NN)]
            return tuple(out)

        init = (jnp.full((t, 1), NEG, F32), jnp.zeros((t, 1), F32), jnp.zeros((t, LANES), F32)) * 2
        carry = lax.fori_loop(0, i, lambda j, c: tile(j, c, False), init)
        m0, l0, acc0, m1, l1, acc1 = tile(i, carry, True)
        o_ref[...] = jnp.where(first, pltpu.roll(acc0 / l0, HEAD_DIM, 1), acc1 / l1)
        lse_ref[0] = jnp.where(lax.broadcasted_iota(jnp.int32, (t, 2), 1) == 0, m0 + jnp.log(l0), m1 + jnp.log(l1))

    tab = pl.BlockSpec((t, LANES), lambda p, i: (i, 0))
    return pl.pallas_call(
        body,
        out_shape=(jax.ShapeDtypeStruct((s, pairs * LANES), F32), jax.ShapeDtypeStruct((pairs, s, 2), F32)),
        grid=(pairs, s // t),
        in_specs=[pl.BlockSpec((t, 2 * LANES), lambda p, i: (i, p)), pl.BlockSpec((s, 2 * LANES), lambda p, i: (0, p)),
                  pl.BlockSpec((s, LANES), lambda p, i: (0, 0)), tab, tab, tab],
        out_specs=(pl.BlockSpec((t, LANES), lambda p, i: (i, p)), pl.BlockSpec((1, t, 2), lambda p, i: (p, i, 0))),
        compiler_params=_params("parallel", "arbitrary"),
        name=name,
    )(qd, kvd, krp, *tables)


def _mla_bwd(qd, kvd, krp, tables, do, o, lse, *, do_block0, scale, name):
    s = qd.shape[0]
    pairs = qd.shape[1] // (2 * LANES)
    t = min(CAUSAL_TILE, s)

    def body(q_ref, kv_ref, kr_ref, c_ref, lo_ref, hi_ref, do_ref, o_ref, lse_ref, dq_ref, dkv_ref, dkr_ref):
        i = pl.program_id(1)

        @pl.when(i == 0)
        def _():
            dkv_ref[...] = jnp.zeros_like(dkv_ref)
            dkr_ref[...] = jnp.zeros_like(dkr_ref)

        first = lax.broadcasted_iota(jnp.int32, (1, LANES), 1) < HEAD_DIM
        tabs = (c_ref[...], lo_ref[...], hi_ref[...])
        q_heads = [_rotate(q_ref[:, hh * LANES:(hh + 1) * LANES], *tabs).astype(MXU_DTYPE) for hh in range(2)]
        dof = do_ref[...]
        prod = dof * o_ref[...]
        deltas = [jnp.sum(jnp.where(first, prod, 0.0), axis=-1, keepdims=True),
                  jnp.sum(jnp.where(first, 0.0, prod), axis=-1, keepdims=True)]
        do_heads = [jnp.where(first, 0.0, pltpu.roll(dof, HEAD_DIM, 1)).astype(MXU_DTYPE),
                    jnp.where(first, 0.0, dof).astype(MXU_DTYPE)]
        lses = [lse_ref[0][:, hh:hh + 1] for hh in range(2)]

        def tile(j, carry, diagonal):
            off = pl.multiple_of(j * t, t)
            kr_t = kr_ref[pl.ds(off, t), :]
            out, dkr_add = [], None
            for hh in range(2):
                kv_h = kv_ref[pl.ds(off, t), hh * LANES:(hh + 1) * LANES]
                k_h = _mla_keys(kv_h, kr_t, first)
                sc = _dot(q_heads[hh], k_h, NT) * scale
                if diagonal:
                    row, col = _tile_iotas(t)
                    sc = jnp.where(row >= col, sc, NEG)
                p = jnp.exp(sc - lses[hh])
                ds = (p * (_dot(do_heads[hh], kv_h, NT) - deltas[hh]) * scale).astype(MXU_DTYPE)
                dk_full = _dot(ds, q_heads[hh], TN)
                dv_full = _dot(p.astype(MXU_DTYPE), do_heads[hh], TN)
                dkv_ref[pl.ds(off, t), hh * LANES:(hh + 1) * LANES] += jnp.where(first, dk_full, dv_full)
                rot = jnp.where(first, 0.0, dk_full)
                dkr_add = rot if dkr_add is None else dkr_add + rot
                out.append(carry[hh] + _dot(ds, k_h, NN))
            dkr_ref[0, pl.ds(off, t), :] += dkr_add
            return tuple(out)

        zacc = jnp.zeros((t, LANES), F32)
        carry = lax.fori_loop(0, i, lambda j, c: tile(j, c, False), (zacc, zacc))
        dq_heads = tile(i, carry, True)
        for hh in range(2):
            dq_ref[:, hh * LANES:(hh + 1) * LANES] = _rotate(dq_heads[hh], tabs[0], -tabs[1], -tabs[2])

    tab = pl.BlockSpec((t, LANES), lambda p, i: (i, 0))
    qspec = pl.BlockSpec((t, 2 * LANES), lambda p, i: (i, p))
    kvspec = pl.BlockSpec((s, 2 * LANES), lambda p, i: (0, p))
    return pl.pallas_call(
        body,
        out_shape=(jax.ShapeDtypeStruct(qd.shape, F32), jax.ShapeDtypeStruct(kvd.shape, F32),
                   jax.ShapeDtypeStruct((pairs, s, LANES), F32)),
        grid=(pairs, s // t),
        in_specs=[qspec, kvspec, pl.BlockSpec((s, LANES), lambda p, i: (0, 0)), tab, tab, tab,
                  pl.BlockSpec((t, LANES), lambda p, i: (i, do_block0 + p)), pl.BlockSpec((t, LANES), lambda p, i: (i, p)),
                  pl.BlockSpec((1, t, 2), lambda p, i: (p, i, 0))],
        out_specs=(qspec, kvspec, pl.BlockSpec((1, s, LANES), lambda p, i: (p, 0, 0))),
        compiler_params=_params("parallel", "arbitrary"),
        name=name,
    )(qd, kvd, krp, *tables, do, o, lse)


def _split_cumsum(x, tri, terms=2):
    hi = x.astype(BF16)
    if terms == 1:
        return _dot(hi, tri, NN)
    lo = (x - hi.astype(F32)).astype(BF16)
    return _dot(hi, tri, NN) + _dot(lo, tri, NN)


def _chunked_cumsum(x, tri, run, *, reverse, sign, terms=2):
    c = tri.shape[0]
    n = x.shape[1] // c
    parts = [None] * n
    for idx in (reversed(range(n)) if reverse else range(n)):
        xc = x[:, idx * c:(idx + 1) * c]
        parts[idx] = sign * (run + _split_cumsum(xc, tri, terms))
        run = run + jnp.sum(xc, axis=-1, keepdims=True)
    return (parts[0] if n == 1 else jnp.concatenate(parts, axis=1)), run


def _sb_logs(z):
    e = jnp.exp(-jnp.abs(z))
    l1 = jnp.log(1.0 + e)
    return e, jnp.minimum(z, 0.0) - l1, -jnp.maximum(z, 0.0) - l1


def _pair_masks():
    first = lax.broadcasted_iota(jnp.int32, (1, LANES), 1) < HEAD_DIM
    m0 = first.astype(MXU_DTYPE)
    return first, (m0, 1 - m0)


def _sb_fwd(qkv, *, heads, scale, name):
    s = qkv.shape[0]
    pairs = heads * HEAD_DIM // LANES
    t = min(CAUSAL_TILE, s)
    cc = min(CUM_CHUNK, t)

    def body(q_ref, k_ref, v_ref, o_ref, t_ref):
        i = pl.program_id(1)
        first, masks = _pair_masks()
        q_heads = [q_ref[...] * m for m in masks]
        crow, ccol = _tile_iotas(cc)
        after = (crow > ccol).astype(BF16)

        def tile(j, carry, diagonal):
            off = pl.multiple_of(j * t, t)
            kb = k_ref[pl.ds(off, t), :]
            vb = v_ref[pl.ds(off, t), :]
            if diagonal:
                row, col = _tile_iotas(t)
                strict = row > col
            out = []
            for hh in range(2):
                run, acc = carry[2 * hh], carry[2 * hh + 1]
                z = _dot(q_heads[hh], kb, NT) * scale
                _, log_beta, log_keep = _sb_logs(z)
                if diagonal:
                    log_keep = jnp.where(strict, log_keep, 0.0)
                a, run = _chunked_cumsum(log_keep, after, run, reverse=True, sign=1.0)
                w = jnp.exp(log_beta + a)
                if diagonal:
                    w = jnp.where(strict, w, 0.0)
                out += [run, acc + _dot(w.astype(MXU_DTYPE), vb, NN)]
            return tuple(out)

        zero, zacc = jnp.zeros((t, 1), F32), jnp.zeros((t, LANES), F32)
        carry = tile(i, (zero, zacc, zero, zacc), True)
        run0, acc0, run1, acc1 = lax.fori_loop(0, i, lambda jj, c: tile(i - 1 - jj, c, False), carry)
        o_ref[...] = jnp.where(first, acc0, acc1)
        t_ref[0] = jnp.where(lax.broadcasted_iota(jnp.int32, (t, 2), 1) == 0, run0, run1)

    return pl.pallas_call(
        body,
        out_shape=(jax.ShapeDtypeStruct((s, heads * HEAD_DIM), F32), jax.ShapeDtypeStruct((pairs, s, 2), F32)),
        grid=(pairs, s // t),
        in_specs=[pl.BlockSpec((t, LANES), lambda p, i: (i, p)),
                  pl.BlockSpec((s, LANES), lambda p, i: (0, pairs + p)),
                  pl.BlockSpec((s, LANES), lambda p, i: (0, 2 * pairs + p))],
        out_specs=(pl.BlockSpec((t, LANES), lambda p, i: (i, p)), pl.BlockSpec((1, t, 2), lambda p, i: (p, i, 0))),
        compiler_params=_params("parallel", "arbitrary"),
        name=name,
    )(qkv, qkv, qkv)


def _sb_bwd(qkv, do, total, *, heads, scale, name):
    s = qkv.shape[0]
    pairs = heads * HEAD_DIM // LANES
    t = min(CAUSAL_TILE, s)
    cc = min(CUM_CHUNK, t)

    def body(q_ref, k_ref, v_ref, do_ref, t_ref, dq_ref, dk_ref, dv_ref):
        i = pl.program_id(1)

        @pl.when(i == 0)
        def _():
            dk_ref[...] = jnp.zeros_like(dk_ref)
            dv_ref[...] = jnp.zeros_like(dv_ref)

        first, masks = _pair_masks()
        q_heads = [q_ref[...] * m for m in masks]
        do_b = do_ref[...].astype(MXU_DTYPE)
        do_heads = [do_b * m for m in masks]
        tots = [t_ref[0][:, hh:hh + 1] for hh in range(2)]
        crow, ccol = _tile_iotas(cc)
        upto = (crow <= ccol).astype(BF16)
        before = (crow < ccol).astype(BF16)

        def tile(j, carry, diagonal):
            off = pl.multiple_of(j * t, t)
            kb = k_ref[pl.ds(off, t), :]
            vb = v_ref[pl.ds(off, t), :]
            if diagonal:
                row, col = _tile_iotas(t)
                strict = row > col
            out, dk_add, dv_add = [], None, None
            for hh in range(2):
                run_keep, run_g, dq_acc = carry[3 * hh:3 * hh + 3]
                z = _dot(q_heads[hh], kb, NT) * scale
                e, log_beta, log_keep = _sb_logs(z)
                if diagonal:
                    log_keep = jnp.where(strict, log_keep, 0.0)
                a, run_keep = _chunked_cumsum(log_keep, upto, run_keep - tots[hh], reverse=False, sign=-1.0)
                run_keep = run_keep + tots[hh]
                w = jnp.exp(log_beta + a)
                if diagonal:
                    w = jnp.where(strict, w, 0.0)
                g = w * _dot(do_heads[hh], vb, NT)
                prefix, run_g = _chunked_cumsum(g, before, run_g, reverse=False, sign=1.0, terms=1)
                pos = z >= 0.0
                dz = (g * jnp.where(pos, e, 1.0) - jnp.where(pos, 1.0, e) * prefix) * pl.reciprocal(1.0 + e, approx=True)
                if diagonal:
                    dz = jnp.where(strict, dz, 0.0)
                dz = (dz * scale).astype(MXU_DTYPE)
                dk_h = _dot(dz, q_heads[hh], TN)
                dv_h = _dot(w.astype(MXU_DTYPE), do_heads[hh], TN)
                dk_add = dk_h if dk_add is None else dk_add + dk_h
                dv_add = dv_h if dv_add is None else dv_add + dv_h
                out += [run_keep, run_g, dq_acc + _dot(dz, kb, NN)]
            dk_ref[pl.ds(off, t), :] += dk_add
            dv_ref[pl.ds(off, t), :] += dv_add
            return tuple(out)

        zero, zacc = jnp.zeros((t, 1), F32), jnp.zeros((t, LANES), F32)
        carry = lax.fori_loop(0, i, lambda j, c: tile(j, c, False), (zero, zero, zacc, zero, zero, zacc))
        res = tile(i, carry, True)
        dq_ref[...] = jnp.where(first, res[2], res[5])

    qspec = pl.BlockSpec((t, LANES), lambda p, i: (i, p))
    shp = jax.ShapeDtypeStruct((s, heads * HEAD_DIM), F32)
    return pl.pallas_call(
        body,
        out_shape=(shp, shp, shp),
        grid=(pairs, s // t),
        in_specs=[qspec, pl.BlockSpec((s, LANES), lambda p, i: (0, pairs + p)),
                  pl.BlockSpec((s, LANES), lambda p, i: (0, 2 * pairs + p)), qspec,
                  pl.BlockSpec((1, t, 2), lambda p, i: (p, i, 0))],
        out_specs=(qspec, pl.BlockSpec((s, LANES), lambda p, i: (0, p)), pl.BlockSpec((s, LANES), lambda p, i: (0, p))),
        compiler_params=_params("parallel", "arbitrary"),
        name=name,
    )(qkv, qkv, qkv, do, total)


def _loss_head(y, target, *, name):
    s, d = y.shape
    tr = _pick(s, 256, 8)

    def body(y_ref, t_ref, dy_ref, loss_ref):
        err = y_ref[...] - t_ref[...]
        dy_ref[...] = err * (1.0 / d)

        @pl.when(pl.program_id(0) == 0)
        def _():
            loss_ref[...] = jnp.zeros_like(loss_ref)

        per_tok = jnp.mean(err * err, axis=-1, keepdims=True)
        loss_ref[...] += 0.5 * jnp.sum(per_tok, axis=0, keepdims=True)

    row = pl.BlockSpec((tr, d), lambda i: (i, 0))
    return pl.pallas_call(
        body,
        out_shape=(jax.ShapeDtypeStruct((s, d), F32), jax.ShapeDtypeStruct((1, LANES), F32)),
        grid=(s // tr,),
        in_specs=[row, row],
        out_specs=(row, pl.BlockSpec((1, LANES), lambda i: (0, 0))),
        compiler_params=_params("arbitrary"),
        name=name,
    )(y, target)


def _adamw(w, grads, m, v, *, name, deps=()):
    nl, r, c = w.shape
    cp = grads[0].shape[1]
    tr = _pick(r, 256, 8)

    def body(*refs):
        w_ref, m_ref, v_ref = refs[:3]
        g_refs = refs[3:3 + nl]
        g_out, d_ref, m2_ref, v2_ref = refs[3 + nl + len(deps):]
        layer = pl.program_id(0)
        gv = g_refs[0][:, :c]
        for n in range(1, nl):
            gv = jnp.where(layer == n, g_refs[n][:, :c], gv)
        m2 = ADAM_B1 * m_ref[0] + (1.0 - ADAM_B1) * gv
        v2 = ADAM_B2 * v_ref[0] + (1.0 - ADAM_B2) * (gv * gv)
        m_hat = m2 / (1.0 - ADAM_B1 ** ADAM_STEP)
        v_hat = v2 / (1.0 - ADAM_B2 ** ADAM_STEP)
        g_out[0] = gv
        d_ref[0] = -ADAM_LR * (m_hat / (jnp.sqrt(v_hat) + ADAM_EPS) + ADAM_WD * w_ref[0])
        m2_ref[0] = m2
        v2_ref[0] = v2

    blk = pl.BlockSpec((1, tr, c), lambda l, i: (l, i, 0))
    gspec = pl.BlockSpec((tr, cp), lambda l, i: (i, 0))
    shp = jax.ShapeDtypeStruct((nl, r, c), F32)
    return pl.pallas_call(
        body,
        out_shape=(shp, shp, shp, shp),
        grid=(nl, r // tr),
        in_specs=[blk, blk, blk] + [gspec] * nl + [ANY] * len(deps),
        out_specs=(blk, blk, blk, blk),
        compiler_params=_params("parallel", "parallel"),
        name=name,
    )(w, m, v, *grads, *deps)


def _pair_sum(mine, recv, my_c, *, name):
    _, r, c = mine.shape
    tr = _pick(r, 512, 16)

    def body(c_ref, a_ref, b_ref, o_ref):
        o_ref[0] = (a_ref[0].astype(F32) + b_ref[0].astype(F32)).astype(o_ref.dtype)

    grid_spec = pltpu.PrefetchScalarGridSpec(
        num_scalar_prefetch=1,
        grid=(4, r // tr),
        in_specs=[pl.BlockSpec((1, tr, c), lambda kk, i, c_ref: (2 * kk + c_ref[0], i, 0)),
                  pl.BlockSpec((1, tr, c), lambda kk, i, c_ref: (kk, i, 0))],
        out_specs=pl.BlockSpec((1, tr, c), lambda kk, i, c_ref: (kk, i, 0)),
    )
    return pl.pallas_call(
        body,
        out_shape=jax.ShapeDtypeStruct((4, r, c), mine.dtype),
        grid_spec=grid_spec,
        compiler_params=_params("parallel", "parallel"),
        name=name,
    )(my_c.reshape(1).astype(jnp.int32), mine, recv)


def _final_sum(partial, recv, my_chip, *, name):
    _, r, c = partial.shape
    tr = _pick(r, 512, 16)

    def body(chip_ref, p_ref, r0, r1, r2, o_ref):
        o_ref[...] = ((p_ref[0].astype(F32) + r0[0].astype(F32)) + r1[0].astype(F32)) + r2[0].astype(F32)

    def slot(n):
        return pl.BlockSpec((1, tr, c), lambda i, chip_ref: (n, i, 0))

    grid_spec = pltpu.PrefetchScalarGridSpec(
        num_scalar_prefetch=1,
        grid=(r // tr,),
        in_specs=[pl.BlockSpec((1, tr, c), lambda i, chip_ref: (chip_ref[0], i, 0)), slot(0), slot(1), slot(2)],
        out_specs=pl.BlockSpec((tr, c), lambda i, chip_ref: (i, 0)),
    )
    return pl.pallas_call(
        body,
        out_shape=jax.ShapeDtypeStruct((r, c), F32),
        grid_spec=grid_spec,
        compiler_params=_params("parallel"),
        name=name,
    )(my_chip.reshape(1).astype(jnp.int32), partial, recv, recv, recv)


def _sum_devices(stack, *, name):
    n, r, c = stack.shape

    def body(s_ref, o_ref):
        acc = s_ref[0]
        for dev in range(1, n):
            acc = acc + s_ref[dev]
        o_ref[...] = acc

    return pl.pallas_call(
        body,
        out_shape=jax.ShapeDtypeStruct((r, c), F32),
        in_specs=[pl.BlockSpec(memory_space=pltpu.VMEM)],
        out_specs=pl.BlockSpec(memory_space=pltpu.VMEM),
        name=name,
    )(stack)


def _mesh_pos():
    return lax.axis_index("x"), lax.axis_index("y"), lax.axis_index("c")


def _all_gather(shards, *, name, deps=()):
    n = len(shards)

    def body(*refs):
        x_refs, out_refs = refs[:n], refs[n + len(deps):2 * n + len(deps)]
        send_sems, recv_sems, local_sems = refs[2 * n + len(deps):]
        x, y, cc = _mesh_pos()
        me, sibling = (x, y, cc), (x, y, 1 - cc)
        chips = [(1 - x, y), (x, 1 - y), (1 - x, 1 - y)]

        def rows(a, px, py, pc):
            return out_refs[a].at[4 * px + 2 * py + pc]

        def copy(a, kk, block, to, src=None):
            return pltpu.make_async_remote_copy(
                src_ref=rows(a, *block) if src is None else src, dst_ref=rows(a, *block),
                send_sem=send_sems.at[7 * a + kk], recv_sem=recv_sems.at[7 * a + kk],
                device_id=to, device_id_type=MESH)

        sends, own = [], []
        for a in range(n):
            own.append(pltpu.make_async_copy(x_refs[a], rows(a, *me), local_sems.at[a]))
            own[a].start()
            first = [copy(a, 0, me, sibling, src=x_refs[a])]
            first += [copy(a, 1 + j, me, (*chip, cc), src=x_refs[a]) for j, chip in enumerate(chips)]
            for cp in first:
                cp.start()
            sends += first
        for a in range(n):
            for j, chip in enumerate(chips):
                copy(a, 1 + j, (*chip, cc), me).wait_recv()
                passed = copy(a, 4 + j, (*chip, cc), sibling)
                passed.start()
                sends.append(passed)
        for a in range(n):
            copy(a, 0, sibling, me).wait_recv()
            for j, chip in enumerate(chips):
                copy(a, 4 + j, (*chip, 1 - cc), me).wait_recv()
        for cp in sends:
            cp.wait_send()
        for cp in own:
            cp.wait()

    return pl.pallas_call(
        body,
        out_shape=tuple(jax.ShapeDtypeStruct((N_DEV,) + t.shape, t.dtype) for t in shards),
        in_specs=[ANY] * (n + len(deps)),
        out_specs=tuple([ANY] * n),
        scratch_shapes=[pltpu.SemaphoreType.DMA((7 * n,)), pltpu.SemaphoreType.DMA((7 * n,)),
                        pltpu.SemaphoreType.DMA((n,))],
        name=name,
    )(*shards, *deps)


def _plan_own_blocks(n):
    def plan(refs, send_sems, recv_sems, outgoing):
        x, y, cc = _mesh_pos()
        peers = [(x, y, 1 - cc), (1 - x, y, cc), (x, 1 - y, cc), (1 - x, 1 - y, cc)]
        copies = []
        for a in range(n):
            land = refs[n + a]
            for kk, (px, py, pc) in enumerate(peers):
                block = (x, y, cc) if outgoing else (px, py, pc)
                rows = land.at[4 * block[0] + 2 * block[1] + block[2]]
                copies.append(pltpu.make_async_remote_copy(
                    src_ref=refs[a] if outgoing else rows, dst_ref=rows, send_sem=send_sems.at[4 * a + kk],
                    recv_sem=recv_sems.at[4 * a + kk], device_id=(px, py, pc), device_id_type=MESH))
        return copies

    plan.n_sems = 4 * n
    return plan


def _plan_pass_on(n):
    def plan(refs, send_sems, recv_sems, outgoing):
        x, y, cc = _mesh_pos()
        copies = []
        for a in range(n):
            for j, (px, py) in enumerate([(1 - x, y), (x, 1 - y), (1 - x, 1 - y)]):
                rows = refs[a].at[4 * px + 2 * py + (cc if outgoing else 1 - cc)]
                copies.append(pltpu.make_async_remote_copy(
                    src_ref=rows, dst_ref=rows, send_sem=send_sems.at[3 * a + j], recv_sem=recv_sems.at[3 * a + j],
                    device_id=(x, y, 1 - cc), device_id_type=MESH))
        return copies

    plan.n_sems = 3 * n
    return plan


def _plan_to_sibling(n):
    def plan(refs, send_sems, recv_sems, outgoing):
        x, y, cc = _mesh_pos()
        copies = []
        for a in range(n):
            for chip in range(4):
                dst = refs[n + a].at[chip]
                copies.append(pltpu.make_async_remote_copy(
                    src_ref=refs[a].at[2 * chip + (1 - cc)] if outgoing else dst, dst_ref=dst,
                    send_sem=send_sems.at[4 * a + chip], recv_sem=recv_sems.at[4 * a + chip],
                    device_id=(x, y, 1 - cc), device_id_type=MESH))
        return copies

    plan.n_sems = 4 * n
    return plan


def _plan_to_chips(n):
    def plan(refs, send_sems, recv_sems, outgoing):
        x, y, cc = _mesh_pos()
        copies = []
        for a in range(n):
            for j, (px, py) in enumerate([(1 - x, y), (x, 1 - y), (1 - x, 1 - y)]):
                dst = refs[n + a].at[j]
                copies.append(pltpu.make_async_remote_copy(
                    src_ref=refs[a].at[2 * px + py] if outgoing else dst, dst_ref=dst,
                    send_sem=send_sems.at[3 * a + j], recv_sem=recv_sems.at[3 * a + j],
                    device_id=(px, py, cc), device_id_type=MESH))
        return copies

    plan.n_sems = 3 * n
    return plan


def _in_hbm(t):
    return pltpu.with_memory_space_constraint(t, pltpu.HBM)


def _exchange_start(plan, bufs, after, *, name):
    nb, na = len(bufs), len(after)

    def body(*refs):
        outs = refs[nb + na:]
        for cp in plan(refs[:nb], outs[0], outs[1], True):
            cp.start()
        outs[2 + nb][...] = jnp.zeros_like(outs[2 + nb])

    res = pl.pallas_call(
        body,
        out_shape=(pltpu.SemaphoreType.DMA((plan.n_sems,)), pltpu.SemaphoreType.DMA((plan.n_sems,)),
                   *[pltpu.HBM(t.shape, t.dtype) for t in bufs], jax.ShapeDtypeStruct((8, LANES), F32)),
        in_specs=[HBM_SPEC] * nb + [ANY] * na,
        out_specs=(SEM_SPEC, SEM_SPEC, *[HBM_SPEC] * nb, pl.BlockSpec(memory_space=pltpu.VMEM)),
        input_output_aliases={i: 2 + i for i in range(nb)},
        compiler_params=pltpu.CompilerParams(has_side_effects=DATAFLOW_EFFECT),
        name=name,
    )(*[_in_hbm(t) for t in bufs], *after)
    return plan, res[:2], list(res[2:2 + nb]), res[2 + nb]


def _exchange_wait(flight, after, *, name):
    plan, sems, bufs, _ = flight
    nb = len(bufs)

    def body(*refs):
        send_sems, recv_sems = refs[nb], refs[nb + 1]
        for cp in plan(refs[:nb], send_sems, recv_sems, False):
            cp.wait_recv()
        for cp in plan(refs[:nb], send_sems, recv_sems, True):
            cp.wait_send()

    res = pl.pallas_call(
        body,
        out_shape=tuple(pltpu.HBM(t.shape, t.dtype) for t in bufs),
        in_specs=[HBM_SPEC] * nb + [SEM_SPEC, SEM_SPEC] + [ANY] * len(after),
        out_specs=tuple([HBM_SPEC] * nb),
        input_output_aliases={i: i for i in range(nb)},
        compiler_params=pltpu.CompilerParams(has_side_effects=DATAFLOW_EFFECT),
        name=name,
    )(*bufs, *sems, *after)
    return list(res)


_W_GROUPS = {"even": ("even_w_in", "even_w_out"), "mlp0": ("mlp_w1_0", "mlp_w2_0"),
             "odd": ("odd_w_in", "odd_w_uq", "odd_w_ukv", "odd_w_out"), "mlp1": ("mlp_w1_1", "mlp_w2_1")}


class _Exchanges:
    def __init__(self, shards, gains, n_q, n_kv):
        self.n_q, self.n_kv = n_q, n_kv
        self.mx, self.my, self.mc = _mesh_pos()
        self.dev = 4 * self.mx + 2 * self.my + self.mc
        self.shards = shards
        self.gains = gains
        self.flights, self.gathered, self.grad_blocks, self.grad_names, self.reduced = {}, {}, {}, {}, {}

    def start(self):
        names = _W_GROUPS["even"]
        got = _all_gather([self.shards[n] for n in names] + [self.gains], name="comm_even_gather")
        self.gathered.update(zip(names, got[:-1]))
        self.all_gains = got[-1][:, 0]
        return self._w_begin("mlp0", [got[0]])

    def _w_begin(self, group, after):
        srcs = [self.shards[n] for n in _W_GROUPS[group]]
        lands = [lax.dynamic_update_slice(lax.empty((N_DEV,) + t.shape, t.dtype), t[None], (self.dev, 0, 0)) for t in srcs]
        self.flights[group] = _exchange_start(_plan_own_blocks(len(srcs)), srcs + lands, after, name=f"comm_{group}_own_start")
        return [self.flights[group][3]]

    def _w_turn(self, group, after):
        bufs = _exchange_wait(self.flights[group], after, name=f"comm_{group}_own_wait")
        n = len(bufs) // 2
        self.flights[group] = _exchange_start(_plan_pass_on(n), bufs[n:], [], name=f"comm_{group}_pass_start")
        return [self.flights[group][3]]

    def _w_end(self, group, after):
        self.gathered.update(zip(_W_GROUPS[group], _exchange_wait(self.flights.pop(group), after, name=f"comm_{group}_pass_wait")))

    def weights(self, group):
        return {n: self.gathered[n] for n in _W_GROUPS[group]}

    def norm_gains(self):
        return (self.all_gains[:, :self.n_q].reshape(-1), self.all_gains[:, self.n_q:self.n_q + self.n_kv].reshape(-1))

    def grads(self, group, named_blocks):
        self.grad_names[group] = [n for n, _ in named_blocks]
        self.grad_blocks[group] = [t for _, t in named_blocks]

    def _g_begin(self, group, after):
        blocks = self.grad_blocks[group]
        lands = [lax.empty((4,) + t.shape[1:], t.dtype) for t in blocks]
        self.flights[group] = _exchange_start(_plan_to_sibling(len(blocks)), blocks + lands, after, name=f"comm_{group}_sib_start")
        return [self.flights[group][3]]

    def _g_turn(self, group, after):
        bufs = _exchange_wait(self.flights[group], after, name=f"comm_{group}_sib_wait")
        n = len(bufs) // 2
        partial = [_pair_sum(a, b, self.mc, name=f"pair_sum_{nm}") for nm, a, b in zip(self.grad_names[group], bufs[:n], bufs[n:])]
        lands = [lax.empty((3,) + t.shape[1:], t.dtype) for t in partial]
        self.flights[group] = _exchange_start(_plan_to_chips(n), partial + lands, [], name=f"comm_{group}_chips_start")
        return [self.flights[group][3]]

    def _g_end(self, group, after):
        bufs = _exchange_wait(self.flights.pop(group), after, name=f"comm_{group}_chips_wait")
        n = len(bufs) // 2
        for nm, a, b in zip(self.grad_names[group], bufs[:n], bufs[n:]):
            self.reduced[nm] = _final_sum(a, b, 2 * self.mx + self.my, name=f"final_sum_{nm}")

    _SCHEDULE = {
        "even_out": (("w_turn", "mlp0"), ("w_begin", "odd")),
        "ln1_l0": (("w_end", "mlp0"),),
        "ln2_l0": (("w_turn", "odd"), ("w_begin", "mlp1"), ("w_end", "odd")),
        "odd_out": (("w_turn", "mlp1"),),
        "ln1_l1": (("w_end", "mlp1"),),
        "dw_l1": (("g_begin", "mlp1"),),
        "ln1_bwd_l1": (("g_turn", "mlp1"),),
        "odd_in_dw": (("g_begin", "odd"),),
        "mlp2_dx_l0": (("g_end", "mlp1"), ("g_turn", "odd")),
        "dw_l0": (("g_begin", "mlp0"),),
        "ln1_bwd_l0": (("g_end", "odd"), ("g_turn", "mlp0")),
        "even_in_dw": (("g_begin", "even"),),
        "even_in_dx": (("g_end", "mlp0"), ("g_turn", "even")),
        "finish": (("g_end", "even"),),
    }

    def sync(self, tag, after):
        latest, started = list(after), []
        for what, group in self._SCHEDULE[tag]:
            out = getattr(self, "_" + what)(group, latest)
            if out:
                latest = started = out
        return started


def _alibi(n):
    return 2.0 ** (-8.0 * np.arange(1, n + 1, dtype=np.float32) / n)


def _heads(t, n):
    s = t.shape[0]
    return t.reshape(s, n, t.shape[1] // n).transpose(1, 0, 2)


def _unheads(t):
    n, s, dh = t.shape
    return t.transpose(1, 0, 2).reshape(s, n * dh)


def _to_strided(t, d):
    h, s, x = t.shape
    return t.reshape(h, s // d, d, x).transpose(0, 2, 1, 3).reshape(h, s, x)


def _from_strided(t, d):
    h, s, x = t.shape
    return t.reshape(h, d, s // d, x).transpose(0, 2, 1, 3).reshape(h, s, x)


def _true_columns(t, c, n_pad):
    r = t.shape[1]
    w = t[:, :, :c].transpose(1, 0, 2).reshape(r, N_DEV * c)
    return jnp.pad(w, ((0, 0), (0, n_pad - N_DEV * c)))


def _column_blocks(t, c, cp):
    r = t.shape[0]
    b = t[:, :N_DEV * c].reshape(r, N_DEV, c).transpose(1, 0, 2)
    return jnp.pad(b, ((0, 0), (0, 0), (0, cp - c)))


def _stack_rows(t, nq):
    return t.reshape(nq, BLK, A_KV_HEADS, A_GROUP, HEAD_DIM).transpose(2, 0, 3, 1, 4).reshape(
        A_KV_HEADS, 1, nq * A_GROUP * BLK, HEAD_DIM)


def _unstack_rows(t, nq):
    return t.reshape(A_KV_HEADS, nq, A_GROUP, BLK, HEAD_DIM).transpose(1, 3, 0, 2, 4).reshape(
        nq * BLK, A_Q_W)


def _lead_block(t):
    return jnp.pad(t, ((0, 0), (BLK, 0), (0, 0)))


def _columns(t):
    return t.transpose(1, 0, 2).reshape(t.shape[1], -1)


def _rows(t):
    return t.reshape(-1, t.shape[2])


def _by_column_block(t):
    return t.reshape(t.shape[0], N_DEV, -1).transpose(1, 0, 2)


def _local_step(x0, target, comm, sinks, ln1_g, ln1_b, ln2_g, ln2_b):
    s, d = x0.shape
    scale_h = 1.0 / math.sqrt(HEAD_DIM)
    scale_d = 1.0 / math.sqrt(D_NOPE + D_ROPE)
    nq = s // BLK
    even_c, odd_c = EVEN_IN // N_DEV, ODD_IN // N_DEV
    bf = lambda t: t.astype(MXU_DTYPE)
    blocks = dict(b_blocks=True)

    tok = comm.start()
    w_even = comm.weights("even")
    even_cp, even_n = w_even["even_w_in"].shape[2], -(-EVEN_IN // 1024) * 1024
    w_even_in, w_even_out = _true_columns(w_even["even_w_in"], even_c, even_n), _columns(w_even["even_w_out"])
    x0b = bf(x0)
    h_e = _mm(x0b, w_even_in, out_dtypes=(MXU_DTYPE,), name="even_in_fwd", deps=tok)
    qa = _stack_rows(h_e[:, :A_Q_W], nq)
    ka = _lead_block(_heads(h_e[:, A_Q_W:A_Q_W + A_KV_W], A_KV_HEADS))[:, None]
    va = _lead_block(_heads(h_e[:, A_Q_W + A_KV_W:A_Q_W + 2 * A_KV_W], A_KV_HEADS))[:, None]
    rows_a = A_GROUP * BLK
    slope_a = jnp.asarray(np.repeat(_alibi(A_Q_HEADS).reshape(A_KV_HEADS, A_GROUP), BLK, axis=1).reshape(
        A_KV_HEADS, 1, rows_a, 1))
    sink_a = jnp.broadcast_to(sinks.reshape(A_KV_HEADS, A_GROUP, 1), (A_KV_HEADS, A_GROUP, BLK)).reshape(
        A_KV_HEADS, 1, rows_a, 1)
    a_cfg = dict(scale=scale_h, n_back=A_WINDOW - 1, bps=nq)
    oa, lse_a = _band_fwd(qa, ka, va, slope_a, sink_a, name="swa_fwd", **a_cfg)
    b_in, b_cfg, b_out, b_lse = [], [], [], []
    base = A_Q_W + 2 * A_KV_W
    for gi, (window, dil) in enumerate(B_PATTERNS):
        blk = h_e[:, base + gi * 3 * B_W: base + (gi + 1) * 3 * B_W].reshape(s, 3, B_HEADS, HEAD_DIM)
        qs, ks, vs = (_to_strided(blk[:, n].transpose(1, 0, 2), dil) for n in range(3))
        slope = jnp.asarray(np.broadcast_to((_alibi(B_HEADS) * dil).reshape(1, B_HEADS, 1, 1), (1, B_HEADS, BLK, 1)))
        ins = (qs[None], _lead_block(ks)[None], _lead_block(vs)[None], slope)
        cfg = dict(scale=scale_h, n_back=window // dil, bps=nq // dil)
        o, lse = _band_fwd(*ins, None, name=f"dil{gi}_fwd", **cfg)
        b_in.append(ins)
        b_cfg.append(cfg)
        b_out.append(_from_strided(o[0], dil))
        b_lse.append(_from_strided(lse[0], dil))
    ob, lse_b = _merge(b_out, b_lse, name="dil_merge")
    y_e = bf(jnp.concatenate([_unstack_rows(oa, nq), _unheads(ob)], axis=1))
    mixed = _mm(y_e, w_even_out, name="even_out_fwd")
    tok = comm.sync("even_out", [mixed])
    x0n, x0nb, xh1_0, r1_0 = _ln_fwd(x0, mixed, ln1_g[0], ln1_b[0], name="ln1_fwd_l0", deps=tok)
    comm.sync("ln1_l0", [x0nb])
    w_mlp0 = comm.weights("mlp0")
    w1_0, w2_0 = w_mlp0["mlp_w1_0"], _rows(w_mlp0["mlp_w2_0"])
    act0, hid0 = _mm(x0nb, w1_0, out_dtypes=(MXU_DTYPE, MXU_DTYPE), epilogue=_relu_sq, name="mlp1_fwd_l0", **blocks)
    mlp = _mm(hid0, w2_0, name="mlp2_fwd_l0")
    x1, x1b, xh2_0, r2_0 = _ln_fwd(x0n, mlp, ln2_g[0], ln2_b[0], name="ln2_fwd_l0")

    tok = comm.sync("ln2_l0", [x1b])
    w_odd = comm.weights("odd")
    odd_cp, odd_n = w_odd["odd_w_in"].shape[2], -(-ODD_IN // 1024) * 1024
    w_odd_in, w_uq, w_ukv, w_odd_out = (_true_columns(w_odd["odd_w_in"], odd_c, odd_n), _columns(w_odd["odd_w_uq"]),
                                        _columns(w_odd["odd_w_ukv"]), _rows(w_odd["odd_w_out"]))
    gq, gkv = comm.norm_gains()
    h_o = _mm(x1b, w_odd_in, name="odd_in_fwd", deps=tok)
    qkv_c = bf(h_o[:, :3 * C_W])
    oc, sb_total = _sb_fwd(qkv_c, heads=C_HEADS, scale=scale_h, name="sb_fwd")
    o_cq, o_ckv, o_kr = 3 * C_W, 3 * C_W + D_Q_RANK, 3 * C_W + D_Q_RANK + D_KV_RANK
    cq, ckv, kr = h_o[:, o_cq:o_ckv], h_o[:, o_ckv:o_kr], h_o[:, o_kr:o_kr + D_ROPE]
    ncq, rq = _rms_fwd(cq, gq, name="rms_q_fwd")
    nckv, rkv = _rms_fwd(ckv, gkv, name="rms_kv_fwd")
    lane_pad = LANES - D_NOPE - D_ROPE
    w_uq = jnp.pad(w_uq.reshape(D_Q_RANK, D_HEADS, D_NOPE + D_ROPE), ((0, 0), (0, 0), (0, lane_pad))).reshape(
        D_Q_RANK, D_HEADS * LANES)
    qd = _mm(ncq, w_uq, name="uq_fwd")
    kvd = _mm(nckv, w_ukv, out_dtypes=(MXU_DTYPE,), name="ukv_fwd")
    rope_t = _rope_tables(s, inverse=False)
    krp = _rope(jnp.pad(kr, ((0, 0), (D_NOPE, lane_pad)))[None], rope_t, out_dtype=MXU_DTYPE, name="rope_k_fwd")[0]
    od, lse_d = _mla_fwd(qd, kvd, krp, rope_t, scale=scale_d, name="mla_fwd")
    y_o = bf(jnp.concatenate([oc, od], axis=1))
    mixed = _mm(y_o, w_odd_out, name="odd_out_fwd")
    tok = comm.sync("odd_out", [mixed])
    x1n, x1nb, xh1_1, r1_1 = _ln_fwd(x1, mixed, ln1_g[1], ln1_b[1], name="ln1_fwd_l1", deps=tok)
    comm.sync("ln1_l1", [x1nb])
    w_mlp1 = comm.weights("mlp1")
    w1_1, w2_1 = w_mlp1["mlp_w1_1"], _rows(w_mlp1["mlp_w2_1"])
    act1, hid1 = _mm(x1nb, w1_1, out_dtypes=(MXU_DTYPE, MXU_DTYPE), epilogue=_relu_sq, name="mlp1_fwd_l1", **blocks)
    mlp = _mm(hid1, w2_1, name="mlp2_fwd_l1")
    y, _, xh2_1, r2_1 = _ln_fwd(x1n, mlp, ln2_g[1], ln2_b[1], name="ln2_fwd_l1")
    dy, loss_vec = _loss_head(y, target, name="loss_head")

    def mlp_block_bwd(g_out, layer, w1, w2, xh2, r2, xh1, r1, act, hid, xnb):
        du2, du2b, dg2, db2 = _ln_bwd(g_out, xh2, r2, ln2_g[layer], name=f"ln2_bwd_l{layer}")
        dpre = _mm(du2b, w2, nt=True, out_dtypes=(MXU_DTYPE,), epilogue=_relu_sq_grad, extra=act, name=f"mlp2_dx_l{layer}")
        tok = comm.sync("mlp2_dx_l0", [dpre]) if layer == 0 else []
        dw2 = _mm(hid, du2b, ta=True, out_dtypes=(BF16,), name=f"mlp2_dw_l{layer}", deps=tok)
        dw1 = _mm(xnb, dpre, ta=True, out_blocks=True, out_dtypes=(BF16,), name=f"mlp1_dw_l{layer}")
        comm.grads(f"mlp{layer}", [(f"mlp_w1_{layer}", dw1), (f"mlp_w2_{layer}", dw2.reshape(N_DEV, -1, d))])
        tok = comm.sync(f"dw_l{layer}", [dw1])
        dxn = _mm(dpre, w1, nt=True, epilogue=_add_alpha, extra=du2, name=f"mlp1_dx_l{layer}", deps=tok, **blocks)
        du1, du1b, dg1, db1 = _ln_bwd(dxn, xh1, r1, ln1_g[layer], name=f"ln1_bwd_l{layer}")
        return du1, du1b, comm.sync(f"ln1_bwd_l{layer}", [du1b]), (dg1, db1, dg2, db2)

    du1, du1b, tok, ln_1 = mlp_block_bwd(dy, 1, w1_1, w2_1, xh2_1, r2_1, xh1_1, r1_1, act1, hid1, x1nb)
    d_odd_out = _mm(y_o, du1b, ta=True, out_dtypes=(BF16,), name="odd_out_dw").reshape(N_DEV, -1, d)
    dy_o = _mm(du1b, w_odd_out, nt=True, name="odd_out_dx", deps=tok)
    dqc, dkc, dvc = _sb_bwd(qkv_c, dy_o, sb_total, heads=C_HEADS, scale=scale_h, name="sb_bwd")
    dqd, dkvd, dkr_pairs = _mla_bwd(qd, kvd, krp, rope_t, dy_o, od, lse_d, do_block0=C_W // LANES, scale=scale_d,
                                    name="mla_bwd")
    _, dkr_sum = _rope(dkr_pairs, _rope_tables(s, inverse=True), out_dtype=F32, head_sum=True, name="rope_k_bwd")
    dqd, dkvd = bf(dqd), bf(dkvd)
    d_uq = _mm(ncq, dqd, ta=True, out_dtypes=(BF16,), name="uq_dw").reshape(D_Q_RANK, D_HEADS, LANES)[:, :, :D_NOPE + D_ROPE].reshape(
        D_Q_RANK, D_HEADS * (D_NOPE + D_ROPE))
    dncq = _mm(dqd, w_uq, nt=True, name="uq_dx")
    d_ukv = _mm(nckv, dkvd, ta=True, out_dtypes=(BF16,), name="ukv_dw")
    dnckv = _mm(dkvd, w_ukv, nt=True, name="ukv_dx")
    dcq, dgq = _rms_bwd(dncq, cq, rq, gq, name="rms_q_bwd")
    dckv, dgkv = _rms_bwd(dnckv, ckv, rkv, gkv, name="rms_kv_bwd")
    dh_o = bf(jnp.concatenate(
        [dqc, dkc, dvc, dcq, dckv, dkr_sum[:, D_NOPE:D_NOPE + D_ROPE], jnp.zeros((s, odd_n - ODD_IN), F32)], axis=1))
    d_odd_in = _column_blocks(_mm(x1b, dh_o, ta=True, out_dtypes=(BF16,), name="odd_in_dw"), odd_c, odd_cp)
    comm.grads("odd", [("odd_w_in", d_odd_in), ("odd_w_uq", _by_column_block(d_uq)),
                       ("odd_w_ukv", _by_column_block(d_ukv)), ("odd_w_out", d_odd_out)])
    tok = comm.sync("odd_in_dw", [d_odd_in])
    dx1 = _mm(dh_o, w_odd_in, nt=True, epilogue=_add_alpha, extra=du1, name="odd_in_dx", deps=tok)

    du1, du1b, tok, ln_0 = mlp_block_bwd(dx1, 0, w1_0, w2_0, xh2_0, r2_0, xh1_0, r1_0, act0, hid0, x0nb)
    d_even_out = _mm(y_e, du1b, ta=True, out_dtypes=(BF16,), name="even_out_dw")
    dy_e = _mm(du1b, w_even_out, nt=True, name="even_out_dx", deps=tok)
    doa, dob = _stack_rows(dy_e[:, :A_Q_W], nq), _heads(dy_e[:, A_Q_W:], B_HEADS)
    dqa, dka, dva, dsink = _band_bwd(qa, ka, va, doa, oa, lse_a, slope_a, sink_a, name="swa_bwd", **a_cfg)
    pieces = [_unstack_rows(dqa, nq), _unheads(dka[:, 0, BLK:]), _unheads(dva[:, 0, BLK:])]
    for gi, (_, dil) in enumerate(B_PATTERNS):
        qs, ks, vs, slope = b_in[gi]
        dq, dk, dv = _band_bwd(qs, ks, vs, _to_strided(dob, dil)[None], _to_strided(ob, dil)[None],
                               _to_strided(lse_b, dil)[None], slope, None, name=f"dil{gi}_bwd", **b_cfg[gi])
        pieces += [_unheads(_from_strided(t, dil)) for t in (dq[0], dk[0, :, BLK:], dv[0, :, BLK:])]
    dh_e = bf(jnp.concatenate(pieces + [jnp.zeros((s, even_n - EVEN_IN), F32)], axis=1))
    d_even_in = _column_blocks(_mm(x0b, dh_e, ta=True, out_dtypes=(BF16,), name="even_in_dw"), even_c, even_cp)
    comm.grads("even", [("even_w_in", d_even_in), ("even_w_out", _by_column_block(d_even_out))])
    tok = comm.sync("even_in_dw", [d_even_in])
    grad_x = _mm(dh_e, w_even_in, nt=True, epilogue=_add_alpha, extra=du1, name="even_in_dx", deps=tok)
    tok = comm.sync("even_in_dx", [grad_x])

    ln = [jnp.concatenate([a, b], axis=0) for a, b in zip(ln_0, ln_1)]
    small = {"ln": ln, "sinks": dsink[:, :, 0].reshape(-1), "gq": dgq[0], "gkv": dgkv[0], "loss": loss_vec[0, :1]}
    return grad_x, small, tok


def kernel(x, even_w_in, even_sinks, even_w_out, odd_w_in, odd_q_norm_g, odd_kv_norm_g, odd_w_uq, odd_w_ukv, odd_w_out, ln1_g, ln1_b, mlp_w1, mlp_w2, ln2_g, ln2_b, loss_target, m_even_w_in, m_even_sinks, m_even_w_out, m_odd_w_in, m_odd_q_norm_g, m_odd_kv_norm_g, m_odd_w_uq, m_odd_w_ukv, m_odd_w_out, m_ln1_g, m_ln1_b, m_mlp_w1, m_mlp_w2, m_ln2_g, m_ln2_b, v_even_w_in, v_even_sinks, v_even_w_out, v_odd_w_in, v_odd_q_norm_g, v_odd_kv_norm_g, v_odd_w_uq, v_odd_w_ukv, v_odd_w_out, v_ln1_g, v_ln1_b, v_mlp_w1, v_mlp_w2, v_ln2_g, v_ln2_b):
    weights = dict(even_w_in=even_w_in, even_sinks=even_sinks, even_w_out=even_w_out, odd_w_in=odd_w_in,
                   odd_q_norm_g=odd_q_norm_g, odd_kv_norm_g=odd_kv_norm_g, odd_w_uq=odd_w_uq, odd_w_ukv=odd_w_ukv,
                   odd_w_out=odd_w_out, ln1_g=ln1_g, ln1_b=ln1_b, mlp_w1=mlp_w1, mlp_w2=mlp_w2, ln2_g=ln2_g, ln2_b=ln2_b)
    mom_m = dict(even_w_in=m_even_w_in, even_sinks=m_even_sinks, even_w_out=m_even_w_out, odd_w_in=m_odd_w_in,
                 odd_q_norm_g=m_odd_q_norm_g, odd_kv_norm_g=m_odd_kv_norm_g, odd_w_uq=m_odd_w_uq, odd_w_ukv=m_odd_w_ukv,
                 odd_w_out=m_odd_w_out, ln1_g=m_ln1_g, ln1_b=m_ln1_b, mlp_w1=m_mlp_w1, mlp_w2=m_mlp_w2, ln2_g=m_ln2_g, ln2_b=m_ln2_b)
    mom_v = dict(even_w_in=v_even_w_in, even_sinks=v_even_sinks, even_w_out=v_even_w_out, odd_w_in=v_odd_w_in,
                 odd_q_norm_g=v_odd_q_norm_g, odd_kv_norm_g=v_odd_kv_norm_g, odd_w_uq=v_odd_w_uq, odd_w_ukv=v_odd_w_ukv,
                 odd_w_out=v_odd_w_out, ln1_g=v_ln1_g, ln1_b=v_ln1_b, mlp_w1=v_mlp_w1, mlp_w2=v_mlp_w2, ln2_g=v_ln2_g, ln2_b=v_ln2_b)
    order = list(weights)
    n_q, n_kv = odd_q_norm_g.shape[1], odd_kv_norm_g.shape[1]

    def lane_padded(t):
        return jnp.pad(t, ((0, 0), (0, _lane_pad(t.shape[1]) - t.shape[1]))).astype(BF16)

    shards = {"even_w_in": lane_padded(even_w_in[0]), "even_w_out": even_w_out[0].astype(BF16),
              "mlp_w1_0": mlp_w1[0].astype(BF16), "mlp_w2_0": mlp_w2[0].astype(BF16),
              "odd_w_in": lane_padded(odd_w_in[0]), "odd_w_uq": odd_w_uq[0].astype(BF16),
              "odd_w_ukv": odd_w_ukv[0].astype(BF16), "odd_w_out": odd_w_out[0].astype(BF16),
              "mlp_w1_1": mlp_w1[1].astype(BF16), "mlp_w2_1": mlp_w2[1].astype(BF16)}
    gains = jnp.concatenate([odd_q_norm_g, odd_kv_norm_g, jnp.zeros((1, LANES - n_q - n_kv), F32)], axis=1)
    comm = _Exchanges(shards, gains, n_q, n_kv)
    dev = comm.dev

    grad_x, small, last_started = _local_step(x[0], loss_target[0], comm, even_sinks[0], ln1_g, ln1_b, ln2_g, ln2_b)

    grads, delta, new_m, new_v = {}, {}, {}, {}

    def update(n):
        g_list = [comm.reduced[f"{n}_0"], comm.reduced[f"{n}_1"]] if n.startswith("mlp") else [comm.reduced[n]]
        grads[n], delta[n], new_m[n], new_v[n] = _adamw(weights[n], g_list, mom_m[n], mom_v[n], name=f"adamw_{n}",
                                                        deps=last_started)

    early = ("mlp_w1", "mlp_w2", "odd_w_in", "odd_w_uq", "odd_w_ukv", "odd_w_out")
    for n in early:
        update(n)
    comm.sync("finish", [new_v[n] for n in early])
    update("even_w_in")
    update("even_w_out")

    small_parts = [t.reshape(-1) for t in small["ln"]] + [small["sinks"], small["gq"], small["gkv"], small["loss"]]
    small_sizes = [p.shape[0] for p in small_parts]
    n_small = sum(small_sizes)
    small_rows = -(-n_small // (8 * LANES)) * 8
    small_flat = jnp.concatenate(small_parts + [jnp.zeros((small_rows * LANES - n_small,), F32)]).reshape(small_rows, LANES)
    (small_all,) = _all_gather([small_flat], name="comm_small_gather", deps=[comm.reduced["even_w_in"]])
    totals = _sum_devices(small_all, name="small_sum").reshape(-1)
    tot, off = [], 0
    for size in small_sizes:
        tot.append(totals[off:off + size])
        off += size
    for i, n in enumerate(("ln1_g", "ln1_b", "ln2_g", "ln2_b")):
        grads[n] = tot[i].reshape(weights[n].shape)
    grads["even_sinks"] = tot[4].reshape(even_sinks.shape)
    grads["odd_q_norm_g"] = lax.dynamic_slice(tot[5], (dev * n_q,), (n_q,)).reshape(odd_q_norm_g.shape)
    grads["odd_kv_norm_g"] = lax.dynamic_slice(tot[6], (dev * n_kv,), (n_kv,)).reshape(odd_kv_norm_g.shape)
    loss = tot[7][0]

    small_names = [n for n in order if n not in early + ("even_w_in", "even_w_out")]
    n_sm = sum(weights[n].size for n in small_names)
    sm_rows = -(-n_sm // (8 * LANES)) * 8

    def pack_small(group):
        flat = [group[n].reshape(-1) for n in small_names]
        return jnp.concatenate(flat + [jnp.zeros((sm_rows * LANES - n_sm,), F32)]).reshape(1, sm_rows, LANES)

    res = _adamw(pack_small(weights), [pack_small(grads)[0]], pack_small(mom_m), pack_small(mom_v), name="adamw_small")
    off = 0
    for n in small_names:
        size = weights[n].size
        delta[n], new_m[n], new_v[n] = (t.reshape(-1)[off:off + size].reshape(weights[n].shape) for t in res[1:])
        off += size

    return (loss, grad_x[None], *[grads[n] for n in order], *[delta[n] for n in order],
            *[new_m[n] for n in order], *[new_v[n] for n in order])
```

```python
import math

import jax
import jax.numpy as jnp
import numpy as np
from jax import lax
from jax.experimental import pallas as pl
from jax.experimental.pallas import tpu as pltpu

F32 = jnp.float32
BF16 = jnp.bfloat16
MXU_DTYPE = BF16

HEAD_DIM = 64
A_Q_HEADS, A_KV_HEADS, A_WINDOW = 16, 2, 128
A_GROUP = A_Q_HEADS // A_KV_HEADS
B_HEADS = 8
B_PATTERNS = ((128, 1), (512, 4), (2048, 16))
C_HEADS = 16
D_HEADS, D_Q_RANK, D_KV_RANK, D_NOPE, D_ROPE, D_V = 16, 512, 256, 64, 32, 64
ROPE_BASE = 10000.0
LN_EPS, RMS_EPS = 1e-5, 1e-6
DEPTH = 2
ALPHA = (2 * DEPTH) ** 0.25
A_Q_W, A_KV_W, B_W = A_Q_HEADS * HEAD_DIM, A_KV_HEADS * HEAD_DIM, B_HEADS * HEAD_DIM
EVEN_IN = A_Q_W + 2 * A_KV_W + 3 * B_W * len(B_PATTERNS)
C_W = C_HEADS * HEAD_DIM
ODD_IN = 3 * C_W + D_Q_RANK + D_KV_RANK + D_ROPE
ADAM_LR, ADAM_B1, ADAM_B2, ADAM_EPS, ADAM_WD, ADAM_STEP = 0.001, 0.9, 0.999, 1e-08, 0.01, 10

N_DEV = 8
LANES = 128
BLK = 128
CAUSAL_TILE = 512
CUM_CHUNK = 256
NEG = -1e30
VMEM_LIMIT = 48 * 1024 * 1024

NN = ((1,), (0,))
NT = ((1,), (1,))
TN = ((0,), (0,))
MESH = pl.DeviceIdType.MESH
ANY = pl.BlockSpec(memory_space=pl.ANY)
HBM_SPEC = pl.BlockSpec(memory_space=pltpu.HBM)
SEM_SPEC = pl.BlockSpec(memory_space=pltpu.SEMAPHORE)
DATAFLOW_EFFECT = pltpu.SideEffectType.DATAFLOW_SIDE_EFFECTING


def _dot(a, b, dims):
    return lax.dot_general(a, b, (dims, ((), ())), preferred_element_type=F32)


def _bdot(a, b, dims):
    return jnp.stack([_dot(a[n], b[n], dims) for n in range(a.shape[0])])


def _params(*sem):
    return pltpu.CompilerParams(dimension_semantics=tuple(sem), vmem_limit_bytes=VMEM_LIMIT)


def _pick(n, cap, mult=LANES):
    if n <= cap:
        return n
    for t in range(cap - cap % mult, 0, -mult):
        if n % t == 0:
            return t
    raise ValueError(f"no tile for {n}")


def _lane_pad(c):
    return -(-c // LANES) * LANES


def _mm(a, b, *, name, nt=False, ta=False, b_blocks=False, out_blocks=False, out_dtypes=(F32,), epilogue=None, extra=None, deps=()):
    m, k = a.shape[::-1] if ta else a.shape
    if b_blocks:
        nb, kin, c = b.shape
        n = kin if nt else nb * c
        k_full = nb * c if nt else kin
    else:
        n, k_full = (b.shape if nt else b.shape[::-1])
    assert k == k_full, (a.shape, b.shape, nt, b_blocks)
    tm = _pick(m, 1024, 8)
    if b_blocks and not nt:
        tn, tk = c, _pick(k, 3072)
    elif b_blocks:
        per_step = max(g for g in (1, 2, 4, 8) if g * c <= 2048)
        tn, tk = _pick(n, 1024), per_step * c
    elif out_blocks:
        tn, tk = n // N_DEV, _pick(k, 3072)
    else:
        tn, tk = _pick(n, 512), _pick(k, 3072)
        if k > tk:
            tn, tk = _pick(n, 1024), _pick(k, 2048)
    nk = k // tk
    n_out = len(out_dtypes)

    def body(*refs):
        a_ref, b_ref = refs[0], refs[1]
        e_ref = refs[2] if extra is not None else None
        first_out = 2 + (extra is not None) + len(deps)
        out_refs = refs[first_out:first_out + n_out]

        def finish(acc):
            e = None if e_ref is None else e_ref[...]
            outs = (acc,) if epilogue is None else epilogue(acc, e)
            for r, o in zip(out_refs, outs):
                r[...] = o.astype(r.dtype).reshape(r.shape)

        if b_blocks and nt:
            part = _dot(a_ref[:, :c], b_ref[0], NT)
            for blk in range(1, per_step):
                part += _dot(a_ref[:, blk * c:(blk + 1) * c], b_ref[blk], NT)
        elif ta:
            part = _dot(a_ref[...], b_ref[...], TN)
        else:
            part = _dot(a_ref[...], b_ref[0] if b_blocks else b_ref[...], NT if nt else NN)
        if nk == 1:
            finish(part)
        else:
            acc_ref = refs[first_out + n_out]
            kk = pl.program_id(2)

            @pl.when(kk == 0)
            def _():
                acc_ref[...] = part

            @pl.when(kk > 0)
            def _():
                acc_ref[...] += part

            @pl.when(kk == nk - 1)
            def _():
                finish(acc_ref[...])

    if b_blocks and not nt:
        b_spec = pl.BlockSpec((1, tk, tn), lambda i, j, kk: (j, kk, 0))
    elif b_blocks:
        b_spec = pl.BlockSpec((per_step, tn, c), lambda i, j, kk: (kk, j, 0))
    elif nt:
        b_spec = pl.BlockSpec((tn, tk), lambda i, j, kk: (j, kk))
    else:
        b_spec = pl.BlockSpec((tk, tn), lambda i, j, kk: (kk, j))
    a_spec = pl.BlockSpec((tk, tm), lambda i, j, kk: (kk, i)) if ta else pl.BlockSpec((tm, tk), lambda i, j, kk: (i, kk))
    in_specs = [a_spec, b_spec]
    ins = [a.astype(MXU_DTYPE), b.astype(MXU_DTYPE)]
    if extra is not None:
        in_specs.append(pl.BlockSpec((tm, tn), lambda i, j, kk: (i, j)))
        ins.append(extra)
    in_specs += [ANY] * len(deps)
    ins += list(deps)
    if out_blocks:
        out_shape = tuple(jax.ShapeDtypeStruct((N_DEV, m, tn), d) for d in out_dtypes)
        out_specs = tuple(pl.BlockSpec((1, tm, tn), lambda i, j, kk: (j, i, 0)) for _ in out_dtypes)
    else:
        out_shape = tuple(jax.ShapeDtypeStruct((m, n), d) for d in out_dtypes)
        out_specs = tuple(pl.BlockSpec((tm, tn), lambda i, j, kk: (i, j)) for _ in out_dtypes)
    outs = pl.pallas_call(
        body,
        out_shape=out_shape,
        grid=(m // tm, n // tn, nk),
        in_specs=in_specs,
        out_specs=out_specs,
        scratch_shapes=[pltpu.VMEM((tm, tn), F32)] if nk > 1 else [],
        compiler_params=_params("parallel", "parallel", "arbitrary"),
        name=name,
    )(*ins)
    return outs[0] if n_out == 1 else outs


def _relu_sq(acc, _):
    act = jnp.maximum(acc, 0.0)
    return act, act * act


def _relu_sq_grad(acc, act):
    return (acc * (2.0 * act.astype(F32)),)


def _add_alpha(acc, du):
    return (acc + ALPHA * du,)


def _ln_fwd(x, mixed, g, b, *, name, deps=()):
    s, d = x.shape
    tr = _pick(s, 256, 8)

    def body(x_ref, m_ref, g_ref, b_ref, *rest):
        y_ref, yb_ref, xh_ref, r_ref = rest[len(deps):]
        u = ALPHA * x_ref[...] + m_ref[...]
        mu = jnp.mean(u, axis=-1, keepdims=True)
        xc = u - mu
        var = jnp.mean(xc * xc, axis=-1, keepdims=True)
        r = lax.rsqrt(var + LN_EPS)
        xh = xc * r
        y = xh * g_ref[...] + b_ref[...]
        y_ref[...] = y
        yb_ref[...] = y.astype(MXU_DTYPE)
        xh_ref[...] = xh
        r_ref[...] = r

    row = pl.BlockSpec((tr, d), lambda i: (i, 0))
    vec = pl.BlockSpec((1, d), lambda i: (0, 0))
    return pl.pallas_call(
        body,
        out_shape=(jax.ShapeDtypeStruct((s, d), F32), jax.ShapeDtypeStruct((s, d), MXU_DTYPE),
                   jax.ShapeDtypeStruct((s, d), F32), jax.ShapeDtypeStruct((s, 1), F32)),
        grid=(s // tr,),
        in_specs=[row, row, vec, vec] + [ANY] * len(deps),
        out_specs=(row, row, row, pl.BlockSpec((tr, 1), lambda i: (i, 0))),
        compiler_params=_params("parallel"),
        name=name,
    )(x, mixed, g.reshape(1, d), b.reshape(1, d), *deps)


def _ln_bwd(dy, xh, r, g, *, name):
    s, d = dy.shape
    tr = _pick(s, 256, 8)

    def body(dy_ref, xh_ref, r_ref, g_ref, du_ref, dub_ref, dg_ref, db_ref):
        dyv, xhv = dy_ref[...], xh_ref[...]
        dxh = dyv * g_ref[...]
        c1 = jnp.mean(dxh, axis=-1, keepdims=True)
        c2 = jnp.mean(dxh * xhv, axis=-1, keepdims=True)
        du = r_ref[...] * (dxh - c1 - xhv * c2)
        du_ref[...] = du
        dub_ref[...] = du.astype(MXU_DTYPE)

        @pl.when(pl.program_id(0) == 0)
        def _():
            dg_ref[...] = jnp.zeros_like(dg_ref)
            db_ref[...] = jnp.zeros_like(db_ref)

        dg_ref[...] += jnp.sum(dyv * xhv, axis=0, keepdims=True)
        db_ref[...] += jnp.sum(dyv, axis=0, keepdims=True)

    row = pl.BlockSpec((tr, d), lambda i: (i, 0))
    vec = pl.BlockSpec((1, d), lambda i: (0, 0))
    return pl.pallas_call(
        body,
        out_shape=(jax.ShapeDtypeStruct((s, d), F32), jax.ShapeDtypeStruct((s, d), MXU_DTYPE),
                   jax.ShapeDtypeStruct((1, d), F32), jax.ShapeDtypeStruct((1, d), F32)),
        grid=(s // tr,),
        in_specs=[row, row, pl.BlockSpec((tr, 1), lambda i: (i, 0)), vec],
        out_specs=(row, row, vec, vec),
        compiler_params=_params("arbitrary"),
        name=name,
    )(dy, xh, r, g.reshape(1, d))


def _rms_fwd(x, g, *, name):
    s, d = x.shape
    tr = _pick(s, 512, 8)

    def body(x_ref, g_ref, y_ref, r_ref):
        xv = x_ref[...]
        r = lax.rsqrt(jnp.mean(xv * xv, axis=-1, keepdims=True) + RMS_EPS)
        y_ref[...] = (xv * r * g_ref[...]).astype(y_ref.dtype)
        r_ref[...] = r

    return pl.pallas_call(
        body,
        out_shape=(jax.ShapeDtypeStruct((s, d), MXU_DTYPE), jax.ShapeDtypeStruct((s, 1), F32)),
        grid=(s // tr,),
        in_specs=[pl.BlockSpec((tr, d), lambda i: (i, 0)), pl.BlockSpec((1, d), lambda i: (0, 0))],
        out_specs=(pl.BlockSpec((tr, d), lambda i: (i, 0)), pl.BlockSpec((tr, 1), lambda i: (i, 0))),
        compiler_params=_params("parallel"),
        name=name,
    )(x, g.reshape(1, d))


def _rms_bwd(dy, x, r, g, *, name):
    s, d = x.shape
    tr = _pick(s, 512, 8)

    def body(dy_ref, x_ref, r_ref, g_ref, dx_ref, dg_ref):
        dyv, rv = dy_ref[...], r_ref[...]
        xn = x_ref[...] * rv
        dxn = dyv * g_ref[...]
        dx_ref[...] = rv * (dxn - xn * jnp.mean(dxn * xn, axis=-1, keepdims=True))

        @pl.when(pl.program_id(0) == 0)
        def _():
            dg_ref[...] = jnp.zeros_like(dg_ref)

        dg_ref[...] += jnp.sum(dyv * xn, axis=0, keepdims=True)

    row = pl.BlockSpec((tr, d), lambda i: (i, 0))
    vec = pl.BlockSpec((1, d), lambda i: (0, 0))
    return pl.pallas_call(
        body,
        out_shape=(jax.ShapeDtypeStruct((s, d), F32), jax.ShapeDtypeStruct((1, d), F32)),
        grid=(s // tr,),
        in_specs=[row, row, pl.BlockSpec((tr, 1), lambda i: (i, 0)), vec],
        out_specs=(row, vec),
        compiler_params=_params("arbitrary"),
        name=name,
    )(dy, x, r, g.reshape(1, d))


def _rope_tables(s, inverse):
    inv_freq = ROPE_BASE ** (-jnp.arange(0, D_ROPE, 2, dtype=F32) / D_ROPE)
    ang = jnp.arange(s, dtype=F32)[:, None] * inv_freq[None, :]
    cos, sin = jnp.cos(ang), jnp.sin(ang)
    if inverse:
        sin = -sin
    half = D_ROPE // 2
    one, zero = jnp.ones((s, D_NOPE), F32), jnp.zeros((s, D_NOPE), F32)
    pad1, pad0 = jnp.ones((s, LANES - D_NOPE - D_ROPE), F32), jnp.zeros((s, LANES - D_NOPE - D_ROPE), F32)
    zh = jnp.zeros((s, half), F32)
    c = jnp.concatenate([one, cos, cos, pad1], axis=1)
    s_lo = jnp.concatenate([zero, -sin, zh, pad0], axis=1)
    s_hi = jnp.concatenate([zero, zh, sin, pad0], axis=1)
    return c, s_lo, s_hi


def _rope(x, tables, *, out_dtype, head_sum=False, name):
    h, s, w = x.shape
    ts = _pick(s, 2048, 8)

    def body(x_ref, c_ref, lo_ref, hi_ref, y_ref, *sum_ref):
        y = _rotate(x_ref[0], c_ref[...], lo_ref[...], hi_ref[...])
        y_ref[0] = y.astype(y_ref.dtype)
        if head_sum:
            @pl.when(pl.program_id(1) == 0)
            def _():
                sum_ref[0][...] = jnp.zeros_like(sum_ref[0])

            sum_ref[0][...] += y

    tab = pl.BlockSpec((ts, w), lambda i, hh: (i, 0))
    blk = pl.BlockSpec((1, ts, w), lambda i, hh: (hh, i, 0))
    out_shape = [jax.ShapeDtypeStruct((h, s, w), out_dtype)]
    out_specs = [blk]
    if head_sum:
        out_shape.append(jax.ShapeDtypeStruct((s, w), F32))
        out_specs.append(tab)
    res = pl.pallas_call(
        body,
        out_shape=tuple(out_shape),
        grid=(s // ts, h),
        in_specs=[blk, tab, tab, tab],
        out_specs=tuple(out_specs),
        compiler_params=_params("parallel", "arbitrary"),
        name=name,
    )(x, *tables)
    return res if head_sum else res[0]


def _band_scores(q, kw, slope, i, *, scale, n_back, bps):
    b, r, _ = q.shape
    sc = _bdot(q, kw, NT) * scale
    shape = (b, r, 2 * BLK)
    row = lax.broadcasted_iota(jnp.int32, shape, 1) & (BLK - 1)
    col = lax.broadcasted_iota(jnp.int32, shape, 2)
    rel = BLK + row - col
    first_col = jnp.where(i % bps == 0, BLK, 0)
    valid = (rel >= 0) & (rel <= n_back) & (col >= first_col)
    return jnp.where(valid, sc - slope * rel.astype(F32), NEG)


def _band_fwd(q, k, v, slope, sink, *, scale, n_back, bps, name):
    g, b, rows, dh = q.shape
    r = slope.shape[2]
    nq = rows // r
    skv = k.shape[2]
    use_sink = sink is not None

    def body(*refs):
        q_ref, k_ref, v_ref, slope_ref = refs[:4]
        sink_ref = refs[4] if use_sink else None
        o_ref, lse_ref = refs[4 + use_sink:]
        i = pl.program_id(1)
        off = pl.multiple_of(i * BLK, BLK)
        kw = k_ref[0, :, pl.ds(off, 2 * BLK), :]
        vw = v_ref[0, :, pl.ds(off, 2 * BLK), :]
        sc = _band_scores(q_ref[0], kw, slope_ref[0], i, scale=scale, n_back=n_back, bps=bps)
        m = jnp.max(sc, axis=-1, keepdims=True)
        if use_sink:
            m = jnp.maximum(m, sink_ref[0])
        p = jnp.exp(sc - m)
        l = jnp.sum(p, axis=-1, keepdims=True)
        if use_sink:
            l = l + jnp.exp(sink_ref[0] - m)
        o_ref[0] = _bdot(p.astype(MXU_DTYPE), vw, NN) / l
        lse_ref[0] = m + jnp.log(l)

    qspec = pl.BlockSpec((1, b, r, dh), lambda gg, i: (gg, 0, i, 0))
    kspec = pl.BlockSpec((1, b, skv, dh), lambda gg, i: (gg, 0, 0, 0))
    rspec = pl.BlockSpec((1, b, r, 1), lambda gg, i: (gg, 0, 0, 0))
    ins = [q, k, v, slope] + ([sink] if use_sink else [])
    return pl.pallas_call(
        body,
        out_shape=(jax.ShapeDtypeStruct((g, b, rows, dh), F32), jax.ShapeDtypeStruct((g, b, rows, 1), F32)),
        grid=(g, nq),
        in_specs=[qspec, kspec, kspec, rspec] + ([rspec] if use_sink else []),
        out_specs=(qspec, pl.BlockSpec((1, b, r, 1), lambda gg, i: (gg, 0, i, 0))),
        compiler_params=_params("parallel", "arbitrary"),
        name=name,
    )(*ins)


def _band_bwd(q, k, v, do, o, lse, slope, sink, *, scale, n_back, bps, name):
    g, b, rows, dh = q.shape
    r = slope.shape[2]
    nq = rows // r
    skv = k.shape[2]
    use_sink = sink is not None
    stacked = r // BLK

    def body(*refs):
        q_ref, k_ref, v_ref, do_ref, o_ref, lse_ref, slope_ref = refs[:7]
        sink_ref = refs[7] if use_sink else None
        dq_ref, dk_ref, dv_ref = refs[7 + use_sink:10 + use_sink]
        i = pl.program_id(1)

        @pl.when(i == 0)
        def _():
            dk_ref[...] = jnp.zeros_like(dk_ref)
            dv_ref[...] = jnp.zeros_like(dv_ref)

        off = pl.multiple_of(i * BLK, BLK)
        qb = q_ref[0]
        kw = k_ref[0, :, pl.ds(off, 2 * BLK), :]
        vw = v_ref[0, :, pl.ds(off, 2 * BLK), :]
        dof = do_ref[0]
        dob = dof.astype(MXU_DTYPE)
        lse_b = lse_ref[0]
        delta = jnp.sum(dof * o_ref[0], axis=-1, keepdims=True)
        sc = _band_scores(qb, kw, slope_ref[0], i, scale=scale, n_back=n_back, bps=bps)
        p = jnp.exp(sc - lse_b)
        ds = (p * (_bdot(dob, vw, NT) - delta) * scale).astype(MXU_DTYPE)
        dq_ref[0] = _bdot(ds, kw, NN)
        dk_ref[0, :, pl.ds(off, 2 * BLK), :] += _bdot(ds, qb, TN)
        dv_ref[0, :, pl.ds(off, 2 * BLK), :] += _bdot(p.astype(MXU_DTYPE), dob, TN)

        if use_sink:
            dsink_ref = refs[10 + use_sink]

            @pl.when(i == 0)
            def _():
                dsink_ref[...] = jnp.zeros_like(dsink_ref)

            contrib = -jnp.exp(sink_ref[0] - lse_b) * delta
            for n in range(stacked):
                part = jnp.sum(contrib[0, n * BLK:(n + 1) * BLK, :], axis=0, keepdims=True)
                dsink_ref[0, n:n + 1, :] += jnp.broadcast_to(part, (1, LANES))

    def qspec(w):
        return pl.BlockSpec((1, b, r, w), lambda gg, i: (gg, 0, i, 0))

    kspec = pl.BlockSpec((1, b, skv, dh), lambda gg, i: (gg, 0, 0, 0))
    rspec = pl.BlockSpec((1, b, r, 1), lambda gg, i: (gg, 0, 0, 0))
    ins = [q, k, v, do, o, lse, slope] + ([sink] if use_sink else [])
    in_specs = [qspec(dh), kspec, kspec, qspec(dh), qspec(dh), qspec(1), rspec] + ([rspec] if use_sink else [])
    out_shape = [jax.ShapeDtypeStruct((g, b, rows, dh), F32), jax.ShapeDtypeStruct((g, b, skv, dh), F32),
                 jax.ShapeDtypeStruct((g, b, skv, dh), F32)]
    out_specs = [qspec(dh), kspec, kspec]
    if use_sink:
        assert b == 1
        out_shape.append(jax.ShapeDtypeStruct((g, stacked, LANES), F32))
        out_specs.append(pl.BlockSpec((1, stacked, LANES), lambda gg, i: (gg, 0, 0)))
    return pl.pallas_call(
        body,
        out_shape=tuple(out_shape),
        grid=(g, nq),
        in_specs=in_specs,
        out_specs=tuple(out_specs),
        compiler_params=_params("parallel", "arbitrary"),
        name=name,
    )(*ins)


def _pair_masks():
    first = lax.broadcasted_iota(jnp.int32, (1, LANES), 1) < HEAD_DIM
    m0 = first.astype(MXU_DTYPE)
    return first, (m0, 1 - m0)


def _dil_window(ref, i):
    prev = pl.multiple_of(jnp.maximum(i - 1, 0) * BLK, BLK)
    cur = pl.multiple_of(i * BLK, BLK)
    return prev, cur, jnp.concatenate([ref[pl.ds(prev, BLK), :], ref[pl.ds(cur, BLK), :]], axis=0)


def _dil_mask(i, n_back, bps):
    row = lax.broadcasted_iota(jnp.int32, (BLK, 2 * BLK), 0)
    col = lax.broadcasted_iota(jnp.int32, (BLK, 2 * BLK), 1)
    rel = BLK + row - col
    first_col = jnp.where(i % bps == 0, BLK, 0)
    return (rel >= 0) & (rel <= n_back) & (col >= first_col), rel.astype(F32)


def _dil_fwd(slab, slopes, *, scale, n_back, bps, name):
    s, w = slab.shape[0], slab.shape[1] // 3

    def body(q_ref, k_ref, v_ref, o_ref, lse_ref):
        i = pl.program_id(0)
        first, masks = _pair_masks()
        _, _, kw = _dil_window(k_ref, i)
        _, _, vw = _dil_window(v_ref, i)
        valid, rel = _dil_mask(i, n_back, bps)
        for p in range(w // LANES):
            cols = slice(p * LANES, (p + 1) * LANES)
            qp, kp, vp = q_ref[:, cols], kw[:, cols], vw[:, cols]
            outs, lses = [], []
            for hh in range(2):
                sc = _dot(qp * masks[hh], kp, NT) * scale - float(slopes[2 * p + hh]) * rel
                sc = jnp.where(valid, sc, NEG)
                m = jnp.max(sc, axis=-1, keepdims=True)
                e = jnp.exp(sc - m)
                l = jnp.sum(e, axis=-1, keepdims=True)
                outs.append(_dot(e.astype(MXU_DTYPE), vp, NN) / l)
                lses.append(m + jnp.log(l))
            o_ref[:, cols] = jnp.where(first, outs[0], outs[1])
            lse_ref[:, cols] = jnp.where(first, lses[0], lses[1])

    blk = pl.BlockSpec((BLK, w), lambda i: (i, 0))
    return pl.pallas_call(
        body,
        out_shape=(jax.ShapeDtypeStruct((s, w), F32), jax.ShapeDtypeStruct((s, w), F32)),
        grid=(s // BLK,),
        in_specs=[blk, pl.BlockSpec((s, w), lambda i: (0, 1)), pl.BlockSpec((s, w), lambda i: (0, 2))],
        out_specs=(blk, blk),
        compiler_params=_params("arbitrary"),
        name=name,
    )(slab, slab, slab)


def _dil_bwd(slab, pack, slopes, *, scale, n_back, bps, name):
    s, w = slab.shape[0], slab.shape[1] // 3

    def body(q_ref, k_ref, v_ref, do_ref, o_ref, lse_ref, dq_ref, dk_ref, dv_ref):
        i = pl.program_id(0)

        @pl.when(i == 0)
        def _():
            dk_ref[...] = jnp.zeros_like(dk_ref)
            dv_ref[...] = jnp.zeros_like(dv_ref)

        first, masks = _pair_masks()
        prev, cur, kw = _dil_window(k_ref, i)
        _, _, vw = _dil_window(v_ref, i)
        valid, rel = _dil_mask(i, n_back, bps)
        for p in range(w // LANES):
            cols = slice(p * LANES, (p + 1) * LANES)
            qp, kp, vp = q_ref[:, cols], kw[:, cols], vw[:, cols]
            dof, lse_p = do_ref[:, cols], lse_ref[:, cols]
            prod = dof * o_ref[:, cols]
            do_b = dof.astype(MXU_DTYPE)
            dqs, dk_add, dv_add = [], None, None
            for hh in range(2):
                qh, doh = qp * masks[hh], do_b * masks[hh]
                delta = jnp.sum(jnp.where(first, prod, 0.0) if hh == 0 else jnp.where(first, 0.0, prod), axis=-1, keepdims=True)
                sc = _dot(qh, kp, NT) * scale - float(slopes[2 * p + hh]) * rel
                e = jnp.exp(jnp.where(valid, sc, NEG) - lse_p[:, hh * HEAD_DIM:hh * HEAD_DIM + 1])
                ds = (e * (_dot(doh, vp, NT) - delta) * scale).astype(MXU_DTYPE)
                dqs.append(_dot(ds, kp, NN))
                dk_h, dv_h = _dot(ds, qh, TN), _dot(e.astype(MXU_DTYPE), doh, TN)
                dk_add = dk_h if dk_add is None else dk_add + dk_h
                dv_add = dv_h if dv_add is None else dv_add + dv_h
            dq_ref[:, cols] = jnp.where(first, dqs[0], dqs[1])
            dk_ref[pl.ds(prev, BLK), cols] += dk_add[:BLK]
            dk_ref[pl.ds(cur, BLK), cols] += dk_add[BLK:]
            dv_ref[pl.ds(prev, BLK), cols] += dv_add[:BLK]
            dv_ref[pl.ds(cur, BLK), cols] += dv_add[BLK:]

    def blk(c):
        return pl.BlockSpec((BLK, w), lambda i: (i, c))

    def whole(c):
        return pl.BlockSpec((s, w), lambda i: (0, c))

    shp = jax.ShapeDtypeStruct((s, w), F32)
    return pl.pallas_call(
        body,
        out_shape=(shp, shp, shp),
        grid=(s // BLK,),
        in_specs=[blk(0), whole(1), whole(2), blk(0), blk(1), blk(2)],
        out_specs=(blk(0), whole(0), whole(0)),
        compiler_params=_params("arbitrary"),
        name=name,
    )(slab, slab, slab, pack, pack, pack)


def _merge(outs, lses, *, name):
    s, w = outs[0].shape
    tr = _pick(s, 512, 8)

    def body(o0, o1, o2, l0, l1, l2, ob_ref, lt_ref):
        a, b, c = l0[...], l1[...], l2[...]
        m = jnp.maximum(jnp.maximum(a, b), c)
        ea, eb, ec = jnp.exp(a - m), jnp.exp(b - m), jnp.exp(c - m)
        den = ea + eb + ec
        ob_ref[...] = (ea / den) * o0[...] + (eb / den) * o1[...] + (ec / den) * o2[...]
        lt_ref[...] = m + jnp.log(den)

    spec = pl.BlockSpec((tr, w), lambda i: (i, 0))
    return pl.pallas_call(
        body,
        out_shape=(jax.ShapeDtypeStruct((s, w), F32), jax.ShapeDtypeStruct((s, w), F32)),
        grid=(s // tr,),
        in_specs=[spec] * 6,
        out_specs=(spec, spec),
        compiler_params=_params("parallel"),
        name=name,
    )(*outs, *lses)


def _tile_iotas(t):
    return lax.broadcasted_iota(jnp.int32, (t, t), 0), lax.broadcasted_iota(jnp.int32, (t, t), 1)


def _rotate(x, c, s_lo, s_hi):
    half = D_ROPE // 2
    return x * c + pltpu.roll(x, LANES - half, 1) * s_lo + pltpu.roll(x, half, 1) * s_hi


def _mla_keys(kv_h, kr_t, first):
    return jnp.where(first, kv_h, kr_t)


def _mla_fwd(qd, kvd, krp, tables, *, scale, name):
    s = qd.shape[0]
    pairs = qd.shape[1] // (2 * LANES)
    t = min(CAUSAL_TILE, s)

    def body(q_ref, kv_ref, kr_ref, c_ref, lo_ref, hi_ref, o_ref, lse_ref):
        i = pl.program_id(1)
        first = lax.broadcasted_iota(jnp.int32, (1, LANES), 1) < HEAD_DIM
        tabs = (c_ref[...], lo_ref[...], hi_ref[...])
        q_heads = [_rotate(q_ref[:, hh * LANES:(hh + 1) * LANES], *tabs).astype(MXU_DTYPE) for hh in range(2)]

        def tile(j, carry, diagonal):
            off = pl.multiple_of(j * t, t)
            kr_t = kr_ref[pl.ds(off, t), :]
            out = []
            for hh in range(2):
                m, l, acc = carry[3 * hh:3 * hh + 3]
                kv_h = kv_ref[pl.ds(off, t), hh * LANES:(hh + 1) * LANES]
                sc = _dot(q_heads[hh], _mla_keys(kv_h, kr_t, first), NT) * scale
                if diagonal:
                    row, col = _tile_iotas(t)
                    sc = jnp.where(row >= col, sc, NEG)
                m_new = jnp.maximum(m, jnp.max(sc, axis=-1, keepdims=True))
                a = jnp.exp(m - m_new)
                p = jnp.exp(sc - m_new)
                out += [m_new, a * l + jnp.sum(p, axis=-1, keepdims=True), a * acc + _dot(p.astype(MXU_DTYPE), kv_h, NN)]
            return tuple(out)

        init = (jnp.full((t, 1), NEG, F32), jnp.zeros((t, 1), F32), jnp.zeros((t, LANES), F32)) * 2
        carry = lax.fori_loop(0, i, lambda j, c: tile(j, c, False), init)
        m0, l0, acc0, m1, l1, acc1 = tile(i, carry, True)
        o_ref[...] = jnp.where(first, pltpu.roll(acc0 / l0, HEAD_DIM, 1), acc1 / l1)
        lse_ref[0] = jnp.where(lax.broadcasted_iota(jnp.int32, (t, 2), 1) == 0, m0 + jnp.log(l0), m1 + jnp.log(l1))

    tab = pl.BlockSpec((t, LANES), lambda p, i: (i, 0))
    return pl.pallas_call(
        body,
        out_shape=(jax.ShapeDtypeStruct((s, pairs * LANES), F32), jax.ShapeDtypeStruct((pairs, s, 2), F32)),
        grid=(pairs, s // t),
        in_specs=[pl.BlockSpec((t, 2 * LANES), lambda p, i: (i, p)), pl.BlockSpec((s, 2 * LANES), lambda p, i: (0, p)),
                  pl.BlockSpec((s, LANES), lambda p, i: (0, 0)), tab, tab, tab],
        out_specs=(pl.BlockSpec((t, LANES), lambda p, i: (i, p)), pl.BlockSpec((1, t, 2), lambda p, i: (p, i, 0))),
        compiler_params=_params("parallel", "arbitrary"),
        name=name,
    )(qd, kvd, krp, *tables)


def _mla_bwd(qd, kvd, krp, tables, do, o, lse, *, do_block0, scale, name):
    s = qd.shape[0]
    pairs = qd.shape[1] // (2 * LANES)
    t = min(CAUSAL_TILE, s)

    def body(q_ref, kv_ref, kr_ref, c_ref, lo_ref, hi_ref, do_ref, o_ref, lse_ref, dq_ref, dkv_ref, dkr_ref):
        i = pl.program_id(1)

        @pl.when(i == 0)
        def _():
            dkv_ref[...] = jnp.zeros_like(dkv_ref)
            dkr_ref[...] = jnp.zeros_like(dkr_ref)

        first = lax.broadcasted_iota(jnp.int32, (1, LANES), 1) < HEAD_DIM
        tabs = (c_ref[...], lo_ref[...], hi_ref[...])
        q_heads = [_rotate(q_ref[:, hh * LANES:(hh + 1) * LANES], *tabs).astype(MXU_DTYPE) for hh in range(2)]
        dof = do_ref[...]
        prod = dof * o_ref[...]
        deltas = [jnp.sum(jnp.where(first, prod, 0.0), axis=-1, keepdims=True),
                  jnp.sum(jnp.where(first, 0.0, prod), axis=-1, keepdims=True)]
        do_heads = [jnp.where(first, 0.0, pltpu.roll(dof, HEAD_DIM, 1)).astype(MXU_DTYPE),
                    jnp.where(first, 0.0, dof).astype(MXU_DTYPE)]
        lses = [lse_ref[0][:, hh:hh + 1] for hh in range(2)]

        def tile(j, carry, diagonal):
            off = pl.multiple_of(j * t, t)
            kr_t = kr_ref[pl.ds(off, t), :]
            out, dkr_add = [], None
            for hh in range(2):
                kv_h = kv_ref[pl.ds(off, t), hh * LANES:(hh + 1) * LANES]
                k_h = _mla_keys(kv_h, kr_t, first)
                sc = _dot(q_heads[hh], k_h, NT) * scale
                if diagonal:
                    row, col = _tile_iotas(t)
                    sc = jnp.where(row >= col, sc, NEG)
                p = jnp.exp(sc - lses[hh])
                ds = (p * (_dot(do_heads[hh], kv_h, NT) - deltas[hh]) * scale).astype(MXU_DTYPE)
                dk_full = _dot(ds, q_heads[hh], TN)
                dv_full = _dot(p.astype(MXU_DTYPE), do_heads[hh], TN)
                dkv_ref[pl.ds(off, t), hh * LANES:(hh + 1) * LANES] += jnp.where(first, dk_full, dv_full)
                rot = jnp.where(first, 0.0, dk_full)
                dkr_add = rot if dkr_add is None else dkr_add + rot
                out.append(carry[hh] + _dot(ds, k_h, NN))
            dkr_ref[0, pl.ds(off, t), :] += dkr_add
            return tuple(out)

        zacc = jnp.zeros((t, LANES), F32)
        carry = lax.fori_loop(0, i, lambda j, c: tile(j, c, False), (zacc, zacc))
        dq_heads = tile(i, carry, True)
        for hh in range(2):
            dq_ref[:, hh * LANES:(hh + 1) * LANES] = _rotate(dq_heads[hh], tabs[0], -tabs[1], -tabs[2])

    tab = pl.BlockSpec((t, LANES), lambda p, i: (i, 0))
    qspec = pl.BlockSpec((t, 2 * LANES), lambda p, i: (i, p))
    kvspec = pl.BlockSpec((s, 2 * LANES), lambda p, i: (0, p))
    return pl.pallas_call(
        body,
        out_shape=(jax.ShapeDtypeStruct(qd.shape, F32), jax.ShapeDtypeStruct(kvd.shape, F32),
                   jax.ShapeDtypeStruct((pairs, s, LANES), F32)),
        grid=(pairs, s // t),
        in_specs=[qspec, kvspec, pl.BlockSpec((s, LANES), lambda p, i: (0, 0)), tab, tab, tab,
                  pl.BlockSpec((t, LANES), lambda p, i: (i, do_block0 + p)), pl.BlockSpec((t, LANES), lambda p, i: (i, p)),
                  pl.BlockSpec((1, t, 2), lambda p, i: (p, i, 0))],
        out_specs=(qspec, kvspec, pl.BlockSpec((1, s, LANES), lambda p, i: (p, 0, 0))),
        compiler_params=_params("parallel", "arbitrary"),
        name=name,
    )(qd, kvd, krp, *tables, do, o, lse)


def _split_cumsum(x, tri, terms=2):
    hi = x.astype(BF16)
    if terms == 1:
        return _dot(hi, tri, NN)
    lo = (x - hi.astype(F32)).astype(BF16)
    return _dot(hi, tri, NN) + _dot(lo, tri, NN)


def _chunked_cumsum(x, tri, run, *, reverse, sign, terms=2):
    c = tri.shape[0]
    n = x.shape[1] // c
    parts = [None] * n
    for idx in (reversed(range(n)) if reverse else range(n)):
        xc = x[:, idx * c:(idx + 1) * c]
        parts[idx] = sign * (run + _split_cumsum(xc, tri, terms))
        run = run + jnp.sum(xc, axis=-1, keepdims=True)
    return (parts[0] if n == 1 else jnp.concatenate(parts, axis=1)), run


def _sb_logs(z):
    e = jnp.exp(-jnp.abs(z))
    l1 = jnp.log(1.0 + e)
    return e, jnp.minimum(z, 0.0) - l1, -jnp.maximum(z, 0.0) - l1


def _sb_fwd(qkv, *, heads, scale, name):
    s = qkv.shape[0]
    pairs = heads * HEAD_DIM // LANES
    t = min(CAUSAL_TILE, s)
    cc = min(CUM_CHUNK, t)

    def body(q_ref, k_ref, v_ref, o_ref, t_ref):
        i = pl.program_id(1)
        first, masks = _pair_masks()
        q_heads = [q_ref[...] * m for m in masks]
        crow, ccol = _tile_iotas(cc)
        after = (crow > ccol).astype(BF16)

        def tile(j, carry, diagonal):
            off = pl.multiple_of(j * t, t)
            kb = k_ref[pl.ds(off, t), :]
            vb = v_ref[pl.ds(off, t), :]
            if diagonal:
                row, col = _tile_iotas(t)
                strict = row > col
            out = []
            for hh in range(2):
                run, acc = carry[2 * hh], carry[2 * hh + 1]
                z = _dot(q_heads[hh], kb, NT) * scale
                _, log_beta, log_keep = _sb_logs(z)
                if diagonal:
                    log_keep = jnp.where(strict, log_keep, 0.0)
                a, run = _chunked_cumsum(log_keep, after, run, reverse=True, sign=1.0)
                w = jnp.exp(log_beta + a)
                if diagonal:
                    w = jnp.where(strict, w, 0.0)
                out += [run, acc + _dot(w.astype(MXU_DTYPE), vb, NN)]
            return tuple(out)

        zero, zacc = jnp.zeros((t, 1), F32), jnp.zeros((t, LANES), F32)
        carry = tile(i, (zero, zacc, zero, zacc), True)
        run0, acc0, run1, acc1 = lax.fori_loop(0, i, lambda jj, c: tile(i - 1 - jj, c, False), carry)
        o_ref[...] = jnp.where(first, acc0, acc1)
        t_ref[0] = jnp.where(lax.broadcasted_iota(jnp.int32, (t, 2), 1) == 0, run0, run1)

    return pl.pallas_call(
        body,
        out_shape=(jax.ShapeDtypeStruct((s, heads * HEAD_DIM), F32), jax.ShapeDtypeStruct((pairs, s, 2), F32)),
        grid=(pairs, s // t),
        in_specs=[pl.BlockSpec((t, LANES), lambda p, i: (i, p)),
                  pl.BlockSpec((s, LANES), lambda p, i: (0, pairs + p)),
                  pl.BlockSpec((s, LANES), lambda p, i: (0, 2 * pairs + p))],
        out_specs=(pl.BlockSpec((t, LANES), lambda p, i: (i, p)), pl.BlockSpec((1, t, 2), lambda p, i: (p, i, 0))),
        compiler_params=_params("parallel", "arbitrary"),
        name=name,
    )(qkv, qkv, qkv)


def _sb_bwd(qkv, do, total, *, heads, scale, name):
    s = qkv.shape[0]
    pairs = heads * HEAD_DIM // LANES
    t = min(CAUSAL_TILE, s)
    cc = min(CUM_CHUNK, t)

    def body(q_ref, k_ref, v_ref, do_ref, t_ref, dq_ref, dk_ref, dv_ref):
        i = pl.program_id(1)

        @pl.when(i == 0)
        def _():
            dk_ref[...] = jnp.zeros_like(dk_ref)
            dv_ref[...] = jnp.zeros_like(dv_ref)

        first, masks = _pair_masks()
        q_heads = [q_ref[...] * m for m in masks]
        do_b = do_ref[...].astype(MXU_DTYPE)
        do_heads = [do_b * m for m in masks]
        tots = [t_ref[0][:, hh:hh + 1] for hh in range(2)]
        crow, ccol = _tile_iotas(cc)
        upto = (crow <= ccol).astype(BF16)
        before = (crow < ccol).astype(BF16)

        def tile(j, carry, diagonal):
            off = pl.multiple_of(j * t, t)
            kb = k_ref[pl.ds(off, t), :]
            vb = v_ref[pl.ds(off, t), :]
            if diagonal:
                row, col = _tile_iotas(t)
                strict = row > col
            out, dk_add, dv_add = [], None, None
            for hh in range(2):
                run_keep, run_g, dq_acc = carry[3 * hh:3 * hh + 3]
                z = _dot(q_heads[hh], kb, NT) * scale
                e, log_beta, log_keep = _sb_logs(z)
                if diagonal:
                    log_keep = jnp.where(strict, log_keep, 0.0)
                a, run_keep = _chunked_cumsum(log_keep, upto, run_keep - tots[hh], reverse=False, sign=-1.0)
                run_keep = run_keep + tots[hh]
                w = jnp.exp(log_beta + a)
                if diagonal:
                    w = jnp.where(strict, w, 0.0)
                g = w * _dot(do_heads[hh], vb, NT)
                prefix, run_g = _chunked_cumsum(g, before, run_g, reverse=False, sign=1.0, terms=1)
                pos = z >= 0.0
                dz = (g * jnp.where(pos, e, 1.0) - jnp.where(pos, 1.0, e) * prefix) * pl.reciprocal(1.0 + e, approx=True)
                if diagonal:
                    dz = jnp.where(strict, dz, 0.0)
                dz = (dz * scale).astype(MXU_DTYPE)
                dk_h = _dot(dz, q_heads[hh], TN)
                dv_h = _dot(w.astype(MXU_DTYPE), do_heads[hh], TN)
                dk_add = dk_h if dk_add is None else dk_add + dk_h
                dv_add = dv_h if dv_add is None else dv_add + dv_h
                out += [run_keep, run_g, dq_acc + _dot(dz, kb, NN)]
            dk_ref[pl.ds(off, t), :] += dk_add
            dv_ref[pl.ds(off, t), :] += dv_add
            return tuple(out)

        zero, zacc = jnp.zeros((t, 1), F32), jnp.zeros((t, LANES), F32)
        carry = lax.fori_loop(0, i, lambda j, c: tile(j, c, False), (zero, zero, zacc, zero, zero, zacc))
        res = tile(i, carry, True)
        dq_ref[...] = jnp.where(first, res[2], res[5])

    qspec = pl.BlockSpec((t, LANES), lambda p, i: (i, p))
    shp = jax.ShapeDtypeStruct((s, heads * HEAD_DIM), F32)
    return pl.pallas_call(
        body,
        out_shape=(shp, shp, shp),
        grid=(pairs, s // t),
        in_specs=[qspec, pl.BlockSpec((s, LANES), lambda p, i: (0, pairs + p)),
                  pl.BlockSpec((s, LANES), lambda p, i: (0, 2 * pairs + p)), qspec,
                  pl.BlockSpec((1, t, 2), lambda p, i: (p, i, 0))],
        out_specs=(qspec, pl.BlockSpec((s, LANES), lambda p, i: (0, p)), pl.BlockSpec((s, LANES), lambda p, i: (0, p))),
        compiler_params=_params("parallel", "arbitrary"),
        name=name,
    )(qkv, qkv, qkv, do, total)


def _loss_head(y, target, *, name):
    s, d = y.shape
    tr = _pick(s, 256, 8)

    def body(y_ref, t_ref, dy_ref, loss_ref):
        err = y_ref[...] - t_ref[...]
        dy_ref[...] = err * (1.0 / d)

        @pl.when(pl.program_id(0) == 0)
        def _():
            loss_ref[...] = jnp.zeros_like(loss_ref)

        per_tok = jnp.mean(err * err, axis=-1, keepdims=True)
        loss_ref[...] += 0.5 * jnp.sum(per_tok, axis=0, keepdims=True)

    row = pl.BlockSpec((tr, d), lambda i: (i, 0))
    return pl.pallas_call(
        body,
        out_shape=(jax.ShapeDtypeStruct((s, d), F32), jax.ShapeDtypeStruct((1, LANES), F32)),
        grid=(s // tr,),
        in_specs=[row, row],
        out_specs=(row, pl.BlockSpec((1, LANES), lambda i: (0, 0))),
        compiler_params=_params("arbitrary"),
        name=name,
    )(y, target)


def _adamw(w, grads, m, v, *, name, deps=()):
    nl, r, c = w.shape
    cp = grads[0].shape[1]
    tr = _pick(r, 256, 8)

    def body(*refs):
        w_ref, m_ref, v_ref = refs[:3]
        g_refs = refs[3:3 + nl]
        g_out, d_ref, m2_ref, v2_ref = refs[3 + nl + len(deps):]
        layer = pl.program_id(0)
        gv = g_refs[0][:, :c]
        for n in range(1, nl):
            gv = jnp.where(layer == n, g_refs[n][:, :c], gv)
        m2 = ADAM_B1 * m_ref[0] + (1.0 - ADAM_B1) * gv
        v2 = ADAM_B2 * v_ref[0] + (1.0 - ADAM_B2) * (gv * gv)
        m_hat = m2 / (1.0 - ADAM_B1 ** ADAM_STEP)
        v_hat = v2 / (1.0 - ADAM_B2 ** ADAM_STEP)
        g_out[0] = gv
        d_ref[0] = -ADAM_LR * (m_hat / (jnp.sqrt(v_hat) + ADAM_EPS) + ADAM_WD * w_ref[0])
        m2_ref[0] = m2
        v2_ref[0] = v2

    blk = pl.BlockSpec((1, tr, c), lambda l, i: (l, i, 0))
    gspec = pl.BlockSpec((tr, cp), lambda l, i: (i, 0))
    shp = jax.ShapeDtypeStruct((nl, r, c), F32)
    return pl.pallas_call(
        body,
        out_shape=(shp, shp, shp, shp),
        grid=(nl, r // tr),
        in_specs=[blk, blk, blk] + [gspec] * nl + [ANY] * len(deps),
        out_specs=(blk, blk, blk, blk),
        compiler_params=_params("parallel", "parallel"),
        name=name,
    )(w, m, v, *grads, *deps)


def _pair_sum(mine, recv, my_c, *, name):
    _, r, c = mine.shape
    tr = _pick(r, 512, 16)

    def body(c_ref, a_ref, b_ref, o_ref):
        o_ref[0] = (a_ref[0].astype(F32) + b_ref[0].astype(F32)).astype(o_ref.dtype)

    grid_spec = pltpu.PrefetchScalarGridSpec(
        num_scalar_prefetch=1,
        grid=(4, r // tr),
        in_specs=[pl.BlockSpec((1, tr, c), lambda kk, i, c_ref: (2 * kk + c_ref[0], i, 0)),
                  pl.BlockSpec((1, tr, c), lambda kk, i, c_ref: (kk, i, 0))],
        out_specs=pl.BlockSpec((1, tr, c), lambda kk, i, c_ref: (kk, i, 0)),
    )
    return pl.pallas_call(
        body,
        out_shape=jax.ShapeDtypeStruct((4, r, c), mine.dtype),
        grid_spec=grid_spec,
        compiler_params=_params("parallel", "parallel"),
        name=name,
    )(my_c.reshape(1).astype(jnp.int32), mine, recv)


def _final_sum(partial, recv, my_chip, *, name):
    _, r, c = partial.shape
    tr = _pick(r, 512, 16)

    def body(chip_ref, p_ref, r0, r1, r2, o_ref):
        o_ref[...] = ((p_ref[0].astype(F32) + r0[0].astype(F32)) + r1[0].astype(F32)) + r2[0].astype(F32)

    def slot(n):
        return pl.BlockSpec((1, tr, c), lambda i, chip_ref: (n, i, 0))

    grid_spec = pltpu.PrefetchScalarGridSpec(
        num_scalar_prefetch=1,
        grid=(r // tr,),
        in_specs=[pl.BlockSpec((1, tr, c), lambda i, chip_ref: (chip_ref[0], i, 0)), slot(0), slot(1), slot(2)],
        out_specs=pl.BlockSpec((tr, c), lambda i, chip_ref: (i, 0)),
    )
    return pl.pallas_call(
        body,
        out_shape=jax.ShapeDtypeStruct((r, c), F32),
        grid_spec=grid_spec,
        compiler_params=_params("parallel"),
        name=name,
    )(my_chip.reshape(1).astype(jnp.int32), partial, recv, recv, recv)


def _sum_devices(stack, *, name):
    n, r, c = stack.shape

    def body(s_ref, o_ref):
        acc = s_ref[0]
        for dev in range(1, n):
            acc = acc + s_ref[dev]
        o_ref[...] = acc

    return pl.pallas_call(
        body,
        out_shape=jax.ShapeDtypeStruct((r, c), F32),
        in_specs=[pl.BlockSpec(memory_space=pltpu.VMEM)],
        out_specs=pl.BlockSpec(memory_space=pltpu.VMEM),
        name=name,
    )(stack)


def _mesh_pos():
    return lax.axis_index("x"), lax.axis_index("y"), lax.axis_index("c")


def _all_gather(shards, *, name, deps=()):
    n = len(shards)

    def body(*refs):
        x_refs, out_refs = refs[:n], refs[n + len(deps):2 * n + len(deps)]
        send_sems, recv_sems, local_sems = refs[2 * n + len(deps):]
        x, y, cc = _mesh_pos()
        me, sibling = (x, y, cc), (x, y, 1 - cc)
        chips = [(1 - x, y), (x, 1 - y), (1 - x, 1 - y)]

        def rows(a, px, py, pc):
            return out_refs[a].at[4 * px + 2 * py + pc]

        def copy(a, kk, block, to, src=None):
            return pltpu.make_async_remote_copy(
                src_ref=rows(a, *block) if src is None else src, dst_ref=rows(a, *block),
                send_sem=send_sems.at[7 * a + kk], recv_sem=recv_sems.at[7 * a + kk],
                device_id=to, device_id_type=MESH)

        sends, own = [], []
        for a in range(n):
            own.append(pltpu.make_async_copy(x_refs[a], rows(a, *me), local_sems.at[a]))
            own[a].start()
            first = [copy(a, 0, me, sibling, src=x_refs[a])]
            first += [copy(a, 1 + j, me, (*chip, cc), src=x_refs[a]) for j, chip in enumerate(chips)]
            for cp in first:
                cp.start()
            sends += first
        for a in range(n):
            for j, chip in enumerate(chips):
                copy(a, 1 + j, (*chip, cc), me).wait_recv()
                passed = copy(a, 4 + j, (*chip, cc), sibling)
                passed.start()
                sends.append(passed)
        for a in range(n):
            copy(a, 0, sibling, me).wait_recv()
            for j, chip in enumerate(chips):
                copy(a, 4 + j, (*chip, 1 - cc), me).wait_recv()
        for cp in sends:
            cp.wait_send()
        for cp in own:
            cp.wait()

    return pl.pallas_call(
        body,
        out_shape=tuple(jax.ShapeDtypeStruct((N_DEV,) + t.shape, t.dtype) for t in shards),
        in_specs=[ANY] * (n + len(deps)),
        out_specs=tuple([ANY] * n),
        scratch_shapes=[pltpu.SemaphoreType.DMA((7 * n,)), pltpu.SemaphoreType.DMA((7 * n,)),
                        pltpu.SemaphoreType.DMA((n,))],
        name=name,
    )(*shards, *deps)


def _plan_own_blocks(n):
    def plan(refs, send_sems, recv_sems, outgoing):
        x, y, cc = _mesh_pos()
        peers = [(x, y, 1 - cc), (1 - x, y, cc), (x, 1 - y, cc), (1 - x, 1 - y, cc)]
        copies = []
        for a in range(n):
            land = refs[n + a]
            for kk, (px, py, pc) in enumerate(peers):
                block = (x, y, cc) if outgoing else (px, py, pc)
                rows = land.at[4 * block[0] + 2 * block[1] + block[2]]
                copies.append(pltpu.make_async_remote_copy(
                    src_ref=refs[a] if outgoing else rows, dst_ref=rows, send_sem=send_sems.at[4 * a + kk],
                    recv_sem=recv_sems.at[4 * a + kk], device_id=(px, py, pc), device_id_type=MESH))
        return copies

    plan.n_sems = 4 * n
    return plan


def _plan_pass_on(n):
    def plan(refs, send_sems, recv_sems, outgoing):
        x, y, cc = _mesh_pos()
        copies = []
        for a in range(n):
            for j, (px, py) in enumerate([(1 - x, y), (x, 1 - y), (1 - x, 1 - y)]):
                rows = refs[a].at[4 * px + 2 * py + (cc if outgoing else 1 - cc)]
                copies.append(pltpu.make_async_remote_copy(
                    src_ref=rows, dst_ref=rows, send_sem=send_sems.at[3 * a + j], recv_sem=recv_sems.at[3 * a + j],
                    device_id=(x, y, 1 - cc), device_id_type=MESH))
        return copies

    plan.n_sems = 3 * n
    return plan


def _plan_to_sibling(n):
    def plan(refs, send_sems, recv_sems, outgoing):
        x, y, cc = _mesh_pos()
        copies = []
        for a in range(n):
            for chip in range(4):
                dst = refs[n + a].at[chip]
                copies.append(pltpu.make_async_remote_copy(
                    src_ref=refs[a].at[2 * chip + (1 - cc)] if outgoing else dst, dst_ref=dst,
                    send_sem=send_sems.at[4 * a + chip], recv_sem=recv_sems.at[4 * a + chip],
                    device_id=(x, y, 1 - cc), device_id_type=MESH))
        return copies

    plan.n_sems = 4 * n
    return plan


def _plan_to_chips(n):
    def plan(refs, send_sems, recv_sems, outgoing):
        x, y, cc = _mesh_pos()
        copies = []
        for a in range(n):
            for j, (px, py) in enumerate([(1 - x, y), (x, 1 - y), (1 - x, 1 - y)]):
                dst = refs[n + a].at[j]
                copies.append(pltpu.make_async_remote_copy(
                    src_ref=refs[a].at[2 * px + py] if outgoing else dst, dst_ref=dst,
                    send_sem=send_sems.at[3 * a + j], recv_sem=recv_sems.at[3 * a + j],
                    device_id=(px, py, cc), device_id_type=MESH))
        return copies

    plan.n_sems = 3 * n
    return plan


def _in_hbm(t):
    return pltpu.with_memory_space_constraint(t, pltpu.HBM)


def _exchange_start(plan, bufs, after, *, name):
    nb, na = len(bufs), len(after)

    def body(*refs):
        outs = refs[nb + na:]
        for cp in plan(refs[:nb], outs[0], outs[1], True):
            cp.start()
        outs[2 + nb][...] = jnp.zeros_like(outs[2 + nb])

    res = pl.pallas_call(
        body,
        out_shape=(pltpu.SemaphoreType.DMA((plan.n_sems,)), pltpu.SemaphoreType.DMA((plan.n_sems,)),
                   *[pltpu.HBM(t.shape, t.dtype) for t in bufs], jax.ShapeDtypeStruct((8, LANES), F32)),
        in_specs=[HBM_SPEC] * nb + [ANY] * na,
        out_specs=(SEM_SPEC, SEM_SPEC, *[HBM_SPEC] * nb, pl.BlockSpec(memory_space=pltpu.VMEM)),
        input_output_aliases={i: 2 + i for i in range(nb)},
        compiler_params=pltpu.CompilerParams(has_side_effects=DATAFLOW_EFFECT),
        name=name,
    )(*[_in_hbm(t) for t in bufs], *after)
    return plan, res[:2], list(res[2:2 + nb]), res[2 + nb]


def _exchange_wait(flight, after, *, name):
    plan, sems, bufs, _ = flight
    nb = len(bufs)

    def body(*refs):
        send_sems, recv_sems = refs[nb], refs[nb + 1]
        for cp in plan(refs[:nb], send_sems, recv_sems, False):
            cp.wait_recv()
        for cp in plan(refs[:nb], send_sems, recv_sems, True):
            cp.wait_send()

    res = pl.pallas_call(
        body,
        out_shape=tuple(pltpu.HBM(t.shape, t.dtype) for t in bufs),
        in_specs=[HBM_SPEC] * nb + [SEM_SPEC, SEM_SPEC] + [ANY] * len(after),
        out_specs=tuple([HBM_SPEC] * nb),
        input_output_aliases={i: i for i in range(nb)},
        compiler_params=pltpu.CompilerParams(has_side_effects=DATAFLOW_EFFECT),
        name=name,
    )(*bufs, *sems, *after)
    return list(res)


_W_GROUPS = {"even": ("even_w_in", "even_w_out"), "mlp0": ("mlp_w1_0", "mlp_w2_0"),
             "odd": ("odd_w_in", "odd_w_uq", "odd_w_ukv", "odd_w_out"), "mlp1": ("mlp_w1_1", "mlp_w2_1")}


class _Exchanges:
    def __init__(self, shards, gains, n_q, n_kv):
        self.n_q, self.n_kv = n_q, n_kv
        self.mx, self.my, self.mc = _mesh_pos()
        self.dev = 4 * self.mx + 2 * self.my + self.mc
        self.shards = shards
        self.gains = gains
        self.flights, self.gathered, self.grad_blocks, self.grad_names, self.reduced = {}, {}, {}, {}, {}

    def start(self):
        names = _W_GROUPS["even"]
        got = _all_gather([self.shards[n] for n in names] + [self.gains], name="comm_even_gather")
        self.gathered.update(zip(names, got[:-1]))
        self.all_gains = got[-1][:, 0]
        return self._w_begin("mlp0", [got[0]])

    def _w_begin(self, group, after):
        srcs = [self.shards[n] for n in _W_GROUPS[group]]
        lands = [lax.dynamic_update_slice(lax.empty((N_DEV,) + t.shape, t.dtype), t[None], (self.dev, 0, 0)) for t in srcs]
        self.flights[group] = _exchange_start(_plan_own_blocks(len(srcs)), srcs + lands, after, name=f"comm_{group}_own_start")
        return [self.flights[group][3]]

    def _w_turn(self, group, after):
        bufs = _exchange_wait(self.flights[group], after, name=f"comm_{group}_own_wait")
        n = len(bufs) // 2
        self.flights[group] = _exchange_start(_plan_pass_on(n), bufs[n:], [], name=f"comm_{group}_pass_start")
        return [self.flights[group][3]]

    def _w_end(self, group, after):
        self.gathered.update(zip(_W_GROUPS[group], _exchange_wait(self.flights.pop(group), after, name=f"comm_{group}_pass_wait")))

    def weights(self, group):
        return {n: self.gathered[n] for n in _W_GROUPS[group]}

    def norm_gains(self):
        return (self.all_gains[:, :self.n_q].reshape(-1), self.all_gains[:, self.n_q:self.n_q + self.n_kv].reshape(-1))

    def grads(self, group, named_blocks):
        self.grad_names[group] = [n for n, _ in named_blocks]
        self.grad_blocks[group] = [t for _, t in named_blocks]

    def _g_begin(self, group, after):
        blocks = self.grad_blocks[group]
        lands = [lax.empty((4,) + t.shape[1:], t.dtype) for t in blocks]
        self.flights[group] = _exchange_start(_plan_to_sibling(len(blocks)), blocks + lands, after, name=f"comm_{group}_sib_start")
        return [self.flights[group][3]]

    def _g_turn(self, group, after):
        bufs = _exchange_wait(self.flights[group], after, name=f"comm_{group}_sib_wait")
        n = len(bufs) // 2
        partial = [_pair_sum(a, b, self.mc, name=f"pair_sum_{nm}") for nm, a, b in zip(self.grad_names[group], bufs[:n], bufs[n:])]
        lands = [lax.empty((3,) + t.shape[1:], t.dtype) for t in partial]
        self.flights[group] = _exchange_start(_plan_to_chips(n), partial + lands, [], name=f"comm_{group}_chips_start")
        return [self.flights[group][3]]

    def _g_end(self, group, after):
        bufs = _exchange_wait(self.flights.pop(group), after, name=f"comm_{group}_chips_wait")
        n = len(bufs) // 2
        for nm, a, b in zip(self.grad_names[group], bufs[:n], bufs[n:]):
            self.reduced[nm] = _final_sum(a, b, 2 * self.mx + self.my, name=f"final_sum_{nm}")

    _SCHEDULE = {
        "even_out": (("w_turn", "mlp0"), ("w_begin", "odd")),
        "ln1_l0": (("w_end", "mlp0"),),
        "ln2_l0": (("w_turn", "odd"), ("w_begin", "mlp1"), ("w_end", "odd")),
        "odd_out": (("w_turn", "mlp1"),),
        "ln1_l1": (("w_end", "mlp1"),),
        "dw_l1": (("g_begin", "mlp1"),),
        "ln1_bwd_l1": (("g_turn", "mlp1"),),
        "odd_in_dw": (("g_begin", "odd"),),
        "mlp2_dx_l0": (("g_end", "mlp1"), ("g_turn", "odd")),
        "dw_l0": (("g_begin", "mlp0"),),
        "ln1_bwd_l0": (("g_end", "odd"), ("g_turn", "mlp0")),
        "even_in_dw": (("g_begin", "even"),),
        "even_in_dx": (("g_end", "mlp0"), ("g_turn", "even")),
        "finish": (("g_end", "even"),),
    }

    def sync(self, tag, after):
        latest, started = list(after), []
        for what, group in self._SCHEDULE[tag]:
            out = getattr(self, "_" + what)(group, latest)
            if out:
                latest = started = out
        return started


def _alibi(n):
    return 2.0 ** (-8.0 * np.arange(1, n + 1, dtype=np.float32) / n)


def _heads(t, n):
    s = t.shape[0]
    return t.reshape(s, n, t.shape[1] // n).transpose(1, 0, 2)


def _unheads(t):
    n, s, dh = t.shape
    return t.transpose(1, 0, 2).reshape(s, n * dh)


def _to_strided(t, d):
    s, x = t.shape
    return t if d == 1 else t.reshape(s // d, d, x).transpose(1, 0, 2).reshape(s, x)


def _from_strided(t, d):
    s, x = t.shape
    return t if d == 1 else t.reshape(d, s // d, x).transpose(1, 0, 2).reshape(s, x)


def _true_columns(t, c, n_pad):
    r = t.shape[1]
    w = t[:, :, :c].transpose(1, 0, 2).reshape(r, N_DEV * c)
    return jnp.pad(w, ((0, 0), (0, n_pad - N_DEV * c)))


def _column_blocks(t, c, cp):
    r = t.shape[0]
    b = t[:, :N_DEV * c].reshape(r, N_DEV, c).transpose(1, 0, 2)
    return jnp.pad(b, ((0, 0), (0, 0), (0, cp - c)))


def _stack_rows(t, nq):
    return t.reshape(nq, BLK, A_KV_HEADS, A_GROUP, HEAD_DIM).transpose(2, 0, 3, 1, 4).reshape(
        A_KV_HEADS, 1, nq * A_GROUP * BLK, HEAD_DIM)


def _unstack_rows(t, nq):
    return t.reshape(A_KV_HEADS, nq, A_GROUP, BLK, HEAD_DIM).transpose(1, 3, 0, 2, 4).reshape(
        nq * BLK, A_Q_W)


def _lead_block(t):
    return jnp.pad(t, ((0, 0), (BLK, 0), (0, 0)))


def _columns(t):
    return t.transpose(1, 0, 2).reshape(t.shape[1], -1)


def _rows(t):
    return t.reshape(-1, t.shape[2])


def _by_column_block(t):
    return t.reshape(t.shape[0], N_DEV, -1).transpose(1, 0, 2)


def _local_step(x0, target, comm, sinks, ln1_g, ln1_b, ln2_g, ln2_b):
    s, d = x0.shape
    scale_h = 1.0 / math.sqrt(HEAD_DIM)
    scale_d = 1.0 / math.sqrt(D_NOPE + D_ROPE)
    nq = s // BLK
    even_c, odd_c = EVEN_IN // N_DEV, ODD_IN // N_DEV
    bf = lambda t: t.astype(MXU_DTYPE)
    blocks = dict(b_blocks=True)

    tok = comm.start()
    w_even = comm.weights("even")
    even_cp, even_n = w_even["even_w_in"].shape[2], -(-EVEN_IN // 1024) * 1024
    w_even_in, w_even_out = _true_columns(w_even["even_w_in"], even_c, even_n), _columns(w_even["even_w_out"])
    x0b = bf(x0)
    h_e = _mm(x0b, w_even_in, out_dtypes=(MXU_DTYPE,), name="even_in_fwd", deps=tok)
    qa = _stack_rows(h_e[:, :A_Q_W], nq)
    ka = _lead_block(_heads(h_e[:, A_Q_W:A_Q_W + A_KV_W], A_KV_HEADS))[:, None]
    va = _lead_block(_heads(h_e[:, A_Q_W + A_KV_W:A_Q_W + 2 * A_KV_W], A_KV_HEADS))[:, None]
    rows_a = A_GROUP * BLK
    slope_a = jnp.asarray(np.repeat(_alibi(A_Q_HEADS).reshape(A_KV_HEADS, A_GROUP), BLK, axis=1).reshape(
        A_KV_HEADS, 1, rows_a, 1))
    sink_a = jnp.broadcast_to(sinks.reshape(A_KV_HEADS, A_GROUP, 1), (A_KV_HEADS, A_GROUP, BLK)).reshape(
        A_KV_HEADS, 1, rows_a, 1)
    a_cfg = dict(scale=scale_h, n_back=A_WINDOW - 1, bps=nq)
    oa, lse_a = _band_fwd(qa, ka, va, slope_a, sink_a, name="swa_fwd", **a_cfg)
    b_slab, b_cfg, b_out, b_lse = [], [], [], []
    base = A_Q_W + 2 * A_KV_W
    for gi, (window, dil) in enumerate(B_PATTERNS):
        slab = _to_strided(h_e[:, base + gi * 3 * B_W: base + (gi + 1) * 3 * B_W], dil)
        cfg = dict(slopes=_alibi(B_HEADS) * dil, scale=scale_h, n_back=window // dil, bps=nq // dil)
        o, lse = _dil_fwd(slab, name=f"dil{gi}_fwd", **cfg)
        both = _from_strided(jnp.concatenate([o, lse], axis=1), dil)
        b_slab.append(slab)
        b_cfg.append(cfg)
        b_out.append(both[:, :B_W])
        b_lse.append(both[:, B_W:])
    ob, lse_b = _merge(b_out, b_lse, name="dil_merge")
    y_e = bf(jnp.concatenate([_unstack_rows(oa, nq), ob], axis=1))
    mixed = _mm(y_e, w_even_out, name="even_out_fwd")
    tok = comm.sync("even_out", [mixed])
    x0n, x0nb, xh1_0, r1_0 = _ln_fwd(x0, mixed, ln1_g[0], ln1_b[0], name="ln1_fwd_l0", deps=tok)
    comm.sync("ln1_l0", [x0nb])
    w_mlp0 = comm.weights("mlp0")
    w1_0, w2_0 = w_mlp0["mlp_w1_0"], _rows(w_mlp0["mlp_w2_0"])
    act0, hid0 = _mm(x0nb, w1_0, out_dtypes=(MXU_DTYPE, MXU_DTYPE), epilogue=_relu_sq, name="mlp1_fwd_l0", **blocks)
    mlp = _mm(hid0, w2_0, name="mlp2_fwd_l0")
    x1, x1b, xh2_0, r2_0 = _ln_fwd(x0n, mlp, ln2_g[0], ln2_b[0], name="ln2_fwd_l0")

    tok = comm.sync("ln2_l0", [x1b])
    w_odd = comm.weights("odd")
    odd_cp, odd_n = w_odd["odd_w_in"].shape[2], -(-ODD_IN // 1024) * 1024
    w_odd_in, w_uq, w_ukv, w_odd_out = (_true_columns(w_odd["odd_w_in"], odd_c, odd_n), _columns(w_odd["odd_w_uq"]),
                                        _columns(w_odd["odd_w_ukv"]), _rows(w_odd["odd_w_out"]))
    gq, gkv = comm.norm_gains()
    h_o = _mm(x1b, w_odd_in, name="odd_in_fwd", deps=tok)
    qkv_c = bf(h_o[:, :3 * C_W])
    oc, sb_total = _sb_fwd(qkv_c, heads=C_HEADS, scale=scale_h, name="sb_fwd")
    o_cq, o_ckv, o_kr = 3 * C_W, 3 * C_W + D_Q_RANK, 3 * C_W + D_Q_RANK + D_KV_RANK
    cq, ckv, kr = h_o[:, o_cq:o_ckv], h_o[:, o_ckv:o_kr], h_o[:, o_kr:o_kr + D_ROPE]
    ncq, rq = _rms_fwd(cq, gq, name="rms_q_fwd")
    nckv, rkv = _rms_fwd(ckv, gkv, name="rms_kv_fwd")
    lane_pad = LANES - D_NOPE - D_ROPE
    w_uq = jnp.pad(w_uq.reshape(D_Q_RANK, D_HEADS, D_NOPE + D_ROPE), ((0, 0), (0, 0), (0, lane_pad))).reshape(
        D_Q_RANK, D_HEADS * LANES)
    qd = _mm(ncq, w_uq, name="uq_fwd")
    kvd = _mm(nckv, w_ukv, out_dtypes=(MXU_DTYPE,), name="ukv_fwd")
    rope_t = _rope_tables(s, inverse=False)
    krp = _rope(jnp.pad(kr, ((0, 0), (D_NOPE, lane_pad)))[None], rope_t, out_dtype=MXU_DTYPE, name="rope_k_fwd")[0]
    od, lse_d = _mla_fwd(qd, kvd, krp, rope_t, scale=scale_d, name="mla_fwd")
    y_o = bf(jnp.concatenate([oc, od], axis=1))
    mixed = _mm(y_o, w_odd_out, name="odd_out_fwd")
    tok = comm.sync("odd_out", [mixed])
    x1n, x1nb, xh1_1, r1_1 = _ln_fwd(x1, mixed, ln1_g[1], ln1_b[1], name="ln1_fwd_l1", deps=tok)
    comm.sync("ln1_l1", [x1nb])
    w_mlp1 = comm.weights("mlp1")
    w1_1, w2_1 = w_mlp1["mlp_w1_1"], _rows(w_mlp1["mlp_w2_1"])
    act1, hid1 = _mm(x1nb, w1_1, out_dtypes=(MXU_DTYPE, MXU_DTYPE), epilogue=_relu_sq, name="mlp1_fwd_l1", **blocks)
    mlp = _mm(hid1, w2_1, name="mlp2_fwd_l1")
    y, _, xh2_1, r2_1 = _ln_fwd(x1n, mlp, ln2_g[1], ln2_b[1], name="ln2_fwd_l1")
    dy, loss_vec = _loss_head(y, target, name="loss_head")

    def mlp_block_bwd(g_out, layer, w1, w2, xh2, r2, xh1, r1, act, hid, xnb):
        du2, du2b, dg2, db2 = _ln_bwd(g_out, xh2, r2, ln2_g[layer], name=f"ln2_bwd_l{layer}")
        dpre = _mm(du2b, w2, nt=True, out_dtypes=(MXU_DTYPE,), epilogue=_relu_sq_grad, extra=act, name=f"mlp2_dx_l{layer}")
        tok = comm.sync("mlp2_dx_l0", [dpre]) if layer == 0 else []
        dw2 = _mm(hid, du2b, ta=True, out_dtypes=(BF16,), name=f"mlp2_dw_l{layer}", deps=tok)
        dw1 = _mm(xnb, dpre, ta=True, out_blocks=True, out_dtypes=(BF16,), name=f"mlp1_dw_l{layer}")
        comm.grads(f"mlp{layer}", [(f"mlp_w1_{layer}", dw1), (f"mlp_w2_{layer}", dw2.reshape(N_DEV, -1, d))])
        tok = comm.sync(f"dw_l{layer}", [dw1])
        dxn = _mm(dpre, w1, nt=True, epilogue=_add_alpha, extra=du2, name=f"mlp1_dx_l{layer}", deps=tok, **blocks)
        du1, du1b, dg1, db1 = _ln_bwd(dxn, xh1, r1, ln1_g[layer], name=f"ln1_bwd_l{layer}")
        return du1, du1b, comm.sync(f"ln1_bwd_l{layer}", [du1b]), (dg1, db1, dg2, db2)

    du1, du1b, tok, ln_1 = mlp_block_bwd(dy, 1, w1_1, w2_1, xh2_1, r2_1, xh1_1, r1_1, act1, hid1, x1nb)
    d_odd_out = _mm(y_o, du1b, ta=True, out_dtypes=(BF16,), name="odd_out_dw").reshape(N_DEV, -1, d)
    dy_o = _mm(du1b, w_odd_out, nt=True, name="odd_out_dx", deps=tok)
    dqc, dkc, dvc = _sb_bwd(qkv_c, dy_o, sb_total, heads=C_HEADS, scale=scale_h, name="sb_bwd")
    dqd, dkvd, dkr_pairs = _mla_bwd(qd, kvd, krp, rope_t, dy_o, od, lse_d, do_block0=C_W // LANES, scale=scale_d,
                                    name="mla_bwd")
    _, dkr_sum = _rope(dkr_pairs, _rope_tables(s, inverse=True), out_dtype=F32, head_sum=True, name="rope_k_bwd")
    dqd, dkvd = bf(dqd), bf(dkvd)
    d_uq = _mm(ncq, dqd, ta=True, out_dtypes=(BF16,), name="uq_dw").reshape(D_Q_RANK, D_HEADS, LANES)[:, :, :D_NOPE + D_ROPE].reshape(
        D_Q_RANK, D_HEADS * (D_NOPE + D_ROPE))
    dncq = _mm(dqd, w_uq, nt=True, name="uq_dx")
    d_ukv = _mm(nckv, dkvd, ta=True, out_dtypes=(BF16,), name="ukv_dw")
    dnckv = _mm(dkvd, w_ukv, nt=True, name="ukv_dx")
    dcq, dgq = _rms_bwd(dncq, cq, rq, gq, name="rms_q_bwd")
    dckv, dgkv = _rms_bwd(dnckv, ckv, rkv, gkv, name="rms_kv_bwd")
    dh_o = bf(jnp.concatenate(
        [dqc, dkc, dvc, dcq, dckv, dkr_sum[:, D_NOPE:D_NOPE + D_ROPE], jnp.zeros((s, odd_n - ODD_IN), F32)], axis=1))
    d_odd_in = _column_blocks(_mm(x1b, dh_o, ta=True, out_dtypes=(BF16,), name="odd_in_dw"), odd_c, odd_cp)
    comm.grads("odd", [("odd_w_in", d_odd_in), ("odd_w_uq", _by_column_block(d_uq)),
                       ("odd_w_ukv", _by_column_block(d_ukv)), ("odd_w_out", d_odd_out)])
    tok = comm.sync("odd_in_dw", [d_odd_in])
    dx1 = _mm(dh_o, w_odd_in, nt=True, epilogue=_add_alpha, extra=du1, name="odd_in_dx", deps=tok)

    du1, du1b, tok, ln_0 = mlp_block_bwd(dx1, 0, w1_0, w2_0, xh2_0, r2_0, xh1_0, r1_0, act0, hid0, x0nb)
    d_even_out = _mm(y_e, du1b, ta=True, out_dtypes=(BF16,), name="even_out_dw")
    dy_e = _mm(du1b, w_even_out, nt=True, name="even_out_dx", deps=tok)
    doa = _stack_rows(dy_e[:, :A_Q_W], nq)
    dqa, dka, dva, dsink = _band_bwd(qa, ka, va, doa, oa, lse_a, slope_a, sink_a, name="swa_bwd", **a_cfg)
    pieces = [_unstack_rows(dqa, nq), _unheads(dka[:, 0, BLK:]), _unheads(dva[:, 0, BLK:])]
    pack = jnp.concatenate([dy_e[:, A_Q_W:], ob, lse_b], axis=1)
    for gi, (_, dil) in enumerate(B_PATTERNS):
        grads = _dil_bwd(b_slab[gi], _to_strided(pack, dil), name=f"dil{gi}_bwd", **b_cfg[gi])
        pieces.append(_from_strided(jnp.concatenate(grads, axis=1), dil))
    dh_e = bf(jnp.concatenate(pieces + [jnp.zeros((s, even_n - EVEN_IN), F32)], axis=1))
    d_even_in = _column_blocks(_mm(x0b, dh_e, ta=True, out_dtypes=(BF16,), name="even_in_dw"), even_c, even_cp)
    comm.grads("even", [("even_w_in", d_even_in), ("even_w_out", _by_column_block(d_even_out))])
    tok = comm.sync("even_in_dw", [d_even_in])
    grad_x = _mm(dh_e, w_even_in, nt=True, epilogue=_add_alpha, extra=du1, name="even_in_dx", deps=tok)
    tok = comm.sync("even_in_dx", [grad_x])

    ln = [jnp.concatenate([a, b], axis=0) for a, b in zip(ln_0, ln_1)]
    small = {"ln": ln, "sinks": dsink[:, :, 0].reshape(-1), "gq": dgq[0], "gkv": dgkv[0], "loss": loss_vec[0, :1]}
    return grad_x, small, tok


def kernel(x, even_w_in, even_sinks, even_w_out, odd_w_in, odd_q_norm_g, odd_kv_norm_g, odd_w_uq, odd_w_ukv, odd_w_out, ln1_g, ln1_b, mlp_w1, mlp_w2, ln2_g, ln2_b, loss_target, m_even_w_in, m_even_sinks, m_even_w_out, m_odd_w_in, m_odd_q_norm_g, m_odd_kv_norm_g, m_odd_w_uq, m_odd_w_ukv, m_odd_w_out, m_ln1_g, m_ln1_b, m_mlp_w1, m_mlp_w2, m_ln2_g, m_ln2_b, v_even_w_in, v_even_sinks, v_even_w_out, v_odd_w_in, v_odd_q_norm_g, v_odd_kv_norm_g, v_odd_w_uq, v_odd_w_ukv, v_odd_w_out, v_ln1_g, v_ln1_b, v_mlp_w1, v_mlp_w2, v_ln2_g, v_ln2_b):
    weights = dict(even_w_in=even_w_in, even_sinks=even_sinks, even_w_out=even_w_out, odd_w_in=odd_w_in,
                   odd_q_norm_g=odd_q_norm_g, odd_kv_norm_g=odd_kv_norm_g, odd_w_uq=odd_w_uq, odd_w_ukv=odd_w_ukv,
                   odd_w_out=odd_w_out, ln1_g=ln1_g, ln1_b=ln1_b, mlp_w1=mlp_w1, mlp_w2=mlp_w2, ln2_g=ln2_g, ln2_b=ln2_b)
    mom_m = dict(even_w_in=m_even_w_in, even_sinks=m_even_sinks, even_w_out=m_even_w_out, odd_w_in=m_odd_w_in,
                 odd_q_norm_g=m_odd_q_norm_g, odd_kv_norm_g=m_odd_kv_norm_g, odd_w_uq=m_odd_w_uq, odd_w_ukv=m_odd_w_ukv,
                 odd_w_out=m_odd_w_out, ln1_g=m_ln1_g, ln1_b=m_ln1_b, mlp_w1=m_mlp_w1, mlp_w2=m_mlp_w2, ln2_g=m_ln2_g, ln2_b=m_ln2_b)
    mom_v = dict(even_w_in=v_even_w_in, even_sinks=v_even_sinks, even_w_out=v_even_w_out, odd_w_in=v_odd_w_in,
                 odd_q_norm_g=v_odd_q_norm_g, odd_kv_norm_g=v_odd_kv_norm_g, odd_w_uq=v_odd_w_uq, odd_w_ukv=v_odd_w_ukv,
                 odd_w_out=v_odd_w_out, ln1_g=v_ln1_g, ln1_b=v_ln1_b, mlp_w1=v_mlp_w1, mlp_w2=v_mlp_w2, ln2_g=v_ln2_g, ln2_b=v_ln2_b)
    order = list(weights)
    n_q, n_kv = odd_q_norm_g.shape[1], odd_kv_norm_g.shape[1]

    def lane_padded(t):
        return jnp.pad(t, ((0, 0), (0, _lane_pad(t.shape[1]) - t.shape[1]))).astype(BF16)

    shards = {"even_w_in": lane_padded(even_w_in[0]), "even_w_out": even_w_out[0].astype(BF16),
              "mlp_w1_0": mlp_w1[0].astype(BF16), "mlp_w2_0": mlp_w2[0].astype(BF16),
              "odd_w_in": lane_padded(odd_w_in[0]), "odd_w_uq": odd_w_uq[0].astype(BF16),
              "odd_w_ukv": odd_w_ukv[0].astype(BF16), "odd_w_out": odd_w_out[0].astype(BF16),
              "mlp_w1_1": mlp_w1[1].astype(BF16), "mlp_w2_1": mlp_w2[1].astype(BF16)}
    gains = jnp.concatenate([odd_q_norm_g, odd_kv_norm_g, jnp.zeros((1, LANES - n_q - n_kv), F32)], axis=1)
    comm = _Exchanges(shards, gains, n_q, n_kv)
    dev = comm.dev

    grad_x, small, last_started = _local_step(x[0], loss_target[0], comm, even_sinks[0], ln1_g, ln1_b, ln2_g, ln2_b)

    grads, delta, new_m, new_v = {}, {}, {}, {}

    def update(n):
        g_list = [comm.reduced[f"{n}_0"], comm.reduced[f"{n}_1"]] if n.startswith("mlp") else [comm.reduced[n]]
        grads[n], delta[n], new_m[n], new_v[n] = _adamw(weights[n], g_list, mom_m[n], mom_v[n], name=f"adamw_{n}",
                                                        deps=last_started)

    early = ("mlp_w1", "mlp_w2", "odd_w_in", "odd_w_uq", "odd_w_ukv", "odd_w_out")
    for n in early:
        update(n)
    comm.sync("finish", [new_v[n] for n in early])
    update("even_w_in")
    update("even_w_out")

    small_parts = [t.reshape(-1) for t in small["ln"]] + [small["sinks"], small["gq"], small["gkv"], small["loss"]]
    small_sizes = [p.shape[0] for p in small_parts]
    n_small = sum(small_sizes)
    small_rows = -(-n_small // (8 * LANES)) * 8
    small_flat = jnp.concatenate(small_parts + [jnp.zeros((small_rows * LANES - n_small,), F32)]).reshape(small_rows, LANES)
    (small_all,) = _all_gather([small_flat], name="comm_small_gather", deps=[comm.reduced["even_w_in"]])
    totals = _sum_devices(small_all, name="small_sum").reshape(-1)
    tot, off = [], 0
    for size in small_sizes:
        tot.append(totals[off:off + size])
        off += size
    for i, n in enumerate(("ln1_g", "ln1_b", "ln2_g", "ln2_b")):
        grads[n] = tot[i].reshape(weights[n].shape)
    grads["even_sinks"] = tot[4].reshape(even_sinks.shape)
    grads["odd_q_norm_g"] = lax.dynamic_slice(tot[5], (dev * n_q,), (n_q,)).reshape(odd_q_norm_g.shape)
    grads["odd_kv_norm_g"] = lax.dynamic_slice(tot[6], (dev * n_kv,), (n_kv,)).reshape(odd_kv_norm_g.shape)
    loss = tot[7][0]

    small_names = [n for n in order if n not in early + ("even_w_in", "even_w_out")]
    n_sm = sum(weights[n].size for n in small_names)
    sm_rows = -(-n_sm // (8 * LANES)) * 8

    def pack_small(group):
        flat = [group[n].reshape(-1) for n in small_names]
        return jnp.concatenate(flat + [jnp.zeros((sm_rows * LANES - n_sm,), F32)]).reshape(1, sm_rows, LANES)

    res = _adamw(pack_small(weights), [pack_small(grads)[0]], pack_small(mom_m), pack_small(mom_v), name="adamw_small")
    off = 0
    for n in small_names:
        size = weights[n].size
        delta[n], new_m[n], new_v[n] = (t.reshape(-1)[off:off + size].reshape(weights[n].shape) for t in res[1:])
        off += size

    return (loss, grad_x[None], *[grads[n] for n in order], *[delta[n] for n in order],
            *[new_m[n] for n in order], *[new_v[n] for n in order])
```

```python
import math

import jax
import jax.numpy as jnp
import numpy as np
from jax import lax
from jax.experimental import pallas as pl
from jax.experimental.pallas import tpu as pltpu

F32 = jnp.float32
BF16 = jnp.bfloat16
MXU_DTYPE = BF16

HEAD_DIM = 64
A_Q_HEADS, A_KV_HEADS, A_WINDOW = 16, 2, 128
A_GROUP = A_Q_HEADS // A_KV_HEADS
B_HEADS = 8
B_PATTERNS = ((128, 1), (512, 4), (2048, 16))
C_HEADS = 16
D_HEADS, D_Q_RANK, D_KV_RANK, D_NOPE, D_ROPE, D_V = 16, 512, 256, 64, 32, 64
ROPE_BASE = 10000.0
LN_EPS, RMS_EPS = 1e-5, 1e-6
DEPTH = 2
ALPHA = (2 * DEPTH) ** 0.25
A_Q_W, A_KV_W, B_W = A_Q_HEADS * HEAD_DIM, A_KV_HEADS * HEAD_DIM, B_HEADS * HEAD_DIM
EVEN_IN = A_Q_W + 2 * A_KV_W + 3 * B_W * len(B_PATTERNS)
C_W = C_HEADS * HEAD_DIM
ODD_IN = 3 * C_W + D_Q_RANK + D_KV_RANK + D_ROPE
ADAM_LR, ADAM_B1, ADAM_B2, ADAM_EPS, ADAM_WD, ADAM_STEP = 0.001, 0.9, 0.999, 1e-08, 0.01, 10

N_DEV = 8
LANES = 128
BLK = 128
CAUSAL_TILE = 512
CUM_CHUNK = 256
NEG = -1e30
VMEM_LIMIT = 48 * 1024 * 1024

NN = ((1,), (0,))
NT = ((1,), (1,))
TN = ((0,), (0,))
MESH = pl.DeviceIdType.MESH
ANY = pl.BlockSpec(memory_space=pl.ANY)
HBM_SPEC = pl.BlockSpec(memory_space=pltpu.HBM)
SEM_SPEC = pl.BlockSpec(memory_space=pltpu.SEMAPHORE)
DATAFLOW_EFFECT = pltpu.SideEffectType.DATAFLOW_SIDE_EFFECTING


def _dot(a, b, dims):
    return lax.dot_general(a, b, (dims, ((), ())), preferred_element_type=F32)


def _bdot(a, b, dims):
    return jnp.stack([_dot(a[n], b[n], dims) for n in range(a.shape[0])])


def _params(*sem):
    return pltpu.CompilerParams(dimension_semantics=tuple(sem), vmem_limit_bytes=VMEM_LIMIT)


def _pick(n, cap, mult=LANES):
    if n <= cap:
        return n
    for t in range(cap - cap % mult, 0, -mult):
        if n % t == 0:
            return t
    raise ValueError(f"no tile for {n}")


def _lane_pad(c):
    return -(-c // LANES) * LANES


def _mm(a, b, *, name, nt=False, ta=False, b_blocks=False, out_blocks=False, out_dtypes=(F32,), epilogue=None, extra=None, deps=()):
    m, k = a.shape[::-1] if ta else a.shape
    if b_blocks:
        nb, kin, c = b.shape
        n = kin if nt else nb * c
        k_full = nb * c if nt else kin
    else:
        n, k_full = (b.shape if nt else b.shape[::-1])
    assert k == k_full, (a.shape, b.shape, nt, b_blocks)
    tm = _pick(m, 1024, 8)
    if b_blocks and not nt:
        tn, tk = c, _pick(k, 3072)
    elif b_blocks:
        per_step = max(g for g in (1, 2, 4, 8) if g * c <= 2048)
        tn, tk = _pick(n, 1024), per_step * c
    elif out_blocks:
        tn, tk = n // N_DEV, _pick(k, 3072)
    else:
        tn, tk = _pick(n, 512), _pick(k, 3072)
        if k > tk:
            tn, tk = _pick(n, 1024), _pick(k, 2048)
    nk = k // tk
    n_out = len(out_dtypes)

    def body(*refs):
        a_ref, b_ref = refs[0], refs[1]
        e_ref = refs[2] if extra is not None else None
        first_out = 2 + (extra is not None) + len(deps)
        out_refs = refs[first_out:first_out + n_out]

        def finish(acc):
            e = None if e_ref is None else e_ref[...]
            outs = (acc,) if epilogue is None else epilogue(acc, e)
            for r, o in zip(out_refs, outs):
                r[...] = o.astype(r.dtype).reshape(r.shape)

        if b_blocks and nt:
            part = _dot(a_ref[:, :c], b_ref[0], NT)
            for blk in range(1, per_step):
                part += _dot(a_ref[:, blk * c:(blk + 1) * c], b_ref[blk], NT)
        elif ta:
            part = _dot(a_ref[...], b_ref[...], TN)
        else:
            part = _dot(a_ref[...], b_ref[0] if b_blocks else b_ref[...], NT if nt else NN)
        if nk == 1:
            finish(part)
        else:
            acc_ref = refs[first_out + n_out]
            kk = pl.program_id(2)

            @pl.when(kk == 0)
            def _():
                acc_ref[...] = part

            @pl.when(kk > 0)
            def _():
                acc_ref[...] += part

            @pl.when(kk == nk - 1)
            def _():
                finish(acc_ref[...])

    if b_blocks and not nt:
        b_spec = pl.BlockSpec((1, tk, tn), lambda i, j, kk: (j, kk, 0))
    elif b_blocks:
        b_spec = pl.BlockSpec((per_step, tn, c), lambda i, j, kk: (kk, j, 0))
    elif nt:
        b_spec = pl.BlockSpec((tn, tk), lambda i, j, kk: (j, kk))
    else:
        b_spec = pl.BlockSpec((tk, tn), lambda i, j, kk: (kk, j))
    a_spec = pl.BlockSpec((tk, tm), lambda i, j, kk: (kk, i)) if ta else pl.BlockSpec((tm, tk), lambda i, j, kk: (i, kk))
    in_specs = [a_spec, b_spec]
    ins = [a.astype(MXU_DTYPE), b.astype(MXU_DTYPE)]
    if extra is not None:
        in_specs.append(pl.BlockSpec((tm, tn), lambda i, j, kk: (i, j)))
        ins.append(extra)
    in_specs += [ANY] * len(deps)
    ins += list(deps)
    if out_blocks:
        out_shape = tuple(jax.ShapeDtypeStruct((N_DEV, m, tn), d) for d in out_dtypes)
        out_specs = tuple(pl.BlockSpec((1, tm, tn), lambda i, j, kk: (j, i, 0)) for _ in out_dtypes)
    else:
        out_shape = tuple(jax.ShapeDtypeStruct((m, n), d) for d in out_dtypes)
        out_specs = tuple(pl.BlockSpec((tm, tn), lambda i, j, kk: (i, j)) for _ in out_dtypes)
    outs = pl.pallas_call(
        body,
        out_shape=out_shape,
        grid=(m // tm, n // tn, nk),
        in_specs=in_specs,
        out_specs=out_specs,
        scratch_shapes=[pltpu.VMEM((tm, tn), F32)] if nk > 1 else [],
        compiler_params=_params("parallel", "parallel", "arbitrary"),
        name=name,
    )(*ins)
    return outs[0] if n_out == 1 else outs


def _relu_sq(acc, _):
    act = jnp.maximum(acc, 0.0)
    return act, act * act


def _relu_sq_grad(acc, act):
    return (acc * (2.0 * act.astype(F32)),)


def _add_alpha(acc, du):
    return (acc + ALPHA * du,)


def _ln_fwd(x, mixed, g, b, *, name, deps=()):
    s, d = x.shape
    tr = _pick(s, 256, 8)

    def body(x_ref, m_ref, g_ref, b_ref, *rest):
        y_ref, yb_ref, xh_ref, r_ref = rest[len(deps):]
        u = ALPHA * x_ref[...] + m_ref[...]
        mu = jnp.mean(u, axis=-1, keepdims=True)
        xc = u - mu
        var = jnp.mean(xc * xc, axis=-1, keepdims=True)
        r = lax.rsqrt(var + LN_EPS)
        xh = xc * r
        y = xh * g_ref[...] + b_ref[...]
        y_ref[...] = y
        yb_ref[...] = y.astype(MXU_DTYPE)
        xh_ref[...] = xh
        r_ref[...] = r

    row = pl.BlockSpec((tr, d), lambda i: (i, 0))
    vec = pl.BlockSpec((1, d), lambda i: (0, 0))
    return pl.pallas_call(
        body,
        out_shape=(jax.ShapeDtypeStruct((s, d), F32), jax.ShapeDtypeStruct((s, d), MXU_DTYPE),
                   jax.ShapeDtypeStruct((s, d), F32), jax.ShapeDtypeStruct((s, 1), F32)),
        grid=(s // tr,),
        in_specs=[row, row, vec, vec] + [ANY] * len(deps),
        out_specs=(row, row, row, pl.BlockSpec((tr, 1), lambda i: (i, 0))),
        compiler_params=_params("parallel"),
        name=name,
    )(x, mixed, g.reshape(1, d), b.reshape(1, d), *deps)


def _ln_bwd(dy, xh, r, g, *, name):
    s, d = dy.shape
    tr = _pick(s, 256, 8)

    def body(dy_ref, xh_ref, r_ref, g_ref, du_ref, dub_ref, dg_ref, db_ref):
        dyv, xhv = dy_ref[...], xh_ref[...]
        dxh = dyv * g_ref[...]
        c1 = jnp.mean(dxh, axis=-1, keepdims=True)
        c2 = jnp.mean(dxh * xhv, axis=-1, keepdims=True)
        du = r_ref[...] * (dxh - c1 - xhv * c2)
        du_ref[...] = du
        dub_ref[...] = du.astype(MXU_DTYPE)

        @pl.when(pl.program_id(0) == 0)
        def _():
            dg_ref[...] = jnp.zeros_like(dg_ref)
            db_ref[...] = jnp.zeros_like(db_ref)

        dg_ref[...] += jnp.sum(dyv * xhv, axis=0, keepdims=True)
        db_ref[...] += jnp.sum(dyv, axis=0, keepdims=True)

    row = pl.BlockSpec((tr, d), lambda i: (i, 0))
    vec = pl.BlockSpec((1, d), lambda i: (0, 0))
    return pl.pallas_call(
        body,
        out_shape=(jax.ShapeDtypeStruct((s, d), F32), jax.ShapeDtypeStruct((s, d), MXU_DTYPE),
                   jax.ShapeDtypeStruct((1, d), F32), jax.ShapeDtypeStruct((1, d), F32)),
        grid=(s // tr,),
        in_specs=[row, row, pl.BlockSpec((tr, 1), lambda i: (i, 0)), vec],
        out_specs=(row, row, vec, vec),
        compiler_params=_params("arbitrary"),
        name=name,
    )(dy, xh, r, g.reshape(1, d))


def _rms_fwd(x, g, *, name):
    s, d = x.shape
    tr = _pick(s, 512, 8)

    def body(x_ref, g_ref, y_ref, r_ref):
        xv = x_ref[...]
        r = lax.rsqrt(jnp.mean(xv * xv, axis=-1, keepdims=True) + RMS_EPS)
        y_ref[...] = (xv * r * g_ref[...]).astype(y_ref.dtype)
        r_ref[...] = r

    return pl.pallas_call(
        body,
        out_shape=(jax.ShapeDtypeStruct((s, d), MXU_DTYPE), jax.ShapeDtypeStruct((s, 1), F32)),
        grid=(s // tr,),
        in_specs=[pl.BlockSpec((tr, d), lambda i: (i, 0)), pl.BlockSpec((1, d), lambda i: (0, 0))],
        out_specs=(pl.BlockSpec((tr, d), lambda i: (i, 0)), pl.BlockSpec((tr, 1), lambda i: (i, 0))),
        compiler_params=_params("parallel"),
        name=name,
    )(x, g.reshape(1, d))


def _rms_bwd(dy, x, r, g, *, name):
    s, d = x.shape
    tr = _pick(s, 512, 8)

    def body(dy_ref, x_ref, r_ref, g_ref, dx_ref, dg_ref):
        dyv, rv = dy_ref[...], r_ref[...]
        xn = x_ref[...] * rv
        dxn = dyv * g_ref[...]
        dx_ref[...] = rv * (dxn - xn * jnp.mean(dxn * xn, axis=-1, keepdims=True))

        @pl.when(pl.program_id(0) == 0)
        def _():
            dg_ref[...] = jnp.zeros_like(dg_ref)

        dg_ref[...] += jnp.sum(dyv * xn, axis=0, keepdims=True)

    row = pl.BlockSpec((tr, d), lambda i: (i, 0))
    vec = pl.BlockSpec((1, d), lambda i: (0, 0))
    return pl.pallas_call(
        body,
        out_shape=(jax.ShapeDtypeStruct((s, d), F32), jax.ShapeDtypeStruct((1, d), F32)),
        grid=(s // tr,),
        in_specs=[row, row, pl.BlockSpec((tr, 1), lambda i: (i, 0)), vec],
        out_specs=(row, vec),
        compiler_params=_params("arbitrary"),
        name=name,
    )(dy, x, r, g.reshape(1, d))


def _rope_tables(s, inverse):
    inv_freq = ROPE_BASE ** (-jnp.arange(0, D_ROPE, 2, dtype=F32) / D_ROPE)
    ang = jnp.arange(s, dtype=F32)[:, None] * inv_freq[None, :]
    cos, sin = jnp.cos(ang), jnp.sin(ang)
    if inverse:
        sin = -sin
    half = D_ROPE // 2
    one, zero = jnp.ones((s, D_NOPE), F32), jnp.zeros((s, D_NOPE), F32)
    pad1, pad0 = jnp.ones((s, LANES - D_NOPE - D_ROPE), F32), jnp.zeros((s, LANES - D_NOPE - D_ROPE), F32)
    zh = jnp.zeros((s, half), F32)
    c = jnp.concatenate([one, cos, cos, pad1], axis=1)
    s_lo = jnp.concatenate([zero, -sin, zh, pad0], axis=1)
    s_hi = jnp.concatenate([zero, zh, sin, pad0], axis=1)
    return c, s_lo, s_hi


def _rope(x, tables, *, out_dtype, head_sum=False, name):
    h, s, w = x.shape
    ts = _pick(s, 2048, 8)

    def body(x_ref, c_ref, lo_ref, hi_ref, y_ref, *sum_ref):
        y = _rotate(x_ref[0], c_ref[...], lo_ref[...], hi_ref[...])
        y_ref[0] = y.astype(y_ref.dtype)
        if head_sum:
            @pl.when(pl.program_id(1) == 0)
            def _():
                sum_ref[0][...] = jnp.zeros_like(sum_ref[0])

            sum_ref[0][...] += y

    tab = pl.BlockSpec((ts, w), lambda i, hh: (i, 0))
    blk = pl.BlockSpec((1, ts, w), lambda i, hh: (hh, i, 0))
    out_shape = [jax.ShapeDtypeStruct((h, s, w), out_dtype)]
    out_specs = [blk]
    if head_sum:
        out_shape.append(jax.ShapeDtypeStruct((s, w), F32))
        out_specs.append(tab)
    res = pl.pallas_call(
        body,
        out_shape=tuple(out_shape),
        grid=(s // ts, h),
        in_specs=[blk, tab, tab, tab],
        out_specs=tuple(out_specs),
        compiler_params=_params("parallel", "arbitrary"),
        name=name,
    )(x, *tables)
    return res if head_sum else res[0]


def _band_scores(q, kw, slope, i, *, scale, n_back, bps):
    b, r, _ = q.shape
    sc = _bdot(q, kw, NT) * scale
    shape = (b, r, 2 * BLK)
    row = lax.broadcasted_iota(jnp.int32, shape, 1) & (BLK - 1)
    col = lax.broadcasted_iota(jnp.int32, shape, 2)
    rel = BLK + row - col
    first_col = jnp.where(i % bps == 0, BLK, 0)
    valid = (rel >= 0) & (rel <= n_back) & (col >= first_col)
    return jnp.where(valid, sc - slope * rel.astype(F32), NEG)


def _band_fwd(q, k, v, slope, sink, *, scale, n_back, bps, name):
    g, b, rows, dh = q.shape
    r = slope.shape[2]
    nq = rows // r
    skv = k.shape[2]
    use_sink = sink is not None

    def body(*refs):
        q_ref, k_ref, v_ref, slope_ref = refs[:4]
        sink_ref = refs[4] if use_sink else None
        o_ref, lse_ref = refs[4 + use_sink:]
        i = pl.program_id(1)
        off = pl.multiple_of(i * BLK, BLK)
        kw = k_ref[0, :, pl.ds(off, 2 * BLK), :]
        vw = v_ref[0, :, pl.ds(off, 2 * BLK), :]
        sc = _band_scores(q_ref[0], kw, slope_ref[0], i, scale=scale, n_back=n_back, bps=bps)
        m = jnp.max(sc, axis=-1, keepdims=True)
        if use_sink:
            m = jnp.maximum(m, sink_ref[0])
        p = jnp.exp(sc - m)
        l = jnp.sum(p, axis=-1, keepdims=True)
        if use_sink:
            l = l + jnp.exp(sink_ref[0] - m)
        o_ref[0] = _bdot(p.astype(MXU_DTYPE), vw, NN) / l
        lse_ref[0] = m + jnp.log(l)

    qspec = pl.BlockSpec((1, b, r, dh), lambda gg, i: (gg, 0, i, 0))
    kspec = pl.BlockSpec((1, b, skv, dh), lambda gg, i: (gg, 0, 0, 0))
    rspec = pl.BlockSpec((1, b, r, 1), lambda gg, i: (gg, 0, 0, 0))
    ins = [q, k, v, slope] + ([sink] if use_sink else [])
    return pl.pallas_call(
        body,
        out_shape=(jax.ShapeDtypeStruct((g, b, rows, dh), F32), jax.ShapeDtypeStruct((g, b, rows, 1), F32)),
        grid=(g, nq),
        in_specs=[qspec, kspec, kspec, rspec] + ([rspec] if use_sink else []),
        out_specs=(qspec, pl.BlockSpec((1, b, r, 1), lambda gg, i: (gg, 0, i, 0))),
        compiler_params=_params("parallel", "arbitrary"),
        name=name,
    )(*ins)


def _band_bwd(q, k, v, do, o, lse, slope, sink, *, scale, n_back, bps, name):
    g, b, rows, dh = q.shape
    r = slope.shape[2]
    nq = rows // r
    skv = k.shape[2]
    use_sink = sink is not None
    stacked = r // BLK

    def body(*refs):
        q_ref, k_ref, v_ref, do_ref, o_ref, lse_ref, slope_ref = refs[:7]
        sink_ref = refs[7] if use_sink else None
        dq_ref, dk_ref, dv_ref = refs[7 + use_sink:10 + use_sink]
        i = pl.program_id(1)

        @pl.when(i == 0)
        def _():
            dk_ref[...] = jnp.zeros_like(dk_ref)
            dv_ref[...] = jnp.zeros_like(dv_ref)

        off = pl.multiple_of(i * BLK, BLK)
        qb = q_ref[0]
        kw = k_ref[0, :, pl.ds(off, 2 * BLK), :]
        vw = v_ref[0, :, pl.ds(off, 2 * BLK), :]
        dof = do_ref[0]
        dob = dof.astype(MXU_DTYPE)
        lse_b = lse_ref[0]
        delta = jnp.sum(dof * o_ref[0], axis=-1, keepdims=True)
        sc = _band_scores(qb, kw, slope_ref[0], i, scale=scale, n_back=n_back, bps=bps)
        p = jnp.exp(sc - lse_b)
        ds = (p * (_bdot(dob, vw, NT) - delta) * scale).astype(MXU_DTYPE)
        dq_ref[0] = _bdot(ds, kw, NN)
        dk_ref[0, :, pl.ds(off, 2 * BLK), :] += _bdot(ds, qb, TN)
        dv_ref[0, :, pl.ds(off, 2 * BLK), :] += _bdot(p.astype(MXU_DTYPE), dob, TN)

        if use_sink:
            dsink_ref = refs[10 + use_sink]

            @pl.when(i == 0)
            def _():
                dsink_ref[...] = jnp.zeros_like(dsink_ref)

            contrib = -jnp.exp(sink_ref[0] - lse_b) * delta
            for n in range(stacked):
                part = jnp.sum(contrib[0, n * BLK:(n + 1) * BLK, :], axis=0, keepdims=True)
                dsink_ref[0, n:n + 1, :] += jnp.broadcast_to(part, (1, LANES))

    def qspec(w):
        return pl.BlockSpec((1, b, r, w), lambda gg, i: (gg, 0, i, 0))

    kspec = pl.BlockSpec((1, b, skv, dh), lambda gg, i: (gg, 0, 0, 0))
    rspec = pl.BlockSpec((1, b, r, 1), lambda gg, i: (gg, 0, 0, 0))
    ins = [q, k, v, do, o, lse, slope] + ([sink] if use_sink else [])
    in_specs = [qspec(dh), kspec, kspec, qspec(dh), qspec(dh), qspec(1), rspec] + ([rspec] if use_sink else [])
    out_shape = [jax.ShapeDtypeStruct((g, b, rows, dh), F32), jax.ShapeDtypeStruct((g, b, skv, dh), F32),
                 jax.ShapeDtypeStruct((g, b, skv, dh), F32)]
    out_specs = [qspec(dh), kspec, kspec]
    if use_sink:
        assert b == 1
        out_shape.append(jax.ShapeDtypeStruct((g, stacked, LANES), F32))
        out_specs.append(pl.BlockSpec((1, stacked, LANES), lambda gg, i: (gg, 0, 0)))
    return pl.pallas_call(
        body,
        out_shape=tuple(out_shape),
        grid=(g, nq),
        in_specs=in_specs,
        out_specs=tuple(out_specs),
        compiler_params=_params("parallel", "arbitrary"),
        name=name,
    )(*ins)


def _pair_masks():
    first = lax.broadcasted_iota(jnp.int32, (1, LANES), 1) < HEAD_DIM
    m0 = first.astype(MXU_DTYPE)
    return first, (m0, 1 - m0)


def _dil_window(ref, i):
    prev = pl.multiple_of(jnp.maximum(i - 1, 0) * BLK, BLK)
    cur = pl.multiple_of(i * BLK, BLK)
    return prev, cur, jnp.concatenate([ref[pl.ds(prev, BLK), :], ref[pl.ds(cur, BLK), :]], axis=0)


def _dil_mask(i, n_back, bps):
    row = lax.broadcasted_iota(jnp.int32, (BLK, 2 * BLK), 0)
    col = lax.broadcasted_iota(jnp.int32, (BLK, 2 * BLK), 1)
    rel = BLK + row - col
    first_col = jnp.where(i % bps == 0, BLK, 0)
    return (rel >= 0) & (rel <= n_back) & (col >= first_col), rel.astype(F32)


def _dil_fwd(slab, slopes, *, scale, n_back, bps, name):
    s, w = slab.shape[0], slab.shape[1] // 3

    def body(q_ref, k_ref, v_ref, o_ref, lse_ref):
        i = pl.program_id(0)
        first, masks = _pair_masks()
        _, _, kw = _dil_window(k_ref, i)
        _, _, vw = _dil_window(v_ref, i)
        valid, rel = _dil_mask(i, n_back, bps)
        for p in range(w // LANES):
            cols = slice(p * LANES, (p + 1) * LANES)
            qp, kp, vp = q_ref[:, cols], kw[:, cols], vw[:, cols]
            outs, lses = [], []
            for hh in range(2):
                sc = _dot(qp * masks[hh], kp, NT) * scale - float(slopes[2 * p + hh]) * rel
                sc = jnp.where(valid, sc, NEG)
                m = jnp.max(sc, axis=-1, keepdims=True)
                e = jnp.exp(sc - m)
                l = jnp.sum(e, axis=-1, keepdims=True)
                outs.append(_dot(e.astype(MXU_DTYPE), vp, NN) / l)
                lses.append(m + jnp.log(l))
            o_ref[:, cols] = jnp.where(first, outs[0], outs[1])
            lse_ref[:, cols] = jnp.where(first, lses[0], lses[1])

    blk = pl.BlockSpec((BLK, w), lambda i: (i, 0))
    return pl.pallas_call(
        body,
        out_shape=(jax.ShapeDtypeStruct((s, w), F32), jax.ShapeDtypeStruct((s, w), F32)),
        grid=(s // BLK,),
        in_specs=[blk, pl.BlockSpec((s, w), lambda i: (0, 1)), pl.BlockSpec((s, w), lambda i: (0, 2))],
        out_specs=(blk, blk),
        compiler_params=_params("arbitrary"),
        name=name,
    )(slab, slab, slab)


def _dil_bwd(slab, pack, slopes, *, scale, n_back, bps, name):
    s, w = slab.shape[0], slab.shape[1] // 3

    def body(q_ref, k_ref, v_ref, do_ref, o_ref, lse_ref, dq_ref, dk_ref, dv_ref):
        i = pl.program_id(0)

        @pl.when(i == 0)
        def _():
            dk_ref[...] = jnp.zeros_like(dk_ref)
            dv_ref[...] = jnp.zeros_like(dv_ref)

        first, masks = _pair_masks()
        prev, cur, kw = _dil_window(k_ref, i)
        _, _, vw = _dil_window(v_ref, i)
        valid, rel = _dil_mask(i, n_back, bps)
        for p in range(w // LANES):
            cols = slice(p * LANES, (p + 1) * LANES)
            qp, kp, vp = q_ref[:, cols], kw[:, cols], vw[:, cols]
            dof, lse_p = do_ref[:, cols], lse_ref[:, cols]
            prod = dof * o_ref[:, cols]
            do_b = dof.astype(MXU_DTYPE)
            dqs, dk_add, dv_add = [], None, None
            for hh in range(2):
                qh, doh = qp * masks[hh], do_b * masks[hh]
                delta = jnp.sum(jnp.where(first, prod, 0.0) if hh == 0 else jnp.where(first, 0.0, prod), axis=-1, keepdims=True)
                sc = _dot(qh, kp, NT) * scale - float(slopes[2 * p + hh]) * rel
                e = jnp.exp(jnp.where(valid, sc, NEG) - lse_p[:, hh * HEAD_DIM:hh * HEAD_DIM + 1])
                ds = (e * (_dot(doh, vp, NT) - delta) * scale).astype(MXU_DTYPE)
                dqs.append(_dot(ds, kp, NN))
                dk_h, dv_h = _dot(ds, qh, TN), _dot(e.astype(MXU_DTYPE), doh, TN)
                dk_add = dk_h if dk_add is None else dk_add + dk_h
                dv_add = dv_h if dv_add is None else dv_add + dv_h
            dq_ref[:, cols] = jnp.where(first, dqs[0], dqs[1])
            dk_ref[pl.ds(prev, BLK), cols] += dk_add[:BLK]
            dk_ref[pl.ds(cur, BLK), cols] += dk_add[BLK:]
            dv_ref[pl.ds(prev, BLK), cols] += dv_add[:BLK]
            dv_ref[pl.ds(cur, BLK), cols] += dv_add[BLK:]

    def blk(c):
        return pl.BlockSpec((BLK, w), lambda i: (i, c))

    def whole(c):
        return pl.BlockSpec((s, w), lambda i: (0, c))

    shp = jax.ShapeDtypeStruct((s, w), F32)
    return pl.pallas_call(
        body,
        out_shape=(shp, shp, shp),
        grid=(s // BLK,),
        in_specs=[blk(0), whole(1), whole(2), blk(0), blk(1), blk(2)],
        out_specs=(blk(0), whole(0), whole(0)),
        compiler_params=_params("arbitrary"),
        name=name,
    )(slab, slab, slab, pack, pack, pack)


def _merge(outs, lses, *, name):
    s, w = outs[0].shape
    tr = _pick(s, 512, 8)

    def body(o0, o1, o2, l0, l1, l2, ob_ref, lt_ref):
        a, b, c = l0[...], l1[...], l2[...]
        m = jnp.maximum(jnp.maximum(a, b), c)
        ea, eb, ec = jnp.exp(a - m), jnp.exp(b - m), jnp.exp(c - m)
        den = ea + eb + ec
        ob_ref[...] = (ea / den) * o0[...] + (eb / den) * o1[...] + (ec / den) * o2[...]
        lt_ref[...] = m + jnp.log(den)

    spec = pl.BlockSpec((tr, w), lambda i: (i, 0))
    return pl.pallas_call(
        body,
        out_shape=(jax.ShapeDtypeStruct((s, w), F32), jax.ShapeDtypeStruct((s, w), F32)),
        grid=(s // tr,),
        in_specs=[spec] * 6,
        out_specs=(spec, spec),
        compiler_params=_params("parallel"),
        name=name,
    )(*outs, *lses)


def _tile_iotas(t):
    return lax.broadcasted_iota(jnp.int32, (t, t), 0), lax.broadcasted_iota(jnp.int32, (t, t), 1)


def _rotate(x, c, s_lo, s_hi):
    half = D_ROPE // 2
    return x * c + pltpu.roll(x, LANES - half, 1) * s_lo + pltpu.roll(x, half, 1) * s_hi


def _mla_keys(kv_h, kr_t, first):
    return jnp.where(first, kv_h, kr_t)


def _mla_fwd(qd, kvd, krp, tables, *, scale, name):
    s = qd.shape[0]
    pairs = qd.shape[1] // (2 * LANES)
    t = min(CAUSAL_TILE, s)

    def body(q_ref, kv_ref, kr_ref, c_ref, lo_ref, hi_ref, o_ref, lse_ref):
        i = pl.program_id(1)
        first = lax.broadcasted_iota(jnp.int32, (1, LANES), 1) < HEAD_DIM
        tabs = (c_ref[...], lo_ref[...], hi_ref[...])
        q_heads = [_rotate(q_ref[:, hh * LANES:(hh + 1) * LANES], *tabs).astype(MXU_DTYPE) for hh in range(2)]

        def tile(j, carry, diagonal):
            off = pl.multiple_of(j * t, t)
            kr_t = kr_ref[pl.ds(off, t), :]
            out = []
            for hh in range(2):
                m, l, acc = carry[3 * hh:3 * hh + 3]
                kv_h = kv_ref[pl.ds(off, t), hh * LANES:(hh + 1) * LANES]
                sc = _dot(q_heads[hh], _mla_keys(kv_h, kr_t, first), NT) * scale
                if diagonal:
                    row, col = _tile_iotas(t)
                    sc = jnp.where(row >= col, sc, NEG)
                m_new = jnp.maximum(m, jnp.max(sc, axis=-1, keepdims=True))
                a = jnp.exp(m - m_new)
                p = jnp.exp(sc - m_new)
                out += [m_new, a * l + jnp.sum(p, axis=-1, keepdims=True), a * acc + _dot(p.astype(MXU_DTYPE), kv_h, NN)]
            return tuple(out)

        init = (jnp.full((t, 1), NEG, F32), jnp.zeros((t, 1), F32), jnp.zeros((t, LANES), F32)) * 2
        carry = lax.fori_loop(0, i, lambda j, c: tile(j, c, False), init)
        m0, l0, acc0, m1, l1, acc1 = tile(i, carry, True)
        o_ref[...] = jnp.where(first, pltpu.roll(acc0 / l0, HEAD_DIM, 1), acc1 / l1)
        lse_ref[0] = jnp.where(lax.broadcasted_iota(jnp.int32, (t, 2), 1) == 0, m0 + jnp.log(l0), m1 + jnp.log(l1))

    tab = pl.BlockSpec((t, LANES), lambda p, i: (i, 0))
    return pl.pallas_call(
        body,
        out_shape=(jax.ShapeDtypeStruct((s, pairs * LANES), F32), jax.ShapeDtypeStruct((pairs, s, 2), F32)),
        grid=(pairs, s // t),
        in_specs=[pl.BlockSpec((t, 2 * LANES), lambda p, i: (i, p)), pl.BlockSpec((s, 2 * LANES), lambda p, i: (0, p)),
                  pl.BlockSpec((s, LANES), lambda p, i: (0, 0)), tab, tab, tab],
        out_specs=(pl.BlockSpec((t, LANES), lambda p, i: (i, p)), pl.BlockSpec((1, t, 2), lambda p, i: (p, i, 0))),
        compiler_params=_params("parallel", "arbitrary"),
        name=name,
    )(qd, kvd, krp, *tables)


def _mla_bwd(qd, kvd, krp, tables, do, o, lse, *, do_block0, scale, name):
    s = qd.shape[0]
    pairs = qd.shape[1] // (2 * LANES)
    t = min(CAUSAL_TILE, s)

    def body(q_ref, kv_ref, kr_ref, c_ref, lo_ref, hi_ref, do_ref, o_ref, lse_ref, dq_ref, dkv_ref, dkr_ref):
        i = pl.program_id(1)

        @pl.when(i == 0)
        def _():
            dkv_ref[...] = jnp.zeros_like(dkv_ref)
            dkr_ref[...] = jnp.zeros_like(dkr_ref)

        first = lax.broadcasted_iota(jnp.int32, (1, LANES), 1) < HEAD_DIM
        tabs = (c_ref[...], lo_ref[...], hi_ref[...])
        q_heads = [_rotate(q_ref[:, hh * LANES:(hh + 1) * LANES], *tabs).astype(MXU_DTYPE) for hh in range(2)]
        dof = do_ref[...]
        prod = dof * o_ref[...]
        deltas = [jnp.sum(jnp.where(first, prod, 0.0), axis=-1, keepdims=True),
                  jnp.sum(jnp.where(first, 0.0, prod), axis=-1, keepdims=True)]
        do_heads = [jnp.where(first, 0.0, pltpu.roll(dof, HEAD_DIM, 1)).astype(MXU_DTYPE),
                    jnp.where(first, 0.0, dof).astype(MXU_DTYPE)]
        lses = [lse_ref[0][:, hh:hh + 1] for hh in range(2)]

        def tile(j, carry, diagonal):
            off = pl.multiple_of(j * t, t)
            kr_t = kr_ref[pl.ds(off, t), :]
            out, dkr_add = [], None
            for hh in range(2):
                kv_h = kv_ref[pl.ds(off, t), hh * LANES:(hh + 1) * LANES]
                k_h = _mla_keys(kv_h, kr_t, first)
                sc = _dot(q_heads[hh], k_h, NT) * scale
                if diagonal:
                    row, col = _tile_iotas(t)
                    sc = jnp.where(row >= col, sc, NEG)
                p = jnp.exp(sc - lses[hh])
                ds = (p * (_dot(do_heads[hh], kv_h, NT) - deltas[hh]) * scale).astype(MXU_DTYPE)
                dk_full = _dot(ds, q_heads[hh], TN)
                dv_full = _dot(p.astype(MXU_DTYPE), do_heads[hh], TN)
                dkv_ref[pl.ds(off, t), hh * LANES:(hh + 1) * LANES] += jnp.where(first, dk_full, dv_full)
                rot = jnp.where(first, 0.0, dk_full)
                dkr_add = rot if dkr_add is None else dkr_add + rot
                out.append(carry[hh] + _dot(ds, k_h, NN))
            dkr_ref[0, pl.ds(off, t), :] += dkr_add
            return tuple(out)

        zacc = jnp.zeros((t, LANES), F32)
        carry = lax.fori_loop(0, i, lambda j, c: tile(j, c, False), (zacc, zacc))
        dq_heads = tile(i, carry, True)
        for hh in range(2):
            dq_ref[:, hh * LANES:(hh + 1) * LANES] = _rotate(dq_heads[hh], tabs[0], -tabs[1], -tabs[2])

    tab = pl.BlockSpec((t, LANES), lambda p, i: (i, 0))
    qspec = pl.BlockSpec((t, 2 * LANES), lambda p, i: (i, p))
    kvspec = pl.BlockSpec((s, 2 * LANES), lambda p, i: (0, p))
    return pl.pallas_call(
        body,
        out_shape=(jax.ShapeDtypeStruct(qd.shape, F32), jax.ShapeDtypeStruct(kvd.shape, F32),
                   jax.ShapeDtypeStruct((pairs, s, LANES), F32)),
        grid=(pairs, s // t),
        in_specs=[qspec, kvspec, pl.BlockSpec((s, LANES), lambda p, i: (0, 0)), tab, tab, tab,
                  pl.BlockSpec((t, LANES), lambda p, i: (i, do_block0 + p)), pl.BlockSpec((t, LANES), lambda p, i: (i, p)),
                  pl.BlockSpec((1, t, 2), lambda p, i: (p, i, 0))],
        out_specs=(qspec, kvspec, pl.BlockSpec((1, s, LANES), lambda p, i: (p, 0, 0))),
        compiler_params=_params("parallel", "arbitrary"),
        name=name,
    )(qd, kvd, krp, *tables, do, o, lse)


def _split_cumsum(x, tri, terms=2):
    hi = x.astype(BF16)
    if terms == 1:
        return _dot(hi, tri, NN)
    lo = (x - hi.astype(F32)).astype(BF16)
    return _dot(hi, tri, NN) + _dot(lo, tri, NN)


def _chunked_cumsum(x, tri, run, *, reverse, negate=False, terms=2):
    c = tri.shape[0]
    n = x.shape[1] // c
    parts = [None] * n
    for idx in (reversed(range(n)) if reverse else range(n)):
        xc = x[:, idx * c:(idx + 1) * c]
        sums = _split_cumsum(xc, tri, terms)
        parts[idx] = (-run) - sums if negate else run + sums
        run = run + jnp.sum(xc, axis=-1, keepdims=True)
    return (parts[0] if n == 1 else jnp.concatenate(parts, axis=1)), run


def _sb_logs(z):
    l1 = jnp.log(1.0 + jnp.exp(-jnp.abs(z)))
    return jnp.minimum(z, 0.0) - l1, -jnp.maximum(z, 0.0) - l1


def _scaled_query_heads(q, masks, scale):
    assert math.log2(scale).is_integer(), scale
    return [q * (m * scale).astype(q.dtype) for m in masks]


def _sb_fwd(qkv, *, heads, scale, name):
    s = qkv.shape[0]
    pairs = heads * HEAD_DIM // LANES
    t = min(CAUSAL_TILE, s)
    cc = min(CUM_CHUNK, t)

    def body(q_ref, k_ref, v_ref, o_ref, t_ref):
        i = pl.program_id(1)
        first, masks = _pair_masks()
        q_heads = _scaled_query_heads(q_ref[...], masks, scale)
        crow, ccol = _tile_iotas(cc)
        after = (crow > ccol).astype(BF16)

        def tile(j, carry, diagonal):
            off = pl.multiple_of(j * t, t)
            kb = k_ref[pl.ds(off, t), :]
            vb = v_ref[pl.ds(off, t), :]
            if diagonal:
                row, col = _tile_iotas(t)
                strict = row > col
            out = []
            for hh in range(2):
                run, acc = carry[2 * hh], carry[2 * hh + 1]
                log_beta, log_keep = _sb_logs(_dot(q_heads[hh], kb, NT))
                if diagonal:
                    log_keep = jnp.where(strict, log_keep, 0.0)
                a, run = _chunked_cumsum(log_keep, after, run, reverse=True)
                w = jnp.exp(log_beta + a)
                if diagonal:
                    w = jnp.where(strict, w, 0.0)
                out += [run, acc + _dot(w.astype(MXU_DTYPE), vb, NN)]
            return tuple(out)

        zero, zacc = jnp.zeros((t, 1), F32), jnp.zeros((t, LANES), F32)
        carry = tile(i, (zero, zacc, zero, zacc), True)
        run0, acc0, run1, acc1 = lax.fori_loop(0, i, lambda jj, c: tile(i - 1 - jj, c, False), carry)
        o_ref[...] = jnp.where(first, acc0, acc1)
        t_ref[0] = jnp.where(lax.broadcasted_iota(jnp.int32, (t, 2), 1) == 0, run0, run1)

    return pl.pallas_call(
        body,
        out_shape=(jax.ShapeDtypeStruct((s, heads * HEAD_DIM), F32), jax.ShapeDtypeStruct((pairs, s, 2), F32)),
        grid=(pairs, s // t),
        in_specs=[pl.BlockSpec((t, LANES), lambda p, i: (i, p)),
                  pl.BlockSpec((s, LANES), lambda p, i: (0, pairs + p)),
                  pl.BlockSpec((s, LANES), lambda p, i: (0, 2 * pairs + p))],
        out_specs=(pl.BlockSpec((t, LANES), lambda p, i: (i, p)), pl.BlockSpec((1, t, 2), lambda p, i: (p, i, 0))),
        compiler_params=_params("parallel", "arbitrary"),
        name=name,
    )(qkv, qkv, qkv)


def _sb_bwd(qkv, do, total, *, heads, scale, name):
    s = qkv.shape[0]
    pairs = heads * HEAD_DIM // LANES
    t = min(CAUSAL_TILE, s)
    cc = min(CUM_CHUNK, t)

    def body(q_ref, k_ref, v_ref, do_ref, t_ref, dq_ref, dk_ref, dv_ref):
        i = pl.program_id(1)

        @pl.when(i == 0)
        def _():
            dk_ref[...] = jnp.zeros_like(dk_ref)
            dv_ref[...] = jnp.zeros_like(dv_ref)

        first, masks = _pair_masks()
        q_heads = _scaled_query_heads(q_ref[...], masks, scale)
        do_b = do_ref[...].astype(MXU_DTYPE)
        do_heads = [do_b * m for m in masks]
        tots = [t_ref[0][:, hh:hh + 1] for hh in range(2)]
        crow, ccol = _tile_iotas(cc)
        upto = (crow <= ccol).astype(BF16)
        before = (crow < ccol).astype(BF16)

        def tile(j, carry, diagonal):
            off = pl.multiple_of(j * t, t)
            kb = k_ref[pl.ds(off, t), :]
            vb = v_ref[pl.ds(off, t), :]
            if diagonal:
                row, col = _tile_iotas(t)
                strict = row > col
            out, dk_add, dv_add = [], None, None
            for hh in range(2):
                run_keep, run_g, dq_acc = carry[3 * hh:3 * hh + 3]
                log_beta, log_keep = _sb_logs(_dot(q_heads[hh], kb, NT))
                keep = jnp.exp(log_keep)
                if diagonal:
                    log_keep = jnp.where(strict, log_keep, 0.0)
                a, run_keep = _chunked_cumsum(log_keep, upto, run_keep - tots[hh], reverse=False, negate=True)
                run_keep = run_keep + tots[hh]
                w = jnp.exp(log_beta + a)
                if diagonal:
                    w = jnp.where(strict, w, 0.0)
                g = w * _dot(do_heads[hh], vb, NT)
                prefix, run_g = _chunked_cumsum(g, before, run_g, reverse=False, terms=1)
                dz = g * keep - (1.0 - keep) * prefix
                if diagonal:
                    dz = jnp.where(strict, dz, 0.0)
                dz = dz.astype(MXU_DTYPE)
                dk_h = _dot(dz, q_heads[hh], TN)
                dv_h = _dot(w.astype(MXU_DTYPE), do_heads[hh], TN)
                dk_add = dk_h if dk_add is None else dk_add + dk_h
                dv_add = dv_h if dv_add is None else dv_add + dv_h
                out += [run_keep, run_g, dq_acc + _dot(dz, kb, NN)]
            dk_ref[pl.ds(off, t), :] += dk_add
            dv_ref[pl.ds(off, t), :] += dv_add
            return tuple(out)

        zero, zacc = jnp.zeros((t, 1), F32), jnp.zeros((t, LANES), F32)
        carry = lax.fori_loop(0, i, lambda j, c: tile(j, c, False), (zero, zero, zacc, zero, zero, zacc))
        res = tile(i, carry, True)
        dq_ref[...] = jnp.where(first, res[2], res[5]) * scale

    qspec = pl.BlockSpec((t, LANES), lambda p, i: (i, p))
    shp = jax.ShapeDtypeStruct((s, heads * HEAD_DIM), F32)
    return pl.pallas_call(
        body,
        out_shape=(shp, shp, shp),
        grid=(pairs, s // t),
        in_specs=[qspec, pl.BlockSpec((s, LANES), lambda p, i: (0, pairs + p)),
                  pl.BlockSpec((s, LANES), lambda p, i: (0, 2 * pairs + p)), qspec,
                  pl.BlockSpec((1, t, 2), lambda p, i: (p, i, 0))],
        out_specs=(qspec, pl.BlockSpec((s, LANES), lambda p, i: (0, p)), pl.BlockSpec((s, LANES), lambda p, i: (0, p))),
        compiler_params=_params("parallel", "arbitrary"),
        name=name,
    )(qkv, qkv, qkv, do, total)


def _loss_head(y, target, *, name):
    s, d = y.shape
    tr = _pick(s, 256, 8)

    def body(y_ref, t_ref, dy_ref, loss_ref):
        err = y_ref[...] - t_ref[...]
        dy_ref[...] = err * (1.0 / d)

        @pl.when(pl.program_id(0) == 0)
        def _():
            loss_ref[...] = jnp.zeros_like(loss_ref)

        per_tok = jnp.mean(err * err, axis=-1, keepdims=True)
        loss_ref[...] += 0.5 * jnp.sum(per_tok, axis=0, keepdims=True)

    row = pl.BlockSpec((tr, d), lambda i: (i, 0))
    return pl.pallas_call(
        body,
        out_shape=(jax.ShapeDtypeStruct((s, d), F32), jax.ShapeDtypeStruct((1, LANES), F32)),
        grid=(s // tr,),
        in_specs=[row, row],
        out_specs=(row, pl.BlockSpec((1, LANES), lambda i: (0, 0))),
        compiler_params=_params("arbitrary"),
        name=name,
    )(y, target)


def _adamw(w, grads, m, v, *, name, my_chip=None, deps=()):
    nl, r, c = w.shape
    pieces = my_chip is not None
    cp = (grads[0][0] if pieces else grads[0]).shape[-1]
    tr = _pick(r, min(256, max(16, 262144 // cp)), 8)
    per = 4 if pieces else 1

    def body(chip_ref, *refs):
        w_ref, m_ref, v_ref = refs[:3]
        g_refs = refs[3:3 + per * nl]
        g_out, d_ref, m2_ref, v2_ref = refs[3 + per * nl + len(deps):]

        def grad(n):
            if not pieces:
                return g_refs[n][:, :c]
            own, r0, r1, r2 = (t[0, :, :c].astype(F32) for t in g_refs[4 * n:4 * n + 4])
            return ((own + r0) + r1) + r2

        layer = pl.program_id(0)
        gv = grad(0)
        for n in range(1, nl):
            gv = jnp.where(layer == n, grad(n), gv)
        m2 = ADAM_B1 * m_ref[0] + (1.0 - ADAM_B1) * gv
        v2 = ADAM_B2 * v_ref[0] + (1.0 - ADAM_B2) * (gv * gv)
        m_hat = m2 / (1.0 - ADAM_B1 ** ADAM_STEP)
        v_hat = v2 / (1.0 - ADAM_B2 ** ADAM_STEP)
        g_out[0] = gv
        d_ref[0] = -ADAM_LR * (m_hat / (jnp.sqrt(v_hat) + ADAM_EPS) + ADAM_WD * w_ref[0])
        m2_ref[0] = m2
        v2_ref[0] = v2

    blk = pl.BlockSpec((1, tr, c), lambda l, i, chip_ref: (l, i, 0))
    if pieces:
        g_specs = [pl.BlockSpec((1, tr, cp), lambda l, i, chip_ref: (chip_ref[0], i, 0))]
        g_specs += [pl.BlockSpec((1, tr, cp), lambda l, i, chip_ref, k=k: (k, i, 0)) for k in range(3)]
        g_ins = [t for partial, recv in grads for t in (partial, recv, recv, recv)]
        chip = my_chip.reshape(1).astype(jnp.int32)
    else:
        g_specs, g_ins, chip = [pl.BlockSpec((tr, cp), lambda l, i, chip_ref: (i, 0))], list(grads), jnp.zeros((1,), jnp.int32)
    shp = jax.ShapeDtypeStruct((nl, r, c), F32)
    grid_spec = pltpu.PrefetchScalarGridSpec(
        num_scalar_prefetch=1,
        grid=(nl, r // tr),
        in_specs=[blk, blk, blk] + g_specs * nl + [ANY] * len(deps),
        out_specs=(blk, blk, blk, blk),
    )
    return pl.pallas_call(
        body,
        out_shape=(shp, shp, shp, shp),
        grid_spec=grid_spec,
        compiler_params=_params("parallel", "parallel"),
        name=name,
    )(chip, w, m, v, *g_ins, *deps)


def _pair_sum(mine, recv, my_c, *, name):
    _, r, c = mine.shape
    tr = _pick(r, 512, 16)

    def body(c_ref, a_ref, b_ref, o_ref):
        o_ref[0] = (a_ref[0].astype(F32) + b_ref[0].astype(F32)).astype(o_ref.dtype)

    grid_spec = pltpu.PrefetchScalarGridSpec(
        num_scalar_prefetch=1,
        grid=(4, r // tr),
        in_specs=[pl.BlockSpec((1, tr, c), lambda kk, i, c_ref: (2 * kk + c_ref[0], i, 0)),
                  pl.BlockSpec((1, tr, c), lambda kk, i, c_ref: (kk, i, 0))],
        out_specs=pl.BlockSpec((1, tr, c), lambda kk, i, c_ref: (kk, i, 0)),
    )
    return pl.pallas_call(
        body,
        out_shape=jax.ShapeDtypeStruct((4, r, c), mine.dtype),
        grid_spec=grid_spec,
        compiler_params=_params("parallel", "parallel"),
        name=name,
    )(my_c.reshape(1).astype(jnp.int32), mine, recv)


def _sum_devices(stack, *, name):
    n, r, c = stack.shape

    def body(s_ref, o_ref):
        acc = s_ref[0]
        for dev in range(1, n):
            acc = acc + s_ref[dev]
        o_ref[...] = acc

    return pl.pallas_call(
        body,
        out_shape=jax.ShapeDtypeStruct((r, c), F32),
        in_specs=[pl.BlockSpec(memory_space=pltpu.VMEM)],
        out_specs=pl.BlockSpec(memory_space=pltpu.VMEM),
        name=name,
    )(stack)


def _mesh_pos():
    return lax.axis_index("x"), lax.axis_index("y"), lax.axis_index("c")


def _all_gather(shards, *, name, deps=()):
    n = len(shards)

    def body(*refs):
        x_refs, out_refs = refs[:n], refs[n + len(deps):2 * n + len(deps)]
        send_sems, recv_sems, local_sems = refs[2 * n + len(deps):]
        x, y, cc = _mesh_pos()
        me, sibling = (x, y, cc), (x, y, 1 - cc)
        chips = [(1 - x, y), (x, 1 - y), (1 - x, 1 - y)]

        def rows(a, px, py, pc):
            return out_refs[a].at[4 * px + 2 * py + pc]

        def copy(a, kk, block, to, src=None):
            return pltpu.make_async_remote_copy(
                src_ref=rows(a, *block) if src is None else src, dst_ref=rows(a, *block),
                send_sem=send_sems.at[7 * a + kk], recv_sem=recv_sems.at[7 * a + kk],
                device_id=to, device_id_type=MESH)

        sends, own = [], []
        for a in range(n):
            own.append(pltpu.make_async_copy(x_refs[a], rows(a, *me), local_sems.at[a]))
            own[a].start()
            first = [copy(a, 0, me, sibling, src=x_refs[a])]
            first += [copy(a, 1 + j, me, (*chip, cc), src=x_refs[a]) for j, chip in enumerate(chips)]
            for cp in first:
                cp.start()
            sends += first
        for a in range(n):
            for j, chip in enumerate(chips):
                copy(a, 1 + j, (*chip, cc), me).wait_recv()
                passed = copy(a, 4 + j, (*chip, cc), sibling)
                passed.start()
                sends.append(passed)
        for a in range(n):
            copy(a, 0, sibling, me).wait_recv()
            for j, chip in enumerate(chips):
                copy(a, 4 + j, (*chip, 1 - cc), me).wait_recv()
        for cp in sends:
            cp.wait_send()
        for cp in own:
            cp.wait()

    return pl.pallas_call(
        body,
        out_shape=tuple(jax.ShapeDtypeStruct((N_DEV,) + t.shape, t.dtype) for t in shards),
        in_specs=[ANY] * (n + len(deps)),
        out_specs=tuple([ANY] * n),
        scratch_shapes=[pltpu.SemaphoreType.DMA((7 * n,)), pltpu.SemaphoreType.DMA((7 * n,)),
                        pltpu.SemaphoreType.DMA((n,))],
        name=name,
    )(*shards, *deps)


def _plan_own_blocks(n):
    def plan(refs, send_sems, recv_sems, outgoing):
        x, y, cc = _mesh_pos()
        peers = [(x, y, 1 - cc), (1 - x, y, cc), (x, 1 - y, cc), (1 - x, 1 - y, cc)]
        copies = []
        for a in range(n):
            land = refs[n + a]
            for kk, (px, py, pc) in enumerate(peers):
                block = (x, y, cc) if outgoing else (px, py, pc)
                rows = land.at[4 * block[0] + 2 * block[1] + block[2]]
                copies.append(pltpu.make_async_remote_copy(
                    src_ref=refs[a] if outgoing else rows, dst_ref=rows, send_sem=send_sems.at[4 * a + kk],
                    recv_sem=recv_sems.at[4 * a + kk], device_id=(px, py, pc), device_id_type=MESH))
        return copies

    plan.n_sems = 4 * n
    return plan


def _plan_pass_on(n):
    def plan(refs, send_sems, recv_sems, outgoing):
        x, y, cc = _mesh_pos()
        copies = []
        for a in range(n):
            for j, (px, py) in enumerate([(1 - x, y), (x, 1 - y), (1 - x, 1 - y)]):
                rows = refs[a].at[4 * px + 2 * py + (cc if outgoing else 1 - cc)]
                copies.append(pltpu.make_async_remote_copy(
                    src_ref=rows, dst_ref=rows, send_sem=send_sems.at[3 * a + j], recv_sem=recv_sems.at[3 * a + j],
                    device_id=(x, y, 1 - cc), device_id_type=MESH))
        return copies

    plan.n_sems = 3 * n
    return plan


def _plan_to_sibling(n):
    def plan(refs, send_sems, recv_sems, outgoing):
        x, y, cc = _mesh_pos()
        copies = []
        for a in range(n):
            for chip in range(4):
                dst = refs[n + a].at[chip]
                copies.append(pltpu.make_async_remote_copy(
                    src_ref=refs[a].at[2 * chip + (1 - cc)] if outgoing else dst, dst_ref=dst,
                    send_sem=send_sems.at[4 * a + chip], recv_sem=recv_sems.at[4 * a + chip],
                    device_id=(x, y, 1 - cc), device_id_type=MESH))
        return copies

    plan.n_sems = 4 * n
    return plan


def _plan_to_chips(n):
    def plan(refs, send_sems, recv_sems, outgoing):
        x, y, cc = _mesh_pos()
        copies = []
        for a in range(n):
            for j, (px, py) in enumerate([(1 - x, y), (x, 1 - y), (1 - x, 1 - y)]):
                dst = refs[n + a].at[j]
                copies.append(pltpu.make_async_remote_copy(
                    src_ref=refs[a].at[2 * px + py] if outgoing else dst, dst_ref=dst,
                    send_sem=send_sems.at[3 * a + j], recv_sem=recv_sems.at[3 * a + j],
                    device_id=(px, py, cc), device_id_type=MESH))
        return copies

    plan.n_sems = 3 * n
    return plan


def _in_hbm(t):
    return pltpu.with_memory_space_constraint(t, pltpu.HBM)


def _exchange_start(plan, bufs, after, *, name):
    nb, na = len(bufs), len(after)

    def body(*refs):
        outs = refs[nb + na:]
        for cp in plan(refs[:nb], outs[0], outs[1], True):
            cp.start()
        outs[2 + nb][...] = jnp.zeros_like(outs[2 + nb])

    res = pl.pallas_call(
        body,
        out_shape=(pltpu.SemaphoreType.DMA((plan.n_sems,)), pltpu.SemaphoreType.DMA((plan.n_sems,)),
                   *[pltpu.HBM(t.shape, t.dtype) for t in bufs], jax.ShapeDtypeStruct((8, LANES), F32)),
        in_specs=[HBM_SPEC] * nb + [ANY] * na,
        out_specs=(SEM_SPEC, SEM_SPEC, *[HBM_SPEC] * nb, pl.BlockSpec(memory_space=pltpu.VMEM)),
        input_output_aliases={i: 2 + i for i in range(nb)},
        compiler_params=pltpu.CompilerParams(has_side_effects=DATAFLOW_EFFECT),
        name=name,
    )(*[_in_hbm(t) for t in bufs], *after)
    return plan, res[:2], list(res[2:2 + nb]), res[2 + nb]


def _exchange_wait(flight, after, *, name):
    plan, sems, bufs, _ = flight
    nb = len(bufs)

    def body(*refs):
        send_sems, recv_sems = refs[nb], refs[nb + 1]
        for cp in plan(refs[:nb], send_sems, recv_sems, False):
            cp.wait_recv()
        for cp in plan(refs[:nb], send_sems, recv_sems, True):
            cp.wait_send()

    res = pl.pallas_call(
        body,
        out_shape=tuple(pltpu.HBM(t.shape, t.dtype) for t in bufs),
        in_specs=[HBM_SPEC] * nb + [SEM_SPEC, SEM_SPEC] + [ANY] * len(after),
        out_specs=tuple([HBM_SPEC] * nb),
        input_output_aliases={i: i for i in range(nb)},
        compiler_params=pltpu.CompilerParams(has_side_effects=DATAFLOW_EFFECT),
        name=name,
    )(*bufs, *sems, *after)
    return list(res)


_W_GROUPS = {"even": ("even_w_in", "even_w_out"), "mlp0": ("mlp_w1_0", "mlp_w2_0"),
             "odd": ("odd_w_in", "odd_w_uq", "odd_w_ukv", "odd_w_out"), "mlp1": ("mlp_w1_1", "mlp_w2_1")}


class _Exchanges:
    def __init__(self, shards, gains, n_q, n_kv):
        self.n_q, self.n_kv = n_q, n_kv
        self.mx, self.my, self.mc = _mesh_pos()
        self.dev = 4 * self.mx + 2 * self.my + self.mc
        self.shards = shards
        self.gains = gains
        self.flights, self.gathered, self.grad_blocks, self.grad_names, self.reduced = {}, {}, {}, {}, {}

    def start(self):
        names = _W_GROUPS["even"]
        got = _all_gather([self.shards[n] for n in names] + [self.gains], name="comm_even_gather")
        self.gathered.update(zip(names, got[:-1]))
        self.all_gains = got[-1][:, 0]
        return self._w_begin("mlp0", [got[0]])

    def _w_begin(self, group, after):
        srcs = [self.shards[n] for n in _W_GROUPS[group]]
        lands = [lax.dynamic_update_slice(lax.empty((N_DEV,) + t.shape, t.dtype), t[None], (self.dev, 0, 0)) for t in srcs]
        self.flights[group] = _exchange_start(_plan_own_blocks(len(srcs)), srcs + lands, after, name=f"comm_{group}_own_start")
        return [self.flights[group][3]]

    def _w_turn(self, group, after):
        bufs = _exchange_wait(self.flights[group], after, name=f"comm_{group}_own_wait")
        n = len(bufs) // 2
        self.flights[group] = _exchange_start(_plan_pass_on(n), bufs[n:], [], name=f"comm_{group}_pass_start")
        return [self.flights[group][3]]

    def _w_end(self, group, after):
        self.gathered.update(zip(_W_GROUPS[group], _exchange_wait(self.flights.pop(group), after, name=f"comm_{group}_pass_wait")))

    def weights(self, group):
        return {n: self.gathered[n] for n in _W_GROUPS[group]}

    def norm_gains(self):
        return (self.all_gains[:, :self.n_q].reshape(-1), self.all_gains[:, self.n_q:self.n_q + self.n_kv].reshape(-1))

    def grads(self, group, named_blocks):
        self.grad_names[group] = [n for n, _ in named_blocks]
        self.grad_blocks[group] = [t for _, t in named_blocks]

    def _g_begin(self, group, after):
        blocks = self.grad_blocks[group]
        lands = [lax.empty((4,) + t.shape[1:], t.dtype) for t in blocks]
        self.flights[group] = _exchange_start(_plan_to_sibling(len(blocks)), blocks + lands, after, name=f"comm_{group}_sib_start")
        return [self.flights[group][3]]

    def _g_turn(self, group, after):
        bufs = _exchange_wait(self.flights[group], after, name=f"comm_{group}_sib_wait")
        n = len(bufs) // 2
        partial = [_pair_sum(a, b, self.mc, name=f"pair_sum_{nm}") for nm, a, b in zip(self.grad_names[group], bufs[:n], bufs[n:])]
        lands = [lax.empty((3,) + t.shape[1:], t.dtype) for t in partial]
        self.flights[group] = _exchange_start(_plan_to_chips(n), partial + lands, [], name=f"comm_{group}_chips_start")
        return [self.flights[group][3]]

    def _g_end(self, group, after):
        bufs = _exchange_wait(self.flights.pop(group), after, name=f"comm_{group}_chips_wait")
        n = len(bufs) // 2
        self.reduced.update(zip(self.grad_names[group], zip(bufs[:n], bufs[n:])))

    _SCHEDULE = {
        "even_out": (("w_turn", "mlp0"), ("w_begin", "odd")),
        "ln1_l0": (("w_end", "mlp0"),),
        "ln2_l0": (("w_turn", "odd"), ("w_begin", "mlp1"), ("w_end", "odd")),
        "odd_out": (("w_turn", "mlp1"),),
        "ln1_l1": (("w_end", "mlp1"),),
        "dw_l1": (("g_begin", "mlp1"),),
        "ln1_bwd_l1": (("g_turn", "mlp1"),),
        "odd_in_dw": (("g_begin", "odd"),),
        "mlp2_dx_l0": (("g_end", "mlp1"), ("g_turn", "odd")),
        "dw_l0": (("g_begin", "mlp0"),),
        "ln1_bwd_l0": (("g_end", "odd"), ("g_turn", "mlp0")),
        "even_in_dw": (("g_begin", "even"),),
        "even_in_dx": (("g_end", "mlp0"), ("g_turn", "even")),
        "finish": (("g_end", "even"),),
    }

    def sync(self, tag, after):
        latest, started = list(after), []
        for what, group in self._SCHEDULE[tag]:
            out = getattr(self, "_" + what)(group, latest)
            if out:
                latest = started = out
        return started


def _alibi(n):
    return 2.0 ** (-8.0 * np.arange(1, n + 1, dtype=np.float32) / n)


def _heads(t, n):
    s = t.shape[0]
    return t.reshape(s, n, t.shape[1] // n).transpose(1, 0, 2)


def _unheads(t):
    n, s, dh = t.shape
    return t.transpose(1, 0, 2).reshape(s, n * dh)


def _to_strided(t, d):
    s, x = t.shape
    return t if d == 1 else t.reshape(s // d, d, x).transpose(1, 0, 2).reshape(s, x)


def _from_strided(t, d):
    s, x = t.shape
    return t if d == 1 else t.reshape(d, s // d, x).transpose(1, 0, 2).reshape(s, x)


def _true_columns(t, c, n_pad):
    r = t.shape[1]
    w = t[:, :, :c].transpose(1, 0, 2).reshape(r, N_DEV * c)
    return jnp.pad(w, ((0, 0), (0, n_pad - N_DEV * c)))


def _column_blocks(t, c, cp):
    r = t.shape[0]
    b = t[:, :N_DEV * c].reshape(r, N_DEV, c).transpose(1, 0, 2)
    return jnp.pad(b, ((0, 0), (0, 0), (0, cp - c)))


def _stack_rows(t, nq):
    return t.reshape(nq, BLK, A_KV_HEADS, A_GROUP, HEAD_DIM).transpose(2, 0, 3, 1, 4).reshape(
        A_KV_HEADS, 1, nq * A_GROUP * BLK, HEAD_DIM)


def _unstack_rows(t, nq):
    return t.reshape(A_KV_HEADS, nq, A_GROUP, BLK, HEAD_DIM).transpose(1, 3, 0, 2, 4).reshape(
        nq * BLK, A_Q_W)


def _lead_block(t):
    return jnp.pad(t, ((0, 0), (BLK, 0), (0, 0)))


def _columns(t):
    return t.transpose(1, 0, 2).reshape(t.shape[1], -1)


def _rows(t):
    return t.reshape(-1, t.shape[2])


def _by_column_block(t):
    return t.reshape(t.shape[0], N_DEV, -1).transpose(1, 0, 2)


def _local_step(x0, target, comm, sinks, ln1_g, ln1_b, ln2_g, ln2_b):
    s, d = x0.shape
    scale_h = 1.0 / math.sqrt(HEAD_DIM)
    scale_d = 1.0 / math.sqrt(D_NOPE + D_ROPE)
    nq = s // BLK
    even_c, odd_c = EVEN_IN // N_DEV, ODD_IN // N_DEV
    bf = lambda t: t.astype(MXU_DTYPE)
    blocks = dict(b_blocks=True)

    tok = comm.start()
    w_even = comm.weights("even")
    even_cp, even_n = w_even["even_w_in"].shape[2], -(-EVEN_IN // 1024) * 1024
    w_even_in, w_even_out = _true_columns(w_even["even_w_in"], even_c, even_n), _columns(w_even["even_w_out"])
    x0b = bf(x0)
    h_e = _mm(x0b, w_even_in, out_dtypes=(MXU_DTYPE,), name="even_in_fwd", deps=tok)
    qa = _stack_rows(h_e[:, :A_Q_W], nq)
    ka = _lead_block(_heads(h_e[:, A_Q_W:A_Q_W + A_KV_W], A_KV_HEADS))[:, None]
    va = _lead_block(_heads(h_e[:, A_Q_W + A_KV_W:A_Q_W + 2 * A_KV_W], A_KV_HEADS))[:, None]
    rows_a = A_GROUP * BLK
    slope_a = jnp.asarray(np.repeat(_alibi(A_Q_HEADS).reshape(A_KV_HEADS, A_GROUP), BLK, axis=1).reshape(
        A_KV_HEADS, 1, rows_a, 1))
    sink_a = jnp.broadcast_to(sinks.reshape(A_KV_HEADS, A_GROUP, 1), (A_KV_HEADS, A_GROUP, BLK)).reshape(
        A_KV_HEADS, 1, rows_a, 1)
    a_cfg = dict(scale=scale_h, n_back=A_WINDOW - 1, bps=nq)
    oa, lse_a = _band_fwd(qa, ka, va, slope_a, sink_a, name="swa_fwd", **a_cfg)
    b_slab, b_cfg, b_out, b_lse = [], [], [], []
    base = A_Q_W + 2 * A_KV_W
    for gi, (window, dil) in enumerate(B_PATTERNS):
        slab = _to_strided(h_e[:, base + gi * 3 * B_W: base + (gi + 1) * 3 * B_W], dil)
        cfg = dict(slopes=_alibi(B_HEADS) * dil, scale=scale_h, n_back=window // dil, bps=nq // dil)
        o, lse = _dil_fwd(slab, name=f"dil{gi}_fwd", **cfg)
        both = _from_strided(jnp.concatenate([o, lse], axis=1), dil)
        b_slab.append(slab)
        b_cfg.append(cfg)
        b_out.append(both[:, :B_W])
        b_lse.append(both[:, B_W:])
    ob, lse_b = _merge(b_out, b_lse, name="dil_merge")
    y_e = bf(jnp.concatenate([_unstack_rows(oa, nq), ob], axis=1))
    mixed = _mm(y_e, w_even_out, name="even_out_fwd")
    tok = comm.sync("even_out", [mixed])
    x0n, x0nb, xh1_0, r1_0 = _ln_fwd(x0, mixed, ln1_g[0], ln1_b[0], name="ln1_fwd_l0", deps=tok)
    comm.sync("ln1_l0", [x0nb])
    w_mlp0 = comm.weights("mlp0")
    w1_0, w2_0 = w_mlp0["mlp_w1_0"], _rows(w_mlp0["mlp_w2_0"])
    act0, hid0 = _mm(x0nb, w1_0, out_dtypes=(MXU_DTYPE, MXU_DTYPE), epilogue=_relu_sq, name="mlp1_fwd_l0", **blocks)
    mlp = _mm(hid0, w2_0, name="mlp2_fwd_l0")
    x1, x1b, xh2_0, r2_0 = _ln_fwd(x0n, mlp, ln2_g[0], ln2_b[0], name="ln2_fwd_l0")

    tok = comm.sync("ln2_l0", [x1b])
    w_odd = comm.weights("odd")
    odd_cp, odd_n = w_odd["odd_w_in"].shape[2], -(-ODD_IN // 1024) * 1024
    w_odd_in, w_uq, w_ukv, w_odd_out = (_true_columns(w_odd["odd_w_in"], odd_c, odd_n), _columns(w_odd["odd_w_uq"]),
                                        _columns(w_odd["odd_w_ukv"]), _rows(w_odd["odd_w_out"]))
    gq, gkv = comm.norm_gains()
    h_o = _mm(x1b, w_odd_in, name="odd_in_fwd", deps=tok)
    qkv_c = bf(h_o[:, :3 * C_W])
    oc, sb_total = _sb_fwd(qkv_c, heads=C_HEADS, scale=scale_h, name="sb_fwd")
    o_cq, o_ckv, o_kr = 3 * C_W, 3 * C_W + D_Q_RANK, 3 * C_W + D_Q_RANK + D_KV_RANK
    cq, ckv, kr = h_o[:, o_cq:o_ckv], h_o[:, o_ckv:o_kr], h_o[:, o_kr:o_kr + D_ROPE]
    ncq, rq = _rms_fwd(cq, gq, name="rms_q_fwd")
    nckv, rkv = _rms_fwd(ckv, gkv, name="rms_kv_fwd")
    lane_pad = LANES - D_NOPE - D_ROPE
    w_uq = jnp.pad(w_uq.reshape(D_Q_RANK, D_HEADS, D_NOPE + D_ROPE), ((0, 0), (0, 0), (0, lane_pad))).reshape(
        D_Q_RANK, D_HEADS * LANES)
    qd = _mm(ncq, w_uq, name="uq_fwd")
    kvd = _mm(nckv, w_ukv, out_dtypes=(MXU_DTYPE,), name="ukv_fwd")
    rope_t = _rope_tables(s, inverse=False)
    krp = _rope(jnp.pad(kr, ((0, 0), (D_NOPE, lane_pad)))[None], rope_t, out_dtype=MXU_DTYPE, name="rope_k_fwd")[0]
    od, lse_d = _mla_fwd(qd, kvd, krp, rope_t, scale=scale_d, name="mla_fwd")
    y_o = bf(jnp.concatenate([oc, od], axis=1))
    mixed = _mm(y_o, w_odd_out, name="odd_out_fwd")
    tok = comm.sync("odd_out", [mixed])
    x1n, x1nb, xh1_1, r1_1 = _ln_fwd(x1, mixed, ln1_g[1], ln1_b[1], name="ln1_fwd_l1", deps=tok)
    comm.sync("ln1_l1", [x1nb])
    w_mlp1 = comm.weights("mlp1")
    w1_1, w2_1 = w_mlp1["mlp_w1_1"], _rows(w_mlp1["mlp_w2_1"])
    act1, hid1 = _mm(x1nb, w1_1, out_dtypes=(MXU_DTYPE, MXU_DTYPE), epilogue=_relu_sq, name="mlp1_fwd_l1", **blocks)
    mlp = _mm(hid1, w2_1, name="mlp2_fwd_l1")
    y, _, xh2_1, r2_1 = _ln_fwd(x1n, mlp, ln2_g[1], ln2_b[1], name="ln2_fwd_l1")
    dy, loss_vec = _loss_head(y, target, name="loss_head")

    def mlp_block_bwd(g_out, layer, w1, w2, xh2, r2, xh1, r1, act, hid, xnb):
        du2, du2b, dg2, db2 = _ln_bwd(g_out, xh2, r2, ln2_g[layer], name=f"ln2_bwd_l{layer}")
        dpre = _mm(du2b, w2, nt=True, out_dtypes=(MXU_DTYPE,), epilogue=_relu_sq_grad, extra=act, name=f"mlp2_dx_l{layer}")
        tok = comm.sync("mlp2_dx_l0", [dpre]) if layer == 0 else []
        dw2 = _mm(hid, du2b, ta=True, out_dtypes=(BF16,), name=f"mlp2_dw_l{layer}", deps=tok)
        dw1 = _mm(xnb, dpre, ta=True, out_blocks=True, out_dtypes=(BF16,), name=f"mlp1_dw_l{layer}")
        comm.grads(f"mlp{layer}", [(f"mlp_w1_{layer}", dw1), (f"mlp_w2_{layer}", dw2.reshape(N_DEV, -1, d))])
        tok = comm.sync(f"dw_l{layer}", [dw1])
        dxn = _mm(dpre, w1, nt=True, epilogue=_add_alpha, extra=du2, name=f"mlp1_dx_l{layer}", deps=tok, **blocks)
        du1, du1b, dg1, db1 = _ln_bwd(dxn, xh1, r1, ln1_g[layer], name=f"ln1_bwd_l{layer}")
        return du1, du1b, comm.sync(f"ln1_bwd_l{layer}", [du1b]), (dg1, db1, dg2, db2)

    du1, du1b, tok, ln_1 = mlp_block_bwd(dy, 1, w1_1, w2_1, xh2_1, r2_1, xh1_1, r1_1, act1, hid1, x1nb)
    d_odd_out = _mm(y_o, du1b, ta=True, out_dtypes=(BF16,), name="odd_out_dw").reshape(N_DEV, -1, d)
    dy_o = _mm(du1b, w_odd_out, nt=True, name="odd_out_dx", deps=tok)
    dqc, dkc, dvc = _sb_bwd(qkv_c, dy_o, sb_total, heads=C_HEADS, scale=scale_h, name="sb_bwd")
    dqd, dkvd, dkr_pairs = _mla_bwd(qd, kvd, krp, rope_t, dy_o, od, lse_d, do_block0=C_W // LANES, scale=scale_d,
                                    name="mla_bwd")
    _, dkr_sum = _rope(dkr_pairs, _rope_tables(s, inverse=True), out_dtype=F32, head_sum=True, name="rope_k_bwd")
    dqd, dkvd = bf(dqd), bf(dkvd)
    d_uq = _mm(ncq, dqd, ta=True, out_dtypes=(BF16,), name="uq_dw").reshape(D_Q_RANK, D_HEADS, LANES)[:, :, :D_NOPE + D_ROPE].reshape(
        D_Q_RANK, D_HEADS * (D_NOPE + D_ROPE))
    dncq = _mm(dqd, w_uq, nt=True, name="uq_dx")
    d_ukv = _mm(nckv, dkvd, ta=True, out_dtypes=(BF16,), name="ukv_dw")
    dnckv = _mm(dkvd, w_ukv, nt=True, name="ukv_dx")
    dcq, dgq = _rms_bwd(dncq, cq, rq, gq, name="rms_q_bwd")
    dckv, dgkv = _rms_bwd(dnckv, ckv, rkv, gkv, name="rms_kv_bwd")
    dh_o = bf(jnp.concatenate(
        [dqc, dkc, dvc, dcq, dckv, dkr_sum[:, D_NOPE:D_NOPE + D_ROPE], jnp.zeros((s, odd_n - ODD_IN), F32)], axis=1))
    d_odd_in = _column_blocks(_mm(x1b, dh_o, ta=True, out_dtypes=(BF16,), name="odd_in_dw"), odd_c, odd_cp)
    comm.grads("odd", [("odd_w_in", d_odd_in), ("odd_w_uq", _by_column_block(d_uq)),
                       ("odd_w_ukv", _by_column_block(d_ukv)), ("odd_w_out", d_odd_out)])
    tok = comm.sync("odd_in_dw", [d_odd_in])
    dx1 = _mm(dh_o, w_odd_in, nt=True, epilogue=_add_alpha, extra=du1, name="odd_in_dx", deps=tok)

    du1, du1b, tok, ln_0 = mlp_block_bwd(dx1, 0, w1_0, w2_0, xh2_0, r2_0, xh1_0, r1_0, act0, hid0, x0nb)
    d_even_out = _mm(y_e, du1b, ta=True, out_dtypes=(BF16,), name="even_out_dw")
    dy_e = _mm(du1b, w_even_out, nt=True, name="even_out_dx", deps=tok)
    doa = _stack_rows(dy_e[:, :A_Q_W], nq)
    dqa, dka, dva, dsink = _band_bwd(qa, ka, va, doa, oa, lse_a, slope_a, sink_a, name="swa_bwd", **a_cfg)
    pieces = [_unstack_rows(dqa, nq), _unheads(dka[:, 0, BLK:]), _unheads(dva[:, 0, BLK:])]
    pack = jnp.concatenate([dy_e[:, A_Q_W:], ob, lse_b], axis=1)
    for gi, (_, dil) in enumerate(B_PATTERNS):
        grads = _dil_bwd(b_slab[gi], _to_strided(pack, dil), name=f"dil{gi}_bwd", **b_cfg[gi])
        pieces.append(_from_strided(jnp.concatenate(grads, axis=1), dil))
    dh_e = bf(jnp.concatenate(pieces + [jnp.zeros((s, even_n - EVEN_IN), F32)], axis=1))
    d_even_in = _column_blocks(_mm(x0b, dh_e, ta=True, out_dtypes=(BF16,), name="even_in_dw"), even_c, even_cp)
    comm.grads("even", [("even_w_in", d_even_in), ("even_w_out", _by_column_block(d_even_out))])
    tok = comm.sync("even_in_dw", [d_even_in])
    grad_x = _mm(dh_e, w_even_in, nt=True, epilogue=_add_alpha, extra=du1, name="even_in_dx", deps=tok)
    tok = comm.sync("even_in_dx", [grad_x])

    ln = [jnp.concatenate([a, b], axis=0) for a, b in zip(ln_0, ln_1)]
    small = {"ln": ln, "sinks": dsink[:, :, 0].reshape(-1), "gq": dgq[0], "gkv": dgkv[0], "loss": loss_vec[0, :1]}
    return grad_x, small, tok


def kernel(x, even_w_in, even_sinks, even_w_out, odd_w_in, odd_q_norm_g, odd_kv_norm_g, odd_w_uq, odd_w_ukv, odd_w_out, ln1_g, ln1_b, mlp_w1, mlp_w2, ln2_g, ln2_b, loss_target, m_even_w_in, m_even_sinks, m_even_w_out, m_odd_w_in, m_odd_q_norm_g, m_odd_kv_norm_g, m_odd_w_uq, m_odd_w_ukv, m_odd_w_out, m_ln1_g, m_ln1_b, m_mlp_w1, m_mlp_w2, m_ln2_g, m_ln2_b, v_even_w_in, v_even_sinks, v_even_w_out, v_odd_w_in, v_odd_q_norm_g, v_odd_kv_norm_g, v_odd_w_uq, v_odd_w_ukv, v_odd_w_out, v_ln1_g, v_ln1_b, v_mlp_w1, v_mlp_w2, v_ln2_g, v_ln2_b):
    weights = dict(even_w_in=even_w_in, even_sinks=even_sinks, even_w_out=even_w_out, odd_w_in=odd_w_in,
                   odd_q_norm_g=odd_q_norm_g, odd_kv_norm_g=odd_kv_norm_g, odd_w_uq=odd_w_uq, odd_w_ukv=odd_w_ukv,
                   odd_w_out=odd_w_out, ln1_g=ln1_g, ln1_b=ln1_b, mlp_w1=mlp_w1, mlp_w2=mlp_w2, ln2_g=ln2_g, ln2_b=ln2_b)
    mom_m = dict(even_w_in=m_even_w_in, even_sinks=m_even_sinks, even_w_out=m_even_w_out, odd_w_in=m_odd_w_in,
                 odd_q_norm_g=m_odd_q_norm_g, odd_kv_norm_g=m_odd_kv_norm_g, odd_w_uq=m_odd_w_uq, odd_w_ukv=m_odd_w_ukv,
                 odd_w_out=m_odd_w_out, ln1_g=m_ln1_g, ln1_b=m_ln1_b, mlp_w1=m_mlp_w1, mlp_w2=m_mlp_w2, ln2_g=m_ln2_g, ln2_b=m_ln2_b)
    mom_v = dict(even_w_in=v_even_w_in, even_sinks=v_even_sinks, even_w_out=v_even_w_out, odd_w_in=v_odd_w_in,
                 odd_q_norm_g=v_odd_q_norm_g, odd_kv_norm_g=v_odd_kv_norm_g, odd_w_uq=v_odd_w_uq, odd_w_ukv=v_odd_w_ukv,
                 odd_w_out=v_odd_w_out, ln1_g=v_ln1_g, ln1_b=v_ln1_b, mlp_w1=v_mlp_w1, mlp_w2=v_mlp_w2, ln2_g=v_ln2_g, ln2_b=v_ln2_b)
    order = list(weights)
    n_q, n_kv = odd_q_norm_g.shape[1], odd_kv_norm_g.shape[1]

    def lane_padded(t):
        return jnp.pad(t, ((0, 0), (0, _lane_pad(t.shape[1]) - t.shape[1]))).astype(BF16)

    shards = {"even_w_in": lane_padded(even_w_in[0]), "even_w_out": even_w_out[0].astype(BF16),
              "mlp_w1_0": mlp_w1[0].astype(BF16), "mlp_w2_0": mlp_w2[0].astype(BF16),
              "odd_w_in": lane_padded(odd_w_in[0]), "odd_w_uq": odd_w_uq[0].astype(BF16),
              "odd_w_ukv": odd_w_ukv[0].astype(BF16), "odd_w_out": odd_w_out[0].astype(BF16),
              "mlp_w1_1": mlp_w1[1].astype(BF16), "mlp_w2_1": mlp_w2[1].astype(BF16)}
    gains = jnp.concatenate([odd_q_norm_g, odd_kv_norm_g, jnp.zeros((1, LANES - n_q - n_kv), F32)], axis=1)
    comm = _Exchanges(shards, gains, n_q, n_kv)
    dev = comm.dev

    grad_x, small, last_started = _local_step(x[0], loss_target[0], comm, even_sinks[0], ln1_g, ln1_b, ln2_g, ln2_b)

    grads, delta, new_m, new_v = {}, {}, {}, {}

    def update(n):
        g_list = [comm.reduced[f"{n}_0"], comm.reduced[f"{n}_1"]] if n.startswith("mlp") else [comm.reduced[n]]
        grads[n], delta[n], new_m[n], new_v[n] = _adamw(weights[n], g_list, mom_m[n], mom_v[n], name=f"adamw_{n}",
                                                        my_chip=2 * comm.mx + comm.my, deps=last_started)

    early = ("mlp_w1", "mlp_w2", "odd_w_in", "odd_w_uq", "odd_w_ukv", "odd_w_out")
    for n in early:
        update(n)
    comm.sync("finish", [new_v[n] for n in early])
    update("even_w_in")
    update("even_w_out")

    small_parts = [t.reshape(-1) for t in small["ln"]] + [small["sinks"], small["gq"], small["gkv"], small["loss"]]
    small_sizes = [p.shape[0] for p in small_parts]
    n_small = sum(small_sizes)
    small_rows = -(-n_small // (8 * LANES)) * 8
    small_flat = jnp.concatenate(small_parts + [jnp.zeros((small_rows * LANES - n_small,), F32)]).reshape(small_rows, LANES)
    (small_all,) = _all_gather([small_flat], name="comm_small_gather", deps=[grads["even_w_in"]])
    totals = _sum_devices(small_all, name="small_sum").reshape(-1)
    tot, off = [], 0
    for size in small_sizes:
        tot.append(totals[off:off + size])
        off += size
    for i, n in enumerate(("ln1_g", "ln1_b", "ln2_g", "ln2_b")):
        grads[n] = tot[i].reshape(weights[n].shape)
    grads["even_sinks"] = tot[4].reshape(even_sinks.shape)
    grads["odd_q_norm_g"] = lax.dynamic_slice(tot[5], (dev * n_q,), (n_q,)).reshape(odd_q_norm_g.shape)
    grads["odd_kv_norm_g"] = lax.dynamic_slice(tot[6], (dev * n_kv,), (n_kv,)).reshape(odd_kv_norm_g.shape)
    loss = tot[7][0]

    small_names = [n for n in order if n not in early + ("even_w_in", "even_w_out")]
    n_sm = sum(weights[n].size for n in small_names)
    sm_rows = -(-n_sm // (8 * LANES)) * 8

    def pack_small(group):
        flat = [group[n].reshape(-1) for n in small_names]
        return jnp.concatenate(flat + [jnp.zeros((sm_rows * LANES - n_sm,), F32)]).reshape(1, sm_rows, LANES)

    res = _adamw(pack_small(weights), [pack_small(grads)[0]], pack_small(mom_m), pack_small(mom_v), name="adamw_small")
    off = 0
    for n in small_names:
        size = weights[n].size
        delta[n], new_m[n], new_v[n] = (t.reshape(-1)[off:off + size].reshape(weights[n].shape) for t in res[1:])
        off += size

    return (loss, grad_x[None], *[grads[n] for n in order], *[delta[n] for n in order],
            *[new_m[n] for n in order], *[new_v[n] for n in order])
```

```python
import math

import jax
import jax.numpy as jnp
import numpy as np
from jax import lax
from jax.experimental import pallas as pl
from jax.experimental.pallas import tpu as pltpu

F32 = jnp.float32
BF16 = jnp.bfloat16
MXU_DTYPE = BF16

HEAD_DIM = 64
A_Q_HEADS, A_KV_HEADS, A_WINDOW = 16, 2, 128
A_GROUP = A_Q_HEADS // A_KV_HEADS
B_HEADS = 8
B_PATTERNS = ((128, 1), (512, 4), (2048, 16))
C_HEADS = 16
D_HEADS, D_Q_RANK, D_KV_RANK, D_NOPE, D_ROPE, D_V = 16, 512, 256, 64, 32, 64
ROPE_BASE = 10000.0
LN_EPS, RMS_EPS = 1e-5, 1e-6
DEPTH = 2
ALPHA = (2 * DEPTH) ** 0.25
A_Q_W, A_KV_W, B_W = A_Q_HEADS * HEAD_DIM, A_KV_HEADS * HEAD_DIM, B_HEADS * HEAD_DIM
EVEN_IN = A_Q_W + 2 * A_KV_W + 3 * B_W * len(B_PATTERNS)
C_W = C_HEADS * HEAD_DIM
ODD_IN = 3 * C_W + D_Q_RANK + D_KV_RANK + D_ROPE
ADAM_LR, ADAM_B1, ADAM_B2, ADAM_EPS, ADAM_WD, ADAM_STEP = 0.001, 0.9, 0.999, 1e-08, 0.01, 10

N_DEV = 8
LANES = 128
BLK = 128
CAUSAL_TILE = 512
CUM_CHUNK = 256
NEG = -1e30
VMEM_LIMIT = 48 * 1024 * 1024

NN = ((1,), (0,))
NT = ((1,), (1,))
TN = ((0,), (0,))
MESH = pl.DeviceIdType.MESH
ANY = pl.BlockSpec(memory_space=pl.ANY)
HBM_SPEC = pl.BlockSpec(memory_space=pltpu.HBM)
SEM_SPEC = pl.BlockSpec(memory_space=pltpu.SEMAPHORE)
DATAFLOW_EFFECT = pltpu.SideEffectType.DATAFLOW_SIDE_EFFECTING


def _dot(a, b, dims):
    return lax.dot_general(a, b, (dims, ((), ())), preferred_element_type=F32)


def _bdot(a, b, dims):
    return jnp.stack([_dot(a[n], b[n], dims) for n in range(a.shape[0])])


def _params(*sem):
    return pltpu.CompilerParams(dimension_semantics=tuple(sem), vmem_limit_bytes=VMEM_LIMIT)


def _pick(n, cap, mult=LANES):
    if n <= cap:
        return n
    for t in range(cap - cap % mult, 0, -mult):
        if n % t == 0:
            return t
    raise ValueError(f"no tile for {n}")


def _lane_pad(c):
    return -(-c // LANES) * LANES


def _mm(a, b, *, name, nt=False, ta=False, b_blocks=False, out_blocks=False, out_dtypes=(F32,), epilogue=None, extra=None, deps=()):
    m, k = a.shape[::-1] if ta else a.shape
    if b_blocks:
        nb, kin, c = b.shape
        n = kin if nt else nb * c
        k_full = nb * c if nt else kin
    else:
        n, k_full = (b.shape if nt else b.shape[::-1])
    assert k == k_full, (a.shape, b.shape, nt, b_blocks)
    tm = _pick(m, 1024, 8)
    if b_blocks and not nt:
        tn, tk = c, _pick(k, 3072)
    elif b_blocks:
        per_step = max(g for g in (1, 2, 4, 8) if g * c <= 2048)
        tn, tk = _pick(n, 1024), per_step * c
    elif out_blocks:
        tn, tk = n // N_DEV, _pick(k, 3072)
    else:
        tn, tk = _pick(n, 512), _pick(k, 3072)
        if k > tk:
            tm, tk = _pick(m, 512, 8), k
    nk = k // tk
    n_out = len(out_dtypes)

    def body(*refs):
        a_ref, b_ref = refs[0], refs[1]
        e_ref = refs[2] if extra is not None else None
        first_out = 2 + (extra is not None) + len(deps)
        out_refs = refs[first_out:first_out + n_out]

        def finish(acc):
            e = None if e_ref is None else e_ref[...]
            outs = (acc,) if epilogue is None else epilogue(acc, e)
            for r, o in zip(out_refs, outs):
                r[...] = o.astype(r.dtype).reshape(r.shape)

        if b_blocks and nt:
            part = _dot(a_ref[:, :c], b_ref[0], NT)
            for blk in range(1, per_step):
                part += _dot(a_ref[:, blk * c:(blk + 1) * c], b_ref[blk], NT)
        elif ta:
            part = _dot(a_ref[...], b_ref[...], TN)
        else:
            part = _dot(a_ref[...], b_ref[0] if b_blocks else b_ref[...], NT if nt else NN)
        if nk == 1:
            finish(part)
        else:
            acc_ref = refs[first_out + n_out]
            kk = pl.program_id(2)

            @pl.when(kk == 0)
            def _():
                acc_ref[...] = part

            @pl.when(kk > 0)
            def _():
                acc_ref[...] += part

            @pl.when(kk == nk - 1)
            def _():
                finish(acc_ref[...])

    if b_blocks and not nt:
        b_spec = pl.BlockSpec((1, tk, tn), lambda i, j, kk: (j, kk, 0))
    elif b_blocks:
        b_spec = pl.BlockSpec((per_step, tn, c), lambda i, j, kk: (kk, j, 0))
    elif nt:
        b_spec = pl.BlockSpec((tn, tk), lambda i, j, kk: (j, kk))
    else:
        b_spec = pl.BlockSpec((tk, tn), lambda i, j, kk: (kk, j))
    a_spec = pl.BlockSpec((tk, tm), lambda i, j, kk: (kk, i)) if ta else pl.BlockSpec((tm, tk), lambda i, j, kk: (i, kk))
    in_specs = [a_spec, b_spec]
    ins = [a.astype(MXU_DTYPE), b.astype(MXU_DTYPE)]
    if extra is not None:
        in_specs.append(pl.BlockSpec((tm, tn), lambda i, j, kk: (i, j)))
        ins.append(extra)
    in_specs += [ANY] * len(deps)
    ins += list(deps)
    if out_blocks:
        out_shape = tuple(jax.ShapeDtypeStruct((N_DEV, m, tn), d) for d in out_dtypes)
        out_specs = tuple(pl.BlockSpec((1, tm, tn), lambda i, j, kk: (j, i, 0)) for _ in out_dtypes)
    else:
        out_shape = tuple(jax.ShapeDtypeStruct((m, n), d) for d in out_dtypes)
        out_specs = tuple(pl.BlockSpec((tm, tn), lambda i, j, kk: (i, j)) for _ in out_dtypes)
    outs = pl.pallas_call(
        body,
        out_shape=out_shape,
        grid=(m // tm, n // tn, nk),
        in_specs=in_specs,
        out_specs=out_specs,
        scratch_shapes=[pltpu.VMEM((tm, tn), F32)] if nk > 1 else [],
        compiler_params=_params("parallel", "parallel", "arbitrary"),
        name=name,
    )(*ins)
    return outs[0] if n_out == 1 else outs


def _relu_sq(acc, _):
    act = jnp.maximum(acc, 0.0)
    return act, act * act


def _relu_sq_grad(acc, act):
    return (acc * (2.0 * act.astype(F32)),)


def _add_alpha(acc, du):
    return (acc + ALPHA * du,)


def _ln_fwd(x, mixed, g, b, *, name, deps=()):
    s, d = x.shape
    tr = _pick(s, 256, 8)

    def body(x_ref, m_ref, g_ref, b_ref, *rest):
        y_ref, yb_ref, xh_ref, r_ref = rest[len(deps):]
        u = ALPHA * x_ref[...] + m_ref[...]
        mu = jnp.mean(u, axis=-1, keepdims=True)
        xc = u - mu
        var = jnp.mean(xc * xc, axis=-1, keepdims=True)
        r = lax.rsqrt(var + LN_EPS)
        xh = xc * r
        y = xh * g_ref[...] + b_ref[...]
        y_ref[...] = y
        yb_ref[...] = y.astype(MXU_DTYPE)
        xh_ref[...] = xh
        r_ref[...] = r

    row = pl.BlockSpec((tr, d), lambda i: (i, 0))
    vec = pl.BlockSpec((1, d), lambda i: (0, 0))
    return pl.pallas_call(
        body,
        out_shape=(jax.ShapeDtypeStruct((s, d), F32), jax.ShapeDtypeStruct((s, d), MXU_DTYPE),
                   jax.ShapeDtypeStruct((s, d), F32), jax.ShapeDtypeStruct((s, 1), F32)),
        grid=(s // tr,),
        in_specs=[row, row, vec, vec] + [ANY] * len(deps),
        out_specs=(row, row, row, pl.BlockSpec((tr, 1), lambda i: (i, 0))),
        compiler_params=_params("parallel"),
        name=name,
    )(x, mixed, g.reshape(1, d), b.reshape(1, d), *deps)


def _ln_bwd(dy, xh, r, g, *, name):
    s, d = dy.shape
    tr = _pick(s, 256, 8)

    def body(dy_ref, xh_ref, r_ref, g_ref, du_ref, dub_ref, dg_ref, db_ref):
        dyv, xhv = dy_ref[...], xh_ref[...]
        dxh = dyv * g_ref[...]
        c1 = jnp.mean(dxh, axis=-1, keepdims=True)
        c2 = jnp.mean(dxh * xhv, axis=-1, keepdims=True)
        du = r_ref[...] * (dxh - c1 - xhv * c2)
        du_ref[...] = du
        dub_ref[...] = du.astype(MXU_DTYPE)

        @pl.when(pl.program_id(0) == 0)
        def _():
            dg_ref[...] = jnp.zeros_like(dg_ref)
            db_ref[...] = jnp.zeros_like(db_ref)

        dg_ref[...] += jnp.sum(dyv * xhv, axis=0, keepdims=True)
        db_ref[...] += jnp.sum(dyv, axis=0, keepdims=True)

    row = pl.BlockSpec((tr, d), lambda i: (i, 0))
    vec = pl.BlockSpec((1, d), lambda i: (0, 0))
    return pl.pallas_call(
        body,
        out_shape=(jax.ShapeDtypeStruct((s, d), F32), jax.ShapeDtypeStruct((s, d), MXU_DTYPE),
                   jax.ShapeDtypeStruct((1, d), F32), jax.ShapeDtypeStruct((1, d), F32)),
        grid=(s // tr,),
        in_specs=[row, row, pl.BlockSpec((tr, 1), lambda i: (i, 0)), vec],
        out_specs=(row, row, vec, vec),
        compiler_params=_params("arbitrary"),
        name=name,
    )(dy, xh, r, g.reshape(1, d))


def _rms_fwd(x, g, *, name):
    s, d = x.shape
    tr = _pick(s, 512, 8)

    def body(x_ref, g_ref, y_ref, r_ref):
        xv = x_ref[...]
        r = lax.rsqrt(jnp.mean(xv * xv, axis=-1, keepdims=True) + RMS_EPS)
        y_ref[...] = (xv * r * g_ref[...]).astype(y_ref.dtype)
        r_ref[...] = r

    return pl.pallas_call(
        body,
        out_shape=(jax.ShapeDtypeStruct((s, d), MXU_DTYPE), jax.ShapeDtypeStruct((s, 1), F32)),
        grid=(s // tr,),
        in_specs=[pl.BlockSpec((tr, d), lambda i: (i, 0)), pl.BlockSpec((1, d), lambda i: (0, 0))],
        out_specs=(pl.BlockSpec((tr, d), lambda i: (i, 0)), pl.BlockSpec((tr, 1), lambda i: (i, 0))),
        compiler_params=_params("parallel"),
        name=name,
    )(x, g.reshape(1, d))


def _rms_bwd(dy, x, r, g, *, name):
    s, d = x.shape
    tr = _pick(s, 512, 8)

    def body(dy_ref, x_ref, r_ref, g_ref, dx_ref, dg_ref):
        dyv, rv = dy_ref[...], r_ref[...]
        xn = x_ref[...] * rv
        dxn = dyv * g_ref[...]
        dx_ref[...] = rv * (dxn - xn * jnp.mean(dxn * xn, axis=-1, keepdims=True))

        @pl.when(pl.program_id(0) == 0)
        def _():
            dg_ref[...] = jnp.zeros_like(dg_ref)

        dg_ref[...] += jnp.sum(dyv * xn, axis=0, keepdims=True)

    row = pl.BlockSpec((tr, d), lambda i: (i, 0))
    vec = pl.BlockSpec((1, d), lambda i: (0, 0))
    return pl.pallas_call(
        body,
        out_shape=(jax.ShapeDtypeStruct((s, d), F32), jax.ShapeDtypeStruct((1, d), F32)),
        grid=(s // tr,),
        in_specs=[row, row, pl.BlockSpec((tr, 1), lambda i: (i, 0)), vec],
        out_specs=(row, vec),
        compiler_params=_params("arbitrary"),
        name=name,
    )(dy, x, r, g.reshape(1, d))


def _rope_tables(s, inverse):
    inv_freq = ROPE_BASE ** (-jnp.arange(0, D_ROPE, 2, dtype=F32) / D_ROPE)
    ang = jnp.arange(s, dtype=F32)[:, None] * inv_freq[None, :]
    cos, sin = jnp.cos(ang), jnp.sin(ang)
    if inverse:
        sin = -sin
    half = D_ROPE // 2
    one, zero = jnp.ones((s, D_NOPE), F32), jnp.zeros((s, D_NOPE), F32)
    pad1, pad0 = jnp.ones((s, LANES - D_NOPE - D_ROPE), F32), jnp.zeros((s, LANES - D_NOPE - D_ROPE), F32)
    zh = jnp.zeros((s, half), F32)
    c = jnp.concatenate([one, cos, cos, pad1], axis=1)
    s_lo = jnp.concatenate([zero, -sin, zh, pad0], axis=1)
    s_hi = jnp.concatenate([zero, zh, sin, pad0], axis=1)
    return c, s_lo, s_hi


def _rope(x, tables, *, out_dtype, head_sum=False, name):
    h, s, w = x.shape
    ts = _pick(s, 2048, 8)

    def body(x_ref, c_ref, lo_ref, hi_ref, y_ref, *sum_ref):
        y = _rotate(x_ref[0], c_ref[...], lo_ref[...], hi_ref[...])
        y_ref[0] = y.astype(y_ref.dtype)
        if head_sum:
            @pl.when(pl.program_id(1) == 0)
            def _():
                sum_ref[0][...] = jnp.zeros_like(sum_ref[0])

            sum_ref[0][...] += y

    tab = pl.BlockSpec((ts, w), lambda i, hh: (i, 0))
    blk = pl.BlockSpec((1, ts, w), lambda i, hh: (hh, i, 0))
    out_shape = [jax.ShapeDtypeStruct((h, s, w), out_dtype)]
    out_specs = [blk]
    if head_sum:
        out_shape.append(jax.ShapeDtypeStruct((s, w), F32))
        out_specs.append(tab)
    res = pl.pallas_call(
        body,
        out_shape=tuple(out_shape),
        grid=(s // ts, h),
        in_specs=[blk, tab, tab, tab],
        out_specs=tuple(out_specs),
        compiler_params=_params("parallel", "arbitrary"),
        name=name,
    )(x, *tables)
    return res if head_sum else res[0]


def _band_scores(q, kw, slope, i, *, scale, n_back, bps):
    b, r, _ = q.shape
    sc = _bdot(q, kw, NT) * scale
    shape = (b, r, 2 * BLK)
    row = lax.broadcasted_iota(jnp.int32, shape, 1) & (BLK - 1)
    col = lax.broadcasted_iota(jnp.int32, shape, 2)
    rel = BLK + row - col
    first_col = jnp.where(i % bps == 0, BLK, 0)
    valid = (rel >= 0) & (rel <= n_back) & (col >= first_col)
    return jnp.where(valid, sc - slope * rel.astype(F32), NEG)


def _band_fwd(q, k, v, slope, sink, *, scale, n_back, bps, name):
    g, b, rows, dh = q.shape
    r = slope.shape[2]
    nq = rows // r
    skv = k.shape[2]
    use_sink = sink is not None

    def body(*refs):
        q_ref, k_ref, v_ref, slope_ref = refs[:4]
        sink_ref = refs[4] if use_sink else None
        o_ref, lse_ref = refs[4 + use_sink:]
        i = pl.program_id(1)
        off = pl.multiple_of(i * BLK, BLK)
        kw = k_ref[0, :, pl.ds(off, 2 * BLK), :]
        vw = v_ref[0, :, pl.ds(off, 2 * BLK), :]
        sc = _band_scores(q_ref[0], kw, slope_ref[0], i, scale=scale, n_back=n_back, bps=bps)
        m = jnp.max(sc, axis=-1, keepdims=True)
        if use_sink:
            m = jnp.maximum(m, sink_ref[0])
        p = jnp.exp(sc - m)
        l = jnp.sum(p, axis=-1, keepdims=True)
        if use_sink:
            l = l + jnp.exp(sink_ref[0] - m)
        o_ref[0] = _bdot(p.astype(MXU_DTYPE), vw, NN) / l
        lse_ref[0] = m + jnp.log(l)

    qspec = pl.BlockSpec((1, b, r, dh), lambda gg, i: (gg, 0, i, 0))
    kspec = pl.BlockSpec((1, b, skv, dh), lambda gg, i: (gg, 0, 0, 0))
    rspec = pl.BlockSpec((1, b, r, 1), lambda gg, i: (gg, 0, 0, 0))
    ins = [q, k, v, slope] + ([sink] if use_sink else [])
    return pl.pallas_call(
        body,
        out_shape=(jax.ShapeDtypeStruct((g, b, rows, dh), F32), jax.ShapeDtypeStruct((g, b, rows, 1), F32)),
        grid=(g, nq),
        in_specs=[qspec, kspec, kspec, rspec] + ([rspec] if use_sink else []),
        out_specs=(qspec, pl.BlockSpec((1, b, r, 1), lambda gg, i: (gg, 0, i, 0))),
        compiler_params=_params("parallel", "arbitrary"),
        name=name,
    )(*ins)


def _band_bwd(q, k, v, do, o, lse, slope, sink, *, scale, n_back, bps, name):
    g, b, rows, dh = q.shape
    r = slope.shape[2]
    nq = rows // r
    skv = k.shape[2]
    use_sink = sink is not None
    stacked = r // BLK

    def body(*refs):
        q_ref, k_ref, v_ref, do_ref, o_ref, lse_ref, slope_ref = refs[:7]
        sink_ref = refs[7] if use_sink else None
        dq_ref, dk_ref, dv_ref = refs[7 + use_sink:10 + use_sink]
        i = pl.program_id(1)

        @pl.when(i == 0)
        def _():
            dk_ref[...] = jnp.zeros_like(dk_ref)
            dv_ref[...] = jnp.zeros_like(dv_ref)

        off = pl.multiple_of(i * BLK, BLK)
        qb = q_ref[0]
        kw = k_ref[0, :, pl.ds(off, 2 * BLK), :]
        vw = v_ref[0, :, pl.ds(off, 2 * BLK), :]
        dof = do_ref[0]
        dob = dof.astype(MXU_DTYPE)
        lse_b = lse_ref[0]
        delta = jnp.sum(dof * o_ref[0], axis=-1, keepdims=True)
        sc = _band_scores(qb, kw, slope_ref[0], i, scale=scale, n_back=n_back, bps=bps)
        p = jnp.exp(sc - lse_b)
        ds = (p * (_bdot(dob, vw, NT) - delta) * scale).astype(MXU_DTYPE)
        dq_ref[0] = _bdot(ds, kw, NN)
        dk_ref[0, :, pl.ds(off, 2 * BLK), :] += _bdot(ds, qb, TN)
        dv_ref[0, :, pl.ds(off, 2 * BLK), :] += _bdot(p.astype(MXU_DTYPE), dob, TN)

        if use_sink:
            dsink_ref = refs[10 + use_sink]

            @pl.when(i == 0)
            def _():
                dsink_ref[...] = jnp.zeros_like(dsink_ref)

            contrib = -jnp.exp(sink_ref[0] - lse_b) * delta
            for n in range(stacked):
                part = jnp.sum(contrib[0, n * BLK:(n + 1) * BLK, :], axis=0, keepdims=True)
                dsink_ref[0, n:n + 1, :] += jnp.broadcast_to(part, (1, LANES))

    def qspec(w):
        return pl.BlockSpec((1, b, r, w), lambda gg, i: (gg, 0, i, 0))

    kspec = pl.BlockSpec((1, b, skv, dh), lambda gg, i: (gg, 0, 0, 0))
    rspec = pl.BlockSpec((1, b, r, 1), lambda gg, i: (gg, 0, 0, 0))
    ins = [q, k, v, do, o, lse, slope] + ([sink] if use_sink else [])
    in_specs = [qspec(dh), kspec, kspec, qspec(dh), qspec(dh), qspec(1), rspec] + ([rspec] if use_sink else [])
    out_shape = [jax.ShapeDtypeStruct((g, b, rows, dh), F32), jax.ShapeDtypeStruct((g, b, skv, dh), F32),
                 jax.ShapeDtypeStruct((g, b, skv, dh), F32)]
    out_specs = [qspec(dh), kspec, kspec]
    if use_sink:
        assert b == 1
        out_shape.append(jax.ShapeDtypeStruct((g, stacked, LANES), F32))
        out_specs.append(pl.BlockSpec((1, stacked, LANES), lambda gg, i: (gg, 0, 0)))
    return pl.pallas_call(
        body,
        out_shape=tuple(out_shape),
        grid=(g, nq),
        in_specs=in_specs,
        out_specs=tuple(out_specs),
        compiler_params=_params("parallel", "arbitrary"),
        name=name,
    )(*ins)


def _pair_masks():
    first = lax.broadcasted_iota(jnp.int32, (1, LANES), 1) < HEAD_DIM
    m0 = first.astype(MXU_DTYPE)
    return first, (m0, 1 - m0)


def _dil_window(ref, i):
    prev = pl.multiple_of(jnp.maximum(i - 1, 0) * BLK, BLK)
    cur = pl.multiple_of(i * BLK, BLK)
    return prev, cur, jnp.concatenate([ref[pl.ds(prev, BLK), :], ref[pl.ds(cur, BLK), :]], axis=0)


def _dil_mask(i, n_back, bps):
    row = lax.broadcasted_iota(jnp.int32, (BLK, 2 * BLK), 0)
    col = lax.broadcasted_iota(jnp.int32, (BLK, 2 * BLK), 1)
    rel = BLK + row - col
    first_col = jnp.where(i % bps == 0, BLK, 0)
    return (rel >= 0) & (rel <= n_back) & (col >= first_col), rel.astype(F32)


def _dil_fwd(slab, slopes, *, scale, n_back, bps, name):
    s, w = slab.shape[0], slab.shape[1] // 3

    def body(q_ref, k_ref, v_ref, o_ref, lse_ref):
        i = pl.program_id(0)
        first, masks = _pair_masks()
        _, _, kw = _dil_window(k_ref, i)
        _, _, vw = _dil_window(v_ref, i)
        valid, rel = _dil_mask(i, n_back, bps)
        for p in range(w // LANES):
            cols = slice(p * LANES, (p + 1) * LANES)
            qp, kp, vp = q_ref[:, cols], kw[:, cols], vw[:, cols]
            outs, lses = [], []
            for hh in range(2):
                sc = _dot(qp * masks[hh], kp, NT) * scale - float(slopes[2 * p + hh]) * rel
                sc = jnp.where(valid, sc, NEG)
                m = jnp.max(sc, axis=-1, keepdims=True)
                e = jnp.exp(sc - m)
                l = jnp.sum(e, axis=-1, keepdims=True)
                outs.append(_dot(e.astype(MXU_DTYPE), vp, NN) / l)
                lses.append(m + jnp.log(l))
            o_ref[:, cols] = jnp.where(first, outs[0], outs[1])
            lse_ref[:, cols] = jnp.where(first, lses[0], lses[1])

    blk = pl.BlockSpec((BLK, w), lambda i: (i, 0))
    return pl.pallas_call(
        body,
        out_shape=(jax.ShapeDtypeStruct((s, w), F32), jax.ShapeDtypeStruct((s, w), F32)),
        grid=(s // BLK,),
        in_specs=[blk, pl.BlockSpec((s, w), lambda i: (0, 1)), pl.BlockSpec((s, w), lambda i: (0, 2))],
        out_specs=(blk, blk),
        compiler_params=_params("arbitrary"),
        name=name,
    )(slab, slab, slab)


def _dil_bwd(slab, pack, slopes, *, scale, n_back, bps, name):
    s, w = slab.shape[0], slab.shape[1] // 3

    def body(q_ref, k_ref, v_ref, do_ref, o_ref, lse_ref, dq_ref, dk_ref, dv_ref):
        i = pl.program_id(0)

        @pl.when(i == 0)
        def _():
            dk_ref[...] = jnp.zeros_like(dk_ref)
            dv_ref[...] = jnp.zeros_like(dv_ref)

        first, masks = _pair_masks()
        prev, cur, kw = _dil_window(k_ref, i)
        _, _, vw = _dil_window(v_ref, i)
        valid, rel = _dil_mask(i, n_back, bps)
        for p in range(w // LANES):
            cols = slice(p * LANES, (p + 1) * LANES)
            qp, kp, vp = q_ref[:, cols], kw[:, cols], vw[:, cols]
            dof, lse_p = do_ref[:, cols], lse_ref[:, cols]
            prod = dof * o_ref[:, cols]
            do_b = dof.astype(MXU_DTYPE)
            dqs, dk_add, dv_add = [], None, None
            for hh in range(2):
                qh, doh = qp * masks[hh], do_b * masks[hh]
                delta = jnp.sum(jnp.where(first, prod, 0.0) if hh == 0 else jnp.where(first, 0.0, prod), axis=-1, keepdims=True)
                sc = _dot(qh, kp, NT) * scale - float(slopes[2 * p + hh]) * rel
                e = jnp.exp(jnp.where(valid, sc, NEG) - lse_p[:, hh * HEAD_DIM:hh * HEAD_DIM + 1])
                ds = (e * (_dot(doh, vp, NT) - delta) * scale).astype(MXU_DTYPE)
                dqs.append(_dot(ds, kp, NN))
                dk_h, dv_h = _dot(ds, qh, TN), _dot(e.astype(MXU_DTYPE), doh, TN)
                dk_add = dk_h if dk_add is None else dk_add + dk_h
                dv_add = dv_h if dv_add is None else dv_add + dv_h
            dq_ref[:, cols] = jnp.where(first, dqs[0], dqs[1])
            dk_ref[pl.ds(prev, BLK), cols] += dk_add[:BLK]
            dk_ref[pl.ds(cur, BLK), cols] += dk_add[BLK:]
            dv_ref[pl.ds(prev, BLK), cols] += dv_add[:BLK]
            dv_ref[pl.ds(cur, BLK), cols] += dv_add[BLK:]

    def blk(c):
        return pl.BlockSpec((BLK, w), lambda i: (i, c))

    def whole(c):
        return pl.BlockSpec((s, w), lambda i: (0, c))

    shp = jax.ShapeDtypeStruct((s, w), F32)
    return pl.pallas_call(
        body,
        out_shape=(shp, shp, shp),
        grid=(s // BLK,),
        in_specs=[blk(0), whole(1), whole(2), blk(0), blk(1), blk(2)],
        out_specs=(blk(0), whole(0), whole(0)),
        compiler_params=_params("arbitrary"),
        name=name,
    )(slab, slab, slab, pack, pack, pack)


def _merge(outs, lses, *, name):
    s, w = outs[0].shape
    tr = _pick(s, 512, 8)

    def body(o0, o1, o2, l0, l1, l2, ob_ref, lt_ref):
        a, b, c = l0[...], l1[...], l2[...]
        m = jnp.maximum(jnp.maximum(a, b), c)
        ea, eb, ec = jnp.exp(a - m), jnp.exp(b - m), jnp.exp(c - m)
        den = ea + eb + ec
        ob_ref[...] = (ea / den) * o0[...] + (eb / den) * o1[...] + (ec / den) * o2[...]
        lt_ref[...] = m + jnp.log(den)

    spec = pl.BlockSpec((tr, w), lambda i: (i, 0))
    return pl.pallas_call(
        body,
        out_shape=(jax.ShapeDtypeStruct((s, w), F32), jax.ShapeDtypeStruct((s, w), F32)),
        grid=(s // tr,),
        in_specs=[spec] * 6,
        out_specs=(spec, spec),
        compiler_params=_params("parallel"),
        name=name,
    )(*outs, *lses)


def _tile_iotas(t):
    return lax.broadcasted_iota(jnp.int32, (t, t), 0), lax.broadcasted_iota(jnp.int32, (t, t), 1)


def _rotate(x, c, s_lo, s_hi):
    half = D_ROPE // 2
    return x * c + pltpu.roll(x, LANES - half, 1) * s_lo + pltpu.roll(x, half, 1) * s_hi


def _mla_keys(kv_h, kr_t, first):
    return jnp.where(first, kv_h, kr_t)


def _mla_fwd(qd, kvd, krp, tables, *, scale, name):
    s = qd.shape[0]
    pairs = qd.shape[1] // (2 * LANES)
    t = min(CAUSAL_TILE, s)

    def body(q_ref, kv_ref, kr_ref, c_ref, lo_ref, hi_ref, o_ref, lse_ref):
        i = pl.program_id(1)
        first = lax.broadcasted_iota(jnp.int32, (1, LANES), 1) < HEAD_DIM
        tabs = (c_ref[...], lo_ref[...], hi_ref[...])
        q_heads = [_rotate(q_ref[:, hh * LANES:(hh + 1) * LANES], *tabs).astype(MXU_DTYPE) for hh in range(2)]

        def tile(j, carry, diagonal):
            off = pl.multiple_of(j * t, t)
            kr_t = kr_ref[pl.ds(off, t), :]
            out = []
            for hh in range(2):
                m, l, acc = carry[3 * hh:3 * hh + 3]
                kv_h = kv_ref[pl.ds(off, t), hh * LANES:(hh + 1) * LANES]
                sc = _dot(q_heads[hh], _mla_keys(kv_h, kr_t, first), NT) * scale
                if diagonal:
                    row, col = _tile_iotas(t)
                    sc = jnp.where(row >= col, sc, NEG)
                m_new = jnp.maximum(m, jnp.max(sc, axis=-1, keepdims=True))
                a = jnp.exp(m - m_new)
                p = jnp.exp(sc - m_new)
                out += [m_new, a * l + jnp.sum(p, axis=-1, keepdims=True), a * acc + _dot(p.astype(MXU_DTYPE), kv_h, NN)]
            return tuple(out)

        init = (jnp.full((t, 1), NEG, F32), jnp.zeros((t, 1), F32), jnp.zeros((t, LANES), F32)) * 2
        carry = lax.fori_loop(0, i, lambda j, c: tile(j, c, False), init)
        m0, l0, acc0, m1, l1, acc1 = tile(i, carry, True)
        o_ref[...] = jnp.where(first, pltpu.roll(acc0 / l0, HEAD_DIM, 1), acc1 / l1)
        lse_ref[0] = jnp.where(lax.broadcasted_iota(jnp.int32, (t, 2), 1) == 0, m0 + jnp.log(l0), m1 + jnp.log(l1))

    tab = pl.BlockSpec((t, LANES), lambda p, i: (i, 0))
    return pl.pallas_call(
        body,
        out_shape=(jax.ShapeDtypeStruct((s, pairs * LANES), F32), jax.ShapeDtypeStruct((pairs, s, 2), F32)),
        grid=(pairs, s // t),
        in_specs=[pl.BlockSpec((t, 2 * LANES), lambda p, i: (i, p)), pl.BlockSpec((s, 2 * LANES), lambda p, i: (0, p)),
                  pl.BlockSpec((s, LANES), lambda p, i: (0, 0)), tab, tab, tab],
        out_specs=(pl.BlockSpec((t, LANES), lambda p, i: (i, p)), pl.BlockSpec((1, t, 2), lambda p, i: (p, i, 0))),
        compiler_params=_params("parallel", "arbitrary"),
        name=name,
    )(qd, kvd, krp, *tables)


def _mla_bwd(qd, kvd, krp, tables, do, o, lse, *, do_block0, scale, name):
    s = qd.shape[0]
    pairs = qd.shape[1] // (2 * LANES)
    t = min(CAUSAL_TILE, s)

    def body(q_ref, kv_ref, kr_ref, c_ref, lo_ref, hi_ref, do_ref, o_ref, lse_ref, dq_ref, dkv_ref, dkr_ref):
        i = pl.program_id(1)

        @pl.when(i == 0)
        def _():
            dkv_ref[...] = jnp.zeros_like(dkv_ref)
            dkr_ref[...] = jnp.zeros_like(dkr_ref)

        first = lax.broadcasted_iota(jnp.int32, (1, LANES), 1) < HEAD_DIM
        tabs = (c_ref[...], lo_ref[...], hi_ref[...])
        q_heads = [_rotate(q_ref[:, hh * LANES:(hh + 1) * LANES], *tabs).astype(MXU_DTYPE) for hh in range(2)]
        dof = do_ref[...]
        prod = dof * o_ref[...]
        deltas = [jnp.sum(jnp.where(first, prod, 0.0), axis=-1, keepdims=True),
                  jnp.sum(jnp.where(first, 0.0, prod), axis=-1, keepdims=True)]
        do_heads = [jnp.where(first, 0.0, pltpu.roll(dof, HEAD_DIM, 1)).astype(MXU_DTYPE),
                    jnp.where(first, 0.0, dof).astype(MXU_DTYPE)]
        lses = [lse_ref[0][:, hh:hh + 1] for hh in range(2)]

        def tile(j, carry, diagonal):
            off = pl.multiple_of(j * t, t)
            kr_t = kr_ref[pl.ds(off, t), :]
            out, dkr_add = [], None
            for hh in range(2):
                kv_h = kv_ref[pl.ds(off, t), hh * LANES:(hh + 1) * LANES]
                k_h = _mla_keys(kv_h, kr_t, first)
                sc = _dot(q_heads[hh], k_h, NT) * scale
                if diagonal:
                    row, col = _tile_iotas(t)
                    sc = jnp.where(row >= col, sc, NEG)
                p = jnp.exp(sc - lses[hh])
                ds = (p * (_dot(do_heads[hh], kv_h, NT) - deltas[hh]) * scale).astype(MXU_DTYPE)
                dk_full = _dot(ds, q_heads[hh], TN)
                dv_full = _dot(p.astype(MXU_DTYPE), do_heads[hh], TN)
                dkv_ref[pl.ds(off, t), hh * LANES:(hh + 1) * LANES] += jnp.where(first, dk_full, dv_full)
                rot = jnp.where(first, 0.0, dk_full)
                dkr_add = rot if dkr_add is None else dkr_add + rot
                out.append(carry[hh] + _dot(ds, k_h, NN))
            dkr_ref[0, pl.ds(off, t), :] += dkr_add
            return tuple(out)

        zacc = jnp.zeros((t, LANES), F32)
        carry = lax.fori_loop(0, i, lambda j, c: tile(j, c, False), (zacc, zacc))
        dq_heads = tile(i, carry, True)
        for hh in range(2):
            dq_ref[:, hh * LANES:(hh + 1) * LANES] = _rotate(dq_heads[hh], tabs[0], -tabs[1], -tabs[2])

    tab = pl.BlockSpec((t, LANES), lambda p, i: (i, 0))
    qspec = pl.BlockSpec((t, 2 * LANES), lambda p, i: (i, p))
    kvspec = pl.BlockSpec((s, 2 * LANES), lambda p, i: (0, p))
    return pl.pallas_call(
        body,
        out_shape=(jax.ShapeDtypeStruct(qd.shape, F32), jax.ShapeDtypeStruct(kvd.shape, F32),
                   jax.ShapeDtypeStruct((pairs, s, LANES), F32)),
        grid=(pairs, s // t),
        in_specs=[qspec, kvspec, pl.BlockSpec((s, LANES), lambda p, i: (0, 0)), tab, tab, tab,
                  pl.BlockSpec((t, LANES), lambda p, i: (i, do_block0 + p)), pl.BlockSpec((t, LANES), lambda p, i: (i, p)),
                  pl.BlockSpec((1, t, 2), lambda p, i: (p, i, 0))],
        out_specs=(qspec, kvspec, pl.BlockSpec((1, s, LANES), lambda p, i: (p, 0, 0))),
        compiler_params=_params("parallel", "arbitrary"),
        name=name,
    )(qd, kvd, krp, *tables, do, o, lse)


def _split_cumsum(x, tri, terms=2):
    hi = x.astype(BF16)
    if terms == 1:
        return _dot(hi, tri, NN)
    lo = (x - hi.astype(F32)).astype(BF16)
    return _dot(hi, tri, NN) + _dot(lo, tri, NN)


def _chunked_cumsum(x, tri, run, *, reverse, negate=False, terms=2):
    c = tri.shape[0]
    n = x.shape[1] // c
    parts = [None] * n
    for idx in (reversed(range(n)) if reverse else range(n)):
        xc = x[:, idx * c:(idx + 1) * c]
        sums = _split_cumsum(xc, tri, terms)
        parts[idx] = (-run) - sums if negate else run + sums
        run = run + jnp.sum(xc, axis=-1, keepdims=True)
    return (parts[0] if n == 1 else jnp.concatenate(parts, axis=1)), run


def _sb_logs(z):
    l1 = jnp.log(1.0 + jnp.exp(-jnp.abs(z)))
    return jnp.minimum(z, 0.0) - l1, -jnp.maximum(z, 0.0) - l1


def _scaled_query_heads(q, masks, scale):
    assert math.log2(scale).is_integer(), scale
    return [q * (m * scale).astype(q.dtype) for m in masks]


def _sb_fwd(qkv, *, heads, scale, name):
    s = qkv.shape[0]
    pairs = heads * HEAD_DIM // LANES
    t = min(CAUSAL_TILE, s)
    cc = min(CUM_CHUNK, t)

    def body(q_ref, k_ref, v_ref, o_ref, t_ref):
        i = pl.program_id(1)
        first, masks = _pair_masks()
        q_heads = _scaled_query_heads(q_ref[...], masks, scale)
        crow, ccol = _tile_iotas(cc)
        after = (crow > ccol).astype(BF16)

        def tile(j, carry, diagonal):
            off = pl.multiple_of(j * t, t)
            kb = k_ref[pl.ds(off, t), :]
            vb = v_ref[pl.ds(off, t), :]
            if diagonal:
                row, col = _tile_iotas(t)
                strict = row > col
            out = []
            for hh in range(2):
                run, acc = carry[2 * hh], carry[2 * hh + 1]
                log_beta, log_keep = _sb_logs(_dot(q_heads[hh], kb, NT))
                if diagonal:
                    log_keep = jnp.where(strict, log_keep, 0.0)
                a, run = _chunked_cumsum(log_keep, after, run, reverse=True)
                w = jnp.exp(log_beta + a)
                if diagonal:
                    w = jnp.where(strict, w, 0.0)
                out += [run, acc + _dot(w.astype(MXU_DTYPE), vb, NN)]
            return tuple(out)

        zero, zacc = jnp.zeros((t, 1), F32), jnp.zeros((t, LANES), F32)
        carry = tile(i, (zero, zacc, zero, zacc), True)
        run0, acc0, run1, acc1 = lax.fori_loop(0, i, lambda jj, c: tile(i - 1 - jj, c, False), carry)
        o_ref[...] = jnp.where(first, acc0, acc1)
        t_ref[0] = jnp.where(lax.broadcasted_iota(jnp.int32, (t, 2), 1) == 0, run0, run1)

    return pl.pallas_call(
        body,
        out_shape=(jax.ShapeDtypeStruct((s, heads * HEAD_DIM), F32), jax.ShapeDtypeStruct((pairs, s, 2), F32)),
        grid=(pairs, s // t),
        in_specs=[pl.BlockSpec((t, LANES), lambda p, i: (i, p)),
                  pl.BlockSpec((s, LANES), lambda p, i: (0, pairs + p)),
                  pl.BlockSpec((s, LANES), lambda p, i: (0, 2 * pairs + p))],
        out_specs=(pl.BlockSpec((t, LANES), lambda p, i: (i, p)), pl.BlockSpec((1, t, 2), lambda p, i: (p, i, 0))),
        compiler_params=_params("parallel", "arbitrary"),
        name=name,
    )(qkv, qkv, qkv)


def _sb_bwd(qkv, do, total, *, heads, scale, name):
    s = qkv.shape[0]
    pairs = heads * HEAD_DIM // LANES
    t = min(CAUSAL_TILE, s)
    cc = min(CUM_CHUNK, t)

    def body(q_ref, k_ref, v_ref, do_ref, t_ref, dq_ref, dk_ref, dv_ref):
        i = pl.program_id(1)

        @pl.when(i == 0)
        def _():
            dk_ref[...] = jnp.zeros_like(dk_ref)
            dv_ref[...] = jnp.zeros_like(dv_ref)

        first, masks = _pair_masks()
        q_heads = _scaled_query_heads(q_ref[...], masks, scale)
        do_b = do_ref[...].astype(MXU_DTYPE)
        do_heads = [do_b * m for m in masks]
        tots = [t_ref[0][:, hh:hh + 1] for hh in range(2)]
        crow, ccol = _tile_iotas(cc)
        upto = (crow <= ccol).astype(BF16)
        before = (crow < ccol).astype(BF16)

        def tile(j, carry, diagonal):
            off = pl.multiple_of(j * t, t)
            kb = k_ref[pl.ds(off, t), :]
            vb = v_ref[pl.ds(off, t), :]
            if diagonal:
                row, col = _tile_iotas(t)
                strict = row > col
            out, dk_add, dv_add = [], None, None
            for hh in range(2):
                run_keep, run_g, dq_acc = carry[3 * hh:3 * hh + 3]
                log_beta, log_keep = _sb_logs(_dot(q_heads[hh], kb, NT))
                keep = jnp.exp(log_keep)
                if diagonal:
                    log_keep = jnp.where(strict, log_keep, 0.0)
                a, run_keep = _chunked_cumsum(log_keep, upto, run_keep - tots[hh], reverse=False, negate=True)
                run_keep = run_keep + tots[hh]
                w = jnp.exp(log_beta + a)
                if diagonal:
                    w = jnp.where(strict, w, 0.0)
                g = w * _dot(do_heads[hh], vb, NT)
                prefix, run_g = _chunked_cumsum(g, before, run_g, reverse=False, terms=1)
                dz = g * keep - (1.0 - keep) * prefix
                if diagonal:
                    dz = jnp.where(strict, dz, 0.0)
                dz = dz.astype(MXU_DTYPE)
                dk_h = _dot(dz, q_heads[hh], TN)
                dv_h = _dot(w.astype(MXU_DTYPE), do_heads[hh], TN)
                dk_add = dk_h if dk_add is None else dk_add + dk_h
                dv_add = dv_h if dv_add is None else dv_add + dv_h
                out += [run_keep, run_g, dq_acc + _dot(dz, kb, NN)]
            dk_ref[pl.ds(off, t), :] += dk_add
            dv_ref[pl.ds(off, t), :] += dv_add
            return tuple(out)

        zero, zacc = jnp.zeros((t, 1), F32), jnp.zeros((t, LANES), F32)
        carry = lax.fori_loop(0, i, lambda j, c: tile(j, c, False), (zero, zero, zacc, zero, zero, zacc))
        res = tile(i, carry, True)
        dq_ref[...] = jnp.where(first, res[2], res[5]) * scale

    qspec = pl.BlockSpec((t, LANES), lambda p, i: (i, p))
    shp = jax.ShapeDtypeStruct((s, heads * HEAD_DIM), F32)
    return pl.pallas_call(
        body,
        out_shape=(shp, shp, shp),
        grid=(pairs, s // t),
        in_specs=[qspec, pl.BlockSpec((s, LANES), lambda p, i: (0, pairs + p)),
                  pl.BlockSpec((s, LANES), lambda p, i: (0, 2 * pairs + p)), qspec,
                  pl.BlockSpec((1, t, 2), lambda p, i: (p, i, 0))],
        out_specs=(qspec, pl.BlockSpec((s, LANES), lambda p, i: (0, p)), pl.BlockSpec((s, LANES), lambda p, i: (0, p))),
        compiler_params=_params("parallel", "arbitrary"),
        name=name,
    )(qkv, qkv, qkv, do, total)


def _loss_head(y, target, *, name):
    s, d = y.shape
    tr = _pick(s, 256, 8)

    def body(y_ref, t_ref, dy_ref, loss_ref):
        err = y_ref[...] - t_ref[...]
        dy_ref[...] = err * (1.0 / d)

        @pl.when(pl.program_id(0) == 0)
        def _():
            loss_ref[...] = jnp.zeros_like(loss_ref)

        per_tok = jnp.mean(err * err, axis=-1, keepdims=True)
        loss_ref[...] += 0.5 * jnp.sum(per_tok, axis=0, keepdims=True)

    row = pl.BlockSpec((tr, d), lambda i: (i, 0))
    return pl.pallas_call(
        body,
        out_shape=(jax.ShapeDtypeStruct((s, d), F32), jax.ShapeDtypeStruct((1, LANES), F32)),
        grid=(s // tr,),
        in_specs=[row, row],
        out_specs=(row, pl.BlockSpec((1, LANES), lambda i: (0, 0))),
        compiler_params=_params("arbitrary"),
        name=name,
    )(y, target)


def _adamw(w, grads, m, v, *, name, my_chip=None, deps=()):
    nl, r, c = w.shape
    pieces = my_chip is not None
    cp = (grads[0][0] if pieces else grads[0]).shape[-1]
    tr = _pick(r, min(256, max(16, 262144 // cp)), 8)
    per = 4 if pieces else 1

    def body(chip_ref, *refs):
        w_ref, m_ref, v_ref = refs[:3]
        g_refs = refs[3:3 + per * nl]
        g_out, d_ref, m2_ref, v2_ref = refs[3 + per * nl + len(deps):]

        def grad(n):
            if not pieces:
                return g_refs[n][:, :c]
            own, r0, r1, r2 = (t[0, :, :c].astype(F32) for t in g_refs[4 * n:4 * n + 4])
            return ((own + r0) + r1) + r2

        layer = pl.program_id(0)
        gv = grad(0)
        for n in range(1, nl):
            gv = jnp.where(layer == n, grad(n), gv)
        m2 = ADAM_B1 * m_ref[0] + (1.0 - ADAM_B1) * gv
        v2 = ADAM_B2 * v_ref[0] + (1.0 - ADAM_B2) * (gv * gv)
        m_hat = m2 / (1.0 - ADAM_B1 ** ADAM_STEP)
        v_hat = v2 / (1.0 - ADAM_B2 ** ADAM_STEP)
        g_out[0] = gv
        d_ref[0] = -ADAM_LR * (m_hat / (jnp.sqrt(v_hat) + ADAM_EPS) + ADAM_WD * w_ref[0])
        m2_ref[0] = m2
        v2_ref[0] = v2

    blk = pl.BlockSpec((1, tr, c), lambda l, i, chip_ref: (l, i, 0))
    if pieces:
        g_specs = [pl.BlockSpec((1, tr, cp), lambda l, i, chip_ref: (chip_ref[0], i, 0))]
        g_specs += [pl.BlockSpec((1, tr, cp), lambda l, i, chip_ref, k=k: (k, i, 0)) for k in range(3)]
        g_ins = [t for partial, recv in grads for t in (partial, recv, recv, recv)]
        chip = my_chip.reshape(1).astype(jnp.int32)
    else:
        g_specs, g_ins, chip = [pl.BlockSpec((tr, cp), lambda l, i, chip_ref: (i, 0))], list(grads), jnp.zeros((1,), jnp.int32)
    shp = jax.ShapeDtypeStruct((nl, r, c), F32)
    grid_spec = pltpu.PrefetchScalarGridSpec(
        num_scalar_prefetch=1,
        grid=(nl, r // tr),
        in_specs=[blk, blk, blk] + g_specs * nl + [ANY] * len(deps),
        out_specs=(blk, blk, blk, blk),
    )
    return pl.pallas_call(
        body,
        out_shape=(shp, shp, shp, shp),
        grid_spec=grid_spec,
        compiler_params=_params("parallel", "parallel"),
        name=name,
    )(chip, w, m, v, *g_ins, *deps)


def _pair_sum(mine, recv, my_c, *, name):
    _, r, c = mine.shape
    tr = _pick(r, 512, 16)

    def body(c_ref, a_ref, b_ref, o_ref):
        o_ref[0] = (a_ref[0].astype(F32) + b_ref[0].astype(F32)).astype(o_ref.dtype)

    grid_spec = pltpu.PrefetchScalarGridSpec(
        num_scalar_prefetch=1,
        grid=(4, r // tr),
        in_specs=[pl.BlockSpec((1, tr, c), lambda kk, i, c_ref: (2 * kk + c_ref[0], i, 0)),
                  pl.BlockSpec((1, tr, c), lambda kk, i, c_ref: (kk, i, 0))],
        out_specs=pl.BlockSpec((1, tr, c), lambda kk, i, c_ref: (kk, i, 0)),
    )
    return pl.pallas_call(
        body,
        out_shape=jax.ShapeDtypeStruct((4, r, c), mine.dtype),
        grid_spec=grid_spec,
        compiler_params=_params("parallel", "parallel"),
        name=name,
    )(my_c.reshape(1).astype(jnp.int32), mine, recv)


def _sum_devices(stack, *, name):
    n, r, c = stack.shape

    def body(s_ref, o_ref):
        acc = s_ref[0]
        for dev in range(1, n):
            acc = acc + s_ref[dev]
        o_ref[...] = acc

    return pl.pallas_call(
        body,
        out_shape=jax.ShapeDtypeStruct((r, c), F32),
        in_specs=[pl.BlockSpec(memory_space=pltpu.VMEM)],
        out_specs=pl.BlockSpec(memory_space=pltpu.VMEM),
        name=name,
    )(stack)


def _mesh_pos():
    return lax.axis_index("x"), lax.axis_index("y"), lax.axis_index("c")


def _all_gather(shards, *, name, deps=()):
    n = len(shards)

    def body(*refs):
        x_refs, out_refs = refs[:n], refs[n + len(deps):2 * n + len(deps)]
        send_sems, recv_sems, local_sems = refs[2 * n + len(deps):]
        x, y, cc = _mesh_pos()
        me, sibling = (x, y, cc), (x, y, 1 - cc)
        chips = [(1 - x, y), (x, 1 - y), (1 - x, 1 - y)]

        def rows(a, px, py, pc):
            return out_refs[a].at[4 * px + 2 * py + pc]

        def copy(a, kk, block, to, src=None):
            return pltpu.make_async_remote_copy(
                src_ref=rows(a, *block) if src is None else src, dst_ref=rows(a, *block),
                send_sem=send_sems.at[7 * a + kk], recv_sem=recv_sems.at[7 * a + kk],
                device_id=to, device_id_type=MESH)

        sends, own = [], []
        for a in range(n):
            own.append(pltpu.make_async_copy(x_refs[a], rows(a, *me), local_sems.at[a]))
            own[a].start()
            first = [copy(a, 0, me, sibling, src=x_refs[a])]
            first += [copy(a, 1 + j, me, (*chip, cc), src=x_refs[a]) for j, chip in enumerate(chips)]
            for cp in first:
                cp.start()
            sends += first
        for a in range(n):
            for j, chip in enumerate(chips):
                copy(a, 1 + j, (*chip, cc), me).wait_recv()
                passed = copy(a, 4 + j, (*chip, cc), sibling)
                passed.start()
                sends.append(passed)
        for a in range(n):
            copy(a, 0, sibling, me).wait_recv()
            for j, chip in enumerate(chips):
                copy(a, 4 + j, (*chip, 1 - cc), me).wait_recv()
        for cp in sends:
            cp.wait_send()
        for cp in own:
            cp.wait()

    return pl.pallas_call(
        body,
        out_shape=tuple(jax.ShapeDtypeStruct((N_DEV,) + t.shape, t.dtype) for t in shards),
        in_specs=[ANY] * (n + len(deps)),
        out_specs=tuple([ANY] * n),
        scratch_shapes=[pltpu.SemaphoreType.DMA((7 * n,)), pltpu.SemaphoreType.DMA((7 * n,)),
                        pltpu.SemaphoreType.DMA((n,))],
        name=name,
    )(*shards, *deps)


def _plan_own_blocks(n):
    def plan(refs, send_sems, recv_sems, outgoing):
        x, y, cc = _mesh_pos()
        peers = [(x, y, 1 - cc), (1 - x, y, cc), (x, 1 - y, cc), (1 - x, 1 - y, cc)]
        copies = []
        for a in range(n):
            land = refs[n + a]
            for kk, (px, py, pc) in enumerate(peers):
                block = (x, y, cc) if outgoing else (px, py, pc)
                rows = land.at[4 * block[0] + 2 * block[1] + block[2]]
                copies.append(pltpu.make_async_remote_copy(
                    src_ref=refs[a] if outgoing else rows, dst_ref=rows, send_sem=send_sems.at[4 * a + kk],
                    recv_sem=recv_sems.at[4 * a + kk], device_id=(px, py, pc), device_id_type=MESH))
        return copies

    plan.n_sems = 4 * n
    return plan


def _plan_pass_on(n):
    def plan(refs, send_sems, recv_sems, outgoing):
        x, y, cc = _mesh_pos()
        copies = []
        for a in range(n):
            for j, (px, py) in enumerate([(1 - x, y), (x, 1 - y), (1 - x, 1 - y)]):
                rows = refs[a].at[4 * px + 2 * py + (cc if outgoing else 1 - cc)]
                copies.append(pltpu.make_async_remote_copy(
                    src_ref=rows, dst_ref=rows, send_sem=send_sems.at[3 * a + j], recv_sem=recv_sems.at[3 * a + j],
                    device_id=(x, y, 1 - cc), device_id_type=MESH))
        return copies

    plan.n_sems = 3 * n
    return plan


def _plan_to_sibling(n):
    def plan(refs, send_sems, recv_sems, outgoing):
        x, y, cc = _mesh_pos()
        copies = []
        for a in range(n):
            for chip in range(4):
                dst = refs[n + a].at[chip]
                copies.append(pltpu.make_async_remote_copy(
                    src_ref=refs[a].at[2 * chip + (1 - cc)] if outgoing else dst, dst_ref=dst,
                    send_sem=send_sems.at[4 * a + chip], recv_sem=recv_sems.at[4 * a + chip],
                    device_id=(x, y, 1 - cc), device_id_type=MESH))
        return copies

    plan.n_sems = 4 * n
    return plan


def _plan_to_chips(n):
    def plan(refs, send_sems, recv_sems, outgoing):
        x, y, cc = _mesh_pos()
        copies = []
        for a in range(n):
            for j, (px, py) in enumerate([(1 - x, y), (x, 1 - y), (1 - x, 1 - y)]):
                dst = refs[n + a].at[j]
                copies.append(pltpu.make_async_remote_copy(
                    src_ref=refs[a].at[2 * px + py] if outgoing else dst, dst_ref=dst,
                    send_sem=send_sems.at[3 * a + j], recv_sem=recv_sems.at[3 * a + j],
                    device_id=(px, py, cc), device_id_type=MESH))
        return copies

    plan.n_sems = 3 * n
    return plan


def _in_hbm(t):
    return pltpu.with_memory_space_constraint(t, pltpu.HBM)


def _exchange_start(plan, bufs, after, *, name):
    nb, na = len(bufs), len(after)

    def body(*refs):
        outs = refs[nb + na:]
        for cp in plan(refs[:nb], outs[0], outs[1], True):
            cp.start()
        outs[2 + nb][...] = jnp.zeros_like(outs[2 + nb])

    res = pl.pallas_call(
        body,
        out_shape=(pltpu.SemaphoreType.DMA((plan.n_sems,)), pltpu.SemaphoreType.DMA((plan.n_sems,)),
                   *[pltpu.HBM(t.shape, t.dtype) for t in bufs], jax.ShapeDtypeStruct((8, LANES), F32)),
        in_specs=[HBM_SPEC] * nb + [ANY] * na,
        out_specs=(SEM_SPEC, SEM_SPEC, *[HBM_SPEC] * nb, pl.BlockSpec(memory_space=pltpu.VMEM)),
        input_output_aliases={i: 2 + i for i in range(nb)},
        compiler_params=pltpu.CompilerParams(has_side_effects=DATAFLOW_EFFECT),
        name=name,
    )(*[_in_hbm(t) for t in bufs], *after)
    return plan, res[:2], list(res[2:2 + nb]), res[2 + nb]


def _exchange_wait(flight, after, *, name):
    plan, sems, bufs, _ = flight
    nb = len(bufs)

    def body(*refs):
        send_sems, recv_sems = refs[nb], refs[nb + 1]
        for cp in plan(refs[:nb], send_sems, recv_sems, False):
            cp.wait_recv()
        for cp in plan(refs[:nb], send_sems, recv_sems, True):
            cp.wait_send()

    res = pl.pallas_call(
        body,
        out_shape=tuple(pltpu.HBM(t.shape, t.dtype) for t in bufs),
        in_specs=[HBM_SPEC] * nb + [SEM_SPEC, SEM_SPEC] + [ANY] * len(after),
        out_specs=tuple([HBM_SPEC] * nb),
        input_output_aliases={i: i for i in range(nb)},
        compiler_params=pltpu.CompilerParams(has_side_effects=DATAFLOW_EFFECT),
        name=name,
    )(*bufs, *sems, *after)
    return list(res)


_W_GROUPS = {"even": ("even_w_in", "even_w_out"), "mlp0": ("mlp_w1_0", "mlp_w2_0"),
             "odd": ("odd_w_in", "odd_w_uq", "odd_w_ukv", "odd_w_out"), "mlp1": ("mlp_w1_1", "mlp_w2_1")}


class _Exchanges:
    def __init__(self, shards, gains, n_q, n_kv):
        self.n_q, self.n_kv = n_q, n_kv
        self.mx, self.my, self.mc = _mesh_pos()
        self.dev = 4 * self.mx + 2 * self.my + self.mc
        self.shards = shards
        self.gains = gains
        self.flights, self.gathered, self.grad_blocks, self.grad_names, self.reduced = {}, {}, {}, {}, {}

    def start(self):
        names = _W_GROUPS["even"]
        got = _all_gather([self.shards[n] for n in names] + [self.gains], name="comm_even_gather")
        self.gathered.update(zip(names, got[:-1]))
        self.all_gains = got[-1][:, 0]
        return self._w_begin("mlp0", [got[0]])

    def _w_begin(self, group, after):
        srcs = [self.shards[n] for n in _W_GROUPS[group]]
        lands = [lax.dynamic_update_slice(lax.empty((N_DEV,) + t.shape, t.dtype), t[None], (self.dev, 0, 0)) for t in srcs]
        self.flights[group] = _exchange_start(_plan_own_blocks(len(srcs)), srcs + lands, after, name=f"comm_{group}_own_start")
        return [self.flights[group][3]]

    def _w_turn(self, group, after):
        bufs = _exchange_wait(self.flights[group], after, name=f"comm_{group}_own_wait")
        n = len(bufs) // 2
        self.flights[group] = _exchange_start(_plan_pass_on(n), bufs[n:], [], name=f"comm_{group}_pass_start")
        return [self.flights[group][3]]

    def _w_end(self, group, after):
        self.gathered.update(zip(_W_GROUPS[group], _exchange_wait(self.flights.pop(group), after, name=f"comm_{group}_pass_wait")))

    def weights(self, group):
        return {n: self.gathered[n] for n in _W_GROUPS[group]}

    def norm_gains(self):
        return (self.all_gains[:, :self.n_q].reshape(-1), self.all_gains[:, self.n_q:self.n_q + self.n_kv].reshape(-1))

    def grads(self, group, named_blocks):
        self.grad_names[group] = [n for n, _ in named_blocks]
        self.grad_blocks[group] = [t for _, t in named_blocks]

    def _g_begin(self, group, after):
        blocks = self.grad_blocks[group]
        lands = [lax.empty((4,) + t.shape[1:], t.dtype) for t in blocks]
        self.flights[group] = _exchange_start(_plan_to_sibling(len(blocks)), blocks + lands, after, name=f"comm_{group}_sib_start")
        return [self.flights[group][3]]

    def _g_turn(self, group, after):
        bufs = _exchange_wait(self.flights[group], after, name=f"comm_{group}_sib_wait")
        n = len(bufs) // 2
        partial = [_pair_sum(a, b, self.mc, name=f"pair_sum_{nm}") for nm, a, b in zip(self.grad_names[group], bufs[:n], bufs[n:])]
        lands = [lax.empty((3,) + t.shape[1:], t.dtype) for t in partial]
        self.flights[group] = _exchange_start(_plan_to_chips(n), partial + lands, [], name=f"comm_{group}_chips_start")
        return [self.flights[group][3]]

    def _g_end(self, group, after):
        bufs = _exchange_wait(self.flights.pop(group), after, name=f"comm_{group}_chips_wait")
        n = len(bufs) // 2
        self.reduced.update(zip(self.grad_names[group], zip(bufs[:n], bufs[n:])))

    _SCHEDULE = {
        "even_out": (("w_turn", "mlp0"), ("w_begin", "odd")),
        "ln1_l0": (("w_end", "mlp0"),),
        "ln2_l0": (("w_turn", "odd"), ("w_begin", "mlp1"), ("w_end", "odd")),
        "odd_out": (("w_turn", "mlp1"),),
        "ln1_l1": (("w_end", "mlp1"),),
        "dw_l1": (("g_begin", "mlp1"),),
        "ln1_bwd_l1": (("g_turn", "mlp1"),),
        "odd_in_dw": (("g_begin", "odd"),),
        "mlp2_dx_l0": (("g_end", "mlp1"), ("g_turn", "odd")),
        "dw_l0": (("g_begin", "mlp0"),),
        "ln1_bwd_l0": (("g_end", "odd"), ("g_turn", "mlp0")),
        "even_in_dw": (("g_begin", "even"),),
        "even_in_dx": (("g_end", "mlp0"), ("g_turn", "even")),
        "finish": (("g_end", "even"),),
    }

    def sync(self, tag, after):
        latest, started = list(after), []
        for what, group in self._SCHEDULE[tag]:
            out = getattr(self, "_" + what)(group, latest)
            if out:
                latest = started = out
        return started


def _alibi(n):
    return 2.0 ** (-8.0 * np.arange(1, n + 1, dtype=np.float32) / n)


def _heads(t, n):
    s = t.shape[0]
    return t.reshape(s, n, t.shape[1] // n).transpose(1, 0, 2)


def _unheads(t):
    n, s, dh = t.shape
    return t.transpose(1, 0, 2).reshape(s, n * dh)


def _to_strided(t, d):
    s, x = t.shape
    return t if d == 1 else t.reshape(s // d, d, x).transpose(1, 0, 2).reshape(s, x)


def _from_strided(t, d):
    s, x = t.shape
    return t if d == 1 else t.reshape(d, s // d, x).transpose(1, 0, 2).reshape(s, x)


def _true_columns(t, c, n_pad):
    r = t.shape[1]
    w = t[:, :, :c].transpose(1, 0, 2).reshape(r, N_DEV * c)
    return jnp.pad(w, ((0, 0), (0, n_pad - N_DEV * c)))


def _column_blocks(t, c, cp):
    r = t.shape[0]
    b = t[:, :N_DEV * c].reshape(r, N_DEV, c).transpose(1, 0, 2)
    return jnp.pad(b, ((0, 0), (0, 0), (0, cp - c)))


def _stack_rows(t, nq):
    return t.reshape(nq, BLK, A_KV_HEADS, A_GROUP, HEAD_DIM).transpose(2, 0, 3, 1, 4).reshape(
        A_KV_HEADS, 1, nq * A_GROUP * BLK, HEAD_DIM)


def _unstack_rows(t, nq):
    return t.reshape(A_KV_HEADS, nq, A_GROUP, BLK, HEAD_DIM).transpose(1, 3, 0, 2, 4).reshape(
        nq * BLK, A_Q_W)


def _lead_block(t):
    return jnp.pad(t, ((0, 0), (BLK, 0), (0, 0)))


def _columns(t):
    return t.transpose(1, 0, 2).reshape(t.shape[1], -1)


def _rows(t):
    return t.reshape(-1, t.shape[2])


def _by_column_block(t):
    return t.reshape(t.shape[0], N_DEV, -1).transpose(1, 0, 2)


def _local_step(x0, target, comm, sinks, ln1_g, ln1_b, ln2_g, ln2_b):
    s, d = x0.shape
    scale_h = 1.0 / math.sqrt(HEAD_DIM)
    scale_d = 1.0 / math.sqrt(D_NOPE + D_ROPE)
    nq = s // BLK
    even_c, odd_c = EVEN_IN // N_DEV, ODD_IN // N_DEV
    bf = lambda t: t.astype(MXU_DTYPE)
    blocks = dict(b_blocks=True)

    tok = comm.start()
    w_even = comm.weights("even")
    even_cp, even_n = w_even["even_w_in"].shape[2], -(-EVEN_IN // 1024) * 1024
    w_even_in, w_even_out = _true_columns(w_even["even_w_in"], even_c, even_n), _columns(w_even["even_w_out"])
    x0b = bf(x0)
    h_e = _mm(x0b, w_even_in, out_dtypes=(MXU_DTYPE,), name="even_in_fwd", deps=tok)
    qa = _stack_rows(h_e[:, :A_Q_W], nq)
    ka = _lead_block(_heads(h_e[:, A_Q_W:A_Q_W + A_KV_W], A_KV_HEADS))[:, None]
    va = _lead_block(_heads(h_e[:, A_Q_W + A_KV_W:A_Q_W + 2 * A_KV_W], A_KV_HEADS))[:, None]
    rows_a = A_GROUP * BLK
    slope_a = jnp.asarray(np.repeat(_alibi(A_Q_HEADS).reshape(A_KV_HEADS, A_GROUP), BLK, axis=1).reshape(
        A_KV_HEADS, 1, rows_a, 1))
    sink_a = jnp.broadcast_to(sinks.reshape(A_KV_HEADS, A_GROUP, 1), (A_KV_HEADS, A_GROUP, BLK)).reshape(
        A_KV_HEADS, 1, rows_a, 1)
    a_cfg = dict(scale=scale_h, n_back=A_WINDOW - 1, bps=nq)
    oa, lse_a = _band_fwd(qa, ka, va, slope_a, sink_a, name="swa_fwd", **a_cfg)
    b_slab, b_cfg, b_out, b_lse = [], [], [], []
    base = A_Q_W + 2 * A_KV_W
    for gi, (window, dil) in enumerate(B_PATTERNS):
        slab = _to_strided(h_e[:, base + gi * 3 * B_W: base + (gi + 1) * 3 * B_W], dil)
        cfg = dict(slopes=_alibi(B_HEADS) * dil, scale=scale_h, n_back=window // dil, bps=nq // dil)
        o, lse = _dil_fwd(slab, name=f"dil{gi}_fwd", **cfg)
        both = _from_strided(jnp.concatenate([o, lse], axis=1), dil)
        b_slab.append(slab)
        b_cfg.append(cfg)
        b_out.append(both[:, :B_W])
        b_lse.append(both[:, B_W:])
    ob, lse_b = _merge(b_out, b_lse, name="dil_merge")
    y_e = bf(jnp.concatenate([_unstack_rows(oa, nq), ob], axis=1))
    mixed = _mm(y_e, w_even_out, name="even_out_fwd")
    tok = comm.sync("even_out", [mixed])
    x0n, x0nb, xh1_0, r1_0 = _ln_fwd(x0, mixed, ln1_g[0], ln1_b[0], name="ln1_fwd_l0", deps=tok)
    comm.sync("ln1_l0", [x0nb])
    w_mlp0 = comm.weights("mlp0")
    w1_0, w2_0 = w_mlp0["mlp_w1_0"], _rows(w_mlp0["mlp_w2_0"])
    act0, hid0 = _mm(x0nb, w1_0, out_dtypes=(MXU_DTYPE, MXU_DTYPE), epilogue=_relu_sq, name="mlp1_fwd_l0", **blocks)
    mlp = _mm(hid0, w2_0, name="mlp2_fwd_l0")
    x1, x1b, xh2_0, r2_0 = _ln_fwd(x0n, mlp, ln2_g[0], ln2_b[0], name="ln2_fwd_l0")

    tok = comm.sync("ln2_l0", [x1b])
    w_odd = comm.weights("odd")
    odd_cp, odd_n = w_odd["odd_w_in"].shape[2], -(-ODD_IN // 1024) * 1024
    w_odd_in, w_uq, w_ukv, w_odd_out = (_true_columns(w_odd["odd_w_in"], odd_c, odd_n), _columns(w_odd["odd_w_uq"]),
                                        _columns(w_odd["odd_w_ukv"]), _rows(w_odd["odd_w_out"]))
    gq, gkv = comm.norm_gains()
    h_o = _mm(x1b, w_odd_in, name="odd_in_fwd", deps=tok)
    qkv_c = bf(h_o[:, :3 * C_W])
    oc, sb_total = _sb_fwd(qkv_c, heads=C_HEADS, scale=scale_h, name="sb_fwd")
    o_cq, o_ckv, o_kr = 3 * C_W, 3 * C_W + D_Q_RANK, 3 * C_W + D_Q_RANK + D_KV_RANK
    cq, ckv, kr = h_o[:, o_cq:o_ckv], h_o[:, o_ckv:o_kr], h_o[:, o_kr:o_kr + D_ROPE]
    ncq, rq = _rms_fwd(cq, gq, name="rms_q_fwd")
    nckv, rkv = _rms_fwd(ckv, gkv, name="rms_kv_fwd")
    lane_pad = LANES - D_NOPE - D_ROPE
    w_uq = jnp.pad(w_uq.reshape(D_Q_RANK, D_HEADS, D_NOPE + D_ROPE), ((0, 0), (0, 0), (0, lane_pad))).reshape(
        D_Q_RANK, D_HEADS * LANES)
    qd = _mm(ncq, w_uq, name="uq_fwd")
    kvd = _mm(nckv, w_ukv, out_dtypes=(MXU_DTYPE,), name="ukv_fwd")
    rope_t = _rope_tables(s, inverse=False)
    krp = _rope(jnp.pad(kr, ((0, 0), (D_NOPE, lane_pad)))[None], rope_t, out_dtype=MXU_DTYPE, name="rope_k_fwd")[0]
    od, lse_d = _mla_fwd(qd, kvd, krp, rope_t, scale=scale_d, name="mla_fwd")
    y_o = bf(jnp.concatenate([oc, od], axis=1))
    mixed = _mm(y_o, w_odd_out, name="odd_out_fwd")
    tok = comm.sync("odd_out", [mixed])
    x1n, x1nb, xh1_1, r1_1 = _ln_fwd(x1, mixed, ln1_g[1], ln1_b[1], name="ln1_fwd_l1", deps=tok)
    comm.sync("ln1_l1", [x1nb])
    w_mlp1 = comm.weights("mlp1")
    w1_1, w2_1 = w_mlp1["mlp_w1_1"], _rows(w_mlp1["mlp_w2_1"])
    act1, hid1 = _mm(x1nb, w1_1, out_dtypes=(MXU_DTYPE, MXU_DTYPE), epilogue=_relu_sq, name="mlp1_fwd_l1", **blocks)
    mlp = _mm(hid1, w2_1, name="mlp2_fwd_l1")
    y, _, xh2_1, r2_1 = _ln_fwd(x1n, mlp, ln2_g[1], ln2_b[1], name="ln2_fwd_l1")
    dy, loss_vec = _loss_head(y, target, name="loss_head")

    def mlp_block_bwd(g_out, layer, w1, w2, xh2, r2, xh1, r1, act, hid, xnb):
        du2, du2b, dg2, db2 = _ln_bwd(g_out, xh2, r2, ln2_g[layer], name=f"ln2_bwd_l{layer}")
        dpre = _mm(du2b, w2, nt=True, out_dtypes=(MXU_DTYPE,), epilogue=_relu_sq_grad, extra=act, name=f"mlp2_dx_l{layer}")
        tok = comm.sync("mlp2_dx_l0", [dpre]) if layer == 0 else []
        dw2 = _mm(hid, du2b, ta=True, out_dtypes=(BF16,), name=f"mlp2_dw_l{layer}", deps=tok)
        dw1 = _mm(xnb, dpre, ta=True, out_blocks=True, out_dtypes=(BF16,), name=f"mlp1_dw_l{layer}")
        comm.grads(f"mlp{layer}", [(f"mlp_w1_{layer}", dw1), (f"mlp_w2_{layer}", dw2.reshape(N_DEV, -1, d))])
        tok = comm.sync(f"dw_l{layer}", [dw1])
        dxn = _mm(dpre, w1, nt=True, epilogue=_add_alpha, extra=du2, name=f"mlp1_dx_l{layer}", deps=tok, **blocks)
        du1, du1b, dg1, db1 = _ln_bwd(dxn, xh1, r1, ln1_g[layer], name=f"ln1_bwd_l{layer}")
        return du1, du1b, comm.sync(f"ln1_bwd_l{layer}", [du1b]), (dg1, db1, dg2, db2)

    du1, du1b, tok, ln_1 = mlp_block_bwd(dy, 1, w1_1, w2_1, xh2_1, r2_1, xh1_1, r1_1, act1, hid1, x1nb)
    d_odd_out = _mm(y_o, du1b, ta=True, out_dtypes=(BF16,), name="odd_out_dw").reshape(N_DEV, -1, d)
    dy_o = _mm(du1b, w_odd_out, nt=True, name="odd_out_dx", deps=tok)
    dqc, dkc, dvc = _sb_bwd(qkv_c, dy_o, sb_total, heads=C_HEADS, scale=scale_h, name="sb_bwd")
    dqd, dkvd, dkr_pairs = _mla_bwd(qd, kvd, krp, rope_t, dy_o, od, lse_d, do_block0=C_W // LANES, scale=scale_d,
                                    name="mla_bwd")
    _, dkr_sum = _rope(dkr_pairs, _rope_tables(s, inverse=True), out_dtype=F32, head_sum=True, name="rope_k_bwd")
    dqd, dkvd = bf(dqd), bf(dkvd)
    d_uq = _mm(ncq, dqd, ta=True, out_dtypes=(BF16,), name="uq_dw").reshape(D_Q_RANK, D_HEADS, LANES)[:, :, :D_NOPE + D_ROPE].reshape(
        D_Q_RANK, D_HEADS * (D_NOPE + D_ROPE))
    dncq = _mm(dqd, w_uq, nt=True, name="uq_dx")
    d_ukv = _mm(nckv, dkvd, ta=True, out_dtypes=(BF16,), name="ukv_dw")
    dnckv = _mm(dkvd, w_ukv, nt=True, name="ukv_dx")
    dcq, dgq = _rms_bwd(dncq, cq, rq, gq, name="rms_q_bwd")
    dckv, dgkv = _rms_bwd(dnckv, ckv, rkv, gkv, name="rms_kv_bwd")
    dh_o = bf(jnp.concatenate(
        [dqc, dkc, dvc, dcq, dckv, dkr_sum[:, D_NOPE:D_NOPE + D_ROPE], jnp.zeros((s, odd_n - ODD_IN), F32)], axis=1))
    d_odd_in = _column_blocks(_mm(x1b, dh_o, ta=True, out_dtypes=(BF16,), name="odd_in_dw"), odd_c, odd_cp)
    comm.grads("odd", [("odd_w_in", d_odd_in), ("odd_w_uq", _by_column_block(d_uq)),
                       ("odd_w_ukv", _by_column_block(d_ukv)), ("odd_w_out", d_odd_out)])
    tok = comm.sync("odd_in_dw", [d_odd_in])
    dx1 = _mm(dh_o, w_odd_in, nt=True, epilogue=_add_alpha, extra=du1, name="odd_in_dx", deps=tok)

    du1, du1b, tok, ln_0 = mlp_block_bwd(dx1, 0, w1_0, w2_0, xh2_0, r2_0, xh1_0, r1_0, act0, hid0, x0nb)
    d_even_out = _mm(y_e, du1b, ta=True, out_dtypes=(BF16,), name="even_out_dw")
    dy_e = _mm(du1b, w_even_out, nt=True, name="even_out_dx", deps=tok)
    doa = _stack_rows(dy_e[:, :A_Q_W], nq)
    dqa, dka, dva, dsink = _band_bwd(qa, ka, va, doa, oa, lse_a, slope_a, sink_a, name="swa_bwd", **a_cfg)
    pieces = [_unstack_rows(dqa, nq), _unheads(dka[:, 0, BLK:]), _unheads(dva[:, 0, BLK:])]
    pack = jnp.concatenate([dy_e[:, A_Q_W:], ob, lse_b], axis=1)
    for gi, (_, dil) in enumerate(B_PATTERNS):
        grads = _dil_bwd(b_slab[gi], _to_strided(pack, dil), name=f"dil{gi}_bwd", **b_cfg[gi])
        pieces.append(_from_strided(jnp.concatenate(grads, axis=1), dil))
    dh_e = bf(jnp.concatenate(pieces + [jnp.zeros((s, even_n - EVEN_IN), F32)], axis=1))
    d_even_in = _column_blocks(_mm(x0b, dh_e, ta=True, out_dtypes=(BF16,), name="even_in_dw"), even_c, even_cp)
    comm.grads("even", [("even_w_in", d_even_in), ("even_w_out", _by_column_block(d_even_out))])
    tok = comm.sync("even_in_dw", [d_even_in])
    grad_x = _mm(dh_e, w_even_in, nt=True, epilogue=_add_alpha, extra=du1, name="even_in_dx", deps=tok)
    tok = comm.sync("even_in_dx", [grad_x])

    ln = [jnp.concatenate([a, b], axis=0) for a, b in zip(ln_0, ln_1)]
    small = {"ln": ln, "sinks": dsink[:, :, 0].reshape(-1), "gq": dgq[0], "gkv": dgkv[0], "loss": loss_vec[0, :1]}
    return grad_x, small, tok


def kernel(x, even_w_in, even_sinks, even_w_out, odd_w_in, odd_q_norm_g, odd_kv_norm_g, odd_w_uq, odd_w_ukv, odd_w_out, ln1_g, ln1_b, mlp_w1, mlp_w2, ln2_g, ln2_b, loss_target, m_even_w_in, m_even_sinks, m_even_w_out, m_odd_w_in, m_odd_q_norm_g, m_odd_kv_norm_g, m_odd_w_uq, m_odd_w_ukv, m_odd_w_out, m_ln1_g, m_ln1_b, m_mlp_w1, m_mlp_w2, m_ln2_g, m_ln2_b, v_even_w_in, v_even_sinks, v_even_w_out, v_odd_w_in, v_odd_q_norm_g, v_odd_kv_norm_g, v_odd_w_uq, v_odd_w_ukv, v_odd_w_out, v_ln1_g, v_ln1_b, v_mlp_w1, v_mlp_w2, v_ln2_g, v_ln2_b):
    weights = dict(even_w_in=even_w_in, even_sinks=even_sinks, even_w_out=even_w_out, odd_w_in=odd_w_in,
                   odd_q_norm_g=odd_q_norm_g, odd_kv_norm_g=odd_kv_norm_g, odd_w_uq=odd_w_uq, odd_w_ukv=odd_w_ukv,
                   odd_w_out=odd_w_out, ln1_g=ln1_g, ln1_b=ln1_b, mlp_w1=mlp_w1, mlp_w2=mlp_w2, ln2_g=ln2_g, ln2_b=ln2_b)
    mom_m = dict(even_w_in=m_even_w_in, even_sinks=m_even_sinks, even_w_out=m_even_w_out, odd_w_in=m_odd_w_in,
                 odd_q_norm_g=m_odd_q_norm_g, odd_kv_norm_g=m_odd_kv_norm_g, odd_w_uq=m_odd_w_uq, odd_w_ukv=m_odd_w_ukv,
                 odd_w_out=m_odd_w_out, ln1_g=m_ln1_g, ln1_b=m_ln1_b, mlp_w1=m_mlp_w1, mlp_w2=m_mlp_w2, ln2_g=m_ln2_g, ln2_b=m_ln2_b)
    mom_v = dict(even_w_in=v_even_w_in, even_sinks=v_even_sinks, even_w_out=v_even_w_out, odd_w_in=v_odd_w_in,
                 odd_q_norm_g=v_odd_q_norm_g, odd_kv_norm_g=v_odd_kv_norm_g, odd_w_uq=v_odd_w_uq, odd_w_ukv=v_odd_w_ukv,
                 odd_w_out=v_odd_w_out, ln1_g=v_ln1_g, ln1_b=v_ln1_b, mlp_w1=v_mlp_w1, mlp_w2=v_mlp_w2, ln2_g=v_ln2_g, ln2_b=v_ln2_b)
    order = list(weights)
    n_q, n_kv = odd_q_norm_g.shape[1], odd_kv_norm_g.shape[1]

    def lane_padded(t):
        return jnp.pad(t, ((0, 0), (0, _lane_pad(t.shape[1]) - t.shape[1]))).astype(BF16)

    shards = {"even_w_in": lane_padded(even_w_in[0]), "even_w_out": even_w_out[0].astype(BF16),
              "mlp_w1_0": mlp_w1[0].astype(BF16), "mlp_w2_0": mlp_w2[0].astype(BF16),
              "odd_w_in": lane_padded(odd_w_in[0]), "odd_w_uq": odd_w_uq[0].astype(BF16),
              "odd_w_ukv": odd_w_ukv[0].astype(BF16), "odd_w_out": odd_w_out[0].astype(BF16),
              "mlp_w1_1": mlp_w1[1].astype(BF16), "mlp_w2_1": mlp_w2[1].astype(BF16)}
    gains = jnp.concatenate([odd_q_norm_g, odd_kv_norm_g, jnp.zeros((1, LANES - n_q - n_kv), F32)], axis=1)
    comm = _Exchanges(shards, gains, n_q, n_kv)
    dev = comm.dev

    grad_x, small, last_started = _local_step(x[0], loss_target[0], comm, even_sinks[0], ln1_g, ln1_b, ln2_g, ln2_b)

    grads, delta, new_m, new_v = {}, {}, {}, {}

    def update(n):
        g_list = [comm.reduced[f"{n}_0"], comm.reduced[f"{n}_1"]] if n.startswith("mlp") else [comm.reduced[n]]
        grads[n], delta[n], new_m[n], new_v[n] = _adamw(weights[n], g_list, mom_m[n], mom_v[n], name=f"adamw_{n}",
                                                        my_chip=2 * comm.mx + comm.my, deps=last_started)

    early = ("mlp_w1", "mlp_w2", "odd_w_in", "odd_w_uq", "odd_w_ukv", "odd_w_out")
    for n in early:
        update(n)
    comm.sync("finish", [new_v[n] for n in early])
    update("even_w_in")
    update("even_w_out")

    small_parts = [t.reshape(-1) for t in small["ln"]] + [small["sinks"], small["gq"], small["gkv"], small["loss"]]
    small_sizes = [p.shape[0] for p in small_parts]
    n_small = sum(small_sizes)
    small_rows = -(-n_small // (8 * LANES)) * 8
    small_flat = jnp.concatenate(small_parts + [jnp.zeros((small_rows * LANES - n_small,), F32)]).reshape(small_rows, LANES)
    (small_all,) = _all_gather([small_flat], name="comm_small_gather", deps=[grads["even_w_in"]])
    totals = _sum_devices(small_all, name="small_sum").reshape(-1)
    tot, off = [], 0
    for size in small_sizes:
        tot.append(totals[off:off + size])
        off += size
    for i, n in enumerate(("ln1_g", "ln1_b", "ln2_g", "ln2_b")):
        grads[n] = tot[i].reshape(weights[n].shape)
    grads["even_sinks"] = tot[4].reshape(even_sinks.shape)
    grads["odd_q_norm_g"] = lax.dynamic_slice(tot[5], (dev * n_q,), (n_q,)).reshape(odd_q_norm_g.shape)
    grads["odd_kv_norm_g"] = lax.dynamic_slice(tot[6], (dev * n_kv,), (n_kv,)).reshape(odd_kv_norm_g.shape)
    loss = tot[7][0]

    small_names = [n for n in order if n not in early + ("even_w_in", "even_w_out")]
    n_sm = sum(weights[n].size for n in small_names)
    sm_rows = -(-n_sm // (8 * LANES)) * 8

    def pack_small(group):
        flat = [group[n].reshape(-1) for n in small_names]
        return jnp.concatenate(flat + [jnp.zeros((sm_rows * LANES - n_sm,), F32)]).reshape(1, sm_rows, LANES)

    res = _adamw(pack_small(weights), [pack_small(grads)[0]], pack_small(mom_m), pack_small(mom_v), name="adamw_small")
    off = 0
    for n in small_names:
        size = weights[n].size
        delta[n], new_m[n], new_v[n] = (t.reshape(-1)[off:off + size].reshape(weights[n].shape) for t in res[1:])
        off += size

    return (loss, grad_x[None], *[grads[n] for n in order], *[delta[n] for n in order],
            *[new_m[n] for n in order], *[new_v[n] for n in order])
```

```python
import math

import jax
import jax.numpy as jnp
import numpy as np
from jax import lax
from jax.experimental import pallas as pl
from jax.experimental.pallas import tpu as pltpu

F32 = jnp.float32
BF16 = jnp.bfloat16
MXU_DTYPE = BF16

HEAD_DIM = 64
A_Q_HEADS, A_KV_HEADS, A_WINDOW = 16, 2, 128
A_GROUP = A_Q_HEADS // A_KV_HEADS
B_HEADS = 8
B_PATTERNS = ((128, 1), (512, 4), (2048, 16))
C_HEADS = 16
D_HEADS, D_Q_RANK, D_KV_RANK, D_NOPE, D_ROPE, D_V = 16, 512, 256, 64, 32, 64
ROPE_BASE = 10000.0
LN_EPS, RMS_EPS = 1e-5, 1e-6
DEPTH = 2
ALPHA = (2 * DEPTH) ** 0.25
A_Q_W, A_KV_W, B_W = A_Q_HEADS * HEAD_DIM, A_KV_HEADS * HEAD_DIM, B_HEADS * HEAD_DIM
EVEN_IN = A_Q_W + 2 * A_KV_W + 3 * B_W * len(B_PATTERNS)
C_W = C_HEADS * HEAD_DIM
ODD_IN = 3 * C_W + D_Q_RANK + D_KV_RANK + D_ROPE
ADAM_LR, ADAM_B1, ADAM_B2, ADAM_EPS, ADAM_WD, ADAM_STEP = 0.001, 0.9, 0.999, 1e-08, 0.01, 10

N_DEV = 8
LANES = 128
BLK = 128
CAUSAL_TILE = 512
CUM_CHUNK = 256
NEG = -1e30
VMEM_LIMIT = 48 * 1024 * 1024

NN = ((1,), (0,))
NT = ((1,), (1,))
TN = ((0,), (0,))
MESH = pl.DeviceIdType.MESH
ANY = pl.BlockSpec(memory_space=pl.ANY)
HBM_SPEC = pl.BlockSpec(memory_space=pltpu.HBM)
SEM_SPEC = pl.BlockSpec(memory_space=pltpu.SEMAPHORE)
DATAFLOW_EFFECT = pltpu.SideEffectType.DATAFLOW_SIDE_EFFECTING


def _dot(a, b, dims):
    return lax.dot_general(a, b, (dims, ((), ())), preferred_element_type=F32)


def _bdot(a, b, dims):
    return jnp.stack([_dot(a[n], b[n], dims) for n in range(a.shape[0])])


def _params(*sem):
    return pltpu.CompilerParams(dimension_semantics=tuple(sem), vmem_limit_bytes=VMEM_LIMIT)


def _pick(n, cap, mult=LANES):
    if n <= cap:
        return n
    for t in range(cap - cap % mult, 0, -mult):
        if n % t == 0:
            return t
    raise ValueError(f"no tile for {n}")


def _lane_pad(c):
    return -(-c // LANES) * LANES


def _mm(a, b, *, name, nt=False, ta=False, b_blocks=False, out_blocks=False, out_dtypes=(F32,), epilogue=None, extra=None, deps=()):
    m, k = a.shape[::-1] if ta else a.shape
    if b_blocks:
        nb, kin, c = b.shape
        n = kin if nt else nb * c
        k_full = nb * c if nt else kin
    else:
        n, k_full = (b.shape if nt else b.shape[::-1])
    assert k == k_full, (a.shape, b.shape, nt, b_blocks)
    tm = _pick(m, 1024, 8)
    if b_blocks and not nt:
        tn, tk = c, _pick(k, 3072)
    elif b_blocks:
        per_step = max(g for g in (1, 2, 4, 8) if g * c <= 2048)
        tn, tk = _pick(n, 1024), per_step * c
    elif out_blocks:
        tn, tk = n // N_DEV, _pick(k, 3072)
    else:
        tn, tk = _pick(n, 512), _pick(k, 3072)
        if k > tk:
            tn, tk = _pick(n, 1024), _pick(k, 2048)
    nk = k // tk
    n_out = len(out_dtypes)

    def body(*refs):
        a_ref, b_ref = refs[0], refs[1]
        e_ref = refs[2] if extra is not None else None
        first_out = 2 + (extra is not None) + len(deps)
        out_refs = refs[first_out:first_out + n_out]

        def finish(acc):
            e = None if e_ref is None else e_ref[...]
            outs = (acc,) if epilogue is None else epilogue(acc, e)
            for r, o in zip(out_refs, outs):
                r[...] = o.astype(r.dtype).reshape(r.shape)

        if b_blocks and nt:
            part = _dot(a_ref[:, :c], b_ref[0], NT)
            for blk in range(1, per_step):
                part += _dot(a_ref[:, blk * c:(blk + 1) * c], b_ref[blk], NT)
        elif ta:
            part = _dot(a_ref[...], b_ref[...], TN)
        else:
            part = _dot(a_ref[...], b_ref[0] if b_blocks else b_ref[...], NT if nt else NN)
        if nk == 1:
            finish(part)
        else:
            acc_ref = refs[first_out + n_out]
            kk = pl.program_id(2)

            @pl.when(kk == 0)
            def _():
                acc_ref[...] = part

            @pl.when(kk > 0)
            def _():
                acc_ref[...] += part

            @pl.when(kk == nk - 1)
            def _():
                finish(acc_ref[...])

    if b_blocks and not nt:
        b_spec = pl.BlockSpec((1, tk, tn), lambda i, j, kk: (j, kk, 0))
    elif b_blocks:
        b_spec = pl.BlockSpec((per_step, tn, c), lambda i, j, kk: (kk, j, 0))
    elif nt:
        b_spec = pl.BlockSpec((tn, tk), lambda i, j, kk: (j, kk))
    else:
        b_spec = pl.BlockSpec((tk, tn), lambda i, j, kk: (kk, j))
    a_spec = pl.BlockSpec((tk, tm), lambda i, j, kk: (kk, i)) if ta else pl.BlockSpec((tm, tk), lambda i, j, kk: (i, kk))
    in_specs = [a_spec, b_spec]
    ins = [a.astype(MXU_DTYPE), b.astype(MXU_DTYPE)]
    if extra is not None:
        in_specs.append(pl.BlockSpec((tm, tn), lambda i, j, kk: (i, j)))
        ins.append(extra)
    in_specs += [ANY] * len(deps)
    ins += list(deps)
    if out_blocks:
        out_shape = tuple(jax.ShapeDtypeStruct((N_DEV, m, tn), d) for d in out_dtypes)
        out_specs = tuple(pl.BlockSpec((1, tm, tn), lambda i, j, kk: (j, i, 0)) for _ in out_dtypes)
    else:
        out_shape = tuple(jax.ShapeDtypeStruct((m, n), d) for d in out_dtypes)
        out_specs = tuple(pl.BlockSpec((tm, tn), lambda i, j, kk: (i, j)) for _ in out_dtypes)
    outs = pl.pallas_call(
        body,
        out_shape=out_shape,
        grid=(m // tm, n // tn, nk),
        in_specs=in_specs,
        out_specs=out_specs,
        scratch_shapes=[pltpu.VMEM((tm, tn), F32)] if nk > 1 else [],
        compiler_params=_params("parallel", "parallel", "arbitrary"),
        name=name,
    )(*ins)
    return outs[0] if n_out == 1 else outs


def _relu_sq(acc, _):
    act = jnp.maximum(acc, 0.0)
    return act, act * act


def _relu_sq_grad(acc, act):
    return (acc * (2.0 * act.astype(F32)),)


def _add_alpha(acc, du):
    return (acc + ALPHA * du,)


def _ln_fwd(x, mixed, g, b, *, name, deps=()):
    s, d = x.shape
    tr = _pick(s, 256, 8)

    def body(x_ref, m_ref, g_ref, b_ref, *rest):
        y_ref, yb_ref, xh_ref, r_ref = rest[len(deps):]
        u = ALPHA * x_ref[...] + m_ref[...]
        mu = jnp.mean(u, axis=-1, keepdims=True)
        xc = u - mu
        var = jnp.mean(xc * xc, axis=-1, keepdims=True)
        r = lax.rsqrt(var + LN_EPS)
        xh = xc * r
        y = xh * g_ref[...] + b_ref[...]
        y_ref[...] = y
        yb_ref[...] = y.astype(MXU_DTYPE)
        xh_ref[...] = xh
        r_ref[...] = r

    row = pl.BlockSpec((tr, d), lambda i: (i, 0))
    vec = pl.BlockSpec((1, d), lambda i: (0, 0))
    return pl.pallas_call(
        body,
        out_shape=(jax.ShapeDtypeStruct((s, d), F32), jax.ShapeDtypeStruct((s, d), MXU_DTYPE),
                   jax.ShapeDtypeStruct((s, d), F32), jax.ShapeDtypeStruct((s, 1), F32)),
        grid=(s // tr,),
        in_specs=[row, row, vec, vec] + [ANY] * len(deps),
        out_specs=(row, row, row, pl.BlockSpec((tr, 1), lambda i: (i, 0))),
        compiler_params=_params("parallel"),
        name=name,
    )(x, mixed, g.reshape(1, d), b.reshape(1, d), *deps)


def _ln_bwd(dy, xh, r, g, *, name):
    s, d = dy.shape
    tr = _pick(s, 256, 8)

    def body(dy_ref, xh_ref, r_ref, g_ref, du_ref, dub_ref, dg_ref, db_ref):
        dyv, xhv = dy_ref[...], xh_ref[...]
        dxh = dyv * g_ref[...]
        c1 = jnp.mean(dxh, axis=-1, keepdims=True)
        c2 = jnp.mean(dxh * xhv, axis=-1, keepdims=True)
        du = r_ref[...] * (dxh - c1 - xhv * c2)
        du_ref[...] = du
        dub_ref[...] = du.astype(MXU_DTYPE)

        @pl.when(pl.program_id(0) == 0)
        def _():
            dg_ref[...] = jnp.zeros_like(dg_ref)
            db_ref[...] = jnp.zeros_like(db_ref)

        dg_ref[...] += jnp.sum(dyv * xhv, axis=0, keepdims=True)
        db_ref[...] += jnp.sum(dyv, axis=0, keepdims=True)

    row = pl.BlockSpec((tr, d), lambda i: (i, 0))
    vec = pl.BlockSpec((1, d), lambda i: (0, 0))
    return pl.pallas_call(
        body,
        out_shape=(jax.ShapeDtypeStruct((s, d), F32), jax.ShapeDtypeStruct((s, d), MXU_DTYPE),
                   jax.ShapeDtypeStruct((1, d), F32), jax.ShapeDtypeStruct((1, d), F32)),
        grid=(s // tr,),
        in_specs=[row, row, pl.BlockSpec((tr, 1), lambda i: (i, 0)), vec],
        out_specs=(row, row, vec, vec),
        compiler_params=_params("arbitrary"),
        name=name,
    )(dy, xh, r, g.reshape(1, d))


def _rms_fwd(x, g, *, name):
    s, d = x.shape
    tr = _pick(s, 512, 8)

    def body(x_ref, g_ref, y_ref, r_ref):
        xv = x_ref[...]
        r = lax.rsqrt(jnp.mean(xv * xv, axis=-1, keepdims=True) + RMS_EPS)
        y_ref[...] = (xv * r * g_ref[...]).astype(y_ref.dtype)
        r_ref[...] = r

    return pl.pallas_call(
        body,
        out_shape=(jax.ShapeDtypeStruct((s, d), MXU_DTYPE), jax.ShapeDtypeStruct((s, 1), F32)),
        grid=(s // tr,),
        in_specs=[pl.BlockSpec((tr, d), lambda i: (i, 0)), pl.BlockSpec((1, d), lambda i: (0, 0))],
        out_specs=(pl.BlockSpec((tr, d), lambda i: (i, 0)), pl.BlockSpec((tr, 1), lambda i: (i, 0))),
        compiler_params=_params("parallel"),
        name=name,
    )(x, g.reshape(1, d))


def _rms_bwd(dy, x, r, g, *, name):
    s, d = x.shape
    tr = _pick(s, 512, 8)

    def body(dy_ref, x_ref, r_ref, g_ref, dx_ref, dg_ref):
        dyv, rv = dy_ref[...], r_ref[...]
        xn = x_ref[...] * rv
        dxn = dyv * g_ref[...]
        dx_ref[...] = rv * (dxn - xn * jnp.mean(dxn * xn, axis=-1, keepdims=True))

        @pl.when(pl.program_id(0) == 0)
        def _():
            dg_ref[...] = jnp.zeros_like(dg_ref)

        dg_ref[...] += jnp.sum(dyv * xn, axis=0, keepdims=True)

    row = pl.BlockSpec((tr, d), lambda i: (i, 0))
    vec = pl.BlockSpec((1, d), lambda i: (0, 0))
    return pl.pallas_call(
        body,
        out_shape=(jax.ShapeDtypeStruct((s, d), F32), jax.ShapeDtypeStruct((1, d), F32)),
        grid=(s // tr,),
        in_specs=[row, row, pl.BlockSpec((tr, 1), lambda i: (i, 0)), vec],
        out_specs=(row, vec),
        compiler_params=_params("arbitrary"),
        name=name,
    )(dy, x, r, g.reshape(1, d))


def _rope_tables(s, inverse):
    inv_freq = ROPE_BASE ** (-jnp.arange(0, D_ROPE, 2, dtype=F32) / D_ROPE)
    ang = jnp.arange(s, dtype=F32)[:, None] * inv_freq[None, :]
    cos, sin = jnp.cos(ang), jnp.sin(ang)
    if inverse:
        sin = -sin
    half = D_ROPE // 2
    one, zero = jnp.ones((s, D_NOPE), F32), jnp.zeros((s, D_NOPE), F32)
    pad1, pad0 = jnp.ones((s, LANES - D_NOPE - D_ROPE), F32), jnp.zeros((s, LANES - D_NOPE - D_ROPE), F32)
    zh = jnp.zeros((s, half), F32)
    c = jnp.concatenate([one, cos, cos, pad1], axis=1)
    s_lo = jnp.concatenate([zero, -sin, zh, pad0], axis=1)
    s_hi = jnp.concatenate([zero, zh, sin, pad0], axis=1)
    return c, s_lo, s_hi


def _rope(x, tables, *, out_dtype, head_sum=False, name):
    h, s, w = x.shape
    ts = _pick(s, 2048, 8)

    def body(x_ref, c_ref, lo_ref, hi_ref, y_ref, *sum_ref):
        y = _rotate(x_ref[0], c_ref[...], lo_ref[...], hi_ref[...])
        y_ref[0] = y.astype(y_ref.dtype)
        if head_sum:
            @pl.when(pl.program_id(1) == 0)
            def _():
                sum_ref[0][...] = jnp.zeros_like(sum_ref[0])

            sum_ref[0][...] += y

    tab = pl.BlockSpec((ts, w), lambda i, hh: (i, 0))
    blk = pl.BlockSpec((1, ts, w), lambda i, hh: (hh, i, 0))
    out_shape = [jax.ShapeDtypeStruct((h, s, w), out_dtype)]
    out_specs = [blk]
    if head_sum:
        out_shape.append(jax.ShapeDtypeStruct((s, w), F32))
        out_specs.append(tab)
    res = pl.pallas_call(
        body,
        out_shape=tuple(out_shape),
        grid=(s // ts, h),
        in_specs=[blk, tab, tab, tab],
        out_specs=tuple(out_specs),
        compiler_params=_params("parallel", "arbitrary"),
        name=name,
    )(x, *tables)
    return res if head_sum else res[0]


def _band_scores(q, kw, slope, i, *, scale, n_back, bps):
    b, r, _ = q.shape
    sc = _bdot(q, kw, NT) * scale
    shape = (b, r, 2 * BLK)
    row = lax.broadcasted_iota(jnp.int32, shape, 1) & (BLK - 1)
    col = lax.broadcasted_iota(jnp.int32, shape, 2)
    rel = BLK + row - col
    first_col = jnp.where(i % bps == 0, BLK, 0)
    valid = (rel >= 0) & (rel <= n_back) & (col >= first_col)
    return jnp.where(valid, sc - slope * rel.astype(F32), NEG)


def _band_fwd(q, k, v, slope, sink, *, scale, n_back, bps, name):
    g, b, rows, dh = q.shape
    r = slope.shape[2]
    nq = rows // r
    skv = k.shape[2]
    use_sink = sink is not None

    def body(*refs):
        q_ref, k_ref, v_ref, slope_ref = refs[:4]
        sink_ref = refs[4] if use_sink else None
        o_ref, lse_ref = refs[4 + use_sink:]
        i = pl.program_id(1)
        off = pl.multiple_of(i * BLK, BLK)
        kw = k_ref[0, :, pl.ds(off, 2 * BLK), :]
        vw = v_ref[0, :, pl.ds(off, 2 * BLK), :]
        sc = _band_scores(q_ref[0], kw, slope_ref[0], i, scale=scale, n_back=n_back, bps=bps)
        m = jnp.max(sc, axis=-1, keepdims=True)
        if use_sink:
            m = jnp.maximum(m, sink_ref[0])
        p = jnp.exp(sc - m)
        l = jnp.sum(p, axis=-1, keepdims=True)
        if use_sink:
            l = l + jnp.exp(sink_ref[0] - m)
        o_ref[0] = _bdot(p.astype(MXU_DTYPE), vw, NN) / l
        lse_ref[0] = m + jnp.log(l)

    qspec = pl.BlockSpec((1, b, r, dh), lambda gg, i: (gg, 0, i, 0))
    kspec = pl.BlockSpec((1, b, skv, dh), lambda gg, i: (gg, 0, 0, 0))
    rspec = pl.BlockSpec((1, b, r, 1), lambda gg, i: (gg, 0, 0, 0))
    ins = [q, k, v, slope] + ([sink] if use_sink else [])
    return pl.pallas_call(
        body,
        out_shape=(jax.ShapeDtypeStruct((g, b, rows, dh), F32), jax.ShapeDtypeStruct((g, b, rows, 1), F32)),
        grid=(g, nq),
        in_specs=[qspec, kspec, kspec, rspec] + ([rspec] if use_sink else []),
        out_specs=(qspec, pl.BlockSpec((1, b, r, 1), lambda gg, i: (gg, 0, i, 0))),
        compiler_params=_params("parallel", "arbitrary"),
        name=name,
    )(*ins)


def _band_bwd(q, k, v, do, o, lse, slope, sink, *, scale, n_back, bps, name):
    g, b, rows, dh = q.shape
    r = slope.shape[2]
    nq = rows // r
    skv = k.shape[2]
    use_sink = sink is not None
    stacked = r // BLK

    def body(*refs):
        q_ref, k_ref, v_ref, do_ref, o_ref, lse_ref, slope_ref = refs[:7]
        sink_ref = refs[7] if use_sink else None
        dq_ref, dk_ref, dv_ref = refs[7 + use_sink:10 + use_sink]
        i = pl.program_id(1)

        @pl.when(i == 0)
        def _():
            dk_ref[...] = jnp.zeros_like(dk_ref)
            dv_ref[...] = jnp.zeros_like(dv_ref)

        off = pl.multiple_of(i * BLK, BLK)
        qb = q_ref[0]
        kw = k_ref[0, :, pl.ds(off, 2 * BLK), :]
        vw = v_ref[0, :, pl.ds(off, 2 * BLK), :]
        dof = do_ref[0]
        dob = dof.astype(MXU_DTYPE)
        lse_b = lse_ref[0]
        delta = jnp.sum(dof * o_ref[0], axis=-1, keepdims=True)
        sc = _band_scores(qb, kw, slope_ref[0], i, scale=scale, n_back=n_back, bps=bps)
        p = jnp.exp(sc - lse_b)
        ds = (p * (_bdot(dob, vw, NT) - delta) * scale).astype(MXU_DTYPE)
        dq_ref[0] = _bdot(ds, kw, NN)
        dk_ref[0, :, pl.ds(off, 2 * BLK), :] += _bdot(ds, qb, TN)
        dv_ref[0, :, pl.ds(off, 2 * BLK), :] += _bdot(p.astype(MXU_DTYPE), dob, TN)

        if use_sink:
            dsink_ref = refs[10 + use_sink]

            @pl.when(i == 0)
            def _():
                dsink_ref[...] = jnp.zeros_like(dsink_ref)

            contrib = -jnp.exp(sink_ref[0] - lse_b) * delta
            for n in range(stacked):
                part = jnp.sum(contrib[0, n * BLK:(n + 1) * BLK, :], axis=0, keepdims=True)
                dsink_ref[0, n:n + 1, :] += jnp.broadcast_to(part, (1, LANES))

    def qspec(w):
        return pl.BlockSpec((1, b, r, w), lambda gg, i: (gg, 0, i, 0))

    kspec = pl.BlockSpec((1, b, skv, dh), lambda gg, i: (gg, 0, 0, 0))
    rspec = pl.BlockSpec((1, b, r, 1), lambda gg, i: (gg, 0, 0, 0))
    ins = [q, k, v, do, o, lse, slope] + ([sink] if use_sink else [])
    in_specs = [qspec(dh), kspec, kspec, qspec(dh), qspec(dh), qspec(1), rspec] + ([rspec] if use_sink else [])
    out_shape = [jax.ShapeDtypeStruct((g, b, rows, dh), F32), jax.ShapeDtypeStruct((g, b, skv, dh), F32),
                 jax.ShapeDtypeStruct((g, b, skv, dh), F32)]
    out_specs = [qspec(dh), kspec, kspec]
    if use_sink:
        assert b == 1
        out_shape.append(jax.ShapeDtypeStruct((g, stacked, LANES), F32))
        out_specs.append(pl.BlockSpec((1, stacked, LANES), lambda gg, i: (gg, 0, 0)))
    return pl.pallas_call(
        body,
        out_shape=tuple(out_shape),
        grid=(g, nq),
        in_specs=in_specs,
        out_specs=tuple(out_specs),
        compiler_params=_params("parallel", "arbitrary"),
        name=name,
    )(*ins)


def _pair_masks():
    first = lax.broadcasted_iota(jnp.int32, (1, LANES), 1) < HEAD_DIM
    m0 = first.astype(MXU_DTYPE)
    return first, (m0, 1 - m0)


def _dil_window(ref, i):
    prev = pl.multiple_of(jnp.maximum(i - 1, 0) * BLK, BLK)
    cur = pl.multiple_of(i * BLK, BLK)
    return prev, cur, jnp.concatenate([ref[pl.ds(prev, BLK), :], ref[pl.ds(cur, BLK), :]], axis=0)


def _dil_mask(i, n_back, bps):
    row = lax.broadcasted_iota(jnp.int32, (BLK, 2 * BLK), 0)
    col = lax.broadcasted_iota(jnp.int32, (BLK, 2 * BLK), 1)
    rel = BLK + row - col
    first_col = jnp.where(i % bps == 0, BLK, 0)
    return (rel >= 0) & (rel <= n_back) & (col >= first_col), rel.astype(F32)


def _dil_fwd(slab, slopes, *, scale, n_back, bps, name):
    s, w = slab.shape[0], slab.shape[1] // 3

    def body(q_ref, k_ref, v_ref, o_ref, lse_ref):
        i = pl.program_id(0)
        first, masks = _pair_masks()
        _, _, kw = _dil_window(k_ref, i)
        _, _, vw = _dil_window(v_ref, i)
        valid, rel = _dil_mask(i, n_back, bps)
        for p in range(w // LANES):
            cols = slice(p * LANES, (p + 1) * LANES)
            qp, kp, vp = q_ref[:, cols], kw[:, cols], vw[:, cols]
            outs, lses = [], []
            for hh in range(2):
                sc = _dot(qp * masks[hh], kp, NT) * scale - float(slopes[2 * p + hh]) * rel
                sc = jnp.where(valid, sc, NEG)
                m = jnp.max(sc, axis=-1, keepdims=True)
                e = jnp.exp(sc - m)
                l = jnp.sum(e, axis=-1, keepdims=True)
                outs.append(_dot(e.astype(MXU_DTYPE), vp, NN) / l)
                lses.append(m + jnp.log(l))
            o_ref[:, cols] = jnp.where(first, outs[0], outs[1])
            lse_ref[:, cols] = jnp.where(first, lses[0], lses[1])

    blk = pl.BlockSpec((BLK, w), lambda i: (i, 0))
    return pl.pallas_call(
        body,
        out_shape=(jax.ShapeDtypeStruct((s, w), F32), jax.ShapeDtypeStruct((s, w), F32)),
        grid=(s // BLK,),
        in_specs=[blk, pl.BlockSpec((s, w), lambda i: (0, 1)), pl.BlockSpec((s, w), lambda i: (0, 2))],
        out_specs=(blk, blk),
        compiler_params=_params("arbitrary"),
        name=name,
    )(slab, slab, slab)


def _dil_bwd(slab, pack, slopes, *, scale, n_back, bps, name):
    s, w = slab.shape[0], slab.shape[1] // 3

    def body(q_ref, k_ref, v_ref, do_ref, o_ref, lse_ref, dq_ref, dk_ref, dv_ref):
        i = pl.program_id(0)

        @pl.when(i == 0)
        def _():
            dk_ref[...] = jnp.zeros_like(dk_ref)
            dv_ref[...] = jnp.zeros_like(dv_ref)

        first, masks = _pair_masks()
        prev, cur, kw = _dil_window(k_ref, i)
        _, _, vw = _dil_window(v_ref, i)
        valid, rel = _dil_mask(i, n_back, bps)
        for p in range(w // LANES):
            cols = slice(p * LANES, (p + 1) * LANES)
            qp, kp, vp = q_ref[:, cols], kw[:, cols], vw[:, cols]
            dof, lse_p = do_ref[:, cols], lse_ref[:, cols]
            prod = dof * o_ref[:, cols]
            do_b = dof.astype(MXU_DTYPE)
            dqs, dk_add, dv_add = [], None, None
            for hh in range(2):
                qh, doh = qp * masks[hh], do_b * masks[hh]
                delta = jnp.sum(jnp.where(first, prod, 0.0) if hh == 0 else jnp.where(first, 0.0, prod), axis=-1, keepdims=True)
                sc = _dot(qh, kp, NT) * scale - float(slopes[2 * p + hh]) * rel
                e = jnp.exp(jnp.where(valid, sc, NEG) - lse_p[:, hh * HEAD_DIM:hh * HEAD_DIM + 1])
                ds = (e * (_dot(doh, vp, NT) - delta) * scale).astype(MXU_DTYPE)
                dqs.append(_dot(ds, kp, NN))
                dk_h, dv_h = _dot(ds, qh, TN), _dot(e.astype(MXU_DTYPE), doh, TN)
                dk_add = dk_h if dk_add is None else dk_add + dk_h
                dv_add = dv_h if dv_add is None else dv_add + dv_h
            dq_ref[:, cols] = jnp.where(first, dqs[0], dqs[1])
            dk_ref[pl.ds(prev, BLK), cols] += dk_add[:BLK]
            dk_ref[pl.ds(cur, BLK), cols] += dk_add[BLK:]
            dv_ref[pl.ds(prev, BLK), cols] += dv_add[:BLK]
            dv_ref[pl.ds(cur, BLK), cols] += dv_add[BLK:]

    def blk(c):
        return pl.BlockSpec((BLK, w), lambda i: (i, c))

    def whole(c):
        return pl.BlockSpec((s, w), lambda i: (0, c))

    shp = jax.ShapeDtypeStruct((s, w), F32)
    return pl.pallas_call(
        body,
        out_shape=(shp, shp, shp),
        grid=(s // BLK,),
        in_specs=[blk(0), whole(1), whole(2), blk(0), blk(1), blk(2)],
        out_specs=(blk(0), whole(0), whole(0)),
        compiler_params=_params("arbitrary"),
        name=name,
    )(slab, slab, slab, pack, pack, pack)


def _merge(outs, lses, *, name):
    s, w = outs[0].shape
    tr = _pick(s, 512, 8)

    def body(o0, o1, o2, l0, l1, l2, ob_ref, lt_ref):
        a, b, c = l0[...], l1[...], l2[...]
        m = jnp.maximum(jnp.maximum(a, b), c)
        ea, eb, ec = jnp.exp(a - m), jnp.exp(b - m), jnp.exp(c - m)
        den = ea + eb + ec
        ob_ref[...] = (ea / den) * o0[...] + (eb / den) * o1[...] + (ec / den) * o2[...]
        lt_ref[...] = m + jnp.log(den)

    spec = pl.BlockSpec((tr, w), lambda i: (i, 0))
    return pl.pallas_call(
        body,
        out_shape=(jax.ShapeDtypeStruct((s, w), F32), jax.ShapeDtypeStruct((s, w), F32)),
        grid=(s // tr,),
        in_specs=[spec] * 6,
        out_specs=(spec, spec),
        compiler_params=_params("parallel"),
        name=name,
    )(*outs, *lses)


def _tile_iotas(t):
    return lax.broadcasted_iota(jnp.int32, (t, t), 0), lax.broadcasted_iota(jnp.int32, (t, t), 1)


def _rotate(x, c, s_lo, s_hi):
    half = D_ROPE // 2
    return x * c + pltpu.roll(x, LANES - half, 1) * s_lo + pltpu.roll(x, half, 1) * s_hi


def _mla_keys(kv_h, kr_t, first):
    return jnp.where(first, kv_h, kr_t)


def _mla_fwd(qd, kvd, krp, tables, *, scale, name):
    s = qd.shape[0]
    pairs = qd.shape[1] // (2 * LANES)
    t = min(CAUSAL_TILE, s)

    def body(q_ref, kv_ref, kr_ref, c_ref, lo_ref, hi_ref, o_ref, lse_ref):
        i = pl.program_id(1)
        first = lax.broadcasted_iota(jnp.int32, (1, LANES), 1) < HEAD_DIM
        tabs = (c_ref[...], lo_ref[...], hi_ref[...])
        q_heads = [_rotate(q_ref[:, hh * LANES:(hh + 1) * LANES], *tabs).astype(MXU_DTYPE) for hh in range(2)]

        def tile(j, carry, diagonal):
            off = pl.multiple_of(j * t, t)
            kr_t = kr_ref[pl.ds(off, t), :]
            out = []
            for hh in range(2):
                m, l, acc = carry[3 * hh:3 * hh + 3]
                kv_h = kv_ref[pl.ds(off, t), hh * LANES:(hh + 1) * LANES]
                sc = _dot(q_heads[hh], _mla_keys(kv_h, kr_t, first), NT) * scale
                if diagonal:
                    row, col = _tile_iotas(t)
                    sc = jnp.where(row >= col, sc, NEG)
                m_new = jnp.maximum(m, jnp.max(sc, axis=-1, keepdims=True))
                a = jnp.exp(m - m_new)
                p = jnp.exp(sc - m_new)
                out += [m_new, a * l + jnp.sum(p, axis=-1, keepdims=True), a * acc + _dot(p.astype(MXU_DTYPE), kv_h, NN)]
            return tuple(out)

        init = (jnp.full((t, 1), NEG, F32), jnp.zeros((t, 1), F32), jnp.zeros((t, LANES), F32)) * 2
        carry = lax.fori_loop(0, i, lambda j, c: tile(j, c, False), init)
        m0, l0, acc0, m1, l1, acc1 = tile(i, carry, True)
        o_ref[...] = jnp.where(first, pltpu.roll(acc0 / l0, HEAD_DIM, 1), acc1 / l1)
        lse_ref[0] = jnp.where(lax.broadcasted_iota(jnp.int32, (t, 2), 1) == 0, m0 + jnp.log(l0), m1 + jnp.log(l1))

    tab = pl.BlockSpec((t, LANES), lambda p, i: (i, 0))
    return pl.pallas_call(
        body,
        out_shape=(jax.ShapeDtypeStruct((s, pairs * LANES), F32), jax.ShapeDtypeStruct((pairs, s, 2), F32)),
        grid=(pairs, s // t),
        in_specs=[pl.BlockSpec((t, 2 * LANES), lambda p, i: (i, p)), pl.BlockSpec((s, 2 * LANES), lambda p, i: (0, p)),
                  pl.BlockSpec((s, LANES), lambda p, i: (0, 0)), tab, tab, tab],
        out_specs=(pl.BlockSpec((t, LANES), lambda p, i: (i, p)), pl.BlockSpec((1, t, 2), lambda p, i: (p, i, 0))),
        compiler_params=_params("parallel", "arbitrary"),
        name=name,
    )(qd, kvd, krp, *tables)


def _mla_bwd(qd, kvd, krp, tables, do, o, lse, *, do_block0, scale, name):
    s = qd.shape[0]
    pairs = qd.shape[1] // (2 * LANES)
    t = min(CAUSAL_TILE, s)

    def body(q_ref, kv_ref, kr_ref, c_ref, lo_ref, hi_ref, do_ref, o_ref, lse_ref, dq_ref, dkv_ref, dkr_ref):
        i = pl.program_id(1)

        @pl.when(i == 0)
        def _():
            dkv_ref[...] = jnp.zeros_like(dkv_ref)
            dkr_ref[...] = jnp.zeros_like(dkr_ref)

        first = lax.broadcasted_iota(jnp.int32, (1, LANES), 1) < HEAD_DIM
        tabs = (c_ref[...], lo_ref[...], hi_ref[...])
        q_heads = [_rotate(q_ref[:, hh * LANES:(hh + 1) * LANES], *tabs).astype(MXU_DTYPE) for hh in range(2)]
        dof = do_ref[...]
        prod = dof * o_ref[...]
        deltas = [jnp.sum(jnp.where(first, prod, 0.0), axis=-1, keepdims=True),
                  jnp.sum(jnp.where(first, 0.0, prod), axis=-1, keepdims=True)]
        do_heads = [jnp.where(first, 0.0, pltpu.roll(dof, HEAD_DIM, 1)).astype(MXU_DTYPE),
                    jnp.where(first, 0.0, dof).astype(MXU_DTYPE)]
        lses = [lse_ref[0][:, hh:hh + 1] for hh in range(2)]

        def tile(j, carry, diagonal):
            off = pl.multiple_of(j * t, t)
            kr_t = kr_ref[pl.ds(off, t), :]
            out, dkr_add = [], None
            for hh in range(2):
                kv_h = kv_ref[pl.ds(off, t), hh * LANES:(hh + 1) * LANES]
                k_h = _mla_keys(kv_h, kr_t, first)
                sc = _dot(q_heads[hh], k_h, NT) * scale
                if diagonal:
                    row, col = _tile_iotas(t)
                    sc = jnp.where(row >= col, sc, NEG)
                p = jnp.exp(sc - lses[hh])
                ds = (p * (_dot(do_heads[hh], kv_h, NT) - deltas[hh]) * scale).astype(MXU_DTYPE)
                dk_full = _dot(ds, q_heads[hh], TN)
                dv_full = _dot(p.astype(MXU_DTYPE), do_heads[hh], TN)
                dkv_ref[pl.ds(off, t), hh * LANES:(hh + 1) * LANES] += jnp.where(first, dk_full, dv_full)
                rot = jnp.where(first, 0.0, dk_full)
                dkr_add = rot if dkr_add is None else dkr_add + rot
                out.append(carry[hh] + _dot(ds, k_h, NN))
            dkr_ref[0, pl.ds(off, t), :] += dkr_add
            return tuple(out)

        zacc = jnp.zeros((t, LANES), F32)
        carry = lax.fori_loop(0, i, lambda j, c: tile(j, c, False), (zacc, zacc))
        dq_heads = tile(i, carry, True)
        for hh in range(2):
            dq_ref[:, hh * LANES:(hh + 1) * LANES] = _rotate(dq_heads[hh], tabs[0], -tabs[1], -tabs[2])

    tab = pl.BlockSpec((t, LANES), lambda p, i: (i, 0))
    qspec = pl.BlockSpec((t, 2 * LANES), lambda p, i: (i, p))
    kvspec = pl.BlockSpec((s, 2 * LANES), lambda p, i: (0, p))
    return pl.pallas_call(
        body,
        out_shape=(jax.ShapeDtypeStruct(qd.shape, F32), jax.ShapeDtypeStruct(kvd.shape, F32),
                   jax.ShapeDtypeStruct((pairs, s, LANES), F32)),
        grid=(pairs, s // t),
        in_specs=[qspec, kvspec, pl.BlockSpec((s, LANES), lambda p, i: (0, 0)), tab, tab, tab,
                  pl.BlockSpec((t, LANES), lambda p, i: (i, do_block0 + p)), pl.BlockSpec((t, LANES), lambda p, i: (i, p)),
                  pl.BlockSpec((1, t, 2), lambda p, i: (p, i, 0))],
        out_specs=(qspec, kvspec, pl.BlockSpec((1, s, LANES), lambda p, i: (p, 0, 0))),
        compiler_params=_params("parallel", "arbitrary"),
        name=name,
    )(qd, kvd, krp, *tables, do, o, lse)


def _split_cumsum(x, tri, terms=2):
    hi = x.astype(BF16)
    if terms == 1:
        return _dot(hi, tri, NN)
    lo = (x - hi.astype(F32)).astype(BF16)
    return _dot(hi, tri, NN) + _dot(lo, tri, NN)


def _chunked_cumsum(x, tri, run, *, reverse, negate=False, terms=2):
    c = tri.shape[0]
    n = x.shape[1] // c
    parts = [None] * n
    for idx in (reversed(range(n)) if reverse else range(n)):
        xc = x[:, idx * c:(idx + 1) * c]
        sums = _split_cumsum(xc, tri, terms)
        parts[idx] = (-run) - sums if negate else run + sums
        run = run + jnp.sum(xc, axis=-1, keepdims=True)
    return (parts[0] if n == 1 else jnp.concatenate(parts, axis=1)), run


def _sb_logs(z):
    l1 = jnp.log(1.0 + jnp.exp(-jnp.abs(z)))
    return jnp.minimum(z, 0.0) - l1, -jnp.maximum(z, 0.0) - l1


def _scaled_query_heads(q, masks, scale):
    assert math.log2(scale).is_integer(), scale
    return [q * (m * scale).astype(q.dtype) for m in masks]


def _sb_fwd(qkv, *, heads, scale, name):
    s = qkv.shape[0]
    pairs = heads * HEAD_DIM // LANES
    t = min(CAUSAL_TILE, s)
    cc = min(CUM_CHUNK, t)

    def body(q_ref, k_ref, v_ref, o_ref, t_ref):
        i = pl.program_id(1)
        first, masks = _pair_masks()
        q_heads = _scaled_query_heads(q_ref[...], masks, scale)
        crow, ccol = _tile_iotas(cc)
        after = (crow > ccol).astype(BF16)

        def tile(j, carry, diagonal):
            off = pl.multiple_of(j * t, t)
            kb = k_ref[pl.ds(off, t), :]
            vb = v_ref[pl.ds(off, t), :]
            if diagonal:
                row, col = _tile_iotas(t)
                strict = row > col
            out = []
            for hh in range(2):
                run, acc = carry[2 * hh], carry[2 * hh + 1]
                log_beta, log_keep = _sb_logs(_dot(q_heads[hh], kb, NT))
                if diagonal:
                    log_keep = jnp.where(strict, log_keep, 0.0)
                a, run = _chunked_cumsum(log_keep, after, run, reverse=True)
                w = jnp.exp(log_beta + a)
                if diagonal:
                    w = jnp.where(strict, w, 0.0)
                out += [run, acc + _dot(w.astype(MXU_DTYPE), vb, NN)]
            return tuple(out)

        zero, zacc = jnp.zeros((t, 1), F32), jnp.zeros((t, LANES), F32)
        carry = tile(i, (zero, zacc, zero, zacc), True)
        run0, acc0, run1, acc1 = lax.fori_loop(0, i, lambda jj, c: tile(i - 1 - jj, c, False), carry)
        o_ref[...] = jnp.where(first, acc0, acc1)
        t_ref[0] = jnp.where(lax.broadcasted_iota(jnp.int32, (t, 2), 1) == 0, run0, run1)

    return pl.pallas_call(
        body,
        out_shape=(jax.ShapeDtypeStruct((s, heads * HEAD_DIM), F32), jax.ShapeDtypeStruct((pairs, s, 2), F32)),
        grid=(pairs, s // t),
        in_specs=[pl.BlockSpec((t, LANES), lambda p, i: (i, p)),
                  pl.BlockSpec((s, LANES), lambda p, i: (0, pairs + p)),
                  pl.BlockSpec((s, LANES), lambda p, i: (0, 2 * pairs + p))],
        out_specs=(pl.BlockSpec((t, LANES), lambda p, i: (i, p)), pl.BlockSpec((1, t, 2), lambda p, i: (p, i, 0))),
        compiler_params=_params("parallel", "arbitrary"),
        name=name,
    )(qkv, qkv, qkv)


def _sb_bwd(qkv, do, total, *, heads, scale, name):
    s = qkv.shape[0]
    pairs = heads * HEAD_DIM // LANES
    t = min(CAUSAL_TILE, s)
    cc = min(CUM_CHUNK, t)

    def body(q_ref, k_ref, v_ref, do_ref, t_ref, dq_ref, dk_ref, dv_ref):
        i = pl.program_id(1)

        @pl.when(i == 0)
        def _():
            dk_ref[...] = jnp.zeros_like(dk_ref)
            dv_ref[...] = jnp.zeros_like(dv_ref)

        first, masks = _pair_masks()
        q_heads = _scaled_query_heads(q_ref[...], masks, scale)
        do_b = do_ref[...].astype(MXU_DTYPE)
        do_heads = [do_b * m for m in masks]
        tots = [t_ref[0][:, hh:hh + 1] for hh in range(2)]
        crow, ccol = _tile_iotas(cc)
        upto = (crow <= ccol).astype(BF16)
        before = (crow < ccol).astype(BF16)

        def tile(j, carry, diagonal):
            off = pl.multiple_of(j * t, t)
            kb = k_ref[pl.ds(off, t), :]
            vb = v_ref[pl.ds(off, t), :]
            if diagonal:
                row, col = _tile_iotas(t)
                strict = row > col
            out, dk_add, dv_add = [], None, None
            for hh in range(2):
                run_keep, run_g, dq_acc = carry[3 * hh:3 * hh + 3]
                log_beta, log_keep = _sb_logs(_dot(q_heads[hh], kb, NT))
                keep = jnp.exp(log_keep)
                if diagonal:
                    log_keep = jnp.where(strict, log_keep, 0.0)
                a, run_keep = _chunked_cumsum(log_keep, upto, run_keep - tots[hh], reverse=False, negate=True)
                run_keep = run_keep + tots[hh]
                w = jnp.exp(log_beta + a)
                if diagonal:
                    w = jnp.where(strict, w, 0.0)
                g = w * _dot(do_heads[hh], vb, NT)
                prefix, run_g = _chunked_cumsum(g, before, run_g, reverse=False, terms=1)
                dz = g * keep - (1.0 - keep) * prefix
                if diagonal:
                    dz = jnp.where(strict, dz, 0.0)
                dz = dz.astype(MXU_DTYPE)
                dk_h = _dot(dz, q_heads[hh], TN)
                dv_h = _dot(w.astype(MXU_DTYPE), do_heads[hh], TN)
                dk_add = dk_h if dk_add is None else dk_add + dk_h
                dv_add = dv_h if dv_add is None else dv_add + dv_h
                out += [run_keep, run_g, dq_acc + _dot(dz, kb, NN)]
            dk_ref[pl.ds(off, t), :] += dk_add
            dv_ref[pl.ds(off, t), :] += dv_add
            return tuple(out)

        zero, zacc = jnp.zeros((t, 1), F32), jnp.zeros((t, LANES), F32)
        carry = lax.fori_loop(0, i, lambda j, c: tile(j, c, False), (zero, zero, zacc, zero, zero, zacc))
        res = tile(i, carry, True)
        dq_ref[...] = jnp.where(first, res[2], res[5]) * scale

    qspec = pl.BlockSpec((t, LANES), lambda p, i: (i, p))
    shp = jax.ShapeDtypeStruct((s, heads * HEAD_DIM), F32)
    return pl.pallas_call(
        body,
        out_shape=(shp, shp, shp),
        grid=(pairs, s // t),
        in_specs=[qspec, pl.BlockSpec((s, LANES), lambda p, i: (0, pairs + p)),
                  pl.BlockSpec((s, LANES), lambda p, i: (0, 2 * pairs + p)), qspec,
                  pl.BlockSpec((1, t, 2), lambda p, i: (p, i, 0))],
        out_specs=(qspec, pl.BlockSpec((s, LANES), lambda p, i: (0, p)), pl.BlockSpec((s, LANES), lambda p, i: (0, p))),
        compiler_params=_params("parallel", "arbitrary"),
        name=name,
    )(qkv, qkv, qkv, do, total)


def _loss_head(y, target, *, name):
    s, d = y.shape
    tr = _pick(s, 256, 8)

    def body(y_ref, t_ref, dy_ref, loss_ref):
        err = y_ref[...] - t_ref[...]
        dy_ref[...] = err * (1.0 / d)

        @pl.when(pl.program_id(0) == 0)
        def _():
            loss_ref[...] = jnp.zeros_like(loss_ref)

        per_tok = jnp.mean(err * err, axis=-1, keepdims=True)
        loss_ref[...] += 0.5 * jnp.sum(per_tok, axis=0, keepdims=True)

    row = pl.BlockSpec((tr, d), lambda i: (i, 0))
    return pl.pallas_call(
        body,
        out_shape=(jax.ShapeDtypeStruct((s, d), F32), jax.ShapeDtypeStruct((1, LANES), F32)),
        grid=(s // tr,),
        in_specs=[row, row],
        out_specs=(row, pl.BlockSpec((1, LANES), lambda i: (0, 0))),
        compiler_params=_params("arbitrary"),
        name=name,
    )(y, target)


def _adamw(w, grads, m, v, *, name, my_chip=None, deps=()):
    nl, r, c = w.shape
    pieces = my_chip is not None
    cp = (grads[0][0] if pieces else grads[0]).shape[-1]
    tr = _pick(r, min(256, max(16, 262144 // cp)), 8)
    per = 4 if pieces else 1

    def body(chip_ref, *refs):
        w_ref, m_ref, v_ref = refs[:3]
        g_refs = refs[3:3 + per * nl]
        g_out, d_ref, m2_ref, v2_ref = refs[3 + per * nl + len(deps):]

        def grad(n):
            if not pieces:
                return g_refs[n][:, :c]
            own, r0, r1, r2 = (t[0, :, :c].astype(F32) for t in g_refs[4 * n:4 * n + 4])
            return ((own + r0) + r1) + r2

        layer = pl.program_id(0)
        gv = grad(0)
        for n in range(1, nl):
            gv = jnp.where(layer == n, grad(n), gv)
        m2 = ADAM_B1 * m_ref[0] + (1.0 - ADAM_B1) * gv
        v2 = ADAM_B2 * v_ref[0] + (1.0 - ADAM_B2) * (gv * gv)
        m_hat = m2 / (1.0 - ADAM_B1 ** ADAM_STEP)
        v_hat = v2 / (1.0 - ADAM_B2 ** ADAM_STEP)
        g_out[0] = gv
        d_ref[0] = -ADAM_LR * (m_hat / (jnp.sqrt(v_hat) + ADAM_EPS) + ADAM_WD * w_ref[0])
        m2_ref[0] = m2
        v2_ref[0] = v2

    blk = pl.BlockSpec((1, tr, c), lambda l, i, chip_ref: (l, i, 0))
    if pieces:
        g_specs = [pl.BlockSpec((1, tr, cp), lambda l, i, chip_ref: (chip_ref[0], i, 0))]
        g_specs += [pl.BlockSpec((1, tr, cp), lambda l, i, chip_ref, k=k: (k, i, 0)) for k in range(3)]
        g_ins = [t for partial, recv in grads for t in (partial, recv, recv, recv)]
        chip = my_chip.reshape(1).astype(jnp.int32)
    else:
        g_specs, g_ins, chip = [pl.BlockSpec((tr, cp), lambda l, i, chip_ref: (i, 0))], list(grads), jnp.zeros((1,), jnp.int32)
    shp = jax.ShapeDtypeStruct((nl, r, c), F32)
    grid_spec = pltpu.PrefetchScalarGridSpec(
        num_scalar_prefetch=1,
        grid=(nl, r // tr),
        in_specs=[blk, blk, blk] + g_specs * nl + [ANY] * len(deps),
        out_specs=(blk, blk, blk, blk),
    )
    return pl.pallas_call(
        body,
        out_shape=(shp, shp, shp, shp),
        grid_spec=grid_spec,
        compiler_params=_params("parallel", "parallel"),
        name=name,
    )(chip, w, m, v, *g_ins, *deps)


def _pair_sum(mine, recv, my_c, *, name):
    _, r, c = mine.shape
    tr = _pick(r, 512, 16)

    def body(c_ref, a_ref, b_ref, o_ref):
        o_ref[0] = (a_ref[0].astype(F32) + b_ref[0].astype(F32)).astype(o_ref.dtype)

    grid_spec = pltpu.PrefetchScalarGridSpec(
        num_scalar_prefetch=1,
        grid=(4, r // tr),
        in_specs=[pl.BlockSpec((1, tr, c), lambda kk, i, c_ref: (2 * kk + c_ref[0], i, 0)),
                  pl.BlockSpec((1, tr, c), lambda kk, i, c_ref: (kk, i, 0))],
        out_specs=pl.BlockSpec((1, tr, c), lambda kk, i, c_ref: (kk, i, 0)),
    )
    return pl.pallas_call(
        body,
        out_shape=jax.ShapeDtypeStruct((4, r, c), mine.dtype),
        grid_spec=grid_spec,
        compiler_params=_params("parallel", "parallel"),
        name=name,
    )(my_c.reshape(1).astype(jnp.int32), mine, recv)


def _sum_devices(stack, *, name):
    n, r, c = stack.shape

    def body(s_ref, o_ref):
        acc = s_ref[0]
        for dev in range(1, n):
            acc = acc + s_ref[dev]
        o_ref[...] = acc

    return pl.pallas_call(
        body,
        out_shape=jax.ShapeDtypeStruct((r, c), F32),
        in_specs=[pl.BlockSpec(memory_space=pltpu.VMEM)],
        out_specs=pl.BlockSpec(memory_space=pltpu.VMEM),
        name=name,
    )(stack)


def _mesh_pos():
    return lax.axis_index("x"), lax.axis_index("y"), lax.axis_index("c")


def _all_gather(shards, *, name, deps=()):
    n = len(shards)

    def body(*refs):
        x_refs, out_refs = refs[:n], refs[n + len(deps):2 * n + len(deps)]
        send_sems, recv_sems, local_sems = refs[2 * n + len(deps):]
        x, y, cc = _mesh_pos()
        me, sibling = (x, y, cc), (x, y, 1 - cc)
        flip_x, flip_y = cc, 1 - cc
        first = (x + flip_x - 2 * x * flip_x, y + flip_y - 2 * y * flip_y)
        other = (x + flip_y - 2 * x * flip_y, y + flip_x - 2 * y * flip_x)
        diagonal = (1 - x, 1 - y)

        def rows(a, px, py, pc):
            return out_refs[a].at[4 * px + 2 * py + pc]

        def copy(a, kk, block, to, src=None):
            return pltpu.make_async_remote_copy(
                src_ref=rows(a, *block) if src is None else src, dst_ref=rows(a, *block),
                send_sem=send_sems.at[7 * a + kk], recv_sem=recv_sems.at[7 * a + kk],
                device_id=to, device_id_type=MESH)

        sends, own = [], []
        for a in range(n):
            own.append(pltpu.make_async_copy(x_refs[a], rows(a, *me), local_sems.at[a]))
            own[a].start()
            out = [copy(a, 0, me, sibling, src=x_refs[a]), copy(a, 1, me, (*first, cc), src=x_refs[a]),
                   copy(a, 2, me, (*other, cc), src=x_refs[a])]
            for cp in out:
                cp.start()
            sends += out
        for a in range(n):
            copy(a, 1, (*first, cc), me).wait_recv()
            out = [copy(a, 3, (*first, cc), (*other, cc)), copy(a, 4, (*first, cc), sibling)]
            for cp in out:
                cp.start()
            copy(a, 2, (*other, cc), me).wait_recv()
            out.append(copy(a, 5, (*other, cc), sibling))
            out[2].start()
            sends += out
        for a in range(n):
            copy(a, 3, (*diagonal, cc), me).wait_recv()
            passed = copy(a, 6, (*diagonal, cc), sibling)
            passed.start()
            sends.append(passed)
        for a in range(n):
            copy(a, 0, sibling, me).wait_recv()
            copy(a, 4, (*other, 1 - cc), me).wait_recv()
            copy(a, 5, (*first, 1 - cc), me).wait_recv()
            copy(a, 6, (*diagonal, 1 - cc), me).wait_recv()
        for cp in sends:
            cp.wait_send()
        for cp in own:
            cp.wait()

    return pl.pallas_call(
        body,
        out_shape=tuple(jax.ShapeDtypeStruct((N_DEV,) + t.shape, t.dtype) for t in shards),
        in_specs=[ANY] * (n + len(deps)),
        out_specs=tuple([ANY] * n),
        scratch_shapes=[pltpu.SemaphoreType.DMA((7 * n,)), pltpu.SemaphoreType.DMA((7 * n,)),
                        pltpu.SemaphoreType.DMA((n,))],
        name=name,
    )(*shards, *deps)


def _plan_own_blocks(n):
    def plan(refs, send_sems, recv_sems, outgoing):
        x, y, cc = _mesh_pos()
        peers = [(x, y, 1 - cc), (1 - x, y, cc), (x, 1 - y, cc), (1 - x, 1 - y, cc)]
        copies = []
        for a in range(n):
            land = refs[n + a]
            for kk, (px, py, pc) in enumerate(peers):
                block = (x, y, cc) if outgoing else (px, py, pc)
                rows = land.at[4 * block[0] + 2 * block[1] + block[2]]
                copies.append(pltpu.make_async_remote_copy(
                    src_ref=refs[a] if outgoing else rows, dst_ref=rows, send_sem=send_sems.at[4 * a + kk],
                    recv_sem=recv_sems.at[4 * a + kk], device_id=(px, py, pc), device_id_type=MESH))
        return copies

    plan.n_sems = 4 * n
    return plan


def _plan_pass_on(n):
    def plan(refs, send_sems, recv_sems, outgoing):
        x, y, cc = _mesh_pos()
        copies = []
        for a in range(n):
            for j, (px, py) in enumerate([(1 - x, y), (x, 1 - y), (1 - x, 1 - y)]):
                rows = refs[a].at[4 * px + 2 * py + (cc if outgoing else 1 - cc)]
                copies.append(pltpu.make_async_remote_copy(
                    src_ref=rows, dst_ref=rows, send_sem=send_sems.at[3 * a + j], recv_sem=recv_sems.at[3 * a + j],
                    device_id=(x, y, 1 - cc), device_id_type=MESH))
        return copies

    plan.n_sems = 3 * n
    return plan


def _plan_to_sibling(n):
    def plan(refs, send_sems, recv_sems, outgoing):
        x, y, cc = _mesh_pos()
        copies = []
        for a in range(n):
            for chip in range(4):
                dst = refs[n + a].at[chip]
                copies.append(pltpu.make_async_remote_copy(
                    src_ref=refs[a].at[2 * chip + (1 - cc)] if outgoing else dst, dst_ref=dst,
                    send_sem=send_sems.at[4 * a + chip], recv_sem=recv_sems.at[4 * a + chip],
                    device_id=(x, y, 1 - cc), device_id_type=MESH))
        return copies

    plan.n_sems = 4 * n
    return plan


def _plan_to_chips(n):
    def plan(refs, send_sems, recv_sems, outgoing):
        x, y, cc = _mesh_pos()
        copies = []
        for a in range(n):
            for j, (px, py) in enumerate([(1 - x, y), (x, 1 - y), (1 - x, 1 - y)]):
                dst = refs[n + a].at[j]
                copies.append(pltpu.make_async_remote_copy(
                    src_ref=refs[a].at[2 * px + py] if outgoing else dst, dst_ref=dst,
                    send_sem=send_sems.at[3 * a + j], recv_sem=recv_sems.at[3 * a + j],
                    device_id=(px, py, cc), device_id_type=MESH))
        return copies

    plan.n_sems = 3 * n
    return plan


def _in_hbm(t):
    return pltpu.with_memory_space_constraint(t, pltpu.HBM)


def _exchange_start(plan, bufs, after, *, name):
    nb, na = len(bufs), len(after)

    def body(*refs):
        outs = refs[nb + na:]
        for cp in plan(refs[:nb], outs[0], outs[1], True):
            cp.start()
        outs[2 + nb][...] = jnp.zeros_like(outs[2 + nb])

    res = pl.pallas_call(
        body,
        out_shape=(pltpu.SemaphoreType.DMA((plan.n_sems,)), pltpu.SemaphoreType.DMA((plan.n_sems,)),
                   *[pltpu.HBM(t.shape, t.dtype) for t in bufs], jax.ShapeDtypeStruct((8, LANES), F32)),
        in_specs=[HBM_SPEC] * nb + [ANY] * na,
        out_specs=(SEM_SPEC, SEM_SPEC, *[HBM_SPEC] * nb, pl.BlockSpec(memory_space=pltpu.VMEM)),
        input_output_aliases={i: 2 + i for i in range(nb)},
        compiler_params=pltpu.CompilerParams(has_side_effects=DATAFLOW_EFFECT),
        name=name,
    )(*[_in_hbm(t) for t in bufs], *after)
    return plan, res[:2], list(res[2:2 + nb]), res[2 + nb]


def _exchange_wait(flight, after, *, name):
    plan, sems, bufs, _ = flight
    nb = len(bufs)

    def body(*refs):
        send_sems, recv_sems = refs[nb], refs[nb + 1]
        for cp in plan(refs[:nb], send_sems, recv_sems, False):
            cp.wait_recv()
        for cp in plan(refs[:nb], send_sems, recv_sems, True):
            cp.wait_send()

    res = pl.pallas_call(
        body,
        out_shape=tuple(pltpu.HBM(t.shape, t.dtype) for t in bufs),
        in_specs=[HBM_SPEC] * nb + [SEM_SPEC, SEM_SPEC] + [ANY] * len(after),
        out_specs=tuple([HBM_SPEC] * nb),
        input_output_aliases={i: i for i in range(nb)},
        compiler_params=pltpu.CompilerParams(has_side_effects=DATAFLOW_EFFECT),
        name=name,
    )(*bufs, *sems, *after)
    return list(res)


_W_GROUPS = {"even": ("even_w_in", "even_w_out"), "mlp0": ("mlp_w1_0", "mlp_w2_0"),
             "odd": ("odd_w_in", "odd_w_uq", "odd_w_ukv", "odd_w_out"), "mlp1": ("mlp_w1_1", "mlp_w2_1")}


class _Exchanges:
    def __init__(self, shards, gains, n_q, n_kv):
        self.n_q, self.n_kv = n_q, n_kv
        self.mx, self.my, self.mc = _mesh_pos()
        self.dev = 4 * self.mx + 2 * self.my + self.mc
        self.shards = shards
        self.gains = gains
        self.flights, self.gathered, self.grad_blocks, self.grad_names, self.reduced = {}, {}, {}, {}, {}

    def start(self):
        names = _W_GROUPS["even"]
        got = _all_gather([self.shards[n] for n in names] + [self.gains], name="comm_even_gather")
        self.gathered.update(zip(names, got[:-1]))
        self.all_gains = got[-1][:, 0]
        return self._w_begin("mlp0", [got[0]])

    def _w_begin(self, group, after):
        srcs = [self.shards[n] for n in _W_GROUPS[group]]
        lands = [lax.dynamic_update_slice(lax.empty((N_DEV,) + t.shape, t.dtype), t[None], (self.dev, 0, 0)) for t in srcs]
        self.flights[group] = _exchange_start(_plan_own_blocks(len(srcs)), srcs + lands, after, name=f"comm_{group}_own_start")
        return [self.flights[group][3]]

    def _w_turn(self, group, after):
        bufs = _exchange_wait(self.flights[group], after, name=f"comm_{group}_own_wait")
        n = len(bufs) // 2
        self.flights[group] = _exchange_start(_plan_pass_on(n), bufs[n:], [], name=f"comm_{group}_pass_start")
        return [self.flights[group][3]]

    def _w_end(self, group, after):
        self.gathered.update(zip(_W_GROUPS[group], _exchange_wait(self.flights.pop(group), after, name=f"comm_{group}_pass_wait")))

    def weights(self, group):
        return {n: self.gathered[n] for n in _W_GROUPS[group]}

    def norm_gains(self):
        return (self.all_gains[:, :self.n_q].reshape(-1), self.all_gains[:, self.n_q:self.n_q + self.n_kv].reshape(-1))

    def grads(self, group, named_blocks):
        self.grad_names[group] = [n for n, _ in named_blocks]
        self.grad_blocks[group] = [t for _, t in named_blocks]

    def _g_begin(self, group, after):
        blocks = self.grad_blocks[group]
        lands = [lax.empty((4,) + t.shape[1:], t.dtype) for t in blocks]
        self.flights[group] = _exchange_start(_plan_to_sibling(len(blocks)), blocks + lands, after, name=f"comm_{group}_sib_start")
        return [self.flights[group][3]]

    def _g_turn(self, group, after):
        bufs = _exchange_wait(self.flights[group], after, name=f"comm_{group}_sib_wait")
        n = len(bufs) // 2
        partial = [_pair_sum(a, b, self.mc, name=f"pair_sum_{nm}") for nm, a, b in zip(self.grad_names[group], bufs[:n], bufs[n:])]
        lands = [lax.empty((3,) + t.shape[1:], t.dtype) for t in partial]
        self.flights[group] = _exchange_start(_plan_to_chips(n), partial + lands, [], name=f"comm_{group}_chips_start")
        return [self.flights[group][3]]

    def _g_end(self, group, after):
        bufs = _exchange_wait(self.flights.pop(group), after, name=f"comm_{group}_chips_wait")
        n = len(bufs) // 2
        self.reduced.update(zip(self.grad_names[group], zip(bufs[:n], bufs[n:])))

    _SCHEDULE = {
        "even_out": (("w_turn", "mlp0"), ("w_begin", "odd")),
        "ln1_l0": (("w_end", "mlp0"),),
        "ln2_l0": (("w_turn", "odd"), ("w_begin", "mlp1"), ("w_end", "odd")),
        "odd_out": (("w_turn", "mlp1"),),
        "ln1_l1": (("w_end", "mlp1"),),
        "dw_l1": (("g_begin", "mlp1"),),
        "ln1_bwd_l1": (("g_turn", "mlp1"),),
        "odd_in_dw": (("g_begin", "odd"),),
        "mlp2_dx_l0": (("g_end", "mlp1"), ("g_turn", "odd")),
        "dw_l0": (("g_begin", "mlp0"),),
        "ln1_bwd_l0": (("g_end", "odd"), ("g_turn", "mlp0")),
        "even_in_dw": (("g_begin", "even"),),
        "even_in_dx": (("g_end", "mlp0"), ("g_turn", "even")),
        "finish": (("g_end", "even"),),
    }

    def sync(self, tag, after):
        latest, started = list(after), []
        for what, group in self._SCHEDULE[tag]:
            out = getattr(self, "_" + what)(group, latest)
            if out:
                latest = started = out
        return started


def _alibi(n):
    return 2.0 ** (-8.0 * np.arange(1, n + 1, dtype=np.float32) / n)


def _heads(t, n):
    s = t.shape[0]
    return t.reshape(s, n, t.shape[1] // n).transpose(1, 0, 2)


def _unheads(t):
    n, s, dh = t.shape
    return t.transpose(1, 0, 2).reshape(s, n * dh)


def _to_strided(t, d):
    s, x = t.shape
    return t if d == 1 else t.reshape(s // d, d, x).transpose(1, 0, 2).reshape(s, x)


def _from_strided(t, d):
    s, x = t.shape
    return t if d == 1 else t.reshape(d, s // d, x).transpose(1, 0, 2).reshape(s, x)


def _true_columns(t, c, n_pad):
    r = t.shape[1]
    w = t[:, :, :c].transpose(1, 0, 2).reshape(r, N_DEV * c)
    return jnp.pad(w, ((0, 0), (0, n_pad - N_DEV * c)))


def _column_blocks(t, c, cp):
    r = t.shape[0]
    b = t[:, :N_DEV * c].reshape(r, N_DEV, c).transpose(1, 0, 2)
    return jnp.pad(b, ((0, 0), (0, 0), (0, cp - c)))


def _stack_rows(t, nq):
    return t.reshape(nq, BLK, A_KV_HEADS, A_GROUP, HEAD_DIM).transpose(2, 0, 3, 1, 4).reshape(
        A_KV_HEADS, 1, nq * A_GROUP * BLK, HEAD_DIM)


def _unstack_rows(t, nq):
    return t.reshape(A_KV_HEADS, nq, A_GROUP, BLK, HEAD_DIM).transpose(1, 3, 0, 2, 4).reshape(
        nq * BLK, A_Q_W)


def _lead_block(t):
    return jnp.pad(t, ((0, 0), (BLK, 0), (0, 0)))


def _columns(t):
    return t.transpose(1, 0, 2).reshape(t.shape[1], -1)


def _rows(t):
    return t.reshape(-1, t.shape[2])


def _by_column_block(t):
    return t.reshape(t.shape[0], N_DEV, -1).transpose(1, 0, 2)


def _local_step(x0, target, comm, sinks, ln1_g, ln1_b, ln2_g, ln2_b):
    s, d = x0.shape
    scale_h = 1.0 / math.sqrt(HEAD_DIM)
    scale_d = 1.0 / math.sqrt(D_NOPE + D_ROPE)
    nq = s // BLK
    even_c, odd_c = EVEN_IN // N_DEV, ODD_IN // N_DEV
    bf = lambda t: t.astype(MXU_DTYPE)
    blocks = dict(b_blocks=True)

    tok = comm.start()
    w_even = comm.weights("even")
    even_cp, even_n = w_even["even_w_in"].shape[2], -(-EVEN_IN // 1024) * 1024
    w_even_in, w_even_out = _true_columns(w_even["even_w_in"], even_c, even_n), _columns(w_even["even_w_out"])
    x0b = bf(x0)
    h_e = _mm(x0b, w_even_in, out_dtypes=(MXU_DTYPE,), name="even_in_fwd", deps=tok)
    qa = _stack_rows(h_e[:, :A_Q_W], nq)
    ka = _lead_block(_heads(h_e[:, A_Q_W:A_Q_W + A_KV_W], A_KV_HEADS))[:, None]
    va = _lead_block(_heads(h_e[:, A_Q_W + A_KV_W:A_Q_W + 2 * A_KV_W], A_KV_HEADS))[:, None]
    rows_a = A_GROUP * BLK
    slope_a = jnp.asarray(np.repeat(_alibi(A_Q_HEADS).reshape(A_KV_HEADS, A_GROUP), BLK, axis=1).reshape(
        A_KV_HEADS, 1, rows_a, 1))
    sink_a = jnp.broadcast_to(sinks.reshape(A_KV_HEADS, A_GROUP, 1), (A_KV_HEADS, A_GROUP, BLK)).reshape(
        A_KV_HEADS, 1, rows_a, 1)
    a_cfg = dict(scale=scale_h, n_back=A_WINDOW - 1, bps=nq)
    oa, lse_a = _band_fwd(qa, ka, va, slope_a, sink_a, name="swa_fwd", **a_cfg)
    b_slab, b_cfg, b_out, b_lse = [], [], [], []
    base = A_Q_W + 2 * A_KV_W
    for gi, (window, dil) in enumerate(B_PATTERNS):
        slab = _to_strided(h_e[:, base + gi * 3 * B_W: base + (gi + 1) * 3 * B_W], dil)
        cfg = dict(slopes=_alibi(B_HEADS) * dil, scale=scale_h, n_back=window // dil, bps=nq // dil)
        o, lse = _dil_fwd(slab, name=f"dil{gi}_fwd", **cfg)
        both = _from_strided(jnp.concatenate([o, lse], axis=1), dil)
        b_slab.append(slab)
        b_cfg.append(cfg)
        b_out.append(both[:, :B_W])
        b_lse.append(both[:, B_W:])
    ob, lse_b = _merge(b_out, b_lse, name="dil_merge")
    y_e = bf(jnp.concatenate([_unstack_rows(oa, nq), ob], axis=1))
    mixed = _mm(y_e, w_even_out, name="even_out_fwd")
    tok = comm.sync("even_out", [mixed])
    x0n, x0nb, xh1_0, r1_0 = _ln_fwd(x0, mixed, ln1_g[0], ln1_b[0], name="ln1_fwd_l0", deps=tok)
    comm.sync("ln1_l0", [x0nb])
    w_mlp0 = comm.weights("mlp0")
    w1_0, w2_0 = w_mlp0["mlp_w1_0"], _rows(w_mlp0["mlp_w2_0"])
    act0, hid0 = _mm(x0nb, w1_0, out_dtypes=(MXU_DTYPE, MXU_DTYPE), epilogue=_relu_sq, name="mlp1_fwd_l0", **blocks)
    mlp = _mm(hid0, w2_0, name="mlp2_fwd_l0")
    x1, x1b, xh2_0, r2_0 = _ln_fwd(x0n, mlp, ln2_g[0], ln2_b[0], name="ln2_fwd_l0")

    tok = comm.sync("ln2_l0", [x1b])
    w_odd = comm.weights("odd")
    odd_cp, odd_n = w_odd["odd_w_in"].shape[2], -(-ODD_IN // 1024) * 1024
    w_odd_in, w_uq, w_ukv, w_odd_out = (_true_columns(w_odd["odd_w_in"], odd_c, odd_n), _columns(w_odd["odd_w_uq"]),
                                        _columns(w_odd["odd_w_ukv"]), _rows(w_odd["odd_w_out"]))
    gq, gkv = comm.norm_gains()
    h_o = _mm(x1b, w_odd_in, name="odd_in_fwd", deps=tok)
    qkv_c = bf(h_o[:, :3 * C_W])
    oc, sb_total = _sb_fwd(qkv_c, heads=C_HEADS, scale=scale_h, name="sb_fwd")
    o_cq, o_ckv, o_kr = 3 * C_W, 3 * C_W + D_Q_RANK, 3 * C_W + D_Q_RANK + D_KV_RANK
    cq, ckv, kr = h_o[:, o_cq:o_ckv], h_o[:, o_ckv:o_kr], h_o[:, o_kr:o_kr + D_ROPE]
    ncq, rq = _rms_fwd(cq, gq, name="rms_q_fwd")
    nckv, rkv = _rms_fwd(ckv, gkv, name="rms_kv_fwd")
    lane_pad = LANES - D_NOPE - D_ROPE
    w_uq = jnp.pad(w_uq.reshape(D_Q_RANK, D_HEADS, D_NOPE + D_ROPE), ((0, 0), (0, 0), (0, lane_pad))).reshape(
        D_Q_RANK, D_HEADS * LANES)
    qd = _mm(ncq, w_uq, name="uq_fwd")
    kvd = _mm(nckv, w_ukv, out_dtypes=(MXU_DTYPE,), name="ukv_fwd")
    rope_t = _rope_tables(s, inverse=False)
    krp = _rope(jnp.pad(kr, ((0, 0), (D_NOPE, lane_pad)))[None], rope_t, out_dtype=MXU_DTYPE, name="rope_k_fwd")[0]
    od, lse_d = _mla_fwd(qd, kvd, krp, rope_t, scale=scale_d, name="mla_fwd")
    y_o = bf(jnp.concatenate([oc, od], axis=1))
    mixed = _mm(y_o, w_odd_out, name="odd_out_fwd")
    tok = comm.sync("odd_out", [mixed])
    x1n, x1nb, xh1_1, r1_1 = _ln_fwd(x1, mixed, ln1_g[1], ln1_b[1], name="ln1_fwd_l1", deps=tok)
    comm.sync("ln1_l1", [x1nb])
    w_mlp1 = comm.weights("mlp1")
    w1_1, w2_1 = w_mlp1["mlp_w1_1"], _rows(w_mlp1["mlp_w2_1"])
    act1, hid1 = _mm(x1nb, w1_1, out_dtypes=(MXU_DTYPE, MXU_DTYPE), epilogue=_relu_sq, name="mlp1_fwd_l1", **blocks)
    mlp = _mm(hid1, w2_1, name="mlp2_fwd_l1")
    y, _, xh2_1, r2_1 = _ln_fwd(x1n, mlp, ln2_g[1], ln2_b[1], name="ln2_fwd_l1")
    dy, loss_vec = _loss_head(y, target, name="loss_head")

    def mlp_block_bwd(g_out, layer, w1, w2, xh2, r2, xh1, r1, act, hid, xnb):
        du2, du2b, dg2, db2 = _ln_bwd(g_out, xh2, r2, ln2_g[layer], name=f"ln2_bwd_l{layer}")
        dpre = _mm(du2b, w2, nt=True, out_dtypes=(MXU_DTYPE,), epilogue=_relu_sq_grad, extra=act, name=f"mlp2_dx_l{layer}")
        tok = comm.sync("mlp2_dx_l0", [dpre]) if layer == 0 else []
        dw2 = _mm(hid, du2b, ta=True, out_dtypes=(BF16,), name=f"mlp2_dw_l{layer}", deps=tok)
        dw1 = _mm(xnb, dpre, ta=True, out_blocks=True, out_dtypes=(BF16,), name=f"mlp1_dw_l{layer}")
        comm.grads(f"mlp{layer}", [(f"mlp_w1_{layer}", dw1), (f"mlp_w2_{layer}", dw2.reshape(N_DEV, -1, d))])
        tok = comm.sync(f"dw_l{layer}", [dw1])
        dxn = _mm(dpre, w1, nt=True, epilogue=_add_alpha, extra=du2, name=f"mlp1_dx_l{layer}", deps=tok, **blocks)
        du1, du1b, dg1, db1 = _ln_bwd(dxn, xh1, r1, ln1_g[layer], name=f"ln1_bwd_l{layer}")
        return du1, du1b, comm.sync(f"ln1_bwd_l{layer}", [du1b]), (dg1, db1, dg2, db2)

    du1, du1b, tok, ln_1 = mlp_block_bwd(dy, 1, w1_1, w2_1, xh2_1, r2_1, xh1_1, r1_1, act1, hid1, x1nb)
    d_odd_out = _mm(y_o, du1b, ta=True, out_dtypes=(BF16,), name="odd_out_dw").reshape(N_DEV, -1, d)
    dy_o = _mm(du1b, w_odd_out, nt=True, name="odd_out_dx", deps=tok)
    dqc, dkc, dvc = _sb_bwd(qkv_c, dy_o, sb_total, heads=C_HEADS, scale=scale_h, name="sb_bwd")
    dqd, dkvd, dkr_pairs = _mla_bwd(qd, kvd, krp, rope_t, dy_o, od, lse_d, do_block0=C_W // LANES, scale=scale_d,
                                    name="mla_bwd")
    _, dkr_sum = _rope(dkr_pairs, _rope_tables(s, inverse=True), out_dtype=F32, head_sum=True, name="rope_k_bwd")
    dqd, dkvd = bf(dqd), bf(dkvd)
    d_uq = _mm(ncq, dqd, ta=True, out_dtypes=(BF16,), name="uq_dw").reshape(D_Q_RANK, D_HEADS, LANES)[:, :, :D_NOPE + D_ROPE].reshape(
        D_Q_RANK, D_HEADS * (D_NOPE + D_ROPE))
    dncq = _mm(dqd, w_uq, nt=True, name="uq_dx")
    d_ukv = _mm(nckv, dkvd, ta=True, out_dtypes=(BF16,), name="ukv_dw")
    dnckv = _mm(dkvd, w_ukv, nt=True, name="ukv_dx")
    dcq, dgq = _rms_bwd(dncq, cq, rq, gq, name="rms_q_bwd")
    dckv, dgkv = _rms_bwd(dnckv, ckv, rkv, gkv, name="rms_kv_bwd")
    dh_o = bf(jnp.concatenate(
        [dqc, dkc, dvc, dcq, dckv, dkr_sum[:, D_NOPE:D_NOPE + D_ROPE], jnp.zeros((s, odd_n - ODD_IN), F32)], axis=1))
    d_odd_in = _column_blocks(_mm(x1b, dh_o, ta=True, out_dtypes=(BF16,), name="odd_in_dw"), odd_c, odd_cp)
    comm.grads("odd", [("odd_w_in", d_odd_in), ("odd_w_uq", _by_column_block(d_uq)),
                       ("odd_w_ukv", _by_column_block(d_ukv)), ("odd_w_out", d_odd_out)])
    tok = comm.sync("odd_in_dw", [d_odd_in])
    dx1 = _mm(dh_o, w_odd_in, nt=True, epilogue=_add_alpha, extra=du1, name="odd_in_dx", deps=tok)

    du1, du1b, tok, ln_0 = mlp_block_bwd(dx1, 0, w1_0, w2_0, xh2_0, r2_0, xh1_0, r1_0, act0, hid0, x0nb)
    d_even_out = _mm(y_e, du1b, ta=True, out_dtypes=(BF16,), name="even_out_dw")
    dy_e = _mm(du1b, w_even_out, nt=True, name="even_out_dx", deps=tok)
    doa = _stack_rows(dy_e[:, :A_Q_W], nq)
    dqa, dka, dva, dsink = _band_bwd(qa, ka, va, doa, oa, lse_a, slope_a, sink_a, name="swa_bwd", **a_cfg)
    pieces = [_unstack_rows(dqa, nq), _unheads(dka[:, 0, BLK:]), _unheads(dva[:, 0, BLK:])]
    pack = jnp.concatenate([dy_e[:, A_Q_W:], ob, lse_b], axis=1)
    for gi, (_, dil) in enumerate(B_PATTERNS):
        grads = _dil_bwd(b_slab[gi], _to_strided(pack, dil), name=f"dil{gi}_bwd", **b_cfg[gi])
        pieces.append(_from_strided(jnp.concatenate(grads, axis=1), dil))
    dh_e = bf(jnp.concatenate(pieces + [jnp.zeros((s, even_n - EVEN_IN), F32)], axis=1))
    d_even_in = _column_blocks(_mm(x0b, dh_e, ta=True, out_dtypes=(BF16,), name="even_in_dw"), even_c, even_cp)
    comm.grads("even", [("even_w_in", d_even_in), ("even_w_out", _by_column_block(d_even_out))])
    tok = comm.sync("even_in_dw", [d_even_in])
    grad_x = _mm(dh_e, w_even_in, nt=True, epilogue=_add_alpha, extra=du1, name="even_in_dx", deps=tok)
    tok = comm.sync("even_in_dx", [grad_x])

    ln = [jnp.concatenate([a, b], axis=0) for a, b in zip(ln_0, ln_1)]
    small = {"ln": ln, "sinks": dsink[:, :, 0].reshape(-1), "gq": dgq[0], "gkv": dgkv[0], "loss": loss_vec[0, :1]}
    return grad_x, small, tok


def kernel(x, even_w_in, even_sinks, even_w_out, odd_w_in, odd_q_norm_g, odd_kv_norm_g, odd_w_uq, odd_w_ukv, odd_w_out, ln1_g, ln1_b, mlp_w1, mlp_w2, ln2_g, ln2_b, loss_target, m_even_w_in, m_even_sinks, m_even_w_out, m_odd_w_in, m_odd_q_norm_g, m_odd_kv_norm_g, m_odd_w_uq, m_odd_w_ukv, m_odd_w_out, m_ln1_g, m_ln1_b, m_mlp_w1, m_mlp_w2, m_ln2_g, m_ln2_b, v_even_w_in, v_even_sinks, v_even_w_out, v_odd_w_in, v_odd_q_norm_g, v_odd_kv_norm_g, v_odd_w_uq, v_odd_w_ukv, v_odd_w_out, v_ln1_g, v_ln1_b, v_mlp_w1, v_mlp_w2, v_ln2_g, v_ln2_b):
    weights = dict(even_w_in=even_w_in, even_sinks=even_sinks, even_w_out=even_w_out, odd_w_in=odd_w_in,
                   odd_q_norm_g=odd_q_norm_g, odd_kv_norm_g=odd_kv_norm_g, odd_w_uq=odd_w_uq, odd_w_ukv=odd_w_ukv,
                   odd_w_out=odd_w_out, ln1_g=ln1_g, ln1_b=ln1_b, mlp_w1=mlp_w1, mlp_w2=mlp_w2, ln2_g=ln2_g, ln2_b=ln2_b)
    mom_m = dict(even_w_in=m_even_w_in, even_sinks=m_even_sinks, even_w_out=m_even_w_out, odd_w_in=m_odd_w_in,
                 odd_q_norm_g=m_odd_q_norm_g, odd_kv_norm_g=m_odd_kv_norm_g, odd_w_uq=m_odd_w_uq, odd_w_ukv=m_odd_w_ukv,
                 odd_w_out=m_odd_w_out, ln1_g=m_ln1_g, ln1_b=m_ln1_b, mlp_w1=m_mlp_w1, mlp_w2=m_mlp_w2, ln2_g=m_ln2_g, ln2_b=m_ln2_b)
    mom_v = dict(even_w_in=v_even_w_in, even_sinks=v_even_sinks, even_w_out=v_even_w_out, odd_w_in=v_odd_w_in,
                 odd_q_norm_g=v_odd_q_norm_g, odd_kv_norm_g=v_odd_kv_norm_g, odd_w_uq=v_odd_w_uq, odd_w_ukv=v_odd_w_ukv,
                 odd_w_out=v_odd_w_out, ln1_g=v_ln1_g, ln1_b=v_ln1_b, mlp_w1=v_mlp_w1, mlp_w2=v_mlp_w2, ln2_g=v_ln2_g, ln2_b=v_ln2_b)
    order = list(weights)
    n_q, n_kv = odd_q_norm_g.shape[1], odd_kv_norm_g.shape[1]

    def lane_padded(t):
        return jnp.pad(t, ((0, 0), (0, _lane_pad(t.shape[1]) - t.shape[1]))).astype(BF16)

    shards = {"even_w_in": lane_padded(even_w_in[0]), "even_w_out": even_w_out[0].astype(BF16),
              "mlp_w1_0": mlp_w1[0].astype(BF16), "mlp_w2_0": mlp_w2[0].astype(BF16),
              "odd_w_in": lane_padded(odd_w_in[0]), "odd_w_uq": odd_w_uq[0].astype(BF16),
              "odd_w_ukv": odd_w_ukv[0].astype(BF16), "odd_w_out": odd_w_out[0].astype(BF16),
              "mlp_w1_1": mlp_w1[1].astype(BF16), "mlp_w2_1": mlp_w2[1].astype(BF16)}
    gains = jnp.concatenate([odd_q_norm_g, odd_kv_norm_g, jnp.zeros((1, LANES - n_q - n_kv), F32)], axis=1)
    comm = _Exchanges(shards, gains, n_q, n_kv)
    dev = comm.dev

    grad_x, small, last_started = _local_step(x[0], loss_target[0], comm, even_sinks[0], ln1_g, ln1_b, ln2_g, ln2_b)

    grads, delta, new_m, new_v = {}, {}, {}, {}

    def update(n):
        g_list = [comm.reduced[f"{n}_0"], comm.reduced[f"{n}_1"]] if n.startswith("mlp") else [comm.reduced[n]]
        grads[n], delta[n], new_m[n], new_v[n] = _adamw(weights[n], g_list, mom_m[n], mom_v[n], name=f"adamw_{n}",
                                                        my_chip=2 * comm.mx + comm.my, deps=last_started)

    early = ("mlp_w1", "mlp_w2", "odd_w_in", "odd_w_uq", "odd_w_ukv", "odd_w_out")
    for n in early:
        update(n)
    comm.sync("finish", [new_v[n] for n in early])
    update("even_w_in")
    update("even_w_out")

    small_parts = [t.reshape(-1) for t in small["ln"]] + [small["sinks"], small["gq"], small["gkv"], small["loss"]]
    small_sizes = [p.shape[0] for p in small_parts]
    n_small = sum(small_sizes)
    small_rows = -(-n_small // (8 * LANES)) * 8
    small_flat = jnp.concatenate(small_parts + [jnp.zeros((small_rows * LANES - n_small,), F32)]).reshape(small_rows, LANES)
    (small_all,) = _all_gather([small_flat], name="comm_small_gather", deps=[grads["even_w_in"]])
    totals = _sum_devices(small_all, name="small_sum").reshape(-1)
    tot, off = [], 0
    for size in small_sizes:
        tot.append(totals[off:off + size])
        off += size
    for i, n in enumerate(("ln1_g", "ln1_b", "ln2_g", "ln2_b")):
        grads[n] = tot[i].reshape(weights[n].shape)
    grads["even_sinks"] = tot[4].reshape(even_sinks.shape)
    grads["odd_q_norm_g"] = lax.dynamic_slice(tot[5], (dev * n_q,), (n_q,)).reshape(odd_q_norm_g.shape)
    grads["odd_kv_norm_g"] = lax.dynamic_slice(tot[6], (dev * n_kv,), (n_kv,)).reshape(odd_kv_norm_g.shape)
    loss = tot[7][0]

    small_names = [n for n in order if n not in early + ("even_w_in", "even_w_out")]
    n_sm = sum(weights[n].size for n in small_names)
    sm_rows = -(-n_sm // (8 * LANES)) * 8

    def pack_small(group):
        flat = [group[n].reshape(-1) for n in small_names]
        return jnp.concatenate(flat + [jnp.zeros((sm_rows * LANES - n_sm,), F32)]).reshape(1, sm_rows, LANES)

    res = _adamw(pack_small(weights), [pack_small(grads)[0]], pack_small(mom_m), pack_small(mom_v), name="adamw_small")
    off = 0
    for n in small_names:
        size = weights[n].size
        delta[n], new_m[n], new_v[n] = (t.reshape(-1)[off:off + size].reshape(weights[n].shape) for t in res[1:])
        off += size

    return (loss, grad_x[None], *[grads[n] for n in order], *[delta[n] for n in order],
            *[new_m[n] for n in order], *[new_v[n] for n in order])
```

```python
import math

import jax
import jax.numpy as jnp
import numpy as np
from jax import lax
from jax.experimental import pallas as pl
from jax.experimental.pallas import tpu as pltpu

F32 = jnp.float32
BF16 = jnp.bfloat16
MXU_DTYPE = BF16

HEAD_DIM = 64
A_Q_HEADS, A_KV_HEADS, A_WINDOW = 16, 2, 128
A_GROUP = A_Q_HEADS // A_KV_HEADS
B_HEADS = 8
B_PATTERNS = ((128, 1), (512, 4), (2048, 16))
C_HEADS = 16
D_HEADS, D_Q_RANK, D_KV_RANK, D_NOPE, D_ROPE, D_V = 16, 512, 256, 64, 32, 64
ROPE_BASE = 10000.0
LN_EPS, RMS_EPS = 1e-5, 1e-6
DEPTH = 2
ALPHA = (2 * DEPTH) ** 0.25
A_Q_W, A_KV_W, B_W = A_Q_HEADS * HEAD_DIM, A_KV_HEADS * HEAD_DIM, B_HEADS * HEAD_DIM
EVEN_IN = A_Q_W + 2 * A_KV_W + 3 * B_W * len(B_PATTERNS)
C_W = C_HEADS * HEAD_DIM
ODD_IN = 3 * C_W + D_Q_RANK + D_KV_RANK + D_ROPE
ADAM_LR, ADAM_B1, ADAM_B2, ADAM_EPS, ADAM_WD, ADAM_STEP = 0.001, 0.9, 0.999, 1e-08, 0.01, 10

N_DEV = 8
LANES = 128
BLK = 128
CAUSAL_TILE = 512
CUM_CHUNK = 256
NEG = -1e30
VMEM_LIMIT = 48 * 1024 * 1024

NN = ((1,), (0,))
NT = ((1,), (1,))
TN = ((0,), (0,))
MESH = pl.DeviceIdType.MESH
ANY = pl.BlockSpec(memory_space=pl.ANY)
HBM_SPEC = pl.BlockSpec(memory_space=pltpu.HBM)
SEM_SPEC = pl.BlockSpec(memory_space=pltpu.SEMAPHORE)
DATAFLOW_EFFECT = pltpu.SideEffectType.DATAFLOW_SIDE_EFFECTING


def _dot(a, b, dims):
    return lax.dot_general(a, b, (dims, ((), ())), preferred_element_type=F32)


def _bdot(a, b, dims):
    return jnp.stack([_dot(a[n], b[n], dims) for n in range(a.shape[0])])


def _params(*sem):
    return pltpu.CompilerParams(dimension_semantics=tuple(sem), vmem_limit_bytes=VMEM_LIMIT)


def _pick(n, cap, mult=LANES):
    if n <= cap:
        return n
    for t in range(cap - cap % mult, 0, -mult):
        if n % t == 0:
            return t
    raise ValueError(f"no tile for {n}")


def _lane_pad(c):
    return -(-c // LANES) * LANES


def _mm(a, b, *, name, nt=False, ta=False, b_blocks=False, out_blocks=False, out_dtypes=(F32,), epilogue=None, extra=None, deps=()):
    m, k = a.shape[::-1] if ta else a.shape
    if b_blocks:
        nb, kin, c = b.shape
        n = kin if nt else nb * c
        k_full = nb * c if nt else kin
    else:
        n, k_full = (b.shape if nt else b.shape[::-1])
    assert k == k_full, (a.shape, b.shape, nt, b_blocks)
    tm = _pick(m, 1024, 8)
    if b_blocks and not nt:
        tn, tk = c, _pick(k, 3072)
    elif b_blocks:
        per_step = max(g for g in (1, 2, 4, 8) if g * c <= 2048)
        tn, tk = _pick(n, 1024), per_step * c
    elif out_blocks:
        tn, tk = n // N_DEV, _pick(k, 3072)
    else:
        tn, tk = _pick(n, 512), _pick(k, 3072)
        if k > tk:
            tn, tk = _pick(n, 1024), _pick(k, 2048)
    nk = k // tk
    n_out = len(out_dtypes)

    def body(*refs):
        a_ref, b_ref = refs[0], refs[1]
        e_ref = refs[2] if extra is not None else None
        first_out = 2 + (extra is not None) + len(deps)
        out_refs = refs[first_out:first_out + n_out]

        def finish(acc):
            e = None if e_ref is None else e_ref[...]
            outs = (acc,) if epilogue is None else epilogue(acc, e)
            for r, o in zip(out_refs, outs):
                r[...] = o.astype(r.dtype).reshape(r.shape)

        if b_blocks and nt:
            part = _dot(a_ref[:, :c], b_ref[0], NT)
            for blk in range(1, per_step):
                part += _dot(a_ref[:, blk * c:(blk + 1) * c], b_ref[blk], NT)
        elif ta:
            part = _dot(a_ref[...], b_ref[...], TN)
        else:
            part = _dot(a_ref[...], b_ref[0] if b_blocks else b_ref[...], NT if nt else NN)
        if nk == 1:
            finish(part)
        else:
            acc_ref = refs[first_out + n_out]
            kk = pl.program_id(2)

            @pl.when(kk == 0)
            def _():
                acc_ref[...] = part

            @pl.when(kk > 0)
            def _():
                acc_ref[...] += part

            @pl.when(kk == nk - 1)
            def _():
                finish(acc_ref[...])

    if b_blocks and not nt:
        b_spec = pl.BlockSpec((1, tk, tn), lambda i, j, kk: (j, kk, 0))
    elif b_blocks:
        b_spec = pl.BlockSpec((per_step, tn, c), lambda i, j, kk: (kk, j, 0))
    elif nt:
        b_spec = pl.BlockSpec((tn, tk), lambda i, j, kk: (j, kk))
    else:
        b_spec = pl.BlockSpec((tk, tn), lambda i, j, kk: (kk, j))
    a_spec = pl.BlockSpec((tk, tm), lambda i, j, kk: (kk, i)) if ta else pl.BlockSpec((tm, tk), lambda i, j, kk: (i, kk))
    in_specs = [a_spec, b_spec]
    ins = [a.astype(MXU_DTYPE), b.astype(MXU_DTYPE)]
    if extra is not None:
        in_specs.append(pl.BlockSpec((tm, tn), lambda i, j, kk: (i, j)))
        ins.append(extra)
    in_specs += [ANY] * len(deps)
    ins += list(deps)
    if out_blocks:
        out_shape = tuple(jax.ShapeDtypeStruct((N_DEV, m, tn), d) for d in out_dtypes)
        out_specs = tuple(pl.BlockSpec((1, tm, tn), lambda i, j, kk: (j, i, 0)) for _ in out_dtypes)
    else:
        out_shape = tuple(jax.ShapeDtypeStruct((m, n), d) for d in out_dtypes)
        out_specs = tuple(pl.BlockSpec((tm, tn), lambda i, j, kk: (i, j)) for _ in out_dtypes)
    outs = pl.pallas_call(
        body,
        out_shape=out_shape,
        grid=(m // tm, n // tn, nk),
        in_specs=in_specs,
        out_specs=out_specs,
        scratch_shapes=[pltpu.VMEM((tm, tn), F32)] if nk > 1 else [],
        compiler_params=_params("parallel", "parallel", "arbitrary"),
        name=name,
    )(*ins)
    return outs[0] if n_out == 1 else outs


def _relu_sq(acc, _):
    act = jnp.maximum(acc, 0.0)
    return act, act * act


def _relu_sq_grad(acc, act):
    return (acc * (2.0 * act.astype(F32)),)


def _add_alpha(acc, du):
    return (acc + ALPHA * du,)


def _ln_fwd(x, mixed, g, b, *, name, deps=()):
    s, d = x.shape
    tr = _pick(s, 256, 8)

    def body(x_ref, m_ref, g_ref, b_ref, *rest):
        y_ref, yb_ref, xh_ref, r_ref = rest[len(deps):]
        u = ALPHA * x_ref[...] + m_ref[...]
        mu = jnp.mean(u, axis=-1, keepdims=True)
        xc = u - mu
        var = jnp.mean(xc * xc, axis=-1, keepdims=True)
        r = lax.rsqrt(var + LN_EPS)
        xh = xc * r
        y = xh * g_ref[...] + b_ref[...]
        y_ref[...] = y
        yb_ref[...] = y.astype(MXU_DTYPE)
        xh_ref[...] = xh
        r_ref[...] = r

    row = pl.BlockSpec((tr, d), lambda i: (i, 0))
    vec = pl.BlockSpec((1, d), lambda i: (0, 0))
    return pl.pallas_call(
        body,
        out_shape=(jax.ShapeDtypeStruct((s, d), F32), jax.ShapeDtypeStruct((s, d), MXU_DTYPE),
                   jax.ShapeDtypeStruct((s, d), F32), jax.ShapeDtypeStruct((s, 1), F32)),
        grid=(s // tr,),
        in_specs=[row, row, vec, vec] + [ANY] * len(deps),
        out_specs=(row, row, row, pl.BlockSpec((tr, 1), lambda i: (i, 0))),
        compiler_params=_params("parallel"),
        name=name,
    )(x, mixed, g.reshape(1, d), b.reshape(1, d), *deps)


def _ln_bwd(dy, xh, r, g, *, name):
    s, d = dy.shape
    tr = _pick(s, 256, 8)

    def body(dy_ref, xh_ref, r_ref, g_ref, du_ref, dub_ref, dg_ref, db_ref):
        dyv, xhv = dy_ref[...], xh_ref[...]
        dxh = dyv * g_ref[...]
        c1 = jnp.mean(dxh, axis=-1, keepdims=True)
        c2 = jnp.mean(dxh * xhv, axis=-1, keepdims=True)
        du = r_ref[...] * (dxh - c1 - xhv * c2)
        du_ref[...] = du
        dub_ref[...] = du.astype(MXU_DTYPE)

        @pl.when(pl.program_id(0) == 0)
        def _():
            dg_ref[...] = jnp.zeros_like(dg_ref)
            db_ref[...] = jnp.zeros_like(db_ref)

        dg_ref[...] += jnp.sum(dyv * xhv, axis=0, keepdims=True)
        db_ref[...] += jnp.sum(dyv, axis=0, keepdims=True)

    row = pl.BlockSpec((tr, d), lambda i: (i, 0))
    vec = pl.BlockSpec((1, d), lambda i: (0, 0))
    return pl.pallas_call(
        body,
        out_shape=(jax.ShapeDtypeStruct((s, d), F32), jax.ShapeDtypeStruct((s, d), MXU_DTYPE),
                   jax.ShapeDtypeStruct((1, d), F32), jax.ShapeDtypeStruct((1, d), F32)),
        grid=(s // tr,),
        in_specs=[row, row, pl.BlockSpec((tr, 1), lambda i: (i, 0)), vec],
        out_specs=(row, row, vec, vec),
        compiler_params=_params("arbitrary"),
        name=name,
    )(dy, xh, r, g.reshape(1, d))


def _rms_fwd(x, g, *, name):
    s, d = x.shape
    tr = _pick(s, 512, 8)

    def body(x_ref, g_ref, y_ref, r_ref):
        xv = x_ref[...]
        r = lax.rsqrt(jnp.mean(xv * xv, axis=-1, keepdims=True) + RMS_EPS)
        y_ref[...] = (xv * r * g_ref[...]).astype(y_ref.dtype)
        r_ref[...] = r

    return pl.pallas_call(
        body,
        out_shape=(jax.ShapeDtypeStruct((s, d), MXU_DTYPE), jax.ShapeDtypeStruct((s, 1), F32)),
        grid=(s // tr,),
        in_specs=[pl.BlockSpec((tr, d), lambda i: (i, 0)), pl.BlockSpec((1, d), lambda i: (0, 0))],
        out_specs=(pl.BlockSpec((tr, d), lambda i: (i, 0)), pl.BlockSpec((tr, 1), lambda i: (i, 0))),
        compiler_params=_params("parallel"),
        name=name,
    )(x, g.reshape(1, d))


def _rms_bwd(dy, x, r, g, *, name):
    s, d = x.shape
    tr = _pick(s, 512, 8)

    def body(dy_ref, x_ref, r_ref, g_ref, dx_ref, dg_ref):
        dyv, rv = dy_ref[...], r_ref[...]
        xn = x_ref[...] * rv
        dxn = dyv * g_ref[...]
        dx_ref[...] = rv * (dxn - xn * jnp.mean(dxn * xn, axis=-1, keepdims=True))

        @pl.when(pl.program_id(0) == 0)
        def _():
            dg_ref[...] = jnp.zeros_like(dg_ref)

        dg_ref[...] += jnp.sum(dyv * xn, axis=0, keepdims=True)

    row = pl.BlockSpec((tr, d), lambda i: (i, 0))
    vec = pl.BlockSpec((1, d), lambda i: (0, 0))
    return pl.pallas_call(
        body,
        out_shape=(jax.ShapeDtypeStruct((s, d), F32), jax.ShapeDtypeStruct((1, d), F32)),
        grid=(s // tr,),
        in_specs=[row, row, pl.BlockSpec((tr, 1), lambda i: (i, 0)), vec],
        out_specs=(row, vec),
        compiler_params=_params("arbitrary"),
        name=name,
    )(dy, x, r, g.reshape(1, d))


def _rope_tables(s, inverse):
    inv_freq = ROPE_BASE ** (-jnp.arange(0, D_ROPE, 2, dtype=F32) / D_ROPE)
    ang = jnp.arange(s, dtype=F32)[:, None] * inv_freq[None, :]
    cos, sin = jnp.cos(ang), jnp.sin(ang)
    if inverse:
        sin = -sin
    half = D_ROPE // 2
    one, zero = jnp.ones((s, D_NOPE), F32), jnp.zeros((s, D_NOPE), F32)
    pad1, pad0 = jnp.ones((s, LANES - D_NOPE - D_ROPE), F32), jnp.zeros((s, LANES - D_NOPE - D_ROPE), F32)
    zh = jnp.zeros((s, half), F32)
    c = jnp.concatenate([one, cos, cos, pad1], axis=1)
    s_lo = jnp.concatenate([zero, -sin, zh, pad0], axis=1)
    s_hi = jnp.concatenate([zero, zh, sin, pad0], axis=1)
    return c, s_lo, s_hi


def _rope(x, tables, *, out_dtype, head_sum=False, name):
    h, s, w = x.shape
    ts = _pick(s, 2048, 8)

    def body(x_ref, c_ref, lo_ref, hi_ref, y_ref, *sum_ref):
        y = _rotate(x_ref[0], c_ref[...], lo_ref[...], hi_ref[...])
        y_ref[0] = y.astype(y_ref.dtype)
        if head_sum:
            @pl.when(pl.program_id(1) == 0)
            def _():
                sum_ref[0][...] = jnp.zeros_like(sum_ref[0])

            sum_ref[0][...] += y

    tab = pl.BlockSpec((ts, w), lambda i, hh: (i, 0))
    blk = pl.BlockSpec((1, ts, w), lambda i, hh: (hh, i, 0))
    out_shape = [jax.ShapeDtypeStruct((h, s, w), out_dtype)]
    out_specs = [blk]
    if head_sum:
        out_shape.append(jax.ShapeDtypeStruct((s, w), F32))
        out_specs.append(tab)
    res = pl.pallas_call(
        body,
        out_shape=tuple(out_shape),
        grid=(s // ts, h),
        in_specs=[blk, tab, tab, tab],
        out_specs=tuple(out_specs),
        compiler_params=_params("parallel", "arbitrary"),
        name=name,
    )(x, *tables)
    return res if head_sum else res[0]


def _band_scores(q, kw, slope, i, *, scale, n_back, bps):
    b, r, _ = q.shape
    sc = _bdot(q, kw, NT) * scale
    shape = (b, r, 2 * BLK)
    row = lax.broadcasted_iota(jnp.int32, shape, 1) & (BLK - 1)
    col = lax.broadcasted_iota(jnp.int32, shape, 2)
    rel = BLK + row - col
    first_col = jnp.where(i % bps == 0, BLK, 0)
    valid = (rel >= 0) & (rel <= n_back) & (col >= first_col)
    return jnp.where(valid, sc - slope * rel.astype(F32), NEG)


def _band_fwd(q, k, v, slope, sink, *, scale, n_back, bps, name):
    g, b, rows, dh = q.shape
    r = slope.shape[2]
    nq = rows // r
    skv = k.shape[2]
    use_sink = sink is not None

    def body(*refs):
        q_ref, k_ref, v_ref, slope_ref = refs[:4]
        sink_ref = refs[4] if use_sink else None
        o_ref, lse_ref = refs[4 + use_sink:]
        i = pl.program_id(1)
        off = pl.multiple_of(i * BLK, BLK)
        kw = k_ref[0, :, pl.ds(off, 2 * BLK), :]
        vw = v_ref[0, :, pl.ds(off, 2 * BLK), :]
        sc = _band_scores(q_ref[0], kw, slope_ref[0], i, scale=scale, n_back=n_back, bps=bps)
        m = jnp.max(sc, axis=-1, keepdims=True)
        if use_sink:
            m = jnp.maximum(m, sink_ref[0])
        p = jnp.exp(sc - m)
        l = jnp.sum(p, axis=-1, keepdims=True)
        if use_sink:
            l = l + jnp.exp(sink_ref[0] - m)
        o_ref[0] = _bdot(p.astype(MXU_DTYPE), vw, NN) / l
        lse_ref[0] = m + jnp.log(l)

    qspec = pl.BlockSpec((1, b, r, dh), lambda gg, i: (gg, 0, i, 0))
    kspec = pl.BlockSpec((1, b, skv, dh), lambda gg, i: (gg, 0, 0, 0))
    rspec = pl.BlockSpec((1, b, r, 1), lambda gg, i: (gg, 0, 0, 0))
    ins = [q, k, v, slope] + ([sink] if use_sink else [])
    return pl.pallas_call(
        body,
        out_shape=(jax.ShapeDtypeStruct((g, b, rows, dh), F32), jax.ShapeDtypeStruct((g, b, rows, 1), F32)),
        grid=(g, nq),
        in_specs=[qspec, kspec, kspec, rspec] + ([rspec] if use_sink else []),
        out_specs=(qspec, pl.BlockSpec((1, b, r, 1), lambda gg, i: (gg, 0, i, 0))),
        compiler_params=_params("parallel", "arbitrary"),
        name=name,
    )(*ins)


def _band_bwd(q, k, v, do, o, lse, slope, sink, *, scale, n_back, bps, name):
    g, b, rows, dh = q.shape
    r = slope.shape[2]
    nq = rows // r
    skv = k.shape[2]
    use_sink = sink is not None
    stacked = r // BLK

    def body(*refs):
        q_ref, k_ref, v_ref, do_ref, o_ref, lse_ref, slope_ref = refs[:7]
        sink_ref = refs[7] if use_sink else None
        dq_ref, dk_ref, dv_ref = refs[7 + use_sink:10 + use_sink]
        i = pl.program_id(1)

        @pl.when(i == 0)
        def _():
            dk_ref[...] = jnp.zeros_like(dk_ref)
            dv_ref[...] = jnp.zeros_like(dv_ref)

        off = pl.multiple_of(i * BLK, BLK)
        qb = q_ref[0]
        kw = k_ref[0, :, pl.ds(off, 2 * BLK), :]
        vw = v_ref[0, :, pl.ds(off, 2 * BLK), :]
        dof = do_ref[0]
        dob = dof.astype(MXU_DTYPE)
        lse_b = lse_ref[0]
        delta = jnp.sum(dof * o_ref[0], axis=-1, keepdims=True)
        sc = _band_scores(qb, kw, slope_ref[0], i, scale=scale, n_back=n_back, bps=bps)
        p = jnp.exp(sc - lse_b)
        ds = (p * (_bdot(dob, vw, NT) - delta) * scale).astype(MXU_DTYPE)
        dq_ref[0] = _bdot(ds, kw, NN)
        dk_ref[0, :, pl.ds(off, 2 * BLK), :] += _bdot(ds, qb, TN)
        dv_ref[0, :, pl.ds(off, 2 * BLK), :] += _bdot(p.astype(MXU_DTYPE), dob, TN)

        if use_sink:
            dsink_ref = refs[10 + use_sink]

            @pl.when(i == 0)
            def _():
                dsink_ref[...] = jnp.zeros_like(dsink_ref)

            contrib = -jnp.exp(sink_ref[0] - lse_b) * delta
            for n in range(stacked):
                part = jnp.sum(contrib[0, n * BLK:(n + 1) * BLK, :], axis=0, keepdims=True)
                dsink_ref[0, n:n + 1, :] += jnp.broadcast_to(part, (1, LANES))

    def qspec(w):
        return pl.BlockSpec((1, b, r, w), lambda gg, i: (gg, 0, i, 0))

    kspec = pl.BlockSpec((1, b, skv, dh), lambda gg, i: (gg, 0, 0, 0))
    rspec = pl.BlockSpec((1, b, r, 1), lambda gg, i: (gg, 0, 0, 0))
    ins = [q, k, v, do, o, lse, slope] + ([sink] if use_sink else [])
    in_specs = [qspec(dh), kspec, kspec, qspec(dh), qspec(dh), qspec(1), rspec] + ([rspec] if use_sink else [])
    out_shape = [jax.ShapeDtypeStruct((g, b, rows, dh), F32), jax.ShapeDtypeStruct((g, b, skv, dh), F32),
                 jax.ShapeDtypeStruct((g, b, skv, dh), F32)]
    out_specs = [qspec(dh), kspec, kspec]
    if use_sink:
        assert b == 1
        out_shape.append(jax.ShapeDtypeStruct((g, stacked, LANES), F32))
        out_specs.append(pl.BlockSpec((1, stacked, LANES), lambda gg, i: (gg, 0, 0)))
    return pl.pallas_call(
        body,
        out_shape=tuple(out_shape),
        grid=(g, nq),
        in_specs=in_specs,
        out_specs=tuple(out_specs),
        compiler_params=_params("parallel", "arbitrary"),
        name=name,
    )(*ins)


def _pair_masks():
    first = lax.broadcasted_iota(jnp.int32, (1, LANES), 1) < HEAD_DIM
    m0 = first.astype(MXU_DTYPE)
    return first, (m0, 1 - m0)


def _dil_window(ref, i):
    prev = pl.multiple_of(jnp.maximum(i - 1, 0) * BLK, BLK)
    cur = pl.multiple_of(i * BLK, BLK)
    return prev, cur, jnp.concatenate([ref[pl.ds(prev, BLK), :], ref[pl.ds(cur, BLK), :]], axis=0)


def _dil_mask(i, n_back, bps):
    row = lax.broadcasted_iota(jnp.int32, (BLK, 2 * BLK), 0)
    col = lax.broadcasted_iota(jnp.int32, (BLK, 2 * BLK), 1)
    rel = BLK + row - col
    first_col = jnp.where(i % bps == 0, BLK, 0)
    return (rel >= 0) & (rel <= n_back) & (col >= first_col), rel.astype(F32)


def _dil_fwd(slab, slopes, *, scale, n_back, bps, name):
    s, w = slab.shape[0], slab.shape[1] // 3

    def body(q_ref, k_ref, v_ref, o_ref, lse_ref):
        i = pl.program_id(0)
        first, masks = _pair_masks()
        _, _, kw = _dil_window(k_ref, i)
        _, _, vw = _dil_window(v_ref, i)
        valid, rel = _dil_mask(i, n_back, bps)
        for p in range(w // LANES):
            cols = slice(p * LANES, (p + 1) * LANES)
            qp, kp, vp = q_ref[:, cols], kw[:, cols], vw[:, cols]
            outs, lses = [], []
            for hh in range(2):
                sc = _dot(qp * masks[hh], kp, NT) * scale - float(slopes[2 * p + hh]) * rel
                sc = jnp.where(valid, sc, NEG)
                m = jnp.max(sc, axis=-1, keepdims=True)
                e = jnp.exp(sc - m)
                l = jnp.sum(e, axis=-1, keepdims=True)
                outs.append(_dot(e.astype(MXU_DTYPE), vp, NN) / l)
                lses.append(m + jnp.log(l))
            o_ref[:, cols] = jnp.where(first, outs[0], outs[1])
            lse_ref[:, cols] = jnp.where(first, lses[0], lses[1])

    blk = pl.BlockSpec((BLK, w), lambda i: (i, 0))
    return pl.pallas_call(
        body,
        out_shape=(jax.ShapeDtypeStruct((s, w), F32), jax.ShapeDtypeStruct((s, w), F32)),
        grid=(s // BLK,),
        in_specs=[blk, pl.BlockSpec((s, w), lambda i: (0, 1)), pl.BlockSpec((s, w), lambda i: (0, 2))],
        out_specs=(blk, blk),
        compiler_params=_params("arbitrary"),
        name=name,
    )(slab, slab, slab)


def _dil_bwd(slab, pack, slopes, *, scale, n_back, bps, name):
    s, w = slab.shape[0], slab.shape[1] // 3

    def body(q_ref, k_ref, v_ref, do_ref, o_ref, lse_ref, dq_ref, dk_ref, dv_ref):
        i = pl.program_id(0)

        @pl.when(i == 0)
        def _():
            dk_ref[...] = jnp.zeros_like(dk_ref)
            dv_ref[...] = jnp.zeros_like(dv_ref)

        first, masks = _pair_masks()
        prev, cur, kw = _dil_window(k_ref, i)
        _, _, vw = _dil_window(v_ref, i)
        valid, rel = _dil_mask(i, n_back, bps)
        for p in range(w // LANES):
            cols = slice(p * LANES, (p + 1) * LANES)
            qp, kp, vp = q_ref[:, cols], kw[:, cols], vw[:, cols]
            dof, lse_p = do_ref[:, cols], lse_ref[:, cols]
            prod = dof * o_ref[:, cols]
            do_b = dof.astype(MXU_DTYPE)
            dqs, dk_add, dv_add = [], None, None
            for hh in range(2):
                qh, doh = qp * masks[hh], do_b * masks[hh]
                delta = jnp.sum(jnp.where(first, prod, 0.0) if hh == 0 else jnp.where(first, 0.0, prod), axis=-1, keepdims=True)
                sc = _dot(qh, kp, NT) * scale - float(slopes[2 * p + hh]) * rel
                e = jnp.exp(jnp.where(valid, sc, NEG) - lse_p[:, hh * HEAD_DIM:hh * HEAD_DIM + 1])
                ds = (e * (_dot(doh, vp, NT) - delta) * scale).astype(MXU_DTYPE)
                dqs.append(_dot(ds, kp, NN))
                dk_h, dv_h = _dot(ds, qh, TN), _dot(e.astype(MXU_DTYPE), doh, TN)
                dk_add = dk_h if dk_add is None else dk_add + dk_h
                dv_add = dv_h if dv_add is None else dv_add + dv_h
            dq_ref[:, cols] = jnp.where(first, dqs[0], dqs[1])
            dk_ref[pl.ds(prev, BLK), cols] += dk_add[:BLK]
            dk_ref[pl.ds(cur, BLK), cols] += dk_add[BLK:]
            dv_ref[pl.ds(prev, BLK), cols] += dv_add[:BLK]
            dv_ref[pl.ds(cur, BLK), cols] += dv_add[BLK:]

    def blk(c):
        return pl.BlockSpec((BLK, w), lambda i: (i, c))

    def whole(c):
        return pl.BlockSpec((s, w), lambda i: (0, c))

    shp = jax.ShapeDtypeStruct((s, w), F32)
    return pl.pallas_call(
        body,
        out_shape=(shp, shp, shp),
        grid=(s // BLK,),
        in_specs=[blk(0), whole(1), whole(2), blk(0), blk(1), blk(2)],
        out_specs=(blk(0), whole(0), whole(0)),
        compiler_params=_params("arbitrary"),
        name=name,
    )(slab, slab, slab, pack, pack, pack)


def _merge(outs, lses, *, name):
    s, w = outs[0].shape
    tr = _pick(s, 512, 8)

    def body(o0, o1, o2, l0, l1, l2, ob_ref, lt_ref):
        a, b, c = l0[...], l1[...], l2[...]
        m = jnp.maximum(jnp.maximum(a, b), c)
        ea, eb, ec = jnp.exp(a - m), jnp.exp(b - m), jnp.exp(c - m)
        den = ea + eb + ec
        ob_ref[...] = (ea / den) * o0[...] + (eb / den) * o1[...] + (ec / den) * o2[...]
        lt_ref[...] = m + jnp.log(den)

    spec = pl.BlockSpec((tr, w), lambda i: (i, 0))
    return pl.pallas_call(
        body,
        out_shape=(jax.ShapeDtypeStruct((s, w), F32), jax.ShapeDtypeStruct((s, w), F32)),
        grid=(s // tr,),
        in_specs=[spec] * 6,
        out_specs=(spec, spec),
        compiler_params=_params("parallel"),
        name=name,
    )(*outs, *lses)


def _tile_iotas(t):
    return lax.broadcasted_iota(jnp.int32, (t, t), 0), lax.broadcasted_iota(jnp.int32, (t, t), 1)


def _rotate(x, c, s_lo, s_hi):
    half = D_ROPE // 2
    return x * c + pltpu.roll(x, LANES - half, 1) * s_lo + pltpu.roll(x, half, 1) * s_hi


def _mla_keys(kv_h, kr_t, first):
    return jnp.where(first, kv_h, kr_t)


def _mla_fwd(qd, kvd, krp, tables, *, scale, name):
    s = qd.shape[0]
    pairs = qd.shape[1] // (2 * LANES)
    t = min(CAUSAL_TILE, s)

    def body(q_ref, kv_ref, kr_ref, c_ref, lo_ref, hi_ref, o_ref, lse_ref):
        i = pl.program_id(1)
        first = lax.broadcasted_iota(jnp.int32, (1, LANES), 1) < HEAD_DIM
        tabs = (c_ref[...], lo_ref[...], hi_ref[...])
        q_heads = [_rotate(q_ref[:, hh * LANES:(hh + 1) * LANES], *tabs).astype(MXU_DTYPE) for hh in range(2)]

        def tile(j, carry, diagonal):
            off = pl.multiple_of(j * t, t)
            kr_t = kr_ref[pl.ds(off, t), :]
            out = []
            for hh in range(2):
                m, l, acc = carry[3 * hh:3 * hh + 3]
                kv_h = kv_ref[pl.ds(off, t), hh * LANES:(hh + 1) * LANES]
                sc = _dot(q_heads[hh], _mla_keys(kv_h, kr_t, first), NT) * scale
                if diagonal:
                    row, col = _tile_iotas(t)
                    sc = jnp.where(row >= col, sc, NEG)
                m_new = jnp.maximum(m, jnp.max(sc, axis=-1, keepdims=True))
                a = jnp.exp(m - m_new)
                p = jnp.exp(sc - m_new)
                out += [m_new, a * l + jnp.sum(p, axis=-1, keepdims=True), a * acc + _dot(p.astype(MXU_DTYPE), kv_h, NN)]
            return tuple(out)

        init = (jnp.full((t, 1), NEG, F32), jnp.zeros((t, 1), F32), jnp.zeros((t, LANES), F32)) * 2
        carry = lax.fori_loop(0, i, lambda j, c: tile(j, c, False), init)
        m0, l0, acc0, m1, l1, acc1 = tile(i, carry, True)
        o_ref[...] = jnp.where(first, pltpu.roll(acc0 / l0, HEAD_DIM, 1), acc1 / l1)
        lse_ref[0] = jnp.where(lax.broadcasted_iota(jnp.int32, (t, 2), 1) == 0, m0 + jnp.log(l0), m1 + jnp.log(l1))

    tab = pl.BlockSpec((t, LANES), lambda p, i: (i, 0))
    return pl.pallas_call(
        body,
        out_shape=(jax.ShapeDtypeStruct((s, pairs * LANES), F32), jax.ShapeDtypeStruct((pairs, s, 2), F32)),
        grid=(pairs, s // t),
        in_specs=[pl.BlockSpec((t, 2 * LANES), lambda p, i: (i, p)), pl.BlockSpec((s, 2 * LANES), lambda p, i: (0, p)),
                  pl.BlockSpec((s, LANES), lambda p, i: (0, 0)), tab, tab, tab],
        out_specs=(pl.BlockSpec((t, LANES), lambda p, i: (i, p)), pl.BlockSpec((1, t, 2), lambda p, i: (p, i, 0))),
        compiler_params=_params("parallel", "arbitrary"),
        name=name,
    )(qd, kvd, krp, *tables)


def _mla_bwd(qd, kvd, krp, tables, do, o, lse, *, do_block0, scale, name):
    s = qd.shape[0]
    pairs = qd.shape[1] // (2 * LANES)
    t = min(CAUSAL_TILE, s)

    def body(q_ref, kv_ref, kr_ref, c_ref, lo_ref, hi_ref, do_ref, o_ref, lse_ref, dq_ref, dkv_ref, dkr_ref):
        i = pl.program_id(1)

        @pl.when(i == 0)
        def _():
            dkv_ref[...] = jnp.zeros_like(dkv_ref)
            dkr_ref[...] = jnp.zeros_like(dkr_ref)

        first = lax.broadcasted_iota(jnp.int32, (1, LANES), 1) < HEAD_DIM
        tabs = (c_ref[...], lo_ref[...], hi_ref[...])
        q_heads = [_rotate(q_ref[:, hh * LANES:(hh + 1) * LANES], *tabs).astype(MXU_DTYPE) for hh in range(2)]
        dof = do_ref[...]
        prod = dof * o_ref[...]
        deltas = [jnp.sum(jnp.where(first, prod, 0.0), axis=-1, keepdims=True),
                  jnp.sum(jnp.where(first, 0.0, prod), axis=-1, keepdims=True)]
        do_heads = [jnp.where(first, 0.0, pltpu.roll(dof, HEAD_DIM, 1)).astype(MXU_DTYPE),
                    jnp.where(first, 0.0, dof).astype(MXU_DTYPE)]
        lses = [lse_ref[0][:, hh:hh + 1] for hh in range(2)]

        def tile(j, carry, diagonal):
            off = pl.multiple_of(j * t, t)
            kr_t = kr_ref[pl.ds(off, t), :]
            out, dkr_add = [], None
            for hh in range(2):
                kv_h = kv_ref[pl.ds(off, t), hh * LANES:(hh + 1) * LANES]
                k_h = _mla_keys(kv_h, kr_t, first)
                sc = _dot(q_heads[hh], k_h, NT) * scale
                if diagonal:
                    row, col = _tile_iotas(t)
                    sc = jnp.where(row >= col, sc, NEG)
                p = jnp.exp(sc - lses[hh])
                ds = (p * (_dot(do_heads[hh], kv_h, NT) - deltas[hh]) * scale).astype(MXU_DTYPE)
                dk_full = _dot(ds, q_heads[hh], TN)
                dv_full = _dot(p.astype(MXU_DTYPE), do_heads[hh], TN)
                dkv_ref[pl.ds(off, t), hh * LANES:(hh + 1) * LANES] += jnp.where(first, dk_full, dv_full)
                rot = jnp.where(first, 0.0, dk_full)
                dkr_add = rot if dkr_add is None else dkr_add + rot
                out.append(carry[hh] + _dot(ds, k_h, NN))
            dkr_ref[0, pl.ds(off, t), :] += dkr_add
            return tuple(out)

        zacc = jnp.zeros((t, LANES), F32)
        carry = lax.fori_loop(0, i, lambda j, c: tile(j, c, False), (zacc, zacc))
        dq_heads = tile(i, carry, True)
        for hh in range(2):
            dq_ref[:, hh * LANES:(hh + 1) * LANES] = _rotate(dq_heads[hh], tabs[0], -tabs[1], -tabs[2])

    tab = pl.BlockSpec((t, LANES), lambda p, i: (i, 0))
    qspec = pl.BlockSpec((t, 2 * LANES), lambda p, i: (i, p))
    kvspec = pl.BlockSpec((s, 2 * LANES), lambda p, i: (0, p))
    return pl.pallas_call(
        body,
        out_shape=(jax.ShapeDtypeStruct(qd.shape, F32), jax.ShapeDtypeStruct(kvd.shape, F32),
                   jax.ShapeDtypeStruct((pairs, s, LANES), F32)),
        grid=(pairs, s // t),
        in_specs=[qspec, kvspec, pl.BlockSpec((s, LANES), lambda p, i: (0, 0)), tab, tab, tab,
                  pl.BlockSpec((t, LANES), lambda p, i: (i, do_block0 + p)), pl.BlockSpec((t, LANES), lambda p, i: (i, p)),
                  pl.BlockSpec((1, t, 2), lambda p, i: (p, i, 0))],
        out_specs=(qspec, kvspec, pl.BlockSpec((1, s, LANES), lambda p, i: (p, 0, 0))),
        compiler_params=_params("parallel", "arbitrary"),
        name=name,
    )(qd, kvd, krp, *tables, do, o, lse)


def _split_cumsum(x, tri, terms=2):
    hi = x.astype(BF16)
    if terms == 1:
        return _dot(hi, tri, NN)
    lo = (x - hi.astype(F32)).astype(BF16)
    return _dot(hi, tri, NN) + _dot(lo, tri, NN)


def _chunked_cumsum(x, tri, run, *, reverse, negate=False, terms=2):
    c = tri.shape[0]
    n = x.shape[1] // c
    parts = [None] * n
    for idx in (reversed(range(n)) if reverse else range(n)):
        xc = x[:, idx * c:(idx + 1) * c]
        sums = _split_cumsum(xc, tri, terms)
        parts[idx] = (-run) - sums if negate else run + sums
        run = run + jnp.sum(xc, axis=-1, keepdims=True)
    return (parts[0] if n == 1 else jnp.concatenate(parts, axis=1)), run


def _sb_logs(z):
    l1 = jnp.log(1.0 + jnp.exp(-jnp.abs(z)))
    return jnp.minimum(z, 0.0) - l1, -jnp.maximum(z, 0.0) - l1


def _scaled_query_heads(q, masks, scale):
    assert math.log2(scale).is_integer(), scale
    return [q * (m * scale).astype(q.dtype) for m in masks]


def _sb_fwd(qkv, *, heads, scale, name):
    s = qkv.shape[0]
    pairs = heads * HEAD_DIM // LANES
    t = min(CAUSAL_TILE, s)
    cc = min(CUM_CHUNK, t)

    def body(q_ref, k_ref, v_ref, o_ref, t_ref):
        i = pl.program_id(1)
        first, masks = _pair_masks()
        q_heads = _scaled_query_heads(q_ref[...], masks, scale)
        crow, ccol = _tile_iotas(cc)
        after = (crow > ccol).astype(BF16)

        def tile(j, carry, diagonal):
            off = pl.multiple_of(j * t, t)
            kb = k_ref[pl.ds(off, t), :]
            vb = v_ref[pl.ds(off, t), :]
            if diagonal:
                row, col = _tile_iotas(t)
                strict = row > col
            out = []
            for hh in range(2):
                run, acc = carry[2 * hh], carry[2 * hh + 1]
                log_beta, log_keep = _sb_logs(_dot(q_heads[hh], kb, NT))
                if diagonal:
                    log_keep = jnp.where(strict, log_keep, 0.0)
                a, run = _chunked_cumsum(log_keep, after, run, reverse=True)
                w = jnp.exp(log_beta + a)
                if diagonal:
                    w = jnp.where(strict, w, 0.0)
                out += [run, acc + _dot(w.astype(MXU_DTYPE), vb, NN)]
            return tuple(out)

        zero, zacc = jnp.zeros((t, 1), F32), jnp.zeros((t, LANES), F32)
        carry = tile(i, (zero, zacc, zero, zacc), True)
        run0, acc0, run1, acc1 = lax.fori_loop(0, i, lambda jj, c: tile(i - 1 - jj, c, False), carry)
        o_ref[...] = jnp.where(first, acc0, acc1)
        t_ref[0] = jnp.where(lax.broadcasted_iota(jnp.int32, (t, 2), 1) == 0, run0, run1)

    return pl.pallas_call(
        body,
        out_shape=(jax.ShapeDtypeStruct((s, heads * HEAD_DIM), F32), jax.ShapeDtypeStruct((pairs, s, 2), F32)),
        grid=(pairs, s // t),
        in_specs=[pl.BlockSpec((t, LANES), lambda p, i: (i, p)),
                  pl.BlockSpec((s, LANES), lambda p, i: (0, pairs + p)),
                  pl.BlockSpec((s, LANES), lambda p, i: (0, 2 * pairs + p))],
        out_specs=(pl.BlockSpec((t, LANES), lambda p, i: (i, p)), pl.BlockSpec((1, t, 2), lambda p, i: (p, i, 0))),
        compiler_params=_params("parallel", "arbitrary"),
        name=name,
    )(qkv, qkv, qkv)


def _sb_bwd(qkv, do, total, *, heads, scale, name):
    s = qkv.shape[0]
    pairs = heads * HEAD_DIM // LANES
    t = min(CAUSAL_TILE, s)
    cc = min(CUM_CHUNK, t)

    def body(q_ref, k_ref, v_ref, do_ref, t_ref, dq_ref, dk_ref, dv_ref):
        i = pl.program_id(1)

        @pl.when(i == 0)
        def _():
            dk_ref[...] = jnp.zeros_like(dk_ref)
            dv_ref[...] = jnp.zeros_like(dv_ref)

        first, masks = _pair_masks()
        q_heads = _scaled_query_heads(q_ref[...], masks, scale)
        do_b = do_ref[...].astype(MXU_DTYPE)
        do_heads = [do_b * m for m in masks]
        tots = [t_ref[0][:, hh:hh + 1] for hh in range(2)]
        crow, ccol = _tile_iotas(cc)
        upto = (crow <= ccol).astype(BF16)
        before = (crow < ccol).astype(BF16)

        def tile(j, carry, diagonal):
            off = pl.multiple_of(j * t, t)
            kb = k_ref[pl.ds(off, t), :]
            vb = v_ref[pl.ds(off, t), :]
            if diagonal:
                row, col = _tile_iotas(t)
                strict = row > col
            out, dk_add, dv_add = [], None, None
            for hh in range(2):
                run_keep, run_g, dq_acc = carry[3 * hh:3 * hh + 3]
                log_beta, log_keep = _sb_logs(_dot(q_heads[hh], kb, NT))
                keep = jnp.exp(log_keep)
                if diagonal:
                    log_keep = jnp.where(strict, log_keep, 0.0)
                a, run_keep = _chunked_cumsum(log_keep, upto, run_keep - tots[hh], reverse=False, negate=True)
                run_keep = run_keep + tots[hh]
                w = jnp.exp(log_beta + a)
                if diagonal:
                    w = jnp.where(strict, w, 0.0)
                g = w * _dot(do_heads[hh], vb, NT)
                prefix, run_g = _chunked_cumsum(g, before, run_g, reverse=False, terms=1)
                dz = g * keep - (1.0 - keep) * prefix
                if diagonal:
                    dz = jnp.where(strict, dz, 0.0)
                dz = dz.astype(MXU_DTYPE)
                dk_h = _dot(dz, q_heads[hh], TN)
                dv_h = _dot(w.astype(MXU_DTYPE), do_heads[hh], TN)
                dk_add = dk_h if dk_add is None else dk_add + dk_h
                dv_add = dv_h if dv_add is None else dv_add + dv_h
                out += [run_keep, run_g, dq_acc + _dot(dz, kb, NN)]
            dk_ref[pl.ds(off, t), :] += dk_add
            dv_ref[pl.ds(off, t), :] += dv_add
            return tuple(out)

        zero, zacc = jnp.zeros((t, 1), F32), jnp.zeros((t, LANES), F32)
        carry = lax.fori_loop(0, i, lambda j, c: tile(j, c, False), (zero, zero, zacc, zero, zero, zacc))
        res = tile(i, carry, True)
        dq_ref[...] = jnp.where(first, res[2], res[5]) * scale

    qspec = pl.BlockSpec((t, LANES), lambda p, i: (i, p))
    shp = jax.ShapeDtypeStruct((s, heads * HEAD_DIM), F32)
    return pl.pallas_call(
        body,
        out_shape=(shp, shp, shp),
        grid=(pairs, s // t),
        in_specs=[qspec, pl.BlockSpec((s, LANES), lambda p, i: (0, pairs + p)),
                  pl.BlockSpec((s, LANES), lambda p, i: (0, 2 * pairs + p)), qspec,
                  pl.BlockSpec((1, t, 2), lambda p, i: (p, i, 0))],
        out_specs=(qspec, pl.BlockSpec((s, LANES), lambda p, i: (0, p)), pl.BlockSpec((s, LANES), lambda p, i: (0, p))),
        compiler_params=_params("parallel", "arbitrary"),
        name=name,
    )(qkv, qkv, qkv, do, total)


def _loss_head(y, target, *, name):
    s, d = y.shape
    tr = _pick(s, 256, 8)

    def body(y_ref, t_ref, dy_ref, loss_ref):
        err = y_ref[...] - t_ref[...]
        dy_ref[...] = err * (1.0 / d)

        @pl.when(pl.program_id(0) == 0)
        def _():
            loss_ref[...] = jnp.zeros_like(loss_ref)

        per_tok = jnp.mean(err * err, axis=-1, keepdims=True)
        loss_ref[...] += 0.5 * jnp.sum(per_tok, axis=0, keepdims=True)

    row = pl.BlockSpec((tr, d), lambda i: (i, 0))
    return pl.pallas_call(
        body,
        out_shape=(jax.ShapeDtypeStruct((s, d), F32), jax.ShapeDtypeStruct((1, LANES), F32)),
        grid=(s // tr,),
        in_specs=[row, row],
        out_specs=(row, pl.BlockSpec((1, LANES), lambda i: (0, 0))),
        compiler_params=_params("arbitrary"),
        name=name,
    )(y, target)


def _adamw(w, grads, m, v, *, name, my_chip=None, deps=()):
    nl, r, c = w.shape
    pieces = my_chip is not None
    cp = (grads[0][0] if pieces else grads[0]).shape[-1]
    tr = _pick(r, min(256, max(16, 262144 // cp)), 8)
    per = 4 if pieces else 1

    def body(chip_ref, *refs):
        w_ref, m_ref, v_ref = refs[:3]
        g_refs = refs[3:3 + per * nl]
        g_out, d_ref, m2_ref, v2_ref = refs[3 + per * nl + len(deps):]

        def grad(n):
            if not pieces:
                return g_refs[n][:, :c]
            own, r0, r1, r2 = (t[0, :, :c].astype(F32) for t in g_refs[4 * n:4 * n + 4])
            return ((own + r0) + r1) + r2

        layer = pl.program_id(0)
        gv = grad(0)
        for n in range(1, nl):
            gv = jnp.where(layer == n, grad(n), gv)
        m2 = ADAM_B1 * m_ref[0] + (1.0 - ADAM_B1) * gv
        v2 = ADAM_B2 * v_ref[0] + (1.0 - ADAM_B2) * (gv * gv)
        m_hat = m2 / (1.0 - ADAM_B1 ** ADAM_STEP)
        v_hat = v2 / (1.0 - ADAM_B2 ** ADAM_STEP)
        g_out[0] = gv
        d_ref[0] = -ADAM_LR * (m_hat / (jnp.sqrt(v_hat) + ADAM_EPS) + ADAM_WD * w_ref[0])
        m2_ref[0] = m2
        v2_ref[0] = v2

    blk = pl.BlockSpec((1, tr, c), lambda l, i, chip_ref: (l, i, 0))
    if pieces:
        g_specs = [pl.BlockSpec((1, tr, cp), lambda l, i, chip_ref: (chip_ref[0], i, 0))]
        g_specs += [pl.BlockSpec((1, tr, cp), lambda l, i, chip_ref, k=k: (k, i, 0)) for k in range(3)]
        g_ins = [t for partial, recv in grads for t in (partial, recv, recv, recv)]
        chip = my_chip.reshape(1).astype(jnp.int32)
    else:
        g_specs, g_ins, chip = [pl.BlockSpec((tr, cp), lambda l, i, chip_ref: (i, 0))], list(grads), jnp.zeros((1,), jnp.int32)
    shp = jax.ShapeDtypeStruct((nl, r, c), F32)
    grid_spec = pltpu.PrefetchScalarGridSpec(
        num_scalar_prefetch=1,
        grid=(nl, r // tr),
        in_specs=[blk, blk, blk] + g_specs * nl + [ANY] * len(deps),
        out_specs=(blk, blk, blk, blk),
    )
    return pl.pallas_call(
        body,
        out_shape=(shp, shp, shp, shp),
        grid_spec=grid_spec,
        compiler_params=_params("parallel", "parallel"),
        name=name,
    )(chip, w, m, v, *g_ins, *deps)


def _pair_sum(mine, recv, my_c, *, name):
    _, r, c = mine.shape
    tr = _pick(r, 512, 16)

    def body(c_ref, a_ref, b_ref, o_ref):
        o_ref[0] = (a_ref[0].astype(F32) + b_ref[0].astype(F32)).astype(o_ref.dtype)

    grid_spec = pltpu.PrefetchScalarGridSpec(
        num_scalar_prefetch=1,
        grid=(4, r // tr),
        in_specs=[pl.BlockSpec((1, tr, c), lambda kk, i, c_ref: (2 * kk + c_ref[0], i, 0)),
                  pl.BlockSpec((1, tr, c), lambda kk, i, c_ref: (kk, i, 0))],
        out_specs=pl.BlockSpec((1, tr, c), lambda kk, i, c_ref: (kk, i, 0)),
    )
    return pl.pallas_call(
        body,
        out_shape=jax.ShapeDtypeStruct((4, r, c), mine.dtype),
        grid_spec=grid_spec,
        compiler_params=_params("parallel", "parallel"),
        name=name,
    )(my_c.reshape(1).astype(jnp.int32), mine, recv)


def _sum_devices(stack, *, name):
    n, r, c = stack.shape

    def body(s_ref, o_ref):
        acc = s_ref[0]
        for dev in range(1, n):
            acc = acc + s_ref[dev]
        o_ref[...] = acc

    return pl.pallas_call(
        body,
        out_shape=jax.ShapeDtypeStruct((r, c), F32),
        in_specs=[pl.BlockSpec(memory_space=pltpu.VMEM)],
        out_specs=pl.BlockSpec(memory_space=pltpu.VMEM),
        name=name,
    )(stack)


def _mesh_pos():
    return lax.axis_index("x"), lax.axis_index("y"), lax.axis_index("c")


def _all_gather(shards, *, name, deps=()):
    n = len(shards)

    def body(*refs):
        x_refs, out_refs = refs[:n], refs[n + len(deps):2 * n + len(deps)]
        send_sems, recv_sems, local_sems = refs[2 * n + len(deps):]
        x, y, cc = _mesh_pos()
        me, sibling = (x, y, cc), (x, y, 1 - cc)
        flip_x, flip_y = cc, 1 - cc
        first = (x + flip_x - 2 * x * flip_x, y + flip_y - 2 * y * flip_y)
        other = (x + flip_y - 2 * x * flip_y, y + flip_x - 2 * y * flip_x)
        diagonal = (1 - x, 1 - y)

        def rows(a, px, py, pc):
            return out_refs[a].at[4 * px + 2 * py + pc]

        def copy(a, kk, block, to, src=None):
            return pltpu.make_async_remote_copy(
                src_ref=rows(a, *block) if src is None else src, dst_ref=rows(a, *block),
                send_sem=send_sems.at[7 * a + kk], recv_sem=recv_sems.at[7 * a + kk],
                device_id=to, device_id_type=MESH)

        sends, own = [], []
        for a in range(n):
            own.append(pltpu.make_async_copy(x_refs[a], rows(a, *me), local_sems.at[a]))
            own[a].start()
            out = [copy(a, 0, me, sibling, src=x_refs[a]), copy(a, 1, me, (*first, cc), src=x_refs[a]),
                   copy(a, 2, me, (*other, cc), src=x_refs[a])]
            for cp in out:
                cp.start()
            sends += out
        for a in range(n):
            copy(a, 1, (*first, cc), me).wait_recv()
            out = [copy(a, 3, (*first, cc), (*other, cc)), copy(a, 4, (*first, cc), sibling)]
            for cp in out:
                cp.start()
            copy(a, 2, (*other, cc), me).wait_recv()
            out.append(copy(a, 5, (*other, cc), sibling))
            out[2].start()
            sends += out
        for a in range(n):
            copy(a, 3, (*diagonal, cc), me).wait_recv()
            passed = copy(a, 6, (*diagonal, cc), sibling)
            passed.start()
            sends.append(passed)
        for a in range(n):
            copy(a, 0, sibling, me).wait_recv()
            copy(a, 4, (*other, 1 - cc), me).wait_recv()
            copy(a, 5, (*first, 1 - cc), me).wait_recv()
            copy(a, 6, (*diagonal, 1 - cc), me).wait_recv()
        for cp in sends:
            cp.wait_send()
        for cp in own:
            cp.wait()

    return pl.pallas_call(
        body,
        out_shape=tuple(jax.ShapeDtypeStruct((N_DEV,) + t.shape, t.dtype) for t in shards),
        in_specs=[ANY] * (n + len(deps)),
        out_specs=tuple([ANY] * n),
        scratch_shapes=[pltpu.SemaphoreType.DMA((7 * n,)), pltpu.SemaphoreType.DMA((7 * n,)),
                        pltpu.SemaphoreType.DMA((n,))],
        name=name,
    )(*shards, *deps)


def _plan_own_blocks(n):
    def plan(refs, send_sems, recv_sems, outgoing):
        x, y, cc = _mesh_pos()
        peers = [(x, y, 1 - cc), (1 - x, y, cc), (x, 1 - y, cc), (1 - x, 1 - y, cc)]
        copies = []
        for a in range(n):
            land = refs[n + a]
            for kk, (px, py, pc) in enumerate(peers):
                block = (x, y, cc) if outgoing else (px, py, pc)
                rows = land.at[4 * block[0] + 2 * block[1] + block[2]]
                copies.append(pltpu.make_async_remote_copy(
                    src_ref=refs[a] if outgoing else rows, dst_ref=rows, send_sem=send_sems.at[4 * a + kk],
                    recv_sem=recv_sems.at[4 * a + kk], device_id=(px, py, pc), device_id_type=MESH))
        return copies

    plan.n_sems = 4 * n
    return plan


def _plan_pass_on(n):
    def plan(refs, send_sems, recv_sems, outgoing):
        x, y, cc = _mesh_pos()
        copies = []
        for a in range(n):
            for j, (px, py) in enumerate([(1 - x, y), (x, 1 - y), (1 - x, 1 - y)]):
                rows = refs[a].at[4 * px + 2 * py + (cc if outgoing else 1 - cc)]
                copies.append(pltpu.make_async_remote_copy(
                    src_ref=rows, dst_ref=rows, send_sem=send_sems.at[3 * a + j], recv_sem=recv_sems.at[3 * a + j],
                    device_id=(x, y, 1 - cc), device_id_type=MESH))
        return copies

    plan.n_sems = 3 * n
    return plan


def _plan_to_sibling(n):
    def plan(refs, send_sems, recv_sems, outgoing):
        x, y, cc = _mesh_pos()
        copies = []
        for a in range(n):
            for chip in range(4):
                dst = refs[n + a].at[chip]
                copies.append(pltpu.make_async_remote_copy(
                    src_ref=refs[a].at[2 * chip + (1 - cc)] if outgoing else dst, dst_ref=dst,
                    send_sem=send_sems.at[4 * a + chip], recv_sem=recv_sems.at[4 * a + chip],
                    device_id=(x, y, 1 - cc), device_id_type=MESH))
        return copies

    plan.n_sems = 4 * n
    return plan


def _plan_to_chips(n):
    def plan(refs, send_sems, recv_sems, outgoing):
        x, y, cc = _mesh_pos()
        copies = []
        for a in range(n):
            for j, (px, py) in enumerate([(1 - x, y), (x, 1 - y), (1 - x, 1 - y)]):
                dst = refs[n + a].at[j]
                copies.append(pltpu.make_async_remote_copy(
                    src_ref=refs[a].at[2 * px + py] if outgoing else dst, dst_ref=dst,
                    send_sem=send_sems.at[3 * a + j], recv_sem=recv_sems.at[3 * a + j],
                    device_id=(px, py, cc), device_id_type=MESH))
        return copies

    plan.n_sems = 3 * n
    return plan


def _in_hbm(t):
    return pltpu.with_memory_space_constraint(t, pltpu.HBM)


def _exchange_start(plan, bufs, after, *, name):
    nb, na = len(bufs), len(after)

    def body(*refs):
        outs = refs[nb + na:]
        for cp in plan(refs[:nb], outs[0], outs[1], True):
            cp.start()
        outs[2 + nb][...] = jnp.zeros_like(outs[2 + nb])

    res = pl.pallas_call(
        body,
        out_shape=(pltpu.SemaphoreType.DMA((plan.n_sems,)), pltpu.SemaphoreType.DMA((plan.n_sems,)),
                   *[pltpu.HBM(t.shape, t.dtype) for t in bufs], jax.ShapeDtypeStruct((8, LANES), F32)),
        in_specs=[HBM_SPEC] * nb + [ANY] * na,
        out_specs=(SEM_SPEC, SEM_SPEC, *[HBM_SPEC] * nb, pl.BlockSpec(memory_space=pltpu.VMEM)),
        input_output_aliases={i: 2 + i for i in range(nb)},
        compiler_params=pltpu.CompilerParams(has_side_effects=DATAFLOW_EFFECT),
        name=name,
    )(*[_in_hbm(t) for t in bufs], *after)
    return plan, res[:2], list(res[2:2 + nb]), res[2 + nb]


def _exchange_wait(flight, after, *, name):
    plan, sems, bufs, _ = flight
    nb = len(bufs)

    def body(*refs):
        send_sems, recv_sems = refs[nb], refs[nb + 1]
        for cp in plan(refs[:nb], send_sems, recv_sems, False):
            cp.wait_recv()
        for cp in plan(refs[:nb], send_sems, recv_sems, True):
            cp.wait_send()

    res = pl.pallas_call(
        body,
        out_shape=tuple(pltpu.HBM(t.shape, t.dtype) for t in bufs),
        in_specs=[HBM_SPEC] * nb + [SEM_SPEC, SEM_SPEC] + [ANY] * len(after),
        out_specs=tuple([HBM_SPEC] * nb),
        input_output_aliases={i: i for i in range(nb)},
        compiler_params=pltpu.CompilerParams(has_side_effects=DATAFLOW_EFFECT),
        name=name,
    )(*bufs, *sems, *after)
    return list(res)


_W_GROUPS = {"even": ("even_w_in", "even_w_out"), "mlp0": ("mlp_w1_0", "mlp_w2_0"),
             "odd": ("odd_w_in", "odd_w_uq", "odd_w_ukv", "odd_w_out"), "mlp1": ("mlp_w1_1", "mlp_w2_1")}


class _Exchanges:
    def __init__(self, shards, gains, n_q, n_kv):
        self.n_q, self.n_kv = n_q, n_kv
        self.mx, self.my, self.mc = _mesh_pos()
        self.dev = 4 * self.mx + 2 * self.my + self.mc
        self.shards = shards
        self.gains = gains
        self.flights, self.gathered, self.grad_blocks, self.grad_names, self.reduced = {}, {}, {}, {}, {}

    def start(self):
        names = _W_GROUPS["even"]
        got = _all_gather([self.shards[n] for n in names] + [self.gains], name="comm_even_gather")
        self.gathered.update(zip(names, got[:-1]))
        self.all_gains = got[-1][:, 0]
        return self._w_begin("mlp0", [got[0]])

    def _w_begin(self, group, after):
        srcs = [self.shards[n] for n in _W_GROUPS[group]]
        lands = [lax.dynamic_update_slice(lax.empty((N_DEV,) + t.shape, t.dtype), t[None], (self.dev, 0, 0)) for t in srcs]
        self.flights[group] = _exchange_start(_plan_own_blocks(len(srcs)), srcs + lands, after, name=f"comm_{group}_own_start")
        return [self.flights[group][3]]

    def _w_turn(self, group, after):
        bufs = _exchange_wait(self.flights[group], after, name=f"comm_{group}_own_wait")
        n = len(bufs) // 2
        self.flights[group] = _exchange_start(_plan_pass_on(n), bufs[n:], [], name=f"comm_{group}_pass_start")
        return [self.flights[group][3]]

    def _w_end(self, group, after):
        self.gathered.update(zip(_W_GROUPS[group], _exchange_wait(self.flights.pop(group), after, name=f"comm_{group}_pass_wait")))

    def weights(self, group):
        return {n: self.gathered[n] for n in _W_GROUPS[group]}

    def norm_gains(self):
        return (self.all_gains[:, :self.n_q].reshape(-1), self.all_gains[:, self.n_q:self.n_q + self.n_kv].reshape(-1))

    def grads(self, group, named_blocks):
        self.grad_names[group] = [n for n, _ in named_blocks]
        self.grad_blocks[group] = [t for _, t in named_blocks]

    def _g_begin(self, group, after):
        blocks = self.grad_blocks[group]
        lands = [lax.empty((4,) + t.shape[1:], t.dtype) for t in blocks]
        self.flights[group] = _exchange_start(_plan_to_sibling(len(blocks)), blocks + lands, after, name=f"comm_{group}_sib_start")
        return [self.flights[group][3]]

    def _g_turn(self, group, after):
        bufs = _exchange_wait(self.flights[group], after, name=f"comm_{group}_sib_wait")
        n = len(bufs) // 2
        partial = [_pair_sum(a, b, self.mc, name=f"pair_sum_{nm}") for nm, a, b in zip(self.grad_names[group], bufs[:n], bufs[n:])]
        lands = [lax.empty((3,) + t.shape[1:], t.dtype) for t in partial]
        self.flights[group] = _exchange_start(_plan_to_chips(n), partial + lands, [], name=f"comm_{group}_chips_start")
        return [self.flights[group][3]]

    def _g_end(self, group, after):
        bufs = _exchange_wait(self.flights.pop(group), after, name=f"comm_{group}_chips_wait")
        n = len(bufs) // 2
        self.reduced.update(zip(self.grad_names[group], zip(bufs[:n], bufs[n:])))

    _SCHEDULE = {
        "even_out": (("w_turn", "mlp0"), ("w_begin", "odd")),
        "ln1_l0": (("w_end", "mlp0"),),
        "ln2_l0": (("w_turn", "odd"), ("w_begin", "mlp1"), ("w_end", "odd")),
        "odd_out": (("w_turn", "mlp1"),),
        "ln1_l1": (("w_end", "mlp1"),),
        "dw_l1": (("g_begin", "mlp1"),),
        "ln1_bwd_l1": (("g_turn", "mlp1"),),
        "odd_in_dw": (("g_begin", "odd"),),
        "mlp2_dx_l0": (("g_end", "mlp1"), ("g_turn", "odd")),
        "dw_l0": (("g_begin", "mlp0"),),
        "ln1_bwd_l0": (("g_end", "odd"), ("g_turn", "mlp0")),
        "even_in_dw": (("g_begin", "even"),),
        "even_in_dx": (("g_end", "mlp0"), ("g_turn", "even")),
        "finish": (("g_end", "even"),),
    }

    def sync(self, tag, after):
        latest, started = list(after), []
        for what, group in self._SCHEDULE[tag]:
            out = getattr(self, "_" + what)(group, latest)
            if out:
                latest = started = out
        return started


def _alibi(n):
    return 2.0 ** (-8.0 * np.arange(1, n + 1, dtype=np.float32) / n)


def _heads(t, n):
    s = t.shape[0]
    return t.reshape(s, n, t.shape[1] // n).transpose(1, 0, 2)


def _unheads(t):
    n, s, dh = t.shape
    return t.transpose(1, 0, 2).reshape(s, n * dh)


def _to_strided(t, d):
    s, x = t.shape
    return t if d == 1 else t.reshape(s // d, d, x).transpose(1, 0, 2).reshape(s, x)


def _from_strided(t, d):
    s, x = t.shape
    return t if d == 1 else t.reshape(d, s // d, x).transpose(1, 0, 2).reshape(s, x)


def _true_columns(t, c, n_pad):
    tail = [jnp.zeros((t.shape[1], n_pad - N_DEV * c), t.dtype)] if n_pad > N_DEV * c else []
    return jnp.concatenate([t[dev, :, :c] for dev in range(N_DEV)] + tail, axis=1)


def _column_blocks(t, c, cp):
    return jnp.stack([jnp.pad(t[:, dev * c:(dev + 1) * c], ((0, 0), (0, cp - c))) for dev in range(N_DEV)])


def _stack_rows(t, nq):
    return t.reshape(nq, BLK, A_KV_HEADS, A_GROUP, HEAD_DIM).transpose(2, 0, 3, 1, 4).reshape(
        A_KV_HEADS, 1, nq * A_GROUP * BLK, HEAD_DIM)


def _unstack_rows(t, nq):
    return t.reshape(A_KV_HEADS, nq, A_GROUP, BLK, HEAD_DIM).transpose(1, 3, 0, 2, 4).reshape(
        nq * BLK, A_Q_W)


def _lead_block(t):
    return jnp.pad(t, ((0, 0), (BLK, 0), (0, 0)))


def _columns(t):
    return t.transpose(1, 0, 2).reshape(t.shape[1], -1)


def _rows(t):
    return t.reshape(-1, t.shape[2])


def _by_column_block(t):
    return t.reshape(t.shape[0], N_DEV, -1).transpose(1, 0, 2)


def _local_step(x0, target, comm, sinks, ln1_g, ln1_b, ln2_g, ln2_b):
    s, d = x0.shape
    scale_h = 1.0 / math.sqrt(HEAD_DIM)
    scale_d = 1.0 / math.sqrt(D_NOPE + D_ROPE)
    nq = s // BLK
    even_c, odd_c = EVEN_IN // N_DEV, ODD_IN // N_DEV
    bf = lambda t: t.astype(MXU_DTYPE)
    blocks = dict(b_blocks=True)

    tok = comm.start()
    w_even = comm.weights("even")
    even_cp, even_n = w_even["even_w_in"].shape[2], -(-EVEN_IN // 1024) * 1024
    w_even_in, w_even_out = _true_columns(w_even["even_w_in"], even_c, even_n), _columns(w_even["even_w_out"])
    x0b = bf(x0)
    h_e = _mm(x0b, w_even_in, out_dtypes=(MXU_DTYPE,), name="even_in_fwd", deps=tok)
    qa = _stack_rows(h_e[:, :A_Q_W], nq)
    ka = _lead_block(_heads(h_e[:, A_Q_W:A_Q_W + A_KV_W], A_KV_HEADS))[:, None]
    va = _lead_block(_heads(h_e[:, A_Q_W + A_KV_W:A_Q_W + 2 * A_KV_W], A_KV_HEADS))[:, None]
    rows_a = A_GROUP * BLK
    slope_a = jnp.asarray(np.repeat(_alibi(A_Q_HEADS).reshape(A_KV_HEADS, A_GROUP), BLK, axis=1).reshape(
        A_KV_HEADS, 1, rows_a, 1))
    sink_a = jnp.broadcast_to(sinks.reshape(A_KV_HEADS, A_GROUP, 1), (A_KV_HEADS, A_GROUP, BLK)).reshape(
        A_KV_HEADS, 1, rows_a, 1)
    a_cfg = dict(scale=scale_h, n_back=A_WINDOW - 1, bps=nq)
    oa, lse_a = _band_fwd(qa, ka, va, slope_a, sink_a, name="swa_fwd", **a_cfg)
    b_slab, b_cfg, b_out, b_lse = [], [], [], []
    base = A_Q_W + 2 * A_KV_W
    for gi, (window, dil) in enumerate(B_PATTERNS):
        slab = _to_strided(h_e[:, base + gi * 3 * B_W: base + (gi + 1) * 3 * B_W], dil)
        cfg = dict(slopes=_alibi(B_HEADS) * dil, scale=scale_h, n_back=window // dil, bps=nq // dil)
        o, lse = _dil_fwd(slab, name=f"dil{gi}_fwd", **cfg)
        both = _from_strided(jnp.concatenate([o, lse], axis=1), dil)
        b_slab.append(slab)
        b_cfg.append(cfg)
        b_out.append(both[:, :B_W])
        b_lse.append(both[:, B_W:])
    ob, lse_b = _merge(b_out, b_lse, name="dil_merge")
    y_e = bf(jnp.concatenate([_unstack_rows(oa, nq), ob], axis=1))
    mixed = _mm(y_e, w_even_out, name="even_out_fwd")
    tok = comm.sync("even_out", [mixed])
    x0n, x0nb, xh1_0, r1_0 = _ln_fwd(x0, mixed, ln1_g[0], ln1_b[0], name="ln1_fwd_l0", deps=tok)
    comm.sync("ln1_l0", [x0nb])
    w_mlp0 = comm.weights("mlp0")
    w1_0, w2_0 = w_mlp0["mlp_w1_0"], _rows(w_mlp0["mlp_w2_0"])
    act0, hid0 = _mm(x0nb, w1_0, out_dtypes=(MXU_DTYPE, MXU_DTYPE), epilogue=_relu_sq, name="mlp1_fwd_l0", **blocks)
    mlp = _mm(hid0, w2_0, name="mlp2_fwd_l0")
    x1, x1b, xh2_0, r2_0 = _ln_fwd(x0n, mlp, ln2_g[0], ln2_b[0], name="ln2_fwd_l0")

    tok = comm.sync("ln2_l0", [x1b])
    w_odd = comm.weights("odd")
    odd_cp, odd_n = w_odd["odd_w_in"].shape[2], -(-ODD_IN // 1024) * 1024
    w_odd_in, w_uq, w_ukv, w_odd_out = (_true_columns(w_odd["odd_w_in"], odd_c, odd_n), _columns(w_odd["odd_w_uq"]),
                                        _columns(w_odd["odd_w_ukv"]), _rows(w_odd["odd_w_out"]))
    gq, gkv = comm.norm_gains()
    h_o = _mm(x1b, w_odd_in, name="odd_in_fwd", deps=tok)
    qkv_c = bf(h_o[:, :3 * C_W])
    oc, sb_total = _sb_fwd(qkv_c, heads=C_HEADS, scale=scale_h, name="sb_fwd")
    o_cq, o_ckv, o_kr = 3 * C_W, 3 * C_W + D_Q_RANK, 3 * C_W + D_Q_RANK + D_KV_RANK
    cq, ckv, kr = h_o[:, o_cq:o_ckv], h_o[:, o_ckv:o_kr], h_o[:, o_kr:o_kr + D_ROPE]
    ncq, rq = _rms_fwd(cq, gq, name="rms_q_fwd")
    nckv, rkv = _rms_fwd(ckv, gkv, name="rms_kv_fwd")
    lane_pad = LANES - D_NOPE - D_ROPE
    w_uq = jnp.pad(w_uq.reshape(D_Q_RANK, D_HEADS, D_NOPE + D_ROPE), ((0, 0), (0, 0), (0, lane_pad))).reshape(
        D_Q_RANK, D_HEADS * LANES)
    qd = _mm(ncq, w_uq, name="uq_fwd")
    kvd = _mm(nckv, w_ukv, out_dtypes=(MXU_DTYPE,), name="ukv_fwd")
    rope_t = _rope_tables(s, inverse=False)
    krp = _rope(jnp.pad(kr, ((0, 0), (D_NOPE, lane_pad)))[None], rope_t, out_dtype=MXU_DTYPE, name="rope_k_fwd")[0]
    od, lse_d = _mla_fwd(qd, kvd, krp, rope_t, scale=scale_d, name="mla_fwd")
    y_o = bf(jnp.concatenate([oc, od], axis=1))
    mixed = _mm(y_o, w_odd_out, name="odd_out_fwd")
    tok = comm.sync("odd_out", [mixed])
    x1n, x1nb, xh1_1, r1_1 = _ln_fwd(x1, mixed, ln1_g[1], ln1_b[1], name="ln1_fwd_l1", deps=tok)
    comm.sync("ln1_l1", [x1nb])
    w_mlp1 = comm.weights("mlp1")
    w1_1, w2_1 = w_mlp1["mlp_w1_1"], _rows(w_mlp1["mlp_w2_1"])
    act1, hid1 = _mm(x1nb, w1_1, out_dtypes=(MXU_DTYPE, MXU_DTYPE), epilogue=_relu_sq, name="mlp1_fwd_l1", **blocks)
    mlp = _mm(hid1, w2_1, name="mlp2_fwd_l1")
    y, _, xh2_1, r2_1 = _ln_fwd(x1n, mlp, ln2_g[1], ln2_b[1], name="ln2_fwd_l1")
    dy, loss_vec = _loss_head(y, target, name="loss_head")

    def mlp_block_bwd(g_out, layer, w1, w2, xh2, r2, xh1, r1, act, hid, xnb):
        du2, du2b, dg2, db2 = _ln_bwd(g_out, xh2, r2, ln2_g[layer], name=f"ln2_bwd_l{layer}")
        dpre = _mm(du2b, w2, nt=True, out_dtypes=(MXU_DTYPE,), epilogue=_relu_sq_grad, extra=act, name=f"mlp2_dx_l{layer}")
        tok = comm.sync("mlp2_dx_l0", [dpre]) if layer == 0 else []
        dw2 = _mm(hid, du2b, ta=True, out_dtypes=(BF16,), name=f"mlp2_dw_l{layer}", deps=tok)
        dw1 = _mm(xnb, dpre, ta=True, out_blocks=True, out_dtypes=(BF16,), name=f"mlp1_dw_l{layer}")
        comm.grads(f"mlp{layer}", [(f"mlp_w1_{layer}", dw1), (f"mlp_w2_{layer}", dw2.reshape(N_DEV, -1, d))])
        tok = comm.sync(f"dw_l{layer}", [dw1])
        dxn = _mm(dpre, w1, nt=True, epilogue=_add_alpha, extra=du2, name=f"mlp1_dx_l{layer}", deps=tok, **blocks)
        du1, du1b, dg1, db1 = _ln_bwd(dxn, xh1, r1, ln1_g[layer], name=f"ln1_bwd_l{layer}")
        return du1, du1b, comm.sync(f"ln1_bwd_l{layer}", [du1b]), (dg1, db1, dg2, db2)

    du1, du1b, tok, ln_1 = mlp_block_bwd(dy, 1, w1_1, w2_1, xh2_1, r2_1, xh1_1, r1_1, act1, hid1, x1nb)
    d_odd_out = _mm(y_o, du1b, ta=True, out_dtypes=(BF16,), name="odd_out_dw").reshape(N_DEV, -1, d)
    dy_o = _mm(du1b, w_odd_out, nt=True, name="odd_out_dx", deps=tok)
    dqc, dkc, dvc = _sb_bwd(qkv_c, dy_o, sb_total, heads=C_HEADS, scale=scale_h, name="sb_bwd")
    dqd, dkvd, dkr_pairs = _mla_bwd(qd, kvd, krp, rope_t, dy_o, od, lse_d, do_block0=C_W // LANES, scale=scale_d,
                                    name="mla_bwd")
    _, dkr_sum = _rope(dkr_pairs, _rope_tables(s, inverse=True), out_dtype=F32, head_sum=True, name="rope_k_bwd")
    dqd, dkvd = bf(dqd), bf(dkvd)
    d_uq = _mm(ncq, dqd, ta=True, out_dtypes=(BF16,), name="uq_dw").reshape(D_Q_RANK, D_HEADS, LANES)[:, :, :D_NOPE + D_ROPE].reshape(
        D_Q_RANK, D_HEADS * (D_NOPE + D_ROPE))
    dncq = _mm(dqd, w_uq, nt=True, name="uq_dx")
    d_ukv = _mm(nckv, dkvd, ta=True, out_dtypes=(BF16,), name="ukv_dw")
    dnckv = _mm(dkvd, w_ukv, nt=True, name="ukv_dx")
    dcq, dgq = _rms_bwd(dncq, cq, rq, gq, name="rms_q_bwd")
    dckv, dgkv = _rms_bwd(dnckv, ckv, rkv, gkv, name="rms_kv_bwd")
    dh_o = bf(jnp.concatenate(
        [dqc, dkc, dvc, dcq, dckv, dkr_sum[:, D_NOPE:D_NOPE + D_ROPE], jnp.zeros((s, odd_n - ODD_IN), F32)], axis=1))
    d_odd_in = _column_blocks(_mm(x1b, dh_o, ta=True, out_dtypes=(BF16,), name="odd_in_dw"), odd_c, odd_cp)
    comm.grads("odd", [("odd_w_in", d_odd_in), ("odd_w_uq", _by_column_block(d_uq)),
                       ("odd_w_ukv", _by_column_block(d_ukv)), ("odd_w_out", d_odd_out)])
    tok = comm.sync("odd_in_dw", [d_odd_in])
    dx1 = _mm(dh_o, w_odd_in, nt=True, epilogue=_add_alpha, extra=du1, name="odd_in_dx", deps=tok)

    du1, du1b, tok, ln_0 = mlp_block_bwd(dx1, 0, w1_0, w2_0, xh2_0, r2_0, xh1_0, r1_0, act0, hid0, x0nb)
    d_even_out = _mm(y_e, du1b, ta=True, out_dtypes=(BF16,), name="even_out_dw")
    dy_e = _mm(du1b, w_even_out, nt=True, name="even_out_dx", deps=tok)
    doa = _stack_rows(dy_e[:, :A_Q_W], nq)
    dqa, dka, dva, dsink = _band_bwd(qa, ka, va, doa, oa, lse_a, slope_a, sink_a, name="swa_bwd", **a_cfg)
    pieces = [_unstack_rows(dqa, nq), _unheads(dka[:, 0, BLK:]), _unheads(dva[:, 0, BLK:])]
    pack = jnp.concatenate([dy_e[:, A_Q_W:], ob, lse_b], axis=1)
    for gi, (_, dil) in enumerate(B_PATTERNS):
        grads = _dil_bwd(b_slab[gi], _to_strided(pack, dil), name=f"dil{gi}_bwd", **b_cfg[gi])
        pieces.append(_from_strided(jnp.concatenate(grads, axis=1), dil))
    dh_e = bf(jnp.concatenate(pieces + [jnp.zeros((s, even_n - EVEN_IN), F32)], axis=1))
    d_even_in = _column_blocks(_mm(x0b, dh_e, ta=True, out_dtypes=(BF16,), name="even_in_dw"), even_c, even_cp)
    comm.grads("even", [("even_w_in", d_even_in), ("even_w_out", _by_column_block(d_even_out))])
    tok = comm.sync("even_in_dw", [d_even_in])
    grad_x = _mm(dh_e, w_even_in, nt=True, epilogue=_add_alpha, extra=du1, name="even_in_dx", deps=tok)
    tok = comm.sync("even_in_dx", [grad_x])

    ln = [jnp.concatenate([a, b], axis=0) for a, b in zip(ln_0, ln_1)]
    small = {"ln": ln, "sinks": dsink[:, :, 0].reshape(-1), "gq": dgq[0], "gkv": dgkv[0], "loss": loss_vec[0, :1]}
    return grad_x, small, tok


def kernel(x, even_w_in, even_sinks, even_w_out, odd_w_in, odd_q_norm_g, odd_kv_norm_g, odd_w_uq, odd_w_ukv, odd_w_out, ln1_g, ln1_b, mlp_w1, mlp_w2, ln2_g, ln2_b, loss_target, m_even_w_in, m_even_sinks, m_even_w_out, m_odd_w_in, m_odd_q_norm_g, m_odd_kv_norm_g, m_odd_w_uq, m_odd_w_ukv, m_odd_w_out, m_ln1_g, m_ln1_b, m_mlp_w1, m_mlp_w2, m_ln2_g, m_ln2_b, v_even_w_in, v_even_sinks, v_even_w_out, v_odd_w_in, v_odd_q_norm_g, v_odd_kv_norm_g, v_odd_w_uq, v_odd_w_ukv, v_odd_w_out, v_ln1_g, v_ln1_b, v_mlp_w1, v_mlp_w2, v_ln2_g, v_ln2_b):
    weights = dict(even_w_in=even_w_in, even_sinks=even_sinks, even_w_out=even_w_out, odd_w_in=odd_w_in,
                   odd_q_norm_g=odd_q_norm_g, odd_kv_norm_g=odd_kv_norm_g, odd_w_uq=odd_w_uq, odd_w_ukv=odd_w_ukv,
                   odd_w_out=odd_w_out, ln1_g=ln1_g, ln1_b=ln1_b, mlp_w1=mlp_w1, mlp_w2=mlp_w2, ln2_g=ln2_g, ln2_b=ln2_b)
    mom_m = dict(even_w_in=m_even_w_in, even_sinks=m_even_sinks, even_w_out=m_even_w_out, odd_w_in=m_odd_w_in,
                 odd_q_norm_g=m_odd_q_norm_g, odd_kv_norm_g=m_odd_kv_norm_g, odd_w_uq=m_odd_w_uq, odd_w_ukv=m_odd_w_ukv,
                 odd_w_out=m_odd_w_out, ln1_g=m_ln1_g, ln1_b=m_ln1_b, mlp_w1=m_mlp_w1, mlp_w2=m_mlp_w2, ln2_g=m_ln2_g, ln2_b=m_ln2_b)
    mom_v = dict(even_w_in=v_even_w_in, even_sinks=v_even_sinks, even_w_out=v_even_w_out, odd_w_in=v_odd_w_in,
                 odd_q_norm_g=v_odd_q_norm_g, odd_kv_norm_g=v_odd_kv_norm_g, odd_w_uq=v_odd_w_uq, odd_w_ukv=v_odd_w_ukv,
                 odd_w_out=v_odd_w_out, ln1_g=v_ln1_g, ln1_b=v_ln1_b, mlp_w1=v_mlp_w1, mlp_w2=v_mlp_w2, ln2_g=v_ln2_g, ln2_b=v_ln2_b)
    order = list(weights)
    n_q, n_kv = odd_q_norm_g.shape[1], odd_kv_norm_g.shape[1]

    def lane_padded(t):
        return jnp.pad(t, ((0, 0), (0, _lane_pad(t.shape[1]) - t.shape[1]))).astype(BF16)

    shards = {"even_w_in": lane_padded(even_w_in[0]), "even_w_out": even_w_out[0].astype(BF16),
              "mlp_w1_0": mlp_w1[0].astype(BF16), "mlp_w2_0": mlp_w2[0].astype(BF16),
              "odd_w_in": lane_padded(odd_w_in[0]), "odd_w_uq": odd_w_uq[0].astype(BF16),
              "odd_w_ukv": odd_w_ukv[0].astype(BF16), "odd_w_out": odd_w_out[0].astype(BF16),
              "mlp_w1_1": mlp_w1[1].astype(BF16), "mlp_w2_1": mlp_w2[1].astype(BF16)}
    gains = jnp.concatenate([odd_q_norm_g, odd_kv_norm_g, jnp.zeros((1, LANES - n_q - n_kv), F32)], axis=1)
    comm = _Exchanges(shards, gains, n_q, n_kv)
    dev = comm.dev

    grad_x, small, last_started = _local_step(x[0], loss_target[0], comm, even_sinks[0], ln1_g, ln1_b, ln2_g, ln2_b)

    grads, delta, new_m, new_v = {}, {}, {}, {}

    def update(n):
        g_list = [comm.reduced[f"{n}_0"], comm.reduced[f"{n}_1"]] if n.startswith("mlp") else [comm.reduced[n]]
        grads[n], delta[n], new_m[n], new_v[n] = _adamw(weights[n], g_list, mom_m[n], mom_v[n], name=f"adamw_{n}",
                                                        my_chip=2 * comm.mx + comm.my, deps=last_started)

    early = ("mlp_w1", "mlp_w2", "odd_w_in", "odd_w_uq", "odd_w_ukv", "odd_w_out")
    for n in early:
        update(n)
    comm.sync("finish", [new_v[n] for n in early])
    update("even_w_in")
    update("even_w_out")

    small_parts = [t.reshape(-1) for t in small["ln"]] + [small["sinks"], small["gq"], small["gkv"], small["loss"]]
    small_sizes = [p.shape[0] for p in small_parts]
    n_small = sum(small_sizes)
    small_rows = -(-n_small // (8 * LANES)) * 8
    small_flat = jnp.concatenate(small_parts + [jnp.zeros((small_rows * LANES - n_small,), F32)]).reshape(small_rows, LANES)
    (small_all,) = _all_gather([small_flat], name="comm_small_gather", deps=[grads["even_w_in"]])
    totals = _sum_devices(small_all, name="small_sum").reshape(-1)
    tot, off = [], 0
    for size in small_sizes:
        tot.append(totals[off:off + size])
        off += size
    for i, n in enumerate(("ln1_g", "ln1_b", "ln2_g", "ln2_b")):
        grads[n] = tot[i].reshape(weights[n].shape)
    grads["even_sinks"] = tot[4].reshape(even_sinks.shape)
    grads["odd_q_norm_g"] = lax.dynamic_slice(tot[5], (dev * n_q,), (n_q,)).reshape(odd_q_norm_g.shape)
    grads["odd_kv_norm_g"] = lax.dynamic_slice(tot[6], (dev * n_kv,), (n_kv,)).reshape(odd_kv_norm_g.shape)
    loss = tot[7][0]

    small_names = [n for n in order if n not in early + ("even_w_in", "even_w_out")]
    n_sm = sum(weights[n].size for n in small_names)
    sm_rows = -(-n_sm // (8 * LANES)) * 8

    def pack_small(group):
        flat = [group[n].reshape(-1) for n in small_names]
        return jnp.concatenate(flat + [jnp.zeros((sm_rows * LANES - n_sm,), F32)]).reshape(1, sm_rows, LANES)

    res = _adamw(pack_small(weights), [pack_small(grads)[0]], pack_small(mom_m), pack_small(mom_v), name="adamw_small")
    off = 0
    for n in small_names:
        size = weights[n].size
        delta[n], new_m[n], new_v[n] = (t.reshape(-1)[off:off + size].reshape(weights[n].shape) for t in res[1:])
        off += size

    return (loss, grad_x[None], *[grads[n] for n in order], *[delta[n] for n in order],
            *[new_m[n] for n in order], *[new_v[n] for n in order])
```

```python
import math

import jax
import jax.numpy as jnp
import numpy as np
from jax import lax
from jax.experimental import pallas as pl
from jax.experimental.pallas import tpu as pltpu

F32 = jnp.float32
BF16 = jnp.bfloat16
MXU_DTYPE = BF16

HEAD_DIM = 64
A_Q_HEADS, A_KV_HEADS, A_WINDOW = 16, 2, 128
A_GROUP = A_Q_HEADS // A_KV_HEADS
B_HEADS = 8
B_PATTERNS = ((128, 1), (512, 4), (2048, 16))
C_HEADS = 16
D_HEADS, D_Q_RANK, D_KV_RANK, D_NOPE, D_ROPE, D_V = 16, 512, 256, 64, 32, 64
ROPE_BASE = 10000.0
LN_EPS, RMS_EPS = 1e-5, 1e-6
DEPTH = 2
ALPHA = (2 * DEPTH) ** 0.25
A_Q_W, A_KV_W, B_W = A_Q_HEADS * HEAD_DIM, A_KV_HEADS * HEAD_DIM, B_HEADS * HEAD_DIM
EVEN_IN = A_Q_W + 2 * A_KV_W + 3 * B_W * len(B_PATTERNS)
C_W = C_HEADS * HEAD_DIM
ODD_IN = 3 * C_W + D_Q_RANK + D_KV_RANK + D_ROPE
ADAM_LR, ADAM_B1, ADAM_B2, ADAM_EPS, ADAM_WD, ADAM_STEP = 0.001, 0.9, 0.999, 1e-08, 0.01, 10

N_DEV = 8
LANES = 128
BLK = 128
CAUSAL_TILE = 512
CUM_CHUNK = 256
NEG = -1e30
VMEM_LIMIT = 48 * 1024 * 1024

NN = ((1,), (0,))
NT = ((1,), (1,))
TN = ((0,), (0,))
MESH = pl.DeviceIdType.MESH
ANY = pl.BlockSpec(memory_space=pl.ANY)
HBM_SPEC = pl.BlockSpec(memory_space=pltpu.HBM)
SEM_SPEC = pl.BlockSpec(memory_space=pltpu.SEMAPHORE)
DATAFLOW_EFFECT = pltpu.SideEffectType.DATAFLOW_SIDE_EFFECTING


def _dot(a, b, dims):
    return lax.dot_general(a, b, (dims, ((), ())), preferred_element_type=F32)


def _bdot(a, b, dims):
    return jnp.stack([_dot(a[n], b[n], dims) for n in range(a.shape[0])])


def _params(*sem):
    return pltpu.CompilerParams(dimension_semantics=tuple(sem), vmem_limit_bytes=VMEM_LIMIT)


def _pick(n, cap, mult=LANES):
    if n <= cap:
        return n
    for t in range(cap - cap % mult, 0, -mult):
        if n % t == 0:
            return t
    raise ValueError(f"no tile for {n}")


def _lane_pad(c):
    return -(-c // LANES) * LANES


def _mm(a, b, *, name, nt=False, ta=False, b_blocks=False, out_blocks=False, out_dtypes=(F32,), epilogue=None, extra=None, deps=()):
    m, k = a.shape[::-1] if ta else a.shape
    if b_blocks:
        nb, kin, c = b.shape
        n = kin if nt else nb * c
        k_full = nb * c if nt else kin
    else:
        n, k_full = (b.shape if nt else b.shape[::-1])
    assert k == k_full, (a.shape, b.shape, nt, b_blocks)
    tm = _pick(m, 1024, 8)
    if b_blocks and not nt:
        tn, tk = c, _pick(k, 3072)
    elif b_blocks:
        per_step = max(g for g in (1, 2, 4, 8) if g * c <= 2048)
        tn, tk = _pick(n, 1024), per_step * c
    elif out_blocks:
        tn, tk = n // N_DEV, _pick(k, 3072)
    else:
        tn, tk = _pick(n, 512), _pick(k, 3072)
        if k > tk:
            tn, tk = _pick(n, 1024), _pick(k, 2048)
    nk = k // tk
    n_out = len(out_dtypes)

    def body(*refs):
        a_ref, b_ref = refs[0], refs[1]
        e_ref = refs[2] if extra is not None else None
        first_out = 2 + (extra is not None) + len(deps)
        out_refs = refs[first_out:first_out + n_out]

        def finish(acc):
            e = None if e_ref is None else e_ref[...]
            outs = (acc,) if epilogue is None else epilogue(acc, e)
            for r, o in zip(out_refs, outs):
                r[...] = o.astype(r.dtype).reshape(r.shape)

        if b_blocks and nt:
            part = _dot(a_ref[:, :c], b_ref[0], NT)
            for blk in range(1, per_step):
                part += _dot(a_ref[:, blk * c:(blk + 1) * c], b_ref[blk], NT)
        elif ta:
            part = _dot(a_ref[...], b_ref[...], TN)
        else:
            part = _dot(a_ref[...], b_ref[0] if b_blocks else b_ref[...], NT if nt else NN)
        if nk == 1:
            finish(part)
        else:
            acc_ref = refs[first_out + n_out]
            kk = pl.program_id(2)

            @pl.when(kk == 0)
            def _():
                acc_ref[...] = part

            @pl.when(kk > 0)
            def _():
                acc_ref[...] += part

            @pl.when(kk == nk - 1)
            def _():
                finish(acc_ref[...])

    if b_blocks and not nt:
        b_spec = pl.BlockSpec((1, tk, tn), lambda i, j, kk: (j, kk, 0))
    elif b_blocks:
        b_spec = pl.BlockSpec((per_step, tn, c), lambda i, j, kk: (kk, j, 0))
    elif nt:
        b_spec = pl.BlockSpec((tn, tk), lambda i, j, kk: (j, kk))
    else:
        b_spec = pl.BlockSpec((tk, tn), lambda i, j, kk: (kk, j))
    a_spec = pl.BlockSpec((tk, tm), lambda i, j, kk: (kk, i)) if ta else pl.BlockSpec((tm, tk), lambda i, j, kk: (i, kk))
    in_specs = [a_spec, b_spec]
    ins = [a.astype(MXU_DTYPE), b.astype(MXU_DTYPE)]
    if extra is not None:
        in_specs.append(pl.BlockSpec((tm, tn), lambda i, j, kk: (i, j)))
        ins.append(extra)
    in_specs += [ANY] * len(deps)
    ins += list(deps)
    if out_blocks:
        out_shape = tuple(jax.ShapeDtypeStruct((N_DEV, m, tn), d) for d in out_dtypes)
        out_specs = tuple(pl.BlockSpec((1, tm, tn), lambda i, j, kk: (j, i, 0)) for _ in out_dtypes)
    else:
        out_shape = tuple(jax.ShapeDtypeStruct((m, n), d) for d in out_dtypes)
        out_specs = tuple(pl.BlockSpec((tm, tn), lambda i, j, kk: (i, j)) for _ in out_dtypes)
    outs = pl.pallas_call(
        body,
        out_shape=out_shape,
        grid=(m // tm, n // tn, nk),
        in_specs=in_specs,
        out_specs=out_specs,
        scratch_shapes=[pltpu.VMEM((tm, tn), F32)] if nk > 1 else [],
        compiler_params=_params("parallel", "parallel", "arbitrary"),
        name=name,
    )(*ins)
    return outs[0] if n_out == 1 else outs


def _relu_sq(acc, _):
    act = jnp.maximum(acc, 0.0)
    return act, act * act


def _relu_sq_grad(acc, act):
    return (acc * (2.0 * act.astype(F32)),)


def _add_alpha(acc, du):
    return (acc + ALPHA * du,)


def _ln_fwd(x, mixed, g, b, *, name, deps=()):
    s, d = x.shape
    tr = _pick(s, 256, 8)

    def body(x_ref, m_ref, g_ref, b_ref, *rest):
        y_ref, yb_ref, xh_ref, r_ref = rest[len(deps):]
        u = ALPHA * x_ref[...] + m_ref[...]
        mu = jnp.mean(u, axis=-1, keepdims=True)
        xc = u - mu
        var = jnp.mean(xc * xc, axis=-1, keepdims=True)
        r = lax.rsqrt(var + LN_EPS)
        xh = xc * r
        y = xh * g_ref[...] + b_ref[...]
        y_ref[...] = y
        yb_ref[...] = y.astype(MXU_DTYPE)
        xh_ref[...] = xh
        r_ref[...] = r

    row = pl.BlockSpec((tr, d), lambda i: (i, 0))
    vec = pl.BlockSpec((1, d), lambda i: (0, 0))
    return pl.pallas_call(
        body,
        out_shape=(jax.ShapeDtypeStruct((s, d), F32), jax.ShapeDtypeStruct((s, d), MXU_DTYPE),
                   jax.ShapeDtypeStruct((s, d), F32), jax.ShapeDtypeStruct((s, 1), F32)),
        grid=(s // tr,),
        in_specs=[row, row, vec, vec] + [ANY] * len(deps),
        out_specs=(row, row, row, pl.BlockSpec((tr, 1), lambda i: (i, 0))),
        compiler_params=_params("parallel"),
        name=name,
    )(x, mixed, g.reshape(1, d), b.reshape(1, d), *deps)


def _ln_bwd(dy, xh, r, g, *, name):
    s, d = dy.shape
    tr = _pick(s, 256, 8)

    def body(dy_ref, xh_ref, r_ref, g_ref, du_ref, dub_ref, dg_ref, db_ref):
        dyv, xhv = dy_ref[...], xh_ref[...]
        dxh = dyv * g_ref[...]
        c1 = jnp.mean(dxh, axis=-1, keepdims=True)
        c2 = jnp.mean(dxh * xhv, axis=-1, keepdims=True)
        du = r_ref[...] * (dxh - c1 - xhv * c2)
        du_ref[...] = du
        dub_ref[...] = du.astype(MXU_DTYPE)

        @pl.when(pl.program_id(0) == 0)
        def _():
            dg_ref[...] = jnp.zeros_like(dg_ref)
            db_ref[...] = jnp.zeros_like(db_ref)

        dg_ref[...] += jnp.sum(dyv * xhv, axis=0, keepdims=True)
        db_ref[...] += jnp.sum(dyv, axis=0, keepdims=True)

    row = pl.BlockSpec((tr, d), lambda i: (i, 0))
    vec = pl.BlockSpec((1, d), lambda i: (0, 0))
    return pl.pallas_call(
        body,
        out_shape=(jax.ShapeDtypeStruct((s, d), F32), jax.ShapeDtypeStruct((s, d), MXU_DTYPE),
                   jax.ShapeDtypeStruct((1, d), F32), jax.ShapeDtypeStruct((1, d), F32)),
        grid=(s // tr,),
        in_specs=[row, row, pl.BlockSpec((tr, 1), lambda i: (i, 0)), vec],
        out_specs=(row, row, vec, vec),
        compiler_params=_params("arbitrary"),
        name=name,
    )(dy, xh, r, g.reshape(1, d))


def _rms_fwd(x, g, *, name):
    s, d = x.shape
    tr = _pick(s, 512, 8)

    def body(x_ref, g_ref, y_ref, r_ref):
        xv = x_ref[...]
        r = lax.rsqrt(jnp.mean(xv * xv, axis=-1, keepdims=True) + RMS_EPS)
        y_ref[...] = (xv * r * g_ref[...]).astype(y_ref.dtype)
        r_ref[...] = r

    return pl.pallas_call(
        body,
        out_shape=(jax.ShapeDtypeStruct((s, d), MXU_DTYPE), jax.ShapeDtypeStruct((s, 1), F32)),
        grid=(s // tr,),
        in_specs=[pl.BlockSpec((tr, d), lambda i: (i, 0)), pl.BlockSpec((1, d), lambda i: (0, 0))],
        out_specs=(pl.BlockSpec((tr, d), lambda i: (i, 0)), pl.BlockSpec((tr, 1), lambda i: (i, 0))),
        compiler_params=_params("parallel"),
        name=name,
    )(x, g.reshape(1, d))


def _rms_bwd(dy, x, r, g, *, name):
    s, d = x.shape
    tr = _pick(s, 512, 8)

    def body(dy_ref, x_ref, r_ref, g_ref, dx_ref, dg_ref):
        dyv, rv = dy_ref[...], r_ref[...]
        xn = x_ref[...] * rv
        dxn = dyv * g_ref[...]
        dx_ref[...] = rv * (dxn - xn * jnp.mean(dxn * xn, axis=-1, keepdims=True))

        @pl.when(pl.program_id(0) == 0)
        def _():
            dg_ref[...] = jnp.zeros_like(dg_ref)

        dg_ref[...] += jnp.sum(dyv * xn, axis=0, keepdims=True)

    row = pl.BlockSpec((tr, d), lambda i: (i, 0))
    vec = pl.BlockSpec((1, d), lambda i: (0, 0))
    return pl.pallas_call(
        body,
        out_shape=(jax.ShapeDtypeStruct((s, d), F32), jax.ShapeDtypeStruct((1, d), F32)),
        grid=(s // tr,),
        in_specs=[row, row, pl.BlockSpec((tr, 1), lambda i: (i, 0)), vec],
        out_specs=(row, vec),
        compiler_params=_params("arbitrary"),
        name=name,
    )(dy, x, r, g.reshape(1, d))


def _rope_tables(s, inverse):
    inv_freq = ROPE_BASE ** (-jnp.arange(0, D_ROPE, 2, dtype=F32) / D_ROPE)
    ang = jnp.arange(s, dtype=F32)[:, None] * inv_freq[None, :]
    cos, sin = jnp.cos(ang), jnp.sin(ang)
    if inverse:
        sin = -sin
    half = D_ROPE // 2
    one, zero = jnp.ones((s, D_NOPE), F32), jnp.zeros((s, D_NOPE), F32)
    pad1, pad0 = jnp.ones((s, LANES - D_NOPE - D_ROPE), F32), jnp.zeros((s, LANES - D_NOPE - D_ROPE), F32)
    zh = jnp.zeros((s, half), F32)
    c = jnp.concatenate([one, cos, cos, pad1], axis=1)
    s_lo = jnp.concatenate([zero, -sin, zh, pad0], axis=1)
    s_hi = jnp.concatenate([zero, zh, sin, pad0], axis=1)
    return c, s_lo, s_hi


def _rope(x, tables, *, out_dtype, head_sum=False, name):
    h, s, w = x.shape
    ts = _pick(s, 2048, 8)

    def body(x_ref, c_ref, lo_ref, hi_ref, y_ref, *sum_ref):
        y = _rotate(x_ref[0], c_ref[...], lo_ref[...], hi_ref[...])
        y_ref[0] = y.astype(y_ref.dtype)
        if head_sum:
            @pl.when(pl.program_id(1) == 0)
            def _():
                sum_ref[0][...] = jnp.zeros_like(sum_ref[0])

            sum_ref[0][...] += y

    tab = pl.BlockSpec((ts, w), lambda i, hh: (i, 0))
    blk = pl.BlockSpec((1, ts, w), lambda i, hh: (hh, i, 0))
    out_shape = [jax.ShapeDtypeStruct((h, s, w), out_dtype)]
    out_specs = [blk]
    if head_sum:
        out_shape.append(jax.ShapeDtypeStruct((s, w), F32))
        out_specs.append(tab)
    res = pl.pallas_call(
        body,
        out_shape=tuple(out_shape),
        grid=(s // ts, h),
        in_specs=[blk, tab, tab, tab],
        out_specs=tuple(out_specs),
        compiler_params=_params("parallel", "arbitrary"),
        name=name,
    )(x, *tables)
    return res if head_sum else res[0]


def _band_scores(q, kw, slope, i, *, scale, n_back, bps):
    b, r, _ = q.shape
    sc = _bdot(q, kw, NT) * scale
    shape = (b, r, 2 * BLK)
    row = lax.broadcasted_iota(jnp.int32, shape, 1) & (BLK - 1)
    col = lax.broadcasted_iota(jnp.int32, shape, 2)
    rel = BLK + row - col
    first_col = jnp.where(i % bps == 0, BLK, 0)
    valid = (rel >= 0) & (rel <= n_back) & (col >= first_col)
    return jnp.where(valid, sc - slope * rel.astype(F32), NEG)


def _band_fwd(q, k, v, slope, sink, *, scale, n_back, bps, name):
    g, b, rows, dh = q.shape
    r = slope.shape[2]
    nq = rows // r
    skv = k.shape[2]
    use_sink = sink is not None

    def body(*refs):
        q_ref, k_ref, v_ref, slope_ref = refs[:4]
        sink_ref = refs[4] if use_sink else None
        o_ref, lse_ref = refs[4 + use_sink:]
        i = pl.program_id(1)
        off = pl.multiple_of(i * BLK, BLK)
        kw = k_ref[0, :, pl.ds(off, 2 * BLK), :]
        vw = v_ref[0, :, pl.ds(off, 2 * BLK), :]
        sc = _band_scores(q_ref[0], kw, slope_ref[0], i, scale=scale, n_back=n_back, bps=bps)
        m = jnp.max(sc, axis=-1, keepdims=True)
        if use_sink:
            m = jnp.maximum(m, sink_ref[0])
        p = jnp.exp(sc - m)
        l = jnp.sum(p, axis=-1, keepdims=True)
        if use_sink:
            l = l + jnp.exp(sink_ref[0] - m)
        o_ref[0] = _bdot(p.astype(MXU_DTYPE), vw, NN) / l
        lse_ref[0] = m + jnp.log(l)

    qspec = pl.BlockSpec((1, b, r, dh), lambda gg, i: (gg, 0, i, 0))
    kspec = pl.BlockSpec((1, b, skv, dh), lambda gg, i: (gg, 0, 0, 0))
    rspec = pl.BlockSpec((1, b, r, 1), lambda gg, i: (gg, 0, 0, 0))
    ins = [q, k, v, slope] + ([sink] if use_sink else [])
    return pl.pallas_call(
        body,
        out_shape=(jax.ShapeDtypeStruct((g, b, rows, dh), F32), jax.ShapeDtypeStruct((g, b, rows, 1), F32)),
        grid=(g, nq),
        in_specs=[qspec, kspec, kspec, rspec] + ([rspec] if use_sink else []),
        out_specs=(qspec, pl.BlockSpec((1, b, r, 1), lambda gg, i: (gg, 0, i, 0))),
        compiler_params=_params("parallel", "arbitrary"),
        name=name,
    )(*ins)


def _band_bwd(q, k, v, do, o, lse, slope, sink, *, scale, n_back, bps, name):
    g, b, rows, dh = q.shape
    r = slope.shape[2]
    nq = rows // r
    skv = k.shape[2]
    use_sink = sink is not None
    stacked = r // BLK

    def body(*refs):
        q_ref, k_ref, v_ref, do_ref, o_ref, lse_ref, slope_ref = refs[:7]
        sink_ref = refs[7] if use_sink else None
        dq_ref, dk_ref, dv_ref = refs[7 + use_sink:10 + use_sink]
        i = pl.program_id(1)

        @pl.when(i == 0)
        def _():
            dk_ref[...] = jnp.zeros_like(dk_ref)
            dv_ref[...] = jnp.zeros_like(dv_ref)

        off = pl.multiple_of(i * BLK, BLK)
        qb = q_ref[0]
        kw = k_ref[0, :, pl.ds(off, 2 * BLK), :]
        vw = v_ref[0, :, pl.ds(off, 2 * BLK), :]
        dof = do_ref[0]
        dob = dof.astype(MXU_DTYPE)
        lse_b = lse_ref[0]
        delta = jnp.sum(dof * o_ref[0], axis=-1, keepdims=True)
        sc = _band_scores(qb, kw, slope_ref[0], i, scale=scale, n_back=n_back, bps=bps)
        p = jnp.exp(sc - lse_b)
        ds = (p * (_bdot(dob, vw, NT) - delta) * scale).astype(MXU_DTYPE)
        dq_ref[0] = _bdot(ds, kw, NN)
        dk_ref[0, :, pl.ds(off, 2 * BLK), :] += _bdot(ds, qb, TN)
        dv_ref[0, :, pl.ds(off, 2 * BLK), :] += _bdot(p.astype(MXU_DTYPE), dob, TN)

        if use_sink:
            dsink_ref = refs[10 + use_sink]

            @pl.when(i == 0)
            def _():
                dsink_ref[...] = jnp.zeros_like(dsink_ref)

            contrib = -jnp.exp(sink_ref[0] - lse_b) * delta
            for n in range(stacked):
                part = jnp.sum(contrib[0, n * BLK:(n + 1) * BLK, :], axis=0, keepdims=True)
                dsink_ref[0, n:n + 1, :] += jnp.broadcast_to(part, (1, LANES))

    def qspec(w):
        return pl.BlockSpec((1, b, r, w), lambda gg, i: (gg, 0, i, 0))

    kspec = pl.BlockSpec((1, b, skv, dh), lambda gg, i: (gg, 0, 0, 0))
    rspec = pl.BlockSpec((1, b, r, 1), lambda gg, i: (gg, 0, 0, 0))
    ins = [q, k, v, do, o, lse, slope] + ([sink] if use_sink else [])
    in_specs = [qspec(dh), kspec, kspec, qspec(dh), qspec(dh), qspec(1), rspec] + ([rspec] if use_sink else [])
    out_shape = [jax.ShapeDtypeStruct((g, b, rows, dh), F32), jax.ShapeDtypeStruct((g, b, skv, dh), F32),
                 jax.ShapeDtypeStruct((g, b, skv, dh), F32)]
    out_specs = [qspec(dh), kspec, kspec]
    if use_sink:
        assert b == 1
        out_shape.append(jax.ShapeDtypeStruct((g, stacked, LANES), F32))
        out_specs.append(pl.BlockSpec((1, stacked, LANES), lambda gg, i: (gg, 0, 0)))
    return pl.pallas_call(
        body,
        out_shape=tuple(out_shape),
        grid=(g, nq),
        in_specs=in_specs,
        out_specs=tuple(out_specs),
        compiler_params=_params("parallel", "arbitrary"),
        name=name,
    )(*ins)


def _pair_masks():
    first = lax.broadcasted_iota(jnp.int32, (1, LANES), 1) < HEAD_DIM
    m0 = first.astype(MXU_DTYPE)
    return first, (m0, 1 - m0)


def _dil_window(ref, i):
    prev = pl.multiple_of(jnp.maximum(i - 1, 0) * BLK, BLK)
    cur = pl.multiple_of(i * BLK, BLK)
    return prev, cur, jnp.concatenate([ref[pl.ds(prev, BLK), :], ref[pl.ds(cur, BLK), :]], axis=0)


def _dil_mask(i, n_back, bps):
    row = lax.broadcasted_iota(jnp.int32, (BLK, 2 * BLK), 0)
    col = lax.broadcasted_iota(jnp.int32, (BLK, 2 * BLK), 1)
    rel = BLK + row - col
    first_col = jnp.where(i % bps == 0, BLK, 0)
    return (rel >= 0) & (rel <= n_back) & (col >= first_col), rel.astype(F32)


def _dil_fwd(slab, slopes, *, scale, n_back, bps, name):
    s, w = slab.shape[0], slab.shape[1] // 3

    def body(q_ref, k_ref, v_ref, o_ref, lse_ref):
        i = pl.program_id(0)
        first, masks = _pair_masks()
        _, _, kw = _dil_window(k_ref, i)
        _, _, vw = _dil_window(v_ref, i)
        valid, rel = _dil_mask(i, n_back, bps)
        for p in range(w // LANES):
            cols = slice(p * LANES, (p + 1) * LANES)
            qp, kp, vp = q_ref[:, cols], kw[:, cols], vw[:, cols]
            outs, lses = [], []
            for hh in range(2):
                sc = _dot(qp * masks[hh], kp, NT) * scale - float(slopes[2 * p + hh]) * rel
                sc = jnp.where(valid, sc, NEG)
                m = jnp.max(sc, axis=-1, keepdims=True)
                e = jnp.exp(sc - m)
                l = jnp.sum(e, axis=-1, keepdims=True)
                outs.append(_dot(e.astype(MXU_DTYPE), vp, NN) / l)
                lses.append(m + jnp.log(l))
            o_ref[:, cols] = jnp.where(first, outs[0], outs[1])
            lse_ref[:, cols] = jnp.where(first, lses[0], lses[1])

    blk = pl.BlockSpec((BLK, w), lambda i: (i, 0))
    return pl.pallas_call(
        body,
        out_shape=(jax.ShapeDtypeStruct((s, w), F32), jax.ShapeDtypeStruct((s, w), F32)),
        grid=(s // BLK,),
        in_specs=[blk, pl.BlockSpec((s, w), lambda i: (0, 1)), pl.BlockSpec((s, w), lambda i: (0, 2))],
        out_specs=(blk, blk),
        compiler_params=_params("arbitrary"),
        name=name,
    )(slab, slab, slab)


def _dil_bwd(slab, pack, slopes, *, scale, n_back, bps, name):
    s, w = slab.shape[0], slab.shape[1] // 3

    def body(q_ref, k_ref, v_ref, do_ref, o_ref, lse_ref, dq_ref, dk_ref, dv_ref):
        i = pl.program_id(0)

        @pl.when(i == 0)
        def _():
            dk_ref[...] = jnp.zeros_like(dk_ref)
            dv_ref[...] = jnp.zeros_like(dv_ref)

        first, masks = _pair_masks()
        prev, cur, kw = _dil_window(k_ref, i)
        _, _, vw = _dil_window(v_ref, i)
        valid, rel = _dil_mask(i, n_back, bps)
        for p in range(w // LANES):
            cols = slice(p * LANES, (p + 1) * LANES)
            qp, kp, vp = q_ref[:, cols], kw[:, cols], vw[:, cols]
            dof, lse_p = do_ref[:, cols], lse_ref[:, cols]
            prod = dof * o_ref[:, cols]
            do_b = dof.astype(MXU_DTYPE)
            dqs, dk_add, dv_add = [], None, None
            for hh in range(2):
                qh, doh = qp * masks[hh], do_b * masks[hh]
                delta = jnp.sum(jnp.where(first, prod, 0.0) if hh == 0 else jnp.where(first, 0.0, prod), axis=-1, keepdims=True)
                sc = _dot(qh, kp, NT) * scale - float(slopes[2 * p + hh]) * rel
                e = jnp.exp(jnp.where(valid, sc, NEG) - lse_p[:, hh * HEAD_DIM:hh * HEAD_DIM + 1])
                ds = (e * (_dot(doh, vp, NT) - delta) * scale).astype(MXU_DTYPE)
                dqs.append(_dot(ds, kp, NN))
                dk_h, dv_h = _dot(ds, qh, TN), _dot(e.astype(MXU_DTYPE), doh, TN)
                dk_add = dk_h if dk_add is None else dk_add + dk_h
                dv_add = dv_h if dv_add is None else dv_add + dv_h
            dq_ref[:, cols] = jnp.where(first, dqs[0], dqs[1])
            dk_ref[pl.ds(prev, BLK), cols] += dk_add[:BLK]
            dk_ref[pl.ds(cur, BLK), cols] += dk_add[BLK:]
            dv_ref[pl.ds(prev, BLK), cols] += dv_add[:BLK]
            dv_ref[pl.ds(cur, BLK), cols] += dv_add[BLK:]

    def blk(c):
        return pl.BlockSpec((BLK, w), lambda i: (i, c))

    def whole(c):
        return pl.BlockSpec((s, w), lambda i: (0, c))

    shp = jax.ShapeDtypeStruct((s, w), F32)
    return pl.pallas_call(
        body,
        out_shape=(shp, shp, shp),
        grid=(s // BLK,),
        in_specs=[blk(0), whole(1), whole(2), blk(0), blk(1), blk(2)],
        out_specs=(blk(0), whole(0), whole(0)),
        compiler_params=_params("arbitrary"),
        name=name,
    )(slab, slab, slab, pack, pack, pack)


def _merge(outs, lses, *, name):
    s, w = outs[0].shape
    tr = _pick(s, 512, 8)

    def body(o0, o1, o2, l0, l1, l2, ob_ref, lt_ref):
        a, b, c = l0[...], l1[...], l2[...]
        m = jnp.maximum(jnp.maximum(a, b), c)
        ea, eb, ec = jnp.exp(a - m), jnp.exp(b - m), jnp.exp(c - m)
        den = ea + eb + ec
        ob_ref[...] = (ea / den) * o0[...] + (eb / den) * o1[...] + (ec / den) * o2[...]
        lt_ref[...] = m + jnp.log(den)

    spec = pl.BlockSpec((tr, w), lambda i: (i, 0))
    return pl.pallas_call(
        body,
        out_shape=(jax.ShapeDtypeStruct((s, w), F32), jax.ShapeDtypeStruct((s, w), F32)),
        grid=(s // tr,),
        in_specs=[spec] * 6,
        out_specs=(spec, spec),
        compiler_params=_params("parallel"),
        name=name,
    )(*outs, *lses)


def _tile_iotas(t):
    return lax.broadcasted_iota(jnp.int32, (t, t), 0), lax.broadcasted_iota(jnp.int32, (t, t), 1)


def _rotate(x, c, s_lo, s_hi):
    half = D_ROPE // 2
    return x * c + pltpu.roll(x, LANES - half, 1) * s_lo + pltpu.roll(x, half, 1) * s_hi


def _mla_keys(kv_h, kr_t, first):
    return jnp.where(first, kv_h, kr_t)


def _mla_fwd(qd, kvd, krp, tables, *, scale, name):
    s = qd.shape[0]
    pairs = qd.shape[1] // (2 * LANES)
    t = min(CAUSAL_TILE, s)

    def body(q_ref, kv_ref, kr_ref, c_ref, lo_ref, hi_ref, o_ref, lse_ref):
        i = pl.program_id(1)
        first = lax.broadcasted_iota(jnp.int32, (1, LANES), 1) < HEAD_DIM
        tabs = (c_ref[...], lo_ref[...], hi_ref[...])
        q_heads = [_rotate(q_ref[:, hh * LANES:(hh + 1) * LANES], *tabs).astype(MXU_DTYPE) for hh in range(2)]

        def tile(j, carry, diagonal):
            off = pl.multiple_of(j * t, t)
            kr_t = kr_ref[pl.ds(off, t), :]
            out = []
            for hh in range(2):
                m, l, acc = carry[3 * hh:3 * hh + 3]
                kv_h = kv_ref[pl.ds(off, t), hh * LANES:(hh + 1) * LANES]
                sc = _dot(q_heads[hh], _mla_keys(kv_h, kr_t, first), NT) * scale
                if diagonal:
                    row, col = _tile_iotas(t)
                    sc = jnp.where(row >= col, sc, NEG)
                m_new = jnp.maximum(m, jnp.max(sc, axis=-1, keepdims=True))
                a = jnp.exp(m - m_new)
                p = jnp.exp(sc - m_new)
                out += [m_new, a * l + jnp.sum(p, axis=-1, keepdims=True), a * acc + _dot(p.astype(MXU_DTYPE), kv_h, NN)]
            return tuple(out)

        init = (jnp.full((t, 1), NEG, F32), jnp.zeros((t, 1), F32), jnp.zeros((t, LANES), F32)) * 2
        carry = lax.fori_loop(0, i, lambda j, c: tile(j, c, False), init)
        m0, l0, acc0, m1, l1, acc1 = tile(i, carry, True)
        o_ref[...] = jnp.where(first, pltpu.roll(acc0 / l0, HEAD_DIM, 1), acc1 / l1)
        lse_ref[0] = jnp.where(lax.broadcasted_iota(jnp.int32, (t, 2), 1) == 0, m0 + jnp.log(l0), m1 + jnp.log(l1))

    tab = pl.BlockSpec((t, LANES), lambda p, i: (i, 0))
    return pl.pallas_call(
        body,
        out_shape=(jax.ShapeDtypeStruct((s, pairs * LANES), F32), jax.ShapeDtypeStruct((pairs, s, 2), F32)),
        grid=(pairs, s // t),
        in_specs=[pl.BlockSpec((t, 2 * LANES), lambda p, i: (i, p)), pl.BlockSpec((s, 2 * LANES), lambda p, i: (0, p)),
                  pl.BlockSpec((s, LANES), lambda p, i: (0, 0)), tab, tab, tab],
        out_specs=(pl.BlockSpec((t, LANES), lambda p, i: (i, p)), pl.BlockSpec((1, t, 2), lambda p, i: (p, i, 0))),
        compiler_params=_params("parallel", "arbitrary"),
        name=name,
    )(qd, kvd, krp, *tables)


def _mla_bwd(qd, kvd, krp, tables, do, o, lse, *, do_block0, scale, name):
    s = qd.shape[0]
    pairs = qd.shape[1] // (2 * LANES)
    t = min(CAUSAL_TILE, s)

    def body(q_ref, kv_ref, kr_ref, c_ref, lo_ref, hi_ref, do_ref, o_ref, lse_ref, dq_ref, dkv_ref, dkr_ref):
        i = pl.program_id(1)

        @pl.when(i == 0)
        def _():
            dkv_ref[...] = jnp.zeros_like(dkv_ref)
            dkr_ref[...] = jnp.zeros_like(dkr_ref)

        first = lax.broadcasted_iota(jnp.int32, (1, LANES), 1) < HEAD_DIM
        tabs = (c_ref[...], lo_ref[...], hi_ref[...])
        q_heads = [_rotate(q_ref[:, hh * LANES:(hh + 1) * LANES], *tabs).astype(MXU_DTYPE) for hh in range(2)]
        dof = do_ref[...]
        prod = dof * o_ref[...]
        deltas = [jnp.sum(jnp.where(first, prod, 0.0), axis=-1, keepdims=True),
                  jnp.sum(jnp.where(first, 0.0, prod), axis=-1, keepdims=True)]
        do_heads = [jnp.where(first, 0.0, pltpu.roll(dof, HEAD_DIM, 1)).astype(MXU_DTYPE),
                    jnp.where(first, 0.0, dof).astype(MXU_DTYPE)]
        lses = [lse_ref[0][:, hh:hh + 1] for hh in range(2)]

        def tile(j, carry, diagonal):
            off = pl.multiple_of(j * t, t)
            kr_t = kr_ref[pl.ds(off, t), :]
            out, dkr_add = [], None
            for hh in range(2):
                kv_h = kv_ref[pl.ds(off, t), hh * LANES:(hh + 1) * LANES]
                k_h = _mla_keys(kv_h, kr_t, first)
                sc = _dot(q_heads[hh], k_h, NT) * scale
                if diagonal:
                    row, col = _tile_iotas(t)
                    sc = jnp.where(row >= col, sc, NEG)
                p = jnp.exp(sc - lses[hh])
                ds = (p * (_dot(do_heads[hh], kv_h, NT) - deltas[hh]) * scale).astype(MXU_DTYPE)
                dk_full = _dot(ds, q_heads[hh], TN)
                dv_full = _dot(p.astype(MXU_DTYPE), do_heads[hh], TN)
                dkv_ref[pl.ds(off, t), hh * LANES:(hh + 1) * LANES] += jnp.where(first, dk_full, dv_full)
                rot = jnp.where(first, 0.0, dk_full)
                dkr_add = rot if dkr_add is None else dkr_add + rot
                out.append(carry[hh] + _dot(ds, k_h, NN))
            dkr_ref[0, pl.ds(off, t), :] += dkr_add
            return tuple(out)

        zacc = jnp.zeros((t, LANES), F32)
        carry = lax.fori_loop(0, i, lambda j, c: tile(j, c, False), (zacc, zacc))
        dq_heads = tile(i, carry, True)
        for hh in range(2):
            dq_ref[:, hh * LANES:(hh + 1) * LANES] = _rotate(dq_heads[hh], tabs[0], -tabs[1], -tabs[2])

    tab = pl.BlockSpec((t, LANES), lambda p, i: (i, 0))
    qspec = pl.BlockSpec((t, 2 * LANES), lambda p, i: (i, p))
    kvspec = pl.BlockSpec((s, 2 * LANES), lambda p, i: (0, p))
    return pl.pallas_call(
        body,
        out_shape=(jax.ShapeDtypeStruct(qd.shape, F32), jax.ShapeDtypeStruct(kvd.shape, F32),
                   jax.ShapeDtypeStruct((pairs, s, LANES), F32)),
        grid=(pairs, s // t),
        in_specs=[qspec, kvspec, pl.BlockSpec((s, LANES), lambda p, i: (0, 0)), tab, tab, tab,
                  pl.BlockSpec((t, LANES), lambda p, i: (i, do_block0 + p)), pl.BlockSpec((t, LANES), lambda p, i: (i, p)),
                  pl.BlockSpec((1, t, 2), lambda p, i: (p, i, 0))],
        out_specs=(qspec, kvspec, pl.BlockSpec((1, s, LANES), lambda p, i: (p, 0, 0))),
        compiler_params=_params("parallel", "arbitrary"),
        name=name,
    )(qd, kvd, krp, *tables, do, o, lse)


def _split_cumsum(x, tri, terms=2):
    hi = x.astype(BF16)
    if terms == 1:
        return _dot(hi, tri, NN)
    lo = (x - hi.astype(F32)).astype(BF16)
    return _dot(hi, tri, NN) + _dot(lo, tri, NN)


def _chunked_cumsum(x, tri, run, *, reverse, negate=False, terms=2):
    c = tri.shape[0]
    n = x.shape[1] // c
    parts = [None] * n
    for idx in (reversed(range(n)) if reverse else range(n)):
        xc = x[:, idx * c:(idx + 1) * c]
        sums = _split_cumsum(xc, tri, terms)
        parts[idx] = (-run) - sums if negate else run + sums
        run = run + jnp.sum(xc, axis=-1, keepdims=True)
    return (parts[0] if n == 1 else jnp.concatenate(parts, axis=1)), run


def _sb_logs(z):
    l1 = jnp.log(1.0 + jnp.exp(-jnp.abs(z)))
    return jnp.minimum(z, 0.0) - l1, -jnp.maximum(z, 0.0) - l1


def _scaled_query_heads(q, masks, scale):
    assert math.log2(scale).is_integer(), scale
    return [q * (m * scale).astype(q.dtype) for m in masks]


def _sb_fwd(qkv, *, heads, scale, name):
    s = qkv.shape[0]
    pairs = heads * HEAD_DIM // LANES
    t = min(CAUSAL_TILE, s)
    cc = min(CUM_CHUNK, t)

    def body(q_ref, k_ref, v_ref, o_ref, t_ref):
        i = pl.program_id(1)
        first, masks = _pair_masks()
        q_heads = _scaled_query_heads(q_ref[...], masks, scale)
        crow, ccol = _tile_iotas(cc)
        after = (crow > ccol).astype(BF16)

        def tile(j, carry, diagonal):
            off = pl.multiple_of(j * t, t)
            kb = k_ref[pl.ds(off, t), :]
            vb = v_ref[pl.ds(off, t), :]
            if diagonal:
                row, col = _tile_iotas(t)
                strict = row > col
            out = []
            for hh in range(2):
                run, acc = carry[2 * hh], carry[2 * hh + 1]
                log_beta, log_keep = _sb_logs(_dot(q_heads[hh], kb, NT))
                if diagonal:
                    log_keep = jnp.where(strict, log_keep, 0.0)
                a, run = _chunked_cumsum(log_keep, after, run, reverse=True)
                w = jnp.exp(log_beta + a)
                if diagonal:
                    w = jnp.where(strict, w, 0.0)
                out += [run, acc + _dot(w.astype(MXU_DTYPE), vb, NN)]
            return tuple(out)

        zero, zacc = jnp.zeros((t, 1), F32), jnp.zeros((t, LANES), F32)
        carry = tile(i, (zero, zacc, zero, zacc), True)
        run0, acc0, run1, acc1 = lax.fori_loop(0, i, lambda jj, c: tile(i - 1 - jj, c, False), carry)
        o_ref[...] = jnp.where(first, acc0, acc1)
        t_ref[0] = jnp.where(lax.broadcasted_iota(jnp.int32, (t, 2), 1) == 0, run0, run1)

    return pl.pallas_call(
        body,
        out_shape=(jax.ShapeDtypeStruct((s, heads * HEAD_DIM), F32), jax.ShapeDtypeStruct((pairs, s, 2), F32)),
        grid=(pairs, s // t),
        in_specs=[pl.BlockSpec((t, LANES), lambda p, i: (i, p)),
                  pl.BlockSpec((s, LANES), lambda p, i: (0, pairs + p)),
                  pl.BlockSpec((s, LANES), lambda p, i: (0, 2 * pairs + p))],
        out_specs=(pl.BlockSpec((t, LANES), lambda p, i: (i, p)), pl.BlockSpec((1, t, 2), lambda p, i: (p, i, 0))),
        compiler_params=_params("parallel", "arbitrary"),
        name=name,
    )(qkv, qkv, qkv)


def _sb_bwd(qkv, do, total, *, heads, scale, name):
    s = qkv.shape[0]
    pairs = heads * HEAD_DIM // LANES
    t = min(CAUSAL_TILE, s)
    cc = min(CUM_CHUNK, t)

    def body(q_ref, k_ref, v_ref, do_ref, t_ref, dq_ref, dk_ref, dv_ref):
        i = pl.program_id(1)

        @pl.when(i == 0)
        def _():
            dk_ref[...] = jnp.zeros_like(dk_ref)
            dv_ref[...] = jnp.zeros_like(dv_ref)

        first, masks = _pair_masks()
        q_heads = _scaled_query_heads(q_ref[...], masks, scale)
        do_b = do_ref[...].astype(MXU_DTYPE)
        do_heads = [do_b * m for m in masks]
        tots = [t_ref[0][:, hh:hh + 1] for hh in range(2)]
        crow, ccol = _tile_iotas(cc)
        upto = (crow <= ccol).astype(BF16)
        before = (crow < ccol).astype(BF16)

        def tile(j, carry, diagonal):
            off = pl.multiple_of(j * t, t)
            kb = k_ref[pl.ds(off, t), :]
            vb = v_ref[pl.ds(off, t), :]
            if diagonal:
                row, col = _tile_iotas(t)
                strict = row > col
            out, dk_add, dv_add = [], None, None
            for hh in range(2):
                run_keep, run_g, dq_acc = carry[3 * hh:3 * hh + 3]
                log_beta, log_keep = _sb_logs(_dot(q_heads[hh], kb, NT))
                keep = jnp.exp(log_keep)
                if diagonal:
                    log_keep = jnp.where(strict, log_keep, 0.0)
                a, run_keep = _chunked_cumsum(log_keep, upto, run_keep - tots[hh], reverse=False, negate=True)
                run_keep = run_keep + tots[hh]
                w = jnp.exp(log_beta + a)
                if diagonal:
                    w = jnp.where(strict, w, 0.0)
                g = w * _dot(do_heads[hh], vb, NT)
                prefix, run_g = _chunked_cumsum(g, before, run_g, reverse=False, terms=1)
                dz = g * keep - (1.0 - keep) * prefix
                if diagonal:
                    dz = jnp.where(strict, dz, 0.0)
                dz = dz.astype(MXU_DTYPE)
                dk_h = _dot(dz, q_heads[hh], TN)
                dv_h = _dot(w.astype(MXU_DTYPE), do_heads[hh], TN)
                dk_add = dk_h if dk_add is None else dk_add + dk_h
                dv_add = dv_h if dv_add is None else dv_add + dv_h
                out += [run_keep, run_g, dq_acc + _dot(dz, kb, NN)]
            dk_ref[pl.ds(off, t), :] += dk_add
            dv_ref[pl.ds(off, t), :] += dv_add
            return tuple(out)

        zero, zacc = jnp.zeros((t, 1), F32), jnp.zeros((t, LANES), F32)
        carry = lax.fori_loop(0, i, lambda j, c: tile(j, c, False), (zero, zero, zacc, zero, zero, zacc))
        res = tile(i, carry, True)
        dq_ref[...] = jnp.where(first, res[2], res[5]) * scale

    qspec = pl.BlockSpec((t, LANES), lambda p, i: (i, p))
    shp = jax.ShapeDtypeStruct((s, heads * HEAD_DIM), F32)
    return pl.pallas_call(
        body,
        out_shape=(shp, shp, shp),
        grid=(pairs, s // t),
        in_specs=[qspec, pl.BlockSpec((s, LANES), lambda p, i: (0, pairs + p)),
                  pl.BlockSpec((s, LANES), lambda p, i: (0, 2 * pairs + p)), qspec,
                  pl.BlockSpec((1, t, 2), lambda p, i: (p, i, 0))],
        out_specs=(qspec, pl.BlockSpec((s, LANES), lambda p, i: (0, p)), pl.BlockSpec((s, LANES), lambda p, i: (0, p))),
        compiler_params=_params("parallel", "arbitrary"),
        name=name,
    )(qkv, qkv, qkv, do, total)


def _loss_head(y, target, *, name):
    s, d = y.shape
    tr = _pick(s, 256, 8)

    def body(y_ref, t_ref, dy_ref, loss_ref):
        err = y_ref[...] - t_ref[...]
        dy_ref[...] = err * (1.0 / d)

        @pl.when(pl.program_id(0) == 0)
        def _():
            loss_ref[...] = jnp.zeros_like(loss_ref)

        per_tok = jnp.mean(err * err, axis=-1, keepdims=True)
        loss_ref[...] += 0.5 * jnp.sum(per_tok, axis=0, keepdims=True)

    row = pl.BlockSpec((tr, d), lambda i: (i, 0))
    return pl.pallas_call(
        body,
        out_shape=(jax.ShapeDtypeStruct((s, d), F32), jax.ShapeDtypeStruct((1, LANES), F32)),
        grid=(s // tr,),
        in_specs=[row, row],
        out_specs=(row, pl.BlockSpec((1, LANES), lambda i: (0, 0))),
        compiler_params=_params("arbitrary"),
        name=name,
    )(y, target)


def _adamw(w, grads, m, v, *, name, my_chip=None, deps=()):
    nl, r, c = w.shape
    pieces = my_chip is not None
    cp = (grads[0][0] if pieces else grads[0]).shape[-1]
    tr = _pick(r, min(256, max(16, 262144 // cp)), 8)
    per = 4 if pieces else 1

    def body(chip_ref, *refs):
        w_ref, m_ref, v_ref = refs[:3]
        g_refs = refs[3:3 + per * nl]
        g_out, d_ref, m2_ref, v2_ref = refs[3 + per * nl + len(deps):]

        def grad(n):
            if not pieces:
                return g_refs[n][:, :c]
            own, r0, r1, r2 = (t[0, :, :c].astype(F32) for t in g_refs[4 * n:4 * n + 4])
            return ((own + r0) + r1) + r2

        layer = pl.program_id(0)
        gv = grad(0)
        for n in range(1, nl):
            gv = jnp.where(layer == n, grad(n), gv)
        m2 = ADAM_B1 * m_ref[0] + (1.0 - ADAM_B1) * gv
        v2 = ADAM_B2 * v_ref[0] + (1.0 - ADAM_B2) * (gv * gv)
        m_hat = m2 / (1.0 - ADAM_B1 ** ADAM_STEP)
        v_hat = v2 / (1.0 - ADAM_B2 ** ADAM_STEP)
        g_out[0] = gv
        d_ref[0] = -ADAM_LR * (m_hat / (jnp.sqrt(v_hat) + ADAM_EPS) + ADAM_WD * w_ref[0])
        m2_ref[0] = m2
        v2_ref[0] = v2

    blk = pl.BlockSpec((1, tr, c), lambda l, i, chip_ref: (l, i, 0))
    if pieces:
        g_specs = [pl.BlockSpec((1, tr, cp), lambda l, i, chip_ref: (chip_ref[0], i, 0))]
        g_specs += [pl.BlockSpec((1, tr, cp), lambda l, i, chip_ref, k=k: (k, i, 0)) for k in range(3)]
        g_ins = [t for partial, recv in grads for t in (partial, recv, recv, recv)]
        chip = my_chip.reshape(1).astype(jnp.int32)
    else:
        g_specs, g_ins, chip = [pl.BlockSpec((tr, cp), lambda l, i, chip_ref: (i, 0))], list(grads), jnp.zeros((1,), jnp.int32)
    shp = jax.ShapeDtypeStruct((nl, r, c), F32)
    grid_spec = pltpu.PrefetchScalarGridSpec(
        num_scalar_prefetch=1,
        grid=(nl, r // tr),
        in_specs=[blk, blk, blk] + g_specs * nl + [ANY] * len(deps),
        out_specs=(blk, blk, blk, blk),
    )
    return pl.pallas_call(
        body,
        out_shape=(shp, shp, shp, shp),
        grid_spec=grid_spec,
        compiler_params=_params("parallel", "parallel"),
        name=name,
    )(chip, w, m, v, *g_ins, *deps)


def _pair_sum(mine, recv, my_c, *, name):
    _, r, c = mine.shape
    tr = _pick(r, 512, 16)

    def body(c_ref, a_ref, b_ref, o_ref):
        o_ref[0] = (a_ref[0].astype(F32) + b_ref[0].astype(F32)).astype(o_ref.dtype)

    grid_spec = pltpu.PrefetchScalarGridSpec(
        num_scalar_prefetch=1,
        grid=(4, r // tr),
        in_specs=[pl.BlockSpec((1, tr, c), lambda kk, i, c_ref: (2 * kk + c_ref[0], i, 0)),
                  pl.BlockSpec((1, tr, c), lambda kk, i, c_ref: (kk, i, 0))],
        out_specs=pl.BlockSpec((1, tr, c), lambda kk, i, c_ref: (kk, i, 0)),
    )
    return pl.pallas_call(
        body,
        out_shape=jax.ShapeDtypeStruct((4, r, c), mine.dtype),
        grid_spec=grid_spec,
        compiler_params=_params("parallel", "parallel"),
        name=name,
    )(my_c.reshape(1).astype(jnp.int32), mine, recv)


def _sum_devices(stack, *, name):
    n, r, c = stack.shape

    def body(s_ref, o_ref):
        acc = s_ref[0]
        for dev in range(1, n):
            acc = acc + s_ref[dev]
        o_ref[...] = acc

    return pl.pallas_call(
        body,
        out_shape=jax.ShapeDtypeStruct((r, c), F32),
        in_specs=[pl.BlockSpec(memory_space=pltpu.VMEM)],
        out_specs=pl.BlockSpec(memory_space=pltpu.VMEM),
        name=name,
    )(stack)


def _mesh_pos():
    return lax.axis_index("x"), lax.axis_index("y"), lax.axis_index("c")


def _all_gather(shards, *, name, deps=()):
    n = len(shards)

    def body(*refs):
        x_refs, out_refs = refs[:n], refs[n + len(deps):2 * n + len(deps)]
        send_sems, recv_sems, local_sems = refs[2 * n + len(deps):]
        x, y, cc = _mesh_pos()
        me, sibling = (x, y, cc), (x, y, 1 - cc)
        flip_x, flip_y = cc, 1 - cc
        first = (x + flip_x - 2 * x * flip_x, y + flip_y - 2 * y * flip_y)
        other = (x + flip_y - 2 * x * flip_y, y + flip_x - 2 * y * flip_x)
        diagonal = (1 - x, 1 - y)

        def rows(a, px, py, pc):
            return out_refs[a].at[4 * px + 2 * py + pc]

        def copy(a, kk, block, to, src=None):
            return pltpu.make_async_remote_copy(
                src_ref=rows(a, *block) if src is None else src, dst_ref=rows(a, *block),
                send_sem=send_sems.at[7 * a + kk], recv_sem=recv_sems.at[7 * a + kk],
                device_id=to, device_id_type=MESH)

        sends, own = [], []
        for a in range(n):
            own.append(pltpu.make_async_copy(x_refs[a], rows(a, *me), local_sems.at[a]))
            own[a].start()
            out = [copy(a, 0, me, sibling, src=x_refs[a]), copy(a, 1, me, (*first, cc), src=x_refs[a]),
                   copy(a, 2, me, (*other, cc), src=x_refs[a])]
            for cp in out:
                cp.start()
            sends += out
        for a in range(n):
            copy(a, 1, (*first, cc), me).wait_recv()
            out = [copy(a, 3, (*first, cc), (*other, cc)), copy(a, 4, (*first, cc), sibling)]
            for cp in out:
                cp.start()
            copy(a, 2, (*other, cc), me).wait_recv()
            out.append(copy(a, 5, (*other, cc), sibling))
            out[2].start()
            sends += out
        for a in range(n):
            copy(a, 3, (*diagonal, cc), me).wait_recv()
            passed = copy(a, 6, (*diagonal, cc), sibling)
            passed.start()
            sends.append(passed)
        for a in range(n):
            copy(a, 0, sibling, me).wait_recv()
            copy(a, 4, (*other, 1 - cc), me).wait_recv()
            copy(a, 5, (*first, 1 - cc), me).wait_recv()
            copy(a, 6, (*diagonal, 1 - cc), me).wait_recv()
        for cp in sends:
            cp.wait_send()
        for cp in own:
            cp.wait()

    return pl.pallas_call(
        body,
        out_shape=tuple(jax.ShapeDtypeStruct((N_DEV,) + t.shape, t.dtype) for t in shards),
        in_specs=[ANY] * (n + len(deps)),
        out_specs=tuple([ANY] * n),
        scratch_shapes=[pltpu.SemaphoreType.DMA((7 * n,)), pltpu.SemaphoreType.DMA((7 * n,)),
                        pltpu.SemaphoreType.DMA((n,))],
        name=name,
    )(*shards, *deps)


def _plan_own_blocks(n):
    def plan(refs, send_sems, recv_sems, outgoing):
        x, y, cc = _mesh_pos()
        peers = [(x, y, 1 - cc), (1 - x, y, cc), (x, 1 - y, cc), (1 - x, 1 - y, cc)]
        copies = []
        for a in range(n):
            land = refs[n + a]
            for kk, (px, py, pc) in enumerate(peers):
                block = (x, y, cc) if outgoing else (px, py, pc)
                rows = land.at[4 * block[0] + 2 * block[1] + block[2]]
                copies.append(pltpu.make_async_remote_copy(
                    src_ref=refs[a] if outgoing else rows, dst_ref=rows, send_sem=send_sems.at[4 * a + kk],
                    recv_sem=recv_sems.at[4 * a + kk], device_id=(px, py, pc), device_id_type=MESH))
        return copies

    plan.n_sems = 4 * n
    return plan


def _plan_pass_on(n):
    def plan(refs, send_sems, recv_sems, outgoing):
        x, y, cc = _mesh_pos()
        copies = []
        for a in range(n):
            for j, (px, py) in enumerate([(1 - x, y), (x, 1 - y), (1 - x, 1 - y)]):
                rows = refs[a].at[4 * px + 2 * py + (cc if outgoing else 1 - cc)]
                copies.append(pltpu.make_async_remote_copy(
                    src_ref=rows, dst_ref=rows, send_sem=send_sems.at[3 * a + j], recv_sem=recv_sems.at[3 * a + j],
                    device_id=(x, y, 1 - cc), device_id_type=MESH))
        return copies

    plan.n_sems = 3 * n
    return plan


def _plan_to_sibling(n):
    def plan(refs, send_sems, recv_sems, outgoing):
        x, y, cc = _mesh_pos()
        copies = []
        for a in range(n):
            for chip in range(4):
                dst = refs[n + a].at[chip]
                copies.append(pltpu.make_async_remote_copy(
                    src_ref=refs[a].at[2 * chip + (1 - cc)] if outgoing else dst, dst_ref=dst,
                    send_sem=send_sems.at[4 * a + chip], recv_sem=recv_sems.at[4 * a + chip],
                    device_id=(x, y, 1 - cc), device_id_type=MESH))
        return copies

    plan.n_sems = 4 * n
    return plan


def _plan_to_chips(n):
    def plan(refs, send_sems, recv_sems, outgoing):
        x, y, cc = _mesh_pos()
        copies = []
        for a in range(n):
            for j, (px, py) in enumerate([(1 - x, y), (x, 1 - y), (1 - x, 1 - y)]):
                dst = refs[n + a].at[j]
                copies.append(pltpu.make_async_remote_copy(
                    src_ref=refs[a].at[2 * px + py] if outgoing else dst, dst_ref=dst,
                    send_sem=send_sems.at[3 * a + j], recv_sem=recv_sems.at[3 * a + j],
                    device_id=(px, py, cc), device_id_type=MESH))
        return copies

    plan.n_sems = 3 * n
    return plan


def _in_hbm(t):
    return pltpu.with_memory_space_constraint(t, pltpu.HBM)


def _exchange_start(plan, bufs, after, *, name):
    nb, na = len(bufs), len(after)

    def body(*refs):
        outs = refs[nb + na:]
        for cp in plan(refs[:nb], outs[0], outs[1], True):
            cp.start()
        outs[2 + nb][...] = jnp.zeros_like(outs[2 + nb])

    res = pl.pallas_call(
        body,
        out_shape=(pltpu.SemaphoreType.DMA((plan.n_sems,)), pltpu.SemaphoreType.DMA((plan.n_sems,)),
                   *[pltpu.HBM(t.shape, t.dtype) for t in bufs], jax.ShapeDtypeStruct((8, LANES), F32)),
        in_specs=[HBM_SPEC] * nb + [ANY] * na,
        out_specs=(SEM_SPEC, SEM_SPEC, *[HBM_SPEC] * nb, pl.BlockSpec(memory_space=pltpu.VMEM)),
        input_output_aliases={i: 2 + i for i in range(nb)},
        compiler_params=pltpu.CompilerParams(has_side_effects=DATAFLOW_EFFECT),
        name=name,
    )(*[_in_hbm(t) for t in bufs], *after)
    return plan, res[:2], list(res[2:2 + nb]), res[2 + nb]


def _exchange_wait(flight, after, *, name):
    plan, sems, bufs, _ = flight
    nb = len(bufs)

    def body(*refs):
        send_sems, recv_sems = refs[nb], refs[nb + 1]
        for cp in plan(refs[:nb], send_sems, recv_sems, False):
            cp.wait_recv()
        for cp in plan(refs[:nb], send_sems, recv_sems, True):
            cp.wait_send()

    res = pl.pallas_call(
        body,
        out_shape=tuple(pltpu.HBM(t.shape, t.dtype) for t in bufs),
        in_specs=[HBM_SPEC] * nb + [SEM_SPEC, SEM_SPEC] + [ANY] * len(after),
        out_specs=tuple([HBM_SPEC] * nb),
        input_output_aliases={i: i for i in range(nb)},
        compiler_params=pltpu.CompilerParams(has_side_effects=DATAFLOW_EFFECT),
        name=name,
    )(*bufs, *sems, *after)
    return list(res)


_LANE_PADDED = ("even_w_in", "odd_w_in")
_W_GROUPS = {"even": ("even_w_in", "even_w_out", "norm_gains"), "mlp0": ("mlp_w1_0", "mlp_w2_0"),
             "odd": ("odd_w_in", "odd_w_uq", "odd_w_ukv", "odd_w_out"), "mlp1": ("mlp_w1_1", "mlp_w2_1")}


class _Exchanges:
    def __init__(self, raw_shards, n_q, n_kv):
        self.n_q, self.n_kv = n_q, n_kv
        self.mx, self.my, self.mc = _mesh_pos()
        self.dev = 4 * self.mx + 2 * self.my + self.mc
        self.raw = raw_shards
        self.flights, self.gathered, self.grad_blocks, self.grad_names, self.reduced = {}, {}, {}, {}, {}
        self.ready = {}

    def prepare(self, group, behind=()):
        raw = [self.raw[n] for n in _W_GROUPS[group]]
        if behind:
            raw, _ = lax.optimization_barrier((raw, list(behind)))
        srcs = [t if n == "norm_gains" else
                jnp.pad(t, ((0, 0), (0, _lane_pad(t.shape[1]) - t.shape[1]))).astype(BF16) if n in _LANE_PADDED else
                t.astype(BF16) for n, t in zip(_W_GROUPS[group], raw)]
        lands = [lax.dynamic_update_slice(lax.empty((N_DEV,) + t.shape, t.dtype), t[None], (self.dev, 0, 0)) for t in srcs]
        self.ready[group] = srcs + lands
        return lands

    def start(self):
        return self._w_begin("even", [])

    def _w_begin(self, group, after):
        if group not in self.ready:
            self.prepare(group)
        bufs = self.ready.pop(group)
        self.flights[group] = _exchange_start(_plan_own_blocks(len(bufs) // 2), bufs, after, name=f"comm_{group}_own_start")
        return [self.flights[group][3]]

    def _w_turn(self, group, after):
        bufs = _exchange_wait(self.flights[group], after, name=f"comm_{group}_own_wait")
        n = len(bufs) // 2
        self.flights[group] = _exchange_start(_plan_pass_on(n), bufs[n:], [], name=f"comm_{group}_pass_start")
        return [self.flights[group][3]]

    def _w_end(self, group, after):
        self.gathered.update(zip(_W_GROUPS[group], _exchange_wait(self.flights.pop(group), after, name=f"comm_{group}_pass_wait")))

    def weights(self, group):
        return {n: self.gathered[n] for n in _W_GROUPS[group]}

    def norm_gains(self):
        gains = self.gathered["norm_gains"][:, 0]
        return gains[:, :self.n_q].reshape(-1), gains[:, self.n_q:self.n_q + self.n_kv].reshape(-1)

    def grads(self, group, named_blocks):
        self.grad_names[group] = [n for n, _ in named_blocks]
        self.grad_blocks[group] = [t for _, t in named_blocks]

    def _g_begin(self, group, after):
        blocks = self.grad_blocks[group]
        lands = [lax.empty((4,) + t.shape[1:], t.dtype) for t in blocks]
        self.flights[group] = _exchange_start(_plan_to_sibling(len(blocks)), blocks + lands, after, name=f"comm_{group}_sib_start")
        return [self.flights[group][3]]

    def _g_turn(self, group, after):
        bufs = _exchange_wait(self.flights[group], after, name=f"comm_{group}_sib_wait")
        n = len(bufs) // 2
        partial = [_pair_sum(a, b, self.mc, name=f"pair_sum_{nm}") for nm, a, b in zip(self.grad_names[group], bufs[:n], bufs[n:])]
        lands = [lax.empty((3,) + t.shape[1:], t.dtype) for t in partial]
        self.flights[group] = _exchange_start(_plan_to_chips(n), partial + lands, [], name=f"comm_{group}_chips_start")
        return [self.flights[group][3]]

    def _g_end(self, group, after):
        bufs = _exchange_wait(self.flights.pop(group), after, name=f"comm_{group}_chips_wait")
        n = len(bufs) // 2
        self.reduced.update(zip(self.grad_names[group], zip(bufs[:n], bufs[n:])))

    _SCHEDULE = {
        "x_cast": (("w_turn", "even"), ("w_begin", "mlp0"), ("w_end", "even")),
        "even_out": (("w_turn", "mlp0"), ("w_begin", "odd")),
        "ln1_l0": (("w_end", "mlp0"),),
        "ln2_l0": (("w_turn", "odd"), ("w_begin", "mlp1"), ("w_end", "odd")),
        "odd_out": (("w_turn", "mlp1"),),
        "ln1_l1": (("w_end", "mlp1"),),
        "dw_l1": (("g_begin", "mlp1"),),
        "ln1_bwd_l1": (("g_turn", "mlp1"),),
        "odd_in_dw": (("g_begin", "odd"),),
        "mlp2_dx_l0": (("g_end", "mlp1"), ("g_turn", "odd")),
        "dw_l0": (("g_begin", "mlp0"),),
        "ln1_bwd_l0": (("g_end", "odd"), ("g_turn", "mlp0")),
        "even_in_dw": (("g_begin", "even"),),
        "even_in_dx": (("g_end", "mlp0"), ("g_turn", "even")),
        "finish": (("g_end", "even"),),
    }

    def sync(self, tag, after):
        latest, started = list(after), []
        for what, group in self._SCHEDULE[tag]:
            out = getattr(self, "_" + what)(group, latest)
            if out:
                latest = started = out
        return started


def _alibi(n):
    return 2.0 ** (-8.0 * np.arange(1, n + 1, dtype=np.float32) / n)


def _heads(t, n):
    s = t.shape[0]
    return t.reshape(s, n, t.shape[1] // n).transpose(1, 0, 2)


def _unheads(t):
    n, s, dh = t.shape
    return t.transpose(1, 0, 2).reshape(s, n * dh)


def _to_strided(t, d):
    s, x = t.shape
    return t if d == 1 else t.reshape(s // d, d, x).transpose(1, 0, 2).reshape(s, x)


def _from_strided(t, d):
    s, x = t.shape
    return t if d == 1 else t.reshape(d, s // d, x).transpose(1, 0, 2).reshape(s, x)


def _true_columns(t, c, n_pad):
    tail = [jnp.zeros((t.shape[1], n_pad - N_DEV * c), t.dtype)] if n_pad > N_DEV * c else []
    return jnp.concatenate([t[dev, :, :c] for dev in range(N_DEV)] + tail, axis=1)


def _column_blocks(t, c, cp):
    return jnp.stack([jnp.pad(t[:, dev * c:(dev + 1) * c], ((0, 0), (0, cp - c))) for dev in range(N_DEV)])


def _stack_rows(t, nq):
    return t.reshape(nq, BLK, A_KV_HEADS, A_GROUP, HEAD_DIM).transpose(2, 0, 3, 1, 4).reshape(
        A_KV_HEADS, 1, nq * A_GROUP * BLK, HEAD_DIM)


def _unstack_rows(t, nq):
    return t.reshape(A_KV_HEADS, nq, A_GROUP, BLK, HEAD_DIM).transpose(1, 3, 0, 2, 4).reshape(
        nq * BLK, A_Q_W)


def _lead_block(t):
    return jnp.pad(t, ((0, 0), (BLK, 0), (0, 0)))


def _columns(t):
    return t.transpose(1, 0, 2).reshape(t.shape[1], -1)


def _rows(t):
    return t.reshape(-1, t.shape[2])


def _by_column_block(t):
    return t.reshape(t.shape[0], N_DEV, -1).transpose(1, 0, 2)


def _local_step(x0, target, comm, sinks, ln1_g, ln1_b, ln2_g, ln2_b):
    s, d = x0.shape
    scale_h = 1.0 / math.sqrt(HEAD_DIM)
    scale_d = 1.0 / math.sqrt(D_NOPE + D_ROPE)
    nq = s // BLK
    even_c, odd_c = EVEN_IN // N_DEV, ODD_IN // N_DEV
    bf = lambda t: t.astype(MXU_DTYPE)
    blocks = dict(b_blocks=True)

    tok = comm.start()
    x0b = bf(lax.optimization_barrier((x0, tok))[0])
    later = [t for group in ("mlp0", "odd", "mlp1") for t in comm.prepare(group, behind=tok)]
    tok = comm.sync("x_cast", [x0b] + later)
    w_even = comm.weights("even")
    even_cp, even_n = w_even["even_w_in"].shape[2], -(-EVEN_IN // 1024) * 1024
    w_even_in, w_even_out = _true_columns(w_even["even_w_in"], even_c, even_n), _columns(w_even["even_w_out"])
    h_e = _mm(x0b, w_even_in, out_dtypes=(MXU_DTYPE,), name="even_in_fwd", deps=tok)
    qa = _stack_rows(h_e[:, :A_Q_W], nq)
    ka = _lead_block(_heads(h_e[:, A_Q_W:A_Q_W + A_KV_W], A_KV_HEADS))[:, None]
    va = _lead_block(_heads(h_e[:, A_Q_W + A_KV_W:A_Q_W + 2 * A_KV_W], A_KV_HEADS))[:, None]
    rows_a = A_GROUP * BLK
    slope_a = jnp.asarray(np.repeat(_alibi(A_Q_HEADS).reshape(A_KV_HEADS, A_GROUP), BLK, axis=1).reshape(
        A_KV_HEADS, 1, rows_a, 1))
    sink_a = jnp.broadcast_to(sinks.reshape(A_KV_HEADS, A_GROUP, 1), (A_KV_HEADS, A_GROUP, BLK)).reshape(
        A_KV_HEADS, 1, rows_a, 1)
    a_cfg = dict(scale=scale_h, n_back=A_WINDOW - 1, bps=nq)
    oa, lse_a = _band_fwd(qa, ka, va, slope_a, sink_a, name="swa_fwd", **a_cfg)
    b_slab, b_cfg, b_out, b_lse = [], [], [], []
    base = A_Q_W + 2 * A_KV_W
    for gi, (window, dil) in enumerate(B_PATTERNS):
        slab = _to_strided(h_e[:, base + gi * 3 * B_W: base + (gi + 1) * 3 * B_W], dil)
        cfg = dict(slopes=_alibi(B_HEADS) * dil, scale=scale_h, n_back=window // dil, bps=nq // dil)
        o, lse = _dil_fwd(slab, name=f"dil{gi}_fwd", **cfg)
        both = _from_strided(jnp.concatenate([o, lse], axis=1), dil)
        b_slab.append(slab)
        b_cfg.append(cfg)
        b_out.append(both[:, :B_W])
        b_lse.append(both[:, B_W:])
    ob, lse_b = _merge(b_out, b_lse, name="dil_merge")
    y_e = bf(jnp.concatenate([_unstack_rows(oa, nq), ob], axis=1))
    mixed = _mm(y_e, w_even_out, name="even_out_fwd")
    tok = comm.sync("even_out", [mixed])
    x0n, x0nb, xh1_0, r1_0 = _ln_fwd(x0, mixed, ln1_g[0], ln1_b[0], name="ln1_fwd_l0", deps=tok)
    comm.sync("ln1_l0", [x0nb])
    w_mlp0 = comm.weights("mlp0")
    w1_0, w2_0 = w_mlp0["mlp_w1_0"], _rows(w_mlp0["mlp_w2_0"])
    act0, hid0 = _mm(x0nb, w1_0, out_dtypes=(MXU_DTYPE, MXU_DTYPE), epilogue=_relu_sq, name="mlp1_fwd_l0", **blocks)
    mlp = _mm(hid0, w2_0, name="mlp2_fwd_l0")
    x1, x1b, xh2_0, r2_0 = _ln_fwd(x0n, mlp, ln2_g[0], ln2_b[0], name="ln2_fwd_l0")

    tok = comm.sync("ln2_l0", [x1b])
    w_odd = comm.weights("odd")
    odd_cp, odd_n = w_odd["odd_w_in"].shape[2], -(-ODD_IN // 1024) * 1024
    w_odd_in, w_uq, w_ukv, w_odd_out = (_true_columns(w_odd["odd_w_in"], odd_c, odd_n), _columns(w_odd["odd_w_uq"]),
                                        _columns(w_odd["odd_w_ukv"]), _rows(w_odd["odd_w_out"]))
    gq, gkv = comm.norm_gains()
    h_o = _mm(x1b, w_odd_in, name="odd_in_fwd", deps=tok)
    qkv_c = bf(h_o[:, :3 * C_W])
    oc, sb_total = _sb_fwd(qkv_c, heads=C_HEADS, scale=scale_h, name="sb_fwd")
    o_cq, o_ckv, o_kr = 3 * C_W, 3 * C_W + D_Q_RANK, 3 * C_W + D_Q_RANK + D_KV_RANK
    cq, ckv, kr = h_o[:, o_cq:o_ckv], h_o[:, o_ckv:o_kr], h_o[:, o_kr:o_kr + D_ROPE]
    ncq, rq = _rms_fwd(cq, gq, name="rms_q_fwd")
    nckv, rkv = _rms_fwd(ckv, gkv, name="rms_kv_fwd")
    lane_pad = LANES - D_NOPE - D_ROPE
    w_uq = jnp.pad(w_uq.reshape(D_Q_RANK, D_HEADS, D_NOPE + D_ROPE), ((0, 0), (0, 0), (0, lane_pad))).reshape(
        D_Q_RANK, D_HEADS * LANES)
    qd = _mm(ncq, w_uq, name="uq_fwd")
    kvd = _mm(nckv, w_ukv, out_dtypes=(MXU_DTYPE,), name="ukv_fwd")
    rope_t = _rope_tables(s, inverse=False)
    krp = _rope(jnp.pad(kr, ((0, 0), (D_NOPE, lane_pad)))[None], rope_t, out_dtype=MXU_DTYPE, name="rope_k_fwd")[0]
    od, lse_d = _mla_fwd(qd, kvd, krp, rope_t, scale=scale_d, name="mla_fwd")
    y_o = bf(jnp.concatenate([oc, od], axis=1))
    mixed = _mm(y_o, w_odd_out, name="odd_out_fwd")
    tok = comm.sync("odd_out", [mixed])
    x1n, x1nb, xh1_1, r1_1 = _ln_fwd(x1, mixed, ln1_g[1], ln1_b[1], name="ln1_fwd_l1", deps=tok)
    comm.sync("ln1_l1", [x1nb])
    w_mlp1 = comm.weights("mlp1")
    w1_1, w2_1 = w_mlp1["mlp_w1_1"], _rows(w_mlp1["mlp_w2_1"])
    act1, hid1 = _mm(x1nb, w1_1, out_dtypes=(MXU_DTYPE, MXU_DTYPE), epilogue=_relu_sq, name="mlp1_fwd_l1", **blocks)
    mlp = _mm(hid1, w2_1, name="mlp2_fwd_l1")
    y, _, xh2_1, r2_1 = _ln_fwd(x1n, mlp, ln2_g[1], ln2_b[1], name="ln2_fwd_l1")
    dy, loss_vec = _loss_head(y, target, name="loss_head")

    def mlp_block_bwd(g_out, layer, w1, w2, xh2, r2, xh1, r1, act, hid, xnb):
        du2, du2b, dg2, db2 = _ln_bwd(g_out, xh2, r2, ln2_g[layer], name=f"ln2_bwd_l{layer}")
        dpre = _mm(du2b, w2, nt=True, out_dtypes=(MXU_DTYPE,), epilogue=_relu_sq_grad, extra=act, name=f"mlp2_dx_l{layer}")
        tok = comm.sync("mlp2_dx_l0", [dpre]) if layer == 0 else []
        dw2 = _mm(hid, du2b, ta=True, out_dtypes=(BF16,), name=f"mlp2_dw_l{layer}", deps=tok)
        dw1 = _mm(xnb, dpre, ta=True, out_blocks=True, out_dtypes=(BF16,), name=f"mlp1_dw_l{layer}")
        comm.grads(f"mlp{layer}", [(f"mlp_w1_{layer}", dw1), (f"mlp_w2_{layer}", dw2.reshape(N_DEV, -1, d))])
        tok = comm.sync(f"dw_l{layer}", [dw1])
        dxn = _mm(dpre, w1, nt=True, epilogue=_add_alpha, extra=du2, name=f"mlp1_dx_l{layer}", deps=tok, **blocks)
        du1, du1b, dg1, db1 = _ln_bwd(dxn, xh1, r1, ln1_g[layer], name=f"ln1_bwd_l{layer}")
        return du1, du1b, comm.sync(f"ln1_bwd_l{layer}", [du1b]), (dg1, db1, dg2, db2)

    du1, du1b, tok, ln_1 = mlp_block_bwd(dy, 1, w1_1, w2_1, xh2_1, r2_1, xh1_1, r1_1, act1, hid1, x1nb)
    d_odd_out = _mm(y_o, du1b, ta=True, out_dtypes=(BF16,), name="odd_out_dw").reshape(N_DEV, -1, d)
    dy_o = _mm(du1b, w_odd_out, nt=True, name="odd_out_dx", deps=tok)
    dqc, dkc, dvc = _sb_bwd(qkv_c, dy_o, sb_total, heads=C_HEADS, scale=scale_h, name="sb_bwd")
    dqd, dkvd, dkr_pairs = _mla_bwd(qd, kvd, krp, rope_t, dy_o, od, lse_d, do_block0=C_W // LANES, scale=scale_d,
                                    name="mla_bwd")
    _, dkr_sum = _rope(dkr_pairs, _rope_tables(s, inverse=True), out_dtype=F32, head_sum=True, name="rope_k_bwd")
    dqd, dkvd = bf(dqd), bf(dkvd)
    d_uq = _mm(ncq, dqd, ta=True, out_dtypes=(BF16,), name="uq_dw").reshape(D_Q_RANK, D_HEADS, LANES)[:, :, :D_NOPE + D_ROPE].reshape(
        D_Q_RANK, D_HEADS * (D_NOPE + D_ROPE))
    dncq = _mm(dqd, w_uq, nt=True, name="uq_dx")
    d_ukv = _mm(nckv, dkvd, ta=True, out_dtypes=(BF16,), name="ukv_dw")
    dnckv = _mm(dkvd, w_ukv, nt=True, name="ukv_dx")
    dcq, dgq = _rms_bwd(dncq, cq, rq, gq, name="rms_q_bwd")
    dckv, dgkv = _rms_bwd(dnckv, ckv, rkv, gkv, name="rms_kv_bwd")
    dh_o = bf(jnp.concatenate(
        [dqc, dkc, dvc, dcq, dckv, dkr_sum[:, D_NOPE:D_NOPE + D_ROPE], jnp.zeros((s, odd_n - ODD_IN), F32)], axis=1))
    d_odd_in = _column_blocks(_mm(x1b, dh_o, ta=True, out_dtypes=(BF16,), name="odd_in_dw"), odd_c, odd_cp)
    comm.grads("odd", [("odd_w_in", d_odd_in), ("odd_w_uq", _by_column_block(d_uq)),
                       ("odd_w_ukv", _by_column_block(d_ukv)), ("odd_w_out", d_odd_out)])
    tok = comm.sync("odd_in_dw", [d_odd_in])
    dx1 = _mm(dh_o, w_odd_in, nt=True, epilogue=_add_alpha, extra=du1, name="odd_in_dx", deps=tok)

    du1, du1b, tok, ln_0 = mlp_block_bwd(dx1, 0, w1_0, w2_0, xh2_0, r2_0, xh1_0, r1_0, act0, hid0, x0nb)
    d_even_out = _mm(y_e, du1b, ta=True, out_dtypes=(BF16,), name="even_out_dw")
    dy_e = _mm(du1b, w_even_out, nt=True, name="even_out_dx", deps=tok)
    doa = _stack_rows(dy_e[:, :A_Q_W], nq)
    dqa, dka, dva, dsink = _band_bwd(qa, ka, va, doa, oa, lse_a, slope_a, sink_a, name="swa_bwd", **a_cfg)
    pieces = [_unstack_rows(dqa, nq), _unheads(dka[:, 0, BLK:]), _unheads(dva[:, 0, BLK:])]
    pack = jnp.concatenate([dy_e[:, A_Q_W:], ob, lse_b], axis=1)
    for gi, (_, dil) in enumerate(B_PATTERNS):
        grads = _dil_bwd(b_slab[gi], _to_strided(pack, dil), name=f"dil{gi}_bwd", **b_cfg[gi])
        pieces.append(_from_strided(jnp.concatenate(grads, axis=1), dil))
    dh_e = bf(jnp.concatenate(pieces + [jnp.zeros((s, even_n - EVEN_IN), F32)], axis=1))
    d_even_in = _column_blocks(_mm(x0b, dh_e, ta=True, out_dtypes=(BF16,), name="even_in_dw"), even_c, even_cp)
    comm.grads("even", [("even_w_in", d_even_in), ("even_w_out", _by_column_block(d_even_out))])
    tok = comm.sync("even_in_dw", [d_even_in])
    grad_x = _mm(dh_e, w_even_in, nt=True, epilogue=_add_alpha, extra=du1, name="even_in_dx", deps=tok)
    tok = comm.sync("even_in_dx", [grad_x])

    ln = [jnp.concatenate([a, b], axis=0) for a, b in zip(ln_0, ln_1)]
    small = {"ln": ln, "sinks": dsink[:, :, 0].reshape(-1), "gq": dgq[0], "gkv": dgkv[0], "loss": loss_vec[0, :1]}
    return grad_x, small, tok


def kernel(x, even_w_in, even_sinks, even_w_out, odd_w_in, odd_q_norm_g, odd_kv_norm_g, odd_w_uq, odd_w_ukv, odd_w_out, ln1_g, ln1_b, mlp_w1, mlp_w2, ln2_g, ln2_b, loss_target, m_even_w_in, m_even_sinks, m_even_w_out, m_odd_w_in, m_odd_q_norm_g, m_odd_kv_norm_g, m_odd_w_uq, m_odd_w_ukv, m_odd_w_out, m_ln1_g, m_ln1_b, m_mlp_w1, m_mlp_w2, m_ln2_g, m_ln2_b, v_even_w_in, v_even_sinks, v_even_w_out, v_odd_w_in, v_odd_q_norm_g, v_odd_kv_norm_g, v_odd_w_uq, v_odd_w_ukv, v_odd_w_out, v_ln1_g, v_ln1_b, v_mlp_w1, v_mlp_w2, v_ln2_g, v_ln2_b):
    weights = dict(even_w_in=even_w_in, even_sinks=even_sinks, even_w_out=even_w_out, odd_w_in=odd_w_in,
                   odd_q_norm_g=odd_q_norm_g, odd_kv_norm_g=odd_kv_norm_g, odd_w_uq=odd_w_uq, odd_w_ukv=odd_w_ukv,
                   odd_w_out=odd_w_out, ln1_g=ln1_g, ln1_b=ln1_b, mlp_w1=mlp_w1, mlp_w2=mlp_w2, ln2_g=ln2_g, ln2_b=ln2_b)
    mom_m = dict(even_w_in=m_even_w_in, even_sinks=m_even_sinks, even_w_out=m_even_w_out, odd_w_in=m_odd_w_in,
                 odd_q_norm_g=m_odd_q_norm_g, odd_kv_norm_g=m_odd_kv_norm_g, odd_w_uq=m_odd_w_uq, odd_w_ukv=m_odd_w_ukv,
                 odd_w_out=m_odd_w_out, ln1_g=m_ln1_g, ln1_b=m_ln1_b, mlp_w1=m_mlp_w1, mlp_w2=m_mlp_w2, ln2_g=m_ln2_g, ln2_b=m_ln2_b)
    mom_v = dict(even_w_in=v_even_w_in, even_sinks=v_even_sinks, even_w_out=v_even_w_out, odd_w_in=v_odd_w_in,
                 odd_q_norm_g=v_odd_q_norm_g, odd_kv_norm_g=v_odd_kv_norm_g, odd_w_uq=v_odd_w_uq, odd_w_ukv=v_odd_w_ukv,
                 odd_w_out=v_odd_w_out, ln1_g=v_ln1_g, ln1_b=v_ln1_b, mlp_w1=v_mlp_w1, mlp_w2=v_mlp_w2, ln2_g=v_ln2_g, ln2_b=v_ln2_b)
    order = list(weights)
    n_q, n_kv = odd_q_norm_g.shape[1], odd_kv_norm_g.shape[1]

    gains = jnp.concatenate([odd_q_norm_g, odd_kv_norm_g, jnp.zeros((1, LANES - n_q - n_kv), F32)], axis=1)
    raw = {"even_w_in": even_w_in[0], "even_w_out": even_w_out[0], "mlp_w1_0": mlp_w1[0], "mlp_w2_0": mlp_w2[0],
           "odd_w_in": odd_w_in[0], "odd_w_uq": odd_w_uq[0], "odd_w_ukv": odd_w_ukv[0], "odd_w_out": odd_w_out[0],
           "mlp_w1_1": mlp_w1[1], "mlp_w2_1": mlp_w2[1], "norm_gains": gains}
    comm = _Exchanges(raw, n_q, n_kv)
    dev = comm.dev

    grad_x, small, last_started = _local_step(x[0], loss_target[0], comm, even_sinks[0], ln1_g, ln1_b, ln2_g, ln2_b)

    grads, delta, new_m, new_v = {}, {}, {}, {}

    def update(n):
        g_list = [comm.reduced[f"{n}_0"], comm.reduced[f"{n}_1"]] if n.startswith("mlp") else [comm.reduced[n]]
        grads[n], delta[n], new_m[n], new_v[n] = _adamw(weights[n], g_list, mom_m[n], mom_v[n], name=f"adamw_{n}",
                                                        my_chip=2 * comm.mx + comm.my, deps=last_started)

    early = ("mlp_w1", "mlp_w2", "odd_w_in", "odd_w_uq", "odd_w_ukv", "odd_w_out")
    for n in early:
        update(n)
    comm.sync("finish", [new_v[n] for n in early])
    update("even_w_in")
    update("even_w_out")

    small_parts = [t.reshape(-1) for t in small["ln"]] + [small["sinks"], small["gq"], small["gkv"], small["loss"]]
    small_sizes = [p.shape[0] for p in small_parts]
    n_small = sum(small_sizes)
    small_rows = -(-n_small // (8 * LANES)) * 8
    small_flat = jnp.concatenate(small_parts + [jnp.zeros((small_rows * LANES - n_small,), F32)]).reshape(small_rows, LANES)
    (small_all,) = _all_gather([small_flat], name="comm_small_gather", deps=[grads["even_w_in"]])
    totals = _sum_devices(small_all, name="small_sum").reshape(-1)
    tot, off = [], 0
    for size in small_sizes:
        tot.append(totals[off:off + size])
        off += size
    for i, n in enumerate(("ln1_g", "ln1_b", "ln2_g", "ln2_b")):
        grads[n] = tot[i].reshape(weights[n].shape)
    grads["even_sinks"] = tot[4].reshape(even_sinks.shape)
    grads["odd_q_norm_g"] = lax.dynamic_slice(tot[5], (dev * n_q,), (n_q,)).reshape(odd_q_norm_g.shape)
    grads["odd_kv_norm_g"] = lax.dynamic_slice(tot[6], (dev * n_kv,), (n_kv,)).reshape(odd_kv_norm_g.shape)
    loss = tot[7][0]

    small_names = [n for n in order if n not in early + ("even_w_in", "even_w_out")]
    n_sm = sum(weights[n].size for n in small_names)
    sm_rows = -(-n_sm // (8 * LANES)) * 8

    def pack_small(group):
        flat = [group[n].reshape(-1) for n in small_names]
        return jnp.concatenate(flat + [jnp.zeros((sm_rows * LANES - n_sm,), F32)]).reshape(1, sm_rows, LANES)

    res = _adamw(pack_small(weights), [pack_small(grads)[0]], pack_small(mom_m), pack_small(mom_v), name="adamw_small")
    off = 0
    for n in small_names:
        size = weights[n].size
        delta[n], new_m[n], new_v[n] = (t.reshape(-1)[off:off + size].reshape(weights[n].shape) for t in res[1:])
        off += size

    return (loss, grad_x[None], *[grads[n] for n in order], *[delta[n] for n in order],
            *[new_m[n] for n in order], *[new_v[n] for n in order])
```

```python
import math

import jax
import jax.numpy as jnp
import numpy as np
from jax import lax
from jax.experimental import pallas as pl
from jax.experimental.pallas import tpu as pltpu

F32 = jnp.float32
BF16 = jnp.bfloat16
MXU_DTYPE = BF16

HEAD_DIM = 64
A_Q_HEADS, A_KV_HEADS, A_WINDOW = 16, 2, 128
A_GROUP = A_Q_HEADS // A_KV_HEADS
B_HEADS = 8
B_PATTERNS = ((128, 1), (512, 4), (2048, 16))
C_HEADS = 16
D_HEADS, D_Q_RANK, D_KV_RANK, D_NOPE, D_ROPE, D_V = 16, 512, 256, 64, 32, 64
ROPE_BASE = 10000.0
LN_EPS, RMS_EPS = 1e-5, 1e-6
DEPTH = 2
ALPHA = (2 * DEPTH) ** 0.25
A_Q_W, A_KV_W, B_W = A_Q_HEADS * HEAD_DIM, A_KV_HEADS * HEAD_DIM, B_HEADS * HEAD_DIM
EVEN_IN = A_Q_W + 2 * A_KV_W + 3 * B_W * len(B_PATTERNS)
C_W = C_HEADS * HEAD_DIM
ODD_IN = 3 * C_W + D_Q_RANK + D_KV_RANK + D_ROPE
ADAM_LR, ADAM_B1, ADAM_B2, ADAM_EPS, ADAM_WD, ADAM_STEP = 0.001, 0.9, 0.999, 1e-08, 0.01, 10

N_DEV = 8
LANES = 128
BLK = 128
CAUSAL_TILE = 512
CUM_CHUNK = 256
NEG = -1e30
VMEM_LIMIT = 48 * 1024 * 1024

NN = ((1,), (0,))
NT = ((1,), (1,))
TN = ((0,), (0,))
MESH = pl.DeviceIdType.MESH
ANY = pl.BlockSpec(memory_space=pl.ANY)
HBM_SPEC = pl.BlockSpec(memory_space=pltpu.HBM)
SEM_SPEC = pl.BlockSpec(memory_space=pltpu.SEMAPHORE)
DATAFLOW_EFFECT = pltpu.SideEffectType.DATAFLOW_SIDE_EFFECTING


def _dot(a, b, dims):
    return lax.dot_general(a, b, (dims, ((), ())), preferred_element_type=F32)


def _bdot(a, b, dims):
    return jnp.stack([_dot(a[n], b[n], dims) for n in range(a.shape[0])])


def _params(*sem):
    return pltpu.CompilerParams(dimension_semantics=tuple(sem), vmem_limit_bytes=VMEM_LIMIT)


def _pick(n, cap, mult=LANES):
    if n <= cap:
        return n
    for t in range(cap - cap % mult, 0, -mult):
        if n % t == 0:
            return t
    raise ValueError(f"no tile for {n}")


def _lane_pad(c):
    return -(-c // LANES) * LANES


def _mm(a, b, *, name, nt=False, ta=False, b_blocks=False, out_blocks=False, out_dtypes=(F32,), epilogue=None, extra=None, deps=()):
    m, k = a.shape[::-1] if ta else a.shape
    if b_blocks:
        nb, kin, c = b.shape
        n = kin if nt else nb * c
        k_full = nb * c if nt else kin
    else:
        n, k_full = (b.shape if nt else b.shape[::-1])
    assert k == k_full, (a.shape, b.shape, nt, b_blocks)
    tm = _pick(m, 1024, 8)
    if b_blocks and not nt:
        tn, tk = c, _pick(k, 3072)
    elif b_blocks:
        per_step = max(g for g in (1, 2, 4, 8) if g * c <= 2048)
        tn, tk = _pick(n, 1024), per_step * c
    elif out_blocks:
        tn, tk = n // N_DEV, _pick(k, 3072)
    else:
        tn, tk = _pick(n, 512), _pick(k, 3072)
        if k > tk:
            tn, tk = _pick(n, 1024), _pick(k, 2048)
    nk = k // tk
    n_out = len(out_dtypes)

    def body(*refs):
        a_ref, b_ref = refs[0], refs[1]
        e_ref = refs[2] if extra is not None else None
        first_out = 2 + (extra is not None) + len(deps)
        out_refs = refs[first_out:first_out + n_out]

        def finish(acc):
            e = None if e_ref is None else e_ref[...]
            outs = (acc,) if epilogue is None else epilogue(acc, e)
            for r, o in zip(out_refs, outs):
                r[...] = o.astype(r.dtype).reshape(r.shape)

        if b_blocks and nt:
            part = _dot(a_ref[:, :c], b_ref[0], NT)
            for blk in range(1, per_step):
                part += _dot(a_ref[:, blk * c:(blk + 1) * c], b_ref[blk], NT)
        elif ta:
            part = _dot(a_ref[...], b_ref[...], TN)
        else:
            part = _dot(a_ref[...], b_ref[0] if b_blocks else b_ref[...], NT if nt else NN)
        if nk == 1:
            finish(part)
        else:
            acc_ref = refs[first_out + n_out]
            kk = pl.program_id(2)

            @pl.when(kk == 0)
            def _():
                acc_ref[...] = part

            @pl.when(kk > 0)
            def _():
                acc_ref[...] += part

            @pl.when(kk == nk - 1)
            def _():
                finish(acc_ref[...])

    if b_blocks and not nt:
        b_spec = pl.BlockSpec((1, tk, tn), lambda i, j, kk: (j, kk, 0))
    elif b_blocks:
        b_spec = pl.BlockSpec((per_step, tn, c), lambda i, j, kk: (kk, j, 0))
    elif nt:
        b_spec = pl.BlockSpec((tn, tk), lambda i, j, kk: (j, kk))
    else:
        b_spec = pl.BlockSpec((tk, tn), lambda i, j, kk: (kk, j))
    a_spec = pl.BlockSpec((tk, tm), lambda i, j, kk: (kk, i)) if ta else pl.BlockSpec((tm, tk), lambda i, j, kk: (i, kk))
    in_specs = [a_spec, b_spec]
    ins = [a.astype(MXU_DTYPE), b.astype(MXU_DTYPE)]
    if extra is not None:
        in_specs.append(pl.BlockSpec((tm, tn), lambda i, j, kk: (i, j)))
        ins.append(extra)
    in_specs += [ANY] * len(deps)
    ins += list(deps)
    if out_blocks:
        out_shape = tuple(jax.ShapeDtypeStruct((N_DEV, m, tn), d) for d in out_dtypes)
        out_specs = tuple(pl.BlockSpec((1, tm, tn), lambda i, j, kk: (j, i, 0)) for _ in out_dtypes)
    else:
        out_shape = tuple(jax.ShapeDtypeStruct((m, n), d) for d in out_dtypes)
        out_specs = tuple(pl.BlockSpec((tm, tn), lambda i, j, kk: (i, j)) for _ in out_dtypes)
    outs = pl.pallas_call(
        body,
        out_shape=out_shape,
        grid=(m // tm, n // tn, nk),
        in_specs=in_specs,
        out_specs=out_specs,
        scratch_shapes=[pltpu.VMEM((tm, tn), F32)] if nk > 1 else [],
        compiler_params=_params("parallel", "parallel", "arbitrary"),
        name=name,
    )(*ins)
    return outs[0] if n_out == 1 else outs


def _relu_sq(acc, _):
    act = jnp.maximum(acc, 0.0)
    return act, act * act


def _relu_sq_grad(acc, act):
    return (acc * (2.0 * act.astype(F32)),)


def _add_alpha(acc, du):
    return (acc + ALPHA * du,)


def _ln_fwd(x, mixed, g, b, *, name, deps=()):
    s, d = x.shape
    tr = _pick(s, 256, 8)

    def body(x_ref, m_ref, g_ref, b_ref, *rest):
        y_ref, yb_ref, xh_ref, r_ref = rest[len(deps):]
        u = ALPHA * x_ref[...] + m_ref[...]
        mu = jnp.mean(u, axis=-1, keepdims=True)
        xc = u - mu
        var = jnp.mean(xc * xc, axis=-1, keepdims=True)
        r = lax.rsqrt(var + LN_EPS)
        xh = xc * r
        y = xh * g_ref[...] + b_ref[...]
        y_ref[...] = y
        yb_ref[...] = y.astype(MXU_DTYPE)
        xh_ref[...] = xh
        r_ref[...] = r

    row = pl.BlockSpec((tr, d), lambda i: (i, 0))
    vec = pl.BlockSpec((1, d), lambda i: (0, 0))
    return pl.pallas_call(
        body,
        out_shape=(jax.ShapeDtypeStruct((s, d), F32), jax.ShapeDtypeStruct((s, d), MXU_DTYPE),
                   jax.ShapeDtypeStruct((s, d), F32), jax.ShapeDtypeStruct((s, 1), F32)),
        grid=(s // tr,),
        in_specs=[row, row, vec, vec] + [ANY] * len(deps),
        out_specs=(row, row, row, pl.BlockSpec((tr, 1), lambda i: (i, 0))),
        compiler_params=_params("parallel"),
        name=name,
    )(x, mixed, g.reshape(1, d), b.reshape(1, d), *deps)


def _ln_bwd(dy, xh, r, g, *, name):
    s, d = dy.shape
    tr = _pick(s, 256, 8)

    def body(dy_ref, xh_ref, r_ref, g_ref, du_ref, dub_ref, dg_ref, db_ref):
        dyv, xhv = dy_ref[...], xh_ref[...]
        dxh = dyv * g_ref[...]
        c1 = jnp.mean(dxh, axis=-1, keepdims=True)
        c2 = jnp.mean(dxh * xhv, axis=-1, keepdims=True)
        du = r_ref[...] * (dxh - c1 - xhv * c2)
        du_ref[...] = du
        dub_ref[...] = du.astype(MXU_DTYPE)

        @pl.when(pl.program_id(0) == 0)
        def _():
            dg_ref[...] = jnp.zeros_like(dg_ref)
            db_ref[...] = jnp.zeros_like(db_ref)

        dg_ref[...] += jnp.sum(dyv * xhv, axis=0, keepdims=True)
        db_ref[...] += jnp.sum(dyv, axis=0, keepdims=True)

    row = pl.BlockSpec((tr, d), lambda i: (i, 0))
    vec = pl.BlockSpec((1, d), lambda i: (0, 0))
    return pl.pallas_call(
        body,
        out_shape=(jax.ShapeDtypeStruct((s, d), F32), jax.ShapeDtypeStruct((s, d), MXU_DTYPE),
                   jax.ShapeDtypeStruct((1, d), F32), jax.ShapeDtypeStruct((1, d), F32)),
        grid=(s // tr,),
        in_specs=[row, row, pl.BlockSpec((tr, 1), lambda i: (i, 0)), vec],
        out_specs=(row, row, vec, vec),
        compiler_params=_params("arbitrary"),
        name=name,
    )(dy, xh, r, g.reshape(1, d))


def _rms_fwd(x, g, *, name):
    s, d = x.shape
    tr = _pick(s, 512, 8)

    def body(x_ref, g_ref, y_ref, r_ref):
        xv = x_ref[...]
        r = lax.rsqrt(jnp.mean(xv * xv, axis=-1, keepdims=True) + RMS_EPS)
        y_ref[...] = (xv * r * g_ref[...]).astype(y_ref.dtype)
        r_ref[...] = r

    return pl.pallas_call(
        body,
        out_shape=(jax.ShapeDtypeStruct((s, d), MXU_DTYPE), jax.ShapeDtypeStruct((s, 1), F32)),
        grid=(s // tr,),
        in_specs=[pl.BlockSpec((tr, d), lambda i: (i, 0)), pl.BlockSpec((1, d), lambda i: (0, 0))],
        out_specs=(pl.BlockSpec((tr, d), lambda i: (i, 0)), pl.BlockSpec((tr, 1), lambda i: (i, 0))),
        compiler_params=_params("parallel"),
        name=name,
    )(x, g.reshape(1, d))


def _rms_bwd(dy, x, r, g, *, name):
    s, d = x.shape
    tr = _pick(s, 512, 8)

    def body(dy_ref, x_ref, r_ref, g_ref, dx_ref, dg_ref):
        dyv, rv = dy_ref[...], r_ref[...]
        xn = x_ref[...] * rv
        dxn = dyv * g_ref[...]
        dx_ref[...] = rv * (dxn - xn * jnp.mean(dxn * xn, axis=-1, keepdims=True))

        @pl.when(pl.program_id(0) == 0)
        def _():
            dg_ref[...] = jnp.zeros_like(dg_ref)

        dg_ref[...] += jnp.sum(dyv * xn, axis=0, keepdims=True)

    row = pl.BlockSpec((tr, d), lambda i: (i, 0))
    vec = pl.BlockSpec((1, d), lambda i: (0, 0))
    return pl.pallas_call(
        body,
        out_shape=(jax.ShapeDtypeStruct((s, d), F32), jax.ShapeDtypeStruct((1, d), F32)),
        grid=(s // tr,),
        in_specs=[row, row, pl.BlockSpec((tr, 1), lambda i: (i, 0)), vec],
        out_specs=(row, vec),
        compiler_params=_params("arbitrary"),
        name=name,
    )(dy, x, r, g.reshape(1, d))


def _rope_tables(s, inverse):
    inv_freq = ROPE_BASE ** (-jnp.arange(0, D_ROPE, 2, dtype=F32) / D_ROPE)
    ang = jnp.arange(s, dtype=F32)[:, None] * inv_freq[None, :]
    cos, sin = jnp.cos(ang), jnp.sin(ang)
    if inverse:
        sin = -sin
    half = D_ROPE // 2
    one, zero = jnp.ones((s, D_NOPE), F32), jnp.zeros((s, D_NOPE), F32)
    pad1, pad0 = jnp.ones((s, LANES - D_NOPE - D_ROPE), F32), jnp.zeros((s, LANES - D_NOPE - D_ROPE), F32)
    zh = jnp.zeros((s, half), F32)
    c = jnp.concatenate([one, cos, cos, pad1], axis=1)
    s_lo = jnp.concatenate([zero, -sin, zh, pad0], axis=1)
    s_hi = jnp.concatenate([zero, zh, sin, pad0], axis=1)
    return c, s_lo, s_hi


def _rope(x, tables, *, out_dtype, head_sum=False, name):
    h, s, w = x.shape
    ts = _pick(s, 2048, 8)

    def body(x_ref, c_ref, lo_ref, hi_ref, y_ref, *sum_ref):
        y = _rotate(x_ref[0], c_ref[...], lo_ref[...], hi_ref[...])
        y_ref[0] = y.astype(y_ref.dtype)
        if head_sum:
            @pl.when(pl.program_id(1) == 0)
            def _():
                sum_ref[0][...] = jnp.zeros_like(sum_ref[0])

            sum_ref[0][...] += y

    tab = pl.BlockSpec((ts, w), lambda i, hh: (i, 0))
    blk = pl.BlockSpec((1, ts, w), lambda i, hh: (hh, i, 0))
    out_shape = [jax.ShapeDtypeStruct((h, s, w), out_dtype)]
    out_specs = [blk]
    if head_sum:
        out_shape.append(jax.ShapeDtypeStruct((s, w), F32))
        out_specs.append(tab)
    res = pl.pallas_call(
        body,
        out_shape=tuple(out_shape),
        grid=(s // ts, h),
        in_specs=[blk, tab, tab, tab],
        out_specs=tuple(out_specs),
        compiler_params=_params("parallel", "arbitrary"),
        name=name,
    )(x, *tables)
    return res if head_sum else res[0]


def _band_scores(q, kw, slope, i, *, scale, n_back, bps):
    b, r, _ = q.shape
    sc = _bdot(q, kw, NT) * scale
    shape = (b, r, 2 * BLK)
    row = lax.broadcasted_iota(jnp.int32, shape, 1) & (BLK - 1)
    col = lax.broadcasted_iota(jnp.int32, shape, 2)
    rel = BLK + row - col
    first_col = jnp.where(i % bps == 0, BLK, 0)
    valid = (rel >= 0) & (rel <= n_back) & (col >= first_col)
    return jnp.where(valid, sc - slope * rel.astype(F32), NEG)


def _band_fwd(q, k, v, slope, sink, *, scale, n_back, bps, name):
    g, b, rows, dh = q.shape
    r = slope.shape[2]
    nq = rows // r
    skv = k.shape[2]
    use_sink = sink is not None

    def body(*refs):
        q_ref, k_ref, v_ref, slope_ref = refs[:4]
        sink_ref = refs[4] if use_sink else None
        o_ref, lse_ref = refs[4 + use_sink:]
        i = pl.program_id(1)
        off = pl.multiple_of(i * BLK, BLK)
        kw = k_ref[0, :, pl.ds(off, 2 * BLK), :]
        vw = v_ref[0, :, pl.ds(off, 2 * BLK), :]
        sc = _band_scores(q_ref[0], kw, slope_ref[0], i, scale=scale, n_back=n_back, bps=bps)
        m = jnp.max(sc, axis=-1, keepdims=True)
        if use_sink:
            m = jnp.maximum(m, sink_ref[0])
        p = jnp.exp(sc - m)
        l = jnp.sum(p, axis=-1, keepdims=True)
        if use_sink:
            l = l + jnp.exp(sink_ref[0] - m)
        o_ref[0] = _bdot(p.astype(MXU_DTYPE), vw, NN) / l
        lse_ref[0] = m + jnp.log(l)

    qspec = pl.BlockSpec((1, b, r, dh), lambda gg, i: (gg, 0, i, 0))
    kspec = pl.BlockSpec((1, b, skv, dh), lambda gg, i: (gg, 0, 0, 0))
    rspec = pl.BlockSpec((1, b, r, 1), lambda gg, i: (gg, 0, 0, 0))
    ins = [q, k, v, slope] + ([sink] if use_sink else [])
    return pl.pallas_call(
        body,
        out_shape=(jax.ShapeDtypeStruct((g, b, rows, dh), F32), jax.ShapeDtypeStruct((g, b, rows, 1), F32)),
        grid=(g, nq),
        in_specs=[qspec, kspec, kspec, rspec] + ([rspec] if use_sink else []),
        out_specs=(qspec, pl.BlockSpec((1, b, r, 1), lambda gg, i: (gg, 0, i, 0))),
        compiler_params=_params("parallel", "arbitrary"),
        name=name,
    )(*ins)


def _band_bwd(q, k, v, do, o, lse, slope, sink, *, scale, n_back, bps, name):
    g, b, rows, dh = q.shape
    r = slope.shape[2]
    nq = rows // r
    skv = k.shape[2]
    use_sink = sink is not None
    stacked = r // BLK

    def body(*refs):
        q_ref, k_ref, v_ref, do_ref, o_ref, lse_ref, slope_ref = refs[:7]
        sink_ref = refs[7] if use_sink else None
        dq_ref, dk_ref, dv_ref = refs[7 + use_sink:10 + use_sink]
        i = pl.program_id(1)

        @pl.when(i == 0)
        def _():
            dk_ref[...] = jnp.zeros_like(dk_ref)
            dv_ref[...] = jnp.zeros_like(dv_ref)

        off = pl.multiple_of(i * BLK, BLK)
        qb = q_ref[0]
        kw = k_ref[0, :, pl.ds(off, 2 * BLK), :]
        vw = v_ref[0, :, pl.ds(off, 2 * BLK), :]
        dof = do_ref[0]
        dob = dof.astype(MXU_DTYPE)
        lse_b = lse_ref[0]
        delta = jnp.sum(dof * o_ref[0], axis=-1, keepdims=True)
        sc = _band_scores(qb, kw, slope_ref[0], i, scale=scale, n_back=n_back, bps=bps)
        p = jnp.exp(sc - lse_b)
        ds = (p * (_bdot(dob, vw, NT) - delta) * scale).astype(MXU_DTYPE)
        dq_ref[0] = _bdot(ds, kw, NN)
        dk_ref[0, :, pl.ds(off, 2 * BLK), :] += _bdot(ds, qb, TN)
        dv_ref[0, :, pl.ds(off, 2 * BLK), :] += _bdot(p.astype(MXU_DTYPE), dob, TN)

        if use_sink:
            dsink_ref = refs[10 + use_sink]

            @pl.when(i == 0)
            def _():
                dsink_ref[...] = jnp.zeros_like(dsink_ref)

            contrib = -jnp.exp(sink_ref[0] - lse_b) * delta
            for n in range(stacked):
                part = jnp.sum(contrib[0, n * BLK:(n + 1) * BLK, :], axis=0, keepdims=True)
                dsink_ref[0, n:n + 1, :] += jnp.broadcast_to(part, (1, LANES))

    def qspec(w):
        return pl.BlockSpec((1, b, r, w), lambda gg, i: (gg, 0, i, 0))

    kspec = pl.BlockSpec((1, b, skv, dh), lambda gg, i: (gg, 0, 0, 0))
    rspec = pl.BlockSpec((1, b, r, 1), lambda gg, i: (gg, 0, 0, 0))
    ins = [q, k, v, do, o, lse, slope] + ([sink] if use_sink else [])
    in_specs = [qspec(dh), kspec, kspec, qspec(dh), qspec(dh), qspec(1), rspec] + ([rspec] if use_sink else [])
    out_shape = [jax.ShapeDtypeStruct((g, b, rows, dh), F32), jax.ShapeDtypeStruct((g, b, skv, dh), F32),
                 jax.ShapeDtypeStruct((g, b, skv, dh), F32)]
    out_specs = [qspec(dh), kspec, kspec]
    if use_sink:
        assert b == 1
        out_shape.append(jax.ShapeDtypeStruct((g, stacked, LANES), F32))
        out_specs.append(pl.BlockSpec((1, stacked, LANES), lambda gg, i: (gg, 0, 0)))
    return pl.pallas_call(
        body,
        out_shape=tuple(out_shape),
        grid=(g, nq),
        in_specs=in_specs,
        out_specs=tuple(out_specs),
        compiler_params=_params("parallel", "arbitrary"),
        name=name,
    )(*ins)


def _pair_masks():
    first = lax.broadcasted_iota(jnp.int32, (1, LANES), 1) < HEAD_DIM
    m0 = first.astype(MXU_DTYPE)
    return first, (m0, 1 - m0)


def _dil_window(ref, i):
    prev = pl.multiple_of(jnp.maximum(i - 1, 0) * BLK, BLK)
    cur = pl.multiple_of(i * BLK, BLK)
    return prev, cur, jnp.concatenate([ref[pl.ds(prev, BLK), :], ref[pl.ds(cur, BLK), :]], axis=0)


def _dil_mask(i, n_back, bps):
    row = lax.broadcasted_iota(jnp.int32, (BLK, 2 * BLK), 0)
    col = lax.broadcasted_iota(jnp.int32, (BLK, 2 * BLK), 1)
    rel = BLK + row - col
    first_col = jnp.where(i % bps == 0, BLK, 0)
    return (rel >= 0) & (rel <= n_back) & (col >= first_col), rel.astype(F32)


def _dil_fwd(slab, slopes, *, scale, n_back, bps, name):
    s, w = slab.shape[0], slab.shape[1] // 3

    def body(q_ref, k_ref, v_ref, o_ref, lse_ref):
        i = pl.program_id(0)
        first, masks = _pair_masks()
        _, _, kw = _dil_window(k_ref, i)
        _, _, vw = _dil_window(v_ref, i)
        valid, rel = _dil_mask(i, n_back, bps)
        for p in range(w // LANES):
            cols = slice(p * LANES, (p + 1) * LANES)
            qp, kp, vp = q_ref[:, cols], kw[:, cols], vw[:, cols]
            outs, lses = [], []
            for hh in range(2):
                sc = _dot(qp * masks[hh], kp, NT) * scale - float(slopes[2 * p + hh]) * rel
                sc = jnp.where(valid, sc, NEG)
                m = jnp.max(sc, axis=-1, keepdims=True)
                e = jnp.exp(sc - m)
                l = jnp.sum(e, axis=-1, keepdims=True)
                outs.append(_dot(e.astype(MXU_DTYPE), vp, NN) / l)
                lses.append(m + jnp.log(l))
            o_ref[:, cols] = jnp.where(first, outs[0], outs[1])
            lse_ref[:, cols] = jnp.where(first, lses[0], lses[1])

    blk = pl.BlockSpec((BLK, w), lambda i: (i, 0))
    return pl.pallas_call(
        body,
        out_shape=(jax.ShapeDtypeStruct((s, w), F32), jax.ShapeDtypeStruct((s, w), F32)),
        grid=(s // BLK,),
        in_specs=[blk, pl.BlockSpec((s, w), lambda i: (0, 1)), pl.BlockSpec((s, w), lambda i: (0, 2))],
        out_specs=(blk, blk),
        compiler_params=_params("arbitrary"),
        name=name,
    )(slab, slab, slab)


def _dil_bwd(slab, pack, slopes, *, scale, n_back, bps, name):
    s, w = slab.shape[0], slab.shape[1] // 3

    def body(q_ref, k_ref, v_ref, do_ref, o_ref, lse_ref, dq_ref, dk_ref, dv_ref):
        i = pl.program_id(0)

        @pl.when(i == 0)
        def _():
            dk_ref[...] = jnp.zeros_like(dk_ref)
            dv_ref[...] = jnp.zeros_like(dv_ref)

        first, masks = _pair_masks()
        prev, cur, kw = _dil_window(k_ref, i)
        _, _, vw = _dil_window(v_ref, i)
        valid, rel = _dil_mask(i, n_back, bps)
        for p in range(w // LANES):
            cols = slice(p * LANES, (p + 1) * LANES)
            qp, kp, vp = q_ref[:, cols], kw[:, cols], vw[:, cols]
            dof, lse_p = do_ref[:, cols], lse_ref[:, cols]
            prod = dof * o_ref[:, cols]
            do_b = dof.astype(MXU_DTYPE)
            dqs, dk_add, dv_add = [], None, None
            for hh in range(2):
                qh, doh = qp * masks[hh], do_b * masks[hh]
                delta = jnp.sum(jnp.where(first, prod, 0.0) if hh == 0 else jnp.where(first, 0.0, prod), axis=-1, keepdims=True)
                sc = _dot(qh, kp, NT) * scale - float(slopes[2 * p + hh]) * rel
                e = jnp.exp(jnp.where(valid, sc, NEG) - lse_p[:, hh * HEAD_DIM:hh * HEAD_DIM + 1])
                ds = (e * (_dot(doh, vp, NT) - delta) * scale).astype(MXU_DTYPE)
                dqs.append(_dot(ds, kp, NN))
                dk_h, dv_h = _dot(ds, qh, TN), _dot(e.astype(MXU_DTYPE), doh, TN)
                dk_add = dk_h if dk_add is None else dk_add + dk_h
                dv_add = dv_h if dv_add is None else dv_add + dv_h
            dq_ref[:, cols] = jnp.where(first, dqs[0], dqs[1])
            dk_ref[pl.ds(prev, BLK), cols] += dk_add[:BLK]
            dk_ref[pl.ds(cur, BLK), cols] += dk_add[BLK:]
            dv_ref[pl.ds(prev, BLK), cols] += dv_add[:BLK]
            dv_ref[pl.ds(cur, BLK), cols] += dv_add[BLK:]

    def blk(c):
        return pl.BlockSpec((BLK, w), lambda i: (i, c))

    def whole(c):
        return pl.BlockSpec((s, w), lambda i: (0, c))

    shp = jax.ShapeDtypeStruct((s, w), F32)
    return pl.pallas_call(
        body,
        out_shape=(shp, shp, shp),
        grid=(s // BLK,),
        in_specs=[blk(0), whole(1), whole(2), blk(0), blk(1), blk(2)],
        out_specs=(blk(0), whole(0), whole(0)),
        compiler_params=_params("arbitrary"),
        name=name,
    )(slab, slab, slab, pack, pack, pack)


def _merge(outs, lses, *, name):
    s, w = outs[0].shape
    tr = _pick(s, 512, 8)

    def body(o0, o1, o2, l0, l1, l2, ob_ref, lt_ref):
        a, b, c = l0[...], l1[...], l2[...]
        m = jnp.maximum(jnp.maximum(a, b), c)
        ea, eb, ec = jnp.exp(a - m), jnp.exp(b - m), jnp.exp(c - m)
        den = ea + eb + ec
        ob_ref[...] = (ea / den) * o0[...] + (eb / den) * o1[...] + (ec / den) * o2[...]
        lt_ref[...] = m + jnp.log(den)

    spec = pl.BlockSpec((tr, w), lambda i: (i, 0))
    return pl.pallas_call(
        body,
        out_shape=(jax.ShapeDtypeStruct((s, w), F32), jax.ShapeDtypeStruct((s, w), F32)),
        grid=(s // tr,),
        in_specs=[spec] * 6,
        out_specs=(spec, spec),
        compiler_params=_params("parallel"),
        name=name,
    )(*outs, *lses)


def _tile_iotas(t):
    return lax.broadcasted_iota(jnp.int32, (t, t), 0), lax.broadcasted_iota(jnp.int32, (t, t), 1)


def _rotate(x, c, s_lo, s_hi):
    half = D_ROPE // 2
    return x * c + pltpu.roll(x, LANES - half, 1) * s_lo + pltpu.roll(x, half, 1) * s_hi


def _mla_keys(kv_h, kr_t, first):
    return jnp.where(first, kv_h, kr_t)


def _mla_fwd(qd, kvd, krp, tables, *, scale, name):
    s = qd.shape[0]
    pairs = qd.shape[1] // (2 * LANES)
    t = min(CAUSAL_TILE, s)

    def body(q_ref, kv_ref, kr_ref, c_ref, lo_ref, hi_ref, o_ref, lse_ref):
        i = pl.program_id(1)
        first = lax.broadcasted_iota(jnp.int32, (1, LANES), 1) < HEAD_DIM
        tabs = (c_ref[...], lo_ref[...], hi_ref[...])
        q_heads = [(_rotate(q_ref[:, hh * LANES:(hh + 1) * LANES], *tabs) * scale).astype(MXU_DTYPE) for hh in range(2)]

        def tile(j, carry, diagonal):
            off = pl.multiple_of(j * t, t)
            kr_t = kr_ref[pl.ds(off, t), :]
            out = []
            for hh in range(2):
                m, l, acc = carry[3 * hh:3 * hh + 3]
                kv_h = kv_ref[pl.ds(off, t), hh * LANES:(hh + 1) * LANES]
                sc = _dot(q_heads[hh], _mla_keys(kv_h, kr_t, first), NT)
                if diagonal:
                    row, col = _tile_iotas(t)
                    sc = jnp.where(row >= col, sc, NEG)
                m_new = jnp.maximum(m, jnp.max(sc, axis=-1, keepdims=True))
                a = jnp.exp(m - m_new)
                p = jnp.exp(sc - m_new)
                out += [m_new, a * l + jnp.sum(p, axis=-1, keepdims=True), a * acc + _dot(p.astype(MXU_DTYPE), kv_h, NN)]
            return tuple(out)

        init = (jnp.full((t, 1), NEG, F32), jnp.zeros((t, 1), F32), jnp.zeros((t, LANES), F32)) * 2
        carry = lax.fori_loop(0, i, lambda j, c: tile(j, c, False), init)
        m0, l0, acc0, m1, l1, acc1 = tile(i, carry, True)
        o_ref[...] = jnp.where(first, pltpu.roll(acc0 / l0, HEAD_DIM, 1), acc1 / l1)
        lse_ref[0] = jnp.where(lax.broadcasted_iota(jnp.int32, (t, 2), 1) == 0, m0 + jnp.log(l0), m1 + jnp.log(l1))

    tab = pl.BlockSpec((t, LANES), lambda p, i: (i, 0))
    return pl.pallas_call(
        body,
        out_shape=(jax.ShapeDtypeStruct((s, pairs * LANES), F32), jax.ShapeDtypeStruct((pairs, s, 2), F32)),
        grid=(pairs, s // t),
        in_specs=[pl.BlockSpec((t, 2 * LANES), lambda p, i: (i, p)), pl.BlockSpec((s, 2 * LANES), lambda p, i: (0, p)),
                  pl.BlockSpec((s, LANES), lambda p, i: (0, 0)), tab, tab, tab],
        out_specs=(pl.BlockSpec((t, LANES), lambda p, i: (i, p)), pl.BlockSpec((1, t, 2), lambda p, i: (p, i, 0))),
        compiler_params=_params("parallel", "arbitrary"),
        name=name,
    )(qd, kvd, krp, *tables)


def _mla_bwd(qd, kvd, krp, tables, do, o, lse, *, do_block0, scale, name):
    s = qd.shape[0]
    pairs = qd.shape[1] // (2 * LANES)
    t = min(CAUSAL_TILE, s)

    def body(q_ref, kv_ref, kr_ref, c_ref, lo_ref, hi_ref, do_ref, o_ref, lse_ref, dq_ref, dkv_ref, dkr_ref):
        i = pl.program_id(1)

        @pl.when(i == 0)
        def _():
            dkv_ref[...] = jnp.zeros_like(dkv_ref)
            dkr_ref[...] = jnp.zeros_like(dkr_ref)

        first = lax.broadcasted_iota(jnp.int32, (1, LANES), 1) < HEAD_DIM
        tabs = (c_ref[...], lo_ref[...], hi_ref[...])
        q_heads = [(_rotate(q_ref[:, hh * LANES:(hh + 1) * LANES], *tabs) * scale).astype(MXU_DTYPE) for hh in range(2)]
        dof = do_ref[...]
        prod = dof * o_ref[...]
        deltas = [jnp.sum(jnp.where(first, prod, 0.0), axis=-1, keepdims=True),
                  jnp.sum(jnp.where(first, 0.0, prod), axis=-1, keepdims=True)]
        do_heads = [jnp.where(first, 0.0, pltpu.roll(dof, HEAD_DIM, 1)).astype(MXU_DTYPE),
                    jnp.where(first, 0.0, dof).astype(MXU_DTYPE)]
        lses = [lse_ref[0][:, hh:hh + 1] for hh in range(2)]

        def tile(j, carry, diagonal):
            off = pl.multiple_of(j * t, t)
            kr_t = kr_ref[pl.ds(off, t), :]
            out, dkr_add = [], None
            for hh in range(2):
                kv_h = kv_ref[pl.ds(off, t), hh * LANES:(hh + 1) * LANES]
                k_h = _mla_keys(kv_h, kr_t, first)
                sc = _dot(q_heads[hh], k_h, NT)
                if diagonal:
                    row, col = _tile_iotas(t)
                    sc = jnp.where(row >= col, sc, NEG)
                p = jnp.exp(sc - lses[hh])
                ds = (p * (_dot(do_heads[hh], kv_h, NT) - deltas[hh])).astype(MXU_DTYPE)
                dk_full = _dot(ds, q_heads[hh], TN)
                dv_full = _dot(p.astype(MXU_DTYPE), do_heads[hh], TN)
                dkv_ref[pl.ds(off, t), hh * LANES:(hh + 1) * LANES] += jnp.where(first, dk_full, dv_full)
                rot = jnp.where(first, 0.0, dk_full)
                dkr_add = rot if dkr_add is None else dkr_add + rot
                out.append(carry[hh] + _dot(ds, k_h, NN))
            dkr_ref[0, pl.ds(off, t), :] += dkr_add
            return tuple(out)

        zacc = jnp.zeros((t, LANES), F32)
        carry = lax.fori_loop(0, i, lambda j, c: tile(j, c, False), (zacc, zacc))
        dq_heads = tile(i, carry, True)
        for hh in range(2):
            dq_ref[:, hh * LANES:(hh + 1) * LANES] = _rotate(dq_heads[hh] * scale, tabs[0], -tabs[1], -tabs[2])

    tab = pl.BlockSpec((t, LANES), lambda p, i: (i, 0))
    qspec = pl.BlockSpec((t, 2 * LANES), lambda p, i: (i, p))
    kvspec = pl.BlockSpec((s, 2 * LANES), lambda p, i: (0, p))
    return pl.pallas_call(
        body,
        out_shape=(jax.ShapeDtypeStruct(qd.shape, F32), jax.ShapeDtypeStruct(kvd.shape, F32),
                   jax.ShapeDtypeStruct((pairs, s, LANES), F32)),
        grid=(pairs, s // t),
        in_specs=[qspec, kvspec, pl.BlockSpec((s, LANES), lambda p, i: (0, 0)), tab, tab, tab,
                  pl.BlockSpec((t, LANES), lambda p, i: (i, do_block0 + p)), pl.BlockSpec((t, LANES), lambda p, i: (i, p)),
                  pl.BlockSpec((1, t, 2), lambda p, i: (p, i, 0))],
        out_specs=(qspec, kvspec, pl.BlockSpec((1, s, LANES), lambda p, i: (p, 0, 0))),
        compiler_params=_params("parallel", "arbitrary"),
        name=name,
    )(qd, kvd, krp, *tables, do, o, lse)


def _split_cumsum(x, tri, terms=2):
    hi = x.astype(BF16)
    if terms == 1:
        return _dot(hi, tri, NN)
    lo = (x - hi.astype(F32)).astype(BF16)
    return _dot(hi, tri, NN) + _dot(lo, tri, NN)


def _chunked_cumsum(x, tri, run, *, reverse, negate=False, terms=2):
    c = tri.shape[0]
    n = x.shape[1] // c
    parts = [None] * n
    for idx in (reversed(range(n)) if reverse else range(n)):
        xc = x[:, idx * c:(idx + 1) * c]
        sums = _split_cumsum(xc, tri, terms)
        parts[idx] = (-run) - sums if negate else run + sums
        run = run + jnp.sum(xc, axis=-1, keepdims=True)
    return (parts[0] if n == 1 else jnp.concatenate(parts, axis=1)), run


def _sb_logs(z):
    l1 = jnp.log(1.0 + jnp.exp(-jnp.abs(z)))
    return jnp.minimum(z, 0.0) - l1, -jnp.maximum(z, 0.0) - l1


def _scaled_query_heads(q, masks, scale):
    assert math.log2(scale).is_integer(), scale
    return [q * (m * scale).astype(q.dtype) for m in masks]


def _sb_fwd(qkv, *, heads, scale, name):
    s = qkv.shape[0]
    pairs = heads * HEAD_DIM // LANES
    t = min(CAUSAL_TILE, s)
    cc = min(CUM_CHUNK, t)

    def body(q_ref, k_ref, v_ref, o_ref, t_ref):
        i = pl.program_id(1)
        first, masks = _pair_masks()
        q_heads = _scaled_query_heads(q_ref[...], masks, scale)
        crow, ccol = _tile_iotas(cc)
        after = (crow > ccol).astype(BF16)

        def tile(j, carry, diagonal):
            off = pl.multiple_of(j * t, t)
            kb = k_ref[pl.ds(off, t), :]
            vb = v_ref[pl.ds(off, t), :]
            if diagonal:
                row, col = _tile_iotas(t)
                strict = row > col
            out = []
            for hh in range(2):
                run, acc = carry[2 * hh], carry[2 * hh + 1]
                log_beta, log_keep = _sb_logs(_dot(q_heads[hh], kb, NT))
                if diagonal:
                    log_keep = jnp.where(strict, log_keep, 0.0)
                a, run = _chunked_cumsum(log_keep, after, run, reverse=True)
                w = jnp.exp(log_beta + a)
                if diagonal:
                    w = jnp.where(strict, w, 0.0)
                out += [run, acc + _dot(w.astype(MXU_DTYPE), vb, NN)]
            return tuple(out)

        zero, zacc = jnp.zeros((t, 1), F32), jnp.zeros((t, LANES), F32)
        carry = tile(i, (zero, zacc, zero, zacc), True)
        run0, acc0, run1, acc1 = lax.fori_loop(0, i, lambda jj, c: tile(i - 1 - jj, c, False), carry)
        o_ref[...] = jnp.where(first, acc0, acc1)
        t_ref[0] = jnp.where(lax.broadcasted_iota(jnp.int32, (t, 2), 1) == 0, run0, run1)

    return pl.pallas_call(
        body,
        out_shape=(jax.ShapeDtypeStruct((s, heads * HEAD_DIM), F32), jax.ShapeDtypeStruct((pairs, s, 2), F32)),
        grid=(pairs, s // t),
        in_specs=[pl.BlockSpec((t, LANES), lambda p, i: (i, p)),
                  pl.BlockSpec((s, LANES), lambda p, i: (0, pairs + p)),
                  pl.BlockSpec((s, LANES), lambda p, i: (0, 2 * pairs + p))],
        out_specs=(pl.BlockSpec((t, LANES), lambda p, i: (i, p)), pl.BlockSpec((1, t, 2), lambda p, i: (p, i, 0))),
        compiler_params=_params("parallel", "arbitrary"),
        name=name,
    )(qkv, qkv, qkv)


def _sb_bwd(qkv, do, total, *, heads, scale, name):
    s = qkv.shape[0]
    pairs = heads * HEAD_DIM // LANES
    t = min(CAUSAL_TILE, s)
    cc = min(CUM_CHUNK, t)

    def body(q_ref, k_ref, v_ref, do_ref, t_ref, dq_ref, dk_ref, dv_ref):
        i = pl.program_id(1)

        @pl.when(i == 0)
        def _():
            dk_ref[...] = jnp.zeros_like(dk_ref)
            dv_ref[...] = jnp.zeros_like(dv_ref)

        first, masks = _pair_masks()
        q_heads = _scaled_query_heads(q_ref[...], masks, scale)
        do_b = do_ref[...].astype(MXU_DTYPE)
        do_heads = [do_b * m for m in masks]
        tots = [t_ref[0][:, hh:hh + 1] for hh in range(2)]
        crow, ccol = _tile_iotas(cc)
        upto = (crow <= ccol).astype(BF16)
        before = (crow < ccol).astype(BF16)

        def tile(j, carry, diagonal):
            off = pl.multiple_of(j * t, t)
            kb = k_ref[pl.ds(off, t), :]
            vb = v_ref[pl.ds(off, t), :]
            if diagonal:
                row, col = _tile_iotas(t)
                strict = row > col
            out, dk_add, dv_add = [], None, None
            for hh in range(2):
                run_keep, run_g, dq_acc = carry[3 * hh:3 * hh + 3]
                log_beta, log_keep = _sb_logs(_dot(q_heads[hh], kb, NT))
                keep = jnp.exp(log_keep)
                if diagonal:
                    log_keep = jnp.where(strict, log_keep, 0.0)
                a, run_keep = _chunked_cumsum(log_keep, upto, run_keep - tots[hh], reverse=False, negate=True)
                run_keep = run_keep + tots[hh]
                w = jnp.exp(log_beta + a)
                if diagonal:
                    w = jnp.where(strict, w, 0.0)
                g = w * _dot(do_heads[hh], vb, NT)
                prefix, run_g = _chunked_cumsum(g, before, run_g, reverse=False, terms=1)
                dz = g * keep - (1.0 - keep) * prefix
                if diagonal:
                    dz = jnp.where(strict, dz, 0.0)
                dz = dz.astype(MXU_DTYPE)
                dk_h = _dot(dz, q_heads[hh], TN)
                dv_h = _dot(w.astype(MXU_DTYPE), do_heads[hh], TN)
                dk_add = dk_h if dk_add is None else dk_add + dk_h
                dv_add = dv_h if dv_add is None else dv_add + dv_h
                out += [run_keep, run_g, dq_acc + _dot(dz, kb, NN)]
            dk_ref[pl.ds(off, t), :] += dk_add
            dv_ref[pl.ds(off, t), :] += dv_add
            return tuple(out)

        zero, zacc = jnp.zeros((t, 1), F32), jnp.zeros((t, LANES), F32)
        carry = lax.fori_loop(0, i, lambda j, c: tile(j, c, False), (zero, zero, zacc, zero, zero, zacc))
        res = tile(i, carry, True)
        dq_ref[...] = jnp.where(first, res[2], res[5]) * scale

    qspec = pl.BlockSpec((t, LANES), lambda p, i: (i, p))
    shp = jax.ShapeDtypeStruct((s, heads * HEAD_DIM), F32)
    return pl.pallas_call(
        body,
        out_shape=(shp, shp, shp),
        grid=(pairs, s // t),
        in_specs=[qspec, pl.BlockSpec((s, LANES), lambda p, i: (0, pairs + p)),
                  pl.BlockSpec((s, LANES), lambda p, i: (0, 2 * pairs + p)), qspec,
                  pl.BlockSpec((1, t, 2), lambda p, i: (p, i, 0))],
        out_specs=(qspec, pl.BlockSpec((s, LANES), lambda p, i: (0, p)), pl.BlockSpec((s, LANES), lambda p, i: (0, p))),
        compiler_params=_params("parallel", "arbitrary"),
        name=name,
    )(qkv, qkv, qkv, do, total)


def _loss_head(y, target, *, name):
    s, d = y.shape
    tr = _pick(s, 256, 8)

    def body(y_ref, t_ref, dy_ref, loss_ref):
        err = y_ref[...] - t_ref[...]
        dy_ref[...] = err * (1.0 / d)

        @pl.when(pl.program_id(0) == 0)
        def _():
            loss_ref[...] = jnp.zeros_like(loss_ref)

        per_tok = jnp.mean(err * err, axis=-1, keepdims=True)
        loss_ref[...] += 0.5 * jnp.sum(per_tok, axis=0, keepdims=True)

    row = pl.BlockSpec((tr, d), lambda i: (i, 0))
    return pl.pallas_call(
        body,
        out_shape=(jax.ShapeDtypeStruct((s, d), F32), jax.ShapeDtypeStruct((1, LANES), F32)),
        grid=(s // tr,),
        in_specs=[row, row],
        out_specs=(row, pl.BlockSpec((1, LANES), lambda i: (0, 0))),
        compiler_params=_params("arbitrary"),
        name=name,
    )(y, target)


def _adamw(w, grads, m, v, *, name, my_chip=None, deps=()):
    nl, r, c = w.shape
    pieces = my_chip is not None
    cp = (grads[0][0] if pieces else grads[0]).shape[-1]
    tr = _pick(r, min(256, max(16, 262144 // cp)), 8)
    per = 4 if pieces else 1

    def body(chip_ref, *refs):
        w_ref, m_ref, v_ref = refs[:3]
        g_refs = refs[3:3 + per * nl]
        g_out, d_ref, m2_ref, v2_ref = refs[3 + per * nl + len(deps):]

        def grad(n):
            if not pieces:
                return g_refs[n][:, :c]
            own, r0, r1, r2 = (t[0, :, :c].astype(F32) for t in g_refs[4 * n:4 * n + 4])
            return ((own + r0) + r1) + r2

        layer = pl.program_id(0)
        gv = grad(0)
        for n in range(1, nl):
            gv = jnp.where(layer == n, grad(n), gv)
        m2 = ADAM_B1 * m_ref[0] + (1.0 - ADAM_B1) * gv
        v2 = ADAM_B2 * v_ref[0] + (1.0 - ADAM_B2) * (gv * gv)
        m_hat = m2 / (1.0 - ADAM_B1 ** ADAM_STEP)
        v_hat = v2 / (1.0 - ADAM_B2 ** ADAM_STEP)
        g_out[0] = gv
        d_ref[0] = -ADAM_LR * (m_hat / (jnp.sqrt(v_hat) + ADAM_EPS) + ADAM_WD * w_ref[0])
        m2_ref[0] = m2
        v2_ref[0] = v2

    blk = pl.BlockSpec((1, tr, c), lambda l, i, chip_ref: (l, i, 0))
    if pieces:
        g_specs = [pl.BlockSpec((1, tr, cp), lambda l, i, chip_ref: (chip_ref[0], i, 0))]
        g_specs += [pl.BlockSpec((1, tr, cp), lambda l, i, chip_ref, k=k: (k, i, 0)) for k in range(3)]
        g_ins = [t for partial, recv in grads for t in (partial, recv, recv, recv)]
        chip = my_chip.reshape(1).astype(jnp.int32)
    else:
        g_specs, g_ins, chip = [pl.BlockSpec((tr, cp), lambda l, i, chip_ref: (i, 0))], list(grads), jnp.zeros((1,), jnp.int32)
    shp = jax.ShapeDtypeStruct((nl, r, c), F32)
    grid_spec = pltpu.PrefetchScalarGridSpec(
        num_scalar_prefetch=1,
        grid=(nl, r // tr),
        in_specs=[blk, blk, blk] + g_specs * nl + [ANY] * len(deps),
        out_specs=(blk, blk, blk, blk),
    )
    return pl.pallas_call(
        body,
        out_shape=(shp, shp, shp, shp),
        grid_spec=grid_spec,
        compiler_params=_params("parallel", "parallel"),
        name=name,
    )(chip, w, m, v, *g_ins, *deps)


def _pair_sum(mine, recv, my_c, *, name):
    _, r, c = mine.shape
    tr = _pick(r, 512, 16)

    def body(c_ref, a_ref, b_ref, o_ref):
        o_ref[0] = (a_ref[0].astype(F32) + b_ref[0].astype(F32)).astype(o_ref.dtype)

    grid_spec = pltpu.PrefetchScalarGridSpec(
        num_scalar_prefetch=1,
        grid=(4, r // tr),
        in_specs=[pl.BlockSpec((1, tr, c), lambda kk, i, c_ref: (2 * kk + c_ref[0], i, 0)),
                  pl.BlockSpec((1, tr, c), lambda kk, i, c_ref: (kk, i, 0))],
        out_specs=pl.BlockSpec((1, tr, c), lambda kk, i, c_ref: (kk, i, 0)),
    )
    return pl.pallas_call(
        body,
        out_shape=jax.ShapeDtypeStruct((4, r, c), mine.dtype),
        grid_spec=grid_spec,
        compiler_params=_params("parallel", "parallel"),
        name=name,
    )(my_c.reshape(1).astype(jnp.int32), mine, recv)


def _sum_devices(stack, *, name):
    n, r, c = stack.shape

    def body(s_ref, o_ref):
        acc = s_ref[0]
        for dev in range(1, n):
            acc = acc + s_ref[dev]
        o_ref[...] = acc

    return pl.pallas_call(
        body,
        out_shape=jax.ShapeDtypeStruct((r, c), F32),
        in_specs=[pl.BlockSpec(memory_space=pltpu.VMEM)],
        out_specs=pl.BlockSpec(memory_space=pltpu.VMEM),
        name=name,
    )(stack)


def _mesh_pos():
    return lax.axis_index("x"), lax.axis_index("y"), lax.axis_index("c")


def _all_gather(shards, *, name, deps=()):
    n = len(shards)

    def body(*refs):
        x_refs, out_refs = refs[:n], refs[n + len(deps):2 * n + len(deps)]
        send_sems, recv_sems, local_sems = refs[2 * n + len(deps):]
        x, y, cc = _mesh_pos()
        me, sibling = (x, y, cc), (x, y, 1 - cc)
        flip_x, flip_y = cc, 1 - cc
        first = (x + flip_x - 2 * x * flip_x, y + flip_y - 2 * y * flip_y)
        other = (x + flip_y - 2 * x * flip_y, y + flip_x - 2 * y * flip_x)
        diagonal = (1 - x, 1 - y)

        def rows(a, px, py, pc):
            return out_refs[a].at[4 * px + 2 * py + pc]

        def copy(a, kk, block, to, src=None):
            return pltpu.make_async_remote_copy(
                src_ref=rows(a, *block) if src is None else src, dst_ref=rows(a, *block),
                send_sem=send_sems.at[7 * a + kk], recv_sem=recv_sems.at[7 * a + kk],
                device_id=to, device_id_type=MESH)

        sends, own = [], []
        for a in range(n):
            own.append(pltpu.make_async_copy(x_refs[a], rows(a, *me), local_sems.at[a]))
            own[a].start()
            out = [copy(a, 0, me, sibling, src=x_refs[a]), copy(a, 1, me, (*first, cc), src=x_refs[a]),
                   copy(a, 2, me, (*other, cc), src=x_refs[a])]
            for cp in out:
                cp.start()
            sends += out
        for a in range(n):
            copy(a, 1, (*first, cc), me).wait_recv()
            out = [copy(a, 3, (*first, cc), (*other, cc)), copy(a, 4, (*first, cc), sibling)]
            for cp in out:
                cp.start()
            copy(a, 2, (*other, cc), me).wait_recv()
            out.append(copy(a, 5, (*other, cc), sibling))
            out[2].start()
            sends += out
        for a in range(n):
            copy(a, 3, (*diagonal, cc), me).wait_recv()
            passed = copy(a, 6, (*diagonal, cc), sibling)
            passed.start()
            sends.append(passed)
        for a in range(n):
            copy(a, 0, sibling, me).wait_recv()
            copy(a, 4, (*other, 1 - cc), me).wait_recv()
            copy(a, 5, (*first, 1 - cc), me).wait_recv()
            copy(a, 6, (*diagonal, 1 - cc), me).wait_recv()
        for cp in sends:
            cp.wait_send()
        for cp in own:
            cp.wait()

    return pl.pallas_call(
        body,
        out_shape=tuple(jax.ShapeDtypeStruct((N_DEV,) + t.shape, t.dtype) for t in shards),
        in_specs=[ANY] * (n + len(deps)),
        out_specs=tuple([ANY] * n),
        scratch_shapes=[pltpu.SemaphoreType.DMA((7 * n,)), pltpu.SemaphoreType.DMA((7 * n,)),
                        pltpu.SemaphoreType.DMA((n,))],
        name=name,
    )(*shards, *deps)


def _plan_own_blocks(n):
    def plan(refs, send_sems, recv_sems, outgoing):
        x, y, cc = _mesh_pos()
        peers = [(x, y, 1 - cc), (1 - x, y, cc), (x, 1 - y, cc), (1 - x, 1 - y, cc)]
        copies = []
        for a in range(n):
            land = refs[n + a]
            for kk, (px, py, pc) in enumerate(peers):
                block = (x, y, cc) if outgoing else (px, py, pc)
                rows = land.at[4 * block[0] + 2 * block[1] + block[2]]
                copies.append(pltpu.make_async_remote_copy(
                    src_ref=refs[a] if outgoing else rows, dst_ref=rows, send_sem=send_sems.at[4 * a + kk],
                    recv_sem=recv_sems.at[4 * a + kk], device_id=(px, py, pc), device_id_type=MESH))
        return copies

    plan.n_sems = 4 * n
    return plan


def _plan_pass_on(n):
    def plan(refs, send_sems, recv_sems, outgoing):
        x, y, cc = _mesh_pos()
        copies = []
        for a in range(n):
            for j, (px, py) in enumerate([(1 - x, y), (x, 1 - y), (1 - x, 1 - y)]):
                rows = refs[a].at[4 * px + 2 * py + (cc if outgoing else 1 - cc)]
                copies.append(pltpu.make_async_remote_copy(
                    src_ref=rows, dst_ref=rows, send_sem=send_sems.at[3 * a + j], recv_sem=recv_sems.at[3 * a + j],
                    device_id=(x, y, 1 - cc), device_id_type=MESH))
        return copies

    plan.n_sems = 3 * n
    return plan


def _plan_to_sibling(n):
    def plan(refs, send_sems, recv_sems, outgoing):
        x, y, cc = _mesh_pos()
        copies = []
        for a in range(n):
            for chip in range(4):
                dst = refs[n + a].at[chip]
                copies.append(pltpu.make_async_remote_copy(
                    src_ref=refs[a].at[2 * chip + (1 - cc)] if outgoing else dst, dst_ref=dst,
                    send_sem=send_sems.at[4 * a + chip], recv_sem=recv_sems.at[4 * a + chip],
                    device_id=(x, y, 1 - cc), device_id_type=MESH))
        return copies

    plan.n_sems = 4 * n
    return plan


def _plan_to_chips(n):
    def plan(refs, send_sems, recv_sems, outgoing):
        x, y, cc = _mesh_pos()
        copies = []
        for a in range(n):
            for j, (px, py) in enumerate([(1 - x, y), (x, 1 - y), (1 - x, 1 - y)]):
                dst = refs[n + a].at[j]
                copies.append(pltpu.make_async_remote_copy(
                    src_ref=refs[a].at[2 * px + py] if outgoing else dst, dst_ref=dst,
                    send_sem=send_sems.at[3 * a + j], recv_sem=recv_sems.at[3 * a + j],
                    device_id=(px, py, cc), device_id_type=MESH))
        return copies

    plan.n_sems = 3 * n
    return plan


def _in_hbm(t):
    return pltpu.with_memory_space_constraint(t, pltpu.HBM)


def _exchange_start(plan, bufs, after, *, name):
    nb, na = len(bufs), len(after)

    def body(*refs):
        outs = refs[nb + na:]
        for cp in plan(refs[:nb], outs[0], outs[1], True):
            cp.start()
        outs[2 + nb][...] = jnp.zeros_like(outs[2 + nb])

    res = pl.pallas_call(
        body,
        out_shape=(pltpu.SemaphoreType.DMA((plan.n_sems,)), pltpu.SemaphoreType.DMA((plan.n_sems,)),
                   *[pltpu.HBM(t.shape, t.dtype) for t in bufs], jax.ShapeDtypeStruct((8, LANES), F32)),
        in_specs=[HBM_SPEC] * nb + [ANY] * na,
        out_specs=(SEM_SPEC, SEM_SPEC, *[HBM_SPEC] * nb, pl.BlockSpec(memory_space=pltpu.VMEM)),
        input_output_aliases={i: 2 + i for i in range(nb)},
        compiler_params=pltpu.CompilerParams(has_side_effects=DATAFLOW_EFFECT),
        name=name,
    )(*[_in_hbm(t) for t in bufs], *after)
    return plan, res[:2], list(res[2:2 + nb]), res[2 + nb]


def _exchange_wait(flight, after, *, name):
    plan, sems, bufs, _ = flight
    nb = len(bufs)

    def body(*refs):
        send_sems, recv_sems = refs[nb], refs[nb + 1]
        for cp in plan(refs[:nb], send_sems, recv_sems, False):
            cp.wait_recv()
        for cp in plan(refs[:nb], send_sems, recv_sems, True):
            cp.wait_send()

    res = pl.pallas_call(
        body,
        out_shape=tuple(pltpu.HBM(t.shape, t.dtype) for t in bufs),
        in_specs=[HBM_SPEC] * nb + [SEM_SPEC, SEM_SPEC] + [ANY] * len(after),
        out_specs=tuple([HBM_SPEC] * nb),
        input_output_aliases={i: i for i in range(nb)},
        compiler_params=pltpu.CompilerParams(has_side_effects=DATAFLOW_EFFECT),
        name=name,
    )(*bufs, *sems, *after)
    return list(res)


_LANE_PADDED = ("even_w_in", "odd_w_in")
_W_GROUPS = {"even": ("even_w_in", "even_w_out", "norm_gains"), "mlp0": ("mlp_w1_0", "mlp_w2_0"),
             "odd": ("odd_w_in", "odd_w_uq", "odd_w_ukv", "odd_w_out"), "mlp1": ("mlp_w1_1", "mlp_w2_1")}


class _Exchanges:
    def __init__(self, raw_shards, n_q, n_kv):
        self.n_q, self.n_kv = n_q, n_kv
        self.mx, self.my, self.mc = _mesh_pos()
        self.dev = 4 * self.mx + 2 * self.my + self.mc
        self.raw = raw_shards
        self.flights, self.gathered, self.grad_blocks, self.grad_names, self.reduced = {}, {}, {}, {}, {}
        self.ready = {}

    def prepare(self, group, behind=()):
        raw = [self.raw[n] for n in _W_GROUPS[group]]
        if behind:
            raw, _ = lax.optimization_barrier((raw, list(behind)))
        srcs = [t if n == "norm_gains" else
                jnp.pad(t, ((0, 0), (0, _lane_pad(t.shape[1]) - t.shape[1]))).astype(BF16) if n in _LANE_PADDED else
                t.astype(BF16) for n, t in zip(_W_GROUPS[group], raw)]
        lands = [lax.dynamic_update_slice(lax.empty((N_DEV,) + t.shape, t.dtype), t[None], (self.dev, 0, 0)) for t in srcs]
        self.ready[group] = srcs + lands
        return lands

    def start(self):
        return self._w_begin("even", [])

    def _w_begin(self, group, after):
        if group not in self.ready:
            self.prepare(group)
        bufs = self.ready.pop(group)
        self.flights[group] = _exchange_start(_plan_own_blocks(len(bufs) // 2), bufs, after, name=f"comm_{group}_own_start")
        return [self.flights[group][3]]

    def _w_turn(self, group, after):
        bufs = _exchange_wait(self.flights[group], after, name=f"comm_{group}_own_wait")
        n = len(bufs) // 2
        self.flights[group] = _exchange_start(_plan_pass_on(n), bufs[n:], [], name=f"comm_{group}_pass_start")
        return [self.flights[group][3]]

    def _w_end(self, group, after):
        self.gathered.update(zip(_W_GROUPS[group], _exchange_wait(self.flights.pop(group), after, name=f"comm_{group}_pass_wait")))

    def weights(self, group):
        return {n: self.gathered[n] for n in _W_GROUPS[group]}

    def norm_gains(self):
        gains = self.gathered["norm_gains"][:, 0]
        return gains[:, :self.n_q].reshape(-1), gains[:, self.n_q:self.n_q + self.n_kv].reshape(-1)

    def grads(self, group, named_blocks):
        self.grad_names[group] = [n for n, _ in named_blocks]
        self.grad_blocks[group] = [t for _, t in named_blocks]

    def _g_begin(self, group, after):
        blocks = self.grad_blocks[group]
        lands = [lax.empty((4,) + t.shape[1:], t.dtype) for t in blocks]
        self.flights[group] = _exchange_start(_plan_to_sibling(len(blocks)), blocks + lands, after, name=f"comm_{group}_sib_start")
        return [self.flights[group][3]]

    def _g_turn(self, group, after):
        bufs = _exchange_wait(self.flights[group], after, name=f"comm_{group}_sib_wait")
        n = len(bufs) // 2
        partial = [_pair_sum(a, b, self.mc, name=f"pair_sum_{nm}") for nm, a, b in zip(self.grad_names[group], bufs[:n], bufs[n:])]
        lands = [lax.empty((3,) + t.shape[1:], t.dtype) for t in partial]
        self.flights[group] = _exchange_start(_plan_to_chips(n), partial + lands, [], name=f"comm_{group}_chips_start")
        return [self.flights[group][3]]

    def _g_end(self, group, after):
        bufs = _exchange_wait(self.flights.pop(group), after, name=f"comm_{group}_chips_wait")
        n = len(bufs) // 2
        self.reduced.update(zip(self.grad_names[group], zip(bufs[:n], bufs[n:])))

    _SCHEDULE = {
        "x_cast": (("w_turn", "even"), ("w_begin", "mlp0"), ("w_end", "even")),
        "even_out": (("w_turn", "mlp0"), ("w_begin", "odd")),
        "ln1_l0": (("w_end", "mlp0"),),
        "ln2_l0": (("w_turn", "odd"), ("w_begin", "mlp1"), ("w_end", "odd")),
        "odd_out": (("w_turn", "mlp1"),),
        "ln1_l1": (("w_end", "mlp1"),),
        "dw_l1": (("g_begin", "mlp1"),),
        "ln1_bwd_l1": (("g_turn", "mlp1"),),
        "odd_in_dw": (("g_begin", "odd"),),
        "mlp2_dx_l0": (("g_end", "mlp1"), ("g_turn", "odd")),
        "dw_l0": (("g_begin", "mlp0"),),
        "ln1_bwd_l0": (("g_end", "odd"), ("g_turn", "mlp0")),
        "even_in_dw": (("g_begin", "even"),),
        "even_in_dx": (("g_end", "mlp0"), ("g_turn", "even")),
        "finish": (("g_end", "even"),),
    }

    def sync(self, tag, after):
        latest, started = list(after), []
        for what, group in self._SCHEDULE[tag]:
            out = getattr(self, "_" + what)(group, latest)
            if out:
                latest = started = out
        return started


def _alibi(n):
    return 2.0 ** (-8.0 * np.arange(1, n + 1, dtype=np.float32) / n)


def _heads(t, n):
    s = t.shape[0]
    return t.reshape(s, n, t.shape[1] // n).transpose(1, 0, 2)


def _unheads(t):
    n, s, dh = t.shape
    return t.transpose(1, 0, 2).reshape(s, n * dh)


def _to_strided(t, d):
    s, x = t.shape
    return t if d == 1 else t.reshape(s // d, d, x).transpose(1, 0, 2).reshape(s, x)


def _from_strided(t, d):
    s, x = t.shape
    return t if d == 1 else t.reshape(d, s // d, x).transpose(1, 0, 2).reshape(s, x)


def _true_columns(t, c, n_pad):
    tail = [jnp.zeros((t.shape[1], n_pad - N_DEV * c), t.dtype)] if n_pad > N_DEV * c else []
    return jnp.concatenate([t[dev, :, :c] for dev in range(N_DEV)] + tail, axis=1)


def _column_blocks(t, c, cp):
    return jnp.stack([jnp.pad(t[:, dev * c:(dev + 1) * c], ((0, 0), (0, cp - c))) for dev in range(N_DEV)])


def _stack_rows(t, nq):
    return t.reshape(nq, BLK, A_KV_HEADS, A_GROUP, HEAD_DIM).transpose(2, 0, 3, 1, 4).reshape(
        A_KV_HEADS, 1, nq * A_GROUP * BLK, HEAD_DIM)


def _unstack_rows(t, nq):
    return t.reshape(A_KV_HEADS, nq, A_GROUP, BLK, HEAD_DIM).transpose(1, 3, 0, 2, 4).reshape(
        nq * BLK, A_Q_W)


def _lead_block(t):
    return jnp.pad(t, ((0, 0), (BLK, 0), (0, 0)))


def _columns(t):
    return t.transpose(1, 0, 2).reshape(t.shape[1], -1)


def _rows(t):
    return t.reshape(-1, t.shape[2])


def _by_column_block(t):
    return t.reshape(t.shape[0], N_DEV, -1).transpose(1, 0, 2)


def _local_step(x0, target, comm, sinks, ln1_g, ln1_b, ln2_g, ln2_b):
    s, d = x0.shape
    scale_h = 1.0 / math.sqrt(HEAD_DIM)
    scale_d = 1.0 / math.sqrt(D_NOPE + D_ROPE)
    nq = s // BLK
    even_c, odd_c = EVEN_IN // N_DEV, ODD_IN // N_DEV
    bf = lambda t: t.astype(MXU_DTYPE)
    blocks = dict(b_blocks=True)

    tok = comm.start()
    x0b = bf(lax.optimization_barrier((x0, tok))[0])
    later = [t for group in ("mlp0", "odd", "mlp1") for t in comm.prepare(group, behind=tok)]
    tok = comm.sync("x_cast", [x0b] + later)
    w_even = comm.weights("even")
    even_cp, even_n = w_even["even_w_in"].shape[2], -(-EVEN_IN // 1024) * 1024
    w_even_in, w_even_out = _true_columns(w_even["even_w_in"], even_c, even_n), _columns(w_even["even_w_out"])
    h_e = _mm(x0b, w_even_in, out_dtypes=(MXU_DTYPE,), name="even_in_fwd", deps=tok)
    qa = _stack_rows(h_e[:, :A_Q_W], nq)
    ka = _lead_block(_heads(h_e[:, A_Q_W:A_Q_W + A_KV_W], A_KV_HEADS))[:, None]
    va = _lead_block(_heads(h_e[:, A_Q_W + A_KV_W:A_Q_W + 2 * A_KV_W], A_KV_HEADS))[:, None]
    rows_a = A_GROUP * BLK
    slope_a = jnp.asarray(np.repeat(_alibi(A_Q_HEADS).reshape(A_KV_HEADS, A_GROUP), BLK, axis=1).reshape(
        A_KV_HEADS, 1, rows_a, 1))
    sink_a = jnp.broadcast_to(sinks.reshape(A_KV_HEADS, A_GROUP, 1), (A_KV_HEADS, A_GROUP, BLK)).reshape(
        A_KV_HEADS, 1, rows_a, 1)
    a_cfg = dict(scale=scale_h, n_back=A_WINDOW - 1, bps=nq)
    oa, lse_a = _band_fwd(qa, ka, va, slope_a, sink_a, name="swa_fwd", **a_cfg)
    b_slab, b_cfg, b_out, b_lse = [], [], [], []
    base = A_Q_W + 2 * A_KV_W
    for gi, (window, dil) in enumerate(B_PATTERNS):
        slab = _to_strided(h_e[:, base + gi * 3 * B_W: base + (gi + 1) * 3 * B_W], dil)
        cfg = dict(slopes=_alibi(B_HEADS) * dil, scale=scale_h, n_back=window // dil, bps=nq // dil)
        o, lse = _dil_fwd(slab, name=f"dil{gi}_fwd", **cfg)
        both = _from_strided(jnp.concatenate([o, lse], axis=1), dil)
        b_slab.append(slab)
        b_cfg.append(cfg)
        b_out.append(both[:, :B_W])
        b_lse.append(both[:, B_W:])
    ob, lse_b = _merge(b_out, b_lse, name="dil_merge")
    y_e = bf(jnp.concatenate([_unstack_rows(oa, nq), ob], axis=1))
    mixed = _mm(y_e, w_even_out, name="even_out_fwd")
    tok = comm.sync("even_out", [mixed])
    x0n, x0nb, xh1_0, r1_0 = _ln_fwd(x0, mixed, ln1_g[0], ln1_b[0], name="ln1_fwd_l0", deps=tok)
    comm.sync("ln1_l0", [x0nb])
    w_mlp0 = comm.weights("mlp0")
    w1_0, w2_0 = w_mlp0["mlp_w1_0"], _rows(w_mlp0["mlp_w2_0"])
    act0, hid0 = _mm(x0nb, w1_0, out_dtypes=(MXU_DTYPE, MXU_DTYPE), epilogue=_relu_sq, name="mlp1_fwd_l0", **blocks)
    mlp = _mm(hid0, w2_0, name="mlp2_fwd_l0")
    x1, x1b, xh2_0, r2_0 = _ln_fwd(x0n, mlp, ln2_g[0], ln2_b[0], name="ln2_fwd_l0")

    tok = comm.sync("ln2_l0", [x1b])
    w_odd = comm.weights("odd")
    odd_cp, odd_n = w_odd["odd_w_in"].shape[2], -(-ODD_IN // 1024) * 1024
    w_odd_in, w_uq, w_ukv, w_odd_out = (_true_columns(w_odd["odd_w_in"], odd_c, odd_n), _columns(w_odd["odd_w_uq"]),
                                        _columns(w_odd["odd_w_ukv"]), _rows(w_odd["odd_w_out"]))
    gq, gkv = comm.norm_gains()
    h_o = _mm(x1b, w_odd_in, name="odd_in_fwd", deps=tok)
    qkv_c = bf(h_o[:, :3 * C_W])
    oc, sb_total = _sb_fwd(qkv_c, heads=C_HEADS, scale=scale_h, name="sb_fwd")
    o_cq, o_ckv, o_kr = 3 * C_W, 3 * C_W + D_Q_RANK, 3 * C_W + D_Q_RANK + D_KV_RANK
    cq, ckv, kr = h_o[:, o_cq:o_ckv], h_o[:, o_ckv:o_kr], h_o[:, o_kr:o_kr + D_ROPE]
    ncq, rq = _rms_fwd(cq, gq, name="rms_q_fwd")
    nckv, rkv = _rms_fwd(ckv, gkv, name="rms_kv_fwd")
    lane_pad = LANES - D_NOPE - D_ROPE
    w_uq = jnp.pad(w_uq.reshape(D_Q_RANK, D_HEADS, D_NOPE + D_ROPE), ((0, 0), (0, 0), (0, lane_pad))).reshape(
        D_Q_RANK, D_HEADS * LANES)
    qd = _mm(ncq, w_uq, name="uq_fwd")
    kvd = _mm(nckv, w_ukv, out_dtypes=(MXU_DTYPE,), name="ukv_fwd")
    rope_t = _rope_tables(s, inverse=False)
    krp = _rope(jnp.pad(kr, ((0, 0), (D_NOPE, lane_pad)))[None], rope_t, out_dtype=MXU_DTYPE, name="rope_k_fwd")[0]
    od, lse_d = _mla_fwd(qd, kvd, krp, rope_t, scale=scale_d, name="mla_fwd")
    y_o = bf(jnp.concatenate([oc, od], axis=1))
    mixed = _mm(y_o, w_odd_out, name="odd_out_fwd")
    tok = comm.sync("odd_out", [mixed])
    x1n, x1nb, xh1_1, r1_1 = _ln_fwd(x1, mixed, ln1_g[1], ln1_b[1], name="ln1_fwd_l1", deps=tok)
    comm.sync("ln1_l1", [x1nb])
    w_mlp1 = comm.weights("mlp1")
    w1_1, w2_1 = w_mlp1["mlp_w1_1"], _rows(w_mlp1["mlp_w2_1"])
    act1, hid1 = _mm(x1nb, w1_1, out_dtypes=(MXU_DTYPE, MXU_DTYPE), epilogue=_relu_sq, name="mlp1_fwd_l1", **blocks)
    mlp = _mm(hid1, w2_1, name="mlp2_fwd_l1")
    y, _, xh2_1, r2_1 = _ln_fwd(x1n, mlp, ln2_g[1], ln2_b[1], name="ln2_fwd_l1")
    dy, loss_vec = _loss_head(y, target, name="loss_head")

    def mlp_block_bwd(g_out, layer, w1, w2, xh2, r2, xh1, r1, act, hid, xnb):
        du2, du2b, dg2, db2 = _ln_bwd(g_out, xh2, r2, ln2_g[layer], name=f"ln2_bwd_l{layer}")
        dpre = _mm(du2b, w2, nt=True, out_dtypes=(MXU_DTYPE,), epilogue=_relu_sq_grad, extra=act, name=f"mlp2_dx_l{layer}")
        tok = comm.sync("mlp2_dx_l0", [dpre]) if layer == 0 else []
        dw2 = _mm(hid, du2b, ta=True, out_dtypes=(BF16,), name=f"mlp2_dw_l{layer}", deps=tok)
        dw1 = _mm(xnb, dpre, ta=True, out_blocks=True, out_dtypes=(BF16,), name=f"mlp1_dw_l{layer}")
        comm.grads(f"mlp{layer}", [(f"mlp_w1_{layer}", dw1), (f"mlp_w2_{layer}", dw2.reshape(N_DEV, -1, d))])
        tok = comm.sync(f"dw_l{layer}", [dw1])
        dxn = _mm(dpre, w1, nt=True, epilogue=_add_alpha, extra=du2, name=f"mlp1_dx_l{layer}", deps=tok, **blocks)
        du1, du1b, dg1, db1 = _ln_bwd(dxn, xh1, r1, ln1_g[layer], name=f"ln1_bwd_l{layer}")
        return du1, du1b, comm.sync(f"ln1_bwd_l{layer}", [du1b]), (dg1, db1, dg2, db2)

    du1, du1b, tok, ln_1 = mlp_block_bwd(dy, 1, w1_1, w2_1, xh2_1, r2_1, xh1_1, r1_1, act1, hid1, x1nb)
    d_odd_out = _mm(y_o, du1b, ta=True, out_dtypes=(BF16,), name="odd_out_dw").reshape(N_DEV, -1, d)
    dy_o = _mm(du1b, w_odd_out, nt=True, name="odd_out_dx", deps=tok)
    dqc, dkc, dvc = _sb_bwd(qkv_c, dy_o, sb_total, heads=C_HEADS, scale=scale_h, name="sb_bwd")
    dqd, dkvd, dkr_pairs = _mla_bwd(qd, kvd, krp, rope_t, dy_o, od, lse_d, do_block0=C_W // LANES, scale=scale_d,
                                    name="mla_bwd")
    _, dkr_sum = _rope(dkr_pairs, _rope_tables(s, inverse=True), out_dtype=F32, head_sum=True, name="rope_k_bwd")
    dqd, dkvd = bf(dqd), bf(dkvd)
    d_uq = _mm(ncq, dqd, ta=True, out_dtypes=(BF16,), name="uq_dw").reshape(D_Q_RANK, D_HEADS, LANES)[:, :, :D_NOPE + D_ROPE].reshape(
        D_Q_RANK, D_HEADS * (D_NOPE + D_ROPE))
    dncq = _mm(dqd, w_uq, nt=True, name="uq_dx")
    d_ukv = _mm(nckv, dkvd, ta=True, out_dtypes=(BF16,), name="ukv_dw")
    dnckv = _mm(dkvd, w_ukv, nt=True, name="ukv_dx")
    dcq, dgq = _rms_bwd(dncq, cq, rq, gq, name="rms_q_bwd")
    dckv, dgkv = _rms_bwd(dnckv, ckv, rkv, gkv, name="rms_kv_bwd")
    dh_o = bf(jnp.concatenate(
        [dqc, dkc, dvc, dcq, dckv, dkr_sum[:, D_NOPE:D_NOPE + D_ROPE], jnp.zeros((s, odd_n - ODD_IN), F32)], axis=1))
    d_odd_in = _column_blocks(_mm(x1b, dh_o, ta=True, out_dtypes=(BF16,), name="odd_in_dw"), odd_c, odd_cp)
    comm.grads("odd", [("odd_w_in", d_odd_in), ("odd_w_uq", _by_column_block(d_uq)),
                       ("odd_w_ukv", _by_column_block(d_ukv)), ("odd_w_out", d_odd_out)])
    tok = comm.sync("odd_in_dw", [d_odd_in])
    dx1 = _mm(dh_o, w_odd_in, nt=True, epilogue=_add_alpha, extra=du1, name="odd_in_dx", deps=tok)

    du1, du1b, tok, ln_0 = mlp_block_bwd(dx1, 0, w1_0, w2_0, xh2_0, r2_0, xh1_0, r1_0, act0, hid0, x0nb)
    d_even_out = _mm(y_e, du1b, ta=True, out_dtypes=(BF16,), name="even_out_dw")
    dy_e = _mm(du1b, w_even_out, nt=True, name="even_out_dx", deps=tok)
    doa = _stack_rows(dy_e[:, :A_Q_W], nq)
    dqa, dka, dva, dsink = _band_bwd(qa, ka, va, doa, oa, lse_a, slope_a, sink_a, name="swa_bwd", **a_cfg)
    pieces = [_unstack_rows(dqa, nq), _unheads(dka[:, 0, BLK:]), _unheads(dva[:, 0, BLK:])]
    pack = jnp.concatenate([dy_e[:, A_Q_W:], ob, lse_b], axis=1)
    for gi, (_, dil) in enumerate(B_PATTERNS):
        grads = _dil_bwd(b_slab[gi], _to_strided(pack, dil), name=f"dil{gi}_bwd", **b_cfg[gi])
        pieces.append(_from_strided(jnp.concatenate(grads, axis=1), dil))
    dh_e = bf(jnp.concatenate(pieces + [jnp.zeros((s, even_n - EVEN_IN), F32)], axis=1))
    d_even_in = _column_blocks(_mm(x0b, dh_e, ta=True, out_dtypes=(BF16,), name="even_in_dw"), even_c, even_cp)
    comm.grads("even", [("even_w_in", d_even_in), ("even_w_out", _by_column_block(d_even_out))])
    tok = comm.sync("even_in_dw", [d_even_in])
    grad_x = _mm(dh_e, w_even_in, nt=True, epilogue=_add_alpha, extra=du1, name="even_in_dx", deps=tok)
    tok = comm.sync("even_in_dx", [grad_x])

    ln = [jnp.concatenate([a, b], axis=0) for a, b in zip(ln_0, ln_1)]
    small = {"ln": ln, "sinks": dsink[:, :, 0].reshape(-1), "gq": dgq[0], "gkv": dgkv[0], "loss": loss_vec[0, :1]}
    return grad_x, small, tok


def kernel(x, even_w_in, even_sinks, even_w_out, odd_w_in, odd_q_norm_g, odd_kv_norm_g, odd_w_uq, odd_w_ukv, odd_w_out, ln1_g, ln1_b, mlp_w1, mlp_w2, ln2_g, ln2_b, loss_target, m_even_w_in, m_even_sinks, m_even_w_out, m_odd_w_in, m_odd_q_norm_g, m_odd_kv_norm_g, m_odd_w_uq, m_odd_w_ukv, m_odd_w_out, m_ln1_g, m_ln1_b, m_mlp_w1, m_mlp_w2, m_ln2_g, m_ln2_b, v_even_w_in, v_even_sinks, v_even_w_out, v_odd_w_in, v_odd_q_norm_g, v_odd_kv_norm_g, v_odd_w_uq, v_odd_w_ukv, v_odd_w_out, v_ln1_g, v_ln1_b, v_mlp_w1, v_mlp_w2, v_ln2_g, v_ln2_b):
    weights = dict(even_w_in=even_w_in, even_sinks=even_sinks, even_w_out=even_w_out, odd_w_in=odd_w_in,
                   odd_q_norm_g=odd_q_norm_g, odd_kv_norm_g=odd_kv_norm_g, odd_w_uq=odd_w_uq, odd_w_ukv=odd_w_ukv,
                   odd_w_out=odd_w_out, ln1_g=ln1_g, ln1_b=ln1_b, mlp_w1=mlp_w1, mlp_w2=mlp_w2, ln2_g=ln2_g, ln2_b=ln2_b)
    mom_m = dict(even_w_in=m_even_w_in, even_sinks=m_even_sinks, even_w_out=m_even_w_out, odd_w_in=m_odd_w_in,
                 odd_q_norm_g=m_odd_q_norm_g, odd_kv_norm_g=m_odd_kv_norm_g, odd_w_uq=m_odd_w_uq, odd_w_ukv=m_odd_w_ukv,
                 odd_w_out=m_odd_w_out, ln1_g=m_ln1_g, ln1_b=m_ln1_b, mlp_w1=m_mlp_w1, mlp_w2=m_mlp_w2, ln2_g=m_ln2_g, ln2_b=m_ln2_b)
    mom_v = dict(even_w_in=v_even_w_in, even_sinks=v_even_sinks, even_w_out=v_even_w_out, odd_w_in=v_odd_w_in,
                 odd_q_norm_g=v_odd_q_norm_g, odd_kv_norm_g=v_odd_kv_norm_g, odd_w_uq=v_odd_w_uq, odd_w_ukv=v_odd_w_ukv,
                 odd_w_out=v_odd_w_out, ln1_g=v_ln1_g, ln1_b=v_ln1_b, mlp_w1=v_mlp_w1, mlp_w2=v_mlp_w2, ln2_g=v_ln2_g, ln2_b=v_ln2_b)
    order = list(weights)
    n_q, n_kv = odd_q_norm_g.shape[1], odd_kv_norm_g.shape[1]

    gains = jnp.concatenate([odd_q_norm_g, odd_kv_norm_g, jnp.zeros((1, LANES - n_q - n_kv), F32)], axis=1)
    raw = {"even_w_in": even_w_in[0], "even_w_out": even_w_out[0], "mlp_w1_0": mlp_w1[0], "mlp_w2_0": mlp_w2[0],
           "odd_w_in": odd_w_in[0], "odd_w_uq": odd_w_uq[0], "odd_w_ukv": odd_w_ukv[0], "odd_w_out": odd_w_out[0],
           "mlp_w1_1": mlp_w1[1], "mlp_w2_1": mlp_w2[1], "norm_gains": gains}
    comm = _Exchanges(raw, n_q, n_kv)
    dev = comm.dev

    grad_x, small, last_started = _local_step(x[0], loss_target[0], comm, even_sinks[0], ln1_g, ln1_b, ln2_g, ln2_b)

    grads, delta, new_m, new_v = {}, {}, {}, {}

    def update(n):
        g_list = [comm.reduced[f"{n}_0"], comm.reduced[f"{n}_1"]] if n.startswith("mlp") else [comm.reduced[n]]
        grads[n], delta[n], new_m[n], new_v[n] = _adamw(weights[n], g_list, mom_m[n], mom_v[n], name=f"adamw_{n}",
                                                        my_chip=2 * comm.mx + comm.my, deps=last_started)

    early = ("mlp_w1", "mlp_w2", "odd_w_in", "odd_w_uq", "odd_w_ukv", "odd_w_out")
    for n in early:
        update(n)
    comm.sync("finish", [new_v[n] for n in early])
    update("even_w_in")
    update("even_w_out")

    small_parts = [t.reshape(-1) for t in small["ln"]] + [small["sinks"], small["gq"], small["gkv"], small["loss"]]
    small_sizes = [p.shape[0] for p in small_parts]
    n_small = sum(small_sizes)
    small_rows = -(-n_small // (8 * LANES)) * 8
    small_flat = jnp.concatenate(small_parts + [jnp.zeros((small_rows * LANES - n_small,), F32)]).reshape(small_rows, LANES)
    (small_all,) = _all_gather([small_flat], name="comm_small_gather", deps=[grads["even_w_in"]])
    totals = _sum_devices(small_all, name="small_sum").reshape(-1)
    tot, off = [], 0
    for size in small_sizes:
        tot.append(totals[off:off + size])
        off += size
    for i, n in enumerate(("ln1_g", "ln1_b", "ln2_g", "ln2_b")):
        grads[n] = tot[i].reshape(weights[n].shape)
    grads["even_sinks"] = tot[4].reshape(even_sinks.shape)
    grads["odd_q_norm_g"] = lax.dynamic_slice(tot[5], (dev * n_q,), (n_q,)).reshape(odd_q_norm_g.shape)
    grads["odd_kv_norm_g"] = lax.dynamic_slice(tot[6], (dev * n_kv,), (n_kv,)).reshape(odd_kv_norm_g.shape)
    loss = tot[7][0]

    small_names = [n for n in order if n not in early + ("even_w_in", "even_w_out")]
    n_sm = sum(weights[n].size for n in small_names)
    sm_rows = -(-n_sm // (8 * LANES)) * 8

    def pack_small(group):
        flat = [group[n].reshape(-1) for n in small_names]
        return jnp.concatenate(flat + [jnp.zeros((sm_rows * LANES - n_sm,), F32)]).reshape(1, sm_rows, LANES)

    res = _adamw(pack_small(weights), [pack_small(grads)[0]], pack_small(mom_m), pack_small(mom_v), name="adamw_small")
    off = 0
    for n in small_names:
        size = weights[n].size
        delta[n], new_m[n], new_v[n] = (t.reshape(-1)[off:off + size].reshape(weights[n].shape) for t in res[1:])
        off += size

    return (loss, grad_x[None], *[grads[n] for n in order], *[delta[n] for n in order],
            *[new_m[n] for n in order], *[new_v[n] for n in order])
```
